```python
import math
import jax
import jax.numpy as jnp
from jax import lax
import numpy as np

D_MODEL = 2048
BATCH = 8
SEQ = 4096
DEPTH = 1

N_META = 16
MIX_WIDTH = D_MODEL
RW_WIDTH = MIX_WIDTH // 2
RW_HEAD = 64
RW_HEADS = RW_WIDTH // RW_HEAD
RW_LORA_W = 64
RW_LORA_A = 64
RW_GN_EPS = 64e-5
RW_SHIFT_COLS = 3 * RW_WIDTH + RW_LORA_W + RW_LORA_A
DN_WIDTH = MIX_WIDTH - RW_WIDTH
DN_HEAD = 128
DN_HEADS = DN_WIDTH // DN_HEAD
CONV_W = 4
CHUNK = 64
NORM_EPS = 1e-6
IN_COLS = RW_SHIFT_COLS + RW_WIDTH + 3 * DN_WIDTH + 2 * DN_HEADS + DN_WIDTH

kernel_name = "hymba_rwkv7_gated_deltanet_layer"


def rms_norm(x, w, eps=NORM_EPS):
    xf = x.astype(jnp.float32)
    y = xf * lax.rsqrt(jnp.mean(xf * xf, axis=-1, keepdims=True) + eps)
    return (y * w.astype(jnp.float32)).astype(x.dtype)


def l2_normalize(x, eps=1e-6):
    return x * lax.rsqrt(jnp.sum(x * x, axis=-1, keepdims=True) + eps)


def token_shift(p, mu):
    prev = jnp.pad(p, ((0, 0), (1, 0), (0, 0)))[:, :-1]
    return p + (prev - p) * mu


def rwkv7_mix(p_shift, gate, w0, w2, a0, a2, k_k, k_a, r_k, gn_w, gn_b):
    B, L, _ = p_shift.shape
    p_shift = p_shift.astype(jnp.float32)
    r, k, v, lw, la = jnp.split(
        p_shift, [RW_WIDTH, 2 * RW_WIDTH, 3 * RW_WIDTH, 3 * RW_WIDTH + RW_LORA_W], axis=-1)
    w_log = -jax.nn.softplus(-(w0 + jnp.tanh(lw) @ w2)) - 0.5
    decay = jnp.exp(-jnp.exp(w_log))
    a = jax.nn.sigmoid(a0 + la @ a2)
    hd = lambda t: t.reshape(B, L, RW_HEADS, RW_HEAD)
    kk = l2_normalize(hd(k * k_k))
    k = k * (1.0 + (a - 1.0) * k_a)
    rh, kh, vh, dh, ah = hd(r), hd(k), hd(v), hd(decay), hd(a)
    tm = lambda t: jnp.moveaxis(t, 1, 0)
    xs = (tm(rh), tm(dh), tm(kh), tm(vh), tm(kk), tm(kk * ah))

    def step(S, inp):
        r_t, w_t, k_t, v_t, kk_t, b_t = inp
        sa = jnp.einsum('bhvk,bhk->bhv', S, kk_t)
        S = (S * w_t[:, :, None, :] - sa[..., None] * b_t[:, :, None, :]
             + v_t[..., None] * k_t[:, :, None, :])
        y = jnp.einsum('bhvk,bhk->bhv', S, r_t)
        return S, y

    S0 = jnp.zeros((B, RW_HEADS, RW_HEAD, RW_HEAD), jnp.float32)
    _, y = lax.scan(step, S0, xs)
    y = jnp.moveaxis(y, 0, 1)
    mean = jnp.mean(y, axis=-1, keepdims=True)
    var = jnp.mean(jnp.square(y - mean), axis=-1, keepdims=True)
    y = ((y - mean) * lax.rsqrt(var + RW_GN_EPS)).reshape(B, L, RW_WIDTH) * gn_w + gn_b
    bonus = jnp.sum(rh * kh * hd(jnp.broadcast_to(r_k, r.shape)), axis=-1, keepdims=True) * vh
    y = y + bonus.reshape(B, L, RW_WIDTH)
    return y * jax.nn.silu(gate.astype(jnp.float32))


def causal_dwconv(u, w):
    C = u.shape[-1]
    return lax.conv_general_dilated(
        u, w[:, None, :], window_strides=(1,), padding=[(CONV_W - 1, 0)],
        dimension_numbers=('NWC', 'WIO', 'NWC'), feature_group_count=C)


def gated_delta_mix(qkv, b, alpha, z, conv_w, A_log, dt_bias, norm_w):
    B, L, _ = qkv.shape
    f32 = jnp.float32
    qkv = jax.nn.silu(causal_dwconv(qkv.astype(f32), conv_w.astype(f32)))
    q, k, v = jnp.split(qkv, [DN_WIDTH, 2 * DN_WIDTH], axis=-1)
    hd = lambda t: t.reshape(B, L, DN_HEADS, DN_HEAD)
    q = l2_normalize(hd(q)) * (DN_HEAD ** -0.5)
    k = l2_normalize(hd(k))
    v = hd(v)
    beta = jax.nn.sigmoid(b.astype(f32))
    g = -jnp.exp(A_log) * jax.nn.softplus(alpha.astype(f32) + dt_bias)

    pad_f = (-N_META) % CHUNK
    pad_b = (-(L + pad_f)) % CHUNK
    Lp = L + pad_f + pad_b
    Nc = Lp // CHUNK
    pad4 = ((0, 0), (pad_f, pad_b), (0, 0), (0, 0))
    pad3 = ((0, 0), (pad_f, pad_b), (0, 0))
    ch4 = lambda t: jnp.transpose(jnp.pad(t, pad4), (0, 2, 1, 3)).reshape(B, DN_HEADS, Nc, CHUNK, DN_HEAD)
    ch3 = lambda t: jnp.transpose(jnp.pad(t, pad3), (0, 2, 1)).reshape(B, DN_HEADS, Nc, CHUNK)
    q, k, v = ch4(q), ch4(k), ch4(v)
    beta, g = ch3(beta), ch3(g)

    g = jnp.cumsum(g, axis=-1)
    k_beta = k * beta[..., None]
    v_beta = v * beta[..., None]
    tri_incl = jnp.tril(jnp.ones((CHUNK, CHUNK), bool))
    tri_strict = jnp.tril(jnp.ones((CHUNK, CHUNK), bool), -1)
    decay_mask = jnp.exp(jnp.where(tri_incl, g[..., :, None] - g[..., None, :], -jnp.inf))
    M = jnp.where(tri_strict, jnp.einsum('bhncd,bhnsd->bhncs', k_beta, k) * decay_mask, 0.0)
    eye = jnp.eye(CHUNK, dtype=f32)
    T = lax.linalg.triangular_solve(M + eye, jnp.broadcast_to(eye, M.shape),
                                    left_side=True, lower=True, unit_diagonal=True)
    u = jnp.einsum('bhncs,bhnsd->bhncd', T, v_beta)
    w = jnp.einsum('bhncs,bhnsd->bhncd', T, k_beta * jnp.exp(g)[..., None])
    attn = jnp.where(tri_incl, jnp.einsum('bhncd,bhnsd->bhncs', q, k) * decay_mask, 0.0)

    def chunk_step(S, inp):
        q_c, k_c, u_c, w_c, g_c, a_c = inp
        v_new = u_c - jnp.einsum('bhck,bhkv->bhcv', w_c, S)
        o = (jnp.einsum('bhck,bhkv->bhcv', q_c * jnp.exp(g_c)[..., None], S)
             + jnp.einsum('bhcs,bhsv->bhcv', a_c, v_new))
        g_last = g_c[..., -1]
        S = (S * jnp.exp(g_last)[..., None, None]
             + jnp.einsum('bhck,bhcv->bhkv', k_c * jnp.exp(g_last[..., None] - g_c)[..., None], v_new))
        return S, o

    cm = lambda t: jnp.moveaxis(t, 2, 0)
    S0 = jnp.zeros((B, DN_HEADS, DN_HEAD, DN_HEAD), f32)
    _, o = lax.scan(chunk_step, S0, (cm(q), cm(k), cm(u), cm(w), cm(g), cm(attn)))
    o = jnp.moveaxis(o, 0, 2).reshape(B, DN_HEADS, Lp, DN_HEAD)
    o = jnp.transpose(o, (0, 2, 1, 3))[:, pad_f:pad_f + L]
    o = o * lax.rsqrt(jnp.mean(o * o, axis=-1, keepdims=True) + NORM_EPS) * norm_w
    o = o * jax.nn.silu(hd(z.astype(f32)))
    return o.reshape(B, L, DN_WIDTH)


def hybrid_layer(h, norm_w, w_in, shift_mu, rw_w0, rw_w2, rw_a0, rw_a2, rw_k_k, rw_k_a,
                 rw_r_k, rw_gn_w, rw_gn_b, dn_conv_w, dn_A_log, dn_dt_bias, dn_norm_w, w_out):
    u = rms_norm(h, norm_w)
    p = u @ w_in
    s1 = RW_SHIFT_COLS
    s2 = s1 + RW_WIDTH
    s3 = s2 + 3 * DN_WIDTH
    s4 = s3 + DN_HEADS
    s5 = s4 + DN_HEADS
    rw_p, rw_gate, dn_qkv, dn_b, dn_a, dn_z = jnp.split(p, [s1, s2, s3, s4, s5], axis=-1)
    y_a = rwkv7_mix(token_shift(rw_p.astype(jnp.float32), shift_mu), rw_gate, rw_w0, rw_w2,
                    rw_a0, rw_a2, rw_k_k, rw_k_a, rw_r_k, rw_gn_w, rw_gn_b)
    y_b = gated_delta_mix(dn_qkv, dn_b, dn_a, dn_z, dn_conv_w, dn_A_log, dn_dt_bias, dn_norm_w)
    y = jnp.concatenate([y_a, y_b], axis=-1).astype(h.dtype)
    return h + y @ w_out


def _fwd_setup_inputs(seed: int = 0) -> dict:
    key = jax.random.key(seed)
    ks = jax.random.split(key, 20)
    f32 = jnp.float32
    nrm = lambda k, shape, s: s * jax.random.normal(k, shape, f32)
    x = nrm(ks[0], (BATCH, SEQ, D_MODEL), 1.0)
    meta_tokens = nrm(ks[1], (N_META, D_MODEL), 1.0)
    norm_w = 1.0 + nrm(ks[2], (DEPTH, D_MODEL), 0.02)
    w_in = nrm(ks[3], (DEPTH, D_MODEL, IN_COLS), D_MODEL ** -0.5)
    rw_shift_mu = jax.random.uniform(ks[4], (DEPTH, RW_SHIFT_COLS), f32)
    rw_w0 = jax.random.uniform(ks[5], (DEPTH, RW_WIDTH), f32, -6.0, 1.0)
    rw_w2 = nrm(ks[6], (DEPTH, RW_LORA_W, RW_WIDTH), 0.3 * RW_LORA_W ** -0.5)
    rw_a0 = nrm(ks[7], (DEPTH, RW_WIDTH), 0.1)
    rw_a2 = nrm(ks[8], (DEPTH, RW_LORA_A, RW_WIDTH), 0.3 * RW_LORA_A ** -0.5)
    rw_k_k = 0.85 + nrm(ks[9], (DEPTH, RW_WIDTH), 0.05)
    rw_k_a = 1.0 + nrm(ks[10], (DEPTH, RW_WIDTH), 0.05)
    rw_r_k = nrm(ks[11], (DEPTH, RW_WIDTH), 0.1)
    rw_gn_w = 1.0 + nrm(ks[12], (DEPTH, RW_WIDTH), 0.02)
    rw_gn_b = nrm(ks[13], (DEPTH, RW_WIDTH), 0.02)
    dn_conv_w = nrm(ks[14], (DEPTH, CONV_W, 3 * DN_WIDTH), CONV_W ** -0.5)
    dn_A_log = jnp.log(jax.random.uniform(ks[15], (DEPTH, DN_HEADS), f32, 1.0, 16.0))
    dt = jnp.exp(jax.random.uniform(ks[16], (DEPTH, DN_HEADS), f32, math.log(1e-3), math.log(1e-1)))
    dn_dt_bias = dt + jnp.log(-jnp.expm1(-dt))
    dn_norm_w = 1.0 + nrm(ks[17], (DEPTH, DN_HEAD), 0.02)
    w_out = nrm(ks[18], (DEPTH, MIX_WIDTH, D_MODEL), MIX_WIDTH ** -0.5)
    final_norm_w = 1.0 + nrm(ks[19], (D_MODEL,), 0.02)
    return {"x": x, "meta_tokens": meta_tokens, "norm_w": norm_w, "w_in": w_in,
            "rw_shift_mu": rw_shift_mu, "rw_w0": rw_w0, "rw_w2": rw_w2, "rw_a0": rw_a0,
            "rw_a2": rw_a2, "rw_k_k": rw_k_k, "rw_k_a": rw_k_a, "rw_r_k": rw_r_k,
            "rw_gn_w": rw_gn_w, "rw_gn_b": rw_gn_b, "dn_conv_w": dn_conv_w,
            "dn_A_log": dn_A_log, "dn_dt_bias": dn_dt_bias, "dn_norm_w": dn_norm_w,
            "w_out": w_out, "final_norm_w": final_norm_w}


def _fwd_reference(x, meta_tokens, norm_w, w_in, rw_shift_mu, rw_w0, rw_w2, rw_a0, rw_a2,
              rw_k_k, rw_k_a, rw_r_k, rw_gn_w, rw_gn_b, dn_conv_w, dn_A_log, dn_dt_bias,
              dn_norm_w, w_out, final_norm_w):
    B = x.shape[0]
    meta = jnp.broadcast_to(meta_tokens.astype(x.dtype)[None], (B, N_META, x.shape[-1]))
    h = jnp.concatenate([meta, x], axis=1)
    for l in range(DEPTH):
        h = hybrid_layer(h, norm_w[l], w_in[l], rw_shift_mu[l], rw_w0[l], rw_w2[l], rw_a0[l],
                         rw_a2[l], rw_k_k[l], rw_k_a[l], rw_r_k[l], rw_gn_w[l], rw_gn_b[l],
                         dn_conv_w[l], dn_A_log[l], dn_dt_bias[l], dn_norm_w[l], w_out[l])
    h = rms_norm(h, final_norm_w)
    return h[:, N_META:]


import jax as _jax
import jax.numpy as _jnp

TWIN_FORMAT = 'train_step'
FWD_PARAMS = ['x', 'meta_tokens', 'norm_w', 'w_in', 'rw_shift_mu', 'rw_w0', 'rw_w2', 'rw_a0', 'rw_a2', 'rw_k_k', 'rw_k_a', 'rw_r_k', 'rw_gn_w', 'rw_gn_b', 'dn_conv_w', 'dn_A_log', 'dn_dt_bias', 'dn_norm_w', 'w_out', 'final_norm_w']
TWIN_WEIGHTS = ['meta_tokens', 'norm_w', 'w_in', 'rw_shift_mu', 'rw_w0', 'rw_w2', 'rw_a0', 'rw_a2', 'rw_k_k', 'rw_k_a', 'rw_r_k', 'rw_gn_w', 'rw_gn_b', 'dn_conv_w', 'dn_A_log', 'dn_dt_bias', 'dn_norm_w', 'w_out', 'final_norm_w']
TWIN_DIFF_INPUT = 'x'
TWIN_INPUTS = ['x', 'meta_tokens', 'norm_w', 'w_in', 'rw_shift_mu', 'rw_w0', 'rw_w2', 'rw_a0', 'rw_a2', 'rw_k_k', 'rw_k_a', 'rw_r_k', 'rw_gn_w', 'rw_gn_b', 'dn_conv_w', 'dn_A_log', 'dn_dt_bias', 'dn_norm_w', 'w_out', 'final_norm_w', 'loss_target', 'm_meta_tokens', 'm_norm_w', 'm_w_in', 'm_rw_shift_mu', 'm_rw_w0', 'm_rw_w2', 'm_rw_a0', 'm_rw_a2', 'm_rw_k_k', 'm_rw_k_a', 'm_rw_r_k', 'm_rw_gn_w', 'm_rw_gn_b', 'm_dn_conv_w', 'm_dn_A_log', 'm_dn_dt_bias', 'm_dn_norm_w', 'm_w_out', 'm_final_norm_w', 'v_meta_tokens', 'v_norm_w', 'v_w_in', 'v_rw_shift_mu', 'v_rw_w0', 'v_rw_w2', 'v_rw_a0', 'v_rw_a2', 'v_rw_k_k', 'v_rw_k_a', 'v_rw_r_k', 'v_rw_gn_w', 'v_rw_gn_b', 'v_dn_conv_w', 'v_dn_A_log', 'v_dn_dt_bias', 'v_dn_norm_w', 'v_w_out', 'v_final_norm_w']
TWIN_OUTPUTS = ['loss', 'grad_x', 'grad_meta_tokens', 'grad_norm_w', 'grad_w_in', 'grad_rw_shift_mu', 'grad_rw_w0', 'grad_rw_w2', 'grad_rw_a0', 'grad_rw_a2', 'grad_rw_k_k', 'grad_rw_k_a', 'grad_rw_r_k', 'grad_rw_gn_w', 'grad_rw_gn_b', 'grad_dn_conv_w', 'grad_dn_A_log', 'grad_dn_dt_bias', 'grad_dn_norm_w', 'grad_w_out', 'grad_final_norm_w', 'delta_meta_tokens', 'delta_norm_w', 'delta_w_in', 'delta_rw_shift_mu', 'delta_rw_w0', 'delta_rw_w2', 'delta_rw_a0', 'delta_rw_a2', 'delta_rw_k_k', 'delta_rw_k_a', 'delta_rw_r_k', 'delta_rw_gn_w', 'delta_rw_gn_b', 'delta_dn_conv_w', 'delta_dn_A_log', 'delta_dn_dt_bias', 'delta_dn_norm_w', 'delta_w_out', 'delta_final_norm_w', 'new_m_meta_tokens', 'new_m_norm_w', 'new_m_w_in', 'new_m_rw_shift_mu', 'new_m_rw_w0', 'new_m_rw_w2', 'new_m_rw_a0', 'new_m_rw_a2', 'new_m_rw_k_k', 'new_m_rw_k_a', 'new_m_rw_r_k', 'new_m_rw_gn_w', 'new_m_rw_gn_b', 'new_m_dn_conv_w', 'new_m_dn_A_log', 'new_m_dn_dt_bias', 'new_m_dn_norm_w', 'new_m_w_out', 'new_m_final_norm_w', 'new_v_meta_tokens', 'new_v_norm_w', 'new_v_w_in', 'new_v_rw_shift_mu', 'new_v_rw_w0', 'new_v_rw_w2', 'new_v_rw_a0', 'new_v_rw_a2', 'new_v_rw_k_k', 'new_v_rw_k_a', 'new_v_rw_r_k', 'new_v_rw_gn_w', 'new_v_rw_gn_b', 'new_v_dn_conv_w', 'new_v_dn_A_log', 'new_v_dn_dt_bias', 'new_v_dn_norm_w', 'new_v_w_out', 'new_v_final_norm_w']
TWIN_LEAF_KINDS = {'loss': 'loss', 'grad_x': 'grad_x', 'grad_meta_tokens': 'grad_w', 'grad_norm_w': 'grad_w', 'grad_w_in': 'grad_w', 'grad_rw_shift_mu': 'grad_w', 'grad_rw_w0': 'grad_w', 'grad_rw_w2': 'grad_w', 'grad_rw_a0': 'grad_w', 'grad_rw_a2': 'grad_w', 'grad_rw_k_k': 'grad_w', 'grad_rw_k_a': 'grad_w', 'grad_rw_r_k': 'grad_w', 'grad_rw_gn_w': 'grad_w', 'grad_rw_gn_b': 'grad_w', 'grad_dn_conv_w': 'grad_w', 'grad_dn_A_log': 'grad_w', 'grad_dn_dt_bias': 'grad_w', 'grad_dn_norm_w': 'grad_w', 'grad_w_out': 'grad_w', 'grad_final_norm_w': 'grad_w', 'delta_meta_tokens': 'delta_w', 'delta_norm_w': 'delta_w', 'delta_w_in': 'delta_w', 'delta_rw_shift_mu': 'delta_w', 'delta_rw_w0': 'delta_w', 'delta_rw_w2': 'delta_w', 'delta_rw_a0': 'delta_w', 'delta_rw_a2': 'delta_w', 'delta_rw_k_k': 'delta_w', 'delta_rw_k_a': 'delta_w', 'delta_rw_r_k': 'delta_w', 'delta_rw_gn_w': 'delta_w', 'delta_rw_gn_b': 'delta_w', 'delta_dn_conv_w': 'delta_w', 'delta_dn_A_log': 'delta_w', 'delta_dn_dt_bias': 'delta_w', 'delta_dn_norm_w': 'delta_w', 'delta_w_out': 'delta_w', 'delta_final_norm_w': 'delta_w', 'new_m_meta_tokens': 'new_m', 'new_m_norm_w': 'new_m', 'new_m_w_in': 'new_m', 'new_m_rw_shift_mu': 'new_m', 'new_m_rw_w0': 'new_m', 'new_m_rw_w2': 'new_m', 'new_m_rw_a0': 'new_m', 'new_m_rw_a2': 'new_m', 'new_m_rw_k_k': 'new_m', 'new_m_rw_k_a': 'new_m', 'new_m_rw_r_k': 'new_m', 'new_m_rw_gn_w': 'new_m', 'new_m_rw_gn_b': 'new_m', 'new_m_dn_conv_w': 'new_m', 'new_m_dn_A_log': 'new_m', 'new_m_dn_dt_bias': 'new_m', 'new_m_dn_norm_w': 'new_m', 'new_m_w_out': 'new_m', 'new_m_final_norm_w': 'new_m', 'new_v_meta_tokens': 'new_v', 'new_v_norm_w': 'new_v', 'new_v_w_in': 'new_v', 'new_v_rw_shift_mu': 'new_v', 'new_v_rw_w0': 'new_v', 'new_v_rw_w2': 'new_v', 'new_v_rw_a0': 'new_v', 'new_v_rw_a2': 'new_v', 'new_v_rw_k_k': 'new_v', 'new_v_rw_k_a': 'new_v', 'new_v_rw_r_k': 'new_v', 'new_v_rw_gn_w': 'new_v', 'new_v_rw_gn_b': 'new_v', 'new_v_dn_conv_w': 'new_v', 'new_v_dn_A_log': 'new_v', 'new_v_dn_dt_bias': 'new_v', 'new_v_dn_norm_w': 'new_v', 'new_v_w_out': 'new_v', 'new_v_final_norm_w': 'new_v'}


def _forward(args):
    return _fwd_reference(*[args[k] for k in FWD_PARAMS])


def _output_shape():
    def fwd():
        inp = _fwd_setup_inputs(0)
        return _fwd_reference(*[inp[k] for k in FWD_PARAMS])
    out = _jax.eval_shape(fwd)
    return out.shape, out.dtype

N_MICROBATCH = 1
ADAM_LR = 0.001
ADAM_B1 = 0.9
ADAM_B2 = 0.999
ADAM_EPS = 1e-08
ADAM_WD = 0.01
ADAM_STEP = 10
PER_EXAMPLE_BATCH_AXIS = {'x': 0, 'loss_target': 0}
SHARED_INPUTS = []
_WEIGHT_DTYPES = {'meta_tokens': _jnp.float32, 'norm_w': _jnp.float32, 'w_in': _jnp.float32, 'rw_shift_mu': _jnp.float32, 'rw_w0': _jnp.float32, 'rw_w2': _jnp.float32, 'rw_a0': _jnp.float32, 'rw_a2': _jnp.float32, 'rw_k_k': _jnp.float32, 'rw_k_a': _jnp.float32, 'rw_r_k': _jnp.float32, 'rw_gn_w': _jnp.float32, 'rw_gn_b': _jnp.float32, 'dn_conv_w': _jnp.float32, 'dn_A_log': _jnp.float32, 'dn_dt_bias': _jnp.float32, 'dn_norm_w': _jnp.float32, 'w_out': _jnp.float32, 'final_norm_w': _jnp.float32}
MOMENT_SCALE = {'meta_tokens': 3.554278e-03, 'norm_w': 8.554449e-02, 'w_in': 4.286652e-02, 'rw_shift_mu': 7.816759e-02, 'rw_w0': 1.557673e-02, 'rw_w2': 2.252155e-03, 'rw_a0': 1.948969e-02, 'rw_a2': 1.873527e-02, 'rw_k_k': 5.304366e-02, 'rw_k_a': 4.906966e-02, 'rw_r_k': 1.049381e-01, 'rw_gn_w': 4.610449e-02, 'rw_gn_b': 4.647202e-02, 'dn_conv_w': 3.544400e-02, 'dn_A_log': 2.196137e-01, 'dn_dt_bias': 2.190976e-01, 'dn_norm_w': 1.427213e-01, 'w_out': 4.566457e-02, 'final_norm_w': 1.599519e+01}


def _to_microbatches(a, axis):
    t = _jnp.moveaxis(a, axis, 0)
    t = t.reshape((N_MICROBATCH, t.shape[0] // N_MICROBATCH) + t.shape[1:])
    return _jnp.moveaxis(t, 1, axis + 1)


def setup_inputs(seed: int = 0) -> dict:
    inp = _fwd_setup_inputs(seed)
    key = _jax.random.fold_in(_jax.random.key(seed), 7919)
    shape, _ = _output_shape()
    out = dict(inp)
    out["loss_target"] = _jax.random.normal(_jax.random.fold_in(key, 0), shape, _jnp.float32)
    for i, name in enumerate(TWIN_WEIGHTS):
        w = inp[name].astype(_jnp.float32)
        if MOMENT_SCALE is None:
            s = _jnp.sqrt(_jnp.mean(_jnp.square(w)) + 1e-30)
        else:
            s = MOMENT_SCALE[name]
        km, kv = _jax.random.split(_jax.random.fold_in(key, i + 1))
        out[name] = w
        out["m_" + name] = s * _jax.random.normal(km, w.shape, _jnp.float32)
        out["v_" + name] = (s * s) * _jax.random.uniform(kv, w.shape, _jnp.float32, 0.5, 1.5)
    if N_MICROBATCH > 1:
        for name, axis in PER_EXAMPLE_BATCH_AXIS.items():
            out[name] = _to_microbatches(out[name], axis)
    return {'x': out['x'], 'meta_tokens': out['meta_tokens'], 'norm_w': out['norm_w'], 'w_in': out['w_in'], 'rw_shift_mu': out['rw_shift_mu'], 'rw_w0': out['rw_w0'], 'rw_w2': out['rw_w2'], 'rw_a0': out['rw_a0'], 'rw_a2': out['rw_a2'], 'rw_k_k': out['rw_k_k'], 'rw_k_a': out['rw_k_a'], 'rw_r_k': out['rw_r_k'], 'rw_gn_w': out['rw_gn_w'], 'rw_gn_b': out['rw_gn_b'], 'dn_conv_w': out['dn_conv_w'], 'dn_A_log': out['dn_A_log'], 'dn_dt_bias': out['dn_dt_bias'], 'dn_norm_w': out['dn_norm_w'], 'w_out': out['w_out'], 'final_norm_w': out['final_norm_w'], 'loss_target': out['loss_target'], 'm_meta_tokens': out['m_meta_tokens'], 'm_norm_w': out['m_norm_w'], 'm_w_in': out['m_w_in'], 'm_rw_shift_mu': out['m_rw_shift_mu'], 'm_rw_w0': out['m_rw_w0'], 'm_rw_w2': out['m_rw_w2'], 'm_rw_a0': out['m_rw_a0'], 'm_rw_a2': out['m_rw_a2'], 'm_rw_k_k': out['m_rw_k_k'], 'm_rw_k_a': out['m_rw_k_a'], 'm_rw_r_k': out['m_rw_r_k'], 'm_rw_gn_w': out['m_rw_gn_w'], 'm_rw_gn_b': out['m_rw_gn_b'], 'm_dn_conv_w': out['m_dn_conv_w'], 'm_dn_A_log': out['m_dn_A_log'], 'm_dn_dt_bias': out['m_dn_dt_bias'], 'm_dn_norm_w': out['m_dn_norm_w'], 'm_w_out': out['m_w_out'], 'm_final_norm_w': out['m_final_norm_w'], 'v_meta_tokens': out['v_meta_tokens'], 'v_norm_w': out['v_norm_w'], 'v_w_in': out['v_w_in'], 'v_rw_shift_mu': out['v_rw_shift_mu'], 'v_rw_w0': out['v_rw_w0'], 'v_rw_w2': out['v_rw_w2'], 'v_rw_a0': out['v_rw_a0'], 'v_rw_a2': out['v_rw_a2'], 'v_rw_k_k': out['v_rw_k_k'], 'v_rw_k_a': out['v_rw_k_a'], 'v_rw_r_k': out['v_rw_r_k'], 'v_rw_gn_w': out['v_rw_gn_w'], 'v_rw_gn_b': out['v_rw_gn_b'], 'v_dn_conv_w': out['v_dn_conv_w'], 'v_dn_A_log': out['v_dn_A_log'], 'v_dn_dt_bias': out['v_dn_dt_bias'], 'v_dn_norm_w': out['v_dn_norm_w'], 'v_w_out': out['v_w_out'], 'v_final_norm_w': out['v_final_norm_w']}


def _loss(weights, diff, rest, loss_target):
    with _jax.named_scope("forward"):
        args = {**rest, TWIN_DIFF_INPUT: diff, **{k: w.astype(_WEIGHT_DTYPES[k]) for k, w in weights.items()}}
        y = _forward(args)
    with _jax.named_scope("loss_head"):
        err = _jnp.square(y.astype(_jnp.float32) - loss_target)
        return 0.5 * _jnp.sum(_jnp.mean(err, axis=-1)) if err.ndim else 0.5 * err


def _adamw(w, g, m, v):
    m = ADAM_B1 * m + (1.0 - ADAM_B1) * g
    v = ADAM_B2 * v + (1.0 - ADAM_B2) * _jnp.square(g)
    m_hat = m / (1.0 - ADAM_B1 ** ADAM_STEP)
    v_hat = v / (1.0 - ADAM_B2 ** ADAM_STEP)
    delta = -ADAM_LR * (m_hat / (_jnp.sqrt(v_hat) + ADAM_EPS) + ADAM_WD * w)
    return delta, m, v


def reference(x, meta_tokens, norm_w, w_in, rw_shift_mu, rw_w0, rw_w2, rw_a0, rw_a2, rw_k_k, rw_k_a, rw_r_k, rw_gn_w, rw_gn_b, dn_conv_w, dn_A_log, dn_dt_bias, dn_norm_w, w_out, final_norm_w, loss_target, m_meta_tokens, m_norm_w, m_w_in, m_rw_shift_mu, m_rw_w0, m_rw_w2, m_rw_a0, m_rw_a2, m_rw_k_k, m_rw_k_a, m_rw_r_k, m_rw_gn_w, m_rw_gn_b, m_dn_conv_w, m_dn_A_log, m_dn_dt_bias, m_dn_norm_w, m_w_out, m_final_norm_w, v_meta_tokens, v_norm_w, v_w_in, v_rw_shift_mu, v_rw_w0, v_rw_w2, v_rw_a0, v_rw_a2, v_rw_k_k, v_rw_k_a, v_rw_r_k, v_rw_gn_w, v_rw_gn_b, v_dn_conv_w, v_dn_A_log, v_dn_dt_bias, v_dn_norm_w, v_w_out, v_final_norm_w):
    given = dict(x=x, meta_tokens=meta_tokens, norm_w=norm_w, w_in=w_in, rw_shift_mu=rw_shift_mu, rw_w0=rw_w0, rw_w2=rw_w2, rw_a0=rw_a0, rw_a2=rw_a2, rw_k_k=rw_k_k, rw_k_a=rw_k_a, rw_r_k=rw_r_k, rw_gn_w=rw_gn_w, rw_gn_b=rw_gn_b, dn_conv_w=dn_conv_w, dn_A_log=dn_A_log, dn_dt_bias=dn_dt_bias, dn_norm_w=dn_norm_w, w_out=w_out, final_norm_w=final_norm_w, loss_target=loss_target, m_meta_tokens=m_meta_tokens, m_norm_w=m_norm_w, m_w_in=m_w_in, m_rw_shift_mu=m_rw_shift_mu, m_rw_w0=m_rw_w0, m_rw_w2=m_rw_w2, m_rw_a0=m_rw_a0, m_rw_a2=m_rw_a2, m_rw_k_k=m_rw_k_k, m_rw_k_a=m_rw_k_a, m_rw_r_k=m_rw_r_k, m_rw_gn_w=m_rw_gn_w, m_rw_gn_b=m_rw_gn_b, m_dn_conv_w=m_dn_conv_w, m_dn_A_log=m_dn_A_log, m_dn_dt_bias=m_dn_dt_bias, m_dn_norm_w=m_dn_norm_w, m_w_out=m_w_out, m_final_norm_w=m_final_norm_w, v_meta_tokens=v_meta_tokens, v_norm_w=v_norm_w, v_w_in=v_w_in, v_rw_shift_mu=v_rw_shift_mu, v_rw_w0=v_rw_w0, v_rw_w2=v_rw_w2, v_rw_a0=v_rw_a0, v_rw_a2=v_rw_a2, v_rw_k_k=v_rw_k_k, v_rw_k_a=v_rw_k_a, v_rw_r_k=v_rw_r_k, v_rw_gn_w=v_rw_gn_w, v_rw_gn_b=v_rw_gn_b, v_dn_conv_w=v_dn_conv_w, v_dn_A_log=v_dn_A_log, v_dn_dt_bias=v_dn_dt_bias, v_dn_norm_w=v_dn_norm_w, v_w_out=v_w_out, v_final_norm_w=v_final_norm_w)
    weights = {n: given[n] for n in TWIN_WEIGHTS}
    shared = {n: given[n] for n in SHARED_INPUTS}
    per_example = {n: given[n] for n in ['x']}
    grad_fn = _jax.value_and_grad(_loss, argnums=(0, 1))

    def one_microbatch(ex, loss_target):
        ex = dict(ex)
        diff = ex.pop(TWIN_DIFF_INPUT)
        return grad_fn(weights, diff, {**shared, **ex}, loss_target)

    if N_MICROBATCH == 1:
        loss, (grad_w, grad_x) = one_microbatch(per_example, given["loss_target"])
    else:
        def body(carry, xs):
            loss_sum, grad_sum = carry
            l_k, (gw_k, gx_k) = one_microbatch(xs[0], xs[1])
            with _jax.named_scope("update"):
                return (loss_sum + l_k, _jax.tree.map(_jnp.add, grad_sum, gw_k)), gx_k

        init = (_jnp.zeros((), _jnp.float32), _jax.tree.map(_jnp.zeros_like, weights))
        (loss, grad_w), grad_x = _jax.lax.scan(body, init, (per_example, given["loss_target"]))
    with _jax.named_scope("update"):
        delta_w, new_m, new_v = {}, {}, {}
        for n in TWIN_WEIGHTS:
            delta_w[n], new_m[n], new_v[n] = _adamw(weights[n], grad_w[n], given["m_" + n], given["v_" + n])
    return (loss, grad_x, *[grad_w[n] for n in TWIN_WEIGHTS], *[delta_w[n] for n in TWIN_WEIGHTS],
            *[new_m[n] for n in TWIN_WEIGHTS], *[new_v[n] for n in TWIN_WEIGHTS])
```

```python
import functools

import jax
import jax.numpy as jnp
from jax import lax
from jax.experimental import pallas as pl
from jax.experimental.pallas import tpu as pltpu

F32 = jnp.float32
BF16 = jnp.bfloat16
HIGH = lax.Precision.HIGH
HIGHEST = lax.Precision.HIGHEST
MESH = pl.DeviceIdType.MESH

D_MODEL = 2048
N_META = 16
RW_WIDTH = 1024
RW_HEAD = 64
RW_HEADS = 16
RW_LORA = 64
RW_GN_EPS = 64e-5
DN_WIDTH = 1024
DN_HEAD = 128
DN_HEADS = 8
CONV_W = 4
CHUNK = 64
NORM_EPS = 1e-6
IN_COLS = 8336
N_CHIPS = 4
SHARD_COLS = IN_COLS // N_CHIPS

NP_COLS = 8 * 1024 + 256
CB_R, CB_K, CB_V, CB_GATE, CB_DQ, CB_DK, CB_DV, CB_Z = range(8)
CB_S1 = 8192 // 128
CB_S2 = CB_S1 + 1

ADAM_LR = 0.001
ADAM_B1 = 0.9
ADAM_B2 = 0.999
ADAM_EPS = 1e-08
ADAM_WD = 0.01
ADAM_STEP = 10

VMEM_LIMIT_BYTES = 56 * 1024 * 1024
RW_HEADS_PER_STEP = 4
ROW_TILE = 104
MM_ROW_TILE = 520


def _params(sem=None):
    return pltpu.CompilerParams(dimension_semantics=sem, vmem_limit_bytes=VMEM_LIMIT_BYTES)


def _pick(n, target, mult=8):
    best = None
    for d in range(mult, min(n, target) + 1, mult):
        if n % d == 0:
            best = d
    return n if best is None else best


def _mm(name, a, b, mode, tm=MM_ROW_TILE, tn=1408, tk=2048):
    if mode == "nn":
        (M, K), (_, N) = a.shape, b.shape
    elif mode == "nt":
        (M, K), (N, _) = a.shape, b.shape
    else:
        (K, M), (_, N) = a.shape, b.shape
    tm = _pick(M, tm, 128 if mode == "tn" else 8)
    tn = _pick(N, tn, 128)
    tk = _pick(K, tk, 8 if mode == "tn" else 128)
    if mode == "nn":
        a_spec = pl.BlockSpec((tm, tk), lambda i, j, k: (i, k))
        b_spec = pl.BlockSpec((tk, tn), lambda i, j, k: (k, j))
        dims = (((1,), (0,)), ((), ()))
    elif mode == "nt":
        a_spec = pl.BlockSpec((tm, tk), lambda i, j, k: (i, k))
        b_spec = pl.BlockSpec((tn, tk), lambda i, j, k: (j, k))
        dims = (((1,), (1,)), ((), ()))
    else:
        a_spec = pl.BlockSpec((tk, tm), lambda i, j, k: (k, i))
        b_spec = pl.BlockSpec((tk, tn), lambda i, j, k: (k, j))
        dims = (((0,), (0,)), ((), ()))

    def body(a_ref, b_ref, o_ref):
        @pl.when(pl.program_id(2) == 0)
        def _():
            o_ref[...] = jnp.zeros_like(o_ref)

        o_ref[...] += lax.dot_general(a_ref[...].astype(BF16), b_ref[...].astype(BF16), dims,
                                      preferred_element_type=F32)

    return pl.pallas_call(
        body, name=name, grid=(M // tm, N // tn, K // tk),
        in_specs=[a_spec, b_spec], out_specs=pl.BlockSpec((tm, tn), lambda i, j, k: (i, j)),
        out_shape=jax.ShapeDtypeStruct((M, N), F32),
        compiler_params=_params(("parallel", "parallel", "arbitrary")),
    )(a, b)


def _row(arr, width=None, cb=0):
    return (arr, arr.shape[1] if width is None else width, cb)


def _rowwise(name, fn, rows, params, out_widths, tm=ROW_TILE):
    R = rows[0][0].shape[0]
    tm = _pick(R, tm)
    n_r, n_p = len(rows), len(params)

    def body(*refs):
        vals = [r[...] for r in refs[:n_r + n_p]]
        for o_ref, val in zip(refs[n_r + n_p:], fn(*vals)):
            o_ref[...] = val

    in_specs = [pl.BlockSpec((tm, w), lambda i, cb=cb: (i, cb)) for (_, w, cb) in rows]
    in_specs += [pl.BlockSpec(p.shape, lambda i: (0, 0)) for p in params]
    return pl.pallas_call(
        body, name=name, grid=(R // tm,), in_specs=in_specs,
        out_specs=[pl.BlockSpec((tm, w), lambda i: (i, 0)) for w in out_widths],
        out_shape=[jax.ShapeDtypeStruct((R, w), F32) for w in out_widths],
        compiler_params=_params(("parallel",)),
    )(*[r[0] for r in rows], *params)


def _rowwise_bwd(name, fn, rows, params, douts, tm=ROW_TILE):
    R = rows[0][0].shape[0]
    tm = _pick(R, tm)
    n_r, n_p, n_d = len(rows), len(params), len(douts)

    def body(*refs):
        vals = [r[...] for r in refs[:n_r + n_p]]
        cts = tuple(r[...] for r in refs[n_r + n_p:n_r + n_p + n_d])
        grads = jax.vjp(fn, *vals)[1](cts)
        outs = refs[n_r + n_p + n_d:]
        for o_ref, g in zip(outs[:n_r], grads[:n_r]):
            o_ref[...] = g

        @pl.when(pl.program_id(0) == 0)
        def _():
            for o_ref in outs[n_r:]:
                o_ref[...] = jnp.zeros_like(o_ref)

        for o_ref, g in zip(outs[n_r:], grads[n_r:]):
            o_ref[...] += g

    in_specs = [pl.BlockSpec((tm, w), lambda i, cb=cb: (i, cb)) for (_, w, cb) in rows]
    in_specs += [pl.BlockSpec(p.shape, lambda i: (0, 0)) for p in params]
    in_specs += [pl.BlockSpec((tm, w), lambda i, cb=cb: (i, cb)) for (_, w, cb) in douts]
    out_specs = [pl.BlockSpec((tm, w), lambda i: (i, 0)) for (_, w, _) in rows]
    out_specs += [pl.BlockSpec(p.shape, lambda i: (0, 0)) for p in params]
    out_shape = [jax.ShapeDtypeStruct((R, w), F32) for (_, w, _) in rows]
    out_shape += [jax.ShapeDtypeStruct(p.shape, F32) for p in params]
    res = pl.pallas_call(
        body, name=name, grid=(R // tm,), in_specs=in_specs, out_specs=out_specs, out_shape=out_shape,
        compiler_params=_params(("arbitrary",)),
    )(*[r[0] for r in rows], *params, *[d[0] for d in douts])
    return res[:n_r], res[n_r:]


def _shift_sum(name, terms, tm=ROW_TILE):
    R = terms[0][0][0].shape[0]
    w = terms[0][0][1]
    tm = _pick(R, tm)
    nt, nb8 = R // tm, tm // 8
    shifts = [j for (_, j) in terms]

    def body(*refs):
        i = pl.program_id(0)
        acc = None
        for k, j in enumerate(shifts):
            x = refs[2 * k][...]
            if j > 0:
                halo = jnp.where(i == 0, 0.0, refs[2 * k + 1][...])
                x = pltpu.roll(jnp.concatenate([halo, x], axis=0), j, 0)[8:, :]
            elif j < 0:
                halo = jnp.where(i == nt - 1, 0.0, refs[2 * k + 1][...])
                x = pltpu.roll(jnp.concatenate([x, halo], axis=0), tm + 8 + j, 0)[:tm, :]
            acc = x if acc is None else acc + x
        refs[-1][...] = acc

    in_specs, args = [], []
    for (arr, _, cb), j in terms:
        in_specs.append(pl.BlockSpec((tm, w), lambda i, cb=cb: (i, cb)))
        if j > 0:
            in_specs.append(pl.BlockSpec((8, w), lambda i, cb=cb: (jnp.maximum(i * nb8 - 1, 0), cb)))
        else:
            in_specs.append(pl.BlockSpec((8, w), lambda i, cb=cb: (jnp.minimum((i + 1) * nb8, R // 8 - 1), cb)))
        args += [arr, arr]
    return pl.pallas_call(
        body, name=name, grid=(nt,), in_specs=in_specs, out_specs=pl.BlockSpec((tm, w), lambda i: (i, 0)),
        out_shape=jax.ShapeDtypeStruct((R, w), F32), compiler_params=_params(("parallel",)),
    )(*args)


def _softplus(x):
    return jnp.maximum(x, 0.0) + jnp.log(1.0 + jnp.exp(-jnp.abs(x)))


def _silu(x):
    return x * jax.nn.sigmoid(x)


def _rms_fn(h, w):
    return (h * lax.rsqrt(jnp.mean(h * h, axis=-1, keepdims=True) + NORM_EPS) * w,)


def _rw_prep_fn(pr, pr1, pk, pk1, pv, pv1, ps, ps1, mu_r, mu_k, mu_v, mu_s, w0, w2p, a0, a2p, k_k, k_a):
    r = pr + (pr1 - pr) * mu_r
    k = pk + (pk1 - pk) * mu_k
    v = pv + (pv1 - pv) * mu_v
    s = ps + (ps1 - ps) * mu_s
    w_log = -_softplus(-(w0 + jnp.dot(jnp.tanh(s), w2p, precision=HIGH, preferred_element_type=F32))) - 0.5
    log_decay = -jnp.exp(w_log)
    a = jax.nn.sigmoid(a0 + jnp.dot(s, a2p, precision=HIGH, preferred_element_type=F32))
    return r, log_decay, k * (1.0 + (a - 1.0) * k_a), v, k * k_k, a


def _conv_fn(u0, u1, u2, u3, w0, w1, w2, w3):
    return (_silu(u0 * w3 + u1 * w2 + u2 * w1 + u3 * w0),)


def _dn_gate_fn(ps2, a_log_b, dt_b):
    lane = lax.broadcasted_iota(jnp.int32, (DN_HEAD, DN_WIDTH), 0)
    head = jnp.right_shift(lax.broadcasted_iota(jnp.int32, (DN_HEAD, DN_WIDTH), 1), 7)
    sel_b = (lane == head).astype(F32)
    sel_a = (lane == head + DN_HEADS).astype(F32)
    b = jnp.dot(ps2, sel_b, precision=HIGHEST, preferred_element_type=F32)
    al = jnp.dot(ps2, sel_a, precision=HIGHEST, preferred_element_type=F32)
    return jax.nn.sigmoid(b), -jnp.exp(a_log_b) * _softplus(al + dt_b)


def _tril_masks(c):
    t = lax.broadcasted_iota(jnp.int32, (c, c), 0)
    s = lax.broadcasted_iota(jnp.int32, (c, c), 1)
    return s <= t, s < t


def _tri_inv(a, mm):
    c = a.shape[-1]
    eye = (lax.broadcasted_iota(jnp.int32, (c, c), 0) == lax.broadcasted_iota(jnp.int32, (c, c), 1)).astype(F32)
    x = eye - a
    p = a
    n = 2
    while n < c:
        p = mm(p, p)
        x = x + mm(x, p)
        n *= 2
    return x


def _rw_chunk_fn(H, r, lw, k2, v, kkp, a, gate, rk, gnw, gnb):
    B, C, _ = r.shape
    bmm = lambda x, y: jnp.einsum("bij,bjk->bik", x, y, precision=HIGH, preferred_element_type=F32)
    bmm_nt = lambda x, y: jnp.einsum("bik,bjk->bij", x, y, precision=HIGH, preferred_element_type=F32)
    bmm_tn = lambda x, y: jnp.einsum("bki,bkj->bij", x, y, precision=HIGH, preferred_element_type=F32)
    incl, strict = _tril_masks(C)
    tril = jnp.broadcast_to(incl.astype(F32), (B, C, C))
    kk = kkp * lax.rsqrt(jnp.sum(kkp * kkp, axis=-1, keepdims=True) + 1e-6)
    b = kk * a
    g_incl = jnp.einsum("bij,bjk->bik", tril, lw, precision=HIGHEST, preferred_element_type=F32)
    g_excl = g_incl - lw
    inv = jnp.exp(-g_incl)
    alpha, beta, kappa, rho = kk * jnp.exp(g_excl), b * inv, k2 * inv, r * jnp.exp(g_incl)
    a_ab = jnp.where(strict, bmm_nt(alpha, beta), 0.0)
    a_ak = jnp.where(strict, bmm_nt(alpha, kappa), 0.0)
    t_inv = _tri_inv(a_ab, bmm)
    u = bmm(t_inv, -bmm(alpha, H) - bmm(a_ak, v))
    y = (bmm(rho, H) + bmm(jnp.where(incl, bmm_nt(rho, beta), 0.0), u)
         + bmm(jnp.where(incl, bmm_nt(rho, kappa), 0.0), v))
    g_tot = jnp.einsum("bki,bkj->bij", lw, jnp.ones((B, C, H.shape[-1]), F32), precision=HIGHEST,
                       preferred_element_type=F32)
    H_new = jnp.exp(g_tot) * (H + bmm_tn(beta, u) + bmm_tn(kappa, v))
    mean = jnp.mean(y, axis=-1, keepdims=True)
    var = jnp.mean(jnp.square(y - mean), axis=-1, keepdims=True)
    yn = (y - mean) * lax.rsqrt(var + RW_GN_EPS) * gnw + gnb
    bonus = jnp.sum(r * k2 * rk, axis=-1, keepdims=True) * v
    return (yn + bonus) * _silu(gate), H_new


def _dn_chunk_fn(S, qc, kc, vc, bb, gb, z, nw):
    C, D = qc.shape
    mm = lambda x, y: jnp.dot(x, y, precision=HIGH, preferred_element_type=F32)
    mm_x = lambda x, y: jnp.dot(x, y, precision=HIGHEST, preferred_element_type=F32)
    nt = lambda x, y, p=HIGH: lax.dot_general(x, y, (((1,), (1,)), ((), ())), precision=p,
                                              preferred_element_type=F32)
    tn = lambda x, y: lax.dot_general(x, y, (((0,), (0,)), ((), ())), precision=HIGH, preferred_element_type=F32)
    incl, strict = _tril_masks(C)
    q = qc * lax.rsqrt(jnp.sum(qc * qc, axis=-1, keepdims=True) + 1e-6) * (D ** -0.5)
    k = kc * lax.rsqrt(jnp.sum(kc * kc, axis=-1, keepdims=True) + 1e-6)
    kb, vb = k * bb, vc * bb
    G = mm_x(incl.astype(F32), gb)
    first = (lax.broadcasted_iota(jnp.int32, (C, D), 1) == 0).astype(F32)
    diff = nt(G, first, HIGHEST) - nt(first, G, HIGHEST)
    dmask = jnp.where(incl, jnp.exp(jnp.where(incl, diff, 0.0)), 0.0)
    M = jnp.where(strict, nt(kb, k) * dmask, 0.0)
    T = _tri_inv(M, mm)
    eG = jnp.exp(G)
    u = mm(T, vb)
    w = mm(T, kb * eG)
    attn = jnp.where(incl, nt(q, k) * dmask, 0.0)
    v_new = u - mm(w, S)
    o = mm(q * eG, S) + mm(attn, v_new)
    g_last = mm_x(jnp.ones((C, C), F32), gb)
    g_last_d = mm_x(jnp.ones((D, C), F32), gb)
    S_new = S * jnp.exp(g_last_d) + tn(k * jnp.exp(g_last - G), v_new)
    on = o * lax.rsqrt(jnp.mean(o * o, axis=-1, keepdims=True) + NORM_EPS) * nw
    return on * _silu(z), S_new


def _rw_chunk_fwd(seqs, pars):
    NH, Lp, K = seqs[0].shape
    HB, C = RW_HEADS_PER_STEP, CHUNK
    nc = Lp // C

    def body(*refs):
        seq_refs, par_refs = refs[:7], refs[7:10]
        y_ref, ck_ref, h_ref = refs[10:]

        @pl.when(pl.program_id(1) == 0)
        def _():
            h_ref[...] = jnp.zeros_like(h_ref)

        H = h_ref[...]
        ck_ref[0] = H
        y, H_new = _rw_chunk_fn(H, *[r[...] for r in seq_refs], *[r[...] for r in par_refs])
        y_ref[...] = y
        h_ref[...] = H_new

    seq_spec = pl.BlockSpec((HB, C, K), lambda g, c: (g, c, 0))
    par_spec = pl.BlockSpec((HB, 1, K), lambda g, c: (g, 0, 0))
    return pl.pallas_call(
        body, name="rw_chunk_fwd", grid=(NH // HB, nc),
        in_specs=[seq_spec] * 7 + [par_spec] * 3,
        out_specs=[seq_spec, pl.BlockSpec((1, HB, K, K), lambda g, c: (c, g, 0, 0))],
        out_shape=[jax.ShapeDtypeStruct((NH, Lp, K), F32), jax.ShapeDtypeStruct((nc, NH, K, K), F32)],
        scratch_shapes=[pltpu.VMEM((HB, K, K), F32)],
        compiler_params=_params(("parallel", "arbitrary")),
    )(*seqs, *pars)


def _rw_chunk_bwd(seqs, pars, ckpt, dy):
    NH, Lp, K = seqs[0].shape
    HB, C = RW_HEADS_PER_STEP, CHUNK
    nc = Lp // C

    def body(*refs):
        seq_refs, par_refs = refs[:7], refs[7:10]
        ck_ref, dy_ref = refs[10:12]
        dseq_refs, dpar_refs, dh_ref = refs[12:19], refs[19:22], refs[22]

        @pl.when(pl.program_id(1) == 0)
        def _():
            dh_ref[...] = jnp.zeros_like(dh_ref)
            for o_ref in dpar_refs:
                o_ref[...] = jnp.zeros_like(o_ref)

        vals = [ck_ref[0]] + [r[...] for r in seq_refs] + [r[...] for r in par_refs]
        grads = jax.vjp(_rw_chunk_fn, *vals)[1]((dy_ref[...], dh_ref[...]))
        dh_ref[...] = grads[0]
        for o_ref, g in zip(dseq_refs, grads[1:8]):
            o_ref[...] = g
        for o_ref, g in zip(dpar_refs, grads[8:11]):
            o_ref[...] += g

    seq_spec = pl.BlockSpec((HB, C, K), lambda g, c: (g, nc - 1 - c, 0))
    par_spec = pl.BlockSpec((HB, 1, K), lambda g, c: (g, 0, 0))
    res = pl.pallas_call(
        body, name="rw_chunk_bwd", grid=(NH // HB, nc),
        in_specs=[seq_spec] * 7 + [par_spec] * 3
        + [pl.BlockSpec((1, HB, K, K), lambda g, c: (nc - 1 - c, g, 0, 0)), seq_spec],
        out_specs=[seq_spec] * 7 + [par_spec] * 3,
        out_shape=[jax.ShapeDtypeStruct((NH, Lp, K), F32)] * 7 + [jax.ShapeDtypeStruct((NH, 1, K), F32)] * 3,
        scratch_shapes=[pltpu.VMEM((HB, K, K), F32)],
        compiler_params=_params(("parallel", "arbitrary")),
    )(*seqs, *pars, ckpt, dy)
    return res[:7], res[7:]


def _dn_chunk_fwd(rows, nw):
    Lp = rows[0][0].shape[0]
    C, D, NH = CHUNK, DN_HEAD, DN_HEADS
    nc = Lp // C

    def body(*refs):
        in_refs, nw_ref = refs[:6], refs[6]
        y_ref, ck_ref, s_ref = refs[7:]

        @pl.when(pl.program_id(0) == 0)
        def _():
            s_ref[...] = jnp.zeros_like(s_ref)

        for h in range(NH):
            sl = slice(h * D, (h + 1) * D)
            S = s_ref[h]
            ck_ref[0, h] = S
            y, S_new = _dn_chunk_fn(S, *[r[:, sl] for r in in_refs], nw_ref[...])
            y_ref[:, sl] = y
            s_ref[h] = S_new

    return pl.pallas_call(
        body, name="dn_chunk_fwd", grid=(nc,),
        in_specs=[pl.BlockSpec((C, w), lambda c, cb=cb: (c, cb)) for (_, w, cb) in rows]
        + [pl.BlockSpec((1, D), lambda c: (0, 0))],
        out_specs=[pl.BlockSpec((C, NH * D), lambda c: (c, 0)),
                   pl.BlockSpec((1, NH, D, D), lambda c: (c, 0, 0, 0))],
        out_shape=[jax.ShapeDtypeStruct((Lp, NH * D), F32), jax.ShapeDtypeStruct((nc, NH, D, D), F32)],
        scratch_shapes=[pltpu.VMEM((NH, D, D), F32)],
        compiler_params=_params(("arbitrary",)),
    )(*[r[0] for r in rows], nw)


def _dn_chunk_bwd(rows, nw, ckpt, dy):
    Lp = rows[0][0].shape[0]
    C, D, NH = CHUNK, DN_HEAD, DN_HEADS
    nc = Lp // C

    def body(*refs):
        in_refs, nw_ref, ck_ref, dy_ref = refs[:6], refs[6], refs[7], refs[8]
        d_refs, dnw_ref, ds_ref = refs[9:15], refs[15], refs[16]

        @pl.when(pl.program_id(0) == 0)
        def _():
            ds_ref[...] = jnp.zeros_like(ds_ref)
            dnw_ref[...] = jnp.zeros_like(dnw_ref)

        for h in range(NH):
            sl = slice(h * D, (h + 1) * D)
            vals = [ck_ref[0, h]] + [r[:, sl] for r in in_refs] + [nw_ref[...]]
            grads = jax.vjp(_dn_chunk_fn, *vals)[1]((dy_ref[:, sl], ds_ref[h]))
            ds_ref[h] = grads[0]
            for o_ref, g in zip(d_refs, grads[1:7]):
                o_ref[:, sl] = g
            dnw_ref[...] += grads[7]

    rev = lambda c: nc - 1 - c
    dy_arr, dy_w, dy_cb = dy
    res = pl.pallas_call(
        body, name="dn_chunk_bwd", grid=(nc,),
        in_specs=[pl.BlockSpec((C, w), lambda c, cb=cb: (rev(c), cb)) for (_, w, cb) in rows]
        + [pl.BlockSpec((1, D), lambda c: (0, 0)),
           pl.BlockSpec((1, NH, D, D), lambda c: (rev(c), 0, 0, 0)),
           pl.BlockSpec((C, dy_w), lambda c: (rev(c), dy_cb))],
        out_specs=[pl.BlockSpec((C, NH * D), lambda c: (rev(c), 0))] * 6 + [pl.BlockSpec((1, D), lambda c: (0, 0))],
        out_shape=[jax.ShapeDtypeStruct((Lp, NH * D), F32)] * 6 + [jax.ShapeDtypeStruct((1, D), F32)],
        scratch_shapes=[pltpu.VMEM((NH, D, D), F32)],
        compiler_params=_params(("arbitrary",)),
    )(*[r[0] for r in rows], nw, ckpt, dy_arr)
    return res[:6], res[6]


def _loss_head(h, yw, tgt, fw, n_real, tm=ROW_TILE):
    Lp, Dm = h.shape
    tm = _pick(Lp, tm)

    def out_fn(z, fw_):
        return z * lax.rsqrt(jnp.mean(z * z, axis=-1, keepdims=True) + NORM_EPS) * fw_

    def body(h_ref, yw_ref, t_ref, fw_ref, loss_ref, dz_ref, dfw_ref):
        i = pl.program_id(0)

        @pl.when(i == 0)
        def _():
            loss_ref[...] = jnp.zeros_like(loss_ref)
            dfw_ref[...] = jnp.zeros_like(dfw_ref)

        row = i * tm + lax.broadcasted_iota(jnp.int32, (tm, 1), 0)
        mask = ((row >= N_META) & (row < n_real)).astype(F32)
        z = h_ref[...] + yw_ref[...]
        o, vjp = jax.vjp(out_fn, z, fw_ref[...])
        err = (o - t_ref[...]) * mask
        row_loss = 0.5 * jnp.mean(jnp.square(err), axis=-1, keepdims=True)
        dz, dfw = vjp(err * (1.0 / Dm))
        loss_ref[...] += jnp.sum(row_loss, axis=0, keepdims=True)
        dz_ref[...] = dz
        dfw_ref[...] += dfw

    row_spec = pl.BlockSpec((tm, Dm), lambda i: (i, 0))
    return pl.pallas_call(
        body, name="loss_head", grid=(Lp // tm,),
        in_specs=[row_spec, row_spec, row_spec, pl.BlockSpec((1, Dm), lambda i: (0, 0))],
        out_specs=[pl.BlockSpec((8, 128), lambda i: (0, 0)), row_spec, pl.BlockSpec((1, Dm), lambda i: (0, 0))],
        out_shape=[jax.ShapeDtypeStruct((8, 128), F32), jax.ShapeDtypeStruct((Lp, Dm), F32),
                   jax.ShapeDtypeStruct((1, Dm), F32)],
        compiler_params=_params(("arbitrary",)),
    )(h, yw, tgt, fw)


def _exchange(name, x, masks, slot_kind, per_dest):
    n_slots = {"chip": 4, "core": 2, "dev": 8}[slot_kind]
    blk_shape = x.shape[1:] if per_dest else x.shape
    n = len(masks)

    def body(x_ref, o_ref, send_sems, recv_sems, local_sem):
        mx, my, mc = lax.axis_index("x"), lax.axis_index("y"), lax.axis_index("c")

        def slot(px, py, pc):
            return {"chip": 2 * px + py, "core": pc, "dev": 4 * px + 2 * py + pc}[slot_kind]

        def peer(m):
            return (mx + m[0]) % 2, (my + m[1]) % 2, (mc + m[2]) % 2

        own_src = x_ref.at[2 * mx + my] if per_dest else x_ref
        own = pltpu.make_async_copy(own_src, o_ref.at[slot(mx, my, mc)], local_sem)
        own.start()
        sends = []
        for k, m in enumerate(masks):
            px, py, pc = peer(m)
            src = x_ref.at[2 * px + py] if per_dest else x_ref
            sends.append(pltpu.make_async_remote_copy(
                src_ref=src, dst_ref=o_ref.at[slot(mx, my, mc)], send_sem=send_sems.at[k],
                recv_sem=recv_sems.at[k], device_id=(px, py, pc), device_id_type=MESH))
        for cp in sends:
            cp.start()
        for k, m in enumerate(masks):
            px, py, pc = peer(m)
            pltpu.make_async_remote_copy(
                src_ref=own_src, dst_ref=o_ref.at[slot(px, py, pc)], send_sem=send_sems.at[k],
                recv_sem=recv_sems.at[k], device_id=(px, py, pc), device_id_type=MESH).wait_recv()
        for cp in sends:
            cp.wait_send()
        own.wait()

    return pl.pallas_call(
        body, name=name,
        in_specs=[pl.BlockSpec(memory_space=pl.ANY)], out_specs=pl.BlockSpec(memory_space=pl.ANY),
        out_shape=jax.ShapeDtypeStruct((n_slots,) + tuple(blk_shape), x.dtype),
        scratch_shapes=[pltpu.SemaphoreType.DMA((n,)), pltpu.SemaphoreType.DMA((n,)), pltpu.SemaphoreType.DMA],
        compiler_params=pltpu.CompilerParams(has_side_effects=True),
    )(x)


CHIP_MASKS = [(1, 0, 0), (0, 1, 0), (1, 1, 0)]
CORE_MASKS = [(0, 0, 1)]
ALL_MASKS = [(dx, dy, dc) for dx in (0, 1) for dy in (0, 1) for dc in (0, 1) if (dx, dy, dc) != (0, 0, 0)]


def _gather_chips(name, x):
    return _exchange(name, x, CHIP_MASKS, "chip", False)


def _sum_slots(name, x, tr=128):
    S, R, N = x.shape
    tr = _pick(R, tr)

    def body(x_ref, o_ref):
        acc = x_ref[0]
        for s in range(1, S):
            acc = acc + x_ref[s]
        o_ref[...] = acc

    return pl.pallas_call(
        body, name=name, grid=(R // tr,),
        in_specs=[pl.BlockSpec((S, tr, N), lambda i: (0, i, 0))], out_specs=pl.BlockSpec((tr, N), lambda i: (i, 0)),
        out_shape=jax.ShapeDtypeStruct((R, N), F32), compiler_params=_params(("parallel",)),
    )(x)


def _adamw(name, gparts, w, m, v, tr=128):
    S, R, N = gparts.shape
    tr = _pick(R, tr)
    c1 = 1.0 / (1.0 - ADAM_B1 ** ADAM_STEP)
    c2 = 1.0 / (1.0 - ADAM_B2 ** ADAM_STEP)

    def body(g_ref, w_ref, m_ref, v_ref, go_ref, d_ref, mo_ref, vo_ref):
        g = g_ref[0]
        for s in range(1, S):
            g = g + g_ref[s]
        m_new = ADAM_B1 * m_ref[...] + (1.0 - ADAM_B1) * g
        v_new = ADAM_B2 * v_ref[...] + (1.0 - ADAM_B2) * jnp.square(g)
        go_ref[...] = g
        mo_ref[...] = m_new
        vo_ref[...] = v_new
        d_ref[...] = -ADAM_LR * ((m_new * c1) / (jnp.sqrt(v_new * c2) + ADAM_EPS) + ADAM_WD * w_ref[...])

    spec = pl.BlockSpec((tr, N), lambda i: (i, 0))
    return pl.pallas_call(
        body, name=name, grid=(R // tr,),
        in_specs=[pl.BlockSpec((S, tr, N), lambda i: (0, i, 0)), spec, spec, spec], out_specs=[spec] * 4,
        out_shape=[jax.ShapeDtypeStruct((R, N), F32)] * 4, compiler_params=_params(("parallel",)),
    )(gparts, w, m, v)


def _reduce_to_shard(name, slabs):
    got = _exchange(name + "_chips", slabs, CHIP_MASKS, "chip", True)
    part = _sum_slots(name + "_sum", got)
    return _exchange(name + "_cores", part, CORE_MASKS, "core", False)


def _pack(pieces, cols):
    flat = jnp.concatenate([p.reshape(-1) for p in pieces])
    rows = -(-flat.shape[0] // cols)
    rows = -(-rows // 8) * 8
    return jnp.pad(flat, (0, rows * cols - flat.shape[0])).reshape(rows, cols)


def _unpack(packed, shapes):
    flat = packed.reshape(-1)
    out, off = [], 0
    for shp in shapes:
        n = 1
        for d in shp:
            n *= d
        out.append(flat[off:off + n].reshape(shp))
        off += n
    return out


def _heads_major(x):
    return jnp.transpose(x.reshape(x.shape[0], RW_HEADS, RW_HEAD), (1, 0, 2))


def _heads_minor(x):
    return jnp.transpose(x, (1, 0, 2)).reshape(x.shape[1], RW_WIDTH)


def _to_my_layout(w):
    z = jnp.zeros((w.shape[0], 112), w.dtype)
    return jnp.concatenate([w[:, 0:3072], w[:, 3200:4224], w[:, 4224:7296], w[:, 7312:8336],
                            w[:, 3072:3200], w[:, 7296:7312], z], axis=1)


def _from_my_layout(w):
    return jnp.concatenate([w[:, 0:3072], w[:, 8192:8320], w[:, 3072:4096], w[:, 4096:7168],
                            w[:, 8320:8336], w[:, 7168:8192]], axis=1)


def kernel(x, meta_tokens, norm_w, w_in, rw_shift_mu, rw_w0, rw_w2, rw_a0, rw_a2, rw_k_k, rw_k_a, rw_r_k, rw_gn_w, rw_gn_b, dn_conv_w, dn_A_log, dn_dt_bias, dn_norm_w, w_out, final_norm_w, loss_target, m_meta_tokens, m_norm_w, m_w_in, m_rw_shift_mu, m_rw_w0, m_rw_w2, m_rw_a0, m_rw_a2, m_rw_k_k, m_rw_k_a, m_rw_r_k, m_rw_gn_w, m_rw_gn_b, m_dn_conv_w, m_dn_A_log, m_dn_dt_bias, m_dn_norm_w, m_w_out, m_final_norm_w, v_meta_tokens, v_norm_w, v_w_in, v_rw_shift_mu, v_rw_w0, v_rw_w2, v_rw_a0, v_rw_a2, v_rw_k_k, v_rw_k_a, v_rw_r_k, v_rw_gn_w, v_rw_gn_b, v_dn_conv_w, v_dn_A_log, v_dn_dt_bias, v_dn_norm_w, v_w_out, v_final_norm_w):
    S = x.shape[1]
    L = N_META + S
    Lp = -(-L // CHUNK) * CHUNK

    small_shapes = [(RW_LORA, 256), (RW_LORA, 256), (CONV_W, 768), (N_META, 512)]
    small_mine = _pack([rw_w2[0], rw_a2[0], dn_conv_w[0], meta_tokens], 1024)
    small_all = _gather_chips("gather_small", small_mine)
    per_chip = [_unpack(small_all[s], small_shapes) for s in range(N_CHIPS)]
    w2, a2, conv_w, meta = [jnp.concatenate([per_chip[s][i] for s in range(N_CHIPS)], axis=1) for i in range(4)]
    w_in_all = _gather_chips("gather_w_in", w_in[0].astype(BF16))
    W = _to_my_layout(jnp.concatenate([w_in_all[s] for s in range(N_CHIPS)], axis=1))
    w_out_all = _gather_chips("gather_w_out", w_out[0].astype(BF16))
    Wo = w_out_all.reshape(D_MODEL, D_MODEL)

    tail = [jnp.zeros((Lp - L, D_MODEL), F32)] if Lp > L else []
    h = jnp.concatenate([meta, x[0]] + tail, axis=0)
    tgt = jnp.concatenate([jnp.zeros((N_META, D_MODEL), F32), loss_target[0]] + tail, axis=0)
    (u,) = _rowwise("rms_in", _rms_fn, [_row(h)], [norm_w], [D_MODEL])
    p = _mm("in_proj", u, W, "nn")

    mu = rw_shift_mu
    zpad = jnp.zeros((RW_LORA, RW_WIDTH), F32)
    rw_params = [mu[:, 0:1024], mu[:, 1024:2048], mu[:, 2048:3072], mu[:, 3072:3200], rw_w0,
                 jnp.concatenate([w2, zpad], axis=0), rw_a0, jnp.concatenate([zpad, a2], axis=0), rw_k_k, rw_k_a]
    rw_in = [_row(p, 1024, CB_R), _row(p, 1024, CB_K), _row(p, 1024, CB_V), _row(p, 128, CB_S1)]
    rw_rows = []
    for i, rin in enumerate(rw_in):
        rw_rows += [rin, _row(_shift_sum("rw_shift%d" % i, [(rin, 1)]))]
    rw_seq = _rowwise("rw_prep", _rw_prep_fn, rw_rows, rw_params, [RW_WIDTH] * 6)
    gate = p[:, CB_GATE * 1024:(CB_GATE + 1) * 1024]
    rw_seq_hm = [_heads_major(t) for t in rw_seq] + [_heads_major(gate)]
    rw_pars_hm = [t.reshape(RW_HEADS, 1, RW_HEAD) for t in (rw_r_k, rw_gn_w, rw_gn_b)]
    ya_hm, rw_ck = _rw_chunk_fwd(rw_seq_hm, rw_pars_hm)
    ya = _heads_minor(ya_hm)

    conv_rows, conv_w_rows, dn_c = [], [], []
    for i, cb in enumerate((CB_DQ, CB_DK, CB_DV)):
        src = _row(p, 1024, cb)
        rows_i = [src] + [_row(_shift_sum("dn_shift%d_%d" % (i, j), [(src, j)])) for j in (1, 2, 3)]
        conv_rows.append(rows_i)
        conv_w_rows.append([conv_w[j:j + 1, 1024 * i:1024 * (i + 1)] for j in range(CONV_W)])
        (c_i,) = _rowwise("dn_conv%d" % i, _conv_fn, rows_i, conv_w_rows[i], [DN_WIDTH])
        dn_c.append(c_i)
    a_log_b = jnp.repeat(dn_A_log, DN_HEAD, axis=1)
    dt_b = jnp.repeat(dn_dt_bias, DN_HEAD, axis=1)
    beta_b, g_b = _rowwise("dn_gate", _dn_gate_fn, [_row(p, 128, CB_S2)], [a_log_b, dt_b], [DN_WIDTH] * 2)
    dn_rows = [_row(dn_c[0]), _row(dn_c[1]), _row(dn_c[2]), _row(beta_b), _row(g_b), _row(p, 1024, CB_Z)]
    yb, dn_ck = _dn_chunk_fwd(dn_rows, dn_norm_w)

    y = jnp.concatenate([ya, yb], axis=1)
    yw = _mm("out_proj", y, Wo, "nn", tn=1024)
    loss_acc, dz, d_fw = _loss_head(h, yw, tgt, final_norm_w.reshape(1, D_MODEL), L)
    loss = lax.psum(loss_acc[0, 0], ("x", "y", "c"))

    dy = _mm("d_out_proj", dz, Wo, "nt", tn=1024, tk=2048)
    d_wo = _mm("d_w_out", y, dz, "tn", tm=1024, tn=1024, tk=MM_ROW_TILE)

    d_rw_hm, d_rw_pars = _rw_chunk_bwd(rw_seq_hm, rw_pars_hm, rw_ck, _heads_major(dy[:, :RW_WIDTH]))
    d_rw_seq = [_heads_minor(t) for t in d_rw_hm]
    d_gate = d_rw_seq[6]
    d_prep_rows, d_prep_pars = _rowwise_bwd("rw_prep_bwd", _rw_prep_fn, rw_rows, rw_params,
                                            [_row(t) for t in d_rw_seq[:6]])
    dp_rw = [_shift_sum("rw_shift_bwd%d" % i, [(_row(d_prep_rows[2 * i]), 0), (_row(d_prep_rows[2 * i + 1]), -1)])
             for i in range(4)]

    d_dn, d_dn_nw = _dn_chunk_bwd(dn_rows, dn_norm_w, dn_ck, _row(dy, 1024, 1))
    (d_ps2,), (d_a_log_b, d_dt_b) = _rowwise_bwd("dn_gate_bwd", _dn_gate_fn, [_row(p, 128, CB_S2)], [a_log_b, dt_b],
                                                 [_row(d_dn[3]), _row(d_dn[4])])
    dp_dn, d_conv_parts = [], []
    for i in range(3):
        d_us, d_cw = _rowwise_bwd("dn_conv_bwd%d" % i, _conv_fn, conv_rows[i], conv_w_rows[i], [_row(d_dn[i])])
        dp_dn.append(_shift_sum("dn_shift_bwd%d" % i, [(_row(d_us[j]), -j) for j in range(4)]))
        d_conv_parts.append(jnp.concatenate(d_cw, axis=0))

    dp = jnp.concatenate(dp_rw[:3] + [d_gate] + dp_dn + [d_dn[5], dp_rw[3], d_ps2], axis=1)
    du = _mm("d_in_proj", dp, W, "nt", tn=1024, tk=1408)
    d_W = _mm("d_w_in", u, dp, "tn", tm=1024, tn=1408, tk=MM_ROW_TILE)
    (dh1,), (d_norm_w,) = _rowwise_bwd("rms_in_bwd", _rms_fn, [_row(h)], [norm_w], [_row(du)])
    (dh,) = _rowwise("dh_sum", lambda a_, b_: (a_ + b_,), [_row(dz), _row(dh1)], [], [D_MODEL])
    grad_x = dh[N_META:L][None]

    d_mu = jnp.concatenate(d_prep_pars[0:4], axis=1)
    d_w2, d_a2 = d_prep_pars[5][:RW_LORA], d_prep_pars[7][RW_LORA:]
    d_conv = jnp.concatenate(d_conv_parts, axis=1)
    d_meta = dh[:N_META]
    head_sum = lambda t: jnp.sum(t.reshape(1, DN_HEADS, DN_HEAD), axis=-1)
    rep_names = ["norm_w", "rw_shift_mu", "rw_w0", "rw_a0", "rw_k_k", "rw_k_a", "rw_r_k", "rw_gn_w", "rw_gn_b",
                 "dn_A_log", "dn_dt_bias", "dn_norm_w", "final_norm_w"]
    rep_g = [d_norm_w, d_mu, d_prep_pars[4], d_prep_pars[6], d_prep_pars[8], d_prep_pars[9],
             d_rw_pars[0].reshape(1, RW_WIDTH), d_rw_pars[1].reshape(1, RW_WIDTH), d_rw_pars[2].reshape(1, RW_WIDTH),
             head_sum(d_a_log_b), head_sum(d_dt_b), d_dn_nw, d_fw.reshape(D_MODEL)]
    rep_w = [norm_w, rw_shift_mu, rw_w0, rw_a0, rw_k_k, rw_k_a, rw_r_k, rw_gn_w, rw_gn_b, dn_A_log, dn_dt_bias,
             dn_norm_w, final_norm_w]
    rep_m = [m_norm_w, m_rw_shift_mu, m_rw_w0, m_rw_a0, m_rw_k_k, m_rw_k_a, m_rw_r_k, m_rw_gn_w, m_rw_gn_b,
             m_dn_A_log, m_dn_dt_bias, m_dn_norm_w, m_final_norm_w]
    rep_v = [v_norm_w, v_rw_shift_mu, v_rw_w0, v_rw_a0, v_rw_k_k, v_rw_k_a, v_rw_r_k, v_rw_gn_w, v_rw_gn_b,
             v_dn_A_log, v_dn_dt_bias, v_dn_norm_w, v_final_norm_w]
    rep_shapes = [t.shape for t in rep_w]
    rep_all = _exchange("gather_rep_grads", _pack(rep_g, 128), ALL_MASKS, "dev", False)
    rep_out = _adamw("adam_rep", rep_all, _pack(rep_w, 128), _pack(rep_m, 128), _pack(rep_v, 128))
    rep_out = [dict(zip(rep_names, _unpack(t, rep_shapes))) for t in rep_out]

    sm_slabs = jnp.stack([_pack([d_w2[:, 256 * s:256 * (s + 1)], d_a2[:, 256 * s:256 * (s + 1)],
                                 d_conv[:, 768 * s:768 * (s + 1)], d_meta[:, 512 * s:512 * (s + 1)]], 1024)
                          for s in range(N_CHIPS)])
    sm_parts = _reduce_to_shard("rs_small", sm_slabs)
    sm_w = [rw_w2[0], rw_a2[0], dn_conv_w[0], meta_tokens]
    sm_m = [m_rw_w2[0], m_rw_a2[0], m_dn_conv_w[0], m_meta_tokens]
    sm_v = [v_rw_w2[0], v_rw_a2[0], v_dn_conv_w[0], v_meta_tokens]
    sm_out = _adamw("adam_small", sm_parts, _pack(sm_w, 1024), _pack(sm_m, 1024), _pack(sm_v, 1024))
    sm_names = ["rw_w2", "rw_a2", "dn_conv_w", "meta_tokens"]
    sm_full_shapes = [(1, RW_LORA, 256), (1, RW_LORA, 256), (1, CONV_W, 768), (N_META, 512)]
    sm_out = [dict(zip(sm_names, [t.reshape(shp) for t, shp in zip(_unpack(o, small_shapes), sm_full_shapes)]))
              for o in sm_out]

    wo_parts = _reduce_to_shard("rs_w_out", d_wo.reshape(N_CHIPS, D_MODEL // N_CHIPS, D_MODEL))
    wo_out = _adamw("adam_w_out", wo_parts, w_out[0], m_w_out[0], v_w_out[0])
    d_w_in_slabs = jnp.transpose(_from_my_layout(d_W).reshape(D_MODEL, N_CHIPS, SHARD_COLS), (1, 0, 2))
    wi_parts = _reduce_to_shard("rs_w_in", d_w_in_slabs)
    wi_out = _adamw("adam_w_in", wi_parts, w_in[0], m_w_in[0], v_w_in[0])

    order = ["meta_tokens", "norm_w", "w_in", "rw_shift_mu", "rw_w0", "rw_w2", "rw_a0", "rw_a2", "rw_k_k", "rw_k_a",
             "rw_r_k", "rw_gn_w", "rw_gn_b", "dn_conv_w", "dn_A_log", "dn_dt_bias", "dn_norm_w", "w_out",
             "final_norm_w"]
    outs = [loss, grad_x]
    for kind in range(4):
        table = dict(rep_out[kind])
        table.update(sm_out[kind])
        table["w_in"] = wi_out[kind][None]
        table["w_out"] = wo_out[kind][None]
        outs += [table[n] for n in order]
    return tuple(outs)
```

```python
import functools

import jax
import jax.numpy as jnp
from jax import lax
from jax.experimental import pallas as pl
from jax.experimental.pallas import tpu as pltpu

F32 = jnp.float32
BF16 = jnp.bfloat16
HIGH = lax.Precision.HIGH
HIGHEST = lax.Precision.HIGHEST
MESH = pl.DeviceIdType.MESH

D_MODEL = 2048
N_META = 16
RW_WIDTH = 1024
RW_HEAD = 64
RW_HEADS = 16
RW_LORA = 64
RW_GN_EPS = 64e-5
DN_WIDTH = 1024
DN_HEAD = 128
DN_HEADS = 8
CONV_W = 4
CHUNK = 64
NORM_EPS = 1e-6
IN_COLS = 8336
N_CHIPS = 4
SHARD_COLS = IN_COLS // N_CHIPS

NP_COLS = 8 * 1024 + 256
CB_R, CB_K, CB_V, CB_GATE, CB_DQ, CB_DK, CB_DV, CB_Z = range(8)
CB_S1 = 8192 // 128
CB_S2 = CB_S1 + 1

ADAM_LR = 0.001
ADAM_B1 = 0.9
ADAM_B2 = 0.999
ADAM_EPS = 1e-08
ADAM_WD = 0.01
ADAM_STEP = 10

VMEM_LIMIT_BYTES = 56 * 1024 * 1024
RW_HEADS_PER_STEP = 4
ROW_TILE = 104
MM_ROW_TILE = 520


def _params(sem=None):
    return pltpu.CompilerParams(dimension_semantics=sem, vmem_limit_bytes=VMEM_LIMIT_BYTES)


def _pick(n, target, mult=8):
    best = None
    for d in range(mult, min(n, target) + 1, mult):
        if n % d == 0:
            best = d
    return n if best is None else best


def _mm(name, a, b, mode, tm=MM_ROW_TILE, tn=1408, tk=2048):
    if mode == "nn":
        (M, K), (_, N) = a.shape, b.shape
    elif mode == "nt":
        (M, K), (N, _) = a.shape, b.shape
    else:
        (K, M), (_, N) = a.shape, b.shape
    tm = _pick(M, tm, 128 if mode == "tn" else 8)
    tn = _pick(N, tn, 128)
    tk = _pick(K, tk, 8 if mode == "tn" else 128)
    if mode == "nn":
        a_spec = pl.BlockSpec((tm, tk), lambda i, j, k: (i, k))
        b_spec = pl.BlockSpec((tk, tn), lambda i, j, k: (k, j))
        dims = (((1,), (0,)), ((), ()))
    elif mode == "nt":
        a_spec = pl.BlockSpec((tm, tk), lambda i, j, k: (i, k))
        b_spec = pl.BlockSpec((tn, tk), lambda i, j, k: (j, k))
        dims = (((1,), (1,)), ((), ()))
    else:
        a_spec = pl.BlockSpec((tk, tm), lambda i, j, k: (k, i))
        b_spec = pl.BlockSpec((tk, tn), lambda i, j, k: (k, j))
        dims = (((0,), (0,)), ((), ()))

    def body(a_ref, b_ref, o_ref):
        @pl.when(pl.program_id(2) == 0)
        def _():
            o_ref[...] = jnp.zeros_like(o_ref)

        o_ref[...] += lax.dot_general(a_ref[...].astype(BF16), b_ref[...].astype(BF16), dims,
                                      preferred_element_type=F32)

    return pl.pallas_call(
        body, name=name, grid=(M // tm, N // tn, K // tk),
        in_specs=[a_spec, b_spec], out_specs=pl.BlockSpec((tm, tn), lambda i, j, k: (i, j)),
        out_shape=jax.ShapeDtypeStruct((M, N), F32),
        compiler_params=_params(("parallel", "parallel", "arbitrary")),
    )(a, b)


def _row(arr, width=None, cb=0):
    return (arr, arr.shape[1] if width is None else width, cb)


def _rowwise(name, fn, rows, params, out_widths, tm=ROW_TILE):
    R = rows[0][0].shape[0]
    tm = _pick(R, tm)
    n_r, n_p = len(rows), len(params)

    def body(*refs):
        vals = [r[...] for r in refs[:n_r + n_p]]
        for o_ref, val in zip(refs[n_r + n_p:], fn(*vals)):
            o_ref[...] = val

    in_specs = [pl.BlockSpec((tm, w), lambda i, cb=cb: (i, cb)) for (_, w, cb) in rows]
    in_specs += [pl.BlockSpec(p.shape, lambda i: (0, 0)) for p in params]
    return pl.pallas_call(
        body, name=name, grid=(R // tm,), in_specs=in_specs,
        out_specs=[pl.BlockSpec((tm, w), lambda i: (i, 0)) for w in out_widths],
        out_shape=[jax.ShapeDtypeStruct((R, w), F32) for w in out_widths],
        compiler_params=_params(("parallel",)),
    )(*[r[0] for r in rows], *params)


def _rowwise_bwd(name, fn, rows, params, douts, tm=ROW_TILE):
    R = rows[0][0].shape[0]
    tm = _pick(R, tm)
    n_r, n_p, n_d = len(rows), len(params), len(douts)

    def body(*refs):
        vals = [r[...] for r in refs[:n_r + n_p]]
        cts = tuple(r[...] for r in refs[n_r + n_p:n_r + n_p + n_d])
        grads = jax.vjp(fn, *vals)[1](cts)
        outs = refs[n_r + n_p + n_d:]
        for o_ref, g in zip(outs[:n_r], grads[:n_r]):
            o_ref[...] = g

        @pl.when(pl.program_id(0) == 0)
        def _():
            for o_ref in outs[n_r:]:
                o_ref[...] = jnp.zeros_like(o_ref)

        for o_ref, g in zip(outs[n_r:], grads[n_r:]):
            o_ref[...] += g

    in_specs = [pl.BlockSpec((tm, w), lambda i, cb=cb: (i, cb)) for (_, w, cb) in rows]
    in_specs += [pl.BlockSpec(p.shape, lambda i: (0, 0)) for p in params]
    in_specs += [pl.BlockSpec((tm, w), lambda i, cb=cb: (i, cb)) for (_, w, cb) in douts]
    out_specs = [pl.BlockSpec((tm, w), lambda i: (i, 0)) for (_, w, _) in rows]
    out_specs += [pl.BlockSpec(p.shape, lambda i: (0, 0)) for p in params]
    out_shape = [jax.ShapeDtypeStruct((R, w), F32) for (_, w, _) in rows]
    out_shape += [jax.ShapeDtypeStruct(p.shape, F32) for p in params]
    res = pl.pallas_call(
        body, name=name, grid=(R // tm,), in_specs=in_specs, out_specs=out_specs, out_shape=out_shape,
        compiler_params=_params(("arbitrary",)),
    )(*[r[0] for r in rows], *params, *[d[0] for d in douts])
    return res[:n_r], res[n_r:]


def _shift_sum(name, terms, tm=ROW_TILE):
    R = terms[0][0][0].shape[0]
    w = terms[0][0][1]
    tm = _pick(R, tm)
    nt, nb8 = R // tm, tm // 8
    shifts = [j for (_, j) in terms]

    def body(*refs):
        i = pl.program_id(0)
        acc = None
        for k, j in enumerate(shifts):
            x = refs[2 * k][...]
            if j > 0:
                halo = jnp.where(i == 0, 0.0, refs[2 * k + 1][...])
                x = pltpu.roll(jnp.concatenate([halo, x], axis=0), j, 0)[8:, :]
            elif j < 0:
                halo = jnp.where(i == nt - 1, 0.0, refs[2 * k + 1][...])
                x = pltpu.roll(jnp.concatenate([x, halo], axis=0), tm + 8 + j, 0)[:tm, :]
            acc = x if acc is None else acc + x
        refs[-1][...] = acc

    in_specs, args = [], []
    for (arr, _, cb), j in terms:
        in_specs.append(pl.BlockSpec((tm, w), lambda i, cb=cb: (i, cb)))
        if j > 0:
            in_specs.append(pl.BlockSpec((8, w), lambda i, cb=cb: (jnp.maximum(i * nb8 - 1, 0), cb)))
        else:
            in_specs.append(pl.BlockSpec((8, w), lambda i, cb=cb: (jnp.minimum((i + 1) * nb8, R // 8 - 1), cb)))
        args += [arr, arr]
    return pl.pallas_call(
        body, name=name, grid=(nt,), in_specs=in_specs, out_specs=pl.BlockSpec((tm, w), lambda i: (i, 0)),
        out_shape=jax.ShapeDtypeStruct((R, w), F32), compiler_params=_params(("parallel",)),
    )(*args)


def _softplus(x):
    return jnp.maximum(x, 0.0) + jnp.log(1.0 + jnp.exp(-jnp.abs(x)))


def _silu(x):
    return x * jax.nn.sigmoid(x)


def _rms_fn(h, w):
    return (h * lax.rsqrt(jnp.mean(h * h, axis=-1, keepdims=True) + NORM_EPS) * w,)


def _rw_prep_fn(pr, pr1, pk, pk1, pv, pv1, ps, ps1, mu_r, mu_k, mu_v, mu_s, w0, w2p, a0, a2p, k_k, k_a):
    r = pr + (pr1 - pr) * mu_r
    k = pk + (pk1 - pk) * mu_k
    v = pv + (pv1 - pv) * mu_v
    s = ps + (ps1 - ps) * mu_s
    w_log = -_softplus(-(w0 + jnp.dot(jnp.tanh(s), w2p, precision=HIGH, preferred_element_type=F32))) - 0.5
    log_decay = -jnp.exp(w_log)
    a = jax.nn.sigmoid(a0 + jnp.dot(s, a2p, precision=HIGH, preferred_element_type=F32))
    return r, log_decay, k * (1.0 + (a - 1.0) * k_a), v, k * k_k, a


def _conv_fn(u0, u1, u2, u3, w0, w1, w2, w3):
    return (_silu(u0 * w3 + u1 * w2 + u2 * w1 + u3 * w0),)


def _dn_gate_fn(ps2, a_log_b, dt_b):
    lane = lax.broadcasted_iota(jnp.int32, (DN_HEAD, DN_WIDTH), 0)
    head = jnp.right_shift(lax.broadcasted_iota(jnp.int32, (DN_HEAD, DN_WIDTH), 1), 7)
    sel_b = (lane == head).astype(F32)
    sel_a = (lane == head + DN_HEADS).astype(F32)
    b = jnp.dot(ps2, sel_b, precision=HIGHEST, preferred_element_type=F32)
    al = jnp.dot(ps2, sel_a, precision=HIGHEST, preferred_element_type=F32)
    return jax.nn.sigmoid(b), -jnp.exp(a_log_b) * _softplus(al + dt_b)


def _tril_masks(c):
    t = lax.broadcasted_iota(jnp.int32, (c, c), 0)
    s = lax.broadcasted_iota(jnp.int32, (c, c), 1)
    return s <= t, s < t


def _dot(x, y, cx, cy, prec=HIGH):
    nb = x.ndim - 2
    batch = tuple(range(nb))
    return lax.dot_general(x, y, (((cx + nb,), (cy + nb,)), (batch, batch)), precision=prec,
                           preferred_element_type=F32)


def _tri_inv(a):
    c = a.shape[-1]
    eye = (lax.broadcasted_iota(jnp.int32, (c, c), 0) == lax.broadcasted_iota(jnp.int32, (c, c), 1)).astype(F32)
    x = eye - a
    p = a
    n = 2
    while n < c:
        p = _dot(p, p, 1, 0)
        x = x + _dot(x, p, 1, 0)
        n *= 2
    return x


@jax.custom_vjp
def _tri_inv_saved(a, t):
    return t


def _tri_inv_saved_fwd(a, t):
    return t, t


def _tri_inv_saved_bwd(t, dt):
    return -_dot(_dot(t, dt, 0, 0), t, 1, 1), jnp.zeros_like(t)


_tri_inv_saved.defvjp(_tri_inv_saved_fwd, _tri_inv_saved_bwd)


def _rw_chunk_fn(H, r, lw, k2, v, kkp, a, gate, rk, gnw, gnb, t_saved=None):
    B, C, _ = r.shape
    bmm = lambda x, y: _dot(x, y, 1, 0)
    bmm_nt = lambda x, y: _dot(x, y, 1, 1)
    bmm_tn = lambda x, y: _dot(x, y, 0, 0)
    incl, strict = _tril_masks(C)
    tril = jnp.broadcast_to(incl.astype(F32), (B, C, C))
    kk = kkp * lax.rsqrt(jnp.sum(kkp * kkp, axis=-1, keepdims=True) + 1e-6)
    b = kk * a
    g_incl = _dot(tril, lw, 1, 0, HIGHEST)
    g_excl = g_incl - lw
    inv = jnp.exp(-g_incl)
    alpha, beta, kappa, rho = kk * jnp.exp(g_excl), b * inv, k2 * inv, r * jnp.exp(g_incl)
    a_ab = jnp.where(strict, bmm_nt(alpha, beta), 0.0)
    a_ak = jnp.where(strict, bmm_nt(alpha, kappa), 0.0)
    t_inv = _tri_inv(a_ab) if t_saved is None else _tri_inv_saved(a_ab, t_saved)
    u = bmm(t_inv, -bmm(alpha, H) - bmm(a_ak, v))
    y = (bmm(rho, H) + bmm(jnp.where(incl, bmm_nt(rho, beta), 0.0), u)
         + bmm(jnp.where(incl, bmm_nt(rho, kappa), 0.0), v))
    g_tot = _dot(lw, jnp.ones((B, C, H.shape[-1]), F32), 0, 0, HIGHEST)
    H_new = jnp.exp(g_tot) * (H + bmm_tn(beta, u) + bmm_tn(kappa, v))
    mean = jnp.mean(y, axis=-1, keepdims=True)
    var = jnp.mean(jnp.square(y - mean), axis=-1, keepdims=True)
    yn = (y - mean) * lax.rsqrt(var + RW_GN_EPS) * gnw + gnb
    bonus = jnp.sum(r * k2 * rk, axis=-1, keepdims=True) * v
    return (yn + bonus) * _silu(gate), H_new, t_inv


def _dn_chunk_fn(S, qc, kc, vc, bb, gb, z, nw, t_saved=None):
    C, D = qc.shape
    mm = lambda x, y: _dot(x, y, 1, 0)
    mm_x = lambda x, y: _dot(x, y, 1, 0, HIGHEST)
    nt = lambda x, y, p=HIGH: _dot(x, y, 1, 1, p)
    tn = lambda x, y: _dot(x, y, 0, 0)
    incl, strict = _tril_masks(C)
    q = qc * lax.rsqrt(jnp.sum(qc * qc, axis=-1, keepdims=True) + 1e-6) * (D ** -0.5)
    k = kc * lax.rsqrt(jnp.sum(kc * kc, axis=-1, keepdims=True) + 1e-6)
    kb, vb = k * bb, vc * bb
    G = mm_x(incl.astype(F32), gb)
    first = (lax.broadcasted_iota(jnp.int32, (C, D), 1) == 0).astype(F32)
    diff = nt(G, first, HIGHEST) - nt(first, G, HIGHEST)
    dmask = jnp.where(incl, jnp.exp(jnp.where(incl, diff, 0.0)), 0.0)
    M = jnp.where(strict, nt(kb, k) * dmask, 0.0)
    T = _tri_inv(M) if t_saved is None else _tri_inv_saved(M, t_saved)
    eG = jnp.exp(G)
    u = mm(T, vb)
    w = mm(T, kb * eG)
    attn = jnp.where(incl, nt(q, k) * dmask, 0.0)
    v_new = u - mm(w, S)
    o = mm(q * eG, S) + mm(attn, v_new)
    g_last = mm_x(jnp.ones((C, C), F32), gb)
    g_last_d = mm_x(jnp.ones((D, C), F32), gb)
    S_new = S * jnp.exp(g_last_d) + tn(k * jnp.exp(g_last - G), v_new)
    on = o * lax.rsqrt(jnp.mean(o * o, axis=-1, keepdims=True) + NORM_EPS) * nw
    return on * _silu(z), S_new, T


def _rw_chunk_fwd(seqs, pars):
    NH, Lp, K = seqs[0].shape
    HB, C = RW_HEADS_PER_STEP, CHUNK
    nc = Lp // C

    def body(*refs):
        seq_refs, par_refs = refs[:7], refs[7:10]
        y_ref, ck_ref, t_ref, h_ref = refs[10:]

        @pl.when(pl.program_id(1) == 0)
        def _():
            h_ref[...] = jnp.zeros_like(h_ref)

        H = h_ref[...]
        ck_ref[0] = H
        y, H_new, t_inv = _rw_chunk_fn(H, *[r[...] for r in seq_refs], *[r[...] for r in par_refs])
        y_ref[...] = y
        t_ref[0] = t_inv
        h_ref[...] = H_new

    seq_spec = pl.BlockSpec((HB, C, K), lambda g, c: (g, c, 0))
    par_spec = pl.BlockSpec((HB, 1, K), lambda g, c: (g, 0, 0))
    return pl.pallas_call(
        body, name="rw_chunk_fwd", grid=(NH // HB, nc),
        in_specs=[seq_spec] * 7 + [par_spec] * 3,
        out_specs=[seq_spec, pl.BlockSpec((1, HB, K, K), lambda g, c: (c, g, 0, 0)),
                   pl.BlockSpec((1, HB, C, C), lambda g, c: (c, g, 0, 0))],
        out_shape=[jax.ShapeDtypeStruct((NH, Lp, K), F32), jax.ShapeDtypeStruct((nc, NH, K, K), F32),
                   jax.ShapeDtypeStruct((nc, NH, C, C), F32)],
        scratch_shapes=[pltpu.VMEM((HB, K, K), F32)],
        compiler_params=_params(("parallel", "arbitrary")),
    )(*seqs, *pars)


def _rw_chunk_bwd(seqs, pars, ckpt, tinv, dy):
    NH, Lp, K = seqs[0].shape
    HB, C = RW_HEADS_PER_STEP, CHUNK
    nc = Lp // C

    def body(*refs):
        seq_refs, par_refs = refs[:7], refs[7:10]
        ck_ref, t_ref, dy_ref = refs[10:13]
        dseq_refs, dpar_refs, dh_ref = refs[13:20], refs[20:23], refs[23]

        @pl.when(pl.program_id(1) == 0)
        def _():
            dh_ref[...] = jnp.zeros_like(dh_ref)
            for o_ref in dpar_refs:
                o_ref[...] = jnp.zeros_like(o_ref)

        vals = [ck_ref[0]] + [r[...] for r in seq_refs] + [r[...] for r in par_refs]
        t_saved = t_ref[0]
        fn = lambda *v: _rw_chunk_fn(*v, t_saved=t_saved)[:2]
        grads = jax.vjp(fn, *vals)[1]((dy_ref[...], dh_ref[...]))
        dh_ref[...] = grads[0]
        for o_ref, g in zip(dseq_refs, grads[1:8]):
            o_ref[...] = g
        for o_ref, g in zip(dpar_refs, grads[8:11]):
            o_ref[...] += g

    seq_spec = pl.BlockSpec((HB, C, K), lambda g, c: (g, nc - 1 - c, 0))
    par_spec = pl.BlockSpec((HB, 1, K), lambda g, c: (g, 0, 0))
    res = pl.pallas_call(
        body, name="rw_chunk_bwd", grid=(NH // HB, nc),
        in_specs=[seq_spec] * 7 + [par_spec] * 3
        + [pl.BlockSpec((1, HB, K, K), lambda g, c: (nc - 1 - c, g, 0, 0)),
           pl.BlockSpec((1, HB, C, C), lambda g, c: (nc - 1 - c, g, 0, 0)), seq_spec],
        out_specs=[seq_spec] * 7 + [par_spec] * 3,
        out_shape=[jax.ShapeDtypeStruct((NH, Lp, K), F32)] * 7 + [jax.ShapeDtypeStruct((NH, 1, K), F32)] * 3,
        scratch_shapes=[pltpu.VMEM((HB, K, K), F32)],
        compiler_params=_params(("parallel", "arbitrary")),
    )(*seqs, *pars, ckpt, tinv, dy)
    return res[:7], res[7:]


def _dn_chunk_fwd(rows, nw):
    Lp = rows[0][0].shape[0]
    C, D, NH = CHUNK, DN_HEAD, DN_HEADS
    nc = Lp // C

    def body(*refs):
        in_refs, nw_ref = refs[:6], refs[6]
        y_ref, ck_ref, t_ref, s_ref = refs[7:]

        @pl.when(pl.program_id(0) == 0)
        def _():
            s_ref[...] = jnp.zeros_like(s_ref)

        for h in range(NH):
            sl = slice(h * D, (h + 1) * D)
            S = s_ref[h]
            ck_ref[0, h] = S
            y, S_new, t_inv = _dn_chunk_fn(S, *[r[:, sl] for r in in_refs], nw_ref[...])
            y_ref[:, sl] = y
            t_ref[0, h] = t_inv
            s_ref[h] = S_new

    return pl.pallas_call(
        body, name="dn_chunk_fwd", grid=(nc,),
        in_specs=[pl.BlockSpec((C, w), lambda c, cb=cb: (c, cb)) for (_, w, cb) in rows]
        + [pl.BlockSpec((1, D), lambda c: (0, 0))],
        out_specs=[pl.BlockSpec((C, NH * D), lambda c: (c, 0)),
                   pl.BlockSpec((1, NH, D, D), lambda c: (c, 0, 0, 0)),
                   pl.BlockSpec((1, NH, C, C), lambda c: (c, 0, 0, 0))],
        out_shape=[jax.ShapeDtypeStruct((Lp, NH * D), F32), jax.ShapeDtypeStruct((nc, NH, D, D), F32),
                   jax.ShapeDtypeStruct((nc, NH, C, C), F32)],
        scratch_shapes=[pltpu.VMEM((NH, D, D), F32)],
        compiler_params=_params(("arbitrary",)),
    )(*[r[0] for r in rows], nw)


def _dn_chunk_bwd(rows, nw, ckpt, tinv, dy):
    Lp = rows[0][0].shape[0]
    C, D, NH = CHUNK, DN_HEAD, DN_HEADS
    nc = Lp // C

    def body(*refs):
        in_refs, nw_ref, ck_ref, t_ref, dy_ref = refs[:6], refs[6], refs[7], refs[8], refs[9]
        d_refs, dnw_ref, ds_ref = refs[10:16], refs[16], refs[17]

        @pl.when(pl.program_id(0) == 0)
        def _():
            ds_ref[...] = jnp.zeros_like(ds_ref)
            dnw_ref[...] = jnp.zeros_like(dnw_ref)

        for h in range(NH):
            sl = slice(h * D, (h + 1) * D)
            vals = [ck_ref[0, h]] + [r[:, sl] for r in in_refs] + [nw_ref[...]]
            t_saved = t_ref[0, h]
            fn = lambda *v, t_saved=t_saved: _dn_chunk_fn(*v, t_saved=t_saved)[:2]
            grads = jax.vjp(fn, *vals)[1]((dy_ref[:, sl], ds_ref[h]))
            ds_ref[h] = grads[0]
            for o_ref, g in zip(d_refs, grads[1:7]):
                o_ref[:, sl] = g
            dnw_ref[...] += grads[7]

    rev = lambda c: nc - 1 - c
    dy_arr, dy_w, dy_cb = dy
    res = pl.pallas_call(
        body, name="dn_chunk_bwd", grid=(nc,),
        in_specs=[pl.BlockSpec((C, w), lambda c, cb=cb: (rev(c), cb)) for (_, w, cb) in rows]
        + [pl.BlockSpec((1, D), lambda c: (0, 0)),
           pl.BlockSpec((1, NH, D, D), lambda c: (rev(c), 0, 0, 0)),
           pl.BlockSpec((1, NH, C, C), lambda c: (rev(c), 0, 0, 0)),
           pl.BlockSpec((C, dy_w), lambda c: (rev(c), dy_cb))],
        out_specs=[pl.BlockSpec((C, NH * D), lambda c: (rev(c), 0))] * 6 + [pl.BlockSpec((1, D), lambda c: (0, 0))],
        out_shape=[jax.ShapeDtypeStruct((Lp, NH * D), F32)] * 6 + [jax.ShapeDtypeStruct((1, D), F32)],
        scratch_shapes=[pltpu.VMEM((NH, D, D), F32)],
        compiler_params=_params(("arbitrary",)),
    )(*[r[0] for r in rows], nw, ckpt, tinv, dy_arr)
    return res[:6], res[6]


def _loss_head(h, yw, tgt, fw, n_real, tm=ROW_TILE):
    Lp, Dm = h.shape
    tm = _pick(Lp, tm)

    def out_fn(z, fw_):
        return z * lax.rsqrt(jnp.mean(z * z, axis=-1, keepdims=True) + NORM_EPS) * fw_

    def body(h_ref, yw_ref, t_ref, fw_ref, loss_ref, dz_ref, dfw_ref):
        i = pl.program_id(0)

        @pl.when(i == 0)
        def _():
            loss_ref[...] = jnp.zeros_like(loss_ref)
            dfw_ref[...] = jnp.zeros_like(dfw_ref)

        row = i * tm + lax.broadcasted_iota(jnp.int32, (tm, 1), 0)
        mask = ((row >= N_META) & (row < n_real)).astype(F32)
        z = h_ref[...] + yw_ref[...]
        o, vjp = jax.vjp(out_fn, z, fw_ref[...])
        err = (o - t_ref[...]) * mask
        row_loss = 0.5 * jnp.mean(jnp.square(err), axis=-1, keepdims=True)
        dz, dfw = vjp(err * (1.0 / Dm))
        loss_ref[...] += jnp.sum(row_loss, axis=0, keepdims=True)
        dz_ref[...] = dz
        dfw_ref[...] += dfw

    row_spec = pl.BlockSpec((tm, Dm), lambda i: (i, 0))
    return pl.pallas_call(
        body, name="loss_head", grid=(Lp // tm,),
        in_specs=[row_spec, row_spec, row_spec, pl.BlockSpec((1, Dm), lambda i: (0, 0))],
        out_specs=[pl.BlockSpec((8, 128), lambda i: (0, 0)), row_spec, pl.BlockSpec((1, Dm), lambda i: (0, 0))],
        out_shape=[jax.ShapeDtypeStruct((8, 128), F32), jax.ShapeDtypeStruct((Lp, Dm), F32),
                   jax.ShapeDtypeStruct((1, Dm), F32)],
        compiler_params=_params(("arbitrary",)),
    )(h, yw, tgt, fw)


def _exchange(name, x, masks, slot_kind, per_dest, n_split=1):
    n = len(masks)
    keep_own = slot_kind is not None
    n_slots = {"chip": 4, "core": 2, "dev": 8, None: n}[slot_kind]
    blk_shape = x.shape[1:] if per_dest else x.shape
    rows = blk_shape[0] // n_split

    def body(x_ref, o_ref, send_sems, recv_sems, local_sems):
        mx, my, mc = lax.axis_index("x"), lax.axis_index("y"), lax.axis_index("c")

        def slot(k, px, py, pc):
            return {"chip": 2 * px + py, "core": pc, "dev": 4 * px + 2 * py + pc, None: k}[slot_kind]

        def peer(m):
            return (mx + m[0]) % 2, (my + m[1]) % 2, (mc + m[2]) % 2

        def part(ref, j):
            return ref.at[pl.ds(j * rows, rows)]

        own_src = x_ref.at[2 * mx + my] if per_dest else x_ref
        local = []
        if keep_own:
            own_dst = o_ref.at[slot(0, mx, my, mc)]
            local = [pltpu.make_async_copy(part(own_src, j), part(own_dst, j), local_sems.at[j])
                     for j in range(n_split)]
        for cp in local:
            cp.start()
        sends = []
        for k, m in enumerate(masks):
            px, py, pc = peer(m)
            src = x_ref.at[2 * px + py] if per_dest else x_ref
            dst = o_ref.at[slot(k, mx, my, mc)]
            for j in range(n_split):
                sends.append(pltpu.make_async_remote_copy(
                    src_ref=part(src, j), dst_ref=part(dst, j), send_sem=send_sems.at[k * n_split + j],
                    recv_sem=recv_sems.at[k * n_split + j], device_id=(px, py, pc), device_id_type=MESH))
        for cp in sends:
            cp.start()
        for k, m in enumerate(masks):
            px, py, pc = peer(m)
            landed = o_ref.at[slot(k, px, py, pc)]
            for j in range(n_split):
                pltpu.make_async_remote_copy(
                    src_ref=part(own_src, j), dst_ref=part(landed, j), send_sem=send_sems.at[k * n_split + j],
                    recv_sem=recv_sems.at[k * n_split + j], device_id=(px, py, pc), device_id_type=MESH).wait_recv()
        for cp in sends:
            cp.wait_send()
        for cp in local:
            cp.wait()

    return pl.pallas_call(
        body, name=name,
        in_specs=[pl.BlockSpec(memory_space=pl.ANY)], out_specs=pl.BlockSpec(memory_space=pl.ANY),
        out_shape=jax.ShapeDtypeStruct((n_slots,) + tuple(blk_shape), x.dtype),
        scratch_shapes=[pltpu.SemaphoreType.DMA((n * n_split,)), pltpu.SemaphoreType.DMA((n * n_split,)),
                        pltpu.SemaphoreType.DMA((n_split,))],
        compiler_params=pltpu.CompilerParams(has_side_effects=True),
    )(x)


CHIP_MASKS = [(1, 0, 0), (0, 1, 0), (1, 1, 0)]
CORE_MASKS = [(0, 0, 1)]
ALL_MASKS = [(dx, dy, dc) for dx in (0, 1) for dy in (0, 1) for dc in (0, 1) if (dx, dy, dc) != (0, 0, 0)]


def _gather_chips(name, x):
    return _exchange(name, x, CHIP_MASKS, "chip", False)


def _gather_shards(name, shard, n_split):
    half = shard.shape[0] // 2
    mine = lax.dynamic_slice_in_dim(shard, lax.axis_index("c") * half, half, axis=0)
    by_chip = _exchange(name + "_chips", mine, CHIP_MASKS, "chip", False, n_split)
    return _exchange(name + "_cores", by_chip, CORE_MASKS, "core", False, N_CHIPS)


def _sum_slots(name, x, tr=128):
    S, R, N = x.shape
    tr = _pick(R, tr, 16)

    def body(x_ref, o_ref):
        acc = x_ref[0].astype(F32)
        for s in range(1, S):
            acc = acc + x_ref[s].astype(F32)
        o_ref[...] = acc

    return pl.pallas_call(
        body, name=name, grid=(R // tr,),
        in_specs=[pl.BlockSpec((S, tr, N), lambda i: (0, i, 0))], out_specs=pl.BlockSpec((tr, N), lambda i: (i, 0)),
        out_shape=jax.ShapeDtypeStruct((R, N), F32), compiler_params=_params(("parallel",)),
    )(x)


def _add_to_bf16(name, a, b, tr=128):
    S, R, N = a.shape
    tr = _pick(R, tr, 16)

    def body(a_ref, b_ref, o_ref):
        o_ref[...] = (a_ref[...] + b_ref[...]).astype(BF16)

    spec = pl.BlockSpec((S, tr, N), lambda i: (0, i, 0))
    return pl.pallas_call(
        body, name=name, grid=(R // tr,), in_specs=[spec, spec], out_specs=spec,
        out_shape=jax.ShapeDtypeStruct((S, R, N), BF16), compiler_params=_params(("parallel",)),
    )(a, b)


def _adamw(name, gparts, w, m, v, tr=128):
    S, R, N = gparts.shape
    tr = _pick(R, tr)
    c1 = 1.0 / (1.0 - ADAM_B1 ** ADAM_STEP)
    c2 = 1.0 / (1.0 - ADAM_B2 ** ADAM_STEP)

    def body(g_ref, w_ref, m_ref, v_ref, go_ref, d_ref, mo_ref, vo_ref):
        g = g_ref[0]
        for s in range(1, S):
            g = g + g_ref[s]
        m_new = ADAM_B1 * m_ref[...] + (1.0 - ADAM_B1) * g
        v_new = ADAM_B2 * v_ref[...] + (1.0 - ADAM_B2) * jnp.square(g)
        go_ref[...] = g
        mo_ref[...] = m_new
        vo_ref[...] = v_new
        d_ref[...] = -ADAM_LR * ((m_new * c1) / (jnp.sqrt(v_new * c2) + ADAM_EPS) + ADAM_WD * w_ref[...])

    spec = pl.BlockSpec((tr, N), lambda i: (i, 0))
    return pl.pallas_call(
        body, name=name, grid=(R // tr,),
        in_specs=[pl.BlockSpec((S, tr, N), lambda i: (0, i, 0)), spec, spec, spec], out_specs=[spec] * 4,
        out_shape=[jax.ShapeDtypeStruct((R, N), F32)] * 4, compiler_params=_params(("parallel",)),
    )(gparts, w, m, v)


def _reduce_to_shard(name, slabs, n_split):
    _, R, N = slabs.shape
    half = R // 2
    c = lax.axis_index("c")
    halves = slabs.reshape(N_CHIPS, 2, half, N)
    mine = lax.dynamic_index_in_dim(halves, c, axis=1, keepdims=False)
    theirs = lax.dynamic_index_in_dim(halves, 1 - c, axis=1, keepdims=False)
    from_sibling = _exchange(name + "_sib", theirs, CORE_MASKS, None, False, N_CHIPS)[0]
    wire = _add_to_bf16(name + "_add", mine, from_sibling)
    got = _exchange(name + "_chips", wire, CHIP_MASKS, "chip", True, n_split)
    part = _sum_slots(name + "_sum", got)
    return _exchange(name + "_cores", part, CORE_MASKS, "core", False, n_split).reshape(1, R, N)


def _pack(pieces, cols, row_mult=8):
    flat = jnp.concatenate([p.reshape(-1) for p in pieces])
    rows = -(-flat.shape[0] // cols)
    rows = -(-rows // row_mult) * row_mult
    return jnp.pad(flat, (0, rows * cols - flat.shape[0])).reshape(rows, cols)


def _unpack(packed, shapes):
    flat = packed.reshape(-1)
    out, off = [], 0
    for shp in shapes:
        n = 1
        for d in shp:
            n *= d
        out.append(flat[off:off + n].reshape(shp))
        off += n
    return out


def _heads_major(x):
    return jnp.transpose(x.reshape(x.shape[0], RW_HEADS, RW_HEAD), (1, 0, 2))


def _heads_minor(x):
    return jnp.transpose(x, (1, 0, 2)).reshape(x.shape[1], RW_WIDTH)


def _to_my_layout(w):
    z = jnp.zeros((w.shape[0], 112), w.dtype)
    return jnp.concatenate([w[:, 0:3072], w[:, 3200:4224], w[:, 4224:7296], w[:, 7312:8336],
                            w[:, 3072:3200], w[:, 7296:7312], z], axis=1)


def _from_my_layout(w):
    return jnp.concatenate([w[:, 0:3072], w[:, 8192:8320], w[:, 3072:4096], w[:, 4096:7168],
                            w[:, 8320:8336], w[:, 7168:8192]], axis=1)


def kernel(x, meta_tokens, norm_w, w_in, rw_shift_mu, rw_w0, rw_w2, rw_a0, rw_a2, rw_k_k, rw_k_a, rw_r_k, rw_gn_w, rw_gn_b, dn_conv_w, dn_A_log, dn_dt_bias, dn_norm_w, w_out, final_norm_w, loss_target, m_meta_tokens, m_norm_w, m_w_in, m_rw_shift_mu, m_rw_w0, m_rw_w2, m_rw_a0, m_rw_a2, m_rw_k_k, m_rw_k_a, m_rw_r_k, m_rw_gn_w, m_rw_gn_b, m_dn_conv_w, m_dn_A_log, m_dn_dt_bias, m_dn_norm_w, m_w_out, m_final_norm_w, v_meta_tokens, v_norm_w, v_w_in, v_rw_shift_mu, v_rw_w0, v_rw_w2, v_rw_a0, v_rw_a2, v_rw_k_k, v_rw_k_a, v_rw_r_k, v_rw_gn_w, v_rw_gn_b, v_dn_conv_w, v_dn_A_log, v_dn_dt_bias, v_dn_norm_w, v_w_out, v_final_norm_w):
    S = x.shape[1]
    L = N_META + S
    Lp = -(-L // CHUNK) * CHUNK

    small_shapes = [(RW_LORA, 256), (RW_LORA, 256), (CONV_W, 768), (N_META, 512)]
    small_mine = _pack([rw_w2[0], rw_a2[0], dn_conv_w[0], meta_tokens], 1024)
    small_all = _gather_chips("gather_small", small_mine)
    per_chip = [_unpack(small_all[s], small_shapes) for s in range(N_CHIPS)]
    w2, a2, conv_w, meta = [jnp.concatenate([per_chip[s][i] for s in range(N_CHIPS)], axis=1) for i in range(4)]
    w_in_all = _gather_shards("gather_w_in", w_in[0].astype(BF16), 8)
    W = _to_my_layout(jnp.transpose(w_in_all, (0, 2, 1, 3)).reshape(D_MODEL, IN_COLS))
    w_out_all = _gather_shards("gather_w_out", w_out[0].astype(BF16), 8)
    Wo = jnp.transpose(w_out_all, (1, 0, 2, 3)).reshape(D_MODEL, D_MODEL)

    tail = [jnp.zeros((Lp - L, D_MODEL), F32)] if Lp > L else []
    h = jnp.concatenate([meta, x[0]] + tail, axis=0)
    tgt = jnp.concatenate([jnp.zeros((N_META, D_MODEL), F32), loss_target[0]] + tail, axis=0)
    (u,) = _rowwise("rms_in", _rms_fn, [_row(h)], [norm_w], [D_MODEL])
    p = _mm("in_proj", u, W, "nn")

    mu = rw_shift_mu
    zpad = jnp.zeros((RW_LORA, RW_WIDTH), F32)
    rw_params = [mu[:, 0:1024], mu[:, 1024:2048], mu[:, 2048:3072], mu[:, 3072:3200], rw_w0,
                 jnp.concatenate([w2, zpad], axis=0), rw_a0, jnp.concatenate([zpad, a2], axis=0), rw_k_k, rw_k_a]
    rw_in = [_row(p, 1024, CB_R), _row(p, 1024, CB_K), _row(p, 1024, CB_V), _row(p, 128, CB_S1)]
    rw_rows = []
    for i, rin in enumerate(rw_in):
        rw_rows += [rin, _row(_shift_sum("rw_shift%d" % i, [(rin, 1)]))]
    rw_seq = _rowwise("rw_prep", _rw_prep_fn, rw_rows, rw_params, [RW_WIDTH] * 6)
    gate = p[:, CB_GATE * 1024:(CB_GATE + 1) * 1024]
    rw_seq_hm = [_heads_major(t) for t in rw_seq] + [_heads_major(gate)]
    rw_pars_hm = [t.reshape(RW_HEADS, 1, RW_HEAD) for t in (rw_r_k, rw_gn_w, rw_gn_b)]
    ya_hm, rw_ck, rw_t = _rw_chunk_fwd(rw_seq_hm, rw_pars_hm)
    ya = _heads_minor(ya_hm)

    conv_rows, conv_w_rows, dn_c = [], [], []
    for i, cb in enumerate((CB_DQ, CB_DK, CB_DV)):
        src = _row(p, 1024, cb)
        rows_i = [src] + [_row(_shift_sum("dn_shift%d_%d" % (i, j), [(src, j)])) for j in (1, 2, 3)]
        conv_rows.append(rows_i)
        conv_w_rows.append([conv_w[j:j + 1, 1024 * i:1024 * (i + 1)] for j in range(CONV_W)])
        (c_i,) = _rowwise("dn_conv%d" % i, _conv_fn, rows_i, conv_w_rows[i], [DN_WIDTH])
        dn_c.append(c_i)
    a_log_b = jnp.repeat(dn_A_log, DN_HEAD, axis=1)
    dt_b = jnp.repeat(dn_dt_bias, DN_HEAD, axis=1)
    beta_b, g_b = _rowwise("dn_gate", _dn_gate_fn, [_row(p, 128, CB_S2)], [a_log_b, dt_b], [DN_WIDTH] * 2)
    dn_rows = [_row(dn_c[0]), _row(dn_c[1]), _row(dn_c[2]), _row(beta_b), _row(g_b), _row(p, 1024, CB_Z)]
    yb, dn_ck, dn_t = _dn_chunk_fwd(dn_rows, dn_norm_w)

    y = jnp.concatenate([ya, yb], axis=1)
    yw = _mm("out_proj", y, Wo, "nn", tn=1024)
    loss_acc, dz, d_fw = _loss_head(h, yw, tgt, final_norm_w.reshape(1, D_MODEL), L)
    loss = lax.psum(loss_acc[0, 0], ("x", "y", "c"))

    dy = _mm("d_out_proj", dz, Wo, "nt", tn=1024, tk=2048)
    d_wo = _mm("d_w_out", y, dz, "tn", tm=1024, tn=1024, tk=MM_ROW_TILE)

    d_rw_hm, d_rw_pars = _rw_chunk_bwd(rw_seq_hm, rw_pars_hm, rw_ck, rw_t, _heads_major(dy[:, :RW_WIDTH]))
    d_rw_seq = [_heads_minor(t) for t in d_rw_hm]
    d_gate = d_rw_seq[6]
    d_prep_rows, d_prep_pars = _rowwise_bwd("rw_prep_bwd", _rw_prep_fn, rw_rows, rw_params,
                                            [_row(t) for t in d_rw_seq[:6]])
    dp_rw = [_shift_sum("rw_shift_bwd%d" % i, [(_row(d_prep_rows[2 * i]), 0), (_row(d_prep_rows[2 * i + 1]), -1)])
             for i in range(4)]

    d_dn, d_dn_nw = _dn_chunk_bwd(dn_rows, dn_norm_w, dn_ck, dn_t, _row(dy, 1024, 1))
    (d_ps2,), (d_a_log_b, d_dt_b) = _rowwise_bwd("dn_gate_bwd", _dn_gate_fn, [_row(p, 128, CB_S2)], [a_log_b, dt_b],
                                                 [_row(d_dn[3]), _row(d_dn[4])])
    dp_dn, d_conv_parts = [], []
    for i in range(3):
        d_us, d_cw = _rowwise_bwd("dn_conv_bwd%d" % i, _conv_fn, conv_rows[i], conv_w_rows[i], [_row(d_dn[i])])
        dp_dn.append(_shift_sum("dn_shift_bwd%d" % i, [(_row(d_us[j]), -j) for j in range(4)]))
        d_conv_parts.append(jnp.concatenate(d_cw, axis=0))

    dp = jnp.concatenate(dp_rw[:3] + [d_gate] + dp_dn + [d_dn[5], dp_rw[3], d_ps2], axis=1)
    du = _mm("d_in_proj", dp, W, "nt", tn=1024, tk=1408)
    d_W = _mm("d_w_in", u, dp, "tn", tm=1024, tn=1408, tk=MM_ROW_TILE)
    (dh1,), (d_norm_w,) = _rowwise_bwd("rms_in_bwd", _rms_fn, [_row(h)], [norm_w], [_row(du)])
    (dh,) = _rowwise("dh_sum", lambda a_, b_: (a_ + b_,), [_row(dz), _row(dh1)], [], [D_MODEL])
    grad_x = dh[N_META:L][None]

    d_mu = jnp.concatenate(d_prep_pars[0:4], axis=1)
    d_w2, d_a2 = d_prep_pars[5][:RW_LORA], d_prep_pars[7][RW_LORA:]
    d_conv = jnp.concatenate(d_conv_parts, axis=1)
    d_meta = dh[:N_META]
    head_sum = lambda t: jnp.sum(t.reshape(1, DN_HEADS, DN_HEAD), axis=-1)
    rep_names = ["norm_w", "rw_shift_mu", "rw_w0", "rw_a0", "rw_k_k", "rw_k_a", "rw_r_k", "rw_gn_w", "rw_gn_b",
                 "dn_A_log", "dn_dt_bias", "dn_norm_w", "final_norm_w"]
    rep_g = [d_norm_w, d_mu, d_prep_pars[4], d_prep_pars[6], d_prep_pars[8], d_prep_pars[9],
             d_rw_pars[0].reshape(1, RW_WIDTH), d_rw_pars[1].reshape(1, RW_WIDTH), d_rw_pars[2].reshape(1, RW_WIDTH),
             head_sum(d_a_log_b), head_sum(d_dt_b), d_dn_nw, d_fw.reshape(D_MODEL)]
    rep_w = [norm_w, rw_shift_mu, rw_w0, rw_a0, rw_k_k, rw_k_a, rw_r_k, rw_gn_w, rw_gn_b, dn_A_log, dn_dt_bias,
             dn_norm_w, final_norm_w]
    rep_m = [m_norm_w, m_rw_shift_mu, m_rw_w0, m_rw_a0, m_rw_k_k, m_rw_k_a, m_rw_r_k, m_rw_gn_w, m_rw_gn_b,
             m_dn_A_log, m_dn_dt_bias, m_dn_norm_w, m_final_norm_w]
    rep_v = [v_norm_w, v_rw_shift_mu, v_rw_w0, v_rw_a0, v_rw_k_k, v_rw_k_a, v_rw_r_k, v_rw_gn_w, v_rw_gn_b,
             v_dn_A_log, v_dn_dt_bias, v_dn_norm_w, v_final_norm_w]
    rep_shapes = [t.shape for t in rep_w]
    rep_all = _exchange("gather_rep_grads", _pack(rep_g, 128), ALL_MASKS, "dev", False)
    rep_out = _adamw("adam_rep", rep_all, _pack(rep_w, 128), _pack(rep_m, 128), _pack(rep_v, 128))
    rep_out = [dict(zip(rep_names, _unpack(t, rep_shapes))) for t in rep_out]

    sm_slabs = jnp.stack([_pack([d_w2[:, 256 * s:256 * (s + 1)], d_a2[:, 256 * s:256 * (s + 1)],
                                 d_conv[:, 768 * s:768 * (s + 1)], d_meta[:, 512 * s:512 * (s + 1)]], 1024, 64)
                          for s in range(N_CHIPS)])
    sm_parts = _reduce_to_shard("rs_small", sm_slabs, 1)
    sm_w = [rw_w2[0], rw_a2[0], dn_conv_w[0], meta_tokens]
    sm_m = [m_rw_w2[0], m_rw_a2[0], m_dn_conv_w[0], m_meta_tokens]
    sm_v = [v_rw_w2[0], v_rw_a2[0], v_dn_conv_w[0], v_meta_tokens]
    sm_out = _adamw("adam_small", sm_parts, _pack(sm_w, 1024, 64), _pack(sm_m, 1024, 64), _pack(sm_v, 1024, 64))
    sm_names = ["rw_w2", "rw_a2", "dn_conv_w", "meta_tokens"]
    sm_full_shapes = [(1, RW_LORA, 256), (1, RW_LORA, 256), (1, CONV_W, 768), (N_META, 512)]
    sm_out = [dict(zip(sm_names, [t.reshape(shp) for t, shp in zip(_unpack(o, small_shapes), sm_full_shapes)]))
              for o in sm_out]

    wo_parts = _reduce_to_shard("rs_w_out", d_wo.reshape(N_CHIPS, D_MODEL // N_CHIPS, D_MODEL), 8)
    wo_out = _adamw("adam_w_out", wo_parts, w_out[0], m_w_out[0], v_w_out[0])
    d_w_in_slabs = jnp.transpose(_from_my_layout(d_W).reshape(D_MODEL, N_CHIPS, SHARD_COLS), (1, 0, 2))
    wi_parts = _reduce_to_shard("rs_w_in", d_w_in_slabs, 8)
    wi_out = _adamw("adam_w_in", wi_parts, w_in[0], m_w_in[0], v_w_in[0])

    order = ["meta_tokens", "norm_w", "w_in", "rw_shift_mu", "rw_w0", "rw_w2", "rw_a0", "rw_a2", "rw_k_k", "rw_k_a",
             "rw_r_k", "rw_gn_w", "rw_gn_b", "dn_conv_w", "dn_A_log", "dn_dt_bias", "dn_norm_w", "w_out",
             "final_norm_w"]
    outs = [loss, grad_x]
    for kind in range(4):
        table = dict(rep_out[kind])
        table.update(sm_out[kind])
        table["w_in"] = wi_out[kind][None]
        table["w_out"] = wo_out[kind][None]
        outs += [table[n] for n in order]
    return tuple(outs)
```

```python
import functools

import jax
import jax.numpy as jnp
from jax import lax
from jax.experimental import pallas as pl
from jax.experimental.pallas import tpu as pltpu

F32 = jnp.float32
BF16 = jnp.bfloat16
HIGH = lax.Precision.HIGH
HIGHEST = lax.Precision.HIGHEST
MESH = pl.DeviceIdType.MESH

D_MODEL = 2048
N_META = 16
RW_WIDTH = 1024
RW_HEAD = 64
RW_HEADS = 16
RW_LORA = 64
RW_GN_EPS = 64e-5
DN_WIDTH = 1024
DN_HEAD = 128
DN_HEADS = 8
CONV_W = 4
CHUNK = 64
NORM_EPS = 1e-6
IN_COLS = 8336
N_CHIPS = 4
SHARD_COLS = IN_COLS // N_CHIPS

NP_COLS = 8 * 1024 + 256
CB_R, CB_K, CB_V, CB_GATE, CB_DQ, CB_DK, CB_DV, CB_Z = range(8)
CB_S1 = 8192 // 128
CB_S2 = CB_S1 + 1

ADAM_LR = 0.001
ADAM_B1 = 0.9
ADAM_B2 = 0.999
ADAM_EPS = 1e-08
ADAM_WD = 0.01
ADAM_STEP = 10

VMEM_LIMIT_BYTES = 56 * 1024 * 1024
RW_HEADS_PER_STEP = 8
ROW_TILE = 104
MM_ROW_TILE = 520


def _params(sem=None):
    return pltpu.CompilerParams(dimension_semantics=sem, vmem_limit_bytes=VMEM_LIMIT_BYTES)


def _pick(n, target, mult=8):
    best = None
    for d in range(mult, min(n, target) + 1, mult):
        if n % d == 0:
            best = d
    return n if best is None else best


def _mm(name, a, b, mode, tm=MM_ROW_TILE, tn=1408, tk=2048):
    if mode == "nn":
        (M, K), (_, N) = a.shape, b.shape
    elif mode == "nt":
        (M, K), (N, _) = a.shape, b.shape
    else:
        (K, M), (_, N) = a.shape, b.shape
    tm = _pick(M, tm, 128 if mode == "tn" else 8)
    tn = _pick(N, tn, 128)
    tk = _pick(K, tk, 8 if mode == "tn" else 128)
    if mode == "nn":
        a_spec = pl.BlockSpec((tm, tk), lambda i, j, k: (i, k))
        b_spec = pl.BlockSpec((tk, tn), lambda i, j, k: (k, j))
        dims = (((1,), (0,)), ((), ()))
    elif mode == "nt":
        a_spec = pl.BlockSpec((tm, tk), lambda i, j, k: (i, k))
        b_spec = pl.BlockSpec((tn, tk), lambda i, j, k: (j, k))
        dims = (((1,), (1,)), ((), ()))
    else:
        a_spec = pl.BlockSpec((tk, tm), lambda i, j, k: (k, i))
        b_spec = pl.BlockSpec((tk, tn), lambda i, j, k: (k, j))
        dims = (((0,), (0,)), ((), ()))

    def body(a_ref, b_ref, o_ref):
        @pl.when(pl.program_id(2) == 0)
        def _():
            o_ref[...] = jnp.zeros_like(o_ref)

        o_ref[...] += lax.dot_general(a_ref[...].astype(BF16), b_ref[...].astype(BF16), dims,
                                      preferred_element_type=F32)

    return pl.pallas_call(
        body, name=name, grid=(M // tm, N // tn, K // tk),
        in_specs=[a_spec, b_spec], out_specs=pl.BlockSpec((tm, tn), lambda i, j, k: (i, j)),
        out_shape=jax.ShapeDtypeStruct((M, N), F32),
        compiler_params=_params(("parallel", "parallel", "arbitrary")),
    )(a, b)


def _row(arr, width=None, cb=0):
    return (arr, arr.shape[1] if width is None else width, cb)


def _rowwise(name, fn, rows, params, out_widths, tm=ROW_TILE):
    R = rows[0][0].shape[0]
    tm = _pick(R, tm)
    n_r, n_p = len(rows), len(params)

    def body(*refs):
        vals = [r[...] for r in refs[:n_r + n_p]]
        for o_ref, val in zip(refs[n_r + n_p:], fn(*vals)):
            o_ref[...] = val

    in_specs = [pl.BlockSpec((tm, w), lambda i, cb=cb: (i, cb)) for (_, w, cb) in rows]
    in_specs += [pl.BlockSpec(p.shape, lambda i: (0, 0)) for p in params]
    return pl.pallas_call(
        body, name=name, grid=(R // tm,), in_specs=in_specs,
        out_specs=[pl.BlockSpec((tm, w), lambda i: (i, 0)) for w in out_widths],
        out_shape=[jax.ShapeDtypeStruct((R, w), F32) for w in out_widths],
        compiler_params=_params(("parallel",)),
    )(*[r[0] for r in rows], *params)


def _rowwise_bwd(name, fn, rows, params, douts, tm=ROW_TILE):
    R = rows[0][0].shape[0]
    tm = _pick(R, tm)
    n_r, n_p, n_d = len(rows), len(params), len(douts)

    def body(*refs):
        vals = [r[...] for r in refs[:n_r + n_p]]
        cts = tuple(r[...] for r in refs[n_r + n_p:n_r + n_p + n_d])
        grads = jax.vjp(fn, *vals)[1](cts)
        outs = refs[n_r + n_p + n_d:]
        for o_ref, g in zip(outs[:n_r], grads[:n_r]):
            o_ref[...] = g

        @pl.when(pl.program_id(0) == 0)
        def _():
            for o_ref in outs[n_r:]:
                o_ref[...] = jnp.zeros_like(o_ref)

        for o_ref, g in zip(outs[n_r:], grads[n_r:]):
            o_ref[...] += g

    in_specs = [pl.BlockSpec((tm, w), lambda i, cb=cb: (i, cb)) for (_, w, cb) in rows]
    in_specs += [pl.BlockSpec(p.shape, lambda i: (0, 0)) for p in params]
    in_specs += [pl.BlockSpec((tm, w), lambda i, cb=cb: (i, cb)) for (_, w, cb) in douts]
    out_specs = [pl.BlockSpec((tm, w), lambda i: (i, 0)) for (_, w, _) in rows]
    out_specs += [pl.BlockSpec(p.shape, lambda i: (0, 0)) for p in params]
    out_shape = [jax.ShapeDtypeStruct((R, w), F32) for (_, w, _) in rows]
    out_shape += [jax.ShapeDtypeStruct(p.shape, F32) for p in params]
    res = pl.pallas_call(
        body, name=name, grid=(R // tm,), in_specs=in_specs, out_specs=out_specs, out_shape=out_shape,
        compiler_params=_params(("arbitrary",)),
    )(*[r[0] for r in rows], *params, *[d[0] for d in douts])
    return res[:n_r], res[n_r:]


def _shift_sum(name, terms, tm=ROW_TILE):
    R = terms[0][0][0].shape[0]
    w = terms[0][0][1]
    tm = _pick(R, tm)
    nt, nb8 = R // tm, tm // 8
    shifts = [j for (_, j) in terms]

    def body(*refs):
        i = pl.program_id(0)
        acc = None
        for k, j in enumerate(shifts):
            x = refs[2 * k][...]
            if j > 0:
                halo = jnp.where(i == 0, 0.0, refs[2 * k + 1][...])
                x = pltpu.roll(jnp.concatenate([halo, x], axis=0), j, 0)[8:, :]
            elif j < 0:
                halo = jnp.where(i == nt - 1, 0.0, refs[2 * k + 1][...])
                x = pltpu.roll(jnp.concatenate([x, halo], axis=0), tm + 8 + j, 0)[:tm, :]
            acc = x if acc is None else acc + x
        refs[-1][...] = acc

    in_specs, args = [], []
    for (arr, _, cb), j in terms:
        in_specs.append(pl.BlockSpec((tm, w), lambda i, cb=cb: (i, cb)))
        if j > 0:
            in_specs.append(pl.BlockSpec((8, w), lambda i, cb=cb: (jnp.maximum(i * nb8 - 1, 0), cb)))
        else:
            in_specs.append(pl.BlockSpec((8, w), lambda i, cb=cb: (jnp.minimum((i + 1) * nb8, R // 8 - 1), cb)))
        args += [arr, arr]
    return pl.pallas_call(
        body, name=name, grid=(nt,), in_specs=in_specs, out_specs=pl.BlockSpec((tm, w), lambda i: (i, 0)),
        out_shape=jax.ShapeDtypeStruct((R, w), F32), compiler_params=_params(("parallel",)),
    )(*args)


def _softplus(x):
    return jnp.maximum(x, 0.0) + jnp.log(1.0 + jnp.exp(-jnp.abs(x)))


def _silu(x):
    return x * jax.nn.sigmoid(x)


def _rms_fn(h, w):
    return (h * lax.rsqrt(jnp.mean(h * h, axis=-1, keepdims=True) + NORM_EPS) * w,)


def _rw_prep_fn(pr, pr1, pk, pk1, pv, pv1, ps, ps1, mu_r, mu_k, mu_v, mu_s, w0, w2p, a0, a2p, k_k, k_a):
    r = pr + (pr1 - pr) * mu_r
    k = pk + (pk1 - pk) * mu_k
    v = pv + (pv1 - pv) * mu_v
    s = ps + (ps1 - ps) * mu_s
    w_log = -_softplus(-(w0 + jnp.dot(jnp.tanh(s), w2p, precision=HIGH, preferred_element_type=F32))) - 0.5
    log_decay = -jnp.exp(w_log)
    a = jax.nn.sigmoid(a0 + jnp.dot(s, a2p, precision=HIGH, preferred_element_type=F32))
    return r, log_decay, k * (1.0 + (a - 1.0) * k_a), v, k * k_k, a


def _conv_fn(u0, u1, u2, u3, w0, w1, w2, w3):
    return (_silu(u0 * w3 + u1 * w2 + u2 * w1 + u3 * w0),)


def _dn_gate_fn(ps2, a_log_b, dt_b):
    lane = lax.broadcasted_iota(jnp.int32, (DN_HEAD, DN_WIDTH), 0)
    head = jnp.right_shift(lax.broadcasted_iota(jnp.int32, (DN_HEAD, DN_WIDTH), 1), 7)
    sel_b = (lane == head).astype(F32)
    sel_a = (lane == head + DN_HEADS).astype(F32)
    b = jnp.dot(ps2, sel_b, precision=HIGHEST, preferred_element_type=F32)
    al = jnp.dot(ps2, sel_a, precision=HIGHEST, preferred_element_type=F32)
    return jax.nn.sigmoid(b), -jnp.exp(a_log_b) * _softplus(al + dt_b)


def _tril_masks(c):
    t = lax.broadcasted_iota(jnp.int32, (c, c), 0)
    s = lax.broadcasted_iota(jnp.int32, (c, c), 1)
    return s <= t, s < t


def _dot(x, y, cx, cy, prec=HIGH):
    nb = x.ndim - 2
    batch = tuple(range(nb))
    return lax.dot_general(x, y, (((cx + nb,), (cy + nb,)), (batch, batch)), precision=prec,
                           preferred_element_type=F32)


def _tri_inv(a):
    c = a.shape[-1]
    eye = (lax.broadcasted_iota(jnp.int32, (c, c), 0) == lax.broadcasted_iota(jnp.int32, (c, c), 1)).astype(F32)
    x = eye - a
    p = a
    n = 2
    while n < c:
        p = _dot(p, p, 1, 0)
        x = x + _dot(x, p, 1, 0)
        n *= 2
    return x


@jax.custom_vjp
def _tri_inv_saved(a, t):
    return t


def _tri_inv_saved_fwd(a, t):
    return t, t


def _tri_inv_saved_bwd(t, dt):
    return -_dot(_dot(t, dt, 0, 0), t, 1, 1), jnp.zeros_like(t)


_tri_inv_saved.defvjp(_tri_inv_saved_fwd, _tri_inv_saved_bwd)


def _rw_chunk_fn(S, r, lw, k2, v, kkp, a, gate, rk, gnw, gnb, t_saved=None):
    B, C, _ = r.shape
    bmm = lambda x, y: _dot(x, y, 1, 0)
    bmm_nt = lambda x, y: _dot(x, y, 1, 1)
    bmm_tn = lambda x, y: _dot(x, y, 0, 0)
    incl, strict = _tril_masks(C)
    tril = jnp.broadcast_to(incl.astype(F32), (B, C, C))
    kk = kkp * lax.rsqrt(jnp.sum(kkp * kkp, axis=-1, keepdims=True) + 1e-6)
    b = kk * a
    g_incl = _dot(tril, lw, 1, 0, HIGHEST)
    g_excl = g_incl - lw
    inv = jnp.exp(-g_incl)
    alpha, beta, kappa, rho = kk * jnp.exp(g_excl), b * inv, k2 * inv, r * jnp.exp(g_incl)
    a_ab = jnp.where(strict, bmm_nt(alpha, beta), 0.0)
    a_ak = jnp.where(strict, bmm_nt(alpha, kappa), 0.0)
    t_inv = _tri_inv(a_ab) if t_saved is None else _tri_inv_saved(a_ab, t_saved)
    u = bmm(t_inv, -bmm_nt(alpha, S) - bmm(a_ak, v))
    y = (bmm_nt(rho, S) + bmm(jnp.where(incl, bmm_nt(rho, beta), 0.0), u)
         + bmm(jnp.where(incl, bmm_nt(rho, kappa), 0.0), v))
    g_tot = jnp.sum(lw, axis=-2, keepdims=True)
    S_new = jnp.exp(g_tot) * (S + bmm_tn(u, beta) + bmm_tn(v, kappa))
    mean = jnp.mean(y, axis=-1, keepdims=True)
    var = jnp.mean(jnp.square(y - mean), axis=-1, keepdims=True)
    yn = (y - mean) * lax.rsqrt(var + RW_GN_EPS) * gnw + gnb
    bonus = jnp.sum(r * k2 * rk, axis=-1, keepdims=True) * v
    return (yn + bonus) * _silu(gate), S_new, t_inv


def _dn_chunk_fn(S, qc, kc, vc, bb, gb, z, nw, t_saved=None):
    B, C, D = qc.shape
    mm = lambda x, y: _dot(x, y, 1, 0)
    nt = lambda x, y, p=HIGH: _dot(x, y, 1, 1, p)
    tn = lambda x, y: _dot(x, y, 0, 0)
    incl, strict = _tril_masks(C)
    q = qc * lax.rsqrt(jnp.sum(qc * qc, axis=-1, keepdims=True) + 1e-6) * (D ** -0.5)
    k = kc * lax.rsqrt(jnp.sum(kc * kc, axis=-1, keepdims=True) + 1e-6)
    kb, vb = k * bb, vc * bb
    G = _dot(jnp.broadcast_to(incl.astype(F32), (B, C, C)), gb, 1, 0, HIGHEST)
    lane = lax.broadcasted_iota(jnp.int32, (C, D), 1)
    e0, e1 = (lane == 0).astype(F32), (lane == 1).astype(F32)
    diff = nt(G * e0 + e1, e0 - G * e1, HIGHEST)
    dmask = jnp.where(incl, jnp.exp(jnp.where(incl, diff, 0.0)), 0.0)
    M = jnp.where(strict, nt(kb, k) * dmask, 0.0)
    T = _tri_inv(M) if t_saved is None else _tri_inv_saved(M, t_saved)
    eG = jnp.exp(G)
    u = mm(T, vb)
    w = mm(T, kb * eG)
    attn = jnp.where(incl, nt(q, k) * dmask, 0.0)
    v_new = u - mm(w, S)
    o = mm(q * eG, S) + mm(attn, v_new)
    g_last = jnp.sum(gb, axis=-2, keepdims=True)
    S_new = S * jnp.exp(jnp.broadcast_to(g_last, S.shape)) + tn(k * jnp.exp(g_last - G), v_new)
    on = o * lax.rsqrt(jnp.mean(o * o, axis=-1, keepdims=True) + NORM_EPS) * nw
    return on * _silu(z), S_new, T


def _rw_chunk_fwd(seqs, pars):
    NH, Lp, K = seqs[0].shape
    HB, C = RW_HEADS_PER_STEP, CHUNK
    nc = Lp // C

    def body(*refs):
        seq_refs, par_refs = refs[:7], refs[7:10]
        y_ref, ck_ref, t_ref, h_ref = refs[10:]

        @pl.when(pl.program_id(1) == 0)
        def _():
            h_ref[...] = jnp.zeros_like(h_ref)

        H = h_ref[...]
        ck_ref[0] = H
        y, H_new, t_inv = _rw_chunk_fn(H, *[r[...] for r in seq_refs], *[r[...] for r in par_refs])
        y_ref[...] = y
        t_ref[0] = t_inv
        h_ref[...] = H_new

    seq_spec = pl.BlockSpec((HB, C, K), lambda g, c: (g, c, 0))
    par_spec = pl.BlockSpec((HB, 1, K), lambda g, c: (g, 0, 0))
    return pl.pallas_call(
        body, name="rw_chunk_fwd", grid=(NH // HB, nc),
        in_specs=[seq_spec] * 7 + [par_spec] * 3,
        out_specs=[seq_spec, pl.BlockSpec((1, HB, K, K), lambda g, c: (c, g, 0, 0)),
                   pl.BlockSpec((1, HB, C, C), lambda g, c: (c, g, 0, 0))],
        out_shape=[jax.ShapeDtypeStruct((NH, Lp, K), F32), jax.ShapeDtypeStruct((nc, NH, K, K), F32),
                   jax.ShapeDtypeStruct((nc, NH, C, C), F32)],
        scratch_shapes=[pltpu.VMEM((HB, K, K), F32)],
        compiler_params=_params(("parallel", "arbitrary")),
    )(*seqs, *pars)


def _rw_chunk_bwd(seqs, pars, ckpt, tinv, dy):
    NH, Lp, K = seqs[0].shape
    HB, C = RW_HEADS_PER_STEP, CHUNK
    nc = Lp // C

    def body(*refs):
        seq_refs, par_refs = refs[:7], refs[7:10]
        ck_ref, t_ref, dy_ref = refs[10:13]
        dseq_refs, dpar_refs, dh_ref = refs[13:20], refs[20:23], refs[23]

        @pl.when(pl.program_id(1) == 0)
        def _():
            dh_ref[...] = jnp.zeros_like(dh_ref)
            for o_ref in dpar_refs:
                o_ref[...] = jnp.zeros_like(o_ref)

        vals = [ck_ref[0]] + [r[...] for r in seq_refs] + [r[...] for r in par_refs]
        t_saved = t_ref[0]
        fn = lambda *v: _rw_chunk_fn(*v, t_saved=t_saved)[:2]
        grads = jax.vjp(fn, *vals)[1]((dy_ref[...], dh_ref[...]))
        dh_ref[...] = grads[0]
        for o_ref, g in zip(dseq_refs, grads[1:8]):
            o_ref[...] = g
        for o_ref, g in zip(dpar_refs, grads[8:11]):
            o_ref[...] += g

    seq_spec = pl.BlockSpec((HB, C, K), lambda g, c: (g, nc - 1 - c, 0))
    par_spec = pl.BlockSpec((HB, 1, K), lambda g, c: (g, 0, 0))
    res = pl.pallas_call(
        body, name="rw_chunk_bwd", grid=(NH // HB, nc),
        in_specs=[seq_spec] * 7 + [par_spec] * 3
        + [pl.BlockSpec((1, HB, K, K), lambda g, c: (nc - 1 - c, g, 0, 0)),
           pl.BlockSpec((1, HB, C, C), lambda g, c: (nc - 1 - c, g, 0, 0)), seq_spec],
        out_specs=[seq_spec] * 7 + [par_spec] * 3,
        out_shape=[jax.ShapeDtypeStruct((NH, Lp, K), F32)] * 7 + [jax.ShapeDtypeStruct((NH, 1, K), F32)] * 3,
        scratch_shapes=[pltpu.VMEM((HB, K, K), F32)],
        compiler_params=_params(("parallel", "arbitrary")),
    )(*seqs, *pars, ckpt, tinv, dy)
    return res[:7], res[7:]


def _dn_chunk_fwd(rows, nw):
    Lp = rows[0][0].shape[0]
    C, D, NH = CHUNK, DN_HEAD, DN_HEADS
    nc = Lp // C

    def body(*refs):
        in_refs, nw_ref = refs[:6], refs[6]
        y_ref, ck_ref, t_ref, s_ref = refs[7:]

        @pl.when(pl.program_id(0) == 0)
        def _():
            s_ref[...] = jnp.zeros_like(s_ref)

        S = s_ref[...]
        ck_ref[0] = S
        ins = [jnp.stack([r[:, h * D:(h + 1) * D] for h in range(NH)]) for r in in_refs]
        y, S_new, t_inv = _dn_chunk_fn(S, *ins, nw_ref[...])
        for h in range(NH):
            y_ref[:, h * D:(h + 1) * D] = y[h]
        t_ref[0] = t_inv
        s_ref[...] = S_new

    return pl.pallas_call(
        body, name="dn_chunk_fwd", grid=(nc,),
        in_specs=[pl.BlockSpec((C, w), lambda c, cb=cb: (c, cb)) for (_, w, cb) in rows]
        + [pl.BlockSpec((1, D), lambda c: (0, 0))],
        out_specs=[pl.BlockSpec((C, NH * D), lambda c: (c, 0)),
                   pl.BlockSpec((1, NH, D, D), lambda c: (c, 0, 0, 0)),
                   pl.BlockSpec((1, NH, C, C), lambda c: (c, 0, 0, 0))],
        out_shape=[jax.ShapeDtypeStruct((Lp, NH * D), F32), jax.ShapeDtypeStruct((nc, NH, D, D), F32),
                   jax.ShapeDtypeStruct((nc, NH, C, C), F32)],
        scratch_shapes=[pltpu.VMEM((NH, D, D), F32)],
        compiler_params=_params(("arbitrary",)),
    )(*[r[0] for r in rows], nw)


def _dn_chunk_bwd(rows, nw, ckpt, tinv, dy):
    Lp = rows[0][0].shape[0]
    C, D, NH = CHUNK, DN_HEAD, DN_HEADS
    nc = Lp // C

    def body(*refs):
        in_refs, nw_ref, ck_ref, t_ref, dy_ref = refs[:6], refs[6], refs[7], refs[8], refs[9]
        d_refs, dnw_ref, ds_ref = refs[10:16], refs[16], refs[17]

        @pl.when(pl.program_id(0) == 0)
        def _():
            ds_ref[...] = jnp.zeros_like(ds_ref)
            dnw_ref[...] = jnp.zeros_like(dnw_ref)

        heads = lambda r: jnp.stack([r[:, h * D:(h + 1) * D] for h in range(NH)])
        vals = [ck_ref[0]] + [heads(r) for r in in_refs] + [nw_ref[...]]
        t_saved = t_ref[0]
        fn = lambda *v: _dn_chunk_fn(*v, t_saved=t_saved)[:2]
        grads = jax.vjp(fn, *vals)[1]((heads(dy_ref), ds_ref[...]))
        ds_ref[...] = grads[0]
        for o_ref, g in zip(d_refs, grads[1:7]):
            for h in range(NH):
                o_ref[:, h * D:(h + 1) * D] = g[h]
        dnw_ref[...] += grads[7]

    rev = lambda c: nc - 1 - c
    dy_arr, dy_w, dy_cb = dy
    res = pl.pallas_call(
        body, name="dn_chunk_bwd", grid=(nc,),
        in_specs=[pl.BlockSpec((C, w), lambda c, cb=cb: (rev(c), cb)) for (_, w, cb) in rows]
        + [pl.BlockSpec((1, D), lambda c: (0, 0)),
           pl.BlockSpec((1, NH, D, D), lambda c: (rev(c), 0, 0, 0)),
           pl.BlockSpec((1, NH, C, C), lambda c: (rev(c), 0, 0, 0)),
           pl.BlockSpec((C, dy_w), lambda c: (rev(c), dy_cb))],
        out_specs=[pl.BlockSpec((C, NH * D), lambda c: (rev(c), 0))] * 6 + [pl.BlockSpec((1, D), lambda c: (0, 0))],
        out_shape=[jax.ShapeDtypeStruct((Lp, NH * D), F32)] * 6 + [jax.ShapeDtypeStruct((1, D), F32)],
        scratch_shapes=[pltpu.VMEM((NH, D, D), F32)],
        compiler_params=_params(("arbitrary",)),
    )(*[r[0] for r in rows], nw, ckpt, tinv, dy_arr)
    return res[:6], res[6]


def _loss_head(h, yw, tgt, fw, n_real, tm=ROW_TILE):
    Lp, Dm = h.shape
    tm = _pick(Lp, tm)

    def out_fn(z, fw_):
        return z * lax.rsqrt(jnp.mean(z * z, axis=-1, keepdims=True) + NORM_EPS) * fw_

    def body(h_ref, yw_ref, t_ref, fw_ref, loss_ref, dz_ref, dfw_ref):
        i = pl.program_id(0)

        @pl.when(i == 0)
        def _():
            loss_ref[...] = jnp.zeros_like(loss_ref)
            dfw_ref[...] = jnp.zeros_like(dfw_ref)

        row = i * tm + lax.broadcasted_iota(jnp.int32, (tm, 1), 0)
        mask = ((row >= N_META) & (row < n_real)).astype(F32)
        z = h_ref[...] + yw_ref[...]
        o, vjp = jax.vjp(out_fn, z, fw_ref[...])
        err = (o - t_ref[...]) * mask
        row_loss = 0.5 * jnp.mean(jnp.square(err), axis=-1, keepdims=True)
        dz, dfw = vjp(err * (1.0 / Dm))
        loss_ref[...] += jnp.sum(row_loss, axis=0, keepdims=True)
        dz_ref[...] = dz
        dfw_ref[...] += dfw

    row_spec = pl.BlockSpec((tm, Dm), lambda i: (i, 0))
    return pl.pallas_call(
        body, name="loss_head", grid=(Lp // tm,),
        in_specs=[row_spec, row_spec, row_spec, pl.BlockSpec((1, Dm), lambda i: (0, 0))],
        out_specs=[pl.BlockSpec((8, 128), lambda i: (0, 0)), row_spec, pl.BlockSpec((1, Dm), lambda i: (0, 0))],
        out_shape=[jax.ShapeDtypeStruct((8, 128), F32), jax.ShapeDtypeStruct((Lp, Dm), F32),
                   jax.ShapeDtypeStruct((1, Dm), F32)],
        compiler_params=_params(("arbitrary",)),
    )(h, yw, tgt, fw)


def _exchange(name, x, masks, slot_kind, per_dest, n_split=1, copy_own=True):
    n = len(masks)
    keep_own = slot_kind is not None and copy_own
    n_slots = {"chip": 4, "core": 2, "dev": 8, None: n}[slot_kind]
    blk_shape = x.shape[1:] if per_dest else x.shape
    rows = blk_shape[0] // n_split

    def body(x_ref, o_ref, send_sems, recv_sems, local_sems):
        mx, my, mc = lax.axis_index("x"), lax.axis_index("y"), lax.axis_index("c")

        def slot(k, px, py, pc):
            return {"chip": 2 * px + py, "core": pc, "dev": 4 * px + 2 * py + pc, None: k}[slot_kind]

        def peer(m):
            return (mx + m[0]) % 2, (my + m[1]) % 2, (mc + m[2]) % 2

        def part(ref, j):
            return ref.at[pl.ds(j * rows, rows)]

        own_src = x_ref.at[2 * mx + my] if per_dest else x_ref
        local = []
        if keep_own:
            own_dst = o_ref.at[slot(0, mx, my, mc)]
            local = [pltpu.make_async_copy(part(own_src, j), part(own_dst, j), local_sems.at[j])
                     for j in range(n_split)]
        for cp in local:
            cp.start()
        sends = []
        for k, m in enumerate(masks):
            px, py, pc = peer(m)
            src = x_ref.at[2 * px + py] if per_dest else x_ref
            dst = o_ref.at[slot(k, mx, my, mc)]
            for j in range(n_split):
                sends.append(pltpu.make_async_remote_copy(
                    src_ref=part(src, j), dst_ref=part(dst, j), send_sem=send_sems.at[k * n_split + j],
                    recv_sem=recv_sems.at[k * n_split + j], device_id=(px, py, pc), device_id_type=MESH))
        for cp in sends:
            cp.start()
        for k, m in enumerate(masks):
            px, py, pc = peer(m)
            landed = o_ref.at[slot(k, px, py, pc)]
            for j in range(n_split):
                pltpu.make_async_remote_copy(
                    src_ref=part(own_src, j), dst_ref=part(landed, j), send_sem=send_sems.at[k * n_split + j],
                    recv_sem=recv_sems.at[k * n_split + j], device_id=(px, py, pc), device_id_type=MESH).wait_recv()
        for cp in sends:
            cp.wait_send()
        for cp in local:
            cp.wait()

    return pl.pallas_call(
        body, name=name,
        in_specs=[pl.BlockSpec(memory_space=pl.ANY)], out_specs=pl.BlockSpec(memory_space=pl.ANY),
        out_shape=jax.ShapeDtypeStruct((n_slots,) + tuple(blk_shape), x.dtype),
        scratch_shapes=[pltpu.SemaphoreType.DMA((n * n_split,)), pltpu.SemaphoreType.DMA((n * n_split,)),
                        pltpu.SemaphoreType.DMA((n_split,))],
        compiler_params=pltpu.CompilerParams(has_side_effects=True),
    )(x)


CHIP_MASKS = [(1, 0, 0), (0, 1, 0), (1, 1, 0)]
CORE_MASKS = [(0, 0, 1)]
ALL_MASKS = [(dx, dy, dc) for dx in (0, 1) for dy in (0, 1) for dc in (0, 1) if (dx, dy, dc) != (0, 0, 0)]


def _gather_chips(name, x):
    return _exchange(name, x, CHIP_MASKS, "chip", False)


def _gather_shards(name, shard, n_split):
    half = shard.shape[0] // 2
    c, chip = lax.axis_index("c"), 2 * lax.axis_index("x") + lax.axis_index("y")
    mine = lax.dynamic_slice_in_dim(shard, c * half, half, axis=0)
    by_chip = _exchange(name + "_chips", mine, CHIP_MASKS, "chip", False, n_split, copy_own=False)
    by_chip = lax.dynamic_update_index_in_dim(by_chip, mine, chip, 0)
    both = _exchange(name + "_cores", by_chip, CORE_MASKS, "core", False, N_CHIPS, copy_own=False)
    return lax.dynamic_update_index_in_dim(both, by_chip, c, 0)


def _sum_slots(name, x, tr=128):
    S, R, N = x.shape
    tr = _pick(R, tr, 16)

    def body(x_ref, o_ref):
        acc = x_ref[0].astype(F32)
        for s in range(1, S):
            acc = acc + x_ref[s].astype(F32)
        o_ref[...] = acc

    return pl.pallas_call(
        body, name=name, grid=(R // tr,),
        in_specs=[pl.BlockSpec((S, tr, N), lambda i: (0, i, 0))], out_specs=pl.BlockSpec((tr, N), lambda i: (i, 0)),
        out_shape=jax.ShapeDtypeStruct((R, N), F32), compiler_params=_params(("parallel",)),
    )(x)


def _add_to_bf16(name, a, b, tr=128):
    S, R, N = a.shape
    tr = _pick(R, tr, 16)

    def body(a_ref, b_ref, o_ref):
        o_ref[...] = (a_ref[...] + b_ref[...]).astype(BF16)

    spec = pl.BlockSpec((S, tr, N), lambda i: (0, i, 0))
    return pl.pallas_call(
        body, name=name, grid=(R // tr,), in_specs=[spec, spec], out_specs=spec,
        out_shape=jax.ShapeDtypeStruct((S, R, N), BF16), compiler_params=_params(("parallel",)),
    )(a, b)


def _adamw(name, gparts, w, m, v, tr=128):
    S, R, N = gparts.shape
    tr = _pick(R, tr)
    c1 = 1.0 / (1.0 - ADAM_B1 ** ADAM_STEP)
    c2 = 1.0 / (1.0 - ADAM_B2 ** ADAM_STEP)

    def body(g_ref, w_ref, m_ref, v_ref, go_ref, d_ref, mo_ref, vo_ref):
        g = g_ref[0]
        for s in range(1, S):
            g = g + g_ref[s]
        m_new = ADAM_B1 * m_ref[...] + (1.0 - ADAM_B1) * g
        v_new = ADAM_B2 * v_ref[...] + (1.0 - ADAM_B2) * jnp.square(g)
        go_ref[...] = g
        mo_ref[...] = m_new
        vo_ref[...] = v_new
        d_ref[...] = -ADAM_LR * ((m_new * c1) / (jnp.sqrt(v_new * c2) + ADAM_EPS) + ADAM_WD * w_ref[...])

    spec = pl.BlockSpec((tr, N), lambda i: (i, 0))
    return pl.pallas_call(
        body, name=name, grid=(R // tr,),
        in_specs=[pl.BlockSpec((S, tr, N), lambda i: (0, i, 0)), spec, spec, spec], out_specs=[spec] * 4,
        out_shape=[jax.ShapeDtypeStruct((R, N), F32)] * 4, compiler_params=_params(("parallel",)),
    )(gparts, w, m, v)


def _reduce_to_shard(name, slabs, n_split):
    _, R, N = slabs.shape
    half = R // 2
    c, chip = lax.axis_index("c"), 2 * lax.axis_index("x") + lax.axis_index("y")
    halves = slabs.reshape(N_CHIPS, 2, half, N)
    mine = lax.dynamic_index_in_dim(halves, c, axis=1, keepdims=False)
    theirs = lax.dynamic_index_in_dim(halves, 1 - c, axis=1, keepdims=False)
    from_sibling = _exchange(name + "_sib", theirs, CORE_MASKS, None, False, N_CHIPS)[0]
    wire = _add_to_bf16(name + "_add", mine, from_sibling)
    got = _exchange(name + "_chips", wire, CHIP_MASKS, "chip", True, n_split, copy_own=False)
    got = lax.dynamic_update_index_in_dim(got, lax.dynamic_index_in_dim(wire, chip, 0, keepdims=False), chip, 0)
    part = _sum_slots(name + "_sum", got)
    both = _exchange(name + "_cores", part, CORE_MASKS, "core", False, n_split, copy_own=False)
    return lax.dynamic_update_index_in_dim(both, part, c, 0).reshape(1, R, N)


def _pack(pieces, cols, row_mult=8):
    flat = jnp.concatenate([p.reshape(-1) for p in pieces])
    rows = -(-flat.shape[0] // cols)
    rows = -(-rows // row_mult) * row_mult
    return jnp.pad(flat, (0, rows * cols - flat.shape[0])).reshape(rows, cols)


def _unpack(packed, shapes):
    flat = packed.reshape(-1)
    out, off = [], 0
    for shp in shapes:
        n = 1
        for d in shp:
            n *= d
        out.append(flat[off:off + n].reshape(shp))
        off += n
    return out


def _heads_major(x):
    return jnp.transpose(x.reshape(x.shape[0], RW_HEADS, RW_HEAD), (1, 0, 2))


def _heads_minor(x):
    return jnp.transpose(x, (1, 0, 2)).reshape(x.shape[1], RW_WIDTH)


def _to_my_layout(w):
    z = jnp.zeros((w.shape[0], 112), w.dtype)
    return jnp.concatenate([w[:, 0:3072], w[:, 3200:4224], w[:, 4224:7296], w[:, 7312:8336],
                            w[:, 3072:3200], w[:, 7296:7312], z], axis=1)


def _from_my_layout(w):
    return jnp.concatenate([w[:, 0:3072], w[:, 8192:8320], w[:, 3072:4096], w[:, 4096:7168],
                            w[:, 8320:8336], w[:, 7168:8192]], axis=1)


def kernel(x, meta_tokens, norm_w, w_in, rw_shift_mu, rw_w0, rw_w2, rw_a0, rw_a2, rw_k_k, rw_k_a, rw_r_k, rw_gn_w, rw_gn_b, dn_conv_w, dn_A_log, dn_dt_bias, dn_norm_w, w_out, final_norm_w, loss_target, m_meta_tokens, m_norm_w, m_w_in, m_rw_shift_mu, m_rw_w0, m_rw_w2, m_rw_a0, m_rw_a2, m_rw_k_k, m_rw_k_a, m_rw_r_k, m_rw_gn_w, m_rw_gn_b, m_dn_conv_w, m_dn_A_log, m_dn_dt_bias, m_dn_norm_w, m_w_out, m_final_norm_w, v_meta_tokens, v_norm_w, v_w_in, v_rw_shift_mu, v_rw_w0, v_rw_w2, v_rw_a0, v_rw_a2, v_rw_k_k, v_rw_k_a, v_rw_r_k, v_rw_gn_w, v_rw_gn_b, v_dn_conv_w, v_dn_A_log, v_dn_dt_bias, v_dn_norm_w, v_w_out, v_final_norm_w):
    S = x.shape[1]
    L = N_META + S
    Lp = -(-L // CHUNK) * CHUNK

    small_shapes = [(RW_LORA, 256), (RW_LORA, 256), (CONV_W, 768), (N_META, 512)]
    small_mine = _pack([rw_w2[0], rw_a2[0], dn_conv_w[0], meta_tokens], 1024)
    small_all = _gather_chips("gather_small", small_mine)
    per_chip = [_unpack(small_all[s], small_shapes) for s in range(N_CHIPS)]
    w2, a2, conv_w, meta = [jnp.concatenate([per_chip[s][i] for s in range(N_CHIPS)], axis=1) for i in range(4)]
    w_in_all = _gather_shards("gather_w_in", w_in[0].astype(BF16), 8)
    W = _to_my_layout(jnp.transpose(w_in_all, (0, 2, 1, 3)).reshape(D_MODEL, IN_COLS))
    w_out_all = _gather_shards("gather_w_out", w_out[0].astype(BF16), 8)
    Wo = jnp.transpose(w_out_all, (1, 0, 2, 3)).reshape(D_MODEL, D_MODEL)

    tail = [jnp.zeros((Lp - L, D_MODEL), F32)] if Lp > L else []
    h = jnp.concatenate([meta, x[0]] + tail, axis=0)
    tgt = jnp.concatenate([jnp.zeros((N_META, D_MODEL), F32), loss_target[0]] + tail, axis=0)
    (u,) = _rowwise("rms_in", _rms_fn, [_row(h)], [norm_w], [D_MODEL])
    p = _mm("in_proj", u, W, "nn")

    mu = rw_shift_mu
    zpad = jnp.zeros((RW_LORA, RW_WIDTH), F32)
    rw_params = [mu[:, 0:1024], mu[:, 1024:2048], mu[:, 2048:3072], mu[:, 3072:3200], rw_w0,
                 jnp.concatenate([w2, zpad], axis=0), rw_a0, jnp.concatenate([zpad, a2], axis=0), rw_k_k, rw_k_a]
    rw_in = [_row(p, 1024, CB_R), _row(p, 1024, CB_K), _row(p, 1024, CB_V), _row(p, 128, CB_S1)]
    rw_rows = []
    for i, rin in enumerate(rw_in):
        rw_rows += [rin, _row(_shift_sum("rw_shift%d" % i, [(rin, 1)]))]
    rw_seq = _rowwise("rw_prep", _rw_prep_fn, rw_rows, rw_params, [RW_WIDTH] * 6)
    gate = p[:, CB_GATE * 1024:(CB_GATE + 1) * 1024]
    rw_seq_hm = [_heads_major(t) for t in rw_seq] + [_heads_major(gate)]
    rw_pars_hm = [t.reshape(RW_HEADS, 1, RW_HEAD) for t in (rw_r_k, rw_gn_w, rw_gn_b)]
    ya_hm, rw_ck, rw_t = _rw_chunk_fwd(rw_seq_hm, rw_pars_hm)
    ya = _heads_minor(ya_hm)

    conv_rows, conv_w_rows, dn_c = [], [], []
    for i, cb in enumerate((CB_DQ, CB_DK, CB_DV)):
        src = _row(p, 1024, cb)
        rows_i = [src] + [_row(_shift_sum("dn_shift%d_%d" % (i, j), [(src, j)])) for j in (1, 2, 3)]
        conv_rows.append(rows_i)
        conv_w_rows.append([conv_w[j:j + 1, 1024 * i:1024 * (i + 1)] for j in range(CONV_W)])
        (c_i,) = _rowwise("dn_conv%d" % i, _conv_fn, rows_i, conv_w_rows[i], [DN_WIDTH])
        dn_c.append(c_i)
    a_log_b = jnp.repeat(dn_A_log, DN_HEAD, axis=1)
    dt_b = jnp.repeat(dn_dt_bias, DN_HEAD, axis=1)
    beta_b, g_b = _rowwise("dn_gate", _dn_gate_fn, [_row(p, 128, CB_S2)], [a_log_b, dt_b], [DN_WIDTH] * 2)
    dn_rows = [_row(dn_c[0]), _row(dn_c[1]), _row(dn_c[2]), _row(beta_b), _row(g_b), _row(p, 1024, CB_Z)]
    yb, dn_ck, dn_t = _dn_chunk_fwd(dn_rows, dn_norm_w)

    y = jnp.concatenate([ya, yb], axis=1)
    yw = _mm("out_proj", y, Wo, "nn", tn=1024)
    loss_acc, dz, d_fw = _loss_head(h, yw, tgt, final_norm_w.reshape(1, D_MODEL), L)
    loss = lax.psum(loss_acc[0, 0], ("x", "y", "c"))

    dy = _mm("d_out_proj", dz, Wo, "nt", tn=1024, tk=2048)
    d_wo = _mm("d_w_out", y, dz, "tn", tm=1024, tn=1024, tk=MM_ROW_TILE)

    d_rw_hm, d_rw_pars = _rw_chunk_bwd(rw_seq_hm, rw_pars_hm, rw_ck, rw_t, _heads_major(dy[:, :RW_WIDTH]))
    d_rw_seq = [_heads_minor(t) for t in d_rw_hm]
    d_gate = d_rw_seq[6]
    d_prep_rows, d_prep_pars = _rowwise_bwd("rw_prep_bwd", _rw_prep_fn, rw_rows, rw_params,
                                            [_row(t) for t in d_rw_seq[:6]])
    dp_rw = [_shift_sum("rw_shift_bwd%d" % i, [(_row(d_prep_rows[2 * i]), 0), (_row(d_prep_rows[2 * i + 1]), -1)])
             for i in range(4)]

    d_dn, d_dn_nw = _dn_chunk_bwd(dn_rows, dn_norm_w, dn_ck, dn_t, _row(dy, 1024, 1))
    (d_ps2,), (d_a_log_b, d_dt_b) = _rowwise_bwd("dn_gate_bwd", _dn_gate_fn, [_row(p, 128, CB_S2)], [a_log_b, dt_b],
                                                 [_row(d_dn[3]), _row(d_dn[4])])
    dp_dn, d_conv_parts = [], []
    for i in range(3):
        d_us, d_cw = _rowwise_bwd("dn_conv_bwd%d" % i, _conv_fn, conv_rows[i], conv_w_rows[i], [_row(d_dn[i])])
        dp_dn.append(_shift_sum("dn_shift_bwd%d" % i, [(_row(d_us[j]), -j) for j in range(4)]))
        d_conv_parts.append(jnp.concatenate(d_cw, axis=0))

    dp = jnp.concatenate(dp_rw[:3] + [d_gate] + dp_dn + [d_dn[5], dp_rw[3], d_ps2], axis=1)
    du = _mm("d_in_proj", dp, W, "nt", tn=1024, tk=1408)
    d_W = _mm("d_w_in", u, dp, "tn", tm=1024, tn=1408, tk=MM_ROW_TILE)
    (dh1,), (d_norm_w,) = _rowwise_bwd("rms_in_bwd", _rms_fn, [_row(h)], [norm_w], [_row(du)])
    (dh,) = _rowwise("dh_sum", lambda a_, b_: (a_ + b_,), [_row(dz), _row(dh1)], [], [D_MODEL])
    grad_x = dh[N_META:L][None]

    d_mu = jnp.concatenate(d_prep_pars[0:4], axis=1)
    d_w2, d_a2 = d_prep_pars[5][:RW_LORA], d_prep_pars[7][RW_LORA:]
    d_conv = jnp.concatenate(d_conv_parts, axis=1)
    d_meta = dh[:N_META]
    head_sum = lambda t: jnp.sum(t.reshape(1, DN_HEADS, DN_HEAD), axis=-1)
    rep_names = ["norm_w", "rw_shift_mu", "rw_w0", "rw_a0", "rw_k_k", "rw_k_a", "rw_r_k", "rw_gn_w", "rw_gn_b",
                 "dn_A_log", "dn_dt_bias", "dn_norm_w", "final_norm_w"]
    rep_g = [d_norm_w, d_mu, d_prep_pars[4], d_prep_pars[6], d_prep_pars[8], d_prep_pars[9],
             d_rw_pars[0].reshape(1, RW_WIDTH), d_rw_pars[1].reshape(1, RW_WIDTH), d_rw_pars[2].reshape(1, RW_WIDTH),
             head_sum(d_a_log_b), head_sum(d_dt_b), d_dn_nw, d_fw.reshape(D_MODEL)]
    rep_w = [norm_w, rw_shift_mu, rw_w0, rw_a0, rw_k_k, rw_k_a, rw_r_k, rw_gn_w, rw_gn_b, dn_A_log, dn_dt_bias,
             dn_norm_w, final_norm_w]
    rep_m = [m_norm_w, m_rw_shift_mu, m_rw_w0, m_rw_a0, m_rw_k_k, m_rw_k_a, m_rw_r_k, m_rw_gn_w, m_rw_gn_b,
             m_dn_A_log, m_dn_dt_bias, m_dn_norm_w, m_final_norm_w]
    rep_v = [v_norm_w, v_rw_shift_mu, v_rw_w0, v_rw_a0, v_rw_k_k, v_rw_k_a, v_rw_r_k, v_rw_gn_w, v_rw_gn_b,
             v_dn_A_log, v_dn_dt_bias, v_dn_norm_w, v_final_norm_w]
    rep_shapes = [t.shape for t in rep_w]
    rep_all = _exchange("gather_rep_grads", _pack(rep_g, 128), ALL_MASKS, "dev", False)
    rep_out = _adamw("adam_rep", rep_all, _pack(rep_w, 128), _pack(rep_m, 128), _pack(rep_v, 128))
    rep_out = [dict(zip(rep_names, _unpack(t, rep_shapes))) for t in rep_out]

    sm_slabs = jnp.stack([_pack([d_w2[:, 256 * s:256 * (s + 1)], d_a2[:, 256 * s:256 * (s + 1)],
                                 d_conv[:, 768 * s:768 * (s + 1)], d_meta[:, 512 * s:512 * (s + 1)]], 1024, 64)
                          for s in range(N_CHIPS)])
    sm_parts = _reduce_to_shard("rs_small", sm_slabs, 1)
    sm_w = [rw_w2[0], rw_a2[0], dn_conv_w[0], meta_tokens]
    sm_m = [m_rw_w2[0], m_rw_a2[0], m_dn_conv_w[0], m_meta_tokens]
    sm_v = [v_rw_w2[0], v_rw_a2[0], v_dn_conv_w[0], v_meta_tokens]
    sm_out = _adamw("adam_small", sm_parts, _pack(sm_w, 1024, 64), _pack(sm_m, 1024, 64), _pack(sm_v, 1024, 64))
    sm_names = ["rw_w2", "rw_a2", "dn_conv_w", "meta_tokens"]
    sm_full_shapes = [(1, RW_LORA, 256), (1, RW_LORA, 256), (1, CONV_W, 768), (N_META, 512)]
    sm_out = [dict(zip(sm_names, [t.reshape(shp) for t, shp in zip(_unpack(o, small_shapes), sm_full_shapes)]))
              for o in sm_out]

    wo_parts = _reduce_to_shard("rs_w_out", d_wo.reshape(N_CHIPS, D_MODEL // N_CHIPS, D_MODEL), 8)
    wo_out = _adamw("adam_w_out", wo_parts, w_out[0], m_w_out[0], v_w_out[0])
    d_w_in_slabs = jnp.transpose(_from_my_layout(d_W).reshape(D_MODEL, N_CHIPS, SHARD_COLS), (1, 0, 2))
    wi_parts = _reduce_to_shard("rs_w_in", d_w_in_slabs, 8)
    wi_out = _adamw("adam_w_in", wi_parts, w_in[0], m_w_in[0], v_w_in[0])

    order = ["meta_tokens", "norm_w", "w_in", "rw_shift_mu", "rw_w0", "rw_w2", "rw_a0", "rw_a2", "rw_k_k", "rw_k_a",
             "rw_r_k", "rw_gn_w", "rw_gn_b", "dn_conv_w", "dn_A_log", "dn_dt_bias", "dn_norm_w", "w_out",
             "final_norm_w"]
    outs = [loss, grad_x]
    for kind in range(4):
        table = dict(rep_out[kind])
        table.update(sm_out[kind])
        table["w_in"] = wi_out[kind][None]
        table["w_out"] = wo_out[kind][None]
        outs += [table[n] for n in order]
    return tuple(outs)
```

```python
import functools

import jax
import jax.numpy as jnp
from jax import lax
from jax.experimental import pallas as pl
from jax.experimental.pallas import tpu as pltpu

F32 = jnp.float32
BF16 = jnp.bfloat16
HIGH = lax.Precision.HIGH
HIGHEST = lax.Precision.HIGHEST
MESH = pl.DeviceIdType.MESH

D_MODEL = 2048
N_META = 16
RW_WIDTH = 1024
RW_HEAD = 64
RW_HEADS = 16
RW_LORA = 64
RW_GN_EPS = 64e-5
DN_WIDTH = 1024
DN_HEAD = 128
DN_HEADS = 8
CONV_W = 4
CHUNK = 64
NORM_EPS = 1e-6
IN_COLS = 8336
N_CHIPS = 4
SHARD_COLS = IN_COLS // N_CHIPS

NP_COLS = 8 * 1024 + 256
CB_R, CB_K, CB_V, CB_GATE, CB_DQ, CB_DK, CB_DV, CB_Z = range(8)
CB_S1 = 8192 // 128
CB_S2 = CB_S1 + 1

ADAM_LR = 0.001
ADAM_B1 = 0.9
ADAM_B2 = 0.999
ADAM_EPS = 1e-08
ADAM_WD = 0.01
ADAM_STEP = 10

VMEM_LIMIT_BYTES = 56 * 1024 * 1024
ROW_TILE = 104
MM_ROW_TILE = 520


def _params(sem=None):
    return pltpu.CompilerParams(dimension_semantics=sem, vmem_limit_bytes=VMEM_LIMIT_BYTES)


def _pick(n, target, mult=8):
    best = None
    for d in range(mult, min(n, target) + 1, mult):
        if n % d == 0:
            best = d
    return n if best is None else best


def _mm(name, a, b, mode, tm=MM_ROW_TILE, tn=1408, tk=2048):
    if mode == "nn":
        (M, K), (_, N) = a.shape, b.shape
    elif mode == "nt":
        (M, K), (N, _) = a.shape, b.shape
    else:
        (K, M), (_, N) = a.shape, b.shape
    tm = _pick(M, tm, 128 if mode == "tn" else 8)
    tn = _pick(N, tn, 128)
    tk = _pick(K, tk, 8 if mode == "tn" else 128)
    if mode == "nn":
        a_spec = pl.BlockSpec((tm, tk), lambda i, j, k: (i, k))
        b_spec = pl.BlockSpec((tk, tn), lambda i, j, k: (k, j))
        dims = (((1,), (0,)), ((), ()))
    elif mode == "nt":
        a_spec = pl.BlockSpec((tm, tk), lambda i, j, k: (i, k))
        b_spec = pl.BlockSpec((tn, tk), lambda i, j, k: (j, k))
        dims = (((1,), (1,)), ((), ()))
    else:
        a_spec = pl.BlockSpec((tk, tm), lambda i, j, k: (k, i))
        b_spec = pl.BlockSpec((tk, tn), lambda i, j, k: (k, j))
        dims = (((0,), (0,)), ((), ()))

    def body(a_ref, b_ref, o_ref):
        @pl.when(pl.program_id(2) == 0)
        def _():
            o_ref[...] = jnp.zeros_like(o_ref)

        o_ref[...] += lax.dot_general(a_ref[...].astype(BF16), b_ref[...].astype(BF16), dims,
                                      preferred_element_type=F32)

    return pl.pallas_call(
        body, name=name, grid=(M // tm, N // tn, K // tk),
        in_specs=[a_spec, b_spec], out_specs=pl.BlockSpec((tm, tn), lambda i, j, k: (i, j)),
        out_shape=jax.ShapeDtypeStruct((M, N), F32),
        compiler_params=_params(("parallel", "parallel", "arbitrary")),
    )(a, b)


def _row(arr, width=None, cb=0):
    return (arr, arr.shape[1] if width is None else width, cb)


def _rowwise(name, fn, rows, params, out_widths, tm=ROW_TILE):
    R = rows[0][0].shape[0]
    tm = _pick(R, tm)
    n_r, n_p = len(rows), len(params)

    def body(*refs):
        vals = [r[...] for r in refs[:n_r + n_p]]
        for o_ref, val in zip(refs[n_r + n_p:], fn(*vals)):
            o_ref[...] = val

    in_specs = [pl.BlockSpec((tm, w), lambda i, cb=cb: (i, cb)) for (_, w, cb) in rows]
    in_specs += [pl.BlockSpec(p.shape, lambda i: (0, 0)) for p in params]
    return pl.pallas_call(
        body, name=name, grid=(R // tm,), in_specs=in_specs,
        out_specs=[pl.BlockSpec((tm, w), lambda i: (i, 0)) for w in out_widths],
        out_shape=[jax.ShapeDtypeStruct((R, w), F32) for w in out_widths],
        compiler_params=_params(("parallel",)),
    )(*[r[0] for r in rows], *params)


def _rowwise_bwd(name, fn, rows, params, douts, tm=ROW_TILE):
    R = rows[0][0].shape[0]
    tm = _pick(R, tm)
    n_r, n_p, n_d = len(rows), len(params), len(douts)

    def body(*refs):
        vals = [r[...] for r in refs[:n_r + n_p]]
        cts = tuple(r[...] for r in refs[n_r + n_p:n_r + n_p + n_d])
        grads = jax.vjp(fn, *vals)[1](cts)
        outs = refs[n_r + n_p + n_d:]
        for o_ref, g in zip(outs[:n_r], grads[:n_r]):
            o_ref[...] = g

        @pl.when(pl.program_id(0) == 0)
        def _():
            for o_ref in outs[n_r:]:
                o_ref[...] = jnp.zeros_like(o_ref)

        for o_ref, g in zip(outs[n_r:], grads[n_r:]):
            o_ref[...] += g

    in_specs = [pl.BlockSpec((tm, w), lambda i, cb=cb: (i, cb)) for (_, w, cb) in rows]
    in_specs += [pl.BlockSpec(p.shape, lambda i: (0, 0)) for p in params]
    in_specs += [pl.BlockSpec((tm, w), lambda i, cb=cb: (i, cb)) for (_, w, cb) in douts]
    out_specs = [pl.BlockSpec((tm, w), lambda i: (i, 0)) for (_, w, _) in rows]
    out_specs += [pl.BlockSpec(p.shape, lambda i: (0, 0)) for p in params]
    out_shape = [jax.ShapeDtypeStruct((R, w), F32) for (_, w, _) in rows]
    out_shape += [jax.ShapeDtypeStruct(p.shape, F32) for p in params]
    res = pl.pallas_call(
        body, name=name, grid=(R // tm,), in_specs=in_specs, out_specs=out_specs, out_shape=out_shape,
        compiler_params=_params(("arbitrary",)),
    )(*[r[0] for r in rows], *params, *[d[0] for d in douts])
    return res[:n_r], res[n_r:]


def _shift_sum(name, terms, tm=ROW_TILE):
    R = terms[0][0][0].shape[0]
    w = terms[0][0][1]
    tm = _pick(R, tm)
    nt, nb8 = R // tm, tm // 8
    shifts = [j for (_, j) in terms]

    def body(*refs):
        i = pl.program_id(0)
        acc = None
        for k, j in enumerate(shifts):
            x = refs[2 * k][...]
            if j > 0:
                halo = jnp.where(i == 0, 0.0, refs[2 * k + 1][...])
                x = pltpu.roll(jnp.concatenate([halo, x], axis=0), j, 0)[8:, :]
            elif j < 0:
                halo = jnp.where(i == nt - 1, 0.0, refs[2 * k + 1][...])
                x = pltpu.roll(jnp.concatenate([x, halo], axis=0), tm + 8 + j, 0)[:tm, :]
            acc = x if acc is None else acc + x
        refs[-1][...] = acc

    in_specs, args = [], []
    for (arr, _, cb), j in terms:
        in_specs.append(pl.BlockSpec((tm, w), lambda i, cb=cb: (i, cb)))
        if j > 0:
            in_specs.append(pl.BlockSpec((8, w), lambda i, cb=cb: (jnp.maximum(i * nb8 - 1, 0), cb)))
        else:
            in_specs.append(pl.BlockSpec((8, w), lambda i, cb=cb: (jnp.minimum((i + 1) * nb8, R // 8 - 1), cb)))
        args += [arr, arr]
    return pl.pallas_call(
        body, name=name, grid=(nt,), in_specs=in_specs, out_specs=pl.BlockSpec((tm, w), lambda i: (i, 0)),
        out_shape=jax.ShapeDtypeStruct((R, w), F32), compiler_params=_params(("parallel",)),
    )(*args)


def _softplus(x):
    return jnp.maximum(x, 0.0) + jnp.log(1.0 + jnp.exp(-jnp.abs(x)))


def _silu(x):
    return x * jax.nn.sigmoid(x)


def _rms_fn(h, w):
    return (h * lax.rsqrt(jnp.mean(h * h, axis=-1, keepdims=True) + NORM_EPS) * w,)


def _rw_prep_fn(pr, pr1, pk, pk1, pv, pv1, ps, ps1, mu_r, mu_k, mu_v, mu_s, w0, w2p, a0, a2p, k_k, k_a):
    r = pr + (pr1 - pr) * mu_r
    k = pk + (pk1 - pk) * mu_k
    v = pv + (pv1 - pv) * mu_v
    s = ps + (ps1 - ps) * mu_s
    w_log = -_softplus(-(w0 + jnp.dot(jnp.tanh(s), w2p, precision=HIGH, preferred_element_type=F32))) - 0.5
    log_decay = -jnp.exp(w_log)
    a = jax.nn.sigmoid(a0 + jnp.dot(s, a2p, precision=HIGH, preferred_element_type=F32))
    return r, log_decay, k * (1.0 + (a - 1.0) * k_a), v, k * k_k, a


def _conv_fn(u0, u1, u2, u3, w0, w1, w2, w3):
    return (_silu(u0 * w3 + u1 * w2 + u2 * w1 + u3 * w0),)


def _dn_gate_fn(ps2, a_log_b, dt_b):
    lane = lax.broadcasted_iota(jnp.int32, (DN_HEAD, DN_WIDTH), 0)
    head = jnp.right_shift(lax.broadcasted_iota(jnp.int32, (DN_HEAD, DN_WIDTH), 1), 7)
    sel_b = (lane == head).astype(F32)
    sel_a = (lane == head + DN_HEADS).astype(F32)
    b = jnp.dot(ps2, sel_b, precision=HIGHEST, preferred_element_type=F32)
    al = jnp.dot(ps2, sel_a, precision=HIGHEST, preferred_element_type=F32)
    return jax.nn.sigmoid(b), -jnp.exp(a_log_b) * _softplus(al + dt_b)


def _tril_masks(c):
    t = lax.broadcasted_iota(jnp.int32, (c, c), 0)
    s = lax.broadcasted_iota(jnp.int32, (c, c), 1)
    return s <= t, s < t


def _dot(x, y, cx, cy, prec=HIGH):
    nb = x.ndim - 2
    batch = tuple(range(nb))
    return lax.dot_general(x, y, (((cx + nb,), (cy + nb,)), (batch, batch)), precision=prec,
                           preferred_element_type=F32)


def _tri_inv(a):
    c = a.shape[-1]
    eye = (lax.broadcasted_iota(jnp.int32, (c, c), 0) == lax.broadcasted_iota(jnp.int32, (c, c), 1)).astype(F32)
    x = eye - a
    p = a
    n = 2
    while n < c:
        p = _dot(p, p, 1, 0)
        x = x + _dot(x, p, 1, 0)
        n *= 2
    return x


@jax.custom_vjp
def _tri_inv_saved(a, t):
    return t


def _tri_inv_saved_fwd(a, t):
    return t, t


def _tri_inv_saved_bwd(t, dt):
    return -_dot(_dot(t, dt, 0, 0), t, 1, 1), jnp.zeros_like(t)


_tri_inv_saved.defvjp(_tri_inv_saved_fwd, _tri_inv_saved_bwd)


def _pair_masks():
    lane = lax.broadcasted_iota(jnp.int32, (1, 2 * RW_HEAD), 1)
    m0 = (lane < RW_HEAD).astype(F32)
    return m0, 1.0 - m0


def _pair_bd(x):
    m0, m1 = _pair_masks()
    return jnp.concatenate([x * m0, x * m1], axis=-2)


def _pair_mm(x, y):
    return _dot(x, _pair_bd(y), 1, 0)


def _pair_inv(a):
    c = a.shape[-2]
    row = lax.broadcasted_iota(jnp.int32, (c, 2 * RW_HEAD), 0)
    col = lax.broadcasted_iota(jnp.int32, (c, 2 * RW_HEAD), 1) & (RW_HEAD - 1)
    x = (row == col).astype(F32) - a
    p = a
    n = 2
    while n < c:
        p = _pair_mm(p, p)
        x = x + _pair_mm(x, p)
        n *= 2
    return x


@jax.custom_vjp
def _pair_inv_saved(a, t):
    return t


def _pair_inv_saved_fwd(a, t):
    return t, t


def _pair_inv_saved_bwd(t, dt):
    m0, m1 = _pair_masks()
    c = t.shape[-2]
    z = _dot(t, dt, 0, 0)
    x = z[:, :c, :] * m0 + z[:, c:, :] * m1
    return -_dot(x, _pair_bd(t), 1, 1), jnp.zeros_like(t)


_pair_inv_saved.defvjp(_pair_inv_saved_fwd, _pair_inv_saved_bwd)


def _rw_chunk_fn(S, r, lw, k2, v, kkp, a, gate, rk, gnw, gnb, t_saved=None):
    B, C, P = r.shape
    m0, m1 = _pair_masks()
    seg = lambda x: (jnp.sum(x * m0, axis=-1, keepdims=True) * m0 + jnp.sum(x * m1, axis=-1, keepdims=True) * m1)
    mm = lambda x, y: _dot(x, y, 1, 0)
    nt = lambda x, y: _dot(x, y, 1, 1)
    tn = lambda x, y: _dot(x, y, 0, 0)
    t_idx = lax.broadcasted_iota(jnp.int32, (C, P), 0)
    s_idx = lax.broadcasted_iota(jnp.int32, (C, P), 1) & (RW_HEAD - 1)
    incl, strict = s_idx <= t_idx, s_idx < t_idx
    tril = jnp.broadcast_to(_tril_masks(C)[0].astype(F32), (B, C, C))
    kk = kkp * lax.rsqrt(seg(kkp * kkp) + 1e-6)
    b = kk * a
    g_incl = _dot(tril, lw, 1, 0, HIGHEST)
    g_excl = g_incl - lw
    inv = jnp.exp(-g_incl)
    alpha, beta, kappa, rho = kk * jnp.exp(g_excl), b * inv, k2 * inv, r * jnp.exp(g_incl)
    ar = jnp.concatenate([alpha, rho], axis=-2)
    scores = nt(ar, jnp.concatenate([_pair_bd(beta), _pair_bd(kappa)], axis=-2))
    a_ab = jnp.where(strict, scores[:, :C, :P], 0.0)
    a_ak = jnp.where(strict, scores[:, :C, P:], 0.0)
    r_b = jnp.where(incl, scores[:, C:, :P], 0.0)
    r_k = jnp.where(incl, scores[:, C:, P:], 0.0)
    t_inv = _pair_inv(a_ab) if t_saved is None else _pair_inv_saved(a_ab, t_saved)
    on_state = nt(ar, S)
    u = _pair_mm(t_inv, -on_state[:, :C, :] - _pair_mm(a_ak, v))
    y = on_state[:, C:, :] + mm(jnp.concatenate([r_b, r_k], axis=-1),
                                jnp.concatenate([_pair_bd(u), _pair_bd(v)], axis=-2))
    same_head = ((lax.broadcasted_iota(jnp.int32, (P, P), 0) < RW_HEAD)
                 == (lax.broadcasted_iota(jnp.int32, (P, P), 1) < RW_HEAD))
    fresh = tn(jnp.concatenate([u, v], axis=-2), jnp.concatenate([beta, kappa], axis=-2))
    S_new = jnp.exp(jnp.sum(lw, axis=-2, keepdims=True)) * (S + jnp.where(same_head, fresh, 0.0))
    dev = y - seg(y) * (1.0 / RW_HEAD)
    yn = dev * lax.rsqrt(seg(dev * dev) * (1.0 / RW_HEAD) + RW_GN_EPS) * gnw + gnb
    bonus = seg(r * k2 * rk) * v
    return (yn + bonus) * _silu(gate), S_new, t_inv


def _dn_chunk_fn(S, qc, kc, vc, bb, gb, z, nw, t_saved=None):
    B, C, D = qc.shape
    mm = lambda x, y: _dot(x, y, 1, 0)
    nt = lambda x, y, p=HIGH: _dot(x, y, 1, 1, p)
    tn = lambda x, y: _dot(x, y, 0, 0)
    incl, strict = _tril_masks(C)
    q = qc * lax.rsqrt(jnp.sum(qc * qc, axis=-1, keepdims=True) + 1e-6) * (D ** -0.5)
    k = kc * lax.rsqrt(jnp.sum(kc * kc, axis=-1, keepdims=True) + 1e-6)
    kb, vb = k * bb, vc * bb
    G = _dot(jnp.broadcast_to(incl.astype(F32), (B, C, C)), gb, 1, 0, HIGHEST)
    lane = lax.broadcasted_iota(jnp.int32, (C, D), 1)
    e0, e1 = (lane == 0).astype(F32), (lane == 1).astype(F32)
    diff = nt(G * e0 + e1, e0 - G * e1, HIGHEST)
    dmask = jnp.where(incl, jnp.exp(jnp.where(incl, diff, 0.0)), 0.0)
    M = jnp.where(strict, nt(kb, k) * dmask, 0.0)
    T = _tri_inv(M) if t_saved is None else _tri_inv_saved(M, t_saved)
    eG = jnp.exp(G)
    u = mm(T, vb)
    w = mm(T, kb * eG)
    attn = jnp.where(incl, nt(q, k) * dmask, 0.0)
    v_new = u - mm(w, S)
    o = mm(q * eG, S) + mm(attn, v_new)
    g_last = jnp.sum(gb, axis=-2, keepdims=True)
    S_new = S * jnp.exp(jnp.broadcast_to(g_last, S.shape)) + tn(k * jnp.exp(g_last - G), v_new)
    on = o * lax.rsqrt(jnp.mean(o * o, axis=-1, keepdims=True) + NORM_EPS) * nw
    return on * _silu(z), S_new, T


N_GROUPS = 8
GROUP = 128


def _groups(ref):
    return jnp.stack([ref[:, g * GROUP:(g + 1) * GROUP] for g in range(N_GROUPS)])


def _chunk_fwd(name, fn, rows, pars, t_lanes):
    Lp = rows[0][0].shape[0]
    C, D, NB = CHUNK, GROUP, N_GROUPS
    nc, n_r, n_p = Lp // C, len(rows), len(pars)

    def body(*refs):
        row_refs, par_refs = refs[:n_r], refs[n_r:n_r + n_p]
        y_ref, ck_ref, t_ref, s_ref = refs[n_r + n_p:]

        @pl.when(pl.program_id(0) == 0)
        def _():
            s_ref[...] = jnp.zeros_like(s_ref)

        S = s_ref[...]
        ck_ref[0] = S
        y, S_new, t_inv = fn(S, *[_groups(r) for r in row_refs], *[_groups(r) for r in par_refs])
        for g in range(NB):
            y_ref[:, g * D:(g + 1) * D] = y[g]
        t_ref[0] = t_inv
        s_ref[...] = S_new

    return pl.pallas_call(
        body, name=name, grid=(nc,),
        in_specs=[pl.BlockSpec((C, w), lambda c, cb=cb: (c, cb)) for (_, w, cb) in rows]
        + [pl.BlockSpec((1, NB * D), lambda c: (0, 0))] * n_p,
        out_specs=[pl.BlockSpec((C, NB * D), lambda c: (c, 0)),
                   pl.BlockSpec((1, NB, D, D), lambda c: (c, 0, 0, 0)),
                   pl.BlockSpec((1, NB, C, t_lanes), lambda c: (c, 0, 0, 0))],
        out_shape=[jax.ShapeDtypeStruct((Lp, NB * D), F32), jax.ShapeDtypeStruct((nc, NB, D, D), F32),
                   jax.ShapeDtypeStruct((nc, NB, C, t_lanes), F32)],
        scratch_shapes=[pltpu.VMEM((NB, D, D), F32)],
        compiler_params=_params(("arbitrary",)),
    )(*[r[0] for r in rows], *pars)


def _chunk_bwd(name, fn, rows, pars, ckpt, tinv, dy):
    Lp = rows[0][0].shape[0]
    C, D, NB = CHUNK, GROUP, N_GROUPS
    nc, n_r, n_p = Lp // C, len(rows), len(pars)
    t_lanes = tinv.shape[-1]

    def body(*refs):
        row_refs, par_refs = refs[:n_r], refs[n_r:n_r + n_p]
        ck_ref, t_ref, dy_ref = refs[n_r + n_p:n_r + n_p + 3]
        outs = refs[n_r + n_p + 3:]
        drow_refs, dpar_refs, ds_ref = outs[:n_r], outs[n_r:n_r + n_p], outs[n_r + n_p]

        @pl.when(pl.program_id(0) == 0)
        def _():
            ds_ref[...] = jnp.zeros_like(ds_ref)
            for o_ref in dpar_refs:
                o_ref[...] = jnp.zeros_like(o_ref)

        vals = [ck_ref[0]] + [_groups(r) for r in row_refs] + [_groups(r) for r in par_refs]
        t_saved = t_ref[0]
        grads = jax.vjp(lambda *v: fn(*v, t_saved=t_saved)[:2], *vals)[1]((_groups(dy_ref), ds_ref[...]))
        ds_ref[...] = grads[0]
        for o_ref, g in zip(drow_refs, grads[1:1 + n_r]):
            for i in range(NB):
                o_ref[:, i * D:(i + 1) * D] = g[i]
        for o_ref, g in zip(dpar_refs, grads[1 + n_r:]):
            for i in range(NB):
                o_ref[:, i * D:(i + 1) * D] += g[i]

    rev = lambda c: nc - 1 - c
    dy_arr, dy_w, dy_cb = dy
    res = pl.pallas_call(
        body, name=name, grid=(nc,),
        in_specs=[pl.BlockSpec((C, w), lambda c, cb=cb: (rev(c), cb)) for (_, w, cb) in rows]
        + [pl.BlockSpec((1, NB * D), lambda c: (0, 0))] * n_p
        + [pl.BlockSpec((1, NB, D, D), lambda c: (rev(c), 0, 0, 0)),
           pl.BlockSpec((1, NB, C, t_lanes), lambda c: (rev(c), 0, 0, 0)),
           pl.BlockSpec((C, dy_w), lambda c: (rev(c), dy_cb))],
        out_specs=[pl.BlockSpec((C, NB * D), lambda c: (rev(c), 0))] * n_r
        + [pl.BlockSpec((1, NB * D), lambda c: (0, 0))] * n_p,
        out_shape=[jax.ShapeDtypeStruct((Lp, NB * D), F32)] * n_r + [jax.ShapeDtypeStruct((1, NB * D), F32)] * n_p,
        scratch_shapes=[pltpu.VMEM((NB, D, D), F32)],
        compiler_params=_params(("arbitrary",)),
    )(*[r[0] for r in rows], *pars, ckpt, tinv, dy_arr)
    return res[:n_r], res[n_r:]


def _loss_head(h, yw, tgt, fw, n_real, tm=ROW_TILE):
    Lp, Dm = h.shape
    tm = _pick(Lp, tm)

    def out_fn(z, fw_):
        return z * lax.rsqrt(jnp.mean(z * z, axis=-1, keepdims=True) + NORM_EPS) * fw_

    def body(h_ref, yw_ref, t_ref, fw_ref, loss_ref, dz_ref, dfw_ref):
        i = pl.program_id(0)

        @pl.when(i == 0)
        def _():
            loss_ref[...] = jnp.zeros_like(loss_ref)
            dfw_ref[...] = jnp.zeros_like(dfw_ref)

        row = i * tm + lax.broadcasted_iota(jnp.int32, (tm, 1), 0)
        mask = ((row >= N_META) & (row < n_real)).astype(F32)
        z = h_ref[...] + yw_ref[...]
        o, vjp = jax.vjp(out_fn, z, fw_ref[...])
        err = (o - t_ref[...]) * mask
        row_loss = 0.5 * jnp.mean(jnp.square(err), axis=-1, keepdims=True)
        dz, dfw = vjp(err * (1.0 / Dm))
        loss_ref[...] += jnp.sum(row_loss, axis=0, keepdims=True)
        dz_ref[...] = dz
        dfw_ref[...] += dfw

    row_spec = pl.BlockSpec((tm, Dm), lambda i: (i, 0))
    return pl.pallas_call(
        body, name="loss_head", grid=(Lp // tm,),
        in_specs=[row_spec, row_spec, row_spec, pl.BlockSpec((1, Dm), lambda i: (0, 0))],
        out_specs=[pl.BlockSpec((8, 128), lambda i: (0, 0)), row_spec, pl.BlockSpec((1, Dm), lambda i: (0, 0))],
        out_shape=[jax.ShapeDtypeStruct((8, 128), F32), jax.ShapeDtypeStruct((Lp, Dm), F32),
                   jax.ShapeDtypeStruct((1, Dm), F32)],
        compiler_params=_params(("arbitrary",)),
    )(h, yw, tgt, fw)


def _exchange(name, x, masks, slot_kind, per_dest, n_split=1, copy_own=True):
    n = len(masks)
    keep_own = slot_kind is not None and copy_own
    n_slots = {"chip": 4, "core": 2, "dev": 8, None: n}[slot_kind]
    blk_shape = x.shape[1:] if per_dest else x.shape
    rows = blk_shape[0] // n_split

    def body(x_ref, o_ref, send_sems, recv_sems, local_sems):
        mx, my, mc = lax.axis_index("x"), lax.axis_index("y"), lax.axis_index("c")

        def slot(k, px, py, pc):
            return {"chip": 2 * px + py, "core": pc, "dev": 4 * px + 2 * py + pc, None: k}[slot_kind]

        def peer(m):
            return (mx + m[0]) % 2, (my + m[1]) % 2, (mc + m[2]) % 2

        def part(ref, j):
            return ref.at[pl.ds(j * rows, rows)]

        own_src = x_ref.at[2 * mx + my] if per_dest else x_ref
        local = []
        if keep_own:
            own_dst = o_ref.at[slot(0, mx, my, mc)]
            local = [pltpu.make_async_copy(part(own_src, j), part(own_dst, j), local_sems.at[j])
                     for j in range(n_split)]
        for cp in local:
            cp.start()
        sends = []
        for k, m in enumerate(masks):
            px, py, pc = peer(m)
            src = x_ref.at[2 * px + py] if per_dest else x_ref
            dst = o_ref.at[slot(k, mx, my, mc)]
            for j in range(n_split):
                sends.append(pltpu.make_async_remote_copy(
                    src_ref=part(src, j), dst_ref=part(dst, j), send_sem=send_sems.at[k * n_split + j],
                    recv_sem=recv_sems.at[k * n_split + j], device_id=(px, py, pc), device_id_type=MESH))
        for cp in sends:
            cp.start()
        for k, m in enumerate(masks):
            px, py, pc = peer(m)
            landed = o_ref.at[slot(k, px, py, pc)]
            for j in range(n_split):
                pltpu.make_async_remote_copy(
                    src_ref=part(own_src, j), dst_ref=part(landed, j), send_sem=send_sems.at[k * n_split + j],
                    recv_sem=recv_sems.at[k * n_split + j], device_id=(px, py, pc), device_id_type=MESH).wait_recv()
        for cp in sends:
            cp.wait_send()
        for cp in local:
            cp.wait()

    return pl.pallas_call(
        body, name=name,
        in_specs=[pl.BlockSpec(memory_space=pl.ANY)], out_specs=pl.BlockSpec(memory_space=pl.ANY),
        out_shape=jax.ShapeDtypeStruct((n_slots,) + tuple(blk_shape), x.dtype),
        scratch_shapes=[pltpu.SemaphoreType.DMA((n * n_split,)), pltpu.SemaphoreType.DMA((n * n_split,)),
                        pltpu.SemaphoreType.DMA((n_split,))],
        compiler_params=pltpu.CompilerParams(has_side_effects=True),
    )(x)


CHIP_MASKS = [(1, 0, 0), (0, 1, 0), (1, 1, 0)]
CORE_MASKS = [(0, 0, 1)]
ALL_MASKS = [(dx, dy, dc) for dx in (0, 1) for dy in (0, 1) for dc in (0, 1) if (dx, dy, dc) != (0, 0, 0)]


def _gather_chips(name, x):
    return _exchange(name, x, CHIP_MASKS, "chip", False)


def _gather_shards(name, shard, n_split):
    half = shard.shape[0] // 2
    c, chip = lax.axis_index("c"), 2 * lax.axis_index("x") + lax.axis_index("y")
    mine = lax.dynamic_slice_in_dim(shard, c * half, half, axis=0)
    by_chip = _exchange(name + "_chips", mine, CHIP_MASKS, "chip", False, n_split, copy_own=False)
    by_chip = lax.dynamic_update_index_in_dim(by_chip, mine, chip, 0)
    both = _exchange(name + "_cores", by_chip, CORE_MASKS, "core", False, N_CHIPS, copy_own=False)
    return lax.dynamic_update_index_in_dim(both, by_chip, c, 0)


def _sum_slots(name, x, tr=128):
    S, R, N = x.shape
    tr = _pick(R, tr, 16)

    def body(x_ref, o_ref):
        acc = x_ref[0].astype(F32)
        for s in range(1, S):
            acc = acc + x_ref[s].astype(F32)
        o_ref[...] = acc

    return pl.pallas_call(
        body, name=name, grid=(R // tr,),
        in_specs=[pl.BlockSpec((S, tr, N), lambda i: (0, i, 0))], out_specs=pl.BlockSpec((tr, N), lambda i: (i, 0)),
        out_shape=jax.ShapeDtypeStruct((R, N), F32), compiler_params=_params(("parallel",)),
    )(x)


def _add_to_bf16(name, a, b, tr=128):
    S, R, N = a.shape
    tr = _pick(R, tr, 16)

    def body(a_ref, b_ref, o_ref):
        o_ref[...] = (a_ref[...] + b_ref[...]).astype(BF16)

    spec = pl.BlockSpec((S, tr, N), lambda i: (0, i, 0))
    return pl.pallas_call(
        body, name=name, grid=(R // tr,), in_specs=[spec, spec], out_specs=spec,
        out_shape=jax.ShapeDtypeStruct((S, R, N), BF16), compiler_params=_params(("parallel",)),
    )(a, b)


def _adamw(name, gparts, w, m, v, tr=128):
    S, R, N = gparts.shape
    tr = _pick(R, tr)
    c1 = 1.0 / (1.0 - ADAM_B1 ** ADAM_STEP)
    c2 = 1.0 / (1.0 - ADAM_B2 ** ADAM_STEP)

    def body(g_ref, w_ref, m_ref, v_ref, go_ref, d_ref, mo_ref, vo_ref):
        g = g_ref[0]
        for s in range(1, S):
            g = g + g_ref[s]
        m_new = ADAM_B1 * m_ref[...] + (1.0 - ADAM_B1) * g
        v_new = ADAM_B2 * v_ref[...] + (1.0 - ADAM_B2) * jnp.square(g)
        go_ref[...] = g
        mo_ref[...] = m_new
        vo_ref[...] = v_new
        d_ref[...] = -ADAM_LR * ((m_new * c1) / (jnp.sqrt(v_new * c2) + ADAM_EPS) + ADAM_WD * w_ref[...])

    spec = pl.BlockSpec((tr, N), lambda i: (i, 0))
    return pl.pallas_call(
        body, name=name, grid=(R // tr,),
        in_specs=[pl.BlockSpec((S, tr, N), lambda i: (0, i, 0)), spec, spec, spec], out_specs=[spec] * 4,
        out_shape=[jax.ShapeDtypeStruct((R, N), F32)] * 4, compiler_params=_params(("parallel",)),
    )(gparts, w, m, v)


def _reduce_to_shard(name, slabs, n_split):
    _, R, N = slabs.shape
    half = R // 2
    c, chip = lax.axis_index("c"), 2 * lax.axis_index("x") + lax.axis_index("y")
    halves = slabs.reshape(N_CHIPS, 2, half, N)
    mine = lax.dynamic_index_in_dim(halves, c, axis=1, keepdims=False)
    theirs = lax.dynamic_index_in_dim(halves, 1 - c, axis=1, keepdims=False)
    from_sibling = _exchange(name + "_sib", theirs, CORE_MASKS, None, False, N_CHIPS)[0]
    wire = _add_to_bf16(name + "_add", mine, from_sibling)
    got = _exchange(name + "_chips", wire, CHIP_MASKS, "chip", True, n_split, copy_own=False)
    got = lax.dynamic_update_index_in_dim(got, lax.dynamic_index_in_dim(wire, chip, 0, keepdims=False), chip, 0)
    part = _sum_slots(name + "_sum", got)
    both = _exchange(name + "_cores", part, CORE_MASKS, "core", False, n_split, copy_own=False)
    return lax.dynamic_update_index_in_dim(both, part, c, 0).reshape(1, R, N)


def _pack(pieces, cols, row_mult=8):
    flat = jnp.concatenate([p.reshape(-1) for p in pieces])
    rows = -(-flat.shape[0] // cols)
    rows = -(-rows // row_mult) * row_mult
    return jnp.pad(flat, (0, rows * cols - flat.shape[0])).reshape(rows, cols)


def _unpack(packed, shapes):
    flat = packed.reshape(-1)
    out, off = [], 0
    for shp in shapes:
        n = 1
        for d in shp:
            n *= d
        out.append(flat[off:off + n].reshape(shp))
        off += n
    return out


def _to_my_layout(w):
    z = jnp.zeros((w.shape[0], 112), w.dtype)
    return jnp.concatenate([w[:, 0:3072], w[:, 3200:4224], w[:, 4224:7296], w[:, 7312:8336],
                            w[:, 3072:3200], w[:, 7296:7312], z], axis=1)


def _from_my_layout(w):
    return jnp.concatenate([w[:, 0:3072], w[:, 8192:8320], w[:, 3072:4096], w[:, 4096:7168],
                            w[:, 8320:8336], w[:, 7168:8192]], axis=1)


def kernel(x, meta_tokens, norm_w, w_in, rw_shift_mu, rw_w0, rw_w2, rw_a0, rw_a2, rw_k_k, rw_k_a, rw_r_k, rw_gn_w, rw_gn_b, dn_conv_w, dn_A_log, dn_dt_bias, dn_norm_w, w_out, final_norm_w, loss_target, m_meta_tokens, m_norm_w, m_w_in, m_rw_shift_mu, m_rw_w0, m_rw_w2, m_rw_a0, m_rw_a2, m_rw_k_k, m_rw_k_a, m_rw_r_k, m_rw_gn_w, m_rw_gn_b, m_dn_conv_w, m_dn_A_log, m_dn_dt_bias, m_dn_norm_w, m_w_out, m_final_norm_w, v_meta_tokens, v_norm_w, v_w_in, v_rw_shift_mu, v_rw_w0, v_rw_w2, v_rw_a0, v_rw_a2, v_rw_k_k, v_rw_k_a, v_rw_r_k, v_rw_gn_w, v_rw_gn_b, v_dn_conv_w, v_dn_A_log, v_dn_dt_bias, v_dn_norm_w, v_w_out, v_final_norm_w):
    S = x.shape[1]
    L = N_META + S
    Lp = -(-L // CHUNK) * CHUNK

    small_shapes = [(RW_LORA, 256), (RW_LORA, 256), (CONV_W, 768), (N_META, 512)]
    small_mine = _pack([rw_w2[0], rw_a2[0], dn_conv_w[0], meta_tokens], 1024)
    small_all = _gather_chips("gather_small", small_mine)
    per_chip = [_unpack(small_all[s], small_shapes) for s in range(N_CHIPS)]
    w2, a2, conv_w, meta = [jnp.concatenate([per_chip[s][i] for s in range(N_CHIPS)], axis=1) for i in range(4)]
    w_in_all = _gather_shards("gather_w_in", w_in[0].astype(BF16), 8)
    W = _to_my_layout(jnp.transpose(w_in_all, (0, 2, 1, 3)).reshape(D_MODEL, IN_COLS))
    w_out_all = _gather_shards("gather_w_out", w_out[0].astype(BF16), 8)
    Wo = jnp.transpose(w_out_all, (1, 0, 2, 3)).reshape(D_MODEL, D_MODEL)

    tail = [jnp.zeros((Lp - L, D_MODEL), F32)] if Lp > L else []
    h = jnp.concatenate([meta, x[0]] + tail, axis=0)
    tgt = jnp.concatenate([jnp.zeros((N_META, D_MODEL), F32), loss_target[0]] + tail, axis=0)
    (u,) = _rowwise("rms_in", _rms_fn, [_row(h)], [norm_w], [D_MODEL])
    p = _mm("in_proj", u, W, "nn")

    mu = rw_shift_mu
    zpad = jnp.zeros((RW_LORA, RW_WIDTH), F32)
    rw_params = [mu[:, 0:1024], mu[:, 1024:2048], mu[:, 2048:3072], mu[:, 3072:3200], rw_w0,
                 jnp.concatenate([w2, zpad], axis=0), rw_a0, jnp.concatenate([zpad, a2], axis=0), rw_k_k, rw_k_a]
    rw_in = [_row(p, 1024, CB_R), _row(p, 1024, CB_K), _row(p, 1024, CB_V), _row(p, 128, CB_S1)]
    rw_rows = []
    for i, rin in enumerate(rw_in):
        rw_rows += [rin, _row(_shift_sum("rw_shift%d" % i, [(rin, 1)]))]
    rw_seq = _rowwise("rw_prep", _rw_prep_fn, rw_rows, rw_params, [RW_WIDTH] * 6)
    rw_chunk_rows = [_row(t) for t in rw_seq] + [_row(p, 1024, CB_GATE)]
    rw_chunk_pars = [rw_r_k, rw_gn_w, rw_gn_b]
    ya, rw_ck, rw_t = _chunk_fwd("rw_chunk_fwd", _rw_chunk_fn, rw_chunk_rows, rw_chunk_pars, GROUP)

    conv_rows, conv_w_rows, dn_c = [], [], []
    for i, cb in enumerate((CB_DQ, CB_DK, CB_DV)):
        src = _row(p, 1024, cb)
        rows_i = [src] + [_row(_shift_sum("dn_shift%d_%d" % (i, j), [(src, j)])) for j in (1, 2, 3)]
        conv_rows.append(rows_i)
        conv_w_rows.append([conv_w[j:j + 1, 1024 * i:1024 * (i + 1)] for j in range(CONV_W)])
        (c_i,) = _rowwise("dn_conv%d" % i, _conv_fn, rows_i, conv_w_rows[i], [DN_WIDTH])
        dn_c.append(c_i)
    a_log_b = jnp.repeat(dn_A_log, DN_HEAD, axis=1)
    dt_b = jnp.repeat(dn_dt_bias, DN_HEAD, axis=1)
    beta_b, g_b = _rowwise("dn_gate", _dn_gate_fn, [_row(p, 128, CB_S2)], [a_log_b, dt_b], [DN_WIDTH] * 2)
    dn_rows = [_row(dn_c[0]), _row(dn_c[1]), _row(dn_c[2]), _row(beta_b), _row(g_b), _row(p, 1024, CB_Z)]
    dn_pars = [jnp.tile(dn_norm_w, (1, DN_HEADS))]
    yb, dn_ck, dn_t = _chunk_fwd("dn_chunk_fwd", _dn_chunk_fn, dn_rows, dn_pars, CHUNK)

    y = jnp.concatenate([ya, yb], axis=1)
    yw = _mm("out_proj", y, Wo, "nn", tn=1024)
    loss_acc, dz, d_fw = _loss_head(h, yw, tgt, final_norm_w.reshape(1, D_MODEL), L)
    loss = lax.psum(loss_acc[0, 0], ("x", "y", "c"))

    dy = _mm("d_out_proj", dz, Wo, "nt", tn=1024, tk=2048)
    d_wo = _mm("d_w_out", y, dz, "tn", tm=1024, tn=1024, tk=MM_ROW_TILE)

    d_rw_seq, d_rw_pars = _chunk_bwd("rw_chunk_bwd", _rw_chunk_fn, rw_chunk_rows, rw_chunk_pars, rw_ck, rw_t,
                                     _row(dy, 1024, 0))
    d_gate = d_rw_seq[6]
    d_prep_rows, d_prep_pars = _rowwise_bwd("rw_prep_bwd", _rw_prep_fn, rw_rows, rw_params,
                                            [_row(t) for t in d_rw_seq[:6]])
    dp_rw = [_shift_sum("rw_shift_bwd%d" % i, [(_row(d_prep_rows[2 * i]), 0), (_row(d_prep_rows[2 * i + 1]), -1)])
             for i in range(4)]

    d_dn, (d_dn_nw,) = _chunk_bwd("dn_chunk_bwd", _dn_chunk_fn, dn_rows, dn_pars, dn_ck, dn_t, _row(dy, 1024, 1))
    d_dn_nw = jnp.sum(d_dn_nw.reshape(DN_HEADS, DN_HEAD), axis=0, keepdims=True)
    (d_ps2,), (d_a_log_b, d_dt_b) = _rowwise_bwd("dn_gate_bwd", _dn_gate_fn, [_row(p, 128, CB_S2)], [a_log_b, dt_b],
                                                 [_row(d_dn[3]), _row(d_dn[4])])
    dp_dn, d_conv_parts = [], []
    for i in range(3):
        d_us, d_cw = _rowwise_bwd("dn_conv_bwd%d" % i, _conv_fn, conv_rows[i], conv_w_rows[i], [_row(d_dn[i])])
        dp_dn.append(_shift_sum("dn_shift_bwd%d" % i, [(_row(d_us[j]), -j) for j in range(4)]))
        d_conv_parts.append(jnp.concatenate(d_cw, axis=0))

    dp = jnp.concatenate(dp_rw[:3] + [d_gate] + dp_dn + [d_dn[5], dp_rw[3], d_ps2], axis=1)
    du = _mm("d_in_proj", dp, W, "nt", tn=1024, tk=1408)
    d_W = _mm("d_w_in", u, dp, "tn", tm=1024, tn=1408, tk=MM_ROW_TILE)
    (dh1,), (d_norm_w,) = _rowwise_bwd("rms_in_bwd", _rms_fn, [_row(h)], [norm_w], [_row(du)])
    (dh,) = _rowwise("dh_sum", lambda a_, b_: (a_ + b_,), [_row(dz), _row(dh1)], [], [D_MODEL])
    grad_x = dh[N_META:L][None]

    d_mu = jnp.concatenate(d_prep_pars[0:4], axis=1)
    d_w2, d_a2 = d_prep_pars[5][:RW_LORA], d_prep_pars[7][RW_LORA:]
    d_conv = jnp.concatenate(d_conv_parts, axis=1)
    d_meta = dh[:N_META]
    head_sum = lambda t: jnp.sum(t.reshape(1, DN_HEADS, DN_HEAD), axis=-1)
    rep_names = ["norm_w", "rw_shift_mu", "rw_w0", "rw_a0", "rw_k_k", "rw_k_a", "rw_r_k", "rw_gn_w", "rw_gn_b",
                 "dn_A_log", "dn_dt_bias", "dn_norm_w", "final_norm_w"]
    rep_g = [d_norm_w, d_mu, d_prep_pars[4], d_prep_pars[6], d_prep_pars[8], d_prep_pars[9],
             d_rw_pars[0], d_rw_pars[1], d_rw_pars[2],
             head_sum(d_a_log_b), head_sum(d_dt_b), d_dn_nw, d_fw.reshape(D_MODEL)]
    rep_w = [norm_w, rw_shift_mu, rw_w0, rw_a0, rw_k_k, rw_k_a, rw_r_k, rw_gn_w, rw_gn_b, dn_A_log, dn_dt_bias,
             dn_norm_w, final_norm_w]
    rep_m = [m_norm_w, m_rw_shift_mu, m_rw_w0, m_rw_a0, m_rw_k_k, m_rw_k_a, m_rw_r_k, m_rw_gn_w, m_rw_gn_b,
             m_dn_A_log, m_dn_dt_bias, m_dn_norm_w, m_final_norm_w]
    rep_v = [v_norm_w, v_rw_shift_mu, v_rw_w0, v_rw_a0, v_rw_k_k, v_rw_k_a, v_rw_r_k, v_rw_gn_w, v_rw_gn_b,
             v_dn_A_log, v_dn_dt_bias, v_dn_norm_w, v_final_norm_w]
    rep_shapes = [t.shape for t in rep_w]
    rep_all = _exchange("gather_rep_grads", _pack(rep_g, 128), ALL_MASKS, "dev", False)
    rep_out = _adamw("adam_rep", rep_all, _pack(rep_w, 128), _pack(rep_m, 128), _pack(rep_v, 128))
    rep_out = [dict(zip(rep_names, _unpack(t, rep_shapes))) for t in rep_out]

    sm_slabs = jnp.stack([_pack([d_w2[:, 256 * s:256 * (s + 1)], d_a2[:, 256 * s:256 * (s + 1)],
                                 d_conv[:, 768 * s:768 * (s + 1)], d_meta[:, 512 * s:512 * (s + 1)]], 1024, 64)
                          for s in range(N_CHIPS)])
    sm_parts = _reduce_to_shard("rs_small", sm_slabs, 1)
    sm_w = [rw_w2[0], rw_a2[0], dn_conv_w[0], meta_tokens]
    sm_m = [m_rw_w2[0], m_rw_a2[0], m_dn_conv_w[0], m_meta_tokens]
    sm_v = [v_rw_w2[0], v_rw_a2[0], v_dn_conv_w[0], v_meta_tokens]
    sm_out = _adamw("adam_small", sm_parts, _pack(sm_w, 1024, 64), _pack(sm_m, 1024, 64), _pack(sm_v, 1024, 64))
    sm_names = ["rw_w2", "rw_a2", "dn_conv_w", "meta_tokens"]
    sm_full_shapes = [(1, RW_LORA, 256), (1, RW_LORA, 256), (1, CONV_W, 768), (N_META, 512)]
    sm_out = [dict(zip(sm_names, [t.reshape(shp) for t, shp in zip(_unpack(o, small_shapes), sm_full_shapes)]))
              for o in sm_out]

    wo_parts = _reduce_to_shard("rs_w_out", d_wo.reshape(N_CHIPS, D_MODEL // N_CHIPS, D_MODEL), 8)
    wo_out = _adamw("adam_w_out", wo_parts, w_out[0], m_w_out[0], v_w_out[0])
    d_w_in_slabs = jnp.transpose(_from_my_layout(d_W).reshape(D_MODEL, N_CHIPS, SHARD_COLS), (1, 0, 2))
    wi_parts = _reduce_to_shard("rs_w_in", d_w_in_slabs, 8)
    wi_out = _adamw("adam_w_in", wi_parts, w_in[0], m_w_in[0], v_w_in[0])

    order = ["meta_tokens", "norm_w", "w_in", "rw_shift_mu", "rw_w0", "rw_w2", "rw_a0", "rw_a2", "rw_k_k", "rw_k_a",
             "rw_r_k", "rw_gn_w", "rw_gn_b", "dn_conv_w", "dn_A_log", "dn_dt_bias", "dn_norm_w", "w_out",
             "final_norm_w"]
    outs = [loss, grad_x]
    for kind in range(4):
        table = dict(rep_out[kind])
        table.update(sm_out[kind])
        table["w_in"] = wi_out[kind][None]
        table["w_out"] = wo_out[kind][None]
        outs += [table[n] for n in order]
    return tuple(outs)
```

```python
import functools

import jax
import jax.numpy as jnp
from jax import lax
from jax.experimental import pallas as pl
from jax.experimental.pallas import tpu as pltpu

F32 = jnp.float32
BF16 = jnp.bfloat16
HIGH = lax.Precision.HIGH
HIGHEST = lax.Precision.HIGHEST
MESH = pl.DeviceIdType.MESH

D_MODEL = 2048
N_META = 16
RW_WIDTH = 1024
RW_HEAD = 64
RW_HEADS = 16
RW_LORA = 64
RW_GN_EPS = 64e-5
DN_WIDTH = 1024
DN_HEAD = 128
DN_HEADS = 8
CONV_W = 4
CHUNK = 64
NORM_EPS = 1e-6
IN_COLS = 8336
N_CHIPS = 4
SHARD_COLS = IN_COLS // N_CHIPS

NP_COLS = 8 * 1024 + 256
CB_R, CB_K, CB_V, CB_GATE, CB_DQ, CB_DK, CB_DV, CB_Z = range(8)
CB_S1 = 8192 // 128
CB_S2 = CB_S1 + 1

ADAM_LR = 0.001
ADAM_B1 = 0.9
ADAM_B2 = 0.999
ADAM_EPS = 1e-08
ADAM_WD = 0.01
ADAM_STEP = 10

VMEM_LIMIT_BYTES = 56 * 1024 * 1024
ROW_TILE = 104
MM_ROW_TILE = 520


def _params(sem=None):
    return pltpu.CompilerParams(dimension_semantics=sem, vmem_limit_bytes=VMEM_LIMIT_BYTES)


def _pick(n, target, mult=8):
    best = None
    for d in range(mult, min(n, target) + 1, mult):
        if n % d == 0:
            best = d
    return n if best is None else best


def _mm(name, a, b, mode, tm=MM_ROW_TILE, tn=1408, tk=2048):
    if mode == "nn":
        (M, K), (_, N) = a.shape, b.shape
    elif mode == "nt":
        (M, K), (N, _) = a.shape, b.shape
    else:
        (K, M), (_, N) = a.shape, b.shape
    tm = _pick(M, tm, 128 if mode == "tn" else 8)
    tn = _pick(N, tn, 128)
    tk = _pick(K, tk, 8 if mode == "tn" else 128)
    if mode == "nn":
        a_spec = pl.BlockSpec((tm, tk), lambda i, j, k: (i, k))
        b_spec = pl.BlockSpec((tk, tn), lambda i, j, k: (k, j))
        dims = (((1,), (0,)), ((), ()))
    elif mode == "nt":
        a_spec = pl.BlockSpec((tm, tk), lambda i, j, k: (i, k))
        b_spec = pl.BlockSpec((tn, tk), lambda i, j, k: (j, k))
        dims = (((1,), (1,)), ((), ()))
    else:
        a_spec = pl.BlockSpec((tk, tm), lambda i, j, k: (k, i))
        b_spec = pl.BlockSpec((tk, tn), lambda i, j, k: (k, j))
        dims = (((0,), (0,)), ((), ()))

    def body(a_ref, b_ref, o_ref):
        @pl.when(pl.program_id(2) == 0)
        def _():
            o_ref[...] = jnp.zeros_like(o_ref)

        o_ref[...] += lax.dot_general(a_ref[...].astype(BF16), b_ref[...].astype(BF16), dims,
                                      preferred_element_type=F32)

    return pl.pallas_call(
        body, name=name, grid=(M // tm, N // tn, K // tk),
        in_specs=[a_spec, b_spec], out_specs=pl.BlockSpec((tm, tn), lambda i, j, k: (i, j)),
        out_shape=jax.ShapeDtypeStruct((M, N), F32),
        compiler_params=_params(("parallel", "parallel", "arbitrary")),
    )(a, b)


def _row(arr, width=None, cb=0):
    return (arr, arr.shape[1] if width is None else width, cb)


def _rowwise(name, fn, rows, params, out_widths, tm=ROW_TILE):
    R = rows[0][0].shape[0]
    tm = _pick(R, tm)
    n_r, n_p = len(rows), len(params)

    def body(*refs):
        vals = [r[...] for r in refs[:n_r + n_p]]
        for o_ref, val in zip(refs[n_r + n_p:], fn(*vals)):
            o_ref[...] = val

    in_specs = [pl.BlockSpec((tm, w), lambda i, cb=cb: (i, cb)) for (_, w, cb) in rows]
    in_specs += [pl.BlockSpec(p.shape, lambda i: (0, 0)) for p in params]
    return pl.pallas_call(
        body, name=name, grid=(R // tm,), in_specs=in_specs,
        out_specs=[pl.BlockSpec((tm, w), lambda i: (i, 0)) for w in out_widths],
        out_shape=[jax.ShapeDtypeStruct((R, w), F32) for w in out_widths],
        compiler_params=_params(("parallel",)),
    )(*[r[0] for r in rows], *params)


def _rowwise_bwd(name, fn, rows, params, douts, tm=ROW_TILE):
    R = rows[0][0].shape[0]
    tm = _pick(R, tm)
    n_r, n_p, n_d = len(rows), len(params), len(douts)

    def body(*refs):
        vals = [r[...] for r in refs[:n_r + n_p]]
        cts = tuple(r[...] for r in refs[n_r + n_p:n_r + n_p + n_d])
        grads = jax.vjp(fn, *vals)[1](cts)
        outs = refs[n_r + n_p + n_d:]
        for o_ref, g in zip(outs[:n_r], grads[:n_r]):
            o_ref[...] = g

        @pl.when(pl.program_id(0) == 0)
        def _():
            for o_ref in outs[n_r:]:
                o_ref[...] = jnp.zeros_like(o_ref)

        for o_ref, g in zip(outs[n_r:], grads[n_r:]):
            o_ref[...] += g

    in_specs = [pl.BlockSpec((tm, w), lambda i, cb=cb: (i, cb)) for (_, w, cb) in rows]
    in_specs += [pl.BlockSpec(p.shape, lambda i: (0, 0)) for p in params]
    in_specs += [pl.BlockSpec((tm, w), lambda i, cb=cb: (i, cb)) for (_, w, cb) in douts]
    out_specs = [pl.BlockSpec((tm, w), lambda i: (i, 0)) for (_, w, _) in rows]
    out_specs += [pl.BlockSpec(p.shape, lambda i: (0, 0)) for p in params]
    out_shape = [jax.ShapeDtypeStruct((R, w), F32) for (_, w, _) in rows]
    out_shape += [jax.ShapeDtypeStruct(p.shape, F32) for p in params]
    res = pl.pallas_call(
        body, name=name, grid=(R // tm,), in_specs=in_specs, out_specs=out_specs, out_shape=out_shape,
        compiler_params=_params(("arbitrary",)),
    )(*[r[0] for r in rows], *params, *[d[0] for d in douts])
    return res[:n_r], res[n_r:]


def _shift_sum(name, terms, tm=ROW_TILE):
    R = terms[0][0][0].shape[0]
    w = terms[0][0][1]
    tm = _pick(R, tm)
    nt, nb8 = R // tm, tm // 8
    shifts = [j for (_, j) in terms]

    def body(*refs):
        i = pl.program_id(0)
        acc = None
        for k, j in enumerate(shifts):
            x = refs[2 * k][...]
            if j > 0:
                halo = jnp.where(i == 0, 0.0, refs[2 * k + 1][...])
                x = pltpu.roll(jnp.concatenate([halo, x], axis=0), j, 0)[8:, :]
            elif j < 0:
                halo = jnp.where(i == nt - 1, 0.0, refs[2 * k + 1][...])
                x = pltpu.roll(jnp.concatenate([x, halo], axis=0), tm + 8 + j, 0)[:tm, :]
            acc = x if acc is None else acc + x
        refs[-1][...] = acc

    in_specs, args = [], []
    for (arr, _, cb), j in terms:
        in_specs.append(pl.BlockSpec((tm, w), lambda i, cb=cb: (i, cb)))
        if j > 0:
            in_specs.append(pl.BlockSpec((8, w), lambda i, cb=cb: (jnp.maximum(i * nb8 - 1, 0), cb)))
        else:
            in_specs.append(pl.BlockSpec((8, w), lambda i, cb=cb: (jnp.minimum((i + 1) * nb8, R // 8 - 1), cb)))
        args += [arr, arr]
    return pl.pallas_call(
        body, name=name, grid=(nt,), in_specs=in_specs, out_specs=pl.BlockSpec((tm, w), lambda i: (i, 0)),
        out_shape=jax.ShapeDtypeStruct((R, w), F32), compiler_params=_params(("parallel",)),
    )(*args)


def _softplus(x):
    return jnp.maximum(x, 0.0) + jnp.log(1.0 + jnp.exp(-jnp.abs(x)))


def _silu(x):
    return x * jax.nn.sigmoid(x)


def _rms_fn(h, w):
    return (h * lax.rsqrt(jnp.mean(h * h, axis=-1, keepdims=True) + NORM_EPS) * w,)


def _rw_prep_fn(pr, pr1, pk, pk1, pv, pv1, ps, ps1, mu_r, mu_k, mu_v, mu_s, w0, w2p, a0, a2p, k_k, k_a):
    r = pr + (pr1 - pr) * mu_r
    k = pk + (pk1 - pk) * mu_k
    v = pv + (pv1 - pv) * mu_v
    s = ps + (ps1 - ps) * mu_s
    w_log = -_softplus(-(w0 + jnp.dot(jnp.tanh(s), w2p, precision=HIGH, preferred_element_type=F32))) - 0.5
    log_decay = -jnp.exp(w_log)
    a = jax.nn.sigmoid(a0 + jnp.dot(s, a2p, precision=HIGH, preferred_element_type=F32))
    return r, log_decay, k * (1.0 + (a - 1.0) * k_a), v, k * k_k, a


def _conv_fn(u0, u1, u2, u3, w0, w1, w2, w3):
    return (_silu(u0 * w3 + u1 * w2 + u2 * w1 + u3 * w0),)


def _dn_gate_fn(ps2, a_log_b, dt_b):
    lane = lax.broadcasted_iota(jnp.int32, (DN_HEAD, DN_WIDTH), 0)
    head = jnp.right_shift(lax.broadcasted_iota(jnp.int32, (DN_HEAD, DN_WIDTH), 1), 7)
    sel_b = (lane == head).astype(F32)
    sel_a = (lane == head + DN_HEADS).astype(F32)
    b = jnp.dot(ps2, sel_b, precision=HIGHEST, preferred_element_type=F32)
    al = jnp.dot(ps2, sel_a, precision=HIGHEST, preferred_element_type=F32)
    return jax.nn.sigmoid(b), -jnp.exp(a_log_b) * _softplus(al + dt_b)


def _tril_masks(c):
    t = lax.broadcasted_iota(jnp.int32, (c, c), 0)
    s = lax.broadcasted_iota(jnp.int32, (c, c), 1)
    return s <= t, s < t


def _dot(x, y, cx, cy, prec=HIGH):
    nb = x.ndim - 2
    batch = tuple(range(nb))
    return lax.dot_general(x, y, (((cx + nb,), (cy + nb,)), (batch, batch)), precision=prec,
                           preferred_element_type=F32)


def _tri_inv(a):
    c = a.shape[-1]
    eye = (lax.broadcasted_iota(jnp.int32, (c, c), 0) == lax.broadcasted_iota(jnp.int32, (c, c), 1)).astype(F32)
    x = eye - a
    p = a
    n = 2
    while n < c:
        p = _dot(p, p, 1, 0)
        x = x + _dot(x, p, 1, 0)
        n *= 2
    return x


@jax.custom_vjp
def _tri_inv_saved(a, t):
    return t


def _tri_inv_saved_fwd(a, t):
    return t, t


def _tri_inv_saved_bwd(t, dt):
    return -_dot(_dot(t, dt, 0, 0), t, 1, 1), jnp.zeros_like(t)


_tri_inv_saved.defvjp(_tri_inv_saved_fwd, _tri_inv_saved_bwd)


def _pair_masks():
    lane = lax.broadcasted_iota(jnp.int32, (1, 2 * RW_HEAD), 1)
    m0 = (lane < RW_HEAD).astype(F32)
    return m0, 1.0 - m0


def _pair_bd(x):
    m0, m1 = _pair_masks()
    return jnp.concatenate([x * m0, x * m1], axis=-2)


def _pair_mm(x, y):
    return _dot(x, _pair_bd(y), 1, 0)


def _pair_inv(a):
    c = a.shape[-2]
    row = lax.broadcasted_iota(jnp.int32, (c, 2 * RW_HEAD), 0)
    col = lax.broadcasted_iota(jnp.int32, (c, 2 * RW_HEAD), 1) & (RW_HEAD - 1)
    x = (row == col).astype(F32) - a
    p = a
    n = 2
    while n < c:
        p = _pair_mm(p, p)
        x = x + _pair_mm(x, p)
        n *= 2
    return x


@jax.custom_vjp
def _pair_inv_saved(a, t):
    return t


def _pair_inv_saved_fwd(a, t):
    return t, t


def _pair_inv_saved_bwd(t, dt):
    m0, m1 = _pair_masks()
    c = t.shape[-2]
    z = _dot(t, dt, 0, 0)
    x = z[:, :c, :] * m0 + z[:, c:, :] * m1
    return -_dot(x, _pair_bd(t), 1, 1), jnp.zeros_like(t)


_pair_inv_saved.defvjp(_pair_inv_saved_fwd, _pair_inv_saved_bwd)


def _rw_chunk_fn(S, r, lw, k2, v, kkp, a, gate, rk, gnw, gnb, t_saved=None):
    B, C, P = r.shape
    m0, m1 = _pair_masks()
    seg = lambda x: (jnp.sum(x * m0, axis=-1, keepdims=True) * m0 + jnp.sum(x * m1, axis=-1, keepdims=True) * m1)
    mm = lambda x, y: _dot(x, y, 1, 0)
    nt = lambda x, y: _dot(x, y, 1, 1)
    tn = lambda x, y: _dot(x, y, 0, 0)
    t_idx = lax.broadcasted_iota(jnp.int32, (C, P), 0)
    s_idx = lax.broadcasted_iota(jnp.int32, (C, P), 1) & (RW_HEAD - 1)
    incl, strict = s_idx <= t_idx, s_idx < t_idx
    tril = jnp.broadcast_to(_tril_masks(C)[0].astype(F32), (B, C, C))
    kk = kkp * lax.rsqrt(seg(kkp * kkp) + 1e-6)
    b = kk * a
    g_incl = _dot(tril, lw, 1, 0, HIGHEST)
    g_excl = g_incl - lw
    inv = jnp.exp(-g_incl)
    alpha, beta, kappa, rho = kk * jnp.exp(g_excl), b * inv, k2 * inv, r * jnp.exp(g_incl)
    ar = jnp.concatenate([alpha, rho], axis=-2)
    scores = nt(ar, jnp.concatenate([_pair_bd(beta), _pair_bd(kappa)], axis=-2))
    a_ab = jnp.where(strict, scores[:, :C, :P], 0.0)
    a_ak = jnp.where(strict, scores[:, :C, P:], 0.0)
    r_b = jnp.where(incl, scores[:, C:, :P], 0.0)
    r_k = jnp.where(incl, scores[:, C:, P:], 0.0)
    t_inv = _pair_inv(a_ab) if t_saved is None else _pair_inv_saved(a_ab, t_saved)
    on_state = nt(ar, S)
    u = _pair_mm(t_inv, -on_state[:, :C, :] - _pair_mm(a_ak, v))
    y = on_state[:, C:, :] + mm(jnp.concatenate([r_b, r_k], axis=-1),
                                jnp.concatenate([_pair_bd(u), _pair_bd(v)], axis=-2))
    same_head = ((lax.broadcasted_iota(jnp.int32, (P, P), 0) < RW_HEAD)
                 == (lax.broadcasted_iota(jnp.int32, (P, P), 1) < RW_HEAD))
    fresh = tn(jnp.concatenate([u, v], axis=-2), jnp.concatenate([beta, kappa], axis=-2))
    S_new = jnp.exp(jnp.sum(lw, axis=-2, keepdims=True)) * (S + jnp.where(same_head, fresh, 0.0))
    dev = y - seg(y) * (1.0 / RW_HEAD)
    yn = dev * lax.rsqrt(seg(dev * dev) * (1.0 / RW_HEAD) + RW_GN_EPS) * gnw + gnb
    bonus = seg(r * k2 * rk) * v
    return (yn + bonus) * _silu(gate), S_new, t_inv


def _dn_chunk_fn(S, qc, kc, vc, bb, gb, z, nw, t_saved=None):
    B, C, D = qc.shape
    mm = lambda x, y: _dot(x, y, 1, 0)
    nt = lambda x, y, p=HIGH: _dot(x, y, 1, 1, p)
    tn = lambda x, y: _dot(x, y, 0, 0)
    incl, strict = _tril_masks(C)
    q = qc * lax.rsqrt(jnp.sum(qc * qc, axis=-1, keepdims=True) + 1e-6) * (D ** -0.5)
    k = kc * lax.rsqrt(jnp.sum(kc * kc, axis=-1, keepdims=True) + 1e-6)
    kb, vb = k * bb, vc * bb
    G = _dot(jnp.broadcast_to(incl.astype(F32), (B, C, C)), gb, 1, 0, HIGHEST)
    lane = lax.broadcasted_iota(jnp.int32, (C, D), 1)
    e0, e1 = (lane == 0).astype(F32), (lane == 1).astype(F32)
    diff = nt(G * e0 + e1, e0 - G * e1, HIGHEST)
    dmask = jnp.where(incl, jnp.exp(jnp.where(incl, diff, 0.0)), 0.0)
    M = jnp.where(strict, nt(kb, k) * dmask, 0.0)
    T = _tri_inv(M) if t_saved is None else _tri_inv_saved(M, t_saved)
    eG = jnp.exp(G)
    u = mm(T, vb)
    w = mm(T, kb * eG)
    attn = jnp.where(incl, nt(q, k) * dmask, 0.0)
    v_new = u - mm(w, S)
    o = mm(q * eG, S) + mm(attn, v_new)
    g_last = jnp.sum(gb, axis=-2, keepdims=True)
    S_new = S * jnp.exp(jnp.broadcast_to(g_last, S.shape)) + tn(k * jnp.exp(g_last - G), v_new)
    on = o * lax.rsqrt(jnp.mean(o * o, axis=-1, keepdims=True) + NORM_EPS) * nw
    return on * _silu(z), S_new, T


N_GROUPS = 8
GROUP = 128
HALO = 8


def _groups(x):
    return jnp.stack([x[:, g * GROUP:(g + 1) * GROUP] for g in range(N_GROUPS)])


@functools.partial(jax.custom_vjp, nondiff_argnums=(1,))
def _shift_rows(ext, j):
    return pltpu.roll(ext, j, 0)[HALO:, :]


def _shift_rows_fwd(ext, j):
    return _shift_rows(ext, j), None


def _shift_rows_bwd(j, _, d):
    z = jnp.concatenate([jnp.zeros((HALO, d.shape[1]), d.dtype), d], axis=0)
    return (pltpu.roll(z, z.shape[0] - j, 0),)


_shift_rows.defvjp(_shift_rows_fwd, _shift_rows_bwd)


def _rw_fused_fn(S, h_r, h_k, h_v, h_s, p_r, p_k, p_v, p_s, gate, *pars, t_saved=None):
    prev = lambda h, x: _shift_rows(jnp.concatenate([h, x], axis=0), 1)
    seq = _rw_prep_fn(p_r, prev(h_r, p_r), p_k, prev(h_k, p_k), p_v, prev(h_v, p_v), p_s, prev(h_s, p_s), *pars[:10])
    return _rw_chunk_fn(S, *[_groups(t) for t in seq], _groups(gate), *[_groups(t) for t in pars[10:]],
                        t_saved=t_saved)


def _dn_fused_fn(S, h_q, h_k, h_v, p_q, p_k, p_v, p_s2, z, *pars, t_saved=None):
    conv = []
    for i, (h, x) in enumerate(((h_q, p_q), (h_k, p_k), (h_v, p_v))):
        ext = jnp.concatenate([h, x], axis=0)
        conv += _conv_fn(x, _shift_rows(ext, 1), _shift_rows(ext, 2), _shift_rows(ext, 3), *pars[4 * i:4 * i + 4])
    beta_b, g_b = _dn_gate_fn(p_s2, pars[12], pars[13])
    return _dn_chunk_fn(S, *[_groups(t) for t in conv], _groups(beta_b), _groups(g_b), _groups(z), _groups(pars[14]),
                        t_saved=t_saved)


def _chunk_fwd(name, fn, rows, n_halo, pars, t_lanes):
    Lp = rows[0][0].shape[0]
    C, D, NB = CHUNK, GROUP, N_GROUPS
    nc, n_r, n_p, per = Lp // C, len(rows), len(pars), CHUNK // HALO

    def body(*refs):
        row_refs, halo_refs, par_refs = refs[:n_r], refs[n_r:n_r + n_halo], refs[n_r + n_halo:n_r + n_halo + n_p]
        y_ref, ck_ref, t_ref, s_ref = refs[n_r + n_halo + n_p:]
        first = pl.program_id(0) == 0

        @pl.when(first)
        def _():
            s_ref[...] = jnp.zeros_like(s_ref)

        S = s_ref[...]
        ck_ref[0] = S
        halos = [jnp.where(first, 0.0, r[...]) for r in halo_refs]
        y, S_new, t_inv = fn(S, *halos, *[r[...] for r in row_refs], *[r[...] for r in par_refs])
        for g in range(NB):
            y_ref[:, g * D:(g + 1) * D] = y[g]
        t_ref[0] = t_inv
        s_ref[...] = S_new

    return pl.pallas_call(
        body, name=name, grid=(nc,),
        in_specs=[pl.BlockSpec((C, w), lambda c, cb=cb: (c, cb)) for (_, w, cb) in rows]
        + [pl.BlockSpec((HALO, w), lambda c, cb=cb: (jnp.maximum(c * per - 1, 0), cb)) for (_, w, cb) in rows[:n_halo]]
        + [pl.BlockSpec(p.shape, lambda c: (0, 0)) for p in pars],
        out_specs=[pl.BlockSpec((C, NB * D), lambda c: (c, 0)),
                   pl.BlockSpec((1, NB, D, D), lambda c: (c, 0, 0, 0)),
                   pl.BlockSpec((1, NB, C, t_lanes), lambda c: (c, 0, 0, 0))],
        out_shape=[jax.ShapeDtypeStruct((Lp, NB * D), F32), jax.ShapeDtypeStruct((nc, NB, D, D), F32),
                   jax.ShapeDtypeStruct((nc, NB, C, t_lanes), F32)],
        scratch_shapes=[pltpu.VMEM((NB, D, D), F32)],
        compiler_params=_params(("arbitrary",)),
    )(*[r[0] for r in rows], *[r[0] for r in rows[:n_halo]], *pars)


def _chunk_bwd(name, fn, rows, n_halo, pars, ckpt, tinv, dy):
    Lp = rows[0][0].shape[0]
    C, D, NB = CHUNK, GROUP, N_GROUPS
    nc, n_r, n_p, per = Lp // C, len(rows), len(pars), CHUNK // HALO
    t_lanes = tinv.shape[-1]
    n_in = n_r + n_halo + n_p

    def body(*refs):
        row_refs, halo_refs, par_refs = refs[:n_r], refs[n_r:n_r + n_halo], refs[n_r + n_halo:n_in]
        ck_ref, t_ref, dy_ref = refs[n_in:n_in + 3]
        outs = refs[n_in + 3:]
        drow_refs, dpar_refs = outs[:n_r], outs[n_r:n_r + n_p]
        ds_ref, carry_refs = outs[n_r + n_p], outs[n_r + n_p + 1:]
        i = pl.program_id(0)

        @pl.when(i == 0)
        def _():
            ds_ref[...] = jnp.zeros_like(ds_ref)
            for o_ref in list(dpar_refs) + list(carry_refs):
                o_ref[...] = jnp.zeros_like(o_ref)

        halos = [jnp.where(i == nc - 1, 0.0, r[...]) for r in halo_refs]
        vals = [ck_ref[0]] + halos + [r[...] for r in row_refs] + [r[...] for r in par_refs]
        t_saved = t_ref[0]
        grads = jax.vjp(lambda *v: fn(*v, t_saved=t_saved)[:2], *vals)[1]((_groups(dy_ref), ds_ref[...]))
        ds_ref[...] = grads[0]
        d_halos, d_rows, d_pars = grads[1:1 + n_halo], grads[1 + n_halo:1 + n_halo + n_r], grads[1 + n_halo + n_r:]
        for k, (o_ref, g) in enumerate(zip(drow_refs, d_rows)):
            if k < n_halo:
                g = g + jnp.concatenate([jnp.zeros((C - HALO, g.shape[1]), F32), carry_refs[k][...]], axis=0)
                carry_refs[k][...] = d_halos[k]
            o_ref[...] = g
        for o_ref, g in zip(dpar_refs, d_pars):
            o_ref[...] += g

    rev = lambda c: nc - 1 - c
    dy_arr, dy_w, dy_cb = dy
    res = pl.pallas_call(
        body, name=name, grid=(nc,),
        in_specs=[pl.BlockSpec((C, w), lambda c, cb=cb: (rev(c), cb)) for (_, w, cb) in rows]
        + [pl.BlockSpec((HALO, w), lambda c, cb=cb: (jnp.maximum(rev(c) * per - 1, 0), cb))
           for (_, w, cb) in rows[:n_halo]]
        + [pl.BlockSpec(p.shape, lambda c: (0, 0)) for p in pars]
        + [pl.BlockSpec((1, NB, D, D), lambda c: (rev(c), 0, 0, 0)),
           pl.BlockSpec((1, NB, C, t_lanes), lambda c: (rev(c), 0, 0, 0)),
           pl.BlockSpec((C, dy_w), lambda c: (rev(c), dy_cb))],
        out_specs=[pl.BlockSpec((C, w), lambda c: (rev(c), 0)) for (_, w, _) in rows]
        + [pl.BlockSpec(p.shape, lambda c: (0, 0)) for p in pars],
        out_shape=[jax.ShapeDtypeStruct((Lp, w), F32) for (_, w, _) in rows]
        + [jax.ShapeDtypeStruct(p.shape, F32) for p in pars],
        scratch_shapes=[pltpu.VMEM((NB, D, D), F32)] + [pltpu.VMEM((HALO, w), F32) for (_, w, _) in rows[:n_halo]],
        compiler_params=_params(("arbitrary",)),
    )(*[r[0] for r in rows], *[r[0] for r in rows[:n_halo]], *pars, ckpt, tinv, dy_arr)
    return res[:n_r], res[n_r:]


def _loss_head(h, yw, tgt, fw, n_real, tm=ROW_TILE):
    Lp, Dm = h.shape
    tm = _pick(Lp, tm)

    def out_fn(z, fw_):
        return z * lax.rsqrt(jnp.mean(z * z, axis=-1, keepdims=True) + NORM_EPS) * fw_

    def body(h_ref, yw_ref, t_ref, fw_ref, loss_ref, dz_ref, dfw_ref):
        i = pl.program_id(0)

        @pl.when(i == 0)
        def _():
            loss_ref[...] = jnp.zeros_like(loss_ref)
            dfw_ref[...] = jnp.zeros_like(dfw_ref)

        row = i * tm + lax.broadcasted_iota(jnp.int32, (tm, 1), 0)
        mask = ((row >= N_META) & (row < n_real)).astype(F32)
        z = h_ref[...] + yw_ref[...]
        o, vjp = jax.vjp(out_fn, z, fw_ref[...])
        err = (o - t_ref[...]) * mask
        row_loss = 0.5 * jnp.mean(jnp.square(err), axis=-1, keepdims=True)
        dz, dfw = vjp(err * (1.0 / Dm))
        loss_ref[...] += jnp.sum(row_loss, axis=0, keepdims=True)
        dz_ref[...] = dz
        dfw_ref[...] += dfw

    row_spec = pl.BlockSpec((tm, Dm), lambda i: (i, 0))
    return pl.pallas_call(
        body, name="loss_head", grid=(Lp // tm,),
        in_specs=[row_spec, row_spec, row_spec, pl.BlockSpec((1, Dm), lambda i: (0, 0))],
        out_specs=[pl.BlockSpec((8, 128), lambda i: (0, 0)), row_spec, pl.BlockSpec((1, Dm), lambda i: (0, 0))],
        out_shape=[jax.ShapeDtypeStruct((8, 128), F32), jax.ShapeDtypeStruct((Lp, Dm), F32),
                   jax.ShapeDtypeStruct((1, Dm), F32)],
        compiler_params=_params(("arbitrary",)),
    )(h, yw, tgt, fw)


def _exchange(name, x, masks, slot_kind, per_dest, n_split=1, copy_own=True):
    n = len(masks)
    keep_own = slot_kind is not None and copy_own
    n_slots = {"chip": 4, "core": 2, "dev": 8, None: n}[slot_kind]
    blk_shape = x.shape[1:] if per_dest else x.shape
    rows = blk_shape[0] // n_split

    def body(x_ref, o_ref, send_sems, recv_sems, local_sems):
        mx, my, mc = lax.axis_index("x"), lax.axis_index("y"), lax.axis_index("c")

        def slot(k, px, py, pc):
            return {"chip": 2 * px + py, "core": pc, "dev": 4 * px + 2 * py + pc, None: k}[slot_kind]

        def peer(m):
            return (mx + m[0]) % 2, (my + m[1]) % 2, (mc + m[2]) % 2

        def part(ref, j):
            return ref.at[pl.ds(j * rows, rows)]

        own_src = x_ref.at[2 * mx + my] if per_dest else x_ref
        local = []
        if keep_own:
            own_dst = o_ref.at[slot(0, mx, my, mc)]
            local = [pltpu.make_async_copy(part(own_src, j), part(own_dst, j), local_sems.at[j])
                     for j in range(n_split)]
        for cp in local:
            cp.start()
        sends = []
        for k, m in enumerate(masks):
            px, py, pc = peer(m)
            src = x_ref.at[2 * px + py] if per_dest else x_ref
            dst = o_ref.at[slot(k, mx, my, mc)]
            for j in range(n_split):
                sends.append(pltpu.make_async_remote_copy(
                    src_ref=part(src, j), dst_ref=part(dst, j), send_sem=send_sems.at[k * n_split + j],
                    recv_sem=recv_sems.at[k * n_split + j], device_id=(px, py, pc), device_id_type=MESH))
        for cp in sends:
            cp.start()
        for k, m in enumerate(masks):
            px, py, pc = peer(m)
            landed = o_ref.at[slot(k, px, py, pc)]
            for j in range(n_split):
                pltpu.make_async_remote_copy(
                    src_ref=part(own_src, j), dst_ref=part(landed, j), send_sem=send_sems.at[k * n_split + j],
                    recv_sem=recv_sems.at[k * n_split + j], device_id=(px, py, pc), device_id_type=MESH).wait_recv()
        for cp in sends:
            cp.wait_send()
        for cp in local:
            cp.wait()

    return pl.pallas_call(
        body, name=name,
        in_specs=[pl.BlockSpec(memory_space=pl.ANY)], out_specs=pl.BlockSpec(memory_space=pl.ANY),
        out_shape=jax.ShapeDtypeStruct((n_slots,) + tuple(blk_shape), x.dtype),
        scratch_shapes=[pltpu.SemaphoreType.DMA((n * n_split,)), pltpu.SemaphoreType.DMA((n * n_split,)),
                        pltpu.SemaphoreType.DMA((n_split,))],
        compiler_params=pltpu.CompilerParams(has_side_effects=True),
    )(x)


CHIP_MASKS = [(1, 0, 0), (0, 1, 0), (1, 1, 0)]
CORE_MASKS = [(0, 0, 1)]
ALL_MASKS = [(dx, dy, dc) for dx in (0, 1) for dy in (0, 1) for dc in (0, 1) if (dx, dy, dc) != (0, 0, 0)]


def _gather_chips(name, x):
    return _exchange(name, x, CHIP_MASKS, "chip", False)


def _gather_shards(name, shard, n_split):
    half = shard.shape[0] // 2
    c, chip = lax.axis_index("c"), 2 * lax.axis_index("x") + lax.axis_index("y")
    mine = lax.dynamic_slice_in_dim(shard, c * half, half, axis=0)
    by_chip = _exchange(name + "_chips", mine, CHIP_MASKS, "chip", False, n_split, copy_own=False)
    by_chip = lax.dynamic_update_index_in_dim(by_chip, mine, chip, 0)
    both = _exchange(name + "_cores", by_chip, CORE_MASKS, "core", False, N_CHIPS, copy_own=False)
    return lax.dynamic_update_index_in_dim(both, by_chip, c, 0)


def _sum_slots(name, x, tr=128):
    S, R, N = x.shape
    tr = _pick(R, tr, 16)

    def body(x_ref, o_ref):
        acc = x_ref[0].astype(F32)
        for s in range(1, S):
            acc = acc + x_ref[s].astype(F32)
        o_ref[...] = acc

    return pl.pallas_call(
        body, name=name, grid=(R // tr,),
        in_specs=[pl.BlockSpec((S, tr, N), lambda i: (0, i, 0))], out_specs=pl.BlockSpec((tr, N), lambda i: (i, 0)),
        out_shape=jax.ShapeDtypeStruct((R, N), F32), compiler_params=_params(("parallel",)),
    )(x)


def _add_to_bf16(name, a, b, tr=128):
    S, R, N = a.shape
    tr = _pick(R, tr, 16)

    def body(a_ref, b_ref, o_ref):
        o_ref[...] = (a_ref[...] + b_ref[...]).astype(BF16)

    spec = pl.BlockSpec((S, tr, N), lambda i: (0, i, 0))
    return pl.pallas_call(
        body, name=name, grid=(R // tr,), in_specs=[spec, spec], out_specs=spec,
        out_shape=jax.ShapeDtypeStruct((S, R, N), BF16), compiler_params=_params(("parallel",)),
    )(a, b)


def _adamw(name, gparts, w, m, v, tr=128):
    S, R, N = gparts.shape
    tr = _pick(R, tr)
    c1 = 1.0 / (1.0 - ADAM_B1 ** ADAM_STEP)
    c2 = 1.0 / (1.0 - ADAM_B2 ** ADAM_STEP)

    def body(g_ref, w_ref, m_ref, v_ref, go_ref, d_ref, mo_ref, vo_ref):
        g = g_ref[0]
        for s in range(1, S):
            g = g + g_ref[s]
        m_new = ADAM_B1 * m_ref[...] + (1.0 - ADAM_B1) * g
        v_new = ADAM_B2 * v_ref[...] + (1.0 - ADAM_B2) * jnp.square(g)
        go_ref[...] = g
        mo_ref[...] = m_new
        vo_ref[...] = v_new
        d_ref[...] = -ADAM_LR * ((m_new * c1) / (jnp.sqrt(v_new * c2) + ADAM_EPS) + ADAM_WD * w_ref[...])

    spec = pl.BlockSpec((tr, N), lambda i: (i, 0))
    return pl.pallas_call(
        body, name=name, grid=(R // tr,),
        in_specs=[pl.BlockSpec((S, tr, N), lambda i: (0, i, 0)), spec, spec, spec], out_specs=[spec] * 4,
        out_shape=[jax.ShapeDtypeStruct((R, N), F32)] * 4, compiler_params=_params(("parallel",)),
    )(gparts, w, m, v)


def _reduce_to_shard(name, slabs, n_split):
    _, R, N = slabs.shape
    half = R // 2
    c, chip = lax.axis_index("c"), 2 * lax.axis_index("x") + lax.axis_index("y")
    halves = slabs.reshape(N_CHIPS, 2, half, N)
    mine = lax.dynamic_index_in_dim(halves, c, axis=1, keepdims=False)
    theirs = lax.dynamic_index_in_dim(halves, 1 - c, axis=1, keepdims=False)
    from_sibling = _exchange(name + "_sib", theirs, CORE_MASKS, None, False, N_CHIPS)[0]
    wire = _add_to_bf16(name + "_add", mine, from_sibling)
    got = _exchange(name + "_chips", wire, CHIP_MASKS, "chip", True, n_split, copy_own=False)
    got = lax.dynamic_update_index_in_dim(got, lax.dynamic_index_in_dim(wire, chip, 0, keepdims=False), chip, 0)
    part = _sum_slots(name + "_sum", got)
    both = _exchange(name + "_cores", part, CORE_MASKS, "core", False, n_split, copy_own=False)
    return lax.dynamic_update_index_in_dim(both, part, c, 0).reshape(1, R, N)


def _pack(pieces, cols, row_mult=8):
    flat = jnp.concatenate([p.reshape(-1) for p in pieces])
    rows = -(-flat.shape[0] // cols)
    rows = -(-rows // row_mult) * row_mult
    return jnp.pad(flat, (0, rows * cols - flat.shape[0])).reshape(rows, cols)


def _unpack(packed, shapes):
    flat = packed.reshape(-1)
    out, off = [], 0
    for shp in shapes:
        n = 1
        for d in shp:
            n *= d
        out.append(flat[off:off + n].reshape(shp))
        off += n
    return out


def _to_my_layout(w):
    z = jnp.zeros((w.shape[0], 112), w.dtype)
    return jnp.concatenate([w[:, 0:3072], w[:, 3200:4224], w[:, 4224:7296], w[:, 7312:8336],
                            w[:, 3072:3200], w[:, 7296:7312], z], axis=1)


def _from_my_layout(w):
    return jnp.concatenate([w[:, 0:3072], w[:, 8192:8320], w[:, 3072:4096], w[:, 4096:7168],
                            w[:, 8320:8336], w[:, 7168:8192]], axis=1)


def kernel(x, meta_tokens, norm_w, w_in, rw_shift_mu, rw_w0, rw_w2, rw_a0, rw_a2, rw_k_k, rw_k_a, rw_r_k, rw_gn_w, rw_gn_b, dn_conv_w, dn_A_log, dn_dt_bias, dn_norm_w, w_out, final_norm_w, loss_target, m_meta_tokens, m_norm_w, m_w_in, m_rw_shift_mu, m_rw_w0, m_rw_w2, m_rw_a0, m_rw_a2, m_rw_k_k, m_rw_k_a, m_rw_r_k, m_rw_gn_w, m_rw_gn_b, m_dn_conv_w, m_dn_A_log, m_dn_dt_bias, m_dn_norm_w, m_w_out, m_final_norm_w, v_meta_tokens, v_norm_w, v_w_in, v_rw_shift_mu, v_rw_w0, v_rw_w2, v_rw_a0, v_rw_a2, v_rw_k_k, v_rw_k_a, v_rw_r_k, v_rw_gn_w, v_rw_gn_b, v_dn_conv_w, v_dn_A_log, v_dn_dt_bias, v_dn_norm_w, v_w_out, v_final_norm_w):
    S = x.shape[1]
    L = N_META + S
    Lp = -(-L // CHUNK) * CHUNK

    small_shapes = [(RW_LORA, 256), (RW_LORA, 256), (CONV_W, 768), (N_META, 512)]
    small_mine = _pack([rw_w2[0], rw_a2[0], dn_conv_w[0], meta_tokens], 1024)
    small_all = _gather_chips("gather_small", small_mine)
    per_chip = [_unpack(small_all[s], small_shapes) for s in range(N_CHIPS)]
    w2, a2, conv_w, meta = [jnp.concatenate([per_chip[s][i] for s in range(N_CHIPS)], axis=1) for i in range(4)]
    w_in_all = _gather_shards("gather_w_in", w_in[0].astype(BF16), 8)
    W = _to_my_layout(jnp.transpose(w_in_all, (0, 2, 1, 3)).reshape(D_MODEL, IN_COLS))
    w_out_all = _gather_shards("gather_w_out", w_out[0].astype(BF16), 8)
    Wo = jnp.transpose(w_out_all, (1, 0, 2, 3)).reshape(D_MODEL, D_MODEL)

    tail = [jnp.zeros((Lp - L, D_MODEL), F32)] if Lp > L else []
    h = jnp.concatenate([meta, x[0]] + tail, axis=0)
    tgt = jnp.concatenate([jnp.zeros((N_META, D_MODEL), F32), loss_target[0]] + tail, axis=0)
    (u,) = _rowwise("rms_in", _rms_fn, [_row(h)], [norm_w], [D_MODEL])
    p = _mm("in_proj", u, W, "nn")

    mu = rw_shift_mu
    zpad = jnp.zeros((RW_LORA, RW_WIDTH), F32)
    rw_params = [mu[:, 0:1024], mu[:, 1024:2048], mu[:, 2048:3072], mu[:, 3072:3200], rw_w0,
                 jnp.concatenate([w2, zpad], axis=0), rw_a0, jnp.concatenate([zpad, a2], axis=0), rw_k_k, rw_k_a]
    rw_rows = [_row(p, 1024, CB_R), _row(p, 1024, CB_K), _row(p, 1024, CB_V), _row(p, 128, CB_S1),
               _row(p, 1024, CB_GATE)]
    rw_pars = rw_params + [rw_r_k, rw_gn_w, rw_gn_b]
    ya, rw_ck, rw_t = _chunk_fwd("rw_chunk_fwd", _rw_fused_fn, rw_rows, 4, rw_pars, GROUP)

    dn_rows = [_row(p, 1024, CB_DQ), _row(p, 1024, CB_DK), _row(p, 1024, CB_DV), _row(p, 128, CB_S2),
               _row(p, 1024, CB_Z)]
    dn_pars = [conv_w[j:j + 1, 1024 * i:1024 * (i + 1)] for i in range(3) for j in range(CONV_W)]
    dn_pars += [jnp.repeat(dn_A_log, DN_HEAD, axis=1), jnp.repeat(dn_dt_bias, DN_HEAD, axis=1),
                jnp.tile(dn_norm_w, (1, DN_HEADS))]
    yb, dn_ck, dn_t = _chunk_fwd("dn_chunk_fwd", _dn_fused_fn, dn_rows, 3, dn_pars, CHUNK)

    y = jnp.concatenate([ya, yb], axis=1)
    yw = _mm("out_proj", y, Wo, "nn", tn=1024)
    loss_acc, dz, d_fw = _loss_head(h, yw, tgt, final_norm_w.reshape(1, D_MODEL), L)
    loss = lax.psum(loss_acc[0, 0], ("x", "y", "c"))

    dy = _mm("d_out_proj", dz, Wo, "nt", tn=1024, tk=2048)
    d_wo = _mm("d_w_out", y, dz, "tn", tm=1024, tn=1024, tk=MM_ROW_TILE)

    dp_rw, d_rw_pars = _chunk_bwd("rw_chunk_bwd", _rw_fused_fn, rw_rows, 4, rw_pars, rw_ck, rw_t, _row(dy, 1024, 0))
    d_prep_pars, d_rw_pars = d_rw_pars[:10], d_rw_pars[10:]
    dp_dn, d_dn_pars = _chunk_bwd("dn_chunk_bwd", _dn_fused_fn, dn_rows, 3, dn_pars, dn_ck, dn_t, _row(dy, 1024, 1))
    d_conv_parts = [jnp.concatenate(d_dn_pars[4 * i:4 * i + 4], axis=0) for i in range(3)]
    d_a_log_b, d_dt_b = d_dn_pars[12], d_dn_pars[13]
    d_dn_nw = jnp.sum(d_dn_pars[14].reshape(DN_HEADS, DN_HEAD), axis=0, keepdims=True)

    dp = jnp.concatenate(list(dp_rw[:3]) + [dp_rw[4]] + list(dp_dn[:3]) + [dp_dn[4], dp_rw[3], dp_dn[3]], axis=1)
    du = _mm("d_in_proj", dp, W, "nt", tn=1024, tk=1408)
    d_W = _mm("d_w_in", u, dp, "tn", tm=1024, tn=1408, tk=MM_ROW_TILE)
    (dh1,), (d_norm_w,) = _rowwise_bwd("rms_in_bwd", _rms_fn, [_row(h)], [norm_w], [_row(du)])
    (dh,) = _rowwise("dh_sum", lambda a_, b_: (a_ + b_,), [_row(dz), _row(dh1)], [], [D_MODEL])
    grad_x = dh[N_META:L][None]

    d_mu = jnp.concatenate(d_prep_pars[0:4], axis=1)
    d_w2, d_a2 = d_prep_pars[5][:RW_LORA], d_prep_pars[7][RW_LORA:]
    d_conv = jnp.concatenate(d_conv_parts, axis=1)
    d_meta = dh[:N_META]
    head_sum = lambda t: jnp.sum(t.reshape(1, DN_HEADS, DN_HEAD), axis=-1)
    rep_names = ["norm_w", "rw_shift_mu", "rw_w0", "rw_a0", "rw_k_k", "rw_k_a", "rw_r_k", "rw_gn_w", "rw_gn_b",
                 "dn_A_log", "dn_dt_bias", "dn_norm_w", "final_norm_w"]
    rep_g = [d_norm_w, d_mu, d_prep_pars[4], d_prep_pars[6], d_prep_pars[8], d_prep_pars[9],
             d_rw_pars[0], d_rw_pars[1], d_rw_pars[2],
             head_sum(d_a_log_b), head_sum(d_dt_b), d_dn_nw, d_fw.reshape(D_MODEL)]
    rep_w = [norm_w, rw_shift_mu, rw_w0, rw_a0, rw_k_k, rw_k_a, rw_r_k, rw_gn_w, rw_gn_b, dn_A_log, dn_dt_bias,
             dn_norm_w, final_norm_w]
    rep_m = [m_norm_w, m_rw_shift_mu, m_rw_w0, m_rw_a0, m_rw_k_k, m_rw_k_a, m_rw_r_k, m_rw_gn_w, m_rw_gn_b,
             m_dn_A_log, m_dn_dt_bias, m_dn_norm_w, m_final_norm_w]
    rep_v = [v_norm_w, v_rw_shift_mu, v_rw_w0, v_rw_a0, v_rw_k_k, v_rw_k_a, v_rw_r_k, v_rw_gn_w, v_rw_gn_b,
             v_dn_A_log, v_dn_dt_bias, v_dn_norm_w, v_final_norm_w]
    rep_shapes = [t.shape for t in rep_w]
    rep_all = _exchange("gather_rep_grads", _pack(rep_g, 128), ALL_MASKS, "dev", False)
    rep_out = _adamw("adam_rep", rep_all, _pack(rep_w, 128), _pack(rep_m, 128), _pack(rep_v, 128))
    rep_out = [dict(zip(rep_names, _unpack(t, rep_shapes))) for t in rep_out]

    sm_slabs = jnp.stack([_pack([d_w2[:, 256 * s:256 * (s + 1)], d_a2[:, 256 * s:256 * (s + 1)],
                                 d_conv[:, 768 * s:768 * (s + 1)], d_meta[:, 512 * s:512 * (s + 1)]], 1024, 64)
                          for s in range(N_CHIPS)])
    sm_parts = _reduce_to_shard("rs_small", sm_slabs, 1)
    sm_w = [rw_w2[0], rw_a2[0], dn_conv_w[0], meta_tokens]
    sm_m = [m_rw_w2[0], m_rw_a2[0], m_dn_conv_w[0], m_meta_tokens]
    sm_v = [v_rw_w2[0], v_rw_a2[0], v_dn_conv_w[0], v_meta_tokens]
    sm_out = _adamw("adam_small", sm_parts, _pack(sm_w, 1024, 64), _pack(sm_m, 1024, 64), _pack(sm_v, 1024, 64))
    sm_names = ["rw_w2", "rw_a2", "dn_conv_w", "meta_tokens"]
    sm_full_shapes = [(1, RW_LORA, 256), (1, RW_LORA, 256), (1, CONV_W, 768), (N_META, 512)]
    sm_out = [dict(zip(sm_names, [t.reshape(shp) for t, shp in zip(_unpack(o, small_shapes), sm_full_shapes)]))
              for o in sm_out]

    wo_parts = _reduce_to_shard("rs_w_out", d_wo.reshape(N_CHIPS, D_MODEL // N_CHIPS, D_MODEL), 8)
    wo_out = _adamw("adam_w_out", wo_parts, w_out[0], m_w_out[0], v_w_out[0])
    d_w_in_slabs = jnp.transpose(_from_my_layout(d_W).reshape(D_MODEL, N_CHIPS, SHARD_COLS), (1, 0, 2))
    wi_parts = _reduce_to_shard("rs_w_in", d_w_in_slabs, 8)
    wi_out = _adamw("adam_w_in", wi_parts, w_in[0], m_w_in[0], v_w_in[0])

    order = ["meta_tokens", "norm_w", "w_in", "rw_shift_mu", "rw_w0", "rw_w2", "rw_a0", "rw_a2", "rw_k_k", "rw_k_a",
             "rw_r_k", "rw_gn_w", "rw_gn_b", "dn_conv_w", "dn_A_log", "dn_dt_bias", "dn_norm_w", "w_out",
             "final_norm_w"]
    outs = [loss, grad_x]
    for kind in range(4):
        table = dict(rep_out[kind])
        table.update(sm_out[kind])
        table["w_in"] = wi_out[kind][None]
        table["w_out"] = wo_out[kind][None]
        outs += [table[n] for n in order]
    return tuple(outs)
```

```python
import functools

import jax
import jax.numpy as jnp
from jax import lax
from jax.experimental import pallas as pl
from jax.experimental.pallas import tpu as pltpu

F32 = jnp.float32
BF16 = jnp.bfloat16
HIGH = lax.Precision.HIGH
HIGHEST = lax.Precision.HIGHEST
MESH = pl.DeviceIdType.MESH

D_MODEL = 2048
N_META = 16
RW_WIDTH = 1024
RW_HEAD = 64
RW_HEADS = 16
RW_LORA = 64
RW_GN_EPS = 64e-5
DN_WIDTH = 1024
DN_HEAD = 128
DN_HEADS = 8
CONV_W = 4
CHUNK = 64
NORM_EPS = 1e-6
IN_COLS = 8336
N_CHIPS = 4
SHARD_COLS = IN_COLS // N_CHIPS

NP_COLS = 8 * 1024 + 256
CB_R, CB_K, CB_V, CB_GATE, CB_DQ, CB_DK, CB_DV, CB_Z = range(8)
CB_S1 = 8192 // 128
CB_S2 = CB_S1 + 1

ADAM_LR = 0.001
ADAM_B1 = 0.9
ADAM_B2 = 0.999
ADAM_EPS = 1e-08
ADAM_WD = 0.01
ADAM_STEP = 10

VMEM_LIMIT_BYTES = 56 * 1024 * 1024
ROW_TILE = 104
MM_ROW_TILE = 832


def _params(sem=None):
    return pltpu.CompilerParams(dimension_semantics=sem, vmem_limit_bytes=VMEM_LIMIT_BYTES)


def _pick(n, target, mult=8):
    best = None
    for d in range(mult, min(n, target) + 1, mult):
        if n % d == 0:
            best = d
    return n if best is None else best


def _mm(name, a, b, mode, tm=MM_ROW_TILE, tn=1408, tk=2048):
    if mode == "nn":
        (M, K), (_, N) = a.shape, b.shape
    elif mode == "nt":
        (M, K), (N, _) = a.shape, b.shape
    else:
        (K, M), (_, N) = a.shape, b.shape
    tm = _pick(M, tm, 128 if mode == "tn" else 16)
    tn = _pick(N, tn, 128)
    tk = _pick(K, tk, 8 if mode == "tn" else 128)
    if mode == "nn":
        a_spec = pl.BlockSpec((tm, tk), lambda i, j, k: (i, k))
        b_spec = pl.BlockSpec((tk, tn), lambda i, j, k: (k, j))
        dims = (((1,), (0,)), ((), ()))
    elif mode == "nt":
        a_spec = pl.BlockSpec((tm, tk), lambda i, j, k: (i, k))
        b_spec = pl.BlockSpec((tn, tk), lambda i, j, k: (j, k))
        dims = (((1,), (1,)), ((), ()))
    else:
        a_spec = pl.BlockSpec((tk, tm), lambda i, j, k: (k, i))
        b_spec = pl.BlockSpec((tk, tn), lambda i, j, k: (k, j))
        dims = (((0,), (0,)), ((), ()))

    def body(a_ref, b_ref, o_ref):
        @pl.when(pl.program_id(2) == 0)
        def _():
            o_ref[...] = jnp.zeros_like(o_ref)

        o_ref[...] += lax.dot_general(a_ref[...].astype(BF16), b_ref[...].astype(BF16), dims,
                                      preferred_element_type=F32)

    return pl.pallas_call(
        body, name=name, grid=(M // tm, N // tn, K // tk),
        in_specs=[a_spec, b_spec], out_specs=pl.BlockSpec((tm, tn), lambda i, j, k: (i, j)),
        out_shape=jax.ShapeDtypeStruct((M, N), F32),
        compiler_params=_params(("parallel", "parallel", "arbitrary")),
    )(a, b)


def _row(arr, width=None, cb=0):
    return (arr, arr.shape[1] if width is None else width, cb)


def _rowwise(name, fn, rows, params, out_widths, tm=ROW_TILE, out_dtype=F32):
    R = rows[0][0].shape[0]
    tm = _pick(R, tm, 16 if out_dtype == BF16 else 8)
    n_r, n_p = len(rows), len(params)

    def body(*refs):
        vals = [r[...] for r in refs[:n_r + n_p]]
        for o_ref, val in zip(refs[n_r + n_p:], fn(*vals)):
            o_ref[...] = val.astype(out_dtype)

    in_specs = [pl.BlockSpec((tm, w), lambda i, cb=cb: (i, cb)) for (_, w, cb) in rows]
    in_specs += [pl.BlockSpec(p.shape, lambda i: (0, 0)) for p in params]
    return pl.pallas_call(
        body, name=name, grid=(R // tm,), in_specs=in_specs,
        out_specs=[pl.BlockSpec((tm, w), lambda i: (i, 0)) for w in out_widths],
        out_shape=[jax.ShapeDtypeStruct((R, w), out_dtype) for w in out_widths],
        compiler_params=_params(("parallel",)),
    )(*[r[0] for r in rows], *params)


def _rowwise_bwd(name, fn, rows, params, douts, tm=ROW_TILE):
    R = rows[0][0].shape[0]
    tm = _pick(R, tm)
    n_r, n_p, n_d = len(rows), len(params), len(douts)

    def body(*refs):
        vals = [r[...] for r in refs[:n_r + n_p]]
        cts = tuple(r[...] for r in refs[n_r + n_p:n_r + n_p + n_d])
        grads = jax.vjp(fn, *vals)[1](cts)
        outs = refs[n_r + n_p + n_d:]
        for o_ref, g in zip(outs[:n_r], grads[:n_r]):
            o_ref[...] = g

        @pl.when(pl.program_id(0) == 0)
        def _():
            for o_ref in outs[n_r:]:
                o_ref[...] = jnp.zeros_like(o_ref)

        for o_ref, g in zip(outs[n_r:], grads[n_r:]):
            o_ref[...] += g

    in_specs = [pl.BlockSpec((tm, w), lambda i, cb=cb: (i, cb)) for (_, w, cb) in rows]
    in_specs += [pl.BlockSpec(p.shape, lambda i: (0, 0)) for p in params]
    in_specs += [pl.BlockSpec((tm, w), lambda i, cb=cb: (i, cb)) for (_, w, cb) in douts]
    out_specs = [pl.BlockSpec((tm, w), lambda i: (i, 0)) for (_, w, _) in rows]
    out_specs += [pl.BlockSpec(p.shape, lambda i: (0, 0)) for p in params]
    out_shape = [jax.ShapeDtypeStruct((R, w), F32) for (_, w, _) in rows]
    out_shape += [jax.ShapeDtypeStruct(p.shape, F32) for p in params]
    res = pl.pallas_call(
        body, name=name, grid=(R // tm,), in_specs=in_specs, out_specs=out_specs, out_shape=out_shape,
        compiler_params=_params(("arbitrary",)),
    )(*[r[0] for r in rows], *params, *[d[0] for d in douts])
    return res[:n_r], res[n_r:]


def _shift_sum(name, terms, tm=ROW_TILE):
    R = terms[0][0][0].shape[0]
    w = terms[0][0][1]
    tm = _pick(R, tm)
    nt, nb8 = R // tm, tm // 8
    shifts = [j for (_, j) in terms]

    def body(*refs):
        i = pl.program_id(0)
        acc = None
        for k, j in enumerate(shifts):
            x = refs[2 * k][...]
            if j > 0:
                halo = jnp.where(i == 0, 0.0, refs[2 * k + 1][...])
                x = pltpu.roll(jnp.concatenate([halo, x], axis=0), j, 0)[8:, :]
            elif j < 0:
                halo = jnp.where(i == nt - 1, 0.0, refs[2 * k + 1][...])
                x = pltpu.roll(jnp.concatenate([x, halo], axis=0), tm + 8 + j, 0)[:tm, :]
            acc = x if acc is None else acc + x
        refs[-1][...] = acc

    in_specs, args = [], []
    for (arr, _, cb), j in terms:
        in_specs.append(pl.BlockSpec((tm, w), lambda i, cb=cb: (i, cb)))
        if j > 0:
            in_specs.append(pl.BlockSpec((8, w), lambda i, cb=cb: (jnp.maximum(i * nb8 - 1, 0), cb)))
        else:
            in_specs.append(pl.BlockSpec((8, w), lambda i, cb=cb: (jnp.minimum((i + 1) * nb8, R // 8 - 1), cb)))
        args += [arr, arr]
    return pl.pallas_call(
        body, name=name, grid=(nt,), in_specs=in_specs, out_specs=pl.BlockSpec((tm, w), lambda i: (i, 0)),
        out_shape=jax.ShapeDtypeStruct((R, w), F32), compiler_params=_params(("parallel",)),
    )(*args)


def _softplus(x):
    return jnp.maximum(x, 0.0) + jnp.log(1.0 + jnp.exp(-jnp.abs(x)))


def _silu(x):
    return x * jax.nn.sigmoid(x)


def _rms_fn(h, w):
    return (h * lax.rsqrt(jnp.mean(h * h, axis=-1, keepdims=True) + NORM_EPS) * w,)


def _rw_prep_fn(pr, pr1, pk, pk1, pv, pv1, ps, ps1, mu_r, mu_k, mu_v, mu_s, w0, w2p, a0, a2p, k_k, k_a):
    r = pr + (pr1 - pr) * mu_r
    k = pk + (pk1 - pk) * mu_k
    v = pv + (pv1 - pv) * mu_v
    s = ps + (ps1 - ps) * mu_s
    w_log = -_softplus(-(w0 + jnp.dot(jnp.tanh(s), w2p, precision=HIGH, preferred_element_type=F32))) - 0.5
    log_decay = -jnp.exp(w_log)
    a = jax.nn.sigmoid(a0 + jnp.dot(s, a2p, precision=HIGH, preferred_element_type=F32))
    return r, log_decay, k * (1.0 + (a - 1.0) * k_a), v, k * k_k, a


def _conv_fn(u0, u1, u2, u3, w0, w1, w2, w3):
    return (_silu(u0 * w3 + u1 * w2 + u2 * w1 + u3 * w0),)


def _dn_gate_fn(ps2, a_log_n, dt_n):
    beta_n = jax.nn.sigmoid(ps2)
    g_n = -jnp.exp(a_log_n) * _softplus(ps2 + dt_n)
    lane = lax.broadcasted_iota(jnp.int32, (1, DN_HEAD), 1)
    pick = lambda x, j: jnp.broadcast_to(jnp.sum(x * (lane == j).astype(F32), axis=-1, keepdims=True), x.shape)[None]
    beta = jnp.concatenate([pick(beta_n, h) for h in range(DN_HEADS)], axis=0)
    g = jnp.concatenate([pick(g_n, DN_HEADS + h) for h in range(DN_HEADS)], axis=0)
    return beta, g


def _tril_masks(c):
    t = lax.broadcasted_iota(jnp.int32, (c, c), 0)
    s = lax.broadcasted_iota(jnp.int32, (c, c), 1)
    return s <= t, s < t


def _dot(x, y, cx, cy, prec=HIGH):
    nb = x.ndim - 2
    batch = tuple(range(nb))
    return lax.dot_general(x, y, (((cx + nb,), (cy + nb,)), (batch, batch)), precision=prec,
                           preferred_element_type=F32)


def _tri_inv(a):
    c = a.shape[-1]
    eye = (lax.broadcasted_iota(jnp.int32, (c, c), 0) == lax.broadcasted_iota(jnp.int32, (c, c), 1)).astype(F32)
    x = eye - a
    p = a
    n = 2
    while n < c:
        p = _dot(p, p, 1, 0)
        x = x + _dot(x, p, 1, 0)
        n *= 2
    return x


@jax.custom_vjp
def _tri_inv_saved(a, t):
    return t


def _tri_inv_saved_fwd(a, t):
    return t, t


def _tri_inv_saved_bwd(t, dt):
    return -_dot(_dot(t, dt, 0, 0), t, 1, 1), jnp.zeros_like(t)


_tri_inv_saved.defvjp(_tri_inv_saved_fwd, _tri_inv_saved_bwd)


def _pair_masks():
    lane = lax.broadcasted_iota(jnp.int32, (1, 2 * RW_HEAD), 1)
    m0 = (lane < RW_HEAD).astype(F32)
    return m0, 1.0 - m0


def _pair_bd(x):
    m0, m1 = _pair_masks()
    return jnp.concatenate([x * m0, x * m1], axis=-2)


def _pair_mm(x, y):
    return _dot(x, _pair_bd(y), 1, 0)


def _pair_inv(a):
    c = a.shape[-2]
    row = lax.broadcasted_iota(jnp.int32, (c, 2 * RW_HEAD), 0)
    col = lax.broadcasted_iota(jnp.int32, (c, 2 * RW_HEAD), 1) & (RW_HEAD - 1)
    x = (row == col).astype(F32) - a
    p = a
    n = 2
    while n < c:
        p = _pair_mm(p, p)
        x = x + _pair_mm(x, p)
        n *= 2
    return x


@jax.custom_vjp
def _pair_inv_saved(a, t):
    return t


def _pair_inv_saved_fwd(a, t):
    return t, t


def _pair_inv_saved_bwd(t, dt):
    m0, m1 = _pair_masks()
    c = t.shape[-2]
    z = _dot(t, dt, 0, 0)
    x = z[:, :c, :] * m0 + z[:, c:, :] * m1
    return -_dot(x, _pair_bd(t), 1, 1), jnp.zeros_like(t)


_pair_inv_saved.defvjp(_pair_inv_saved_fwd, _pair_inv_saved_bwd)


def _rw_chunk_fn(S, r, lw, k2, v, kkp, a, gate, rk, gnw, gnb, t_saved=None):
    B, C, P = r.shape
    m0, m1 = _pair_masks()
    seg = lambda x: (jnp.sum(x * m0, axis=-1, keepdims=True) * m0 + jnp.sum(x * m1, axis=-1, keepdims=True) * m1)
    mm = lambda x, y: _dot(x.astype(BF16), y.astype(BF16), 1, 0, None)
    nt = lambda x, y: _dot(x, y, 1, 1)
    tn = lambda x, y: _dot(x, y, 0, 0)
    pair_mm = lambda x, y: mm(x, _pair_bd(y))
    t_idx = lax.broadcasted_iota(jnp.int32, (C, P), 0)
    s_idx = lax.broadcasted_iota(jnp.int32, (C, P), 1) & (RW_HEAD - 1)
    incl, strict = s_idx <= t_idx, s_idx < t_idx
    tril = jnp.broadcast_to(_tril_masks(C)[0].astype(F32), (B, C, C))
    kk = kkp * lax.rsqrt(seg(kkp * kkp) + 1e-6)
    b = kk * a
    g_incl = _dot(tril, lw, 1, 0, HIGHEST)
    g_excl = g_incl - lw
    inv = jnp.exp(-g_incl)
    alpha, beta, kappa, rho = kk * jnp.exp(g_excl), b * inv, k2 * inv, r * jnp.exp(g_incl)
    ar = jnp.concatenate([alpha, rho], axis=-2)
    scores = nt(ar, jnp.concatenate([_pair_bd(beta), _pair_bd(kappa)], axis=-2))
    a_ab = jnp.where(strict, scores[:, :C, :P], 0.0)
    a_ak = jnp.where(strict, scores[:, :C, P:], 0.0)
    r_b = jnp.where(incl, scores[:, C:, :P], 0.0)
    r_k = jnp.where(incl, scores[:, C:, P:], 0.0)
    t_inv = _pair_inv(a_ab) if t_saved is None else _pair_inv_saved(a_ab, t_saved)
    on_state = nt(ar, S)
    u = pair_mm(t_inv, -on_state[:, :C, :] - pair_mm(a_ak, v))
    y = on_state[:, C:, :] + mm(jnp.concatenate([r_b, r_k], axis=-1),
                                jnp.concatenate([_pair_bd(u), _pair_bd(v)], axis=-2))
    same_head = ((lax.broadcasted_iota(jnp.int32, (P, P), 0) < RW_HEAD)
                 == (lax.broadcasted_iota(jnp.int32, (P, P), 1) < RW_HEAD))
    fresh = tn(jnp.concatenate([u, v], axis=-2), jnp.concatenate([beta, kappa], axis=-2))
    S_new = jnp.exp(jnp.sum(lw, axis=-2, keepdims=True)) * (S + jnp.where(same_head, fresh, 0.0))
    dev = y - seg(y) * (1.0 / RW_HEAD)
    yn = dev * lax.rsqrt(seg(dev * dev) * (1.0 / RW_HEAD) + RW_GN_EPS) * gnw + gnb
    bonus = seg(r * k2 * rk) * v
    return (yn + bonus) * _silu(gate), S_new, t_inv


def _dn_chunk_fn(S, qc, kc, vc, bb, gb, z, nw, t_saved=None):
    B, C, D = qc.shape
    mm = lambda x, y: _dot(x.astype(BF16), y.astype(BF16), 1, 0, None)
    nt = lambda x, y, p=None: _dot(x, y, 1, 1, p) if p else _dot(x.astype(BF16), y.astype(BF16), 1, 1, None)
    tn = lambda x, y: _dot(x.astype(BF16), y.astype(BF16), 0, 0, None)
    incl, strict = _tril_masks(C)
    q = qc * lax.rsqrt(jnp.sum(qc * qc, axis=-1, keepdims=True) + 1e-6) * (D ** -0.5)
    k = kc * lax.rsqrt(jnp.sum(kc * kc, axis=-1, keepdims=True) + 1e-6)
    kb, vb = k * bb, vc * bb
    G = _dot(jnp.broadcast_to(incl.astype(F32), (B, C, C)), gb, 1, 0, HIGHEST)
    lane = lax.broadcasted_iota(jnp.int32, (C, D), 1)
    e0, e1 = (lane == 0).astype(F32), (lane == 1).astype(F32)
    diff = nt(G * e0 + e1, e0 - G * e1, HIGHEST)
    dmask = jnp.where(incl, jnp.exp(jnp.where(incl, diff, 0.0)), 0.0)
    M = jnp.where(strict, nt(kb, k) * dmask, 0.0)
    T = _tri_inv(M) if t_saved is None else _tri_inv_saved(M, t_saved)
    eG = jnp.exp(G)
    u = mm(T, vb)
    w = mm(T, kb * eG)
    attn = jnp.where(incl, nt(q, k) * dmask, 0.0)
    v_new = u - mm(w, S)
    o = mm(q * eG, S) + mm(attn, v_new)
    g_last = jnp.sum(gb, axis=-2, keepdims=True)
    S_new = S * jnp.exp(jnp.broadcast_to(g_last, S.shape)) + tn(k * jnp.exp(g_last - G), v_new)
    on = o * lax.rsqrt(jnp.mean(o * o, axis=-1, keepdims=True) + NORM_EPS) * nw
    return on * _silu(z), S_new, T


N_GROUPS = 8
GROUP = 128
HALO = 8


def _groups(x):
    return jnp.concatenate([x[:, g * GROUP:(g + 1) * GROUP][None] for g in range(N_GROUPS)], axis=0)


@functools.partial(jax.custom_vjp, nondiff_argnums=(1,))
def _shift_rows(ext, j):
    return pltpu.roll(ext, j, 0)[HALO:, :]


def _shift_rows_fwd(ext, j):
    return _shift_rows(ext, j), None


def _shift_rows_bwd(j, _, d):
    z = jnp.concatenate([jnp.zeros((HALO, d.shape[1]), d.dtype), d], axis=0)
    return (pltpu.roll(z, z.shape[0] - j, 0),)


_shift_rows.defvjp(_shift_rows_fwd, _shift_rows_bwd)


def _rw_fused_fn(S, h_r, h_k, h_v, h_s, p_r, p_k, p_v, p_s, gate, *pars, t_saved=None):
    prev = lambda h, x: _shift_rows(jnp.concatenate([h, x], axis=0), 1)
    seq = _rw_prep_fn(p_r, prev(h_r, p_r), p_k, prev(h_k, p_k), p_v, prev(h_v, p_v), p_s, prev(h_s, p_s), *pars[:10])
    return _rw_chunk_fn(S, *[_groups(t) for t in seq], _groups(gate), *[_groups(t) for t in pars[10:]],
                        t_saved=t_saved)


def _dn_fused_fn(S, h_q, h_k, h_v, p_q, p_k, p_v, p_s2, z, *pars, t_saved=None):
    conv = []
    for i, (h, x) in enumerate(((h_q, p_q), (h_k, p_k), (h_v, p_v))):
        ext = jnp.concatenate([h, x], axis=0)
        conv += _conv_fn(x, _shift_rows(ext, 1), _shift_rows(ext, 2), _shift_rows(ext, 3), *pars[4 * i:4 * i + 4])
    beta, g = _dn_gate_fn(p_s2, pars[12], pars[13])
    return _dn_chunk_fn(S, *[_groups(t) for t in conv], beta, g, _groups(z), _groups(pars[14]), t_saved=t_saved)


def _chunk_fwd(name, fn, rows, n_halo, pars, t_lanes):
    Lp = rows[0][0].shape[0]
    C, D, NB = CHUNK, GROUP, N_GROUPS
    nc, n_r, n_p, per = Lp // C, len(rows), len(pars), CHUNK // HALO

    def body(*refs):
        row_refs, halo_refs, par_refs = refs[:n_r], refs[n_r:n_r + n_halo], refs[n_r + n_halo:n_r + n_halo + n_p]
        y_ref, ck_ref, t_ref, s_ref = refs[n_r + n_halo + n_p:]
        first = pl.program_id(0) == 0

        @pl.when(first)
        def _():
            s_ref[...] = jnp.zeros_like(s_ref)

        S = s_ref[...]
        ck_ref[0] = S
        halos = [jnp.where(first, 0.0, r[...]) for r in halo_refs]
        y, S_new, t_inv = fn(S, *halos, *[r[...] for r in row_refs], *[r[...] for r in par_refs])
        for g in range(NB):
            y_ref[:, g * D:(g + 1) * D] = y[g].astype(BF16)
        t_ref[0] = t_inv
        s_ref[...] = S_new

    return pl.pallas_call(
        body, name=name, grid=(nc,),
        in_specs=[pl.BlockSpec((C, w), lambda c, cb=cb: (c, cb)) for (_, w, cb) in rows]
        + [pl.BlockSpec((HALO, w), lambda c, cb=cb: (jnp.maximum(c * per - 1, 0), cb)) for (_, w, cb) in rows[:n_halo]]
        + [pl.BlockSpec(p.shape, lambda c: (0, 0)) for p in pars],
        out_specs=[pl.BlockSpec((C, NB * D), lambda c: (c, 0)),
                   pl.BlockSpec((1, NB, D, D), lambda c: (c, 0, 0, 0)),
                   pl.BlockSpec((1, NB, C, t_lanes), lambda c: (c, 0, 0, 0))],
        out_shape=[jax.ShapeDtypeStruct((Lp, NB * D), BF16), jax.ShapeDtypeStruct((nc, NB, D, D), F32),
                   jax.ShapeDtypeStruct((nc, NB, C, t_lanes), F32)],
        scratch_shapes=[pltpu.VMEM((NB, D, D), F32)],
        compiler_params=_params(("arbitrary",)),
    )(*[r[0] for r in rows], *[r[0] for r in rows[:n_halo]], *pars)


def _chunk_bwd(name, fn, rows, n_halo, pars, ckpt, tinv, dy):
    Lp = rows[0][0].shape[0]
    C, D, NB = CHUNK, GROUP, N_GROUPS
    nc, n_r, n_p, per = Lp // C, len(rows), len(pars), CHUNK // HALO
    t_lanes = tinv.shape[-1]
    n_in = n_r + n_halo + n_p

    def body(*refs):
        row_refs, halo_refs, par_refs = refs[:n_r], refs[n_r:n_r + n_halo], refs[n_r + n_halo:n_in]
        ck_ref, t_ref, dy_ref = refs[n_in:n_in + 3]
        outs = refs[n_in + 3:]
        drow_refs, dpar_refs = outs[:n_r], outs[n_r:n_r + n_p]
        ds_ref, carry_refs = outs[n_r + n_p], outs[n_r + n_p + 1:]
        i = pl.program_id(0)

        @pl.when(i == 0)
        def _():
            ds_ref[...] = jnp.zeros_like(ds_ref)
            for o_ref in list(dpar_refs) + list(carry_refs):
                o_ref[...] = jnp.zeros_like(o_ref)

        halos = [jnp.where(i == nc - 1, 0.0, r[...]) for r in halo_refs]
        vals = [ck_ref[0]] + halos + [r[...] for r in row_refs] + [r[...] for r in par_refs]
        t_saved = t_ref[0]
        grads = jax.vjp(lambda *v: fn(*v, t_saved=t_saved)[:2], *vals)[1]((_groups(dy_ref), ds_ref[...]))
        ds_ref[...] = grads[0]
        d_halos, d_rows, d_pars = grads[1:1 + n_halo], grads[1 + n_halo:1 + n_halo + n_r], grads[1 + n_halo + n_r:]
        for k, (o_ref, g) in enumerate(zip(drow_refs, d_rows)):
            if k < n_halo:
                g = g + jnp.concatenate([jnp.zeros((C - HALO, g.shape[1]), F32), carry_refs[k][...]], axis=0)
                carry_refs[k][...] = d_halos[k]
            o_ref[...] = g.astype(BF16)
        for o_ref, g in zip(dpar_refs, d_pars):
            o_ref[...] += g

    rev = lambda c: nc - 1 - c
    dy_arr, dy_w, dy_cb = dy
    res = pl.pallas_call(
        body, name=name, grid=(nc,),
        in_specs=[pl.BlockSpec((C, w), lambda c, cb=cb: (rev(c), cb)) for (_, w, cb) in rows]
        + [pl.BlockSpec((HALO, w), lambda c, cb=cb: (jnp.maximum(rev(c) * per - 1, 0), cb))
           for (_, w, cb) in rows[:n_halo]]
        + [pl.BlockSpec(p.shape, lambda c: (0, 0)) for p in pars]
        + [pl.BlockSpec((1, NB, D, D), lambda c: (rev(c), 0, 0, 0)),
           pl.BlockSpec((1, NB, C, t_lanes), lambda c: (rev(c), 0, 0, 0)),
           pl.BlockSpec((C, dy_w), lambda c: (rev(c), dy_cb))],
        out_specs=[pl.BlockSpec((C, w), lambda c: (rev(c), 0)) for (_, w, _) in rows]
        + [pl.BlockSpec(p.shape, lambda c: (0, 0)) for p in pars],
        out_shape=[jax.ShapeDtypeStruct((Lp, w), BF16) for (_, w, _) in rows]
        + [jax.ShapeDtypeStruct(p.shape, F32) for p in pars],
        scratch_shapes=[pltpu.VMEM((NB, D, D), F32)] + [pltpu.VMEM((HALO, w), F32) for (_, w, _) in rows[:n_halo]],
        compiler_params=_params(("arbitrary",)),
    )(*[r[0] for r in rows], *[r[0] for r in rows[:n_halo]], *pars, ckpt, tinv, dy_arr)
    return res[:n_r], res[n_r:]


def _loss_head(h, yw, tgt, fw, n_real, tm=2 * ROW_TILE):
    Lp, Dm = h.shape
    tm = _pick(Lp, tm, 16)

    def out_fn(z, fw_):
        return z * lax.rsqrt(jnp.mean(z * z, axis=-1, keepdims=True) + NORM_EPS) * fw_

    def body(h_ref, yw_ref, t_ref, fw_ref, loss_ref, dz_ref, dz16_ref, dfw_ref):
        i = pl.program_id(0)

        @pl.when(i == 0)
        def _():
            loss_ref[...] = jnp.zeros_like(loss_ref)
            dfw_ref[...] = jnp.zeros_like(dfw_ref)

        row = i * tm + lax.broadcasted_iota(jnp.int32, (tm, 1), 0)
        mask = ((row >= N_META) & (row < n_real)).astype(F32)
        z = h_ref[...] + yw_ref[...]
        o, vjp = jax.vjp(out_fn, z, fw_ref[...])
        err = (o - t_ref[...]) * mask
        row_loss = 0.5 * jnp.mean(jnp.square(err), axis=-1, keepdims=True)
        dz, dfw = vjp(err * (1.0 / Dm))
        loss_ref[...] += jnp.sum(row_loss, axis=0, keepdims=True)
        dz_ref[...] = dz
        dz16_ref[...] = dz.astype(BF16)
        dfw_ref[...] += dfw

    row_spec = pl.BlockSpec((tm, Dm), lambda i: (i, 0))
    return pl.pallas_call(
        body, name="loss_head", grid=(Lp // tm,),
        in_specs=[row_spec, row_spec, row_spec, pl.BlockSpec((1, Dm), lambda i: (0, 0))],
        out_specs=[pl.BlockSpec((8, 128), lambda i: (0, 0)), row_spec, row_spec,
                   pl.BlockSpec((1, Dm), lambda i: (0, 0))],
        out_shape=[jax.ShapeDtypeStruct((8, 128), F32), jax.ShapeDtypeStruct((Lp, Dm), F32),
                   jax.ShapeDtypeStruct((Lp, Dm), BF16), jax.ShapeDtypeStruct((1, Dm), F32)],
        compiler_params=_params(("arbitrary",)),
    )(h, yw, tgt, fw)


def _exchange(name, x, masks, slot_kind, per_dest, n_split=1, copy_own=True):
    n = len(masks)
    keep_own = slot_kind is not None and copy_own
    n_slots = {"chip": 4, "core": 2, "dev": 8, None: n}[slot_kind]
    blk_shape = x.shape[1:] if per_dest else x.shape
    rows = blk_shape[0] // n_split

    def body(x_ref, o_ref, send_sems, recv_sems, local_sems):
        mx, my, mc = lax.axis_index("x"), lax.axis_index("y"), lax.axis_index("c")

        def slot(k, px, py, pc):
            return {"chip": 2 * px + py, "core": pc, "dev": 4 * px + 2 * py + pc, None: k}[slot_kind]

        def peer(m):
            return (mx + m[0]) % 2, (my + m[1]) % 2, (mc + m[2]) % 2

        def part(ref, j):
            return ref.at[pl.ds(j * rows, rows)]

        own_src = x_ref.at[2 * mx + my] if per_dest else x_ref
        local = []
        if keep_own:
            own_dst = o_ref.at[slot(0, mx, my, mc)]
            local = [pltpu.make_async_copy(part(own_src, j), part(own_dst, j), local_sems.at[j])
                     for j in range(n_split)]
        for cp in local:
            cp.start()
        sends = []
        for k, m in enumerate(masks):
            px, py, pc = peer(m)
            src = x_ref.at[2 * px + py] if per_dest else x_ref
            dst = o_ref.at[slot(k, mx, my, mc)]
            for j in range(n_split):
                sends.append(pltpu.make_async_remote_copy(
                    src_ref=part(src, j), dst_ref=part(dst, j), send_sem=send_sems.at[k * n_split + j],
                    recv_sem=recv_sems.at[k * n_split + j], device_id=(px, py, pc), device_id_type=MESH))
        for cp in sends:
            cp.start()
        for k, m in enumerate(masks):
            px, py, pc = peer(m)
            landed = o_ref.at[slot(k, px, py, pc)]
            for j in range(n_split):
                pltpu.make_async_remote_copy(
                    src_ref=part(own_src, j), dst_ref=part(landed, j), send_sem=send_sems.at[k * n_split + j],
                    recv_sem=recv_sems.at[k * n_split + j], device_id=(px, py, pc), device_id_type=MESH).wait_recv()
        for cp in sends:
            cp.wait_send()
        for cp in local:
            cp.wait()

    return pl.pallas_call(
        body, name=name,
        in_specs=[pl.BlockSpec(memory_space=pl.ANY)], out_specs=pl.BlockSpec(memory_space=pl.ANY),
        out_shape=jax.ShapeDtypeStruct((n_slots,) + tuple(blk_shape), x.dtype),
        scratch_shapes=[pltpu.SemaphoreType.DMA((n * n_split,)), pltpu.SemaphoreType.DMA((n * n_split,)),
                        pltpu.SemaphoreType.DMA((n_split,))],
        compiler_params=pltpu.CompilerParams(has_side_effects=True),
    )(x)


CHIP_MASKS = [(1, 0, 0), (0, 1, 0), (1, 1, 0)]
CORE_MASKS = [(0, 0, 1)]
ALL_MASKS = [(dx, dy, dc) for dx in (0, 1) for dy in (0, 1) for dc in (0, 1) if (dx, dy, dc) != (0, 0, 0)]


def _gather_chips(name, x):
    return _exchange(name, x, CHIP_MASKS, "chip", False)


def _gather_shards(name, shard, n_split):
    half = shard.shape[0] // 2
    c, chip = lax.axis_index("c"), 2 * lax.axis_index("x") + lax.axis_index("y")
    mine = lax.dynamic_slice_in_dim(shard, c * half, half, axis=0)
    by_chip = _exchange(name + "_chips", mine, CHIP_MASKS, "chip", False, n_split, copy_own=False)
    by_chip = lax.dynamic_update_index_in_dim(by_chip, mine, chip, 0)
    both = _exchange(name + "_cores", by_chip, CORE_MASKS, "core", False, N_CHIPS, copy_own=False)
    return lax.dynamic_update_index_in_dim(both, by_chip, c, 0)


def _sum_slots(name, x, tr=128):
    S, R, N = x.shape
    tr = _pick(R, tr, 16)

    def body(x_ref, o_ref):
        acc = x_ref[0].astype(F32)
        for s in range(1, S):
            acc = acc + x_ref[s].astype(F32)
        o_ref[...] = acc

    return pl.pallas_call(
        body, name=name, grid=(R // tr,),
        in_specs=[pl.BlockSpec((S, tr, N), lambda i: (0, i, 0))], out_specs=pl.BlockSpec((tr, N), lambda i: (i, 0)),
        out_shape=jax.ShapeDtypeStruct((R, N), F32), compiler_params=_params(("parallel",)),
    )(x)


def _add_to_bf16(name, a, b, tr=128):
    S, R, N = a.shape
    tr = _pick(R, tr, 16)

    def body(a_ref, b_ref, o_ref):
        o_ref[...] = (a_ref[...] + b_ref[...]).astype(BF16)

    spec = pl.BlockSpec((S, tr, N), lambda i: (0, i, 0))
    return pl.pallas_call(
        body, name=name, grid=(R // tr,), in_specs=[spec, spec], out_specs=spec,
        out_shape=jax.ShapeDtypeStruct((S, R, N), BF16), compiler_params=_params(("parallel",)),
    )(a, b)


def _adamw(name, gparts, w, m, v, tr=128):
    S, R, N = gparts.shape
    tr = _pick(R, tr)
    c1 = 1.0 / (1.0 - ADAM_B1 ** ADAM_STEP)
    c2 = 1.0 / (1.0 - ADAM_B2 ** ADAM_STEP)

    def body(g_ref, w_ref, m_ref, v_ref, go_ref, d_ref, mo_ref, vo_ref):
        g = g_ref[0]
        for s in range(1, S):
            g = g + g_ref[s]
        m_new = ADAM_B1 * m_ref[...] + (1.0 - ADAM_B1) * g
        v_new = ADAM_B2 * v_ref[...] + (1.0 - ADAM_B2) * jnp.square(g)
        go_ref[...] = g
        mo_ref[...] = m_new
        vo_ref[...] = v_new
        d_ref[...] = -ADAM_LR * ((m_new * c1) / (jnp.sqrt(v_new * c2) + ADAM_EPS) + ADAM_WD * w_ref[...])

    spec = pl.BlockSpec((tr, N), lambda i: (i, 0))
    return pl.pallas_call(
        body, name=name, grid=(R // tr,),
        in_specs=[pl.BlockSpec((S, tr, N), lambda i: (0, i, 0)), spec, spec, spec], out_specs=[spec] * 4,
        out_shape=[jax.ShapeDtypeStruct((R, N), F32)] * 4, compiler_params=_params(("parallel",)),
    )(gparts, w, m, v)


def _reduce_to_shard(name, slabs, n_split):
    _, R, N = slabs.shape
    half = R // 2
    c, chip = lax.axis_index("c"), 2 * lax.axis_index("x") + lax.axis_index("y")
    halves = slabs.reshape(N_CHIPS, 2, half, N)
    mine = lax.dynamic_index_in_dim(halves, c, axis=1, keepdims=False)
    theirs = lax.dynamic_index_in_dim(halves, 1 - c, axis=1, keepdims=False)
    from_sibling = _exchange(name + "_sib", theirs, CORE_MASKS, None, False, N_CHIPS)[0]
    wire = _add_to_bf16(name + "_add", mine, from_sibling)
    got = _exchange(name + "_chips", wire, CHIP_MASKS, "chip", True, n_split, copy_own=False)
    got = lax.dynamic_update_index_in_dim(got, lax.dynamic_index_in_dim(wire, chip, 0, keepdims=False), chip, 0)
    part = _sum_slots(name + "_sum", got)
    both = _exchange(name + "_cores", part, CORE_MASKS, "core", False, n_split, copy_own=False)
    return lax.dynamic_update_index_in_dim(both, part, c, 0).reshape(1, R, N)


def _pack(pieces, cols, row_mult=8):
    flat = jnp.concatenate([p.reshape(-1) for p in pieces])
    rows = -(-flat.shape[0] // cols)
    rows = -(-rows // row_mult) * row_mult
    return jnp.pad(flat, (0, rows * cols - flat.shape[0])).reshape(rows, cols)


def _unpack(packed, shapes):
    flat = packed.reshape(-1)
    out, off = [], 0
    for shp in shapes:
        n = 1
        for d in shp:
            n *= d
        out.append(flat[off:off + n].reshape(shp))
        off += n
    return out


def _to_my_layout(w):
    z = jnp.zeros((w.shape[0], 112), w.dtype)
    return jnp.concatenate([w[:, 0:3072], w[:, 3200:4224], w[:, 4224:7296], w[:, 7312:8336],
                            w[:, 3072:3200], w[:, 7296:7312], z], axis=1)


def _from_my_layout(w):
    return jnp.concatenate([w[:, 0:3072], w[:, 8192:8320], w[:, 3072:4096], w[:, 4096:7168],
                            w[:, 8320:8336], w[:, 7168:8192]], axis=1)


def kernel(x, meta_tokens, norm_w, w_in, rw_shift_mu, rw_w0, rw_w2, rw_a0, rw_a2, rw_k_k, rw_k_a, rw_r_k, rw_gn_w, rw_gn_b, dn_conv_w, dn_A_log, dn_dt_bias, dn_norm_w, w_out, final_norm_w, loss_target, m_meta_tokens, m_norm_w, m_w_in, m_rw_shift_mu, m_rw_w0, m_rw_w2, m_rw_a0, m_rw_a2, m_rw_k_k, m_rw_k_a, m_rw_r_k, m_rw_gn_w, m_rw_gn_b, m_dn_conv_w, m_dn_A_log, m_dn_dt_bias, m_dn_norm_w, m_w_out, m_final_norm_w, v_meta_tokens, v_norm_w, v_w_in, v_rw_shift_mu, v_rw_w0, v_rw_w2, v_rw_a0, v_rw_a2, v_rw_k_k, v_rw_k_a, v_rw_r_k, v_rw_gn_w, v_rw_gn_b, v_dn_conv_w, v_dn_A_log, v_dn_dt_bias, v_dn_norm_w, v_w_out, v_final_norm_w):
    S = x.shape[1]
    L = N_META + S
    Lp = -(-L // CHUNK) * CHUNK

    small_shapes = [(RW_LORA, 256), (RW_LORA, 256), (CONV_W, 768), (N_META, 512)]
    small_mine = _pack([rw_w2[0], rw_a2[0], dn_conv_w[0], meta_tokens], 1024)
    small_all = _gather_chips("gather_small", small_mine)
    per_chip = [_unpack(small_all[s], small_shapes) for s in range(N_CHIPS)]
    w2, a2, conv_w, meta = [jnp.concatenate([per_chip[s][i] for s in range(N_CHIPS)], axis=1) for i in range(4)]
    w_in_all = _gather_shards("gather_w_in", w_in[0].astype(BF16), 8)
    W = _to_my_layout(jnp.transpose(w_in_all, (0, 2, 1, 3)).reshape(D_MODEL, IN_COLS))
    w_out_all = _gather_shards("gather_w_out", w_out[0].astype(BF16), 8)
    Wo = jnp.transpose(w_out_all, (1, 0, 2, 3)).reshape(D_MODEL, D_MODEL)

    tail = [jnp.zeros((Lp - L, D_MODEL), F32)] if Lp > L else []
    h = jnp.concatenate([meta, x[0]] + tail, axis=0)
    tgt = jnp.concatenate([jnp.zeros((N_META, D_MODEL), F32), loss_target[0]] + tail, axis=0)
    (u,) = _rowwise("rms_in", _rms_fn, [_row(h)], [norm_w], [D_MODEL], tm=2 * ROW_TILE, out_dtype=BF16)
    p = _mm("in_proj", u, W, "nn")

    mu = rw_shift_mu
    zpad = jnp.zeros((RW_LORA, RW_WIDTH), F32)
    rw_params = [mu[:, 0:1024], mu[:, 1024:2048], mu[:, 2048:3072], mu[:, 3072:3200], rw_w0,
                 jnp.concatenate([w2, zpad], axis=0), rw_a0, jnp.concatenate([zpad, a2], axis=0), rw_k_k, rw_k_a]
    rw_rows = [_row(p, 1024, CB_R), _row(p, 1024, CB_K), _row(p, 1024, CB_V), _row(p, 128, CB_S1),
               _row(p, 1024, CB_GATE)]
    rw_pars = rw_params + [rw_r_k, rw_gn_w, rw_gn_b]
    ya, rw_ck, rw_t = _chunk_fwd("rw_chunk_fwd", _rw_fused_fn, rw_rows, 4, rw_pars, GROUP)

    dn_rows = [_row(p, 1024, CB_DQ), _row(p, 1024, CB_DK), _row(p, 1024, CB_DV), _row(p, 128, CB_S2),
               _row(p, 1024, CB_Z)]
    dn_pars = [conv_w[j:j + 1, 1024 * i:1024 * (i + 1)] for i in range(3) for j in range(CONV_W)]
    narrow = lambda t: jnp.pad(t, ((0, 0), (DN_HEADS, DN_HEAD - 2 * DN_HEADS)))
    dn_pars += [narrow(dn_A_log), narrow(dn_dt_bias), jnp.tile(dn_norm_w, (1, DN_HEADS))]
    yb, dn_ck, dn_t = _chunk_fwd("dn_chunk_fwd", _dn_fused_fn, dn_rows, 3, dn_pars, CHUNK)

    y = jnp.concatenate([ya, yb], axis=1)
    yw = _mm("out_proj", y, Wo, "nn", tn=1024)
    loss_acc, dz, dz16, d_fw = _loss_head(h, yw, tgt, final_norm_w.reshape(1, D_MODEL), L)
    loss = lax.psum(loss_acc[0, 0], ("x", "y", "c"))

    dy = _mm("d_out_proj", dz16, Wo.T, "nn", tn=1024, tk=2048)
    d_wo = _mm("d_w_out", y.T, dz16, "nn", tm=512, tn=1024, tk=Lp)

    dp_rw, d_rw_pars = _chunk_bwd("rw_chunk_bwd", _rw_fused_fn, rw_rows, 4, rw_pars, rw_ck, rw_t, _row(dy, 1024, 0))
    d_prep_pars, d_rw_pars = d_rw_pars[:10], d_rw_pars[10:]
    dp_dn, d_dn_pars = _chunk_bwd("dn_chunk_bwd", _dn_fused_fn, dn_rows, 3, dn_pars, dn_ck, dn_t, _row(dy, 1024, 1))
    d_conv_parts = [jnp.concatenate(d_dn_pars[4 * i:4 * i + 4], axis=0) for i in range(3)]
    d_a_log_b, d_dt_b = d_dn_pars[12], d_dn_pars[13]
    d_dn_nw = jnp.sum(d_dn_pars[14].reshape(DN_HEADS, DN_HEAD), axis=0, keepdims=True)

    dp = jnp.concatenate(list(dp_rw[:3]) + [dp_rw[4]] + list(dp_dn[:3]) + [dp_dn[4], dp_rw[3], dp_dn[3]], axis=1)
    du = _mm("d_in_proj", dp, W.T, "nn", tn=2048, tk=1408)
    d_W = _mm("d_w_in", u.T, dp, "nn", tm=1024, tn=768, tk=Lp)
    (dh,), (d_norm_w,) = _rowwise_bwd("rms_in_bwd", lambda h_, w_: (_rms_fn(h_, w_)[0], h_), [_row(h)], [norm_w],
                                      [_row(du), _row(dz)])
    grad_x = dh[N_META:L][None]

    d_mu = jnp.concatenate(d_prep_pars[0:4], axis=1)
    d_w2, d_a2 = d_prep_pars[5][:RW_LORA], d_prep_pars[7][RW_LORA:]
    d_conv = jnp.concatenate(d_conv_parts, axis=1)
    d_meta = dh[:N_META]
    head_sum = lambda t: t[:, DN_HEADS:2 * DN_HEADS]
    rep_names = ["norm_w", "rw_shift_mu", "rw_w0", "rw_a0", "rw_k_k", "rw_k_a", "rw_r_k", "rw_gn_w", "rw_gn_b",
                 "dn_A_log", "dn_dt_bias", "dn_norm_w", "final_norm_w"]
    rep_g = [d_norm_w, d_mu, d_prep_pars[4], d_prep_pars[6], d_prep_pars[8], d_prep_pars[9],
             d_rw_pars[0], d_rw_pars[1], d_rw_pars[2],
             head_sum(d_a_log_b), head_sum(d_dt_b), d_dn_nw, d_fw.reshape(D_MODEL)]
    rep_w = [norm_w, rw_shift_mu, rw_w0, rw_a0, rw_k_k, rw_k_a, rw_r_k, rw_gn_w, rw_gn_b, dn_A_log, dn_dt_bias,
             dn_norm_w, final_norm_w]
    rep_m = [m_norm_w, m_rw_shift_mu, m_rw_w0, m_rw_a0, m_rw_k_k, m_rw_k_a, m_rw_r_k, m_rw_gn_w, m_rw_gn_b,
             m_dn_A_log, m_dn_dt_bias, m_dn_norm_w, m_final_norm_w]
    rep_v = [v_norm_w, v_rw_shift_mu, v_rw_w0, v_rw_a0, v_rw_k_k, v_rw_k_a, v_rw_r_k, v_rw_gn_w, v_rw_gn_b,
             v_dn_A_log, v_dn_dt_bias, v_dn_norm_w, v_final_norm_w]
    rep_shapes = [t.shape for t in rep_w]
    rep_all = _exchange("gather_rep_grads", _pack(rep_g, 128), ALL_MASKS, "dev", False)
    rep_out = _adamw("adam_rep", rep_all, _pack(rep_w, 128), _pack(rep_m, 128), _pack(rep_v, 128))
    rep_out = [dict(zip(rep_names, _unpack(t, rep_shapes))) for t in rep_out]

    sm_slabs = jnp.stack([_pack([d_w2[:, 256 * s:256 * (s + 1)], d_a2[:, 256 * s:256 * (s + 1)],
                                 d_conv[:, 768 * s:768 * (s + 1)], d_meta[:, 512 * s:512 * (s + 1)]], 1024, 64)
                          for s in range(N_CHIPS)])
    sm_parts = _reduce_to_shard("rs_small", sm_slabs, 1)
    sm_w = [rw_w2[0], rw_a2[0], dn_conv_w[0], meta_tokens]
    sm_m = [m_rw_w2[0], m_rw_a2[0], m_dn_conv_w[0], m_meta_tokens]
    sm_v = [v_rw_w2[0], v_rw_a2[0], v_dn_conv_w[0], v_meta_tokens]
    sm_out = _adamw("adam_small", sm_parts, _pack(sm_w, 1024, 64), _pack(sm_m, 1024, 64), _pack(sm_v, 1024, 64))
    sm_names = ["rw_w2", "rw_a2", "dn_conv_w", "meta_tokens"]
    sm_full_shapes = [(1, RW_LORA, 256), (1, RW_LORA, 256), (1, CONV_W, 768), (N_META, 512)]
    sm_out = [dict(zip(sm_names, [t.reshape(shp) for t, shp in zip(_unpack(o, small_shapes), sm_full_shapes)]))
              for o in sm_out]

    wo_parts = _reduce_to_shard("rs_w_out", d_wo.reshape(N_CHIPS, D_MODEL // N_CHIPS, D_MODEL), 8)
    wo_out = _adamw("adam_w_out", wo_parts, w_out[0], m_w_out[0], v_w_out[0])
    d_w_in_slabs = jnp.transpose(_from_my_layout(d_W).reshape(D_MODEL, N_CHIPS, SHARD_COLS), (1, 0, 2))
    wi_parts = _reduce_to_shard("rs_w_in", d_w_in_slabs, 8)
    wi_out = _adamw("adam_w_in", wi_parts, w_in[0], m_w_in[0], v_w_in[0])

    order = ["meta_tokens", "norm_w", "w_in", "rw_shift_mu", "rw_w0", "rw_w2", "rw_a0", "rw_a2", "rw_k_k", "rw_k_a",
             "rw_r_k", "rw_gn_w", "rw_gn_b", "dn_conv_w", "dn_A_log", "dn_dt_bias", "dn_norm_w", "w_out",
             "final_norm_w"]
    outs = [loss, grad_x]
    for kind in range(4):
        table = dict(rep_out[kind])
        table.update(sm_out[kind])
        table["w_in"] = wi_out[kind][None]
        table["w_out"] = wo_out[kind][None]
        outs += [table[n] for n in order]
    return tuple(outs)
```

```python
import functools

import jax
import jax.numpy as jnp
from jax import lax
from jax.experimental import pallas as pl
from jax.experimental.pallas import tpu as pltpu

F32 = jnp.float32
BF16 = jnp.bfloat16
HIGH = lax.Precision.HIGH
HIGHEST = lax.Precision.HIGHEST
MESH = pl.DeviceIdType.MESH

D_MODEL = 2048
N_META = 16
RW_WIDTH = 1024
RW_HEAD = 64
RW_HEADS = 16
RW_LORA = 64
RW_GN_EPS = 64e-5
DN_WIDTH = 1024
DN_HEAD = 128
DN_HEADS = 8
CONV_W = 4
CHUNK = 64
NORM_EPS = 1e-6
IN_COLS = 8336
N_CHIPS = 4
SHARD_COLS = IN_COLS // N_CHIPS

NP_COLS = 8 * 1024 + 256
CB_R, CB_K, CB_V, CB_GATE, CB_DQ, CB_DK, CB_DV, CB_Z = range(8)
CB_S1 = 8192 // 128
CB_S2 = CB_S1 + 1

ADAM_LR = 0.001
ADAM_B1 = 0.9
ADAM_B2 = 0.999
ADAM_EPS = 1e-08
ADAM_WD = 0.01
ADAM_STEP = 10

VMEM_LIMIT_BYTES = 56 * 1024 * 1024
ROW_TILE = 104
MM_ROW_TILE = 832


def _params(sem=None):
    return pltpu.CompilerParams(dimension_semantics=sem, vmem_limit_bytes=VMEM_LIMIT_BYTES)


def _pick(n, target, mult=8):
    best = None
    for d in range(mult, min(n, target) + 1, mult):
        if n % d == 0:
            best = d
    return n if best is None else best


def _mm(name, a, b, mode, tm=MM_ROW_TILE, tn=1408, tk=2048):
    if mode == "nn":
        (M, K), (_, N) = a.shape, b.shape
    elif mode == "nt":
        (M, K), (N, _) = a.shape, b.shape
    else:
        (K, M), (_, N) = a.shape, b.shape
    tm = _pick(M, tm, 128 if mode == "tn" else 16)
    tn = _pick(N, tn, 128)
    tk = _pick(K, tk, 8 if mode == "tn" else 128)
    if mode == "nn":
        a_spec = pl.BlockSpec((tm, tk), lambda i, j, k: (i, k))
        b_spec = pl.BlockSpec((tk, tn), lambda i, j, k: (k, j))
        dims = (((1,), (0,)), ((), ()))
    elif mode == "nt":
        a_spec = pl.BlockSpec((tm, tk), lambda i, j, k: (i, k))
        b_spec = pl.BlockSpec((tn, tk), lambda i, j, k: (j, k))
        dims = (((1,), (1,)), ((), ()))
    else:
        a_spec = pl.BlockSpec((tk, tm), lambda i, j, k: (k, i))
        b_spec = pl.BlockSpec((tk, tn), lambda i, j, k: (k, j))
        dims = (((0,), (0,)), ((), ()))

    def body(a_ref, b_ref, o_ref):
        @pl.when(pl.program_id(2) == 0)
        def _():
            o_ref[...] = jnp.zeros_like(o_ref)

        o_ref[...] += lax.dot_general(a_ref[...].astype(BF16), b_ref[...].astype(BF16), dims,
                                      preferred_element_type=F32)

    return pl.pallas_call(
        body, name=name, grid=(M // tm, N // tn, K // tk),
        in_specs=[a_spec, b_spec], out_specs=pl.BlockSpec((tm, tn), lambda i, j, k: (i, j)),
        out_shape=jax.ShapeDtypeStruct((M, N), F32),
        compiler_params=_params(("parallel", "parallel", "arbitrary")),
    )(a, b)


def _row(arr, width=None, cb=0):
    return (arr, arr.shape[1] if width is None else width, cb)


def _rowwise(name, fn, rows, params, out_widths, tm=ROW_TILE, out_dtype=F32):
    R = rows[0][0].shape[0]
    tm = _pick(R, tm, 16 if out_dtype == BF16 else 8)
    n_r, n_p = len(rows), len(params)

    def body(*refs):
        vals = [r[...] for r in refs[:n_r + n_p]]
        for o_ref, val in zip(refs[n_r + n_p:], fn(*vals)):
            o_ref[...] = val.astype(out_dtype)

    in_specs = [pl.BlockSpec((tm, w), lambda i, cb=cb: (i, cb)) for (_, w, cb) in rows]
    in_specs += [pl.BlockSpec(p.shape, lambda i: (0, 0)) for p in params]
    return pl.pallas_call(
        body, name=name, grid=(R // tm,), in_specs=in_specs,
        out_specs=[pl.BlockSpec((tm, w), lambda i: (i, 0)) for w in out_widths],
        out_shape=[jax.ShapeDtypeStruct((R, w), out_dtype) for w in out_widths],
        compiler_params=_params(("parallel",)),
    )(*[r[0] for r in rows], *params)


def _rowwise_bwd(name, fn, rows, params, douts, tm=ROW_TILE):
    R = rows[0][0].shape[0]
    tm = _pick(R, tm)
    n_r, n_p, n_d = len(rows), len(params), len(douts)

    def body(*refs):
        vals = [r[...] for r in refs[:n_r + n_p]]
        cts = tuple(r[...] for r in refs[n_r + n_p:n_r + n_p + n_d])
        grads = jax.vjp(fn, *vals)[1](cts)
        outs = refs[n_r + n_p + n_d:]
        for o_ref, g in zip(outs[:n_r], grads[:n_r]):
            o_ref[...] = g

        @pl.when(pl.program_id(0) == 0)
        def _():
            for o_ref in outs[n_r:]:
                o_ref[...] = jnp.zeros_like(o_ref)

        for o_ref, g in zip(outs[n_r:], grads[n_r:]):
            o_ref[...] += g

    in_specs = [pl.BlockSpec((tm, w), lambda i, cb=cb: (i, cb)) for (_, w, cb) in rows]
    in_specs += [pl.BlockSpec(p.shape, lambda i: (0, 0)) for p in params]
    in_specs += [pl.BlockSpec((tm, w), lambda i, cb=cb: (i, cb)) for (_, w, cb) in douts]
    out_specs = [pl.BlockSpec((tm, w), lambda i: (i, 0)) for (_, w, _) in rows]
    out_specs += [pl.BlockSpec(p.shape, lambda i: (0, 0)) for p in params]
    out_shape = [jax.ShapeDtypeStruct((R, w), F32) for (_, w, _) in rows]
    out_shape += [jax.ShapeDtypeStruct(p.shape, F32) for p in params]
    res = pl.pallas_call(
        body, name=name, grid=(R // tm,), in_specs=in_specs, out_specs=out_specs, out_shape=out_shape,
        compiler_params=_params(("arbitrary",)),
    )(*[r[0] for r in rows], *params, *[d[0] for d in douts])
    return res[:n_r], res[n_r:]


def _shift_sum(name, terms, tm=ROW_TILE):
    R = terms[0][0][0].shape[0]
    w = terms[0][0][1]
    tm = _pick(R, tm)
    nt, nb8 = R // tm, tm // 8
    shifts = [j for (_, j) in terms]

    def body(*refs):
        i = pl.program_id(0)
        acc = None
        for k, j in enumerate(shifts):
            x = refs[2 * k][...]
            if j > 0:
                halo = jnp.where(i == 0, 0.0, refs[2 * k + 1][...])
                x = pltpu.roll(jnp.concatenate([halo, x], axis=0), j, 0)[8:, :]
            elif j < 0:
                halo = jnp.where(i == nt - 1, 0.0, refs[2 * k + 1][...])
                x = pltpu.roll(jnp.concatenate([x, halo], axis=0), tm + 8 + j, 0)[:tm, :]
            acc = x if acc is None else acc + x
        refs[-1][...] = acc

    in_specs, args = [], []
    for (arr, _, cb), j in terms:
        in_specs.append(pl.BlockSpec((tm, w), lambda i, cb=cb: (i, cb)))
        if j > 0:
            in_specs.append(pl.BlockSpec((8, w), lambda i, cb=cb: (jnp.maximum(i * nb8 - 1, 0), cb)))
        else:
            in_specs.append(pl.BlockSpec((8, w), lambda i, cb=cb: (jnp.minimum((i + 1) * nb8, R // 8 - 1), cb)))
        args += [arr, arr]
    return pl.pallas_call(
        body, name=name, grid=(nt,), in_specs=in_specs, out_specs=pl.BlockSpec((tm, w), lambda i: (i, 0)),
        out_shape=jax.ShapeDtypeStruct((R, w), F32), compiler_params=_params(("parallel",)),
    )(*args)


def _softplus(x):
    return jnp.maximum(x, 0.0) + jnp.log(1.0 + jnp.exp(-jnp.abs(x)))


def _silu(x):
    return x * jax.nn.sigmoid(x)


def _rms_fn(h, w):
    return (h * lax.rsqrt(jnp.mean(h * h, axis=-1, keepdims=True) + NORM_EPS) * w,)


def _rw_prep_fn(pr, pr1, pk, pk1, pv, pv1, ps, ps1, mu_r, mu_k, mu_v, mu_s, w0, w2p, a0, a2p, k_k, k_a):
    r = pr + (pr1 - pr) * mu_r
    k = pk + (pk1 - pk) * mu_k
    v = pv + (pv1 - pv) * mu_v
    s = ps + (ps1 - ps) * mu_s
    w_log = -_softplus(-(w0 + jnp.dot(jnp.tanh(s), w2p, precision=HIGH, preferred_element_type=F32))) - 0.5
    log_decay = -jnp.exp(w_log)
    a = jax.nn.sigmoid(a0 + jnp.dot(s, a2p, precision=HIGH, preferred_element_type=F32))
    return r, log_decay, k * (1.0 + (a - 1.0) * k_a), v, k * k_k, a


def _conv_fn(u0, u1, u2, u3, w0, w1, w2, w3):
    return (_silu(u0 * w3 + u1 * w2 + u2 * w1 + u3 * w0),)


def _dn_gate_fn(ps2, a_log_n, dt_n):
    beta_n = jax.nn.sigmoid(ps2)
    g_n = -jnp.exp(a_log_n) * _softplus(ps2 + dt_n)
    lane = lax.broadcasted_iota(jnp.int32, (1, DN_HEAD), 1)
    pick = lambda x, j: jnp.broadcast_to(jnp.sum(x * (lane == j).astype(F32), axis=-1, keepdims=True), x.shape)[None]
    beta = jnp.concatenate([pick(beta_n, h) for h in range(DN_HEADS)], axis=0)
    g = jnp.concatenate([pick(g_n, DN_HEADS + h) for h in range(DN_HEADS)], axis=0)
    return beta, g


def _tril_masks(c):
    t = lax.broadcasted_iota(jnp.int32, (c, c), 0)
    s = lax.broadcasted_iota(jnp.int32, (c, c), 1)
    return s <= t, s < t


def _dot(x, y, cx, cy, prec=HIGH):
    nb = x.ndim - 2
    batch = tuple(range(nb))
    return lax.dot_general(x, y, (((cx + nb,), (cy + nb,)), (batch, batch)), precision=prec,
                           preferred_element_type=F32)


def _tri_inv(a):
    c = a.shape[-1]
    eye = (lax.broadcasted_iota(jnp.int32, (c, c), 0) == lax.broadcasted_iota(jnp.int32, (c, c), 1)).astype(F32)
    x = eye - a
    p = a
    n = 2
    while n < c:
        p = _dot(p, p, 1, 0)
        x = x + _dot(x, p, 1, 0)
        n *= 2
    return x


@jax.custom_vjp
def _tri_inv_saved(a, t):
    return t


def _tri_inv_saved_fwd(a, t):
    return t, t


def _tri_inv_saved_bwd(t, dt):
    return -_dot(_dot(t, dt, 0, 0), t, 1, 1), jnp.zeros_like(t)


_tri_inv_saved.defvjp(_tri_inv_saved_fwd, _tri_inv_saved_bwd)


def _pair_masks():
    lane = lax.broadcasted_iota(jnp.int32, (1, 2 * RW_HEAD), 1)
    m0 = (lane < RW_HEAD).astype(F32)
    return m0, 1.0 - m0


def _pair_bd(x):
    m0, m1 = _pair_masks()
    return jnp.concatenate([x * m0, x * m1], axis=-2)


def _pair_mm(x, y):
    return _dot(x, _pair_bd(y), 1, 0)


def _pair_inv(a):
    c = a.shape[-2]
    row = lax.broadcasted_iota(jnp.int32, (c, 2 * RW_HEAD), 0)
    col = lax.broadcasted_iota(jnp.int32, (c, 2 * RW_HEAD), 1) & (RW_HEAD - 1)
    x = (row == col).astype(F32) - a
    p = a
    n = 2
    while n < c:
        p = _pair_mm(p, p)
        x = x + _pair_mm(x, p)
        n *= 2
    return x


@jax.custom_vjp
def _pair_inv_saved(a, t):
    return t


def _pair_inv_saved_fwd(a, t):
    return t, t


def _pair_inv_saved_bwd(t, dt):
    m0, m1 = _pair_masks()
    c = t.shape[-2]
    z = _dot(t, dt, 0, 0)
    x = z[:, :c, :] * m0 + z[:, c:, :] * m1
    return -_dot(x, _pair_bd(t), 1, 1), jnp.zeros_like(t)


_pair_inv_saved.defvjp(_pair_inv_saved_fwd, _pair_inv_saved_bwd)


def _rw_chunk_fn(S, r, lw, k2, v, kkp, a, gate, rk, gnw, gnb, t_saved=None):
    B, C, P = r.shape
    m0, m1 = _pair_masks()
    seg = lambda x: (jnp.sum(x * m0, axis=-1, keepdims=True) * m0 + jnp.sum(x * m1, axis=-1, keepdims=True) * m1)
    mm = lambda x, y: _dot(x.astype(BF16), y.astype(BF16), 1, 0, None)
    nt = lambda x, y: _dot(x, y, 1, 1)
    tn = lambda x, y: _dot(x, y, 0, 0)
    pair_mm = lambda x, y: mm(x, _pair_bd(y))
    t_idx = lax.broadcasted_iota(jnp.int32, (C, P), 0)
    s_idx = lax.broadcasted_iota(jnp.int32, (C, P), 1) & (RW_HEAD - 1)
    incl, strict = s_idx <= t_idx, s_idx < t_idx
    tril = jnp.broadcast_to(_tril_masks(C)[0].astype(F32), (B, C, C))
    kk = kkp * lax.rsqrt(seg(kkp * kkp) + 1e-6)
    b = kk * a
    g_incl = _dot(tril, lw, 1, 0)
    g_excl = g_incl - lw
    inv = jnp.exp(-g_incl)
    alpha, beta, kappa, rho = kk * jnp.exp(g_excl), b * inv, k2 * inv, r * jnp.exp(g_incl)
    ar = jnp.concatenate([alpha, rho], axis=-2)
    scores = nt(ar, jnp.concatenate([_pair_bd(beta), _pair_bd(kappa)], axis=-2))
    a_ab = jnp.where(strict, scores[:, :C, :P], 0.0)
    a_ak = jnp.where(strict, scores[:, :C, P:], 0.0)
    r_b = jnp.where(incl, scores[:, C:, :P], 0.0)
    r_k = jnp.where(incl, scores[:, C:, P:], 0.0)
    t_inv = _pair_inv(a_ab) if t_saved is None else _pair_inv_saved(a_ab, t_saved)
    on_state = nt(ar, S)
    u = pair_mm(t_inv, -on_state[:, :C, :] - pair_mm(a_ak, v))
    y = on_state[:, C:, :] + mm(jnp.concatenate([r_b, r_k], axis=-1),
                                jnp.concatenate([_pair_bd(u), _pair_bd(v)], axis=-2))
    same_head = ((lax.broadcasted_iota(jnp.int32, (P, P), 0) < RW_HEAD)
                 == (lax.broadcasted_iota(jnp.int32, (P, P), 1) < RW_HEAD))
    fresh = tn(jnp.concatenate([u, v], axis=-2), jnp.concatenate([beta, kappa], axis=-2))
    S_new = jnp.exp(jnp.sum(lw, axis=-2, keepdims=True)) * (S + jnp.where(same_head, fresh, 0.0))
    dev = y - seg(y) * (1.0 / RW_HEAD)
    yn = dev * lax.rsqrt(seg(dev * dev) * (1.0 / RW_HEAD) + RW_GN_EPS) * gnw + gnb
    bonus = seg(r * k2 * rk) * v
    return (yn + bonus) * _silu(gate), S_new, t_inv


def _dn_chunk_fn(S, qc, kc, vc, bb, gb, z, nw, t_saved=None):
    B, C, D = qc.shape
    mm = lambda x, y: _dot(x.astype(BF16), y.astype(BF16), 1, 0, None)
    nt = lambda x, y, p=None: _dot(x, y, 1, 1, p) if p else _dot(x.astype(BF16), y.astype(BF16), 1, 1, None)
    tn = lambda x, y: _dot(x.astype(BF16), y.astype(BF16), 0, 0, None)
    incl, strict = _tril_masks(C)
    q = qc * lax.rsqrt(jnp.sum(qc * qc, axis=-1, keepdims=True) + 1e-6) * (D ** -0.5)
    k = kc * lax.rsqrt(jnp.sum(kc * kc, axis=-1, keepdims=True) + 1e-6)
    kb, vb = k * bb, vc * bb
    G = _dot(jnp.broadcast_to(incl.astype(F32), (B, C, C)), gb, 1, 0, HIGHEST)
    lane = lax.broadcasted_iota(jnp.int32, (C, D), 1)
    e0, e1 = (lane == 0).astype(F32), (lane == 1).astype(F32)
    diff = nt(G * e0 + e1, e0 - G * e1, HIGHEST)
    dmask = jnp.where(incl, jnp.exp(jnp.where(incl, diff, 0.0)), 0.0)
    M = jnp.where(strict, nt(kb, k) * dmask, 0.0)
    T = _tri_inv(M) if t_saved is None else _tri_inv_saved(M, t_saved)
    eG = jnp.exp(G)
    u = mm(T, vb)
    w = mm(T, kb * eG)
    attn = jnp.where(incl, nt(q, k) * dmask, 0.0)
    v_new = u - mm(w, S)
    o = mm(q * eG, S) + mm(attn, v_new)
    g_last = jnp.sum(gb, axis=-2, keepdims=True)
    S_new = S * jnp.exp(jnp.broadcast_to(g_last, S.shape)) + tn(k * jnp.exp(g_last - G), v_new)
    on = o * lax.rsqrt(jnp.mean(o * o, axis=-1, keepdims=True) + NORM_EPS) * nw
    return on * _silu(z), S_new, T


N_GROUPS = 8
GROUP = 128
HALO = 8


def _groups(x):
    return jnp.concatenate([x[:, g * GROUP:(g + 1) * GROUP][None] for g in range(N_GROUPS)], axis=0)


@functools.partial(jax.custom_vjp, nondiff_argnums=(1,))
def _shift_rows(ext, j):
    return pltpu.roll(ext, j, 0)[HALO:, :]


def _shift_rows_fwd(ext, j):
    return _shift_rows(ext, j), None


def _shift_rows_bwd(j, _, d):
    z = jnp.concatenate([jnp.zeros((HALO, d.shape[1]), d.dtype), d], axis=0)
    return (pltpu.roll(z, z.shape[0] - j, 0),)


_shift_rows.defvjp(_shift_rows_fwd, _shift_rows_bwd)


def _rw_fused_fn(S, h_r, h_k, h_v, h_s, p_r, p_k, p_v, p_s, gate, *pars, t_saved=None):
    prev = lambda h, x: _shift_rows(jnp.concatenate([h, x], axis=0), 1)
    seq = _rw_prep_fn(p_r, prev(h_r, p_r), p_k, prev(h_k, p_k), p_v, prev(h_v, p_v), p_s, prev(h_s, p_s), *pars[:10])
    return _rw_chunk_fn(S, *[_groups(t) for t in seq], _groups(gate), *[_groups(t) for t in pars[10:]],
                        t_saved=t_saved)


def _dn_fused_fn(S, h_q, h_k, h_v, p_q, p_k, p_v, p_s2, z, *pars, t_saved=None):
    conv = []
    for i, (h, x) in enumerate(((h_q, p_q), (h_k, p_k), (h_v, p_v))):
        ext = jnp.concatenate([h, x], axis=0)
        conv += _conv_fn(x, _shift_rows(ext, 1), _shift_rows(ext, 2), _shift_rows(ext, 3), *pars[4 * i:4 * i + 4])
    beta, g = _dn_gate_fn(p_s2, pars[12], pars[13])
    return _dn_chunk_fn(S, *[_groups(t) for t in conv], beta, g, _groups(z), _groups(pars[14]), t_saved=t_saved)


def _chunk_fwd(name, fn, rows, n_halo, pars, t_lanes):
    Lp = rows[0][0].shape[0]
    C, D, NB = CHUNK, GROUP, N_GROUPS
    nc, n_r, n_p, per = Lp // C, len(rows), len(pars), CHUNK // HALO

    def body(*refs):
        row_refs, halo_refs, par_refs = refs[:n_r], refs[n_r:n_r + n_halo], refs[n_r + n_halo:n_r + n_halo + n_p]
        y_ref, ck_ref, t_ref, s_ref = refs[n_r + n_halo + n_p:]
        first = pl.program_id(0) == 0

        @pl.when(first)
        def _():
            s_ref[...] = jnp.zeros_like(s_ref)

        S = s_ref[...]
        ck_ref[0] = S
        halos = [jnp.where(first, 0.0, r[...]) for r in halo_refs]
        y, S_new, t_inv = fn(S, *halos, *[r[...] for r in row_refs], *[r[...] for r in par_refs])
        for g in range(NB):
            y_ref[:, g * D:(g + 1) * D] = y[g].astype(BF16)
        t_ref[0] = t_inv
        s_ref[...] = S_new

    return pl.pallas_call(
        body, name=name, grid=(nc,),
        in_specs=[pl.BlockSpec((C, w), lambda c, cb=cb: (c, cb)) for (_, w, cb) in rows]
        + [pl.BlockSpec((HALO, w), lambda c, cb=cb: (jnp.maximum(c * per - 1, 0), cb)) for (_, w, cb) in rows[:n_halo]]
        + [pl.BlockSpec(p.shape, lambda c: (0, 0)) for p in pars],
        out_specs=[pl.BlockSpec((C, NB * D), lambda c: (c, 0)),
                   pl.BlockSpec((1, NB, D, D), lambda c: (c, 0, 0, 0)),
                   pl.BlockSpec((1, NB, C, t_lanes), lambda c: (c, 0, 0, 0))],
        out_shape=[jax.ShapeDtypeStruct((Lp, NB * D), BF16), jax.ShapeDtypeStruct((nc, NB, D, D), F32),
                   jax.ShapeDtypeStruct((nc, NB, C, t_lanes), F32)],
        scratch_shapes=[pltpu.VMEM((NB, D, D), F32)],
        compiler_params=_params(("arbitrary",)),
    )(*[r[0] for r in rows], *[r[0] for r in rows[:n_halo]], *pars)


def _chunk_bwd(name, fn, rows, n_halo, pars, ckpt, tinv, dy):
    Lp = rows[0][0].shape[0]
    C, D, NB = CHUNK, GROUP, N_GROUPS
    nc, n_r, n_p, per = Lp // C, len(rows), len(pars), CHUNK // HALO
    t_lanes = tinv.shape[-1]
    n_in = n_r + n_halo + n_p

    def body(*refs):
        row_refs, halo_refs, par_refs = refs[:n_r], refs[n_r:n_r + n_halo], refs[n_r + n_halo:n_in]
        ck_ref, t_ref, dy_ref = refs[n_in:n_in + 3]
        outs = refs[n_in + 3:]
        drow_refs, dpar_refs = outs[:n_r], outs[n_r:n_r + n_p]
        ds_ref, carry_refs = outs[n_r + n_p], outs[n_r + n_p + 1:]
        i = pl.program_id(0)

        @pl.when(i == 0)
        def _():
            ds_ref[...] = jnp.zeros_like(ds_ref)
            for o_ref in list(dpar_refs) + list(carry_refs):
                o_ref[...] = jnp.zeros_like(o_ref)

        halos = [jnp.where(i == nc - 1, 0.0, r[...]) for r in halo_refs]
        vals = [ck_ref[0]] + halos + [r[...] for r in row_refs] + [r[...] for r in par_refs]
        t_saved = t_ref[0]
        grads = jax.vjp(lambda *v: fn(*v, t_saved=t_saved)[:2], *vals)[1]((_groups(dy_ref), ds_ref[...]))
        ds_ref[...] = grads[0]
        d_halos, d_rows, d_pars = grads[1:1 + n_halo], grads[1 + n_halo:1 + n_halo + n_r], grads[1 + n_halo + n_r:]
        for k, (o_ref, g) in enumerate(zip(drow_refs, d_rows)):
            if k < n_halo:
                g = g + jnp.concatenate([jnp.zeros((C - HALO, g.shape[1]), F32), carry_refs[k][...]], axis=0)
                carry_refs[k][...] = d_halos[k]
            o_ref[...] = g.astype(BF16)
        for o_ref, g in zip(dpar_refs, d_pars):
            o_ref[...] += g

    rev = lambda c: nc - 1 - c
    dy_arr, dy_w, dy_cb = dy
    res = pl.pallas_call(
        body, name=name, grid=(nc,),
        in_specs=[pl.BlockSpec((C, w), lambda c, cb=cb: (rev(c), cb)) for (_, w, cb) in rows]
        + [pl.BlockSpec((HALO, w), lambda c, cb=cb: (jnp.maximum(rev(c) * per - 1, 0), cb))
           for (_, w, cb) in rows[:n_halo]]
        + [pl.BlockSpec(p.shape, lambda c: (0, 0)) for p in pars]
        + [pl.BlockSpec((1, NB, D, D), lambda c: (rev(c), 0, 0, 0)),
           pl.BlockSpec((1, NB, C, t_lanes), lambda c: (rev(c), 0, 0, 0)),
           pl.BlockSpec((C, dy_w), lambda c: (rev(c), dy_cb))],
        out_specs=[pl.BlockSpec((C, w), lambda c: (rev(c), 0)) for (_, w, _) in rows]
        + [pl.BlockSpec(p.shape, lambda c: (0, 0)) for p in pars],
        out_shape=[jax.ShapeDtypeStruct((Lp, w), BF16) for (_, w, _) in rows]
        + [jax.ShapeDtypeStruct(p.shape, F32) for p in pars],
        scratch_shapes=[pltpu.VMEM((NB, D, D), F32)] + [pltpu.VMEM((HALO, w), F32) for (_, w, _) in rows[:n_halo]],
        compiler_params=_params(("arbitrary",)),
    )(*[r[0] for r in rows], *[r[0] for r in rows[:n_halo]], *pars, ckpt, tinv, dy_arr)
    return res[:n_r], res[n_r:]


def _loss_head(h, yw, tgt, fw, n_real, tm=2 * ROW_TILE):
    Lp, Dm = h.shape
    tm = _pick(Lp, tm, 16)

    def out_fn(z, fw_):
        return z * lax.rsqrt(jnp.mean(z * z, axis=-1, keepdims=True) + NORM_EPS) * fw_

    def body(h_ref, yw_ref, t_ref, fw_ref, loss_ref, dz_ref, dz16_ref, dfw_ref):
        i = pl.program_id(0)

        @pl.when(i == 0)
        def _():
            loss_ref[...] = jnp.zeros_like(loss_ref)
            dfw_ref[...] = jnp.zeros_like(dfw_ref)

        row = i * tm + lax.broadcasted_iota(jnp.int32, (tm, 1), 0)
        mask = ((row >= N_META) & (row < n_real)).astype(F32)
        z = h_ref[...] + yw_ref[...]
        o, vjp = jax.vjp(out_fn, z, fw_ref[...])
        err = (o - t_ref[...]) * mask
        row_loss = 0.5 * jnp.mean(jnp.square(err), axis=-1, keepdims=True)
        dz, dfw = vjp(err * (1.0 / Dm))
        loss_ref[...] += jnp.sum(row_loss, axis=0, keepdims=True)
        dz_ref[...] = dz
        dz16_ref[...] = dz.astype(BF16)
        dfw_ref[...] += dfw

    row_spec = pl.BlockSpec((tm, Dm), lambda i: (i, 0))
    return pl.pallas_call(
        body, name="loss_head", grid=(Lp // tm,),
        in_specs=[row_spec, row_spec, row_spec, pl.BlockSpec((1, Dm), lambda i: (0, 0))],
        out_specs=[pl.BlockSpec((8, 128), lambda i: (0, 0)), row_spec, row_spec,
                   pl.BlockSpec((1, Dm), lambda i: (0, 0))],
        out_shape=[jax.ShapeDtypeStruct((8, 128), F32), jax.ShapeDtypeStruct((Lp, Dm), F32),
                   jax.ShapeDtypeStruct((Lp, Dm), BF16), jax.ShapeDtypeStruct((1, Dm), F32)],
        compiler_params=_params(("arbitrary",)),
    )(h, yw, tgt, fw)


def _exchange(name, x, masks, slot_kind, per_dest, n_split=1, copy_own=True, other_half=False):
    n = len(masks)
    keep_own = slot_kind is not None and copy_own
    n_slots = {"chip": 4, "core": 2, "dev": 8, None: n}[slot_kind]
    blk_shape = x.shape[1:] if per_dest else x.shape
    if other_half:
        blk_shape = (x.shape[0] // 2,) + tuple(x.shape[1:])
    rows = blk_shape[0] // n_split

    def body(x_ref, o_ref, send_sems, recv_sems, local_sems):
        mx, my, mc = lax.axis_index("x"), lax.axis_index("y"), lax.axis_index("c")
        if other_half:
            x_ref = x_ref.at[pl.ds((1 - mc) * blk_shape[0], blk_shape[0])]

        def slot(k, px, py, pc):
            return {"chip": 2 * px + py, "core": pc, "dev": 4 * px + 2 * py + pc, None: k}[slot_kind]

        def peer(m):
            return (mx + m[0]) % 2, (my + m[1]) % 2, (mc + m[2]) % 2

        def part(ref, j):
            return ref.at[pl.ds(j * rows, rows)]

        own_src = x_ref.at[2 * mx + my] if per_dest else x_ref
        local = []
        if keep_own:
            own_dst = o_ref.at[slot(0, mx, my, mc)]
            local = [pltpu.make_async_copy(part(own_src, j), part(own_dst, j), local_sems.at[j])
                     for j in range(n_split)]
        for cp in local:
            cp.start()
        sends = []
        for k, m in enumerate(masks):
            px, py, pc = peer(m)
            src = x_ref.at[2 * px + py] if per_dest else x_ref
            dst = o_ref.at[slot(k, mx, my, mc)]
            for j in range(n_split):
                sends.append(pltpu.make_async_remote_copy(
                    src_ref=part(src, j), dst_ref=part(dst, j), send_sem=send_sems.at[k * n_split + j],
                    recv_sem=recv_sems.at[k * n_split + j], device_id=(px, py, pc), device_id_type=MESH))
        for cp in sends:
            cp.start()
        for k, m in enumerate(masks):
            px, py, pc = peer(m)
            landed = o_ref.at[slot(k, px, py, pc)]
            for j in range(n_split):
                pltpu.make_async_remote_copy(
                    src_ref=part(own_src, j), dst_ref=part(landed, j), send_sem=send_sems.at[k * n_split + j],
                    recv_sem=recv_sems.at[k * n_split + j], device_id=(px, py, pc), device_id_type=MESH).wait_recv()
        for cp in sends:
            cp.wait_send()
        for cp in local:
            cp.wait()

    return pl.pallas_call(
        body, name=name,
        in_specs=[pl.BlockSpec(memory_space=pl.ANY)], out_specs=pl.BlockSpec(memory_space=pl.ANY),
        out_shape=jax.ShapeDtypeStruct((n_slots,) + tuple(blk_shape), x.dtype),
        scratch_shapes=[pltpu.SemaphoreType.DMA((n * n_split,)), pltpu.SemaphoreType.DMA((n * n_split,)),
                        pltpu.SemaphoreType.DMA((n_split,))],
        compiler_params=pltpu.CompilerParams(has_side_effects=True),
    )(x)


CHIP_MASKS = [(1, 0, 0), (0, 1, 0), (1, 1, 0)]
CORE_MASKS = [(0, 0, 1)]
ALL_MASKS = [(dx, dy, dc) for dx in (0, 1) for dy in (0, 1) for dc in (0, 1) if (dx, dy, dc) != (0, 0, 0)]


def _gather_chips(name, x):
    return _exchange(name, x, CHIP_MASKS, "chip", False)


def _gather_shards(name, shard, n_split):
    half = shard.shape[0] // 2
    c, chip = lax.axis_index("c"), 2 * lax.axis_index("x") + lax.axis_index("y")
    mine = lax.dynamic_slice_in_dim(shard, c * half, half, axis=0)
    by_chip = _exchange(name + "_chips", mine, CHIP_MASKS, "chip", False, n_split, copy_own=False)
    by_chip = lax.dynamic_update_index_in_dim(by_chip, mine, chip, 0)
    both = _exchange(name + "_cores", by_chip, CORE_MASKS, "core", False, N_CHIPS, copy_own=False)
    return lax.dynamic_update_index_in_dim(both, by_chip, c, 0)


def _sum_slots(name, x, tr=128):
    S, R, N = x.shape
    tr = _pick(R, tr, 16)

    def body(x_ref, o_ref):
        acc = x_ref[0].astype(F32)
        for s in range(1, S):
            acc = acc + x_ref[s].astype(F32)
        o_ref[...] = acc

    return pl.pallas_call(
        body, name=name, grid=(R // tr,),
        in_specs=[pl.BlockSpec((S, tr, N), lambda i: (0, i, 0))], out_specs=pl.BlockSpec((tr, N), lambda i: (i, 0)),
        out_shape=jax.ShapeDtypeStruct((R, N), F32), compiler_params=_params(("parallel",)),
    )(x)


def _add_to_bf16(name, a, b, tr=128):
    S, R, N = a.shape
    tr = _pick(R, tr, 16)

    def body(a_ref, b_ref, o_ref):
        o_ref[...] = (a_ref[...] + b_ref[...]).astype(BF16)

    spec = pl.BlockSpec((S, tr, N), lambda i: (0, i, 0))
    return pl.pallas_call(
        body, name=name, grid=(R // tr,), in_specs=[spec, spec], out_specs=spec,
        out_shape=jax.ShapeDtypeStruct((S, R, N), BF16), compiler_params=_params(("parallel",)),
    )(a, b)


def _add_slabs_to_bf16(name, full, recv, c, tr=64):
    R, NP = full.shape
    half = R // 2
    nb = half // tr

    def body(c_ref, a_ref, b_ref, o_ref):
        x = a_ref[...] + b_ref[...]
        for s in range(N_CHIPS):
            o_ref[s] = x[:, s * SHARD_COLS:(s + 1) * SHARD_COLS].astype(BF16)

    grid_spec = pltpu.PrefetchScalarGridSpec(
        num_scalar_prefetch=1, grid=(nb,),
        in_specs=[pl.BlockSpec((tr, NP), lambda i, c_ref: (c_ref[0] * nb + i, 0)),
                  pl.BlockSpec((tr, NP), lambda i, c_ref: (i, 0))],
        out_specs=pl.BlockSpec((N_CHIPS, tr, SHARD_COLS), lambda i, c_ref: (0, i, 0)))
    return pl.pallas_call(
        body, name=name, grid_spec=grid_spec,
        out_shape=jax.ShapeDtypeStruct((N_CHIPS, half, SHARD_COLS), BF16), compiler_params=_params(("parallel",)),
    )(jnp.reshape(c, (1,)).astype(jnp.int32), full, recv)


def _adamw(name, gparts, w, m, v, tr=128):
    S, R, N = gparts.shape
    tr = _pick(R, tr)
    c1 = 1.0 / (1.0 - ADAM_B1 ** ADAM_STEP)
    c2 = 1.0 / (1.0 - ADAM_B2 ** ADAM_STEP)

    def body(g_ref, w_ref, m_ref, v_ref, go_ref, d_ref, mo_ref, vo_ref):
        g = g_ref[0]
        for s in range(1, S):
            g = g + g_ref[s]
        m_new = ADAM_B1 * m_ref[...] + (1.0 - ADAM_B1) * g
        v_new = ADAM_B2 * v_ref[...] + (1.0 - ADAM_B2) * jnp.square(g)
        go_ref[...] = g
        mo_ref[...] = m_new
        vo_ref[...] = v_new
        d_ref[...] = -ADAM_LR * ((m_new * c1) / (jnp.sqrt(v_new * c2) + ADAM_EPS) + ADAM_WD * w_ref[...])

    spec = pl.BlockSpec((tr, N), lambda i: (i, 0))
    return pl.pallas_call(
        body, name=name, grid=(R // tr,),
        in_specs=[pl.BlockSpec((S, tr, N), lambda i: (0, i, 0)), spec, spec, spec], out_specs=[spec] * 4,
        out_shape=[jax.ShapeDtypeStruct((R, N), F32)] * 4, compiler_params=_params(("parallel",)),
    )(gparts, w, m, v)


def _reduce_to_shard(name, slabs, n_split, by_columns=False):
    c, chip = lax.axis_index("c"), 2 * lax.axis_index("x") + lax.axis_index("y")
    if by_columns:
        R, N = slabs.shape[0], SHARD_COLS
        from_sibling = _exchange(name + "_sib", slabs, CORE_MASKS, None, False, n_split, other_half=True)[0]
        wire = _add_slabs_to_bf16(name + "_add", slabs, from_sibling, c)
    else:
        _, R, N = slabs.shape
        half = R // 2
        halves = slabs.reshape(N_CHIPS, 2, half, N)
        mine = lax.dynamic_index_in_dim(halves, c, axis=1, keepdims=False)
        theirs = lax.dynamic_index_in_dim(halves, 1 - c, axis=1, keepdims=False)
        from_sibling = _exchange(name + "_sib", theirs, CORE_MASKS, None, False, N_CHIPS)[0]
        wire = _add_to_bf16(name + "_add", mine, from_sibling)
    got = _exchange(name + "_chips", wire, CHIP_MASKS, "chip", True, n_split, copy_own=False)
    got = lax.dynamic_update_index_in_dim(got, lax.dynamic_index_in_dim(wire, chip, 0, keepdims=False), chip, 0)
    part = _sum_slots(name + "_sum", got)
    both = _exchange(name + "_cores", part, CORE_MASKS, "core", False, n_split, copy_own=False)
    return lax.dynamic_update_index_in_dim(both, part, c, 0).reshape(1, R, N)


def _pack(pieces, cols, row_mult=8):
    flat = jnp.concatenate([p.reshape(-1) for p in pieces])
    rows = -(-flat.shape[0] // cols)
    rows = -(-rows // row_mult) * row_mult
    return jnp.pad(flat, (0, rows * cols - flat.shape[0])).reshape(rows, cols)


def _unpack(packed, shapes):
    flat = packed.reshape(-1)
    out, off = [], 0
    for shp in shapes:
        n = 1
        for d in shp:
            n *= d
        out.append(flat[off:off + n].reshape(shp))
        off += n
    return out


def _to_my_layout(w):
    z = jnp.zeros((w.shape[0], 112), w.dtype)
    return jnp.concatenate([w[:, 0:3072], w[:, 3200:4224], w[:, 4224:7296], w[:, 7312:8336],
                            w[:, 3072:3200], w[:, 7296:7312], z], axis=1)


def kernel(x, meta_tokens, norm_w, w_in, rw_shift_mu, rw_w0, rw_w2, rw_a0, rw_a2, rw_k_k, rw_k_a, rw_r_k, rw_gn_w, rw_gn_b, dn_conv_w, dn_A_log, dn_dt_bias, dn_norm_w, w_out, final_norm_w, loss_target, m_meta_tokens, m_norm_w, m_w_in, m_rw_shift_mu, m_rw_w0, m_rw_w2, m_rw_a0, m_rw_a2, m_rw_k_k, m_rw_k_a, m_rw_r_k, m_rw_gn_w, m_rw_gn_b, m_dn_conv_w, m_dn_A_log, m_dn_dt_bias, m_dn_norm_w, m_w_out, m_final_norm_w, v_meta_tokens, v_norm_w, v_w_in, v_rw_shift_mu, v_rw_w0, v_rw_w2, v_rw_a0, v_rw_a2, v_rw_k_k, v_rw_k_a, v_rw_r_k, v_rw_gn_w, v_rw_gn_b, v_dn_conv_w, v_dn_A_log, v_dn_dt_bias, v_dn_norm_w, v_w_out, v_final_norm_w):
    S = x.shape[1]
    L = N_META + S
    Lp = -(-L // CHUNK) * CHUNK

    small_shapes = [(RW_LORA, 256), (RW_LORA, 256), (CONV_W, 768), (N_META, 512)]
    small_mine = _pack([rw_w2[0], rw_a2[0], dn_conv_w[0], meta_tokens], 1024)
    small_all = _gather_chips("gather_small", small_mine)
    per_chip = [_unpack(small_all[s], small_shapes) for s in range(N_CHIPS)]
    w2, a2, conv_w, meta = [jnp.concatenate([per_chip[s][i] for s in range(N_CHIPS)], axis=1) for i in range(4)]
    w_in_all = _gather_shards("gather_w_in", w_in[0].astype(BF16), 8)
    W_orig = jnp.transpose(w_in_all, (0, 2, 1, 3)).reshape(D_MODEL, IN_COLS)
    W = _to_my_layout(W_orig)
    W_orig_t = jnp.pad(W_orig, ((0, 0), (0, NP_COLS - IN_COLS))).T
    w_out_all = _gather_shards("gather_w_out", w_out[0].astype(BF16), 8)
    Wo = jnp.transpose(w_out_all, (1, 0, 2, 3)).reshape(D_MODEL, D_MODEL)

    tail = [jnp.zeros((Lp - L, D_MODEL), F32)] if Lp > L else []
    h = jnp.concatenate([meta, x[0]] + tail, axis=0)
    tgt = jnp.concatenate([jnp.zeros((N_META, D_MODEL), F32), loss_target[0]] + tail, axis=0)
    (u,) = _rowwise("rms_in", _rms_fn, [_row(h)], [norm_w], [D_MODEL], tm=2 * ROW_TILE, out_dtype=BF16)
    p = _mm("in_proj", u, W, "nn")

    mu = rw_shift_mu
    zpad = jnp.zeros((RW_LORA, RW_WIDTH), F32)
    rw_params = [mu[:, 0:1024], mu[:, 1024:2048], mu[:, 2048:3072], mu[:, 3072:3200], rw_w0,
                 jnp.concatenate([w2, zpad], axis=0), rw_a0, jnp.concatenate([zpad, a2], axis=0), rw_k_k, rw_k_a]
    rw_rows = [_row(p, 1024, CB_R), _row(p, 1024, CB_K), _row(p, 1024, CB_V), _row(p, 128, CB_S1),
               _row(p, 1024, CB_GATE)]
    rw_pars = rw_params + [rw_r_k, rw_gn_w, rw_gn_b]
    ya, rw_ck, rw_t = _chunk_fwd("rw_chunk_fwd", _rw_fused_fn, rw_rows, 4, rw_pars, GROUP)

    dn_rows = [_row(p, 1024, CB_DQ), _row(p, 1024, CB_DK), _row(p, 1024, CB_DV), _row(p, 128, CB_S2),
               _row(p, 1024, CB_Z)]
    dn_pars = [conv_w[j:j + 1, 1024 * i:1024 * (i + 1)] for i in range(3) for j in range(CONV_W)]
    narrow = lambda t: jnp.pad(t, ((0, 0), (DN_HEADS, DN_HEAD - 2 * DN_HEADS)))
    dn_pars += [narrow(dn_A_log), narrow(dn_dt_bias), jnp.tile(dn_norm_w, (1, DN_HEADS))]
    yb, dn_ck, dn_t = _chunk_fwd("dn_chunk_fwd", _dn_fused_fn, dn_rows, 3, dn_pars, CHUNK)

    y = jnp.concatenate([ya, yb], axis=1)
    yw = _mm("out_proj", y, Wo, "nn", tn=1024)
    loss_acc, dz, dz16, d_fw = _loss_head(h, yw, tgt, final_norm_w.reshape(1, D_MODEL), L)
    loss = lax.psum(loss_acc[0, 0], ("x", "y", "c"))

    dy = _mm("d_out_proj", dz16, Wo.T, "nn", tn=1024, tk=2048)
    d_wo = _mm("d_w_out", y.T, dz16, "nn", tm=512, tn=1024, tk=Lp)

    dp_rw, d_rw_pars = _chunk_bwd("rw_chunk_bwd", _rw_fused_fn, rw_rows, 4, rw_pars, rw_ck, rw_t, _row(dy, 1024, 0))
    d_prep_pars, d_rw_pars = d_rw_pars[:10], d_rw_pars[10:]
    dp_dn, d_dn_pars = _chunk_bwd("dn_chunk_bwd", _dn_fused_fn, dn_rows, 3, dn_pars, dn_ck, dn_t, _row(dy, 1024, 1))
    d_conv_parts = [jnp.concatenate(d_dn_pars[4 * i:4 * i + 4], axis=0) for i in range(3)]
    d_a_log_b, d_dt_b = d_dn_pars[12], d_dn_pars[13]
    d_dn_nw = jnp.sum(d_dn_pars[14].reshape(DN_HEADS, DN_HEAD), axis=0, keepdims=True)

    dp = jnp.concatenate(list(dp_rw[:3]) + [dp_rw[3], dp_rw[4]] + list(dp_dn[:3])
                         + [dp_dn[3][:, :2 * DN_HEADS], dp_dn[4], jnp.zeros((Lp, NP_COLS - IN_COLS), BF16)], axis=1)
    du = _mm("d_in_proj", dp, W_orig_t, "nn", tn=2048, tk=1408)
    d_W = _mm("d_w_in", u.T, dp, "nn", tm=1024, tn=768, tk=Lp)
    (dh,), (d_norm_w,) = _rowwise_bwd("rms_in_bwd", lambda h_, w_: (_rms_fn(h_, w_)[0], h_), [_row(h)], [norm_w],
                                      [_row(du), _row(dz)])
    grad_x = dh[N_META:L][None]

    d_mu = jnp.concatenate(d_prep_pars[0:4], axis=1)
    d_w2, d_a2 = d_prep_pars[5][:RW_LORA], d_prep_pars[7][RW_LORA:]
    d_conv = jnp.concatenate(d_conv_parts, axis=1)
    d_meta = dh[:N_META]
    head_sum = lambda t: t[:, DN_HEADS:2 * DN_HEADS]
    rep_names = ["norm_w", "rw_shift_mu", "rw_w0", "rw_a0", "rw_k_k", "rw_k_a", "rw_r_k", "rw_gn_w", "rw_gn_b",
                 "dn_A_log", "dn_dt_bias", "dn_norm_w", "final_norm_w"]
    rep_g = [d_norm_w, d_mu, d_prep_pars[4], d_prep_pars[6], d_prep_pars[8], d_prep_pars[9],
             d_rw_pars[0], d_rw_pars[1], d_rw_pars[2],
             head_sum(d_a_log_b), head_sum(d_dt_b), d_dn_nw, d_fw.reshape(D_MODEL)]
    rep_w = [norm_w, rw_shift_mu, rw_w0, rw_a0, rw_k_k, rw_k_a, rw_r_k, rw_gn_w, rw_gn_b, dn_A_log, dn_dt_bias,
             dn_norm_w, final_norm_w]
    rep_m = [m_norm_w, m_rw_shift_mu, m_rw_w0, m_rw_a0, m_rw_k_k, m_rw_k_a, m_rw_r_k, m_rw_gn_w, m_rw_gn_b,
             m_dn_A_log, m_dn_dt_bias, m_dn_norm_w, m_final_norm_w]
    rep_v = [v_norm_w, v_rw_shift_mu, v_rw_w0, v_rw_a0, v_rw_k_k, v_rw_k_a, v_rw_r_k, v_rw_gn_w, v_rw_gn_b,
             v_dn_A_log, v_dn_dt_bias, v_dn_norm_w, v_final_norm_w]
    rep_shapes = [t.shape for t in rep_w]
    rep_all = _exchange("gather_rep_grads", _pack(rep_g, 128), ALL_MASKS, "dev", False)
    rep_out = _adamw("adam_rep", rep_all, _pack(rep_w, 128), _pack(rep_m, 128), _pack(rep_v, 128))
    rep_out = [dict(zip(rep_names, _unpack(t, rep_shapes))) for t in rep_out]

    sm_slabs = jnp.stack([_pack([d_w2[:, 256 * s:256 * (s + 1)], d_a2[:, 256 * s:256 * (s + 1)],
                                 d_conv[:, 768 * s:768 * (s + 1)], d_meta[:, 512 * s:512 * (s + 1)]], 1024, 64)
                          for s in range(N_CHIPS)])
    sm_parts = _reduce_to_shard("rs_small", sm_slabs, 1)
    sm_w = [rw_w2[0], rw_a2[0], dn_conv_w[0], meta_tokens]
    sm_m = [m_rw_w2[0], m_rw_a2[0], m_dn_conv_w[0], m_meta_tokens]
    sm_v = [v_rw_w2[0], v_rw_a2[0], v_dn_conv_w[0], v_meta_tokens]
    sm_out = _adamw("adam_small", sm_parts, _pack(sm_w, 1024, 64), _pack(sm_m, 1024, 64), _pack(sm_v, 1024, 64))
    sm_names = ["rw_w2", "rw_a2", "dn_conv_w", "meta_tokens"]
    sm_full_shapes = [(1, RW_LORA, 256), (1, RW_LORA, 256), (1, CONV_W, 768), (N_META, 512)]
    sm_out = [dict(zip(sm_names, [t.reshape(shp) for t, shp in zip(_unpack(o, small_shapes), sm_full_shapes)]))
              for o in sm_out]

    wo_parts = _reduce_to_shard("rs_w_out", d_wo.reshape(N_CHIPS, D_MODEL // N_CHIPS, D_MODEL), 8)
    wo_out = _adamw("adam_w_out", wo_parts, w_out[0], m_w_out[0], v_w_out[0])
    wi_parts = _reduce_to_shard("rs_w_in", d_W, 8, by_columns=True)
    wi_out = _adamw("adam_w_in", wi_parts, w_in[0], m_w_in[0], v_w_in[0])

    order = ["meta_tokens", "norm_w", "w_in", "rw_shift_mu", "rw_w0", "rw_w2", "rw_a0", "rw_a2", "rw_k_k", "rw_k_a",
             "rw_r_k", "rw_gn_w", "rw_gn_b", "dn_conv_w", "dn_A_log", "dn_dt_bias", "dn_norm_w", "w_out",
             "final_norm_w"]
    outs = [loss, grad_x]
    for kind in range(4):
        table = dict(rep_out[kind])
        table.update(sm_out[kind])
        table["w_in"] = wi_out[kind][None]
        table["w_out"] = wo_out[kind][None]
        outs += [table[n] for n in order]
    return tuple(outs)
```

```python
import functools

import jax
import jax.numpy as jnp
from jax import lax
from jax.experimental import pallas as pl
from jax.experimental.pallas import tpu as pltpu

F32 = jnp.float32
BF16 = jnp.bfloat16
HIGH = lax.Precision.HIGH
HIGHEST = lax.Precision.HIGHEST
MESH = pl.DeviceIdType.MESH

D_MODEL = 2048
N_META = 16
RW_WIDTH = 1024
RW_HEAD = 64
RW_HEADS = 16
RW_LORA = 64
RW_GN_EPS = 64e-5
DN_WIDTH = 1024
DN_HEAD = 128
DN_HEADS = 8
CONV_W = 4
CHUNK = 64
NORM_EPS = 1e-6
IN_COLS = 8336
N_CHIPS = 4
SHARD_COLS = IN_COLS // N_CHIPS

NP_COLS = 8 * 1024 + 256
CB_R, CB_K, CB_V, CB_GATE, CB_DQ, CB_DK, CB_DV, CB_Z = range(8)
CB_S1 = 8192 // 128
CB_S2 = CB_S1 + 1

ADAM_LR = 0.001
ADAM_B1 = 0.9
ADAM_B2 = 0.999
ADAM_EPS = 1e-08
ADAM_WD = 0.01
ADAM_STEP = 10

VMEM_LIMIT_BYTES = 56 * 1024 * 1024
ROW_TILE = 104
MM_ROW_TILE = 832


def _params(sem=None):
    return pltpu.CompilerParams(dimension_semantics=sem, vmem_limit_bytes=VMEM_LIMIT_BYTES)


def _pick(n, target, mult=8):
    best = None
    for d in range(mult, min(n, target) + 1, mult):
        if n % d == 0:
            best = d
    return n if best is None else best


def _mm(name, a, b, mode, tm=MM_ROW_TILE, tn=1408, tk=2048, dep=None):
    if mode == "nn":
        (M, K), (_, N) = a.shape, b.shape
    elif mode == "nt":
        (M, K), (N, _) = a.shape, b.shape
    else:
        (K, M), (_, N) = a.shape, b.shape
    tm = _pick(M, tm, 128 if mode == "tn" else 16)
    tn = _pick(N, tn, 128)
    tk = _pick(K, tk, 8 if mode == "tn" else 128)
    if mode == "nn":
        a_spec = pl.BlockSpec((tm, tk), lambda i, j, k: (i, k))
        b_spec = pl.BlockSpec((tk, tn), lambda i, j, k: (k, j))
        dims = (((1,), (0,)), ((), ()))
    elif mode == "nt":
        a_spec = pl.BlockSpec((tm, tk), lambda i, j, k: (i, k))
        b_spec = pl.BlockSpec((tn, tk), lambda i, j, k: (j, k))
        dims = (((1,), (1,)), ((), ()))
    else:
        a_spec = pl.BlockSpec((tk, tm), lambda i, j, k: (k, i))
        b_spec = pl.BlockSpec((tk, tn), lambda i, j, k: (k, j))
        dims = (((0,), (0,)), ((), ()))

    def body(a_ref, b_ref, *rest):
        o_ref = rest[-1]

        @pl.when(pl.program_id(2) == 0)
        def _():
            o_ref[...] = jnp.zeros_like(o_ref)

        o_ref[...] += lax.dot_general(a_ref[...].astype(BF16), b_ref[...].astype(BF16), dims,
                                      preferred_element_type=F32)

    deps = [] if dep is None else [dep]
    return pl.pallas_call(
        body, name=name, grid=(M // tm, N // tn, K // tk),
        in_specs=[a_spec, b_spec] + [pl.BlockSpec((8, 128), lambda i, j, k: (0, 0))] * len(deps),
        out_specs=pl.BlockSpec((tm, tn), lambda i, j, k: (i, j)),
        out_shape=jax.ShapeDtypeStruct((M, N), F32),
        compiler_params=_params(("parallel", "parallel", "arbitrary")),
    )(a, b, *deps)


def _row(arr, width=None, cb=0):
    return (arr, arr.shape[1] if width is None else width, cb)


def _rowwise(name, fn, rows, params, out_widths, tm=ROW_TILE, out_dtype=F32):
    R = rows[0][0].shape[0]
    tm = _pick(R, tm, 16 if out_dtype == BF16 else 8)
    n_r, n_p = len(rows), len(params)

    def body(*refs):
        vals = [r[...] for r in refs[:n_r + n_p]]
        for o_ref, val in zip(refs[n_r + n_p:], fn(*vals)):
            o_ref[...] = val.astype(out_dtype)

    in_specs = [pl.BlockSpec((tm, w), lambda i, cb=cb: (i, cb)) for (_, w, cb) in rows]
    in_specs += [pl.BlockSpec(p.shape, lambda i: (0, 0)) for p in params]
    return pl.pallas_call(
        body, name=name, grid=(R // tm,), in_specs=in_specs,
        out_specs=[pl.BlockSpec((tm, w), lambda i: (i, 0)) for w in out_widths],
        out_shape=[jax.ShapeDtypeStruct((R, w), out_dtype) for w in out_widths],
        compiler_params=_params(("parallel",)),
    )(*[r[0] for r in rows], *params)


def _rowwise_bwd(name, fn, rows, params, douts, tm=ROW_TILE):
    R = rows[0][0].shape[0]
    tm = _pick(R, tm)
    n_r, n_p, n_d = len(rows), len(params), len(douts)

    def body(*refs):
        vals = [r[...] for r in refs[:n_r + n_p]]
        cts = tuple(r[...] for r in refs[n_r + n_p:n_r + n_p + n_d])
        grads = jax.vjp(fn, *vals)[1](cts)
        outs = refs[n_r + n_p + n_d:]
        for o_ref, g in zip(outs[:n_r], grads[:n_r]):
            o_ref[...] = g

        @pl.when(pl.program_id(0) == 0)
        def _():
            for o_ref in outs[n_r:]:
                o_ref[...] = jnp.zeros_like(o_ref)

        for o_ref, g in zip(outs[n_r:], grads[n_r:]):
            o_ref[...] += g

    in_specs = [pl.BlockSpec((tm, w), lambda i, cb=cb: (i, cb)) for (_, w, cb) in rows]
    in_specs += [pl.BlockSpec(p.shape, lambda i: (0, 0)) for p in params]
    in_specs += [pl.BlockSpec((tm, w), lambda i, cb=cb: (i, cb)) for (_, w, cb) in douts]
    out_specs = [pl.BlockSpec((tm, w), lambda i: (i, 0)) for (_, w, _) in rows]
    out_specs += [pl.BlockSpec(p.shape, lambda i: (0, 0)) for p in params]
    out_shape = [jax.ShapeDtypeStruct((R, w), F32) for (_, w, _) in rows]
    out_shape += [jax.ShapeDtypeStruct(p.shape, F32) for p in params]
    res = pl.pallas_call(
        body, name=name, grid=(R // tm,), in_specs=in_specs, out_specs=out_specs, out_shape=out_shape,
        compiler_params=_params(("arbitrary",)),
    )(*[r[0] for r in rows], *params, *[d[0] for d in douts])
    return res[:n_r], res[n_r:]


def _shift_sum(name, terms, tm=ROW_TILE):
    R = terms[0][0][0].shape[0]
    w = terms[0][0][1]
    tm = _pick(R, tm)
    nt, nb8 = R // tm, tm // 8
    shifts = [j for (_, j) in terms]

    def body(*refs):
        i = pl.program_id(0)
        acc = None
        for k, j in enumerate(shifts):
            x = refs[2 * k][...]
            if j > 0:
                halo = jnp.where(i == 0, 0.0, refs[2 * k + 1][...])
                x = pltpu.roll(jnp.concatenate([halo, x], axis=0), j, 0)[8:, :]
            elif j < 0:
                halo = jnp.where(i == nt - 1, 0.0, refs[2 * k + 1][...])
                x = pltpu.roll(jnp.concatenate([x, halo], axis=0), tm + 8 + j, 0)[:tm, :]
            acc = x if acc is None else acc + x
        refs[-1][...] = acc

    in_specs, args = [], []
    for (arr, _, cb), j in terms:
        in_specs.append(pl.BlockSpec((tm, w), lambda i, cb=cb: (i, cb)))
        if j > 0:
            in_specs.append(pl.BlockSpec((8, w), lambda i, cb=cb: (jnp.maximum(i * nb8 - 1, 0), cb)))
        else:
            in_specs.append(pl.BlockSpec((8, w), lambda i, cb=cb: (jnp.minimum((i + 1) * nb8, R // 8 - 1), cb)))
        args += [arr, arr]
    return pl.pallas_call(
        body, name=name, grid=(nt,), in_specs=in_specs, out_specs=pl.BlockSpec((tm, w), lambda i: (i, 0)),
        out_shape=jax.ShapeDtypeStruct((R, w), F32), compiler_params=_params(("parallel",)),
    )(*args)


def _softplus(x):
    return jnp.maximum(x, 0.0) + jnp.log(1.0 + jnp.exp(-jnp.abs(x)))


def _silu(x):
    return x * jax.nn.sigmoid(x)


def _rms_fn(h, w):
    return (h * lax.rsqrt(jnp.mean(h * h, axis=-1, keepdims=True) + NORM_EPS) * w,)


def _rw_prep_fn(pr, pr1, pk, pk1, pv, pv1, ps, ps1, mu_r, mu_k, mu_v, mu_s, w0, w2p, a0, a2p, k_k, k_a):
    r = pr + (pr1 - pr) * mu_r
    k = pk + (pk1 - pk) * mu_k
    v = pv + (pv1 - pv) * mu_v
    s = ps + (ps1 - ps) * mu_s
    w_log = -_softplus(-(w0 + jnp.dot(jnp.tanh(s), w2p, precision=HIGH, preferred_element_type=F32))) - 0.5
    log_decay = -jnp.exp(w_log)
    a = jax.nn.sigmoid(a0 + jnp.dot(s, a2p, precision=HIGH, preferred_element_type=F32))
    return r, log_decay, k * (1.0 + (a - 1.0) * k_a), v, k * k_k, a


def _conv_fn(u0, u1, u2, u3, w0, w1, w2, w3):
    return (_silu(u0 * w3 + u1 * w2 + u2 * w1 + u3 * w0),)


def _dn_gate_fn(ps2, a_log_n, dt_n):
    beta_n = jax.nn.sigmoid(ps2)
    g_n = -jnp.exp(a_log_n) * _softplus(ps2 + dt_n)
    lane = lax.broadcasted_iota(jnp.int32, (1, DN_HEAD), 1)
    pick = lambda x, j: jnp.broadcast_to(jnp.sum(x * (lane == j).astype(F32), axis=-1, keepdims=True), x.shape)[None]
    beta = jnp.concatenate([pick(beta_n, h) for h in range(DN_HEADS)], axis=0)
    g = jnp.concatenate([pick(g_n, DN_HEADS + h) for h in range(DN_HEADS)], axis=0)
    return beta, g


def _tril_masks(c):
    t = lax.broadcasted_iota(jnp.int32, (c, c), 0)
    s = lax.broadcasted_iota(jnp.int32, (c, c), 1)
    return s <= t, s < t


def _dot(x, y, cx, cy, prec=HIGH):
    nb = x.ndim - 2
    batch = tuple(range(nb))
    return lax.dot_general(x, y, (((cx + nb,), (cy + nb,)), (batch, batch)), precision=prec,
                           preferred_element_type=F32)


def _tri_inv(a):
    c = a.shape[-1]
    eye = (lax.broadcasted_iota(jnp.int32, (c, c), 0) == lax.broadcasted_iota(jnp.int32, (c, c), 1)).astype(F32)
    x = eye - a
    p = a
    n = 2
    while n < c:
        p = _dot(p, p, 1, 0)
        x = x + _dot(x, p, 1, 0)
        n *= 2
    return x


@jax.custom_vjp
def _tri_inv_saved(a, t):
    return t


def _tri_inv_saved_fwd(a, t):
    return t, t


def _tri_inv_saved_bwd(t, dt):
    return -_dot(_dot(t, dt, 0, 0), t, 1, 1), jnp.zeros_like(t)


_tri_inv_saved.defvjp(_tri_inv_saved_fwd, _tri_inv_saved_bwd)


def _pair_masks():
    lane = lax.broadcasted_iota(jnp.int32, (1, 2 * RW_HEAD), 1)
    m0 = (lane < RW_HEAD).astype(F32)
    return m0, 1.0 - m0


def _pair_bd(x):
    m0, m1 = _pair_masks()
    return jnp.concatenate([x * m0, x * m1], axis=-2)


def _pair_mm(x, y):
    return _dot(x, _pair_bd(y), 1, 0)


def _pair_inv(a):
    c = a.shape[-2]
    row = lax.broadcasted_iota(jnp.int32, (c, 2 * RW_HEAD), 0)
    col = lax.broadcasted_iota(jnp.int32, (c, 2 * RW_HEAD), 1) & (RW_HEAD - 1)
    x = (row == col).astype(F32) - a
    p = a
    n = 2
    while n < c:
        p = _pair_mm(p, p)
        x = x + _pair_mm(x, p)
        n *= 2
    return x


@jax.custom_vjp
def _pair_inv_saved(a, t):
    return t


def _pair_inv_saved_fwd(a, t):
    return t, t


def _pair_inv_saved_bwd(t, dt):
    m0, m1 = _pair_masks()
    c = t.shape[-2]
    z = _dot(t, dt, 0, 0)
    x = z[:, :c, :] * m0 + z[:, c:, :] * m1
    return -_dot(x, _pair_bd(t), 1, 1), jnp.zeros_like(t)


_pair_inv_saved.defvjp(_pair_inv_saved_fwd, _pair_inv_saved_bwd)


def _rw_chunk_fn(S, r, lw, k2, v, kkp, a, gate, rk, gnw, gnb, t_saved=None):
    B, C, P = r.shape
    m0, m1 = _pair_masks()
    seg = lambda x: (jnp.sum(x * m0, axis=-1, keepdims=True) * m0 + jnp.sum(x * m1, axis=-1, keepdims=True) * m1)
    mm = lambda x, y: _dot(x.astype(BF16), y.astype(BF16), 1, 0, None)
    nt = lambda x, y: _dot(x, y, 1, 1)
    tn = lambda x, y: _dot(x, y, 0, 0)
    pair_mm = lambda x, y: mm(x, _pair_bd(y))
    t_idx = lax.broadcasted_iota(jnp.int32, (C, P), 0)
    s_idx = lax.broadcasted_iota(jnp.int32, (C, P), 1) & (RW_HEAD - 1)
    incl, strict = s_idx <= t_idx, s_idx < t_idx
    tril = jnp.broadcast_to(_tril_masks(C)[0].astype(F32), (B, C, C))
    kk = kkp * lax.rsqrt(seg(kkp * kkp) + 1e-6)
    b = kk * a
    g_incl = _dot(tril, lw, 1, 0)
    g_excl = g_incl - lw
    inv = jnp.exp(-g_incl)
    alpha, beta, kappa, rho = kk * jnp.exp(g_excl), b * inv, k2 * inv, r * jnp.exp(g_incl)
    ar = jnp.concatenate([alpha, rho], axis=-2)
    scores = nt(ar, jnp.concatenate([_pair_bd(beta), _pair_bd(kappa)], axis=-2))
    a_ab = jnp.where(strict, scores[:, :C, :P], 0.0)
    a_ak = jnp.where(strict, scores[:, :C, P:], 0.0)
    r_b = jnp.where(incl, scores[:, C:, :P], 0.0)
    r_k = jnp.where(incl, scores[:, C:, P:], 0.0)
    t_inv = _pair_inv(a_ab) if t_saved is None else _pair_inv_saved(a_ab, t_saved)
    on_state = nt(ar, S)
    u = pair_mm(t_inv, -on_state[:, :C, :] - pair_mm(a_ak, v))
    y = on_state[:, C:, :] + mm(jnp.concatenate([r_b, r_k], axis=-1),
                                jnp.concatenate([_pair_bd(u), _pair_bd(v)], axis=-2))
    same_head = ((lax.broadcasted_iota(jnp.int32, (P, P), 0) < RW_HEAD)
                 == (lax.broadcasted_iota(jnp.int32, (P, P), 1) < RW_HEAD))
    fresh = tn(jnp.concatenate([u, v], axis=-2), jnp.concatenate([beta, kappa], axis=-2))
    S_new = jnp.exp(jnp.sum(lw, axis=-2, keepdims=True)) * (S + jnp.where(same_head, fresh, 0.0))
    dev = y - seg(y) * (1.0 / RW_HEAD)
    yn = dev * lax.rsqrt(seg(dev * dev) * (1.0 / RW_HEAD) + RW_GN_EPS) * gnw + gnb
    bonus = seg(r * k2 * rk) * v
    return (yn + bonus) * _silu(gate), S_new, t_inv


def _dn_chunk_fn(S, qc, kc, vc, bb, gb, z, nw, t_saved=None):
    B, C, D = qc.shape
    mm = lambda x, y: _dot(x.astype(BF16), y.astype(BF16), 1, 0, None)
    nt = lambda x, y, p=None: _dot(x, y, 1, 1, p) if p else _dot(x.astype(BF16), y.astype(BF16), 1, 1, None)
    tn = lambda x, y: _dot(x.astype(BF16), y.astype(BF16), 0, 0, None)
    incl, strict = _tril_masks(C)
    q = qc * lax.rsqrt(jnp.sum(qc * qc, axis=-1, keepdims=True) + 1e-6) * (D ** -0.5)
    k = kc * lax.rsqrt(jnp.sum(kc * kc, axis=-1, keepdims=True) + 1e-6)
    kb, vb = k * bb, vc * bb
    G = _dot(jnp.broadcast_to(incl.astype(F32), (B, C, C)), gb, 1, 0, HIGHEST)
    lane = lax.broadcasted_iota(jnp.int32, (C, D), 1)
    e0, e1 = (lane == 0).astype(F32), (lane == 1).astype(F32)
    diff = nt(G * e0 + e1, e0 - G * e1, HIGHEST)
    dmask = jnp.where(incl, jnp.exp(jnp.where(incl, diff, 0.0)), 0.0)
    M = jnp.where(strict, nt(kb, k) * dmask, 0.0)
    T = _tri_inv(M) if t_saved is None else _tri_inv_saved(M, t_saved)
    eG = jnp.exp(G)
    u = mm(T, vb)
    w = mm(T, kb * eG)
    attn = jnp.where(incl, nt(q, k) * dmask, 0.0)
    v_new = u - mm(w, S)
    o = mm(q * eG, S) + mm(attn, v_new)
    g_last = jnp.sum(gb, axis=-2, keepdims=True)
    S_new = S * jnp.exp(jnp.broadcast_to(g_last, S.shape)) + tn(k * jnp.exp(g_last - G), v_new)
    on = o * lax.rsqrt(jnp.mean(o * o, axis=-1, keepdims=True) + NORM_EPS) * nw
    return on * _silu(z), S_new, T


N_GROUPS = 8
GROUP = 128
HALO = 8


def _groups(x):
    return jnp.concatenate([x[:, g * GROUP:(g + 1) * GROUP][None] for g in range(N_GROUPS)], axis=0)


@functools.partial(jax.custom_vjp, nondiff_argnums=(1,))
def _shift_rows(ext, j):
    return pltpu.roll(ext, j, 0)[HALO:, :]


def _shift_rows_fwd(ext, j):
    return _shift_rows(ext, j), None


def _shift_rows_bwd(j, _, d):
    z = jnp.concatenate([jnp.zeros((HALO, d.shape[1]), d.dtype), d], axis=0)
    return (pltpu.roll(z, z.shape[0] - j, 0),)


_shift_rows.defvjp(_shift_rows_fwd, _shift_rows_bwd)


def _rw_fused_fn(S, h_r, h_k, h_v, h_s, p_r, p_k, p_v, p_s, gate, *pars, t_saved=None):
    prev = lambda h, x: _shift_rows(jnp.concatenate([h, x], axis=0), 1)
    seq = _rw_prep_fn(p_r, prev(h_r, p_r), p_k, prev(h_k, p_k), p_v, prev(h_v, p_v), p_s, prev(h_s, p_s), *pars[:10])
    return _rw_chunk_fn(S, *[_groups(t) for t in seq], _groups(gate), *[_groups(t) for t in pars[10:]],
                        t_saved=t_saved)


def _dn_fused_fn(S, h_q, h_k, h_v, p_q, p_k, p_v, p_s2, z, *pars, t_saved=None):
    conv = []
    for i, (h, x) in enumerate(((h_q, p_q), (h_k, p_k), (h_v, p_v))):
        ext = jnp.concatenate([h, x], axis=0)
        conv += _conv_fn(x, _shift_rows(ext, 1), _shift_rows(ext, 2), _shift_rows(ext, 3), *pars[4 * i:4 * i + 4])
    beta, g = _dn_gate_fn(p_s2, pars[12], pars[13])
    return _dn_chunk_fn(S, *[_groups(t) for t in conv], beta, g, _groups(z), _groups(pars[14]), t_saved=t_saved)


def _chunk_fwd(name, fn, rows, n_halo, pars, t_lanes):
    Lp = rows[0][0].shape[0]
    C, D, NB = CHUNK, GROUP, N_GROUPS
    nc, n_r, n_p, per = Lp // C, len(rows), len(pars), CHUNK // HALO

    def body(*refs):
        row_refs, halo_refs, par_refs = refs[:n_r], refs[n_r:n_r + n_halo], refs[n_r + n_halo:n_r + n_halo + n_p]
        y_ref, ck_ref, t_ref, s_ref = refs[n_r + n_halo + n_p:]
        first = pl.program_id(0) == 0

        @pl.when(first)
        def _():
            s_ref[...] = jnp.zeros_like(s_ref)

        S = s_ref[...]
        ck_ref[0] = S
        halos = [jnp.where(first, 0.0, r[...]) for r in halo_refs]
        y, S_new, t_inv = fn(S, *halos, *[r[...] for r in row_refs], *[r[...] for r in par_refs])
        for g in range(NB):
            y_ref[:, g * D:(g + 1) * D] = y[g].astype(BF16)
        t_ref[0] = t_inv
        s_ref[...] = S_new

    return pl.pallas_call(
        body, name=name, grid=(nc,),
        in_specs=[pl.BlockSpec((C, w), lambda c, cb=cb: (c, cb)) for (_, w, cb) in rows]
        + [pl.BlockSpec((HALO, w), lambda c, cb=cb: (jnp.maximum(c * per - 1, 0), cb)) for (_, w, cb) in rows[:n_halo]]
        + [pl.BlockSpec(p.shape, lambda c: (0, 0)) for p in pars],
        out_specs=[pl.BlockSpec((C, NB * D), lambda c: (c, 0)),
                   pl.BlockSpec((1, NB, D, D), lambda c: (c, 0, 0, 0)),
                   pl.BlockSpec((1, NB, C, t_lanes), lambda c: (c, 0, 0, 0))],
        out_shape=[jax.ShapeDtypeStruct((Lp, NB * D), BF16), jax.ShapeDtypeStruct((nc, NB, D, D), F32),
                   jax.ShapeDtypeStruct((nc, NB, C, t_lanes), F32)],
        scratch_shapes=[pltpu.VMEM((NB, D, D), F32)],
        compiler_params=_params(("arbitrary",)),
    )(*[r[0] for r in rows], *[r[0] for r in rows[:n_halo]], *pars)


def _chunk_bwd(name, fn, rows, n_halo, pars, ckpt, tinv, dy):
    Lp = rows[0][0].shape[0]
    C, D, NB = CHUNK, GROUP, N_GROUPS
    nc, n_r, n_p, per = Lp // C, len(rows), len(pars), CHUNK // HALO
    t_lanes = tinv.shape[-1]
    n_in = n_r + n_halo + n_p

    def body(*refs):
        row_refs, halo_refs, par_refs = refs[:n_r], refs[n_r:n_r + n_halo], refs[n_r + n_halo:n_in]
        ck_ref, t_ref, dy_ref = refs[n_in:n_in + 3]
        outs = refs[n_in + 3:]
        drow_refs, dpar_refs = outs[:n_r], outs[n_r:n_r + n_p]
        ds_ref, carry_refs = outs[n_r + n_p], outs[n_r + n_p + 1:]
        i = pl.program_id(0)

        @pl.when(i == 0)
        def _():
            ds_ref[...] = jnp.zeros_like(ds_ref)
            for o_ref in list(dpar_refs) + list(carry_refs):
                o_ref[...] = jnp.zeros_like(o_ref)

        halos = [jnp.where(i == nc - 1, 0.0, r[...]) for r in halo_refs]
        vals = [ck_ref[0]] + halos + [r[...] for r in row_refs] + [r[...] for r in par_refs]
        t_saved = t_ref[0]
        grads = jax.vjp(lambda *v: fn(*v, t_saved=t_saved)[:2], *vals)[1]((_groups(dy_ref), ds_ref[...]))
        ds_ref[...] = grads[0]
        d_halos, d_rows, d_pars = grads[1:1 + n_halo], grads[1 + n_halo:1 + n_halo + n_r], grads[1 + n_halo + n_r:]
        for k, (o_ref, g) in enumerate(zip(drow_refs, d_rows)):
            if k < n_halo:
                g = g + jnp.concatenate([jnp.zeros((C - HALO, g.shape[1]), F32), carry_refs[k][...]], axis=0)
                carry_refs[k][...] = d_halos[k]
            o_ref[...] = g.astype(BF16)
        for o_ref, g in zip(dpar_refs, d_pars):
            o_ref[...] += g

    rev = lambda c: nc - 1 - c
    dy_arr, dy_w, dy_cb = dy
    res = pl.pallas_call(
        body, name=name, grid=(nc,),
        in_specs=[pl.BlockSpec((C, w), lambda c, cb=cb: (rev(c), cb)) for (_, w, cb) in rows]
        + [pl.BlockSpec((HALO, w), lambda c, cb=cb: (jnp.maximum(rev(c) * per - 1, 0), cb))
           for (_, w, cb) in rows[:n_halo]]
        + [pl.BlockSpec(p.shape, lambda c: (0, 0)) for p in pars]
        + [pl.BlockSpec((1, NB, D, D), lambda c: (rev(c), 0, 0, 0)),
           pl.BlockSpec((1, NB, C, t_lanes), lambda c: (rev(c), 0, 0, 0)),
           pl.BlockSpec((C, dy_w), lambda c: (rev(c), dy_cb))],
        out_specs=[pl.BlockSpec((C, w), lambda c: (rev(c), 0)) for (_, w, _) in rows]
        + [pl.BlockSpec(p.shape, lambda c: (0, 0)) for p in pars],
        out_shape=[jax.ShapeDtypeStruct((Lp, w), BF16) for (_, w, _) in rows]
        + [jax.ShapeDtypeStruct(p.shape, F32) for p in pars],
        scratch_shapes=[pltpu.VMEM((NB, D, D), F32)] + [pltpu.VMEM((HALO, w), F32) for (_, w, _) in rows[:n_halo]],
        compiler_params=_params(("arbitrary",)),
    )(*[r[0] for r in rows], *[r[0] for r in rows[:n_halo]], *pars, ckpt, tinv, dy_arr)
    return res[:n_r], res[n_r:]


def _loss_head(h, yw, tgt, fw, n_real, tm=2 * ROW_TILE):
    Lp, Dm = h.shape
    tm = _pick(Lp, tm, 16)

    def out_fn(z, fw_):
        return z * lax.rsqrt(jnp.mean(z * z, axis=-1, keepdims=True) + NORM_EPS) * fw_

    def body(h_ref, yw_ref, t_ref, fw_ref, loss_ref, dz_ref, dz16_ref, dfw_ref):
        i = pl.program_id(0)

        @pl.when(i == 0)
        def _():
            loss_ref[...] = jnp.zeros_like(loss_ref)
            dfw_ref[...] = jnp.zeros_like(dfw_ref)

        row = i * tm + lax.broadcasted_iota(jnp.int32, (tm, 1), 0)
        mask = ((row >= N_META) & (row < n_real)).astype(F32)
        z = h_ref[...] + yw_ref[...]
        o, vjp = jax.vjp(out_fn, z, fw_ref[...])
        err = (o - t_ref[...]) * mask
        row_loss = 0.5 * jnp.mean(jnp.square(err), axis=-1, keepdims=True)
        dz, dfw = vjp(err * (1.0 / Dm))
        loss_ref[...] += jnp.sum(row_loss, axis=0, keepdims=True)
        dz_ref[...] = dz
        dz16_ref[...] = dz.astype(BF16)
        dfw_ref[...] += dfw

    row_spec = pl.BlockSpec((tm, Dm), lambda i: (i, 0))
    return pl.pallas_call(
        body, name="loss_head", grid=(Lp // tm,),
        in_specs=[row_spec, row_spec, row_spec, pl.BlockSpec((1, Dm), lambda i: (0, 0))],
        out_specs=[pl.BlockSpec((8, 128), lambda i: (0, 0)), row_spec, row_spec,
                   pl.BlockSpec((1, Dm), lambda i: (0, 0))],
        out_shape=[jax.ShapeDtypeStruct((8, 128), F32), jax.ShapeDtypeStruct((Lp, Dm), F32),
                   jax.ShapeDtypeStruct((Lp, Dm), BF16), jax.ShapeDtypeStruct((1, Dm), F32)],
        compiler_params=_params(("arbitrary",)),
    )(h, yw, tgt, fw)


def _exchange(name, x, masks, slot_kind, per_dest, n_split=1, copy_own=True, other_half=False):
    n = len(masks)
    keep_own = slot_kind is not None and copy_own
    n_slots = {"chip": 4, "core": 2, "dev": 8, None: n}[slot_kind]
    blk_shape = x.shape[1:] if per_dest else x.shape
    if other_half:
        blk_shape = (x.shape[0] // 2,) + tuple(x.shape[1:])
    rows = blk_shape[0] // n_split

    def body(x_ref, o_ref, send_sems, recv_sems, local_sems):
        mx, my, mc = lax.axis_index("x"), lax.axis_index("y"), lax.axis_index("c")
        if other_half:
            x_ref = x_ref.at[pl.ds((1 - mc) * blk_shape[0], blk_shape[0])]

        def slot(k, px, py, pc):
            return {"chip": 2 * px + py, "core": pc, "dev": 4 * px + 2 * py + pc, None: k}[slot_kind]

        def peer(m):
            return (mx + m[0]) % 2, (my + m[1]) % 2, (mc + m[2]) % 2

        def part(ref, j):
            return ref.at[pl.ds(j * rows, rows)]

        own_src = x_ref.at[2 * mx + my] if per_dest else x_ref
        local = []
        if keep_own:
            own_dst = o_ref.at[slot(0, mx, my, mc)]
            local = [pltpu.make_async_copy(part(own_src, j), part(own_dst, j), local_sems.at[j])
                     for j in range(n_split)]
        for cp in local:
            cp.start()
        sends = []
        for k, m in enumerate(masks):
            px, py, pc = peer(m)
            src = x_ref.at[2 * px + py] if per_dest else x_ref
            dst = o_ref.at[slot(k, mx, my, mc)]
            for j in range(n_split):
                sends.append(pltpu.make_async_remote_copy(
                    src_ref=part(src, j), dst_ref=part(dst, j), send_sem=send_sems.at[k * n_split + j],
                    recv_sem=recv_sems.at[k * n_split + j], device_id=(px, py, pc), device_id_type=MESH))
        for cp in sends:
            cp.start()
        for k, m in enumerate(masks):
            px, py, pc = peer(m)
            landed = o_ref.at[slot(k, px, py, pc)]
            for j in range(n_split):
                pltpu.make_async_remote_copy(
                    src_ref=part(own_src, j), dst_ref=part(landed, j), send_sem=send_sems.at[k * n_split + j],
                    recv_sem=recv_sems.at[k * n_split + j], device_id=(px, py, pc), device_id_type=MESH).wait_recv()
        for cp in sends:
            cp.wait_send()
        for cp in local:
            cp.wait()

    return pl.pallas_call(
        body, name=name,
        in_specs=[pl.BlockSpec(memory_space=pl.ANY)], out_specs=pl.BlockSpec(memory_space=pl.ANY),
        out_shape=jax.ShapeDtypeStruct((n_slots,) + tuple(blk_shape), x.dtype),
        scratch_shapes=[pltpu.SemaphoreType.DMA((n * n_split,)), pltpu.SemaphoreType.DMA((n * n_split,)),
                        pltpu.SemaphoreType.DMA((n_split,))],
        compiler_params=pltpu.CompilerParams(has_side_effects=True),
    )(x)


CHIP_MASKS = [(1, 0, 0), (0, 1, 0), (1, 1, 0)]
CORE_MASKS = [(0, 0, 1)]
ALL_MASKS = [(dx, dy, dc) for dx in (0, 1) for dy in (0, 1) for dc in (0, 1) if (dx, dy, dc) != (0, 0, 0)]

HBM_SPEC = pl.BlockSpec(memory_space=pltpu.HBM)
SEM_SPEC = pl.BlockSpec(memory_space=pltpu.SEMAPHORE)
DATAFLOW = pltpu.SideEffectType.DATAFLOW_SIDE_EFFECTING


def _split_copies(x_ref, land_ref, send_sems, recv_sems, masks, slot_kind, per_dest, n_split):
    mx, my, mc = lax.axis_index("x"), lax.axis_index("y"), lax.axis_index("c")
    slot = lambda px, py, pc: {"chip": 2 * px + py, "core": pc}[slot_kind]
    rows = (x_ref.shape[1] if per_dest else x_ref.shape[0]) // n_split
    part = lambda ref, j: ref.at[pl.ds(j * rows, rows)]
    sends, recvs = [], []
    for k, m in enumerate(masks):
        px, py, pc = (mx + m[0]) % 2, (my + m[1]) % 2, (mc + m[2]) % 2
        src = x_ref.at[2 * px + py] if per_dest else x_ref
        own_src = x_ref.at[2 * mx + my] if per_dest else x_ref
        for j in range(n_split):
            sems = dict(send_sem=send_sems.at[k * n_split + j], recv_sem=recv_sems.at[k * n_split + j],
                        device_id=(px, py, pc), device_id_type=MESH)
            sends.append(pltpu.make_async_remote_copy(
                src_ref=part(src, j), dst_ref=part(land_ref.at[slot(mx, my, mc)], j), **sems))
            recvs.append(pltpu.make_async_remote_copy(
                src_ref=part(own_src, j), dst_ref=part(land_ref.at[slot(px, py, pc)], j), **sems))
    return sends, recvs


def _exchange_start(name, x, masks, slot_kind, per_dest, n_split):
    n = len(masks) * n_split
    blk_shape = x.shape[1:] if per_dest else x.shape
    land_shape = ({"chip": 4, "core": 2}[slot_kind],) + tuple(blk_shape)

    def body(x_ref, land_ref, send_sems, recv_sems, x_thru, land_thru, token):
        for cp in _split_copies(x_ref, land_ref, send_sems, recv_sems, masks, slot_kind, per_dest, n_split)[0]:
            cp.start()
        token[...] = jnp.zeros_like(token)

    return pl.pallas_call(
        body, name=name,
        out_shape=(pltpu.SemaphoreType.DMA((n,)), pltpu.SemaphoreType.DMA((n,)), pltpu.HBM(x.shape, x.dtype),
                   pltpu.HBM(land_shape, x.dtype), jax.ShapeDtypeStruct((8, 128), F32)),
        in_specs=(HBM_SPEC, HBM_SPEC),
        out_specs=(SEM_SPEC, SEM_SPEC, HBM_SPEC, HBM_SPEC, pl.BlockSpec(memory_space=pltpu.VMEM)),
        input_output_aliases={0: 2, 1: 3},
        compiler_params=pltpu.CompilerParams(has_side_effects=DATAFLOW),
    )(pltpu.with_memory_space_constraint(x, pltpu.HBM),
      pltpu.with_memory_space_constraint(lax.empty(land_shape, x.dtype), pltpu.HBM))


def _exchange_wait(name, started, after, masks, slot_kind, per_dest, n_split):
    send_sems, recv_sems, x_thru, land_thru, _ = started

    def body(x_ref, land_ref, send_sems, recv_sems, after_ref, x_out, land_out):
        sends, recvs = _split_copies(x_ref, land_ref, send_sems, recv_sems, masks, slot_kind, per_dest, n_split)
        for cp in sends:
            cp.wait_send()
        for cp in recvs:
            cp.wait_recv()

    return pl.pallas_call(
        body, name=name,
        out_shape=(pltpu.HBM(x_thru.shape, x_thru.dtype), pltpu.HBM(land_thru.shape, land_thru.dtype)),
        in_specs=(HBM_SPEC, HBM_SPEC, SEM_SPEC, SEM_SPEC, pl.BlockSpec(memory_space=pl.ANY)),
        out_specs=(HBM_SPEC, HBM_SPEC), input_output_aliases={0: 0, 1: 1},
        compiler_params=pltpu.CompilerParams(has_side_effects=DATAFLOW),
    )(x_thru, land_thru, send_sems, recv_sems, after)


def _gather_chips(name, x):
    return _exchange(name, x, CHIP_MASKS, "chip", False)


def _gather_shards(name, shard, n_split, overlap=False):
    half = shard.shape[0] // 2
    mine = lax.dynamic_slice_in_dim(shard, lax.axis_index("c") * half, half, axis=0)
    if overlap:
        return _exchange_start(name + "_chips", mine, CHIP_MASKS, "chip", False, n_split)
    by_chip = _exchange(name + "_chips", mine, CHIP_MASKS, "chip", False, n_split, copy_own=False)
    return _gather_tail(name, mine, by_chip)


def _gather_tail(name, mine, by_chip):
    c, chip = lax.axis_index("c"), 2 * lax.axis_index("x") + lax.axis_index("y")
    by_chip = lax.dynamic_update_index_in_dim(by_chip, mine, chip, 0)
    both = _exchange(name + "_cores", by_chip, CORE_MASKS, "core", False, N_CHIPS, copy_own=False)
    return lax.dynamic_update_index_in_dim(both, by_chip, c, 0)


def _gather_finish(name, started, after, n_split):
    mine, by_chip = _exchange_wait(name + "_chips_wait", started, after, CHIP_MASKS, "chip", False, n_split)
    return _gather_tail(name, mine, by_chip)


def _sum_slots(name, x, tr=128):
    S, R, N = x.shape
    tr = _pick(R, tr, 16)

    def body(x_ref, o_ref):
        acc = x_ref[0].astype(F32)
        for s in range(1, S):
            acc = acc + x_ref[s].astype(F32)
        o_ref[...] = acc

    return pl.pallas_call(
        body, name=name, grid=(R // tr,),
        in_specs=[pl.BlockSpec((S, tr, N), lambda i: (0, i, 0))], out_specs=pl.BlockSpec((tr, N), lambda i: (i, 0)),
        out_shape=jax.ShapeDtypeStruct((R, N), F32), compiler_params=_params(("parallel",)),
    )(x)


def _add_to_bf16(name, a, b, tr=128):
    S, R, N = a.shape
    tr = _pick(R, tr, 16)

    def body(a_ref, b_ref, o_ref):
        o_ref[...] = (a_ref[...] + b_ref[...]).astype(BF16)

    spec = pl.BlockSpec((S, tr, N), lambda i: (0, i, 0))
    return pl.pallas_call(
        body, name=name, grid=(R // tr,), in_specs=[spec, spec], out_specs=spec,
        out_shape=jax.ShapeDtypeStruct((S, R, N), BF16), compiler_params=_params(("parallel",)),
    )(a, b)


def _add_slabs_to_bf16(name, full, recv, c, tr=64):
    R, NP = full.shape
    half = R // 2
    nb = half // tr

    def body(c_ref, a_ref, b_ref, o_ref):
        x = a_ref[...] + b_ref[...]
        for s in range(N_CHIPS):
            o_ref[s] = x[:, s * SHARD_COLS:(s + 1) * SHARD_COLS].astype(BF16)

    grid_spec = pltpu.PrefetchScalarGridSpec(
        num_scalar_prefetch=1, grid=(nb,),
        in_specs=[pl.BlockSpec((tr, NP), lambda i, c_ref: (c_ref[0] * nb + i, 0)),
                  pl.BlockSpec((tr, NP), lambda i, c_ref: (i, 0))],
        out_specs=pl.BlockSpec((N_CHIPS, tr, SHARD_COLS), lambda i, c_ref: (0, i, 0)))
    return pl.pallas_call(
        body, name=name, grid_spec=grid_spec,
        out_shape=jax.ShapeDtypeStruct((N_CHIPS, half, SHARD_COLS), BF16), compiler_params=_params(("parallel",)),
    )(jnp.reshape(c, (1,)).astype(jnp.int32), full, recv)


def _adamw(name, gparts, w, m, v, tr=128):
    S, R, N = gparts.shape
    tr = _pick(R, tr)
    c1 = 1.0 / (1.0 - ADAM_B1 ** ADAM_STEP)
    c2 = 1.0 / (1.0 - ADAM_B2 ** ADAM_STEP)

    def body(g_ref, w_ref, m_ref, v_ref, go_ref, d_ref, mo_ref, vo_ref):
        g = g_ref[0]
        for s in range(1, S):
            g = g + g_ref[s]
        m_new = ADAM_B1 * m_ref[...] + (1.0 - ADAM_B1) * g
        v_new = ADAM_B2 * v_ref[...] + (1.0 - ADAM_B2) * jnp.square(g)
        go_ref[...] = g
        mo_ref[...] = m_new
        vo_ref[...] = v_new
        d_ref[...] = -ADAM_LR * ((m_new * c1) / (jnp.sqrt(v_new * c2) + ADAM_EPS) + ADAM_WD * w_ref[...])

    spec = pl.BlockSpec((tr, N), lambda i: (i, 0))
    return pl.pallas_call(
        body, name=name, grid=(R // tr,),
        in_specs=[pl.BlockSpec((S, tr, N), lambda i: (0, i, 0)), spec, spec, spec], out_specs=[spec] * 4,
        out_shape=[jax.ShapeDtypeStruct((R, N), F32)] * 4, compiler_params=_params(("parallel",)),
    )(gparts, w, m, v)


def _reduce_to_shard(name, slabs, n_split, by_columns=False, overlap=False):
    c, chip = lax.axis_index("c"), 2 * lax.axis_index("x") + lax.axis_index("y")
    if by_columns:
        R, N = slabs.shape[0], SHARD_COLS
        from_sibling = _exchange(name + "_sib", slabs, CORE_MASKS, None, False, n_split, other_half=True)[0]
        wire = _add_slabs_to_bf16(name + "_add", slabs, from_sibling, c)
    else:
        _, R, N = slabs.shape
        half = R // 2
        halves = slabs.reshape(N_CHIPS, 2, half, N)
        mine = lax.dynamic_index_in_dim(halves, c, axis=1, keepdims=False)
        theirs = lax.dynamic_index_in_dim(halves, 1 - c, axis=1, keepdims=False)
        from_sibling = _exchange(name + "_sib", theirs, CORE_MASKS, None, False, N_CHIPS)[0]
        wire = _add_to_bf16(name + "_add", mine, from_sibling)
    if overlap:
        return _exchange_start(name + "_chips", wire, CHIP_MASKS, "chip", True, n_split), (R, N)
    got = _exchange(name + "_chips", wire, CHIP_MASKS, "chip", True, n_split, copy_own=False)
    return _reduce_tail(name, wire, got, n_split, R, N)


def _reduce_tail(name, wire, got, n_split, R, N):
    c, chip = lax.axis_index("c"), 2 * lax.axis_index("x") + lax.axis_index("y")
    got = lax.dynamic_update_index_in_dim(got, lax.dynamic_index_in_dim(wire, chip, 0, keepdims=False), chip, 0)
    part = _sum_slots(name + "_sum", got)
    both = _exchange(name + "_cores", part, CORE_MASKS, "core", False, n_split, copy_own=False)
    return lax.dynamic_update_index_in_dim(both, part, c, 0).reshape(1, R, N)


def _reduce_finish(name, started, after, n_split):
    handle, (R, N) = started
    wire, got = _exchange_wait(name + "_chips_wait", handle, after, CHIP_MASKS, "chip", True, n_split)
    return _reduce_tail(name, wire, got, n_split, R, N)


def _pack(pieces, cols, row_mult=8):
    flat = jnp.concatenate([p.reshape(-1) for p in pieces])
    rows = -(-flat.shape[0] // cols)
    rows = -(-rows // row_mult) * row_mult
    return jnp.pad(flat, (0, rows * cols - flat.shape[0])).reshape(rows, cols)


def _unpack(packed, shapes):
    flat = packed.reshape(-1)
    out, off = [], 0
    for shp in shapes:
        n = 1
        for d in shp:
            n *= d
        out.append(flat[off:off + n].reshape(shp))
        off += n
    return out


def _to_my_layout(w):
    z = jnp.zeros((w.shape[0], 112), w.dtype)
    return jnp.concatenate([w[:, 0:3072], w[:, 3200:4224], w[:, 4224:7296], w[:, 7312:8336],
                            w[:, 3072:3200], w[:, 7296:7312], z], axis=1)


def kernel(x, meta_tokens, norm_w, w_in, rw_shift_mu, rw_w0, rw_w2, rw_a0, rw_a2, rw_k_k, rw_k_a, rw_r_k, rw_gn_w, rw_gn_b, dn_conv_w, dn_A_log, dn_dt_bias, dn_norm_w, w_out, final_norm_w, loss_target, m_meta_tokens, m_norm_w, m_w_in, m_rw_shift_mu, m_rw_w0, m_rw_w2, m_rw_a0, m_rw_a2, m_rw_k_k, m_rw_k_a, m_rw_r_k, m_rw_gn_w, m_rw_gn_b, m_dn_conv_w, m_dn_A_log, m_dn_dt_bias, m_dn_norm_w, m_w_out, m_final_norm_w, v_meta_tokens, v_norm_w, v_w_in, v_rw_shift_mu, v_rw_w0, v_rw_w2, v_rw_a0, v_rw_a2, v_rw_k_k, v_rw_k_a, v_rw_r_k, v_rw_gn_w, v_rw_gn_b, v_dn_conv_w, v_dn_A_log, v_dn_dt_bias, v_dn_norm_w, v_w_out, v_final_norm_w):
    S = x.shape[1]
    L = N_META + S
    Lp = -(-L // CHUNK) * CHUNK

    small_shapes = [(RW_LORA, 256), (RW_LORA, 256), (CONV_W, 768), (N_META, 512)]
    small_mine = _pack([rw_w2[0], rw_a2[0], dn_conv_w[0], meta_tokens], 1024)
    small_all = _gather_chips("gather_small", small_mine)
    per_chip = [_unpack(small_all[s], small_shapes) for s in range(N_CHIPS)]
    w2, a2, conv_w, meta = [jnp.concatenate([per_chip[s][i] for s in range(N_CHIPS)], axis=1) for i in range(4)]
    w_in_all = _gather_shards("gather_w_in", w_in[0].astype(BF16), 8)
    W_orig = jnp.transpose(w_in_all, (0, 2, 1, 3)).reshape(D_MODEL, IN_COLS)
    W = _to_my_layout(W_orig)
    W_orig_t = jnp.pad(W_orig, ((0, 0), (0, NP_COLS - IN_COLS))).T
    w_out_started = _gather_shards("gather_w_out", w_out[0].astype(BF16), 8, overlap=True)

    tail = [jnp.zeros((Lp - L, D_MODEL), F32)] if Lp > L else []
    h = jnp.concatenate([meta, x[0]] + tail, axis=0)
    tgt = jnp.concatenate([jnp.zeros((N_META, D_MODEL), F32), loss_target[0]] + tail, axis=0)
    (u,) = _rowwise("rms_in", _rms_fn, [_row(h)], [norm_w], [D_MODEL], tm=2 * ROW_TILE, out_dtype=BF16)
    p = _mm("in_proj", u, W, "nn", dep=w_out_started[4])

    mu = rw_shift_mu
    zpad = jnp.zeros((RW_LORA, RW_WIDTH), F32)
    rw_params = [mu[:, 0:1024], mu[:, 1024:2048], mu[:, 2048:3072], mu[:, 3072:3200], rw_w0,
                 jnp.concatenate([w2, zpad], axis=0), rw_a0, jnp.concatenate([zpad, a2], axis=0), rw_k_k, rw_k_a]
    rw_rows = [_row(p, 1024, CB_R), _row(p, 1024, CB_K), _row(p, 1024, CB_V), _row(p, 128, CB_S1),
               _row(p, 1024, CB_GATE)]
    rw_pars = rw_params + [rw_r_k, rw_gn_w, rw_gn_b]
    ya, rw_ck, rw_t = _chunk_fwd("rw_chunk_fwd", _rw_fused_fn, rw_rows, 4, rw_pars, GROUP)

    dn_rows = [_row(p, 1024, CB_DQ), _row(p, 1024, CB_DK), _row(p, 1024, CB_DV), _row(p, 128, CB_S2),
               _row(p, 1024, CB_Z)]
    dn_pars = [conv_w[j:j + 1, 1024 * i:1024 * (i + 1)] for i in range(3) for j in range(CONV_W)]
    narrow = lambda t: jnp.pad(t, ((0, 0), (DN_HEADS, DN_HEAD - 2 * DN_HEADS)))
    dn_pars += [narrow(dn_A_log), narrow(dn_dt_bias), jnp.tile(dn_norm_w, (1, DN_HEADS))]
    yb, dn_ck, dn_t = _chunk_fwd("dn_chunk_fwd", _dn_fused_fn, dn_rows, 3, dn_pars, CHUNK)

    y = jnp.concatenate([ya, yb], axis=1)
    w_out_all = _gather_finish("gather_w_out", w_out_started, y, 8)
    Wo = jnp.transpose(w_out_all, (1, 0, 2, 3)).reshape(D_MODEL, D_MODEL)
    yw = _mm("out_proj", y, Wo, "nn", tn=1024)
    loss_acc, dz, dz16, d_fw = _loss_head(h, yw, tgt, final_norm_w.reshape(1, D_MODEL), L)
    loss = lax.psum(loss_acc[0, 0], ("x", "y", "c"))

    d_wo = _mm("d_w_out", y.T, dz16, "nn", tm=512, tn=1024, tk=Lp)
    wo_started = _reduce_to_shard("rs_w_out", d_wo.reshape(N_CHIPS, D_MODEL // N_CHIPS, D_MODEL), 8, overlap=True)
    dy = _mm("d_out_proj", dz16, Wo.T, "nn", tn=1024, tk=2048, dep=wo_started[0][4])

    dp_rw, d_rw_pars = _chunk_bwd("rw_chunk_bwd", _rw_fused_fn, rw_rows, 4, rw_pars, rw_ck, rw_t, _row(dy, 1024, 0))
    d_prep_pars, d_rw_pars = d_rw_pars[:10], d_rw_pars[10:]
    dp_dn, d_dn_pars = _chunk_bwd("dn_chunk_bwd", _dn_fused_fn, dn_rows, 3, dn_pars, dn_ck, dn_t, _row(dy, 1024, 1))
    d_conv_parts = [jnp.concatenate(d_dn_pars[4 * i:4 * i + 4], axis=0) for i in range(3)]
    d_a_log_b, d_dt_b = d_dn_pars[12], d_dn_pars[13]
    d_dn_nw = jnp.sum(d_dn_pars[14].reshape(DN_HEADS, DN_HEAD), axis=0, keepdims=True)

    dp = jnp.concatenate(list(dp_rw[:3]) + [dp_rw[3], dp_rw[4]] + list(dp_dn[:3])
                         + [dp_dn[3][:, :2 * DN_HEADS], dp_dn[4], jnp.zeros((Lp, NP_COLS - IN_COLS), BF16)], axis=1)
    d_W = _mm("d_w_in", u.T, dp, "nn", tm=1024, tn=768, tk=Lp)
    wi_started = _reduce_to_shard("rs_w_in", d_W, 8, by_columns=True, overlap=True)
    du = _mm("d_in_proj", dp, W_orig_t, "nn", tn=2048, tk=1408, dep=wi_started[0][4])
    (dh,), (d_norm_w,) = _rowwise_bwd("rms_in_bwd", lambda h_, w_: (_rms_fn(h_, w_)[0], h_), [_row(h)], [norm_w],
                                      [_row(du), _row(dz)])
    grad_x = dh[N_META:L][None]

    d_mu = jnp.concatenate(d_prep_pars[0:4], axis=1)
    d_w2, d_a2 = d_prep_pars[5][:RW_LORA], d_prep_pars[7][RW_LORA:]
    d_conv = jnp.concatenate(d_conv_parts, axis=1)
    d_meta = dh[:N_META]
    head_sum = lambda t: t[:, DN_HEADS:2 * DN_HEADS]
    rep_names = ["norm_w", "rw_shift_mu", "rw_w0", "rw_a0", "rw_k_k", "rw_k_a", "rw_r_k", "rw_gn_w", "rw_gn_b",
                 "dn_A_log", "dn_dt_bias", "dn_norm_w", "final_norm_w"]
    rep_g = [d_norm_w, d_mu, d_prep_pars[4], d_prep_pars[6], d_prep_pars[8], d_prep_pars[9],
             d_rw_pars[0], d_rw_pars[1], d_rw_pars[2],
             head_sum(d_a_log_b), head_sum(d_dt_b), d_dn_nw, d_fw.reshape(D_MODEL)]
    rep_w = [norm_w, rw_shift_mu, rw_w0, rw_a0, rw_k_k, rw_k_a, rw_r_k, rw_gn_w, rw_gn_b, dn_A_log, dn_dt_bias,
             dn_norm_w, final_norm_w]
    rep_m = [m_norm_w, m_rw_shift_mu, m_rw_w0, m_rw_a0, m_rw_k_k, m_rw_k_a, m_rw_r_k, m_rw_gn_w, m_rw_gn_b,
             m_dn_A_log, m_dn_dt_bias, m_dn_norm_w, m_final_norm_w]
    rep_v = [v_norm_w, v_rw_shift_mu, v_rw_w0, v_rw_a0, v_rw_k_k, v_rw_k_a, v_rw_r_k, v_rw_gn_w, v_rw_gn_b,
             v_dn_A_log, v_dn_dt_bias, v_dn_norm_w, v_final_norm_w]
    rep_shapes = [t.shape for t in rep_w]
    rep_all = _exchange("gather_rep_grads", _pack(rep_g, 128), ALL_MASKS, "dev", False)
    rep_out = _adamw("adam_rep", rep_all, _pack(rep_w, 128), _pack(rep_m, 128), _pack(rep_v, 128))
    rep_out = [dict(zip(rep_names, _unpack(t, rep_shapes))) for t in rep_out]

    sm_slabs = jnp.stack([_pack([d_w2[:, 256 * s:256 * (s + 1)], d_a2[:, 256 * s:256 * (s + 1)],
                                 d_conv[:, 768 * s:768 * (s + 1)], d_meta[:, 512 * s:512 * (s + 1)]], 1024, 64)
                          for s in range(N_CHIPS)])
    sm_parts = _reduce_to_shard("rs_small", sm_slabs, 1)
    sm_w = [rw_w2[0], rw_a2[0], dn_conv_w[0], meta_tokens]
    sm_m = [m_rw_w2[0], m_rw_a2[0], m_dn_conv_w[0], m_meta_tokens]
    sm_v = [v_rw_w2[0], v_rw_a2[0], v_dn_conv_w[0], v_meta_tokens]
    sm_out = _adamw("adam_small", sm_parts, _pack(sm_w, 1024, 64), _pack(sm_m, 1024, 64), _pack(sm_v, 1024, 64))
    sm_names = ["rw_w2", "rw_a2", "dn_conv_w", "meta_tokens"]
    sm_full_shapes = [(1, RW_LORA, 256), (1, RW_LORA, 256), (1, CONV_W, 768), (N_META, 512)]
    sm_out = [dict(zip(sm_names, [t.reshape(shp) for t, shp in zip(_unpack(o, small_shapes), sm_full_shapes)]))
              for o in sm_out]

    wo_parts = _reduce_finish("rs_w_out", wo_started, dp, 8)
    wo_out = _adamw("adam_w_out", wo_parts, w_out[0], m_w_out[0], v_w_out[0])
    wi_parts = _reduce_finish("rs_w_in", wi_started, dh, 8)
    wi_out = _adamw("adam_w_in", wi_parts, w_in[0], m_w_in[0], v_w_in[0])

    order = ["meta_tokens", "norm_w", "w_in", "rw_shift_mu", "rw_w0", "rw_w2", "rw_a0", "rw_a2", "rw_k_k", "rw_k_a",
             "rw_r_k", "rw_gn_w", "rw_gn_b", "dn_conv_w", "dn_A_log", "dn_dt_bias", "dn_norm_w", "w_out",
             "final_norm_w"]
    outs = [loss, grad_x]
    for kind in range(4):
        table = dict(rep_out[kind])
        table.update(sm_out[kind])
        table["w_in"] = wi_out[kind][None]
        table["w_out"] = wo_out[kind][None]
        outs += [table[n] for n in order]
    return tuple(outs)
```

```python
import functools

import jax
import jax.numpy as jnp
from jax import lax
from jax.experimental import pallas as pl
from jax.experimental.pallas import tpu as pltpu

F32 = jnp.float32
BF16 = jnp.bfloat16
HIGH = lax.Precision.HIGH
HIGHEST = lax.Precision.HIGHEST
MESH = pl.DeviceIdType.MESH

D_MODEL = 2048
N_META = 16
RW_WIDTH = 1024
RW_HEAD = 64
RW_HEADS = 16
RW_LORA = 64
RW_GN_EPS = 64e-5
DN_WIDTH = 1024
DN_HEAD = 128
DN_HEADS = 8
CONV_W = 4
CHUNK = 64
NORM_EPS = 1e-6
IN_COLS = 8336
N_CHIPS = 4
SHARD_COLS = IN_COLS // N_CHIPS

NP_COLS = 8 * 1024 + 256
GAP_AT = IN_COLS - DN_WIDTH
GAP = NP_COLS - IN_COLS
CB_R, CB_K, CB_V, CB_GATE, CB_DQ, CB_DK, CB_DV, CB_Z = range(8)
CB_S1 = 8192 // 128
CB_S2 = CB_S1 + 1

ADAM_LR = 0.001
ADAM_B1 = 0.9
ADAM_B2 = 0.999
ADAM_EPS = 1e-08
ADAM_WD = 0.01
ADAM_STEP = 10

VMEM_LIMIT_BYTES = 56 * 1024 * 1024
ROW_TILE = 104
MM_ROW_TILE = 832


def _params(sem=None):
    return pltpu.CompilerParams(dimension_semantics=sem, vmem_limit_bytes=VMEM_LIMIT_BYTES)


def _pick(n, target, mult=8):
    best = None
    for d in range(mult, min(n, target) + 1, mult):
        if n % d == 0:
            best = d
    return n if best is None else best


def _mm(name, a, b, mode, tm=MM_ROW_TILE, tn=1408, tk=2048, dep=None):
    if mode == "nn":
        (M, K), (_, N) = a.shape, b.shape
    elif mode == "nt":
        (M, K), (N, _) = a.shape, b.shape
    else:
        (K, M), (_, N) = a.shape, b.shape
    tm = _pick(M, tm, 128 if mode == "tn" else 16)
    tn = _pick(N, tn, 128)
    tk = _pick(K, tk, 8 if mode == "tn" else 128)
    if mode == "nn":
        a_spec = pl.BlockSpec((tm, tk), lambda i, j, k: (i, k))
        b_spec = pl.BlockSpec((tk, tn), lambda i, j, k: (k, j))
        dims = (((1,), (0,)), ((), ()))
    elif mode == "nt":
        a_spec = pl.BlockSpec((tm, tk), lambda i, j, k: (i, k))
        b_spec = pl.BlockSpec((tn, tk), lambda i, j, k: (j, k))
        dims = (((1,), (1,)), ((), ()))
    else:
        a_spec = pl.BlockSpec((tk, tm), lambda i, j, k: (k, i))
        b_spec = pl.BlockSpec((tk, tn), lambda i, j, k: (k, j))
        dims = (((0,), (0,)), ((), ()))

    def body(a_ref, b_ref, *rest):
        o_ref = rest[-1]

        @pl.when(pl.program_id(2) == 0)
        def _():
            o_ref[...] = jnp.zeros_like(o_ref)

        o_ref[...] += lax.dot_general(a_ref[...].astype(BF16), b_ref[...].astype(BF16), dims,
                                      preferred_element_type=F32)

    deps = [] if dep is None else [dep]
    return pl.pallas_call(
        body, name=name, grid=(M // tm, N // tn, K // tk),
        in_specs=[a_spec, b_spec] + [pl.BlockSpec((8, 128), lambda i, j, k: (0, 0))] * len(deps),
        out_specs=pl.BlockSpec((tm, tn), lambda i, j, k: (i, j)),
        out_shape=jax.ShapeDtypeStruct((M, N), F32),
        compiler_params=_params(("parallel", "parallel", "arbitrary")),
    )(a, b, *deps)


def _row(arr, width=None, cb=0):
    return (arr, arr.shape[1] if width is None else width, cb)


def _rowwise(name, fn, rows, params, out_widths, tm=ROW_TILE, out_dtype=F32):
    R = rows[0][0].shape[0]
    tm = _pick(R, tm, 16 if out_dtype == BF16 else 8)
    n_r, n_p = len(rows), len(params)

    def body(*refs):
        vals = [r[...] for r in refs[:n_r + n_p]]
        for o_ref, val in zip(refs[n_r + n_p:], fn(*vals)):
            o_ref[...] = val.astype(out_dtype)

    in_specs = [pl.BlockSpec((tm, w), lambda i, cb=cb: (i, cb)) for (_, w, cb) in rows]
    in_specs += [pl.BlockSpec(p.shape, lambda i: (0, 0)) for p in params]
    return pl.pallas_call(
        body, name=name, grid=(R // tm,), in_specs=in_specs,
        out_specs=[pl.BlockSpec((tm, w), lambda i: (i, 0)) for w in out_widths],
        out_shape=[jax.ShapeDtypeStruct((R, w), out_dtype) for w in out_widths],
        compiler_params=_params(("parallel",)),
    )(*[r[0] for r in rows], *params)


def _rowwise_bwd(name, fn, rows, params, douts, tm=ROW_TILE):
    R = rows[0][0].shape[0]
    tm = _pick(R, tm)
    n_r, n_p, n_d = len(rows), len(params), len(douts)

    def body(*refs):
        vals = [r[...] for r in refs[:n_r + n_p]]
        cts = tuple(r[...] for r in refs[n_r + n_p:n_r + n_p + n_d])
        grads = jax.vjp(fn, *vals)[1](cts)
        outs = refs[n_r + n_p + n_d:]
        for o_ref, g in zip(outs[:n_r], grads[:n_r]):
            o_ref[...] = g

        @pl.when(pl.program_id(0) == 0)
        def _():
            for o_ref in outs[n_r:]:
                o_ref[...] = jnp.zeros_like(o_ref)

        for o_ref, g in zip(outs[n_r:], grads[n_r:]):
            o_ref[...] += g

    in_specs = [pl.BlockSpec((tm, w), lambda i, cb=cb: (i, cb)) for (_, w, cb) in rows]
    in_specs += [pl.BlockSpec(p.shape, lambda i: (0, 0)) for p in params]
    in_specs += [pl.BlockSpec((tm, w), lambda i, cb=cb: (i, cb)) for (_, w, cb) in douts]
    out_specs = [pl.BlockSpec((tm, w), lambda i: (i, 0)) for (_, w, _) in rows]
    out_specs += [pl.BlockSpec(p.shape, lambda i: (0, 0)) for p in params]
    out_shape = [jax.ShapeDtypeStruct((R, w), F32) for (_, w, _) in rows]
    out_shape += [jax.ShapeDtypeStruct(p.shape, F32) for p in params]
    res = pl.pallas_call(
        body, name=name, grid=(R // tm,), in_specs=in_specs, out_specs=out_specs, out_shape=out_shape,
        compiler_params=_params(("arbitrary",)),
    )(*[r[0] for r in rows], *params, *[d[0] for d in douts])
    return res[:n_r], res[n_r:]


def _shift_sum(name, terms, tm=ROW_TILE):
    R = terms[0][0][0].shape[0]
    w = terms[0][0][1]
    tm = _pick(R, tm)
    nt, nb8 = R // tm, tm // 8
    shifts = [j for (_, j) in terms]

    def body(*refs):
        i = pl.program_id(0)
        acc = None
        for k, j in enumerate(shifts):
            x = refs[2 * k][...]
            if j > 0:
                halo = jnp.where(i == 0, 0.0, refs[2 * k + 1][...])
                x = pltpu.roll(jnp.concatenate([halo, x], axis=0), j, 0)[8:, :]
            elif j < 0:
                halo = jnp.where(i == nt - 1, 0.0, refs[2 * k + 1][...])
                x = pltpu.roll(jnp.concatenate([x, halo], axis=0), tm + 8 + j, 0)[:tm, :]
            acc = x if acc is None else acc + x
        refs[-1][...] = acc

    in_specs, args = [], []
    for (arr, _, cb), j in terms:
        in_specs.append(pl.BlockSpec((tm, w), lambda i, cb=cb: (i, cb)))
        if j > 0:
            in_specs.append(pl.BlockSpec((8, w), lambda i, cb=cb: (jnp.maximum(i * nb8 - 1, 0), cb)))
        else:
            in_specs.append(pl.BlockSpec((8, w), lambda i, cb=cb: (jnp.minimum((i + 1) * nb8, R // 8 - 1), cb)))
        args += [arr, arr]
    return pl.pallas_call(
        body, name=name, grid=(nt,), in_specs=in_specs, out_specs=pl.BlockSpec((tm, w), lambda i: (i, 0)),
        out_shape=jax.ShapeDtypeStruct((R, w), F32), compiler_params=_params(("parallel",)),
    )(*args)


def _softplus(x):
    return jnp.maximum(x, 0.0) + jnp.log(1.0 + jnp.exp(-jnp.abs(x)))


def _silu(x):
    return x * jax.nn.sigmoid(x)


def _rms_fn(h, w):
    return (h * lax.rsqrt(jnp.mean(h * h, axis=-1, keepdims=True) + NORM_EPS) * w,)


def _rw_prep_fn(pr, pr1, pk, pk1, pv, pv1, ps, ps1, mu_r, mu_k, mu_v, mu_s, w0, w2p, a0, a2p, k_k, k_a):
    r = pr + (pr1 - pr) * mu_r
    k = pk + (pk1 - pk) * mu_k
    v = pv + (pv1 - pv) * mu_v
    s = ps + (ps1 - ps) * mu_s
    w_log = -_softplus(-(w0 + jnp.dot(jnp.tanh(s), w2p, precision=HIGH, preferred_element_type=F32))) - 0.5
    log_decay = -jnp.exp(w_log)
    a = jax.nn.sigmoid(a0 + jnp.dot(s, a2p, precision=HIGH, preferred_element_type=F32))
    return r, log_decay, k * (1.0 + (a - 1.0) * k_a), v, k * k_k, a


def _conv_fn(u0, u1, u2, u3, w0, w1, w2, w3):
    return (_silu(u0 * w3 + u1 * w2 + u2 * w1 + u3 * w0),)


def _dn_gate_fn(ps2, a_log_n, dt_n):
    beta_n = jax.nn.sigmoid(ps2)
    g_n = -jnp.exp(a_log_n) * _softplus(ps2 + dt_n)
    lane = lax.broadcasted_iota(jnp.int32, (1, DN_HEAD), 1)
    pick = lambda x, j: jnp.broadcast_to(jnp.sum(x * (lane == j).astype(F32), axis=-1, keepdims=True), x.shape)[None]
    beta = jnp.concatenate([pick(beta_n, h) for h in range(DN_HEADS)], axis=0)
    g = jnp.concatenate([pick(g_n, DN_HEADS + h) for h in range(DN_HEADS)], axis=0)
    return beta, g


def _tril_masks(c):
    t = lax.broadcasted_iota(jnp.int32, (c, c), 0)
    s = lax.broadcasted_iota(jnp.int32, (c, c), 1)
    return s <= t, s < t


def _dot(x, y, cx, cy, prec=HIGH):
    nb = x.ndim - 2
    batch = tuple(range(nb))
    return lax.dot_general(x, y, (((cx + nb,), (cy + nb,)), (batch, batch)), precision=prec,
                           preferred_element_type=F32)


def _tri_inv(a):
    c = a.shape[-1]
    eye = (lax.broadcasted_iota(jnp.int32, (c, c), 0) == lax.broadcasted_iota(jnp.int32, (c, c), 1)).astype(F32)
    x = eye - a
    p = a
    n = 2
    while n < c:
        p = _dot(p, p, 1, 0)
        x = x + _dot(x, p, 1, 0)
        n *= 2
    return x


@jax.custom_vjp
def _tri_inv_saved(a, t):
    return t


def _tri_inv_saved_fwd(a, t):
    return t, t


def _tri_inv_saved_bwd(t, dt):
    return -_dot(_dot(t, dt, 0, 0), t, 1, 1), jnp.zeros_like(t)


_tri_inv_saved.defvjp(_tri_inv_saved_fwd, _tri_inv_saved_bwd)


def _pair_masks():
    lane = lax.broadcasted_iota(jnp.int32, (1, 2 * RW_HEAD), 1)
    m0 = (lane < RW_HEAD).astype(F32)
    return m0, 1.0 - m0


def _pair_bd(x):
    m0, m1 = _pair_masks()
    return jnp.concatenate([x * m0, x * m1], axis=-2)


def _pair_mm(x, y):
    return _dot(x, _pair_bd(y), 1, 0)


def _pair_inv(a):
    c = a.shape[-2]
    row = lax.broadcasted_iota(jnp.int32, (c, 2 * RW_HEAD), 0)
    col = lax.broadcasted_iota(jnp.int32, (c, 2 * RW_HEAD), 1) & (RW_HEAD - 1)
    x = (row == col).astype(F32) - a
    p = a
    n = 2
    while n < c:
        p = _pair_mm(p, p)
        x = x + _pair_mm(x, p)
        n *= 2
    return x


@jax.custom_vjp
def _pair_inv_saved(a, t):
    return t


def _pair_inv_saved_fwd(a, t):
    return t, t


def _pair_inv_saved_bwd(t, dt):
    m0, m1 = _pair_masks()
    c = t.shape[-2]
    z = _dot(t, dt, 0, 0)
    x = z[:, :c, :] * m0 + z[:, c:, :] * m1
    return -_dot(x, _pair_bd(t), 1, 1), jnp.zeros_like(t)


_pair_inv_saved.defvjp(_pair_inv_saved_fwd, _pair_inv_saved_bwd)


def _rw_chunk_fn(S, r, lw, k2, v, kkp, a, gate, rk, gnw, gnb, t_saved=None):
    B, C, P = r.shape
    m0, m1 = _pair_masks()
    seg = lambda x: (jnp.sum(x * m0, axis=-1, keepdims=True) * m0 + jnp.sum(x * m1, axis=-1, keepdims=True) * m1)
    mm = lambda x, y: _dot(x.astype(BF16), y.astype(BF16), 1, 0, None)
    nt = lambda x, y: _dot(x, y, 1, 1)
    tn = lambda x, y: _dot(x, y, 0, 0)
    pair_mm = lambda x, y: mm(x, _pair_bd(y))
    t_idx = lax.broadcasted_iota(jnp.int32, (C, P), 0)
    s_idx = lax.broadcasted_iota(jnp.int32, (C, P), 1) & (RW_HEAD - 1)
    incl, strict = s_idx <= t_idx, s_idx < t_idx
    tril = jnp.broadcast_to(_tril_masks(C)[0].astype(F32), (B, C, C))
    kk = kkp * lax.rsqrt(seg(kkp * kkp) + 1e-6)
    b = kk * a
    g_incl = _dot(tril, lw, 1, 0)
    g_excl = g_incl - lw
    inv = jnp.exp(-g_incl)
    alpha, beta, kappa, rho = kk * jnp.exp(g_excl), b * inv, k2 * inv, r * jnp.exp(g_incl)
    ar = jnp.concatenate([alpha, rho], axis=-2)
    scores = nt(ar, jnp.concatenate([_pair_bd(beta), _pair_bd(kappa)], axis=-2))
    a_ab = jnp.where(strict, scores[:, :C, :P], 0.0)
    a_ak = jnp.where(strict, scores[:, :C, P:], 0.0)
    r_b = jnp.where(incl, scores[:, C:, :P], 0.0)
    r_k = jnp.where(incl, scores[:, C:, P:], 0.0)
    t_inv = _pair_inv(a_ab) if t_saved is None else _pair_inv_saved(a_ab, t_saved)
    on_state = nt(ar, S)
    u = pair_mm(t_inv, -on_state[:, :C, :] - pair_mm(a_ak, v))
    y = on_state[:, C:, :] + mm(jnp.concatenate([r_b, r_k], axis=-1),
                                jnp.concatenate([_pair_bd(u), _pair_bd(v)], axis=-2))
    same_head = ((lax.broadcasted_iota(jnp.int32, (P, P), 0) < RW_HEAD)
                 == (lax.broadcasted_iota(jnp.int32, (P, P), 1) < RW_HEAD))
    fresh = tn(jnp.concatenate([u, v], axis=-2), jnp.concatenate([beta, kappa], axis=-2))
    S_new = jnp.exp(jnp.sum(lw, axis=-2, keepdims=True)) * (S + jnp.where(same_head, fresh, 0.0))
    dev = y - seg(y) * (1.0 / RW_HEAD)
    yn = dev * lax.rsqrt(seg(dev * dev) * (1.0 / RW_HEAD) + RW_GN_EPS) * gnw + gnb
    bonus = seg(r * k2 * rk) * v
    return (yn + bonus) * _silu(gate), S_new, t_inv


def _dn_chunk_fn(S, qc, kc, vc, bb, gb, z, nw, t_saved=None):
    B, C, D = qc.shape
    mm = lambda x, y: _dot(x.astype(BF16), y.astype(BF16), 1, 0, None)
    nt = lambda x, y, p=None: _dot(x, y, 1, 1, p) if p else _dot(x.astype(BF16), y.astype(BF16), 1, 1, None)
    tn = lambda x, y: _dot(x.astype(BF16), y.astype(BF16), 0, 0, None)
    incl, strict = _tril_masks(C)
    q = qc * lax.rsqrt(jnp.sum(qc * qc, axis=-1, keepdims=True) + 1e-6) * (D ** -0.5)
    k = kc * lax.rsqrt(jnp.sum(kc * kc, axis=-1, keepdims=True) + 1e-6)
    kb, vb = k * bb, vc * bb
    G = _dot(jnp.broadcast_to(incl.astype(F32), (B, C, C)), gb, 1, 0, HIGHEST)
    lane = lax.broadcasted_iota(jnp.int32, (C, D), 1)
    e0, e1 = (lane == 0).astype(F32), (lane == 1).astype(F32)
    diff = nt(G * e0 + e1, e0 - G * e1, HIGHEST)
    dmask = jnp.where(incl, jnp.exp(jnp.where(incl, diff, 0.0)), 0.0)
    M = jnp.where(strict, nt(kb, k) * dmask, 0.0)
    T = _tri_inv(M) if t_saved is None else _tri_inv_saved(M, t_saved)
    eG = jnp.exp(G)
    u = mm(T, vb)
    w = mm(T, kb * eG)
    attn = jnp.where(incl, nt(q, k) * dmask, 0.0)
    v_new = u - mm(w, S)
    o = mm(q * eG, S) + mm(attn, v_new)
    g_last = jnp.sum(gb, axis=-2, keepdims=True)
    S_new = S * jnp.exp(jnp.broadcast_to(g_last, S.shape)) + tn(k * jnp.exp(g_last - G), v_new)
    on = o * lax.rsqrt(jnp.mean(o * o, axis=-1, keepdims=True) + NORM_EPS) * nw
    return on * _silu(z), S_new, T


N_GROUPS = 8
GROUP = 128
HALO = 8


def _groups(x):
    return jnp.concatenate([x[:, g * GROUP:(g + 1) * GROUP][None] for g in range(N_GROUPS)], axis=0)


@functools.partial(jax.custom_vjp, nondiff_argnums=(1,))
def _shift_rows(ext, j):
    return pltpu.roll(ext, j, 0)[HALO:, :]


def _shift_rows_fwd(ext, j):
    return _shift_rows(ext, j), None


def _shift_rows_bwd(j, _, d):
    z = jnp.concatenate([jnp.zeros((HALO, d.shape[1]), d.dtype), d], axis=0)
    return (pltpu.roll(z, z.shape[0] - j, 0),)


_shift_rows.defvjp(_shift_rows_fwd, _shift_rows_bwd)


def _rw_fused_fn(S, h_r, h_k, h_v, h_s, p_r, p_k, p_v, p_s, gate, *pars, t_saved=None):
    prev = lambda h, x: _shift_rows(jnp.concatenate([h, x], axis=0), 1)
    seq = _rw_prep_fn(p_r, prev(h_r, p_r), p_k, prev(h_k, p_k), p_v, prev(h_v, p_v), p_s, prev(h_s, p_s), *pars[:10])
    return _rw_chunk_fn(S, *[_groups(t) for t in seq], _groups(gate), *[_groups(t) for t in pars[10:]],
                        t_saved=t_saved)


def _dn_fused_fn(S, h_q, h_k, h_v, p_q, p_k, p_v, p_s2, z, *pars, t_saved=None):
    conv = []
    for i, (h, x) in enumerate(((h_q, p_q), (h_k, p_k), (h_v, p_v))):
        ext = jnp.concatenate([h, x], axis=0)
        conv += _conv_fn(x, _shift_rows(ext, 1), _shift_rows(ext, 2), _shift_rows(ext, 3), *pars[4 * i:4 * i + 4])
    beta, g = _dn_gate_fn(p_s2, pars[12], pars[13])
    return _dn_chunk_fn(S, *[_groups(t) for t in conv], beta, g, _groups(z), _groups(pars[14]), t_saved=t_saved)


def _chunk_fwd(name, fn, rows, n_halo, pars, t_lanes):
    Lp = rows[0][0].shape[0]
    C, D, NB = CHUNK, GROUP, N_GROUPS
    nc, n_r, n_p, per = Lp // C, len(rows), len(pars), CHUNK // HALO

    def body(*refs):
        row_refs, halo_refs, par_refs = refs[:n_r], refs[n_r:n_r + n_halo], refs[n_r + n_halo:n_r + n_halo + n_p]
        y_ref, ck_ref, t_ref, s_ref = refs[n_r + n_halo + n_p:]
        first = pl.program_id(0) == 0

        @pl.when(first)
        def _():
            s_ref[...] = jnp.zeros_like(s_ref)

        S = s_ref[...]
        ck_ref[0] = S
        halos = [jnp.where(first, 0.0, r[...]) for r in halo_refs]
        y, S_new, t_inv = fn(S, *halos, *[r[...] for r in row_refs], *[r[...] for r in par_refs])
        for g in range(NB):
            y_ref[:, g * D:(g + 1) * D] = y[g].astype(BF16)
        t_ref[0] = t_inv
        s_ref[...] = S_new

    return pl.pallas_call(
        body, name=name, grid=(nc,),
        in_specs=[pl.BlockSpec((C, w), lambda c, cb=cb: (c, cb)) for (_, w, cb) in rows]
        + [pl.BlockSpec((HALO, w), lambda c, cb=cb: (jnp.maximum(c * per - 1, 0), cb)) for (_, w, cb) in rows[:n_halo]]
        + [pl.BlockSpec(p.shape, lambda c: (0, 0)) for p in pars],
        out_specs=[pl.BlockSpec((C, NB * D), lambda c: (c, 0)),
                   pl.BlockSpec((1, NB, D, D), lambda c: (c, 0, 0, 0)),
                   pl.BlockSpec((1, NB, C, t_lanes), lambda c: (c, 0, 0, 0))],
        out_shape=[jax.ShapeDtypeStruct((Lp, NB * D), BF16), jax.ShapeDtypeStruct((nc, NB, D, D), F32),
                   jax.ShapeDtypeStruct((nc, NB, C, t_lanes), F32)],
        scratch_shapes=[pltpu.VMEM((NB, D, D), F32)],
        compiler_params=_params(("arbitrary",)),
    )(*[r[0] for r in rows], *[r[0] for r in rows[:n_halo]], *pars)


def _chunk_bwd(name, fn, rows, n_halo, pars, ckpt, tinv, dy):
    Lp = rows[0][0].shape[0]
    C, D, NB = CHUNK, GROUP, N_GROUPS
    nc, n_r, n_p, per = Lp // C, len(rows), len(pars), CHUNK // HALO
    t_lanes = tinv.shape[-1]
    n_in = n_r + n_halo + n_p

    def body(*refs):
        row_refs, halo_refs, par_refs = refs[:n_r], refs[n_r:n_r + n_halo], refs[n_r + n_halo:n_in]
        ck_ref, t_ref, dy_ref = refs[n_in:n_in + 3]
        outs = refs[n_in + 3:]
        drow_refs, dpar_refs = outs[:n_r], outs[n_r:n_r + n_p]
        ds_ref, carry_refs = outs[n_r + n_p], outs[n_r + n_p + 1:]
        i = pl.program_id(0)

        @pl.when(i == 0)
        def _():
            ds_ref[...] = jnp.zeros_like(ds_ref)
            for o_ref in list(dpar_refs) + list(carry_refs):
                o_ref[...] = jnp.zeros_like(o_ref)

        halos = [jnp.where(i == nc - 1, 0.0, r[...]) for r in halo_refs]
        vals = [ck_ref[0]] + halos + [r[...] for r in row_refs] + [r[...] for r in par_refs]
        t_saved = t_ref[0]
        grads = jax.vjp(lambda *v: fn(*v, t_saved=t_saved)[:2], *vals)[1]((_groups(dy_ref), ds_ref[...]))
        ds_ref[...] = grads[0]
        d_halos, d_rows, d_pars = grads[1:1 + n_halo], grads[1 + n_halo:1 + n_halo + n_r], grads[1 + n_halo + n_r:]
        for k, (o_ref, g) in enumerate(zip(drow_refs, d_rows)):
            if k < n_halo:
                g = g + jnp.concatenate([jnp.zeros((C - HALO, g.shape[1]), F32), carry_refs[k][...]], axis=0)
                carry_refs[k][...] = d_halos[k]
            o_ref[...] = g.astype(BF16)
        for o_ref, g in zip(dpar_refs, d_pars):
            o_ref[...] += g

    rev = lambda c: nc - 1 - c
    dy_arr, dy_w, dy_cb = dy
    res = pl.pallas_call(
        body, name=name, grid=(nc,),
        in_specs=[pl.BlockSpec((C, w), lambda c, cb=cb: (rev(c), cb)) for (_, w, cb) in rows]
        + [pl.BlockSpec((HALO, w), lambda c, cb=cb: (jnp.maximum(rev(c) * per - 1, 0), cb))
           for (_, w, cb) in rows[:n_halo]]
        + [pl.BlockSpec(p.shape, lambda c: (0, 0)) for p in pars]
        + [pl.BlockSpec((1, NB, D, D), lambda c: (rev(c), 0, 0, 0)),
           pl.BlockSpec((1, NB, C, t_lanes), lambda c: (rev(c), 0, 0, 0)),
           pl.BlockSpec((C, dy_w), lambda c: (rev(c), dy_cb))],
        out_specs=[pl.BlockSpec((C, w), lambda c: (rev(c), 0)) for (_, w, _) in rows]
        + [pl.BlockSpec(p.shape, lambda c: (0, 0)) for p in pars],
        out_shape=[jax.ShapeDtypeStruct((Lp, w), BF16) for (_, w, _) in rows]
        + [jax.ShapeDtypeStruct(p.shape, F32) for p in pars],
        scratch_shapes=[pltpu.VMEM((NB, D, D), F32)] + [pltpu.VMEM((HALO, w), F32) for (_, w, _) in rows[:n_halo]],
        compiler_params=_params(("arbitrary",)),
    )(*[r[0] for r in rows], *[r[0] for r in rows[:n_halo]], *pars, ckpt, tinv, dy_arr)
    return res[:n_r], res[n_r:]


def _loss_head(h, yw, tgt, fw, n_real, tm=2 * ROW_TILE):
    Lp, Dm = h.shape
    tm = _pick(Lp, tm, 16)

    def out_fn(z, fw_):
        return z * lax.rsqrt(jnp.mean(z * z, axis=-1, keepdims=True) + NORM_EPS) * fw_

    def body(h_ref, yw_ref, t_ref, fw_ref, loss_ref, dz_ref, dz16_ref, dfw_ref):
        i = pl.program_id(0)

        @pl.when(i == 0)
        def _():
            loss_ref[...] = jnp.zeros_like(loss_ref)
            dfw_ref[...] = jnp.zeros_like(dfw_ref)

        row = i * tm + lax.broadcasted_iota(jnp.int32, (tm, 1), 0)
        mask = ((row >= N_META) & (row < n_real)).astype(F32)
        z = h_ref[...] + yw_ref[...]
        o, vjp = jax.vjp(out_fn, z, fw_ref[...])
        err = (o - t_ref[...]) * mask
        row_loss = 0.5 * jnp.mean(jnp.square(err), axis=-1, keepdims=True)
        dz, dfw = vjp(err * (1.0 / Dm))
        loss_ref[...] += jnp.sum(row_loss, axis=0, keepdims=True)
        dz_ref[...] = dz
        dz16_ref[...] = dz.astype(BF16)
        dfw_ref[...] += dfw

    row_spec = pl.BlockSpec((tm, Dm), lambda i: (i, 0))
    return pl.pallas_call(
        body, name="loss_head", grid=(Lp // tm,),
        in_specs=[row_spec, row_spec, row_spec, pl.BlockSpec((1, Dm), lambda i: (0, 0))],
        out_specs=[pl.BlockSpec((8, 128), lambda i: (0, 0)), row_spec, row_spec,
                   pl.BlockSpec((1, Dm), lambda i: (0, 0))],
        out_shape=[jax.ShapeDtypeStruct((8, 128), F32), jax.ShapeDtypeStruct((Lp, Dm), F32),
                   jax.ShapeDtypeStruct((Lp, Dm), BF16), jax.ShapeDtypeStruct((1, Dm), F32)],
        compiler_params=_params(("arbitrary",)),
    )(h, yw, tgt, fw)


def _exchange(name, x, masks, slot_kind, per_dest, n_split=1, copy_own=True, other_half=False):
    n = len(masks)
    keep_own = slot_kind is not None and copy_own
    n_slots = {"chip": 4, "core": 2, "dev": 8, None: n}[slot_kind]
    blk_shape = x.shape[1:] if per_dest else x.shape
    if other_half:
        blk_shape = (x.shape[0] // 2,) + tuple(x.shape[1:])
    rows = blk_shape[0] // n_split

    def body(x_ref, o_ref, send_sems, recv_sems, local_sems):
        mx, my, mc = lax.axis_index("x"), lax.axis_index("y"), lax.axis_index("c")
        if other_half:
            x_ref = x_ref.at[pl.ds((1 - mc) * blk_shape[0], blk_shape[0])]

        def slot(k, px, py, pc):
            return {"chip": 2 * px + py, "core": pc, "dev": 4 * px + 2 * py + pc, None: k}[slot_kind]

        def peer(m):
            return (mx + m[0]) % 2, (my + m[1]) % 2, (mc + m[2]) % 2

        def part(ref, j):
            return ref.at[pl.ds(j * rows, rows)]

        own_src = x_ref.at[2 * mx + my] if per_dest else x_ref
        local = []
        if keep_own:
            own_dst = o_ref.at[slot(0, mx, my, mc)]
            local = [pltpu.make_async_copy(part(own_src, j), part(own_dst, j), local_sems.at[j])
                     for j in range(n_split)]
        for cp in local:
            cp.start()
        sends = []
        for k, m in enumerate(masks):
            px, py, pc = peer(m)
            src = x_ref.at[2 * px + py] if per_dest else x_ref
            dst = o_ref.at[slot(k, mx, my, mc)]
            for j in range(n_split):
                sends.append(pltpu.make_async_remote_copy(
                    src_ref=part(src, j), dst_ref=part(dst, j), send_sem=send_sems.at[k * n_split + j],
                    recv_sem=recv_sems.at[k * n_split + j], device_id=(px, py, pc), device_id_type=MESH))
        for cp in sends:
            cp.start()
        for k, m in enumerate(masks):
            px, py, pc = peer(m)
            landed = o_ref.at[slot(k, px, py, pc)]
            for j in range(n_split):
                pltpu.make_async_remote_copy(
                    src_ref=part(own_src, j), dst_ref=part(landed, j), send_sem=send_sems.at[k * n_split + j],
                    recv_sem=recv_sems.at[k * n_split + j], device_id=(px, py, pc), device_id_type=MESH).wait_recv()
        for cp in sends:
            cp.wait_send()
        for cp in local:
            cp.wait()

    return pl.pallas_call(
        body, name=name,
        in_specs=[pl.BlockSpec(memory_space=pl.ANY)], out_specs=pl.BlockSpec(memory_space=pl.ANY),
        out_shape=jax.ShapeDtypeStruct((n_slots,) + tuple(blk_shape), x.dtype),
        scratch_shapes=[pltpu.SemaphoreType.DMA((n * n_split,)), pltpu.SemaphoreType.DMA((n * n_split,)),
                        pltpu.SemaphoreType.DMA((n_split,))],
        compiler_params=pltpu.CompilerParams(has_side_effects=True),
    )(x)


CHIP_MASKS = [(1, 0, 0), (0, 1, 0), (1, 1, 0)]
CORE_MASKS = [(0, 0, 1)]
ALL_MASKS = [(dx, dy, dc) for dx in (0, 1) for dy in (0, 1) for dc in (0, 1) if (dx, dy, dc) != (0, 0, 0)]

HBM_SPEC = pl.BlockSpec(memory_space=pltpu.HBM)
SEM_SPEC = pl.BlockSpec(memory_space=pltpu.SEMAPHORE)
DATAFLOW = pltpu.SideEffectType.DATAFLOW_SIDE_EFFECTING


def _split_copies(x_ref, land_ref, send_sems, recv_sems, masks, slot_kind, per_dest, n_split):
    mx, my, mc = lax.axis_index("x"), lax.axis_index("y"), lax.axis_index("c")
    slot = lambda px, py, pc: {"chip": 2 * px + py, "core": pc}[slot_kind]
    rows = (x_ref.shape[1] if per_dest else x_ref.shape[0]) // n_split
    part = lambda ref, j: ref.at[pl.ds(j * rows, rows)]
    sends, recvs = [], []
    for k, m in enumerate(masks):
        px, py, pc = (mx + m[0]) % 2, (my + m[1]) % 2, (mc + m[2]) % 2
        src = x_ref.at[2 * px + py] if per_dest else x_ref
        own_src = x_ref.at[2 * mx + my] if per_dest else x_ref
        for j in range(n_split):
            sems = dict(send_sem=send_sems.at[k * n_split + j], recv_sem=recv_sems.at[k * n_split + j],
                        device_id=(px, py, pc), device_id_type=MESH)
            sends.append(pltpu.make_async_remote_copy(
                src_ref=part(src, j), dst_ref=part(land_ref.at[slot(mx, my, mc)], j), **sems))
            recvs.append(pltpu.make_async_remote_copy(
                src_ref=part(own_src, j), dst_ref=part(land_ref.at[slot(px, py, pc)], j), **sems))
    return sends, recvs


def _exchange_start(name, x, masks, slot_kind, per_dest, n_split):
    n = len(masks) * n_split
    blk_shape = x.shape[1:] if per_dest else x.shape
    land_shape = ({"chip": 4, "core": 2}[slot_kind],) + tuple(blk_shape)

    def body(x_ref, land_ref, send_sems, recv_sems, x_thru, land_thru, token):
        for cp in _split_copies(x_ref, land_ref, send_sems, recv_sems, masks, slot_kind, per_dest, n_split)[0]:
            cp.start()
        token[...] = jnp.zeros_like(token)

    return pl.pallas_call(
        body, name=name,
        out_shape=(pltpu.SemaphoreType.DMA((n,)), pltpu.SemaphoreType.DMA((n,)), pltpu.HBM(x.shape, x.dtype),
                   pltpu.HBM(land_shape, x.dtype), jax.ShapeDtypeStruct((8, 128), F32)),
        in_specs=(HBM_SPEC, HBM_SPEC),
        out_specs=(SEM_SPEC, SEM_SPEC, HBM_SPEC, HBM_SPEC, pl.BlockSpec(memory_space=pltpu.VMEM)),
        input_output_aliases={0: 2, 1: 3},
        compiler_params=pltpu.CompilerParams(has_side_effects=DATAFLOW),
    )(pltpu.with_memory_space_constraint(x, pltpu.HBM),
      pltpu.with_memory_space_constraint(lax.empty(land_shape, x.dtype), pltpu.HBM))


def _exchange_wait(name, started, after, masks, slot_kind, per_dest, n_split):
    send_sems, recv_sems, x_thru, land_thru, _ = started

    def body(x_ref, land_ref, send_sems, recv_sems, after_ref, x_out, land_out):
        sends, recvs = _split_copies(x_ref, land_ref, send_sems, recv_sems, masks, slot_kind, per_dest, n_split)
        for cp in sends:
            cp.wait_send()
        for cp in recvs:
            cp.wait_recv()

    return pl.pallas_call(
        body, name=name,
        out_shape=(pltpu.HBM(x_thru.shape, x_thru.dtype), pltpu.HBM(land_thru.shape, land_thru.dtype)),
        in_specs=(HBM_SPEC, HBM_SPEC, SEM_SPEC, SEM_SPEC, pl.BlockSpec(memory_space=pl.ANY)),
        out_specs=(HBM_SPEC, HBM_SPEC), input_output_aliases={0: 0, 1: 1},
        compiler_params=pltpu.CompilerParams(has_side_effects=DATAFLOW),
    )(x_thru, land_thru, send_sems, recv_sems, after)


def _gather_chips(name, x):
    return _exchange(name, x, CHIP_MASKS, "chip", False)


def _gather_shards(name, shard, n_split, overlap=False):
    half = shard.shape[0] // 2
    mine = lax.dynamic_slice_in_dim(shard, lax.axis_index("c") * half, half, axis=0)
    if overlap:
        return _exchange_start(name + "_chips", mine, CHIP_MASKS, "chip", False, n_split)
    by_chip = _exchange(name + "_chips", mine, CHIP_MASKS, "chip", False, n_split, copy_own=False)
    return _gather_tail(name, mine, by_chip)


def _gather_tail(name, mine, by_chip):
    c, chip = lax.axis_index("c"), 2 * lax.axis_index("x") + lax.axis_index("y")
    by_chip = lax.dynamic_update_index_in_dim(by_chip, mine, chip, 0)
    both = _exchange(name + "_cores", by_chip, CORE_MASKS, "core", False, N_CHIPS, copy_own=False)
    return lax.dynamic_update_index_in_dim(both, by_chip, c, 0)


def _gather_finish(name, started, after, n_split):
    mine, by_chip = _exchange_wait(name + "_chips_wait", started, after, CHIP_MASKS, "chip", False, n_split)
    return _gather_tail(name, mine, by_chip)


def _sum_slots(name, x, tr=128):
    S, R, N = x.shape
    tr = _pick(R, tr, 16)

    def body(x_ref, o_ref):
        acc = x_ref[0].astype(F32)
        for s in range(1, S):
            acc = acc + x_ref[s].astype(F32)
        o_ref[...] = acc

    return pl.pallas_call(
        body, name=name, grid=(R // tr,),
        in_specs=[pl.BlockSpec((S, tr, N), lambda i: (0, i, 0))], out_specs=pl.BlockSpec((tr, N), lambda i: (i, 0)),
        out_shape=jax.ShapeDtypeStruct((R, N), F32), compiler_params=_params(("parallel",)),
    )(x)


def _add_to_bf16(name, a, b, tr=128):
    S, R, N = a.shape
    tr = _pick(R, tr, 16)

    def body(a_ref, b_ref, o_ref):
        o_ref[...] = (a_ref[...] + b_ref[...]).astype(BF16)

    spec = pl.BlockSpec((S, tr, N), lambda i: (0, i, 0))
    return pl.pallas_call(
        body, name=name, grid=(R // tr,), in_specs=[spec, spec], out_specs=spec,
        out_shape=jax.ShapeDtypeStruct((S, R, N), BF16), compiler_params=_params(("parallel",)),
    )(a, b)


def _add_slabs_to_bf16(name, full, recv, c, tr=64):
    R, NP = full.shape
    half = R // 2
    nb = half // tr

    def body(c_ref, a_ref, b_ref, o_ref):
        x = a_ref[...] + b_ref[...]
        for s in range(N_CHIPS - 1):
            o_ref[s] = x[:, s * SHARD_COLS:(s + 1) * SHARD_COLS].astype(BF16)
        last = jnp.concatenate([x[:, (N_CHIPS - 1) * SHARD_COLS:GAP_AT], x[:, GAP_AT + GAP:]], axis=1)
        o_ref[N_CHIPS - 1] = last.astype(BF16)

    grid_spec = pltpu.PrefetchScalarGridSpec(
        num_scalar_prefetch=1, grid=(nb,),
        in_specs=[pl.BlockSpec((tr, NP), lambda i, c_ref: (c_ref[0] * nb + i, 0)),
                  pl.BlockSpec((tr, NP), lambda i, c_ref: (i, 0))],
        out_specs=pl.BlockSpec((N_CHIPS, tr, SHARD_COLS), lambda i, c_ref: (0, i, 0)))
    return pl.pallas_call(
        body, name=name, grid_spec=grid_spec,
        out_shape=jax.ShapeDtypeStruct((N_CHIPS, half, SHARD_COLS), BF16), compiler_params=_params(("parallel",)),
    )(jnp.reshape(c, (1,)).astype(jnp.int32), full, recv)


def _adamw(name, gparts, w, m, v, tr=128):
    S, R, N = gparts.shape
    tr = _pick(R, tr)
    c1 = 1.0 / (1.0 - ADAM_B1 ** ADAM_STEP)
    c2 = 1.0 / (1.0 - ADAM_B2 ** ADAM_STEP)

    def body(g_ref, w_ref, m_ref, v_ref, go_ref, d_ref, mo_ref, vo_ref):
        g = g_ref[0]
        for s in range(1, S):
            g = g + g_ref[s]
        m_new = ADAM_B1 * m_ref[...] + (1.0 - ADAM_B1) * g
        v_new = ADAM_B2 * v_ref[...] + (1.0 - ADAM_B2) * jnp.square(g)
        go_ref[...] = g
        mo_ref[...] = m_new
        vo_ref[...] = v_new
        d_ref[...] = -ADAM_LR * ((m_new * c1) / (jnp.sqrt(v_new * c2) + ADAM_EPS) + ADAM_WD * w_ref[...])

    spec = pl.BlockSpec((tr, N), lambda i: (i, 0))
    return pl.pallas_call(
        body, name=name, grid=(R // tr,),
        in_specs=[pl.BlockSpec((S, tr, N), lambda i: (0, i, 0)), spec, spec, spec], out_specs=[spec] * 4,
        out_shape=[jax.ShapeDtypeStruct((R, N), F32)] * 4, compiler_params=_params(("parallel",)),
    )(gparts, w, m, v)


def _reduce_to_shard(name, slabs, n_split, by_columns=False, overlap=False):
    c, chip = lax.axis_index("c"), 2 * lax.axis_index("x") + lax.axis_index("y")
    if by_columns:
        R, N = slabs.shape[0], SHARD_COLS
        from_sibling = _exchange(name + "_sib", slabs, CORE_MASKS, None, False, n_split, other_half=True)[0]
        wire = _add_slabs_to_bf16(name + "_add", slabs, from_sibling, c)
    else:
        _, R, N = slabs.shape
        half = R // 2
        halves = slabs.reshape(N_CHIPS, 2, half, N)
        mine = lax.dynamic_index_in_dim(halves, c, axis=1, keepdims=False)
        theirs = lax.dynamic_index_in_dim(halves, 1 - c, axis=1, keepdims=False)
        from_sibling = _exchange(name + "_sib", theirs, CORE_MASKS, None, False, N_CHIPS)[0]
        wire = _add_to_bf16(name + "_add", mine, from_sibling)
    if overlap:
        return _exchange_start(name + "_chips", wire, CHIP_MASKS, "chip", True, n_split), (R, N)
    got = _exchange(name + "_chips", wire, CHIP_MASKS, "chip", True, n_split, copy_own=False)
    return _reduce_tail(name, wire, got, n_split, R, N)


def _reduce_tail(name, wire, got, n_split, R, N):
    c, chip = lax.axis_index("c"), 2 * lax.axis_index("x") + lax.axis_index("y")
    got = lax.dynamic_update_index_in_dim(got, lax.dynamic_index_in_dim(wire, chip, 0, keepdims=False), chip, 0)
    part = _sum_slots(name + "_sum", got)
    both = _exchange(name + "_cores", part, CORE_MASKS, "core", False, n_split, copy_own=False)
    return lax.dynamic_update_index_in_dim(both, part, c, 0).reshape(1, R, N)


def _reduce_finish(name, started, after, n_split):
    handle, (R, N) = started
    wire, got = _exchange_wait(name + "_chips_wait", handle, after, CHIP_MASKS, "chip", True, n_split)
    return _reduce_tail(name, wire, got, n_split, R, N)


def _pack(pieces, cols, row_mult=8):
    flat = jnp.concatenate([p.reshape(-1) for p in pieces])
    rows = -(-flat.shape[0] // cols)
    rows = -(-rows // row_mult) * row_mult
    return jnp.pad(flat, (0, rows * cols - flat.shape[0])).reshape(rows, cols)


def _unpack(packed, shapes):
    flat = packed.reshape(-1)
    out, off = [], 0
    for shp in shapes:
        n = 1
        for d in shp:
            n *= d
        out.append(flat[off:off + n].reshape(shp))
        off += n
    return out


def _to_my_layout(w):
    z = jnp.zeros((w.shape[0], 112), w.dtype)
    return jnp.concatenate([w[:, 0:3072], w[:, 3200:4224], w[:, 4224:7296], w[:, 7312:8336],
                            w[:, 3072:3200], w[:, 7296:7312], z], axis=1)


def kernel(x, meta_tokens, norm_w, w_in, rw_shift_mu, rw_w0, rw_w2, rw_a0, rw_a2, rw_k_k, rw_k_a, rw_r_k, rw_gn_w, rw_gn_b, dn_conv_w, dn_A_log, dn_dt_bias, dn_norm_w, w_out, final_norm_w, loss_target, m_meta_tokens, m_norm_w, m_w_in, m_rw_shift_mu, m_rw_w0, m_rw_w2, m_rw_a0, m_rw_a2, m_rw_k_k, m_rw_k_a, m_rw_r_k, m_rw_gn_w, m_rw_gn_b, m_dn_conv_w, m_dn_A_log, m_dn_dt_bias, m_dn_norm_w, m_w_out, m_final_norm_w, v_meta_tokens, v_norm_w, v_w_in, v_rw_shift_mu, v_rw_w0, v_rw_w2, v_rw_a0, v_rw_a2, v_rw_k_k, v_rw_k_a, v_rw_r_k, v_rw_gn_w, v_rw_gn_b, v_dn_conv_w, v_dn_A_log, v_dn_dt_bias, v_dn_norm_w, v_w_out, v_final_norm_w):
    S = x.shape[1]
    L = N_META + S
    Lp = -(-L // CHUNK) * CHUNK

    small_shapes = [(RW_LORA, 256), (RW_LORA, 256), (CONV_W, 768), (N_META, 512)]
    small_mine = _pack([rw_w2[0], rw_a2[0], dn_conv_w[0], meta_tokens], 1024)
    small_all = _gather_chips("gather_small", small_mine)
    per_chip = [_unpack(small_all[s], small_shapes) for s in range(N_CHIPS)]
    w2, a2, conv_w, meta = [jnp.concatenate([per_chip[s][i] for s in range(N_CHIPS)], axis=1) for i in range(4)]
    w_in_started = _gather_shards("gather_w_in", w_in[0].astype(BF16), 8, overlap=True)
    w_out_started = _gather_shards("gather_w_out", w_out[0].astype(BF16), 8, overlap=True)

    tail = [jnp.zeros((Lp - L, D_MODEL), F32)] if Lp > L else []
    h = jnp.concatenate([meta + w_in_started[4][:1, :1], x[0]] + tail, axis=0)
    tgt = jnp.concatenate([jnp.zeros((N_META, D_MODEL), F32), loss_target[0]] + tail, axis=0)
    (u,) = _rowwise("rms_in", _rms_fn, [_row(h)], [norm_w], [D_MODEL], tm=2 * ROW_TILE, out_dtype=BF16)
    w_in_all = _gather_finish("gather_w_in", w_in_started, u, 8)
    W_orig = jnp.transpose(w_in_all, (0, 2, 1, 3)).reshape(D_MODEL, IN_COLS)
    W = _to_my_layout(W_orig)
    W_gapped_t = jnp.concatenate([W_orig[:, :GAP_AT], jnp.zeros((D_MODEL, GAP), BF16), W_orig[:, GAP_AT:]], axis=1).T
    p = _mm("in_proj", u, W, "nn", dep=w_out_started[4])

    mu = rw_shift_mu
    zpad = jnp.zeros((RW_LORA, RW_WIDTH), F32)
    rw_params = [mu[:, 0:1024], mu[:, 1024:2048], mu[:, 2048:3072], mu[:, 3072:3200], rw_w0,
                 jnp.concatenate([w2, zpad], axis=0), rw_a0, jnp.concatenate([zpad, a2], axis=0), rw_k_k, rw_k_a]
    rw_rows = [_row(p, 1024, CB_R), _row(p, 1024, CB_K), _row(p, 1024, CB_V), _row(p, 128, CB_S1),
               _row(p, 1024, CB_GATE)]
    rw_pars = rw_params + [rw_r_k, rw_gn_w, rw_gn_b]
    ya, rw_ck, rw_t = _chunk_fwd("rw_chunk_fwd", _rw_fused_fn, rw_rows, 4, rw_pars, GROUP)

    dn_rows = [_row(p, 1024, CB_DQ), _row(p, 1024, CB_DK), _row(p, 1024, CB_DV), _row(p, 128, CB_S2),
               _row(p, 1024, CB_Z)]
    dn_pars = [conv_w[j:j + 1, 1024 * i:1024 * (i + 1)] for i in range(3) for j in range(CONV_W)]
    narrow = lambda t: jnp.pad(t, ((0, 0), (DN_HEADS, DN_HEAD - 2 * DN_HEADS)))
    dn_pars += [narrow(dn_A_log), narrow(dn_dt_bias), jnp.tile(dn_norm_w, (1, DN_HEADS))]
    yb, dn_ck, dn_t = _chunk_fwd("dn_chunk_fwd", _dn_fused_fn, dn_rows, 3, dn_pars, CHUNK)

    y = jnp.concatenate([ya, yb], axis=1)
    w_out_all = _gather_finish("gather_w_out", w_out_started, y, 8)
    Wo = jnp.transpose(w_out_all, (1, 0, 2, 3)).reshape(D_MODEL, D_MODEL)
    yw = _mm("out_proj", y, Wo, "nn", tn=1024)
    loss_acc, dz, dz16, d_fw = _loss_head(h, yw, tgt, final_norm_w.reshape(1, D_MODEL), L)
    loss = lax.psum(loss_acc[0, 0], ("x", "y", "c"))

    d_wo = _mm("d_w_out", y.T, dz16, "nn", tm=512, tn=1024, tk=Lp)
    wo_started = _reduce_to_shard("rs_w_out", d_wo.reshape(N_CHIPS, D_MODEL // N_CHIPS, D_MODEL), 8, overlap=True)
    dy = _mm("d_out_proj", dz16, Wo.T, "nn", tn=1024, tk=2048, dep=wo_started[0][4])

    dp_rw, d_rw_pars = _chunk_bwd("rw_chunk_bwd", _rw_fused_fn, rw_rows, 4, rw_pars, rw_ck, rw_t, _row(dy, 1024, 0))
    d_prep_pars, d_rw_pars = d_rw_pars[:10], d_rw_pars[10:]
    dp_dn, d_dn_pars = _chunk_bwd("dn_chunk_bwd", _dn_fused_fn, dn_rows, 3, dn_pars, dn_ck, dn_t, _row(dy, 1024, 1))
    d_conv_parts = [jnp.concatenate(d_dn_pars[4 * i:4 * i + 4], axis=0) for i in range(3)]
    d_a_log_b, d_dt_b = d_dn_pars[12], d_dn_pars[13]
    d_dn_nw = jnp.sum(d_dn_pars[14].reshape(DN_HEADS, DN_HEAD), axis=0, keepdims=True)

    dp = jnp.concatenate(list(dp_rw[:3]) + [dp_rw[3], dp_rw[4]] + list(dp_dn), axis=1)
    d_W = _mm("d_w_in", u.T, dp, "nn", tm=1024, tn=768, tk=Lp)
    wi_started = _reduce_to_shard("rs_w_in", d_W, 8, by_columns=True, overlap=True)
    du = _mm("d_in_proj", dp, W_gapped_t, "nn", tn=2048, tk=1408, dep=wi_started[0][4])
    (dh,), (d_norm_w,) = _rowwise_bwd("rms_in_bwd", lambda h_, w_: (_rms_fn(h_, w_)[0], h_), [_row(h)], [norm_w],
                                      [_row(du), _row(dz)])
    grad_x = dh[N_META:L][None]

    d_mu = jnp.concatenate(d_prep_pars[0:4], axis=1)
    d_w2, d_a2 = d_prep_pars[5][:RW_LORA], d_prep_pars[7][RW_LORA:]
    d_conv = jnp.concatenate(d_conv_parts, axis=1)
    d_meta = dh[:N_META]
    head_sum = lambda t: t[:, DN_HEADS:2 * DN_HEADS]
    rep_names = ["norm_w", "rw_shift_mu", "rw_w0", "rw_a0", "rw_k_k", "rw_k_a", "rw_r_k", "rw_gn_w", "rw_gn_b",
                 "dn_A_log", "dn_dt_bias", "dn_norm_w", "final_norm_w"]
    rep_g = [d_norm_w, d_mu, d_prep_pars[4], d_prep_pars[6], d_prep_pars[8], d_prep_pars[9],
             d_rw_pars[0], d_rw_pars[1], d_rw_pars[2],
             head_sum(d_a_log_b), head_sum(d_dt_b), d_dn_nw, d_fw.reshape(D_MODEL)]
    rep_w = [norm_w, rw_shift_mu, rw_w0, rw_a0, rw_k_k, rw_k_a, rw_r_k, rw_gn_w, rw_gn_b, dn_A_log, dn_dt_bias,
             dn_norm_w, final_norm_w]
    rep_m = [m_norm_w, m_rw_shift_mu, m_rw_w0, m_rw_a0, m_rw_k_k, m_rw_k_a, m_rw_r_k, m_rw_gn_w, m_rw_gn_b,
             m_dn_A_log, m_dn_dt_bias, m_dn_norm_w, m_final_norm_w]
    rep_v = [v_norm_w, v_rw_shift_mu, v_rw_w0, v_rw_a0, v_rw_k_k, v_rw_k_a, v_rw_r_k, v_rw_gn_w, v_rw_gn_b,
             v_dn_A_log, v_dn_dt_bias, v_dn_norm_w, v_final_norm_w]
    rep_shapes = [t.shape for t in rep_w]
    rep_all = _exchange("gather_rep_grads", _pack(rep_g, 128), ALL_MASKS, "dev", False)
    rep_out = _adamw("adam_rep", rep_all, _pack(rep_w, 128), _pack(rep_m, 128), _pack(rep_v, 128))
    rep_out = [dict(zip(rep_names, _unpack(t, rep_shapes))) for t in rep_out]

    sm_slabs = jnp.stack([_pack([d_w2[:, 256 * s:256 * (s + 1)], d_a2[:, 256 * s:256 * (s + 1)],
                                 d_conv[:, 768 * s:768 * (s + 1)], d_meta[:, 512 * s:512 * (s + 1)]], 1024, 64)
                          for s in range(N_CHIPS)])
    sm_parts = _reduce_to_shard("rs_small", sm_slabs, 1)
    sm_w = [rw_w2[0], rw_a2[0], dn_conv_w[0], meta_tokens]
    sm_m = [m_rw_w2[0], m_rw_a2[0], m_dn_conv_w[0], m_meta_tokens]
    sm_v = [v_rw_w2[0], v_rw_a2[0], v_dn_conv_w[0], v_meta_tokens]
    sm_out = _adamw("adam_small", sm_parts, _pack(sm_w, 1024, 64), _pack(sm_m, 1024, 64), _pack(sm_v, 1024, 64))
    sm_names = ["rw_w2", "rw_a2", "dn_conv_w", "meta_tokens"]
    sm_full_shapes = [(1, RW_LORA, 256), (1, RW_LORA, 256), (1, CONV_W, 768), (N_META, 512)]
    sm_out = [dict(zip(sm_names, [t.reshape(shp) for t, shp in zip(_unpack(o, small_shapes), sm_full_shapes)]))
              for o in sm_out]

    wo_parts = _reduce_finish("rs_w_out", wo_started, dp, 8)
    wo_out = _adamw("adam_w_out", wo_parts, w_out[0], m_w_out[0], v_w_out[0])
    wi_parts = _reduce_finish("rs_w_in", wi_started, dh, 8)
    wi_out = _adamw("adam_w_in", wi_parts, w_in[0], m_w_in[0], v_w_in[0])

    order = ["meta_tokens", "norm_w", "w_in", "rw_shift_mu", "rw_w0", "rw_w2", "rw_a0", "rw_a2", "rw_k_k", "rw_k_a",
             "rw_r_k", "rw_gn_w", "rw_gn_b", "dn_conv_w", "dn_A_log", "dn_dt_bias", "dn_norm_w", "w_out",
             "final_norm_w"]
    outs = [loss, grad_x]
    for kind in range(4):
        table = dict(rep_out[kind])
        table.update(sm_out[kind])
        table["w_in"] = wi_out[kind][None]
        table["w_out"] = wo_out[kind][None]
        outs += [table[n] for n in order]
    return tuple(outs)
```

```python
import functools

import jax
import jax.numpy as jnp
from jax import lax
from jax.experimental import pallas as pl
from jax.experimental.pallas import tpu as pltpu

F32 = jnp.float32
BF16 = jnp.bfloat16
HIGH = lax.Precision.HIGH
HIGHEST = lax.Precision.HIGHEST
MESH = pl.DeviceIdType.MESH

D_MODEL = 2048
N_META = 16
RW_WIDTH = 1024
RW_HEAD = 64
RW_HEADS = 16
RW_LORA = 64
RW_GN_EPS = 64e-5
DN_WIDTH = 1024
DN_HEAD = 128
DN_HEADS = 8
CONV_W = 4
CHUNK = 64
NORM_EPS = 1e-6
IN_COLS = 8336
N_CHIPS = 4
SHARD_COLS = IN_COLS // N_CHIPS

NP_COLS = 8 * 1024 + 256
GAP_AT = IN_COLS - DN_WIDTH
GAP = NP_COLS - IN_COLS
CB_R, CB_K, CB_V, CB_GATE, CB_DQ, CB_DK, CB_DV, CB_Z = range(8)
CB_S1 = 8192 // 128
CB_S2 = CB_S1 + 1

ADAM_LR = 0.001
ADAM_B1 = 0.9
ADAM_B2 = 0.999
ADAM_EPS = 1e-08
ADAM_WD = 0.01
ADAM_STEP = 10

VMEM_LIMIT_BYTES = 56 * 1024 * 1024
ROW_TILE = 104
MM_ROW_TILE = 832


def _params(sem=None):
    return pltpu.CompilerParams(dimension_semantics=sem, vmem_limit_bytes=VMEM_LIMIT_BYTES)


def _pick(n, target, mult=8):
    best = None
    for d in range(mult, min(n, target) + 1, mult):
        if n % d == 0:
            best = d
    return n if best is None else best


def _mm(name, a, b, mode, tm=MM_ROW_TILE, tn=1408, tk=2048, dep=None):
    if mode == "nn":
        (M, K), (_, N) = a.shape, b.shape
    elif mode == "nt":
        (M, K), (N, _) = a.shape, b.shape
    else:
        (K, M), (_, N) = a.shape, b.shape
    tm = _pick(M, tm, 128 if mode == "tn" else 16)
    tn = _pick(N, tn, 128)
    tk = _pick(K, tk, 8 if mode == "tn" else 128)
    if mode == "nn":
        a_spec = pl.BlockSpec((tm, tk), lambda i, j, k: (i, k))
        b_spec = pl.BlockSpec((tk, tn), lambda i, j, k: (k, j))
        dims = (((1,), (0,)), ((), ()))
    elif mode == "nt":
        a_spec = pl.BlockSpec((tm, tk), lambda i, j, k: (i, k))
        b_spec = pl.BlockSpec((tn, tk), lambda i, j, k: (j, k))
        dims = (((1,), (1,)), ((), ()))
    else:
        a_spec = pl.BlockSpec((tk, tm), lambda i, j, k: (k, i))
        b_spec = pl.BlockSpec((tk, tn), lambda i, j, k: (k, j))
        dims = (((0,), (0,)), ((), ()))

    def body(a_ref, b_ref, *rest):
        o_ref = rest[-1]

        @pl.when(pl.program_id(2) == 0)
        def _():
            o_ref[...] = jnp.zeros_like(o_ref)

        o_ref[...] += lax.dot_general(a_ref[...].astype(BF16), b_ref[...].astype(BF16), dims,
                                      preferred_element_type=F32)

    deps = [] if dep is None else [dep]
    return pl.pallas_call(
        body, name=name, grid=(M // tm, N // tn, K // tk),
        in_specs=[a_spec, b_spec] + [pl.BlockSpec((8, 128), lambda i, j, k: (0, 0))] * len(deps),
        out_specs=pl.BlockSpec((tm, tn), lambda i, j, k: (i, j)),
        out_shape=jax.ShapeDtypeStruct((M, N), F32),
        compiler_params=_params(("parallel", "parallel", "arbitrary")),
    )(a, b, *deps)


def _row(arr, width=None, cb=0):
    return (arr, arr.shape[1] if width is None else width, cb)


def _rowwise(name, fn, rows, params, out_widths, tm=ROW_TILE, out_dtype=F32):
    R = rows[0][0].shape[0]
    tm = _pick(R, tm, 16 if out_dtype == BF16 else 8)
    n_r, n_p = len(rows), len(params)

    def body(*refs):
        vals = [r[...] for r in refs[:n_r + n_p]]
        for o_ref, val in zip(refs[n_r + n_p:], fn(*vals)):
            o_ref[...] = val.astype(out_dtype)

    in_specs = [pl.BlockSpec((tm, w), lambda i, cb=cb: (i, cb)) for (_, w, cb) in rows]
    in_specs += [pl.BlockSpec(p.shape, lambda i: (0, 0)) for p in params]
    return pl.pallas_call(
        body, name=name, grid=(R // tm,), in_specs=in_specs,
        out_specs=[pl.BlockSpec((tm, w), lambda i: (i, 0)) for w in out_widths],
        out_shape=[jax.ShapeDtypeStruct((R, w), out_dtype) for w in out_widths],
        compiler_params=_params(("parallel",)),
    )(*[r[0] for r in rows], *params)


def _rowwise_bwd(name, fn, rows, params, douts, tm=ROW_TILE):
    R = rows[0][0].shape[0]
    tm = _pick(R, tm)
    n_r, n_p, n_d = len(rows), len(params), len(douts)

    def body(*refs):
        vals = [r[...] for r in refs[:n_r + n_p]]
        cts = tuple(r[...] for r in refs[n_r + n_p:n_r + n_p + n_d])
        grads = jax.vjp(fn, *vals)[1](cts)
        outs = refs[n_r + n_p + n_d:]
        for o_ref, g in zip(outs[:n_r], grads[:n_r]):
            o_ref[...] = g

        @pl.when(pl.program_id(0) == 0)
        def _():
            for o_ref in outs[n_r:]:
                o_ref[...] = jnp.zeros_like(o_ref)

        for o_ref, g in zip(outs[n_r:], grads[n_r:]):
            o_ref[...] += g

    in_specs = [pl.BlockSpec((tm, w), lambda i, cb=cb: (i, cb)) for (_, w, cb) in rows]
    in_specs += [pl.BlockSpec(p.shape, lambda i: (0, 0)) for p in params]
    in_specs += [pl.BlockSpec((tm, w), lambda i, cb=cb: (i, cb)) for (_, w, cb) in douts]
    out_specs = [pl.BlockSpec((tm, w), lambda i: (i, 0)) for (_, w, _) in rows]
    out_specs += [pl.BlockSpec(p.shape, lambda i: (0, 0)) for p in params]
    out_shape = [jax.ShapeDtypeStruct((R, w), F32) for (_, w, _) in rows]
    out_shape += [jax.ShapeDtypeStruct(p.shape, F32) for p in params]
    res = pl.pallas_call(
        body, name=name, grid=(R // tm,), in_specs=in_specs, out_specs=out_specs, out_shape=out_shape,
        compiler_params=_params(("arbitrary",)),
    )(*[r[0] for r in rows], *params, *[d[0] for d in douts])
    return res[:n_r], res[n_r:]


def _shift_sum(name, terms, tm=ROW_TILE):
    R = terms[0][0][0].shape[0]
    w = terms[0][0][1]
    tm = _pick(R, tm)
    nt, nb8 = R // tm, tm // 8
    shifts = [j for (_, j) in terms]

    def body(*refs):
        i = pl.program_id(0)
        acc = None
        for k, j in enumerate(shifts):
            x = refs[2 * k][...]
            if j > 0:
                halo = jnp.where(i == 0, 0.0, refs[2 * k + 1][...])
                x = pltpu.roll(jnp.concatenate([halo, x], axis=0), j, 0)[8:, :]
            elif j < 0:
                halo = jnp.where(i == nt - 1, 0.0, refs[2 * k + 1][...])
                x = pltpu.roll(jnp.concatenate([x, halo], axis=0), tm + 8 + j, 0)[:tm, :]
            acc = x if acc is None else acc + x
        refs[-1][...] = acc

    in_specs, args = [], []
    for (arr, _, cb), j in terms:
        in_specs.append(pl.BlockSpec((tm, w), lambda i, cb=cb: (i, cb)))
        if j > 0:
            in_specs.append(pl.BlockSpec((8, w), lambda i, cb=cb: (jnp.maximum(i * nb8 - 1, 0), cb)))
        else:
            in_specs.append(pl.BlockSpec((8, w), lambda i, cb=cb: (jnp.minimum((i + 1) * nb8, R // 8 - 1), cb)))
        args += [arr, arr]
    return pl.pallas_call(
        body, name=name, grid=(nt,), in_specs=in_specs, out_specs=pl.BlockSpec((tm, w), lambda i: (i, 0)),
        out_shape=jax.ShapeDtypeStruct((R, w), F32), compiler_params=_params(("parallel",)),
    )(*args)


def _softplus(x):
    return jnp.maximum(x, 0.0) + jnp.log(1.0 + jnp.exp(-jnp.abs(x)))


def _silu(x):
    return x * jax.nn.sigmoid(x)


def _rms_fn(h, w):
    return (h * lax.rsqrt(jnp.mean(h * h, axis=-1, keepdims=True) + NORM_EPS) * w,)


def _rw_prep_fn(pr, pr1, pk, pk1, pv, pv1, ps, ps1, mu_r, mu_k, mu_v, mu_s, w0, w2p, a0, a2p, k_k, k_a):
    r = pr + (pr1 - pr) * mu_r
    k = pk + (pk1 - pk) * mu_k
    v = pv + (pv1 - pv) * mu_v
    s = ps + (ps1 - ps) * mu_s
    lora = lambda x, w: jnp.dot(x.astype(BF16), w.astype(BF16), preferred_element_type=F32)
    w_log = -_softplus(-(w0 + lora(jnp.tanh(s), w2p))) - 0.5
    log_decay = -jnp.exp(w_log)
    a = jax.nn.sigmoid(a0 + lora(s, a2p))
    return r, log_decay, k * (1.0 + (a - 1.0) * k_a), v, k * k_k, a


def _conv_fn(u0, u1, u2, u3, w0, w1, w2, w3):
    return (_silu(u0 * w3 + u1 * w2 + u2 * w1 + u3 * w0),)


def _dn_gate_fn(ps2, a_log_n, dt_n):
    beta_n = jax.nn.sigmoid(ps2)
    g_n = -jnp.exp(a_log_n) * _softplus(ps2 + dt_n)
    lane = lax.broadcasted_iota(jnp.int32, (1, DN_HEAD), 1)
    pick = lambda x, j: jnp.broadcast_to(jnp.sum(x * (lane == j).astype(F32), axis=-1, keepdims=True), x.shape)[None]
    beta = jnp.concatenate([pick(beta_n, h) for h in range(DN_HEADS)], axis=0)
    g = jnp.concatenate([pick(g_n, DN_HEADS + h) for h in range(DN_HEADS)], axis=0)
    return beta, g


def _tril_masks(c):
    t = lax.broadcasted_iota(jnp.int32, (c, c), 0)
    s = lax.broadcasted_iota(jnp.int32, (c, c), 1)
    return s <= t, s < t


def _dot(x, y, cx, cy, prec=HIGH):
    nb = x.ndim - 2
    batch = tuple(range(nb))
    return lax.dot_general(x, y, (((cx + nb,), (cy + nb,)), (batch, batch)), precision=prec,
                           preferred_element_type=F32)


def _tri_inv(a):
    c = a.shape[-1]
    eye = (lax.broadcasted_iota(jnp.int32, (c, c), 0) == lax.broadcasted_iota(jnp.int32, (c, c), 1)).astype(F32)
    x = eye - a
    p = a
    n = 2
    while n < c:
        p = _dot(p, p, 1, 0)
        x = x + _dot(x, p, 1, 0)
        n *= 2
    return x


@jax.custom_vjp
def _tri_inv_saved(a, t):
    return t


def _tri_inv_saved_fwd(a, t):
    return t, t


def _tri_inv_saved_bwd(t, dt):
    return -_dot(_dot(t, dt, 0, 0), t, 1, 1), jnp.zeros_like(t)


_tri_inv_saved.defvjp(_tri_inv_saved_fwd, _tri_inv_saved_bwd)


def _pair_masks():
    lane = lax.broadcasted_iota(jnp.int32, (1, 2 * RW_HEAD), 1)
    m0 = (lane < RW_HEAD).astype(F32)
    return m0, 1.0 - m0


def _pair_bd(x):
    m0, m1 = _pair_masks()
    return jnp.concatenate([x * m0, x * m1], axis=-2)


def _pair_mm(x, y):
    return _dot(x, _pair_bd(y), 1, 0)


def _pair_inv(a):
    c = a.shape[-2]
    row = lax.broadcasted_iota(jnp.int32, (c, 2 * RW_HEAD), 0)
    col = lax.broadcasted_iota(jnp.int32, (c, 2 * RW_HEAD), 1) & (RW_HEAD - 1)
    x = (row == col).astype(F32) - a
    p = a
    n = 2
    while n < c:
        p = _pair_mm(p, p)
        x = x + _pair_mm(x, p)
        n *= 2
    return x


@jax.custom_vjp
def _pair_inv_saved(a, t):
    return t


def _pair_inv_saved_fwd(a, t):
    return t, t


def _pair_inv_saved_bwd(t, dt):
    m0, m1 = _pair_masks()
    c = t.shape[-2]
    z = _dot(t, dt, 0, 0)
    x = z[:, :c, :] * m0 + z[:, c:, :] * m1
    return -_dot(x, _pair_bd(t), 1, 1), jnp.zeros_like(t)


_pair_inv_saved.defvjp(_pair_inv_saved_fwd, _pair_inv_saved_bwd)


def _rw_chunk_fn(S, r, lw, k2, v, kkp, a, gate, rk, gnw, gnb, t_saved=None):
    B, C, P = r.shape
    m0, m1 = _pair_masks()
    seg = lambda x: (jnp.sum(x * m0, axis=-1, keepdims=True) * m0 + jnp.sum(x * m1, axis=-1, keepdims=True) * m1)
    mm = lambda x, y: _dot(x.astype(BF16), y.astype(BF16), 1, 0, None)
    nt = lambda x, y: _dot(x, y, 1, 1)
    tn = lambda x, y: _dot(x, y, 0, 0)
    pair_mm = lambda x, y: mm(x, _pair_bd(y))
    t_idx = lax.broadcasted_iota(jnp.int32, (C, P), 0)
    s_idx = lax.broadcasted_iota(jnp.int32, (C, P), 1) & (RW_HEAD - 1)
    incl, strict = s_idx <= t_idx, s_idx < t_idx
    tril = jnp.broadcast_to(_tril_masks(C)[0].astype(F32), (B, C, C))
    kk = kkp * lax.rsqrt(seg(kkp * kkp) + 1e-6)
    b = kk * a
    g_incl = _dot(tril, lw, 1, 0)
    g_excl = g_incl - lw
    inv = jnp.exp(-g_incl)
    alpha, beta, kappa, rho = kk * jnp.exp(g_excl), b * inv, k2 * inv, r * jnp.exp(g_incl)
    ar = jnp.concatenate([alpha, rho], axis=-2)
    scores = nt(ar, jnp.concatenate([_pair_bd(beta), _pair_bd(kappa)], axis=-2))
    a_ab = jnp.where(strict, scores[:, :C, :P], 0.0)
    a_ak = jnp.where(strict, scores[:, :C, P:], 0.0)
    r_b = jnp.where(incl, scores[:, C:, :P], 0.0)
    r_k = jnp.where(incl, scores[:, C:, P:], 0.0)
    t_inv = _pair_inv(a_ab) if t_saved is None else _pair_inv_saved(a_ab, t_saved)
    on_state = nt(ar, S)
    u = pair_mm(t_inv, -on_state[:, :C, :] - pair_mm(a_ak, v))
    y = on_state[:, C:, :] + mm(jnp.concatenate([r_b, r_k], axis=-1),
                                jnp.concatenate([_pair_bd(u), _pair_bd(v)], axis=-2))
    same_head = ((lax.broadcasted_iota(jnp.int32, (P, P), 0) < RW_HEAD)
                 == (lax.broadcasted_iota(jnp.int32, (P, P), 1) < RW_HEAD))
    fresh = tn(jnp.concatenate([u, v], axis=-2), jnp.concatenate([beta, kappa], axis=-2))
    S_new = jnp.exp(jnp.sum(lw, axis=-2, keepdims=True)) * (S + jnp.where(same_head, fresh, 0.0))
    dev = y - seg(y) * (1.0 / RW_HEAD)
    yn = dev * lax.rsqrt(seg(dev * dev) * (1.0 / RW_HEAD) + RW_GN_EPS) * gnw + gnb
    bonus = seg(r * k2 * rk) * v
    return (yn + bonus) * _silu(gate), S_new, t_inv


def _dn_chunk_fn(S, qc, kc, vc, bb, gb, z, nw, t_saved=None):
    B, C, D = qc.shape
    mm = lambda x, y: _dot(x.astype(BF16), y.astype(BF16), 1, 0, None)
    nt = lambda x, y, p=None: _dot(x, y, 1, 1, p) if p else _dot(x.astype(BF16), y.astype(BF16), 1, 1, None)
    tn = lambda x, y: _dot(x.astype(BF16), y.astype(BF16), 0, 0, None)
    incl, strict = _tril_masks(C)
    q = qc * lax.rsqrt(jnp.sum(qc * qc, axis=-1, keepdims=True) + 1e-6) * (D ** -0.5)
    k = kc * lax.rsqrt(jnp.sum(kc * kc, axis=-1, keepdims=True) + 1e-6)
    kb, vb = k * bb, vc * bb
    G = _dot(jnp.broadcast_to(incl.astype(F32), (B, C, C)), gb, 1, 0, HIGHEST)
    lane = lax.broadcasted_iota(jnp.int32, (C, D), 1)
    e0, e1 = (lane == 0).astype(F32), (lane == 1).astype(F32)
    diff = nt(G * e0 + e1, e0 - G * e1, HIGHEST)
    dmask = jnp.where(incl, jnp.exp(jnp.where(incl, diff, 0.0)), 0.0)
    M = jnp.where(strict, nt(kb, k) * dmask, 0.0)
    T = _tri_inv(M) if t_saved is None else _tri_inv_saved(M, t_saved)
    eG = jnp.exp(G)
    u = mm(T, vb)
    w = mm(T, kb * eG)
    attn = jnp.where(incl, nt(q, k) * dmask, 0.0)
    v_new = u - mm(w, S)
    o = mm(q * eG, S) + mm(attn, v_new)
    g_last = jnp.sum(gb, axis=-2, keepdims=True)
    S_new = S * jnp.exp(jnp.broadcast_to(g_last, S.shape)) + tn(k * jnp.exp(g_last - G), v_new)
    on = o * lax.rsqrt(jnp.mean(o * o, axis=-1, keepdims=True) + NORM_EPS) * nw
    return on * _silu(z), S_new, T


N_GROUPS = 8
GROUP = 128
HALO = 8


def _groups(x):
    return jnp.concatenate([x[:, g * GROUP:(g + 1) * GROUP][None] for g in range(N_GROUPS)], axis=0)


@functools.partial(jax.custom_vjp, nondiff_argnums=(1,))
def _shift_rows(ext, j):
    return pltpu.roll(ext, j, 0)[HALO:, :]


def _shift_rows_fwd(ext, j):
    return _shift_rows(ext, j), None


def _shift_rows_bwd(j, _, d):
    z = jnp.concatenate([jnp.zeros((HALO, d.shape[1]), d.dtype), d], axis=0)
    return (pltpu.roll(z, z.shape[0] - j, 0),)


_shift_rows.defvjp(_shift_rows_fwd, _shift_rows_bwd)


def _rw_fused_fn(S, h_r, h_k, h_v, h_s, p_r, p_k, p_v, p_s, gate, *pars, t_saved=None):
    prev = lambda h, x: _shift_rows(jnp.concatenate([h, x], axis=0), 1)
    seq = _rw_prep_fn(p_r, prev(h_r, p_r), p_k, prev(h_k, p_k), p_v, prev(h_v, p_v), p_s, prev(h_s, p_s), *pars[:10])
    return _rw_chunk_fn(S, *[_groups(t) for t in seq], _groups(gate), *[_groups(t) for t in pars[10:]],
                        t_saved=t_saved)


def _dn_fused_fn(S, h_q, h_k, h_v, p_q, p_k, p_v, p_s2, z, *pars, t_saved=None):
    conv = []
    for i, (h, x) in enumerate(((h_q, p_q), (h_k, p_k), (h_v, p_v))):
        ext = jnp.concatenate([h, x], axis=0)
        conv += _conv_fn(x, _shift_rows(ext, 1), _shift_rows(ext, 2), _shift_rows(ext, 3), *pars[4 * i:4 * i + 4])
    beta, g = _dn_gate_fn(p_s2, pars[12], pars[13])
    return _dn_chunk_fn(S, *[_groups(t) for t in conv], beta, g, _groups(z), _groups(pars[14]), t_saved=t_saved)


def _chunk_fwd(name, fn, rows, n_halo, pars, t_lanes):
    Lp = rows[0][0].shape[0]
    C, D, NB = CHUNK, GROUP, N_GROUPS
    nc, n_r, n_p, per = Lp // C, len(rows), len(pars), CHUNK // HALO

    def body(*refs):
        row_refs, halo_refs, par_refs = refs[:n_r], refs[n_r:n_r + n_halo], refs[n_r + n_halo:n_r + n_halo + n_p]
        y_ref, ck_ref, t_ref, s_ref = refs[n_r + n_halo + n_p:]
        first = pl.program_id(0) == 0

        @pl.when(first)
        def _():
            s_ref[...] = jnp.zeros_like(s_ref)

        S = s_ref[...]
        ck_ref[0] = S
        halos = [jnp.where(first, 0.0, r[...]) for r in halo_refs]
        y, S_new, t_inv = fn(S, *halos, *[r[...] for r in row_refs], *[r[...] for r in par_refs])
        for g in range(NB):
            y_ref[:, g * D:(g + 1) * D] = y[g].astype(BF16)
        t_ref[0] = t_inv
        s_ref[...] = S_new

    return pl.pallas_call(
        body, name=name, grid=(nc,),
        in_specs=[pl.BlockSpec((C, w), lambda c, cb=cb: (c, cb)) for (_, w, cb) in rows]
        + [pl.BlockSpec((HALO, w), lambda c, cb=cb: (jnp.maximum(c * per - 1, 0), cb)) for (_, w, cb) in rows[:n_halo]]
        + [pl.BlockSpec(p.shape, lambda c: (0, 0)) for p in pars],
        out_specs=[pl.BlockSpec((C, NB * D), lambda c: (c, 0)),
                   pl.BlockSpec((1, NB, D, D), lambda c: (c, 0, 0, 0)),
                   pl.BlockSpec((1, NB, C, t_lanes), lambda c: (c, 0, 0, 0))],
        out_shape=[jax.ShapeDtypeStruct((Lp, NB * D), BF16), jax.ShapeDtypeStruct((nc, NB, D, D), F32),
                   jax.ShapeDtypeStruct((nc, NB, C, t_lanes), F32)],
        scratch_shapes=[pltpu.VMEM((NB, D, D), F32)],
        compiler_params=_params(("arbitrary",)),
    )(*[r[0] for r in rows], *[r[0] for r in rows[:n_halo]], *pars)


def _chunk_bwd(name, fn, rows, n_halo, pars, ckpt, tinv, dy):
    Lp = rows[0][0].shape[0]
    C, D, NB = CHUNK, GROUP, N_GROUPS
    nc, n_r, n_p, per = Lp // C, len(rows), len(pars), CHUNK // HALO
    t_lanes = tinv.shape[-1]
    n_in = n_r + n_halo + n_p

    def body(*refs):
        row_refs, halo_refs, par_refs = refs[:n_r], refs[n_r:n_r + n_halo], refs[n_r + n_halo:n_in]
        ck_ref, t_ref, dy_ref = refs[n_in:n_in + 3]
        outs = refs[n_in + 3:]
        drow_refs, dpar_refs = outs[:n_r], outs[n_r:n_r + n_p]
        ds_ref, carry_refs = outs[n_r + n_p], outs[n_r + n_p + 1:]
        i = pl.program_id(0)

        @pl.when(i == 0)
        def _():
            ds_ref[...] = jnp.zeros_like(ds_ref)
            for o_ref in list(dpar_refs) + list(carry_refs):
                o_ref[...] = jnp.zeros_like(o_ref)

        halos = [jnp.where(i == nc - 1, 0.0, r[...]) for r in halo_refs]
        vals = [ck_ref[0]] + halos + [r[...] for r in row_refs] + [r[...] for r in par_refs]
        t_saved = t_ref[0]
        grads = jax.vjp(lambda *v: fn(*v, t_saved=t_saved)[:2], *vals)[1]((_groups(dy_ref), ds_ref[...]))
        ds_ref[...] = grads[0]
        d_halos, d_rows, d_pars = grads[1:1 + n_halo], grads[1 + n_halo:1 + n_halo + n_r], grads[1 + n_halo + n_r:]
        for k, (o_ref, g) in enumerate(zip(drow_refs, d_rows)):
            if k < n_halo:
                g = g + jnp.concatenate([jnp.zeros((C - HALO, g.shape[1]), F32), carry_refs[k][...]], axis=0)
                carry_refs[k][...] = d_halos[k]
            o_ref[...] = g.astype(BF16)
        for o_ref, g in zip(dpar_refs, d_pars):
            o_ref[...] += g

    rev = lambda c: nc - 1 - c
    dy_arr, dy_w, dy_cb = dy
    res = pl.pallas_call(
        body, name=name, grid=(nc,),
        in_specs=[pl.BlockSpec((C, w), lambda c, cb=cb: (rev(c), cb)) for (_, w, cb) in rows]
        + [pl.BlockSpec((HALO, w), lambda c, cb=cb: (jnp.maximum(rev(c) * per - 1, 0), cb))
           for (_, w, cb) in rows[:n_halo]]
        + [pl.BlockSpec(p.shape, lambda c: (0, 0)) for p in pars]
        + [pl.BlockSpec((1, NB, D, D), lambda c: (rev(c), 0, 0, 0)),
           pl.BlockSpec((1, NB, C, t_lanes), lambda c: (rev(c), 0, 0, 0)),
           pl.BlockSpec((C, dy_w), lambda c: (rev(c), dy_cb))],
        out_specs=[pl.BlockSpec((C, w), lambda c: (rev(c), 0)) for (_, w, _) in rows]
        + [pl.BlockSpec(p.shape, lambda c: (0, 0)) for p in pars],
        out_shape=[jax.ShapeDtypeStruct((Lp, w), BF16) for (_, w, _) in rows]
        + [jax.ShapeDtypeStruct(p.shape, F32) for p in pars],
        scratch_shapes=[pltpu.VMEM((NB, D, D), F32)] + [pltpu.VMEM((HALO, w), F32) for (_, w, _) in rows[:n_halo]],
        compiler_params=_params(("arbitrary",)),
    )(*[r[0] for r in rows], *[r[0] for r in rows[:n_halo]], *pars, ckpt, tinv, dy_arr)
    return res[:n_r], res[n_r:]


def _loss_head(h, yw, tgt, fw, n_real, tm=2 * ROW_TILE):
    Lp, Dm = h.shape
    tm = _pick(Lp, tm, 16)

    def out_fn(z, fw_):
        return z * lax.rsqrt(jnp.mean(z * z, axis=-1, keepdims=True) + NORM_EPS) * fw_

    def body(h_ref, yw_ref, t_ref, fw_ref, loss_ref, dz_ref, dz16_ref, dfw_ref):
        i = pl.program_id(0)

        @pl.when(i == 0)
        def _():
            loss_ref[...] = jnp.zeros_like(loss_ref)
            dfw_ref[...] = jnp.zeros_like(dfw_ref)

        row = i * tm + lax.broadcasted_iota(jnp.int32, (tm, 1), 0)
        mask = ((row >= N_META) & (row < n_real)).astype(F32)
        z = h_ref[...] + yw_ref[...]
        o, vjp = jax.vjp(out_fn, z, fw_ref[...])
        err = (o - t_ref[...]) * mask
        row_loss = 0.5 * jnp.mean(jnp.square(err), axis=-1, keepdims=True)
        dz, dfw = vjp(err * (1.0 / Dm))
        loss_ref[...] += jnp.sum(row_loss, axis=0, keepdims=True)
        dz_ref[...] = dz
        dz16_ref[...] = dz.astype(BF16)
        dfw_ref[...] += dfw

    row_spec = pl.BlockSpec((tm, Dm), lambda i: (i, 0))
    return pl.pallas_call(
        body, name="loss_head", grid=(Lp // tm,),
        in_specs=[row_spec, row_spec, row_spec, pl.BlockSpec((1, Dm), lambda i: (0, 0))],
        out_specs=[pl.BlockSpec((8, 128), lambda i: (0, 0)), row_spec, row_spec,
                   pl.BlockSpec((1, Dm), lambda i: (0, 0))],
        out_shape=[jax.ShapeDtypeStruct((8, 128), F32), jax.ShapeDtypeStruct((Lp, Dm), F32),
                   jax.ShapeDtypeStruct((Lp, Dm), BF16), jax.ShapeDtypeStruct((1, Dm), F32)],
        compiler_params=_params(("arbitrary",)),
    )(h, yw, tgt, fw)


def _exchange(name, x, masks, slot_kind, per_dest, n_split=1, copy_own=True, other_half=False):
    n = len(masks)
    keep_own = slot_kind is not None and copy_own
    n_slots = {"chip": 4, "core": 2, "dev": 8, None: n}[slot_kind]
    blk_shape = x.shape[1:] if per_dest else x.shape
    if other_half:
        blk_shape = (x.shape[0] // 2,) + tuple(x.shape[1:])
    rows = blk_shape[0] // n_split

    def body(x_ref, o_ref, send_sems, recv_sems, local_sems):
        mx, my, mc = lax.axis_index("x"), lax.axis_index("y"), lax.axis_index("c")
        if other_half:
            x_ref = x_ref.at[pl.ds((1 - mc) * blk_shape[0], blk_shape[0])]

        def slot(k, px, py, pc):
            return {"chip": 2 * px + py, "core": pc, "dev": 4 * px + 2 * py + pc, None: k}[slot_kind]

        def peer(m):
            return (mx + m[0]) % 2, (my + m[1]) % 2, (mc + m[2]) % 2

        def part(ref, j):
            return ref.at[pl.ds(j * rows, rows)]

        own_src = x_ref.at[2 * mx + my] if per_dest else x_ref
        local = []
        if keep_own:
            own_dst = o_ref.at[slot(0, mx, my, mc)]
            local = [pltpu.make_async_copy(part(own_src, j), part(own_dst, j), local_sems.at[j])
                     for j in range(n_split)]
        for cp in local:
            cp.start()
        sends = []
        for k, m in enumerate(masks):
            px, py, pc = peer(m)
            src = x_ref.at[2 * px + py] if per_dest else x_ref
            dst = o_ref.at[slot(k, mx, my, mc)]
            for j in range(n_split):
                sends.append(pltpu.make_async_remote_copy(
                    src_ref=part(src, j), dst_ref=part(dst, j), send_sem=send_sems.at[k * n_split + j],
                    recv_sem=recv_sems.at[k * n_split + j], device_id=(px, py, pc), device_id_type=MESH))
        for cp in sends:
            cp.start()
        for k, m in enumerate(masks):
            px, py, pc = peer(m)
            landed = o_ref.at[slot(k, px, py, pc)]
            for j in range(n_split):
                pltpu.make_async_remote_copy(
                    src_ref=part(own_src, j), dst_ref=part(landed, j), send_sem=send_sems.at[k * n_split + j],
                    recv_sem=recv_sems.at[k * n_split + j], device_id=(px, py, pc), device_id_type=MESH).wait_recv()
        for cp in sends:
            cp.wait_send()
        for cp in local:
            cp.wait()

    return pl.pallas_call(
        body, name=name,
        in_specs=[pl.BlockSpec(memory_space=pl.ANY)], out_specs=pl.BlockSpec(memory_space=pl.ANY),
        out_shape=jax.ShapeDtypeStruct((n_slots,) + tuple(blk_shape), x.dtype),
        scratch_shapes=[pltpu.SemaphoreType.DMA((n * n_split,)), pltpu.SemaphoreType.DMA((n * n_split,)),
                        pltpu.SemaphoreType.DMA((n_split,))],
        compiler_params=pltpu.CompilerParams(has_side_effects=True),
    )(x)


CHIP_MASKS = [(1, 0, 0), (0, 1, 0), (1, 1, 0)]
CORE_MASKS = [(0, 0, 1)]
ALL_MASKS = [(dx, dy, dc) for dx in (0, 1) for dy in (0, 1) for dc in (0, 1) if (dx, dy, dc) != (0, 0, 0)]

HBM_SPEC = pl.BlockSpec(memory_space=pltpu.HBM)
SEM_SPEC = pl.BlockSpec(memory_space=pltpu.SEMAPHORE)
DATAFLOW = pltpu.SideEffectType.DATAFLOW_SIDE_EFFECTING


def _split_copies(x_ref, land_ref, send_sems, recv_sems, masks, slot_kind, per_dest, n_split):
    mx, my, mc = lax.axis_index("x"), lax.axis_index("y"), lax.axis_index("c")
    slot = lambda px, py, pc: {"chip": 2 * px + py, "core": pc}[slot_kind]
    rows = (x_ref.shape[1] if per_dest else x_ref.shape[0]) // n_split
    part = lambda ref, j: ref.at[pl.ds(j * rows, rows)]
    sends, recvs = [], []
    for k, m in enumerate(masks):
        px, py, pc = (mx + m[0]) % 2, (my + m[1]) % 2, (mc + m[2]) % 2
        src = x_ref.at[2 * px + py] if per_dest else x_ref
        own_src = x_ref.at[2 * mx + my] if per_dest else x_ref
        for j in range(n_split):
            sems = dict(send_sem=send_sems.at[k * n_split + j], recv_sem=recv_sems.at[k * n_split + j],
                        device_id=(px, py, pc), device_id_type=MESH)
            sends.append(pltpu.make_async_remote_copy(
                src_ref=part(src, j), dst_ref=part(land_ref.at[slot(mx, my, mc)], j), **sems))
            recvs.append(pltpu.make_async_remote_copy(
                src_ref=part(own_src, j), dst_ref=part(land_ref.at[slot(px, py, pc)], j), **sems))
    return sends, recvs


def _exchange_start(name, x, masks, slot_kind, per_dest, n_split, dep=None):
    n = len(masks) * n_split
    blk_shape = x.shape[1:] if per_dest else x.shape
    land_shape = ({"chip": 4, "core": 2}[slot_kind],) + tuple(blk_shape)
    deps = [] if dep is None else [dep]

    def body(x_ref, land_ref, *rest):
        send_sems, recv_sems, x_thru, land_thru, token = rest[len(deps):]
        for cp in _split_copies(x_ref, land_ref, send_sems, recv_sems, masks, slot_kind, per_dest, n_split)[0]:
            cp.start()
        token[...] = jnp.zeros_like(token)

    return pl.pallas_call(
        body, name=name,
        out_shape=(pltpu.SemaphoreType.DMA((n,)), pltpu.SemaphoreType.DMA((n,)), pltpu.HBM(x.shape, x.dtype),
                   pltpu.HBM(land_shape, x.dtype), jax.ShapeDtypeStruct((8, 128), F32)),
        in_specs=(HBM_SPEC, HBM_SPEC) + (pl.BlockSpec(memory_space=pl.ANY),) * len(deps),
        out_specs=(SEM_SPEC, SEM_SPEC, HBM_SPEC, HBM_SPEC, pl.BlockSpec(memory_space=pltpu.VMEM)),
        input_output_aliases={0: 2, 1: 3},
        compiler_params=pltpu.CompilerParams(has_side_effects=DATAFLOW),
    )(pltpu.with_memory_space_constraint(x, pltpu.HBM),
      pltpu.with_memory_space_constraint(lax.empty(land_shape, x.dtype), pltpu.HBM), *deps)


def _exchange_wait(name, started, after, masks, slot_kind, per_dest, n_split):
    send_sems, recv_sems, x_thru, land_thru, _ = started

    def body(x_ref, land_ref, send_sems, recv_sems, after_ref, x_out, land_out):
        sends, recvs = _split_copies(x_ref, land_ref, send_sems, recv_sems, masks, slot_kind, per_dest, n_split)
        for cp in sends:
            cp.wait_send()
        for cp in recvs:
            cp.wait_recv()

    return pl.pallas_call(
        body, name=name,
        out_shape=(pltpu.HBM(x_thru.shape, x_thru.dtype), pltpu.HBM(land_thru.shape, land_thru.dtype)),
        in_specs=(HBM_SPEC, HBM_SPEC, SEM_SPEC, SEM_SPEC, pl.BlockSpec(memory_space=pl.ANY)),
        out_specs=(HBM_SPEC, HBM_SPEC), input_output_aliases={0: 0, 1: 1},
        compiler_params=pltpu.CompilerParams(has_side_effects=DATAFLOW),
    )(x_thru, land_thru, send_sems, recv_sems, after)


def _gather_chips(name, x):
    return _exchange(name, x, CHIP_MASKS, "chip", False)


def _gather_shards(name, shard, n_split, overlap=False, dep=None):
    half = shard.shape[0] // 2
    mine = lax.dynamic_slice_in_dim(shard, lax.axis_index("c") * half, half, axis=0)
    if overlap:
        return _exchange_start(name + "_chips", mine, CHIP_MASKS, "chip", False, n_split, dep)
    by_chip = _exchange(name + "_chips", mine, CHIP_MASKS, "chip", False, n_split, copy_own=False)
    return _gather_tail(name, mine, by_chip)


def _gather_tail(name, mine, by_chip):
    c, chip = lax.axis_index("c"), 2 * lax.axis_index("x") + lax.axis_index("y")
    by_chip = lax.dynamic_update_index_in_dim(by_chip, mine, chip, 0)
    both = _exchange(name + "_cores", by_chip, CORE_MASKS, "core", False, N_CHIPS, copy_own=False)
    return lax.dynamic_update_index_in_dim(both, by_chip, c, 0)


def _gather_finish(name, started, after, n_split):
    mine, by_chip = _exchange_wait(name + "_chips_wait", started, after, CHIP_MASKS, "chip", False, n_split)
    return _gather_tail(name, mine, by_chip)


def _sum_slots(name, x, tr=128):
    S, R, N = x.shape
    tr = _pick(R, tr, 16)

    def body(x_ref, o_ref):
        acc = x_ref[0].astype(F32)
        for s in range(1, S):
            acc = acc + x_ref[s].astype(F32)
        o_ref[...] = acc

    return pl.pallas_call(
        body, name=name, grid=(R // tr,),
        in_specs=[pl.BlockSpec((S, tr, N), lambda i: (0, i, 0))], out_specs=pl.BlockSpec((tr, N), lambda i: (i, 0)),
        out_shape=jax.ShapeDtypeStruct((R, N), F32), compiler_params=_params(("parallel",)),
    )(x)


def _add_to_bf16(name, a, b, tr=128):
    S, R, N = a.shape
    tr = _pick(R, tr, 16)

    def body(a_ref, b_ref, o_ref):
        o_ref[...] = (a_ref[...] + b_ref[...]).astype(BF16)

    spec = pl.BlockSpec((S, tr, N), lambda i: (0, i, 0))
    return pl.pallas_call(
        body, name=name, grid=(R // tr,), in_specs=[spec, spec], out_specs=spec,
        out_shape=jax.ShapeDtypeStruct((S, R, N), BF16), compiler_params=_params(("parallel",)),
    )(a, b)


def _add_slabs_to_bf16(name, full, recv, c, tr=64):
    R, NP = full.shape
    half = R // 2
    nb = half // tr

    def body(c_ref, a_ref, b_ref, o_ref):
        x = a_ref[...] + b_ref[...]
        for s in range(N_CHIPS - 1):
            o_ref[s] = x[:, s * SHARD_COLS:(s + 1) * SHARD_COLS].astype(BF16)
        last = jnp.concatenate([x[:, (N_CHIPS - 1) * SHARD_COLS:GAP_AT], x[:, GAP_AT + GAP:]], axis=1)
        o_ref[N_CHIPS - 1] = last.astype(BF16)

    grid_spec = pltpu.PrefetchScalarGridSpec(
        num_scalar_prefetch=1, grid=(nb,),
        in_specs=[pl.BlockSpec((tr, NP), lambda i, c_ref: (c_ref[0] * nb + i, 0)),
                  pl.BlockSpec((tr, NP), lambda i, c_ref: (i, 0))],
        out_specs=pl.BlockSpec((N_CHIPS, tr, SHARD_COLS), lambda i, c_ref: (0, i, 0)))
    return pl.pallas_call(
        body, name=name, grid_spec=grid_spec,
        out_shape=jax.ShapeDtypeStruct((N_CHIPS, half, SHARD_COLS), BF16), compiler_params=_params(("parallel",)),
    )(jnp.reshape(c, (1,)).astype(jnp.int32), full, recv)


def _adamw(name, gparts, w, m, v, tr=128):
    S, R, N = gparts.shape
    tr = _pick(R, tr)
    c1 = 1.0 / (1.0 - ADAM_B1 ** ADAM_STEP)
    c2 = 1.0 / (1.0 - ADAM_B2 ** ADAM_STEP)

    def body(g_ref, w_ref, m_ref, v_ref, go_ref, d_ref, mo_ref, vo_ref):
        g = g_ref[0]
        for s in range(1, S):
            g = g + g_ref[s]
        m_new = ADAM_B1 * m_ref[...] + (1.0 - ADAM_B1) * g
        v_new = ADAM_B2 * v_ref[...] + (1.0 - ADAM_B2) * jnp.square(g)
        go_ref[...] = g
        mo_ref[...] = m_new
        vo_ref[...] = v_new
        d_ref[...] = -ADAM_LR * ((m_new * c1) / (jnp.sqrt(v_new * c2) + ADAM_EPS) + ADAM_WD * w_ref[...])

    spec = pl.BlockSpec((tr, N), lambda i: (i, 0))
    return pl.pallas_call(
        body, name=name, grid=(R // tr,),
        in_specs=[pl.BlockSpec((S, tr, N), lambda i: (0, i, 0)), spec, spec, spec], out_specs=[spec] * 4,
        out_shape=[jax.ShapeDtypeStruct((R, N), F32)] * 4, compiler_params=_params(("parallel",)),
    )(gparts, w, m, v)


def _reduce_to_shard(name, slabs, n_split, by_columns=False, overlap=False):
    c, chip = lax.axis_index("c"), 2 * lax.axis_index("x") + lax.axis_index("y")
    if by_columns:
        R, N = slabs.shape[0], SHARD_COLS
        from_sibling = _exchange(name + "_sib", slabs, CORE_MASKS, None, False, n_split, other_half=True)[0]
        wire = _add_slabs_to_bf16(name + "_add", slabs, from_sibling, c)
    else:
        _, R, N = slabs.shape
        half = R // 2
        halves = slabs.reshape(N_CHIPS, 2, half, N)
        mine = lax.dynamic_index_in_dim(halves, c, axis=1, keepdims=False)
        theirs = lax.dynamic_index_in_dim(halves, 1 - c, axis=1, keepdims=False)
        from_sibling = _exchange(name + "_sib", theirs, CORE_MASKS, None, False, N_CHIPS)[0]
        wire = _add_to_bf16(name + "_add", mine, from_sibling)
    if overlap:
        return _exchange_start(name + "_chips", wire, CHIP_MASKS, "chip", True, n_split), (R, N)
    got = _exchange(name + "_chips", wire, CHIP_MASKS, "chip", True, n_split, copy_own=False)
    return _reduce_tail(name, wire, got, n_split, R, N)


def _reduce_tail(name, wire, got, n_split, R, N):
    c, chip = lax.axis_index("c"), 2 * lax.axis_index("x") + lax.axis_index("y")
    got = lax.dynamic_update_index_in_dim(got, lax.dynamic_index_in_dim(wire, chip, 0, keepdims=False), chip, 0)
    part = _sum_slots(name + "_sum", got)
    both = _exchange(name + "_cores", part, CORE_MASKS, "core", False, n_split, copy_own=False)
    return lax.dynamic_update_index_in_dim(both, part, c, 0).reshape(1, R, N)


def _reduce_finish(name, started, after, n_split):
    handle, (R, N) = started
    wire, got = _exchange_wait(name + "_chips_wait", handle, after, CHIP_MASKS, "chip", True, n_split)
    return _reduce_tail(name, wire, got, n_split, R, N)


def _pack(pieces, cols, row_mult=8):
    flat = jnp.concatenate([p.reshape(-1) for p in pieces])
    rows = -(-flat.shape[0] // cols)
    rows = -(-rows // row_mult) * row_mult
    return jnp.pad(flat, (0, rows * cols - flat.shape[0])).reshape(rows, cols)


def _unpack(packed, shapes):
    flat = packed.reshape(-1)
    out, off = [], 0
    for shp in shapes:
        n = 1
        for d in shp:
            n *= d
        out.append(flat[off:off + n].reshape(shp))
        off += n
    return out


def _to_my_layout(w):
    z = jnp.zeros((w.shape[0], 112), w.dtype)
    return jnp.concatenate([w[:, 0:3072], w[:, 3200:4224], w[:, 4224:7296], w[:, 7312:8336],
                            w[:, 3072:3200], w[:, 7296:7312], z], axis=1)


def kernel(x, meta_tokens, norm_w, w_in, rw_shift_mu, rw_w0, rw_w2, rw_a0, rw_a2, rw_k_k, rw_k_a, rw_r_k, rw_gn_w, rw_gn_b, dn_conv_w, dn_A_log, dn_dt_bias, dn_norm_w, w_out, final_norm_w, loss_target, m_meta_tokens, m_norm_w, m_w_in, m_rw_shift_mu, m_rw_w0, m_rw_w2, m_rw_a0, m_rw_a2, m_rw_k_k, m_rw_k_a, m_rw_r_k, m_rw_gn_w, m_rw_gn_b, m_dn_conv_w, m_dn_A_log, m_dn_dt_bias, m_dn_norm_w, m_w_out, m_final_norm_w, v_meta_tokens, v_norm_w, v_w_in, v_rw_shift_mu, v_rw_w0, v_rw_w2, v_rw_a0, v_rw_a2, v_rw_k_k, v_rw_k_a, v_rw_r_k, v_rw_gn_w, v_rw_gn_b, v_dn_conv_w, v_dn_A_log, v_dn_dt_bias, v_dn_norm_w, v_w_out, v_final_norm_w):
    S = x.shape[1]
    L = N_META + S
    Lp = -(-L // CHUNK) * CHUNK

    small_shapes = [(RW_LORA, 256), (RW_LORA, 256), (CONV_W, 768), (N_META, 512)]
    small_mine = _pack([rw_w2[0], rw_a2[0], dn_conv_w[0], meta_tokens], 1024)
    small_all = _gather_chips("gather_small", small_mine)
    per_chip = [_unpack(small_all[s], small_shapes) for s in range(N_CHIPS)]
    w2, a2, conv_w, meta = [jnp.concatenate([per_chip[s][i] for s in range(N_CHIPS)], axis=1) for i in range(4)]
    w_in_started = _gather_shards("gather_w_in", w_in[0].astype(BF16), 8, overlap=True, dep=small_all)
    w_out_started = _gather_shards("gather_w_out", w_out[0].astype(BF16), 8, overlap=True, dep=w_in_started[4])

    tail = [jnp.zeros((Lp - L, D_MODEL), F32)] if Lp > L else []
    h = jnp.concatenate([meta + w_in_started[4][:1, :1], x[0]] + tail, axis=0)
    tgt = jnp.concatenate([jnp.zeros((N_META, D_MODEL), F32), loss_target[0]] + tail, axis=0)
    (u,) = _rowwise("rms_in", _rms_fn, [_row(h)], [norm_w], [D_MODEL], tm=2 * ROW_TILE, out_dtype=BF16)
    w_in_all = _gather_finish("gather_w_in", w_in_started, u, 8)
    W_orig = jnp.transpose(w_in_all, (0, 2, 1, 3)).reshape(D_MODEL, IN_COLS)
    W = _to_my_layout(W_orig)
    W_gapped_t = jnp.concatenate([W_orig[:, :GAP_AT], jnp.zeros((D_MODEL, GAP), BF16), W_orig[:, GAP_AT:]], axis=1).T
    p = _mm("in_proj", u, W, "nn", dep=w_out_started[4])

    mu = rw_shift_mu
    zpad = jnp.zeros((RW_LORA, RW_WIDTH), F32)
    rw_params = [mu[:, 0:1024], mu[:, 1024:2048], mu[:, 2048:3072], mu[:, 3072:3200], rw_w0,
                 jnp.concatenate([w2, zpad], axis=0), rw_a0, jnp.concatenate([zpad, a2], axis=0), rw_k_k, rw_k_a]
    rw_rows = [_row(p, 1024, CB_R), _row(p, 1024, CB_K), _row(p, 1024, CB_V), _row(p, 128, CB_S1),
               _row(p, 1024, CB_GATE)]
    rw_pars = rw_params + [rw_r_k, rw_gn_w, rw_gn_b]
    ya, rw_ck, rw_t = _chunk_fwd("rw_chunk_fwd", _rw_fused_fn, rw_rows, 4, rw_pars, GROUP)

    dn_rows = [_row(p, 1024, CB_DQ), _row(p, 1024, CB_DK), _row(p, 1024, CB_DV), _row(p, 128, CB_S2),
               _row(p, 1024, CB_Z)]
    dn_pars = [conv_w[j:j + 1, 1024 * i:1024 * (i + 1)] for i in range(3) for j in range(CONV_W)]
    narrow = lambda t: jnp.pad(t, ((0, 0), (DN_HEADS, DN_HEAD - 2 * DN_HEADS)))
    dn_pars += [narrow(dn_A_log), narrow(dn_dt_bias), jnp.tile(dn_norm_w, (1, DN_HEADS))]
    yb, dn_ck, dn_t = _chunk_fwd("dn_chunk_fwd", _dn_fused_fn, dn_rows, 3, dn_pars, CHUNK)

    y = jnp.concatenate([ya, yb], axis=1)
    w_out_all = _gather_finish("gather_w_out", w_out_started, y, 8)
    Wo = jnp.transpose(w_out_all, (1, 0, 2, 3)).reshape(D_MODEL, D_MODEL)
    yw = _mm("out_proj", y, Wo, "nn", tn=1024)
    loss_acc, dz, dz16, d_fw = _loss_head(h, yw, tgt, final_norm_w.reshape(1, D_MODEL), L)
    loss = lax.psum(loss_acc[0, 0], ("x", "y", "c"))

    d_wo = _mm("d_w_out", y.T, dz16, "nn", tm=512, tn=1024, tk=Lp)
    wo_started = _reduce_to_shard("rs_w_out", d_wo.reshape(N_CHIPS, D_MODEL // N_CHIPS, D_MODEL), 8, overlap=True)
    dy = _mm("d_out_proj", dz16, Wo.T, "nn", tn=1024, tk=2048, dep=wo_started[0][4])

    dp_rw, d_rw_pars = _chunk_bwd("rw_chunk_bwd", _rw_fused_fn, rw_rows, 4, rw_pars, rw_ck, rw_t, _row(dy, 1024, 0))
    d_prep_pars, d_rw_pars = d_rw_pars[:10], d_rw_pars[10:]
    dp_dn, d_dn_pars = _chunk_bwd("dn_chunk_bwd", _dn_fused_fn, dn_rows, 3, dn_pars, dn_ck, dn_t, _row(dy, 1024, 1))
    d_conv_parts = [jnp.concatenate(d_dn_pars[4 * i:4 * i + 4], axis=0) for i in range(3)]
    d_a_log_b, d_dt_b = d_dn_pars[12], d_dn_pars[13]
    d_dn_nw = jnp.sum(d_dn_pars[14].reshape(DN_HEADS, DN_HEAD), axis=0, keepdims=True)

    dp = jnp.concatenate(list(dp_rw[:3]) + [dp_rw[3], dp_rw[4]] + list(dp_dn), axis=1)
    d_W = _mm("d_w_in", u.T, dp, "nn", tm=1024, tn=768, tk=Lp)
    wi_started = _reduce_to_shard("rs_w_in", d_W, 8, by_columns=True, overlap=True)
    du = _mm("d_in_proj", dp, W_gapped_t, "nn", tn=2048, tk=1408, dep=wi_started[0][4])
    (dh,), (d_norm_w,) = _rowwise_bwd("rms_in_bwd", lambda h_, w_: (_rms_fn(h_, w_)[0], h_), [_row(h)], [norm_w],
                                      [_row(du), _row(dz)])
    grad_x = dh[N_META:L][None]

    d_mu = jnp.concatenate(d_prep_pars[0:4], axis=1)
    d_w2, d_a2 = d_prep_pars[5][:RW_LORA], d_prep_pars[7][RW_LORA:]
    d_conv = jnp.concatenate(d_conv_parts, axis=1)
    d_meta = dh[:N_META]
    head_sum = lambda t: t[:, DN_HEADS:2 * DN_HEADS]
    rep_names = ["norm_w", "rw_shift_mu", "rw_w0", "rw_a0", "rw_k_k", "rw_k_a", "rw_r_k", "rw_gn_w", "rw_gn_b",
                 "dn_A_log", "dn_dt_bias", "dn_norm_w", "final_norm_w"]
    rep_g = [d_norm_w, d_mu, d_prep_pars[4], d_prep_pars[6], d_prep_pars[8], d_prep_pars[9],
             d_rw_pars[0], d_rw_pars[1], d_rw_pars[2],
             head_sum(d_a_log_b), head_sum(d_dt_b), d_dn_nw, d_fw.reshape(D_MODEL)]
    rep_w = [norm_w, rw_shift_mu, rw_w0, rw_a0, rw_k_k, rw_k_a, rw_r_k, rw_gn_w, rw_gn_b, dn_A_log, dn_dt_bias,
             dn_norm_w, final_norm_w]
    rep_m = [m_norm_w, m_rw_shift_mu, m_rw_w0, m_rw_a0, m_rw_k_k, m_rw_k_a, m_rw_r_k, m_rw_gn_w, m_rw_gn_b,
             m_dn_A_log, m_dn_dt_bias, m_dn_norm_w, m_final_norm_w]
    rep_v = [v_norm_w, v_rw_shift_mu, v_rw_w0, v_rw_a0, v_rw_k_k, v_rw_k_a, v_rw_r_k, v_rw_gn_w, v_rw_gn_b,
             v_dn_A_log, v_dn_dt_bias, v_dn_norm_w, v_final_norm_w]
    rep_shapes = [t.shape for t in rep_w]
    rep_all = _exchange("gather_rep_grads", _pack(rep_g, 128), ALL_MASKS, "dev", False)
    rep_out = _adamw("adam_rep", rep_all, _pack(rep_w, 128), _pack(rep_m, 128), _pack(rep_v, 128))
    rep_out = [dict(zip(rep_names, _unpack(t, rep_shapes))) for t in rep_out]

    sm_slabs = jnp.stack([_pack([d_w2[:, 256 * s:256 * (s + 1)], d_a2[:, 256 * s:256 * (s + 1)],
                                 d_conv[:, 768 * s:768 * (s + 1)], d_meta[:, 512 * s:512 * (s + 1)]], 1024, 64)
                          for s in range(N_CHIPS)])
    sm_parts = _reduce_to_shard("rs_small", sm_slabs, 1)
    sm_w = [rw_w2[0], rw_a2[0], dn_conv_w[0], meta_tokens]
    sm_m = [m_rw_w2[0], m_rw_a2[0], m_dn_conv_w[0], m_meta_tokens]
    sm_v = [v_rw_w2[0], v_rw_a2[0], v_dn_conv_w[0], v_meta_tokens]
    sm_out = _adamw("adam_small", sm_parts, _pack(sm_w, 1024, 64), _pack(sm_m, 1024, 64), _pack(sm_v, 1024, 64))
    sm_names = ["rw_w2", "rw_a2", "dn_conv_w", "meta_tokens"]
    sm_full_shapes = [(1, RW_LORA, 256), (1, RW_LORA, 256), (1, CONV_W, 768), (N_META, 512)]
    sm_out = [dict(zip(sm_names, [t.reshape(shp) for t, shp in zip(_unpack(o, small_shapes), sm_full_shapes)]))
              for o in sm_out]

    wo_parts = _reduce_finish("rs_w_out", wo_started, dp, 8)
    wo_out = _adamw("adam_w_out", wo_parts, w_out[0], m_w_out[0], v_w_out[0])
    wi_parts = _reduce_finish("rs_w_in", wi_started, dh, 8)
    wi_out = _adamw("adam_w_in", wi_parts, w_in[0], m_w_in[0], v_w_in[0])

    order = ["meta_tokens", "norm_w", "w_in", "rw_shift_mu", "rw_w0", "rw_w2", "rw_a0", "rw_a2", "rw_k_k", "rw_k_a",
             "rw_r_k", "rw_gn_w", "rw_gn_b", "dn_conv_w", "dn_A_log", "dn_dt_bias", "dn_norm_w", "w_out",
             "final_norm_w"]
    outs = [loss, grad_x]
    for kind in range(4):
        table = dict(rep_out[kind])
        table.update(sm_out[kind])
        table["w_in"] = wi_out[kind][None]
        table["w_out"] = wo_out[kind][None]
        outs += [table[n] for n in order]
    return tuple(outs)
```

```python
import functools

import jax
import jax.numpy as jnp
from jax import lax
from jax.experimental import pallas as pl
from jax.experimental.pallas import tpu as pltpu

F32 = jnp.float32
BF16 = jnp.bfloat16
HIGH = lax.Precision.HIGH
HIGHEST = lax.Precision.HIGHEST
MESH = pl.DeviceIdType.MESH

D_MODEL = 2048
N_META = 16
RW_WIDTH = 1024
RW_HEAD = 64
RW_HEADS = 16
RW_LORA = 64
RW_GN_EPS = 64e-5
DN_WIDTH = 1024
DN_HEAD = 128
DN_HEADS = 8
CONV_W = 4
CHUNK = 64
NORM_EPS = 1e-6
IN_COLS = 8336
N_CHIPS = 4
SHARD_COLS = IN_COLS // N_CHIPS

NP_COLS = 8 * 1024 + 256
GAP_AT = IN_COLS - DN_WIDTH
GAP = NP_COLS - IN_COLS
CB_R, CB_K, CB_V, CB_GATE, CB_DQ, CB_DK, CB_DV, CB_Z = range(8)
CB_S1 = 8192 // 128
CB_S2 = CB_S1 + 1

ADAM_LR = 0.001
ADAM_B1 = 0.9
ADAM_B2 = 0.999
ADAM_EPS = 1e-08
ADAM_WD = 0.01
ADAM_STEP = 10

VMEM_LIMIT_BYTES = 56 * 1024 * 1024
ROW_TILE = 104
MM_ROW_TILE = 832


def _params(sem=None):
    return pltpu.CompilerParams(dimension_semantics=sem, vmem_limit_bytes=VMEM_LIMIT_BYTES)


def _pick(n, target, mult=8):
    best = None
    for d in range(mult, min(n, target) + 1, mult):
        if n % d == 0:
            best = d
    return n if best is None else best


def _mm(name, a, b, mode, tm=MM_ROW_TILE, tn=1408, tk=2048, dep=None):
    if mode == "nn":
        (M, K), (_, N) = a.shape, b.shape
    elif mode == "nt":
        (M, K), (N, _) = a.shape, b.shape
    else:
        (K, M), (_, N) = a.shape, b.shape
    tm = _pick(M, tm, 128 if mode == "tn" else 16)
    tn = _pick(N, tn, 128)
    tk = _pick(K, tk, 8 if mode == "tn" else 128)
    if mode == "nn":
        a_spec = pl.BlockSpec((tm, tk), lambda i, j, k: (i, k))
        b_spec = pl.BlockSpec((tk, tn), lambda i, j, k: (k, j))
        dims = (((1,), (0,)), ((), ()))
    elif mode == "nt":
        a_spec = pl.BlockSpec((tm, tk), lambda i, j, k: (i, k))
        b_spec = pl.BlockSpec((tn, tk), lambda i, j, k: (j, k))
        dims = (((1,), (1,)), ((), ()))
    else:
        a_spec = pl.BlockSpec((tk, tm), lambda i, j, k: (k, i))
        b_spec = pl.BlockSpec((tk, tn), lambda i, j, k: (k, j))
        dims = (((0,), (0,)), ((), ()))

    def body(a_ref, b_ref, *rest):
        o_ref = rest[-1]

        @pl.when(pl.program_id(2) == 0)
        def _():
            o_ref[...] = jnp.zeros_like(o_ref)

        o_ref[...] += lax.dot_general(a_ref[...].astype(BF16), b_ref[...].astype(BF16), dims,
                                      preferred_element_type=F32)

    deps = [] if dep is None else [dep]
    return pl.pallas_call(
        body, name=name, grid=(M // tm, N // tn, K // tk),
        in_specs=[a_spec, b_spec] + [pl.BlockSpec((8, 128), lambda i, j, k: (0, 0))] * len(deps),
        out_specs=pl.BlockSpec((tm, tn), lambda i, j, k: (i, j)),
        out_shape=jax.ShapeDtypeStruct((M, N), F32),
        compiler_params=_params(("parallel", "parallel", "arbitrary")),
    )(a, b, *deps)


def _row(arr, width=None, cb=0):
    return (arr, arr.shape[1] if width is None else width, cb)


def _rowwise(name, fn, rows, params, out_widths, tm=ROW_TILE, out_dtype=F32):
    R = rows[0][0].shape[0]
    tm = _pick(R, tm, 16 if out_dtype == BF16 else 8)
    n_r, n_p = len(rows), len(params)

    def body(*refs):
        vals = [r[...] for r in refs[:n_r + n_p]]
        for o_ref, val in zip(refs[n_r + n_p:], fn(*vals)):
            o_ref[...] = val.astype(out_dtype)

    in_specs = [pl.BlockSpec((tm, w), lambda i, cb=cb: (i, cb)) for (_, w, cb) in rows]
    in_specs += [pl.BlockSpec(p.shape, lambda i: (0, 0)) for p in params]
    return pl.pallas_call(
        body, name=name, grid=(R // tm,), in_specs=in_specs,
        out_specs=[pl.BlockSpec((tm, w), lambda i: (i, 0)) for w in out_widths],
        out_shape=[jax.ShapeDtypeStruct((R, w), out_dtype) for w in out_widths],
        compiler_params=_params(("parallel",)),
    )(*[r[0] for r in rows], *params)


def _rowwise_bwd(name, fn, rows, params, douts, tm=ROW_TILE):
    R = rows[0][0].shape[0]
    tm = _pick(R, tm)
    n_r, n_p, n_d = len(rows), len(params), len(douts)

    def body(*refs):
        vals = [r[...] for r in refs[:n_r + n_p]]
        cts = tuple(r[...] for r in refs[n_r + n_p:n_r + n_p + n_d])
        grads = jax.vjp(fn, *vals)[1](cts)
        outs = refs[n_r + n_p + n_d:]
        for o_ref, g in zip(outs[:n_r], grads[:n_r]):
            o_ref[...] = g

        @pl.when(pl.program_id(0) == 0)
        def _():
            for o_ref in outs[n_r:]:
                o_ref[...] = jnp.zeros_like(o_ref)

        for o_ref, g in zip(outs[n_r:], grads[n_r:]):
            o_ref[...] += g

    in_specs = [pl.BlockSpec((tm, w), lambda i, cb=cb: (i, cb)) for (_, w, cb) in rows]
    in_specs += [pl.BlockSpec(p.shape, lambda i: (0, 0)) for p in params]
    in_specs += [pl.BlockSpec((tm, w), lambda i, cb=cb: (i, cb)) for (_, w, cb) in douts]
    out_specs = [pl.BlockSpec((tm, w), lambda i: (i, 0)) for (_, w, _) in rows]
    out_specs += [pl.BlockSpec(p.shape, lambda i: (0, 0)) for p in params]
    out_shape = [jax.ShapeDtypeStruct((R, w), F32) for (_, w, _) in rows]
    out_shape += [jax.ShapeDtypeStruct(p.shape, F32) for p in params]
    res = pl.pallas_call(
        body, name=name, grid=(R // tm,), in_specs=in_specs, out_specs=out_specs, out_shape=out_shape,
        compiler_params=_params(("arbitrary",)),
    )(*[r[0] for r in rows], *params, *[d[0] for d in douts])
    return res[:n_r], res[n_r:]


def _softplus(x):
    return jnp.maximum(x, 0.0) + jnp.log(1.0 + jnp.exp(-jnp.abs(x)))


def _silu(x):
    return x * jax.nn.sigmoid(x)


def _rms_fn(h, w):
    return (h * lax.rsqrt(jnp.mean(h * h, axis=-1, keepdims=True) + NORM_EPS) * w,)


def _rw_prep_fn(pr, pr1, pk, pk1, pv, pv1, ps, ps1, mu_r, mu_k, mu_v, mu_s, w0, w2p, a0, a2p, k_k, k_a):
    r = pr + (pr1 - pr) * mu_r
    k = pk + (pk1 - pk) * mu_k
    v = pv + (pv1 - pv) * mu_v
    s = ps + (ps1 - ps) * mu_s
    lora = lambda x, w: jnp.dot(x.astype(BF16), w.astype(BF16), preferred_element_type=F32)
    w_log = -_softplus(-(w0 + lora(jnp.tanh(s), w2p))) - 0.5
    log_decay = -jnp.exp(w_log)
    a = jax.nn.sigmoid(a0 + lora(s, a2p))
    return r, log_decay, k * (1.0 + (a - 1.0) * k_a), v, k * k_k, a


def _conv_fn(u0, u1, u2, u3, w0, w1, w2, w3):
    return (_silu(u0 * w3 + u1 * w2 + u2 * w1 + u3 * w0),)


def _dn_gate_fn(ps2, a_log_n, dt_n):
    beta_n = jax.nn.sigmoid(ps2)
    g_n = -jnp.exp(a_log_n) * _softplus(ps2 + dt_n)
    lane = lax.broadcasted_iota(jnp.int32, (1, DN_HEAD), 1)
    pick = lambda x, j: jnp.broadcast_to(jnp.sum(x * (lane == j).astype(F32), axis=-1, keepdims=True), x.shape)[None]
    beta = jnp.concatenate([pick(beta_n, h) for h in range(DN_HEADS)], axis=0)
    g = jnp.concatenate([pick(g_n, DN_HEADS + h) for h in range(DN_HEADS)], axis=0)
    return beta, g


def _tril_masks(c):
    t = lax.broadcasted_iota(jnp.int32, (c, c), 0)
    s = lax.broadcasted_iota(jnp.int32, (c, c), 1)
    return s <= t, s < t


def _dot(x, y, cx, cy, prec=HIGH):
    nb = x.ndim - 2
    batch = tuple(range(nb))
    return lax.dot_general(x, y, (((cx + nb,), (cy + nb,)), (batch, batch)), precision=prec,
                           preferred_element_type=F32)


def _tri_inv(a):
    c = a.shape[-1]
    eye = (lax.broadcasted_iota(jnp.int32, (c, c), 0) == lax.broadcasted_iota(jnp.int32, (c, c), 1)).astype(F32)
    x = eye - a
    p = a
    n = 2
    while n < c:
        p = _dot(p, p, 1, 0)
        x = x + _dot(x, p, 1, 0)
        n *= 2
    return x


@jax.custom_vjp
def _tri_inv_saved(a, t):
    return t


def _tri_inv_saved_fwd(a, t):
    return t, t


def _tri_inv_saved_bwd(t, dt):
    return -_dot(_dot(t, dt, 0, 0), t, 1, 1), jnp.zeros_like(t)


_tri_inv_saved.defvjp(_tri_inv_saved_fwd, _tri_inv_saved_bwd)


def _pair_masks():
    lane = lax.broadcasted_iota(jnp.int32, (1, 2 * RW_HEAD), 1)
    m0 = (lane < RW_HEAD).astype(F32)
    return m0, 1.0 - m0


def _pair_bd(x):
    m0, m1 = _pair_masks()
    return jnp.concatenate([x * m0, x * m1], axis=-2)


def _pair_mm(x, y):
    return _dot(x, _pair_bd(y), 1, 0)


def _pair_inv(a):
    c = a.shape[-2]
    row = lax.broadcasted_iota(jnp.int32, (c, 2 * RW_HEAD), 0)
    col = lax.broadcasted_iota(jnp.int32, (c, 2 * RW_HEAD), 1) & (RW_HEAD - 1)
    x = (row == col).astype(F32) - a
    p = a
    n = 2
    while n < c:
        p = _pair_mm(p, p)
        x = x + _pair_mm(x, p)
        n *= 2
    return x


@jax.custom_vjp
def _pair_inv_saved(a, t):
    return t


def _pair_inv_saved_fwd(a, t):
    return t, t


def _pair_inv_saved_bwd(t, dt):
    m0, m1 = _pair_masks()
    c = t.shape[-2]
    z = _dot(t, dt, 0, 0)
    x = z[:, :c, :] * m0 + z[:, c:, :] * m1
    return -_dot(x, _pair_bd(t), 1, 1), jnp.zeros_like(t)


_pair_inv_saved.defvjp(_pair_inv_saved_fwd, _pair_inv_saved_bwd)


def _rw_chunk_fn(S, r, lw, k2, v, kkp, a, gate, rk, gnw, gnb, t_saved=None):
    B, C, P = r.shape
    m0, m1 = _pair_masks()
    seg = lambda x: (jnp.sum(x * m0, axis=-1, keepdims=True) * m0 + jnp.sum(x * m1, axis=-1, keepdims=True) * m1)
    mm = lambda x, y: _dot(x.astype(BF16), y.astype(BF16), 1, 0, None)
    nt = lambda x, y: _dot(x, y, 1, 1)
    tn = lambda x, y: _dot(x, y, 0, 0)
    pair_mm = lambda x, y: mm(x, _pair_bd(y))
    t_idx = lax.broadcasted_iota(jnp.int32, (C, P), 0)
    s_idx = lax.broadcasted_iota(jnp.int32, (C, P), 1) & (RW_HEAD - 1)
    incl, strict = s_idx <= t_idx, s_idx < t_idx
    tril = jnp.broadcast_to(_tril_masks(C)[0].astype(F32), (B, C, C))
    kk = kkp * lax.rsqrt(seg(kkp * kkp) + 1e-6)
    b = kk * a
    g_incl = _dot(tril, lw, 1, 0)
    g_excl = g_incl - lw
    inv = jnp.exp(-g_incl)
    alpha, beta, kappa, rho = kk * jnp.exp(g_excl), b * inv, k2 * inv, r * jnp.exp(g_incl)
    ar = jnp.concatenate([alpha, rho], axis=-2)
    scores = nt(ar, jnp.concatenate([_pair_bd(beta), _pair_bd(kappa)], axis=-2))
    a_ab = jnp.where(strict, scores[:, :C, :P], 0.0)
    a_ak = jnp.where(strict, scores[:, :C, P:], 0.0)
    r_b = jnp.where(incl, scores[:, C:, :P], 0.0)
    r_k = jnp.where(incl, scores[:, C:, P:], 0.0)
    t_inv = _pair_inv(a_ab) if t_saved is None else _pair_inv_saved(a_ab, t_saved)
    on_state = nt(ar, S)
    u = pair_mm(t_inv, -on_state[:, :C, :] - pair_mm(a_ak, v))
    y = on_state[:, C:, :] + mm(jnp.concatenate([r_b, r_k], axis=-1),
                                jnp.concatenate([_pair_bd(u), _pair_bd(v)], axis=-2))
    same_head = ((lax.broadcasted_iota(jnp.int32, (P, P), 0) < RW_HEAD)
                 == (lax.broadcasted_iota(jnp.int32, (P, P), 1) < RW_HEAD))
    fresh = tn(jnp.concatenate([u, v], axis=-2), jnp.concatenate([beta, kappa], axis=-2))
    S_new = jnp.exp(jnp.sum(lw, axis=-2, keepdims=True)) * (S + jnp.where(same_head, fresh, 0.0))
    dev = y - seg(y) * (1.0 / RW_HEAD)
    yn = dev * lax.rsqrt(seg(dev * dev) * (1.0 / RW_HEAD) + RW_GN_EPS) * gnw + gnb
    bonus = seg(r * k2 * rk) * v
    return (yn + bonus) * _silu(gate), S_new, t_inv


def _dn_chunk_fn(S, qc, kc, vc, bb, gb, z, nw, t_saved=None):
    B, C, D = qc.shape
    mm = lambda x, y: _dot(x.astype(BF16), y.astype(BF16), 1, 0, None)
    nt = lambda x, y, p=None: _dot(x, y, 1, 1, p) if p else _dot(x.astype(BF16), y.astype(BF16), 1, 1, None)
    tn = lambda x, y: _dot(x.astype(BF16), y.astype(BF16), 0, 0, None)
    incl, strict = _tril_masks(C)
    q = qc * lax.rsqrt(jnp.sum(qc * qc, axis=-1, keepdims=True) + 1e-6) * (D ** -0.5)
    k = kc * lax.rsqrt(jnp.sum(kc * kc, axis=-1, keepdims=True) + 1e-6)
    kb, vb = k * bb, vc * bb
    G = _dot(jnp.broadcast_to(incl.astype(F32), (B, C, C)), gb, 1, 0, HIGHEST)
    lane = lax.broadcasted_iota(jnp.int32, (C, D), 1)
    e0, e1 = (lane == 0).astype(F32), (lane == 1).astype(F32)
    diff = nt(G * e0 + e1, e0 - G * e1, HIGHEST)
    dmask = jnp.where(incl, jnp.exp(jnp.where(incl, diff, 0.0)), 0.0)
    M = jnp.where(strict, nt(kb, k) * dmask, 0.0)
    T = _tri_inv(M) if t_saved is None else _tri_inv_saved(M, t_saved)
    eG = jnp.exp(G)
    u = mm(T, vb)
    w = mm(T, kb * eG)
    attn = jnp.where(incl, nt(q, k) * dmask, 0.0)
    v_new = u - mm(w, S)
    o = mm(q * eG, S) + mm(attn, v_new)
    g_last = jnp.sum(gb, axis=-2, keepdims=True)
    S_new = S * jnp.exp(jnp.broadcast_to(g_last, S.shape)) + tn(k * jnp.exp(g_last - G), v_new)
    on = o * lax.rsqrt(jnp.mean(o * o, axis=-1, keepdims=True) + NORM_EPS) * nw
    return on * _silu(z), S_new, T


N_GROUPS = 8
GROUP = 128
HALO = 8


def _groups(x):
    return jnp.concatenate([x[:, g * GROUP:(g + 1) * GROUP][None] for g in range(N_GROUPS)], axis=0)


@functools.partial(jax.custom_vjp, nondiff_argnums=(1,))
def _shift_rows(ext, j):
    return pltpu.roll(ext, j, 0)[HALO:, :]


def _shift_rows_fwd(ext, j):
    return _shift_rows(ext, j), None


def _shift_rows_bwd(j, _, d):
    z = jnp.concatenate([jnp.zeros((HALO, d.shape[1]), d.dtype), d], axis=0)
    return (pltpu.roll(z, z.shape[0] - j, 0),)


_shift_rows.defvjp(_shift_rows_fwd, _shift_rows_bwd)


def _rw_fused_fn(S, h_r, h_k, h_v, h_s, p_r, p_k, p_v, p_s, gate, *pars, t_saved=None):
    prev = lambda h, x: _shift_rows(jnp.concatenate([h, x], axis=0), 1)
    seq = _rw_prep_fn(p_r, prev(h_r, p_r), p_k, prev(h_k, p_k), p_v, prev(h_v, p_v), p_s, prev(h_s, p_s), *pars[:10])
    return _rw_chunk_fn(S, *[_groups(t) for t in seq], _groups(gate), *[_groups(t) for t in pars[10:]],
                        t_saved=t_saved)


def _dn_fused_fn(S, h_q, h_k, h_v, p_q, p_k, p_v, p_s2, z, *pars, t_saved=None):
    conv = []
    for i, (h, x) in enumerate(((h_q, p_q), (h_k, p_k), (h_v, p_v))):
        ext = jnp.concatenate([h, x], axis=0)
        conv += _conv_fn(x, _shift_rows(ext, 1), _shift_rows(ext, 2), _shift_rows(ext, 3), *pars[4 * i:4 * i + 4])
    beta, g = _dn_gate_fn(p_s2, pars[12], pars[13])
    return _dn_chunk_fn(S, *[_groups(t) for t in conv], beta, g, _groups(z), _groups(pars[14]), t_saved=t_saved)


def _chunk_fwd(name, fn, rows, n_halo, pars, t_lanes):
    Lp = rows[0][0].shape[0]
    C, D, NB = CHUNK, GROUP, N_GROUPS
    nc, n_r, n_p, per = Lp // C, len(rows), len(pars), CHUNK // HALO

    def body(*refs):
        row_refs, halo_refs, par_refs = refs[:n_r], refs[n_r:n_r + n_halo], refs[n_r + n_halo:n_r + n_halo + n_p]
        y_ref, ck_ref, t_ref, s_ref = refs[n_r + n_halo + n_p:]
        first = pl.program_id(0) == 0

        @pl.when(first)
        def _():
            s_ref[...] = jnp.zeros_like(s_ref)

        S = s_ref[...]
        ck_ref[0] = S
        halos = [jnp.where(first, 0.0, r[...]) for r in halo_refs]
        y, S_new, t_inv = fn(S, *halos, *[r[...] for r in row_refs], *[r[...] for r in par_refs])
        for g in range(NB):
            y_ref[:, g * D:(g + 1) * D] = y[g].astype(BF16)
        t_ref[0] = t_inv
        s_ref[...] = S_new

    return pl.pallas_call(
        body, name=name, grid=(nc,),
        in_specs=[pl.BlockSpec((C, w), lambda c, cb=cb: (c, cb)) for (_, w, cb) in rows]
        + [pl.BlockSpec((HALO, w), lambda c, cb=cb: (jnp.maximum(c * per - 1, 0), cb)) for (_, w, cb) in rows[:n_halo]]
        + [pl.BlockSpec(p.shape, lambda c: (0, 0)) for p in pars],
        out_specs=[pl.BlockSpec((C, NB * D), lambda c: (c, 0)),
                   pl.BlockSpec((1, NB, D, D), lambda c: (c, 0, 0, 0)),
                   pl.BlockSpec((1, NB, C, t_lanes), lambda c: (c, 0, 0, 0))],
        out_shape=[jax.ShapeDtypeStruct((Lp, NB * D), BF16), jax.ShapeDtypeStruct((nc, NB, D, D), F32),
                   jax.ShapeDtypeStruct((nc, NB, C, t_lanes), F32)],
        scratch_shapes=[pltpu.VMEM((NB, D, D), F32)],
        compiler_params=_params(("arbitrary",)),
    )(*[r[0] for r in rows], *[r[0] for r in rows[:n_halo]], *pars)


def _chunk_bwd(name, fn, rows, n_halo, pars, ckpt, tinv, dy):
    Lp = rows[0][0].shape[0]
    C, D, NB = CHUNK, GROUP, N_GROUPS
    nc, n_r, n_p, per = Lp // C, len(rows), len(pars), CHUNK // HALO
    t_lanes = tinv.shape[-1]
    n_in = n_r + n_halo + n_p

    def body(*refs):
        row_refs, halo_refs, par_refs = refs[:n_r], refs[n_r:n_r + n_halo], refs[n_r + n_halo:n_in]
        ck_ref, t_ref, dy_ref = refs[n_in:n_in + 3]
        outs = refs[n_in + 3:]
        drow_refs, dpar_refs = outs[:n_r], outs[n_r:n_r + n_p]
        ds_ref, carry_refs = outs[n_r + n_p], outs[n_r + n_p + 1:]
        i = pl.program_id(0)

        @pl.when(i == 0)
        def _():
            ds_ref[...] = jnp.zeros_like(ds_ref)
            for o_ref in list(dpar_refs) + list(carry_refs):
                o_ref[...] = jnp.zeros_like(o_ref)

        halos = [jnp.where(i == nc - 1, 0.0, r[...]) for r in halo_refs]
        vals = [ck_ref[0]] + halos + [r[...] for r in row_refs] + [r[...] for r in par_refs]
        t_saved = t_ref[0]
        grads = jax.vjp(lambda *v: fn(*v, t_saved=t_saved)[:2], *vals)[1]((_groups(dy_ref), ds_ref[...]))
        ds_ref[...] = grads[0]
        d_halos, d_rows, d_pars = grads[1:1 + n_halo], grads[1 + n_halo:1 + n_halo + n_r], grads[1 + n_halo + n_r:]
        for k, (o_ref, g) in enumerate(zip(drow_refs, d_rows)):
            if k < n_halo:
                g = g + jnp.concatenate([jnp.zeros((C - HALO, g.shape[1]), F32), carry_refs[k][...]], axis=0)
                carry_refs[k][...] = d_halos[k]
            o_ref[...] = g.astype(BF16)
        for o_ref, g in zip(dpar_refs, d_pars):
            o_ref[...] += g

    rev = lambda c: nc - 1 - c
    dy_arr, dy_w, dy_cb = dy
    res = pl.pallas_call(
        body, name=name, grid=(nc,),
        in_specs=[pl.BlockSpec((C, w), lambda c, cb=cb: (rev(c), cb)) for (_, w, cb) in rows]
        + [pl.BlockSpec((HALO, w), lambda c, cb=cb: (jnp.maximum(rev(c) * per - 1, 0), cb))
           for (_, w, cb) in rows[:n_halo]]
        + [pl.BlockSpec(p.shape, lambda c: (0, 0)) for p in pars]
        + [pl.BlockSpec((1, NB, D, D), lambda c: (rev(c), 0, 0, 0)),
           pl.BlockSpec((1, NB, C, t_lanes), lambda c: (rev(c), 0, 0, 0)),
           pl.BlockSpec((C, dy_w), lambda c: (rev(c), dy_cb))],
        out_specs=[pl.BlockSpec((C, w), lambda c: (rev(c), 0)) for (_, w, _) in rows]
        + [pl.BlockSpec(p.shape, lambda c: (0, 0)) for p in pars],
        out_shape=[jax.ShapeDtypeStruct((Lp, w), BF16) for (_, w, _) in rows]
        + [jax.ShapeDtypeStruct(p.shape, F32) for p in pars],
        scratch_shapes=[pltpu.VMEM((NB, D, D), F32)] + [pltpu.VMEM((HALO, w), F32) for (_, w, _) in rows[:n_halo]],
        compiler_params=_params(("arbitrary",)),
    )(*[r[0] for r in rows], *[r[0] for r in rows[:n_halo]], *pars, ckpt, tinv, dy_arr)
    return res[:n_r], res[n_r:]


def _loss_head(h, yw, tgt, fw, n_real, tm=2 * ROW_TILE):
    Lp, Dm = h.shape
    tm = _pick(Lp, tm, 16)

    def out_fn(z, fw_):
        return z * lax.rsqrt(jnp.mean(z * z, axis=-1, keepdims=True) + NORM_EPS) * fw_

    def body(h_ref, yw_ref, t_ref, fw_ref, loss_ref, dz_ref, dz16_ref, dfw_ref):
        i = pl.program_id(0)

        @pl.when(i == 0)
        def _():
            loss_ref[...] = jnp.zeros_like(loss_ref)
            dfw_ref[...] = jnp.zeros_like(dfw_ref)

        row = i * tm + lax.broadcasted_iota(jnp.int32, (tm, 1), 0)
        mask = ((row >= N_META) & (row < n_real)).astype(F32)
        z = h_ref[...] + yw_ref[...]
        o, vjp = jax.vjp(out_fn, z, fw_ref[...])
        err = (o - t_ref[...]) * mask
        row_loss = 0.5 * jnp.mean(jnp.square(err), axis=-1, keepdims=True)
        dz, dfw = vjp(err * (1.0 / Dm))
        loss_ref[...] += jnp.sum(row_loss, axis=0, keepdims=True)
        dz_ref[...] = dz
        dz16_ref[...] = dz.astype(BF16)
        dfw_ref[...] += dfw

    row_spec = pl.BlockSpec((tm, Dm), lambda i: (i, 0))
    return pl.pallas_call(
        body, name="loss_head", grid=(Lp // tm,),
        in_specs=[row_spec, row_spec, row_spec, pl.BlockSpec((1, Dm), lambda i: (0, 0))],
        out_specs=[pl.BlockSpec((8, 128), lambda i: (0, 0)), row_spec, row_spec,
                   pl.BlockSpec((1, Dm), lambda i: (0, 0))],
        out_shape=[jax.ShapeDtypeStruct((8, 128), F32), jax.ShapeDtypeStruct((Lp, Dm), F32),
                   jax.ShapeDtypeStruct((Lp, Dm), BF16), jax.ShapeDtypeStruct((1, Dm), F32)],
        compiler_params=_params(("arbitrary",)),
    )(h, yw, tgt, fw)


def _exchange(name, x, masks, slot_kind, per_dest, n_split=1, copy_own=True, other_half=False):
    n = len(masks)
    keep_own = slot_kind is not None and copy_own
    n_slots = {"chip": 4, "core": 2, "dev": 8, None: n}[slot_kind]
    blk_shape = x.shape[1:] if per_dest else x.shape
    if other_half:
        blk_shape = (x.shape[0] // 2,) + tuple(x.shape[1:])
    rows = blk_shape[0] // n_split

    def body(x_ref, o_ref, send_sems, recv_sems, local_sems):
        mx, my, mc = lax.axis_index("x"), lax.axis_index("y"), lax.axis_index("c")
        if other_half:
            x_ref = x_ref.at[pl.ds((1 - mc) * blk_shape[0], blk_shape[0])]

        def slot(k, px, py, pc):
            return {"chip": 2 * px + py, "core": pc, "dev": 4 * px + 2 * py + pc, None: k}[slot_kind]

        def peer(m):
            return (mx + m[0]) % 2, (my + m[1]) % 2, (mc + m[2]) % 2

        def part(ref, j):
            return ref.at[pl.ds(j * rows, rows)]

        own_src = x_ref.at[2 * mx + my] if per_dest else x_ref
        local = []
        if keep_own:
            own_dst = o_ref.at[slot(0, mx, my, mc)]
            local = [pltpu.make_async_copy(part(own_src, j), part(own_dst, j), local_sems.at[j])
                     for j in range(n_split)]
        for cp in local:
            cp.start()
        sends = []
        for k, m in enumerate(masks):
            px, py, pc = peer(m)
            src = x_ref.at[2 * px + py] if per_dest else x_ref
            dst = o_ref.at[slot(k, mx, my, mc)]
            for j in range(n_split):
                sends.append(pltpu.make_async_remote_copy(
                    src_ref=part(src, j), dst_ref=part(dst, j), send_sem=send_sems.at[k * n_split + j],
                    recv_sem=recv_sems.at[k * n_split + j], device_id=(px, py, pc), device_id_type=MESH))
        for cp in sends:
            cp.start()
        for k, m in enumerate(masks):
            px, py, pc = peer(m)
            landed = o_ref.at[slot(k, px, py, pc)]
            for j in range(n_split):
                pltpu.make_async_remote_copy(
                    src_ref=part(own_src, j), dst_ref=part(landed, j), send_sem=send_sems.at[k * n_split + j],
                    recv_sem=recv_sems.at[k * n_split + j], device_id=(px, py, pc), device_id_type=MESH).wait_recv()
        for cp in sends:
            cp.wait_send()
        for cp in local:
            cp.wait()

    return pl.pallas_call(
        body, name=name,
        in_specs=[pl.BlockSpec(memory_space=pl.ANY)], out_specs=pl.BlockSpec(memory_space=pl.ANY),
        out_shape=jax.ShapeDtypeStruct((n_slots,) + tuple(blk_shape), x.dtype),
        scratch_shapes=[pltpu.SemaphoreType.DMA((n * n_split,)), pltpu.SemaphoreType.DMA((n * n_split,)),
                        pltpu.SemaphoreType.DMA((n_split,))],
        compiler_params=pltpu.CompilerParams(has_side_effects=True),
    )(x)


CHIP_MASKS = [(1, 0, 0), (0, 1, 0), (1, 1, 0)]
CORE_MASKS = [(0, 0, 1)]
ALL_MASKS = [(dx, dy, dc) for dx in (0, 1) for dy in (0, 1) for dc in (0, 1) if (dx, dy, dc) != (0, 0, 0)]

HBM_SPEC = pl.BlockSpec(memory_space=pltpu.HBM)
SEM_SPEC = pl.BlockSpec(memory_space=pltpu.SEMAPHORE)
DATAFLOW = pltpu.SideEffectType.DATAFLOW_SIDE_EFFECTING


def _split_copies(x_ref, land_ref, send_sems, recv_sems, masks, slot_kind, per_dest, n_split):
    mx, my, mc = lax.axis_index("x"), lax.axis_index("y"), lax.axis_index("c")
    slot = lambda px, py, pc: {"chip": 2 * px + py, "core": pc}[slot_kind]
    rows = (x_ref.shape[1] if per_dest else x_ref.shape[0]) // n_split
    part = lambda ref, j: ref.at[pl.ds(j * rows, rows)]
    sends, recvs = [], []
    for k, m in enumerate(masks):
        px, py, pc = (mx + m[0]) % 2, (my + m[1]) % 2, (mc + m[2]) % 2
        src = x_ref.at[2 * px + py] if per_dest else x_ref
        own_src = x_ref.at[2 * mx + my] if per_dest else x_ref
        for j in range(n_split):
            sems = dict(send_sem=send_sems.at[k * n_split + j], recv_sem=recv_sems.at[k * n_split + j],
                        device_id=(px, py, pc), device_id_type=MESH)
            sends.append(pltpu.make_async_remote_copy(
                src_ref=part(src, j), dst_ref=part(land_ref.at[slot(mx, my, mc)], j), **sems))
            recvs.append(pltpu.make_async_remote_copy(
                src_ref=part(own_src, j), dst_ref=part(land_ref.at[slot(px, py, pc)], j), **sems))
    return sends, recvs


def _exchange_start(name, x, masks, slot_kind, per_dest, n_split, dep=None):
    n = len(masks) * n_split
    blk_shape = x.shape[1:] if per_dest else x.shape
    land_shape = ({"chip": 4, "core": 2}[slot_kind],) + tuple(blk_shape)
    deps = [] if dep is None else [dep]

    def body(x_ref, land_ref, *rest):
        send_sems, recv_sems, x_thru, land_thru, token = rest[len(deps):]
        for cp in _split_copies(x_ref, land_ref, send_sems, recv_sems, masks, slot_kind, per_dest, n_split)[0]:
            cp.start()
        token[...] = jnp.zeros_like(token)

    return pl.pallas_call(
        body, name=name,
        out_shape=(pltpu.SemaphoreType.DMA((n,)), pltpu.SemaphoreType.DMA((n,)), pltpu.HBM(x.shape, x.dtype),
                   pltpu.HBM(land_shape, x.dtype), jax.ShapeDtypeStruct((8, 128), F32)),
        in_specs=(HBM_SPEC, HBM_SPEC) + (pl.BlockSpec(memory_space=pl.ANY),) * len(deps),
        out_specs=(SEM_SPEC, SEM_SPEC, HBM_SPEC, HBM_SPEC, pl.BlockSpec(memory_space=pltpu.VMEM)),
        input_output_aliases={0: 2, 1: 3},
        compiler_params=pltpu.CompilerParams(has_side_effects=DATAFLOW),
    )(pltpu.with_memory_space_constraint(x, pltpu.HBM),
      pltpu.with_memory_space_constraint(lax.empty(land_shape, x.dtype), pltpu.HBM), *deps)


def _exchange_wait(name, started, after, masks, slot_kind, per_dest, n_split):
    send_sems, recv_sems, x_thru, land_thru, _ = started

    def body(x_ref, land_ref, send_sems, recv_sems, after_ref, x_out, land_out):
        sends, recvs = _split_copies(x_ref, land_ref, send_sems, recv_sems, masks, slot_kind, per_dest, n_split)
        for cp in sends:
            cp.wait_send()
        for cp in recvs:
            cp.wait_recv()

    return pl.pallas_call(
        body, name=name,
        out_shape=(pltpu.HBM(x_thru.shape, x_thru.dtype), pltpu.HBM(land_thru.shape, land_thru.dtype)),
        in_specs=(HBM_SPEC, HBM_SPEC, SEM_SPEC, SEM_SPEC, pl.BlockSpec(memory_space=pl.ANY)),
        out_specs=(HBM_SPEC, HBM_SPEC), input_output_aliases={0: 0, 1: 1},
        compiler_params=pltpu.CompilerParams(has_side_effects=DATAFLOW),
    )(x_thru, land_thru, send_sems, recv_sems, after)


def _gather_chips(name, x):
    return _exchange(name, x, CHIP_MASKS, "chip", False)


def _gather_shards(name, shard, n_split, overlap=False, dep=None):
    half = shard.shape[0] // 2
    mine = lax.dynamic_slice_in_dim(shard, lax.axis_index("c") * half, half, axis=0)
    if overlap:
        return _exchange_start(name + "_chips", mine, CHIP_MASKS, "chip", False, n_split, dep)
    by_chip = _exchange(name + "_chips", mine, CHIP_MASKS, "chip", False, n_split, copy_own=False)
    return _gather_tail(name, mine, by_chip)


def _gather_tail(name, mine, by_chip):
    c, chip = lax.axis_index("c"), 2 * lax.axis_index("x") + lax.axis_index("y")
    by_chip = lax.dynamic_update_index_in_dim(by_chip, mine, chip, 0)
    both = _exchange(name + "_cores", by_chip, CORE_MASKS, "core", False, N_CHIPS, copy_own=False)
    return lax.dynamic_update_index_in_dim(both, by_chip, c, 0)


def _gather_finish(name, started, after, n_split):
    mine, by_chip = _exchange_wait(name + "_chips_wait", started, after, CHIP_MASKS, "chip", False, n_split)
    return _gather_tail(name, mine, by_chip)


def _sum_slots(name, x, tr=128):
    S, R, N = x.shape
    tr = _pick(R, tr, 16)

    def body(x_ref, o_ref):
        acc = x_ref[0].astype(F32)
        for s in range(1, S):
            acc = acc + x_ref[s].astype(F32)
        o_ref[...] = acc

    return pl.pallas_call(
        body, name=name, grid=(R // tr,),
        in_specs=[pl.BlockSpec((S, tr, N), lambda i: (0, i, 0))], out_specs=pl.BlockSpec((tr, N), lambda i: (i, 0)),
        out_shape=jax.ShapeDtypeStruct((R, N), F32), compiler_params=_params(("parallel",)),
    )(x)


def _add_to_bf16(name, a, b, tr=128):
    S, R, N = a.shape
    tr = _pick(R, tr, 16)

    def body(a_ref, b_ref, o_ref):
        o_ref[...] = (a_ref[...] + b_ref[...]).astype(BF16)

    spec = pl.BlockSpec((S, tr, N), lambda i: (0, i, 0))
    return pl.pallas_call(
        body, name=name, grid=(R // tr,), in_specs=[spec, spec], out_specs=spec,
        out_shape=jax.ShapeDtypeStruct((S, R, N), BF16), compiler_params=_params(("parallel",)),
    )(a, b)


def _add_slabs_to_bf16(name, full, recv, c, tr=64):
    R, NP = full.shape
    half = R // 2
    nb = half // tr

    def body(c_ref, a_ref, b_ref, o_ref):
        x = a_ref[...] + b_ref[...]
        for s in range(N_CHIPS - 1):
            o_ref[s] = x[:, s * SHARD_COLS:(s + 1) * SHARD_COLS].astype(BF16)
        last = jnp.concatenate([x[:, (N_CHIPS - 1) * SHARD_COLS:GAP_AT], x[:, GAP_AT + GAP:]], axis=1)
        o_ref[N_CHIPS - 1] = last.astype(BF16)

    grid_spec = pltpu.PrefetchScalarGridSpec(
        num_scalar_prefetch=1, grid=(nb,),
        in_specs=[pl.BlockSpec((tr, NP), lambda i, c_ref: (c_ref[0] * nb + i, 0)),
                  pl.BlockSpec((tr, NP), lambda i, c_ref: (i, 0))],
        out_specs=pl.BlockSpec((N_CHIPS, tr, SHARD_COLS), lambda i, c_ref: (0, i, 0)))
    return pl.pallas_call(
        body, name=name, grid_spec=grid_spec,
        out_shape=jax.ShapeDtypeStruct((N_CHIPS, half, SHARD_COLS), BF16), compiler_params=_params(("parallel",)),
    )(jnp.reshape(c, (1,)).astype(jnp.int32), full, recv)


def _adamw(name, gparts, w, m, v, tr=128):
    S, R, N = gparts.shape
    tr = _pick(R, tr)
    c1 = 1.0 / (1.0 - ADAM_B1 ** ADAM_STEP)
    c2 = 1.0 / (1.0 - ADAM_B2 ** ADAM_STEP)

    def body(g_ref, w_ref, m_ref, v_ref, go_ref, d_ref, mo_ref, vo_ref):
        g = g_ref[0]
        for s in range(1, S):
            g = g + g_ref[s]
        m_new = ADAM_B1 * m_ref[...] + (1.0 - ADAM_B1) * g
        v_new = ADAM_B2 * v_ref[...] + (1.0 - ADAM_B2) * jnp.square(g)
        go_ref[...] = g
        mo_ref[...] = m_new
        vo_ref[...] = v_new
        d_ref[...] = -ADAM_LR * ((m_new * c1) / (jnp.sqrt(v_new * c2) + ADAM_EPS) + ADAM_WD * w_ref[...])

    spec = pl.BlockSpec((tr, N), lambda i: (i, 0))
    return pl.pallas_call(
        body, name=name, grid=(R // tr,),
        in_specs=[pl.BlockSpec((S, tr, N), lambda i: (0, i, 0)), spec, spec, spec], out_specs=[spec] * 4,
        out_shape=[jax.ShapeDtypeStruct((R, N), F32)] * 4, compiler_params=_params(("parallel",)),
    )(gparts, w, m, v)


def _adamw_tiles(name, g, w, m, v, block):
    c1 = 1.0 / (1.0 - ADAM_B1 ** ADAM_STEP)
    c2 = 1.0 / (1.0 - ADAM_B2 ** ADAM_STEP)

    def body(g_ref, w_ref, m_ref, v_ref, d_ref, mo_ref, vo_ref):
        g_ = g_ref[...]
        m_new = ADAM_B1 * m_ref[...] + (1.0 - ADAM_B1) * g_
        v_new = ADAM_B2 * v_ref[...] + (1.0 - ADAM_B2) * jnp.square(g_)
        mo_ref[...] = m_new
        vo_ref[...] = v_new
        d_ref[...] = -ADAM_LR * ((m_new * c1) / (jnp.sqrt(v_new * c2) + ADAM_EPS) + ADAM_WD * w_ref[...])

    spec = pl.BlockSpec(block, lambda i, j: (i, j, 0))
    return pl.pallas_call(
        body, name=name, grid=(w.shape[0] // block[0], w.shape[1] // block[1]),
        in_specs=[spec] * 4, out_specs=[spec] * 3,
        out_shape=[jax.ShapeDtypeStruct(w.shape, F32)] * 3, compiler_params=_params(("parallel", "parallel")),
    )(g, w, m, v)


def _reduce_to_shard(name, slabs, n_split, by_columns=False, overlap=False):
    c, chip = lax.axis_index("c"), 2 * lax.axis_index("x") + lax.axis_index("y")
    if by_columns:
        R, N = slabs.shape[0], SHARD_COLS
        from_sibling = _exchange(name + "_sib", slabs, CORE_MASKS, None, False, n_split, other_half=True)[0]
        wire = _add_slabs_to_bf16(name + "_add", slabs, from_sibling, c)
    else:
        _, R, N = slabs.shape
        half = R // 2
        halves = slabs.reshape(N_CHIPS, 2, half, N)
        mine = lax.dynamic_index_in_dim(halves, c, axis=1, keepdims=False)
        theirs = lax.dynamic_index_in_dim(halves, 1 - c, axis=1, keepdims=False)
        from_sibling = _exchange(name + "_sib", theirs, CORE_MASKS, None, False, N_CHIPS)[0]
        wire = _add_to_bf16(name + "_add", mine, from_sibling)
    if overlap:
        return _exchange_start(name + "_chips", wire, CHIP_MASKS, "chip", True, n_split), (R, N)
    got = _exchange(name + "_chips", wire, CHIP_MASKS, "chip", True, n_split, copy_own=False)
    return _reduce_tail(name, wire, got, n_split, R, N)


def _reduce_tail(name, wire, got, n_split, R, N):
    c, chip = lax.axis_index("c"), 2 * lax.axis_index("x") + lax.axis_index("y")
    got = lax.dynamic_update_index_in_dim(got, lax.dynamic_index_in_dim(wire, chip, 0, keepdims=False), chip, 0)
    part = _sum_slots(name + "_sum", got)
    both = _exchange(name + "_cores", part, CORE_MASKS, "core", False, n_split, copy_own=False)
    return lax.dynamic_update_index_in_dim(both, part, c, 0).reshape(1, R, N)


def _reduce_finish(name, started, after, n_split):
    handle, (R, N) = started
    wire, got = _exchange_wait(name + "_chips_wait", handle, after, CHIP_MASKS, "chip", True, n_split)
    return _reduce_tail(name, wire, got, n_split, R, N)


def _pack(pieces, cols, row_mult=8):
    flat = jnp.concatenate([p.reshape(-1) for p in pieces])
    rows = -(-flat.shape[0] // cols)
    rows = -(-rows // row_mult) * row_mult
    return jnp.pad(flat, (0, rows * cols - flat.shape[0])).reshape(rows, cols)


def _unpack(packed, shapes):
    flat = packed.reshape(-1)
    out, off = [], 0
    for shp in shapes:
        n = 1
        for d in shp:
            n *= d
        out.append(flat[off:off + n].reshape(shp))
        off += n
    return out


def _to_my_layout(w):
    z = jnp.zeros((w.shape[0], 112), w.dtype)
    return jnp.concatenate([w[:, 0:3072], w[:, 3200:4224], w[:, 4224:7296], w[:, 7312:8336],
                            w[:, 3072:3200], w[:, 7296:7312], z], axis=1)


def kernel(x, meta_tokens, norm_w, w_in, rw_shift_mu, rw_w0, rw_w2, rw_a0, rw_a2, rw_k_k, rw_k_a, rw_r_k, rw_gn_w, rw_gn_b, dn_conv_w, dn_A_log, dn_dt_bias, dn_norm_w, w_out, final_norm_w, loss_target, m_meta_tokens, m_norm_w, m_w_in, m_rw_shift_mu, m_rw_w0, m_rw_w2, m_rw_a0, m_rw_a2, m_rw_k_k, m_rw_k_a, m_rw_r_k, m_rw_gn_w, m_rw_gn_b, m_dn_conv_w, m_dn_A_log, m_dn_dt_bias, m_dn_norm_w, m_w_out, m_final_norm_w, v_meta_tokens, v_norm_w, v_w_in, v_rw_shift_mu, v_rw_w0, v_rw_w2, v_rw_a0, v_rw_a2, v_rw_k_k, v_rw_k_a, v_rw_r_k, v_rw_gn_w, v_rw_gn_b, v_dn_conv_w, v_dn_A_log, v_dn_dt_bias, v_dn_norm_w, v_w_out, v_final_norm_w):
    S = x.shape[1]
    L = N_META + S
    Lp = -(-L // CHUNK) * CHUNK

    small_shapes = [(RW_LORA, 256), (RW_LORA, 256), (CONV_W, 768), (N_META, 512)]
    small_mine = _pack([rw_w2[0], rw_a2[0], dn_conv_w[0], meta_tokens], 1024)
    small_all = _gather_chips("gather_small", small_mine)
    per_chip = [_unpack(small_all[s], small_shapes) for s in range(N_CHIPS)]
    w2, a2, conv_w, meta = [jnp.concatenate([per_chip[s][i] for s in range(N_CHIPS)], axis=1) for i in range(4)]
    w_in_started = _gather_shards("gather_w_in", w_in[0].astype(BF16), 8, overlap=True, dep=small_all)
    w_out_started = _gather_shards("gather_w_out", w_out[0].astype(BF16), 8, overlap=True, dep=w_in_started[4])

    tail = [jnp.zeros((Lp - L, D_MODEL), F32)] if Lp > L else []
    h = jnp.concatenate([meta + w_in_started[4][:1, :1], x[0]] + tail, axis=0)
    tgt = jnp.concatenate([jnp.zeros((N_META, D_MODEL), F32), loss_target[0]] + tail, axis=0)
    (u,) = _rowwise("rms_in", _rms_fn, [_row(h)], [norm_w], [D_MODEL], tm=2 * ROW_TILE, out_dtype=BF16)
    w_in_all = _gather_finish("gather_w_in", w_in_started, u, 8)
    W_orig = jnp.transpose(w_in_all, (0, 2, 1, 3)).reshape(D_MODEL, IN_COLS)
    W = _to_my_layout(W_orig)
    W_gapped_t = jnp.concatenate([W_orig[:, :GAP_AT], jnp.zeros((D_MODEL, GAP), BF16), W_orig[:, GAP_AT:]], axis=1).T
    p = _mm("in_proj", u, W, "nn", dep=w_out_started[4])

    mu = rw_shift_mu
    zpad = jnp.zeros((RW_LORA, RW_WIDTH), F32)
    rw_params = [mu[:, 0:1024], mu[:, 1024:2048], mu[:, 2048:3072], mu[:, 3072:3200], rw_w0,
                 jnp.concatenate([w2, zpad], axis=0), rw_a0, jnp.concatenate([zpad, a2], axis=0), rw_k_k, rw_k_a]
    rw_rows = [_row(p, 1024, CB_R), _row(p, 1024, CB_K), _row(p, 1024, CB_V), _row(p, 128, CB_S1),
               _row(p, 1024, CB_GATE)]
    rw_pars = rw_params + [rw_r_k, rw_gn_w, rw_gn_b]
    ya, rw_ck, rw_t = _chunk_fwd("rw_chunk_fwd", _rw_fused_fn, rw_rows, 4, rw_pars, GROUP)

    dn_rows = [_row(p, 1024, CB_DQ), _row(p, 1024, CB_DK), _row(p, 1024, CB_DV), _row(p, 128, CB_S2),
               _row(p, 1024, CB_Z)]
    dn_pars = [conv_w[j:j + 1, 1024 * i:1024 * (i + 1)] for i in range(3) for j in range(CONV_W)]
    narrow = lambda t: jnp.pad(t, ((0, 0), (DN_HEADS, DN_HEAD - 2 * DN_HEADS)))
    dn_pars += [narrow(dn_A_log), narrow(dn_dt_bias), jnp.tile(dn_norm_w, (1, DN_HEADS))]
    yb, dn_ck, dn_t = _chunk_fwd("dn_chunk_fwd", _dn_fused_fn, dn_rows, 3, dn_pars, CHUNK)

    y = jnp.concatenate([ya, yb], axis=1)
    w_out_all = _gather_finish("gather_w_out", w_out_started, y, 8)
    Wo = jnp.transpose(w_out_all, (1, 0, 2, 3)).reshape(D_MODEL, D_MODEL)
    yw = _mm("out_proj", y, Wo, "nn", tn=1024)
    loss_acc, dz, dz16, d_fw = _loss_head(h, yw, tgt, final_norm_w.reshape(1, D_MODEL), L)
    loss = lax.psum(loss_acc[0, 0], ("x", "y", "c"))

    d_wo = _mm("d_w_out", y.T, dz16, "nn", tm=512, tn=1024, tk=Lp)
    wo_started = _reduce_to_shard("rs_w_out", d_wo.reshape(N_CHIPS, D_MODEL // N_CHIPS, D_MODEL), 8, overlap=True)
    dy = _mm("d_out_proj", dz16, Wo.T, "nn", tn=1024, tk=2048, dep=wo_started[0][4])

    dp_rw, d_rw_pars = _chunk_bwd("rw_chunk_bwd", _rw_fused_fn, rw_rows, 4, rw_pars, rw_ck, rw_t, _row(dy, 1024, 0))
    d_prep_pars, d_rw_pars = d_rw_pars[:10], d_rw_pars[10:]
    dp_dn, d_dn_pars = _chunk_bwd("dn_chunk_bwd", _dn_fused_fn, dn_rows, 3, dn_pars, dn_ck, dn_t, _row(dy, 1024, 1))
    d_conv_parts = [jnp.concatenate(d_dn_pars[4 * i:4 * i + 4], axis=0) for i in range(3)]
    d_a_log_b, d_dt_b = d_dn_pars[12], d_dn_pars[13]
    d_dn_nw = jnp.sum(d_dn_pars[14].reshape(DN_HEADS, DN_HEAD), axis=0, keepdims=True)

    dp = jnp.concatenate(list(dp_rw[:3]) + [dp_rw[3], dp_rw[4]] + list(dp_dn), axis=1)
    d_W = _mm("d_w_in", u.T, dp, "nn", tm=1024, tn=768, tk=Lp)
    wi_started = _reduce_to_shard("rs_w_in", d_W, 8, by_columns=True, overlap=True)
    du = _mm("d_in_proj", dp, W_gapped_t, "nn", tn=2048, tk=1408, dep=wi_started[0][4])
    (dh,), (d_norm_w,) = _rowwise_bwd("rms_in_bwd", lambda h_, w_: (_rms_fn(h_, w_)[0], h_), [_row(h)], [norm_w],
                                      [_row(du), _row(dz)])
    grad_x = dh[N_META:L][None]

    d_mu = jnp.concatenate(d_prep_pars[0:4], axis=1)
    d_w2, d_a2 = d_prep_pars[5][:RW_LORA], d_prep_pars[7][RW_LORA:]
    d_conv = jnp.concatenate(d_conv_parts, axis=1)
    d_meta = dh[:N_META]
    head_sum = lambda t: t[:, DN_HEADS:2 * DN_HEADS]
    rep_names = ["norm_w", "rw_shift_mu", "rw_w0", "rw_a0", "rw_k_k", "rw_k_a", "rw_r_k", "rw_gn_w", "rw_gn_b",
                 "dn_A_log", "dn_dt_bias", "dn_norm_w", "final_norm_w"]
    rep_g = [d_norm_w, d_mu, d_prep_pars[4], d_prep_pars[6], d_prep_pars[8], d_prep_pars[9],
             d_rw_pars[0], d_rw_pars[1], d_rw_pars[2],
             head_sum(d_a_log_b), head_sum(d_dt_b), d_dn_nw, d_fw.reshape(D_MODEL)]
    rep_w = [norm_w, rw_shift_mu, rw_w0, rw_a0, rw_k_k, rw_k_a, rw_r_k, rw_gn_w, rw_gn_b, dn_A_log, dn_dt_bias,
             dn_norm_w, final_norm_w]
    rep_m = [m_norm_w, m_rw_shift_mu, m_rw_w0, m_rw_a0, m_rw_k_k, m_rw_k_a, m_rw_r_k, m_rw_gn_w, m_rw_gn_b,
             m_dn_A_log, m_dn_dt_bias, m_dn_norm_w, m_final_norm_w]
    rep_v = [v_norm_w, v_rw_shift_mu, v_rw_w0, v_rw_a0, v_rw_k_k, v_rw_k_a, v_rw_r_k, v_rw_gn_w, v_rw_gn_b,
             v_dn_A_log, v_dn_dt_bias, v_dn_norm_w, v_final_norm_w]
    rep_shapes = [t.shape for t in rep_w]
    rep_all = _exchange("gather_rep_grads", _pack(rep_g, 128), ALL_MASKS, "dev", False)
    rep_out = _adamw("adam_rep", rep_all, _pack(rep_w, 128), _pack(rep_m, 128), _pack(rep_v, 128))
    rep_out = [dict(zip(rep_names, _unpack(t, rep_shapes))) for t in rep_out]

    sm_slabs = jnp.stack([_pack([d_w2[:, 256 * s:256 * (s + 1)], d_a2[:, 256 * s:256 * (s + 1)],
                                 d_conv[:, 768 * s:768 * (s + 1)], d_meta[:, 512 * s:512 * (s + 1)]], 1024, 64)
                          for s in range(N_CHIPS)])
    sm_parts = _reduce_to_shard("rs_small", sm_slabs, 1)
    sm_w = [rw_w2[0], rw_a2[0], dn_conv_w[0], meta_tokens]
    sm_m = [m_rw_w2[0], m_rw_a2[0], m_dn_conv_w[0], m_meta_tokens]
    sm_v = [v_rw_w2[0], v_rw_a2[0], v_dn_conv_w[0], v_meta_tokens]
    sm_out = _adamw("adam_small", sm_parts, _pack(sm_w, 1024, 64), _pack(sm_m, 1024, 64), _pack(sm_v, 1024, 64))
    sm_names = ["rw_w2", "rw_a2", "dn_conv_w", "meta_tokens"]
    sm_full_shapes = [(1, RW_LORA, 256), (1, RW_LORA, 256), (1, CONV_W, 768), (N_META, 512)]
    sm_out = [dict(zip(sm_names, [t.reshape(shp) for t, shp in zip(_unpack(o, small_shapes), sm_full_shapes)]))
              for o in sm_out]

    wo_parts = _reduce_finish("rs_w_out", wo_started, dp, 8)
    wo_out = _adamw("adam_w_out", wo_parts, w_out[0], m_w_out[0], v_w_out[0])
    wi_parts = _reduce_finish("rs_w_in", wi_started, dh, 8)
    view = lambda t: jnp.transpose(t[0]).reshape(SHARD_COLS, D_MODEL // GROUP, GROUP)
    unview = lambda t: jnp.transpose(t.reshape(SHARD_COLS, D_MODEL))[None]
    wi_rest = _adamw_tiles("adam_w_in", view(wi_parts), view(w_in), view(m_w_in), view(v_w_in),
                           (SHARD_COLS // 4, 8, GROUP))
    wi_out = [wi_parts] + [unview(t) for t in wi_rest]

    order = ["meta_tokens", "norm_w", "w_in", "rw_shift_mu", "rw_w0", "rw_w2", "rw_a0", "rw_a2", "rw_k_k", "rw_k_a",
             "rw_r_k", "rw_gn_w", "rw_gn_b", "dn_conv_w", "dn_A_log", "dn_dt_bias", "dn_norm_w", "w_out",
             "final_norm_w"]
    outs = [loss, grad_x]
    for kind in range(4):
        table = dict(rep_out[kind])
        table.update(sm_out[kind])
        table["w_in"] = wi_out[kind]
        table["w_out"] = wo_out[kind][None]
        outs += [table[n] for n in order]
    return tuple(outs)
```

```python
import functools

import jax
import jax.numpy as jnp
from jax import lax
from jax.experimental import pallas as pl
from jax.experimental.pallas import tpu as pltpu

F32 = jnp.float32
BF16 = jnp.bfloat16
HIGH = lax.Precision.HIGH
HIGHEST = lax.Precision.HIGHEST
MESH = pl.DeviceIdType.MESH

D_MODEL = 2048
N_META = 16
RW_WIDTH = 1024
RW_HEAD = 64
RW_HEADS = 16
RW_LORA = 64
RW_GN_EPS = 64e-5
DN_WIDTH = 1024
DN_HEAD = 128
DN_HEADS = 8
CONV_W = 4
CHUNK = 64
NORM_EPS = 1e-6
IN_COLS = 8336
N_CHIPS = 4
SHARD_COLS = IN_COLS // N_CHIPS

NP_COLS = 8 * 1024 + 256
GAP_AT = IN_COLS - DN_WIDTH
GAP = NP_COLS - IN_COLS
CB_R, CB_K, CB_V, CB_GATE, CB_DQ, CB_DK, CB_DV, CB_Z = range(8)
CB_S1 = 8192 // 128
CB_S2 = CB_S1 + 1

ADAM_LR = 0.001
ADAM_B1 = 0.9
ADAM_B2 = 0.999
ADAM_EPS = 1e-08
ADAM_WD = 0.01
ADAM_STEP = 10

VMEM_LIMIT_BYTES = 56 * 1024 * 1024
ROW_TILE = 104
MM_ROW_TILE = 832


def _params(sem=None):
    return pltpu.CompilerParams(dimension_semantics=sem, vmem_limit_bytes=VMEM_LIMIT_BYTES)


def _pick(n, target, mult=8):
    best = None
    for d in range(mult, min(n, target) + 1, mult):
        if n % d == 0:
            best = d
    return n if best is None else best


def _mm(name, a, b, mode, tm=MM_ROW_TILE, tn=1408, tk=2048, dep=None):
    if mode == "nn":
        (M, K), (_, N) = a.shape, b.shape
    elif mode == "nt":
        (M, K), (N, _) = a.shape, b.shape
    else:
        (K, M), (_, N) = a.shape, b.shape
    tm = _pick(M, tm, 128 if mode == "tn" else 16)
    tn = _pick(N, tn, 128)
    tk = _pick(K, tk, 8 if mode == "tn" else 128)
    if mode == "nn":
        a_spec = pl.BlockSpec((tm, tk), lambda i, j, k: (i, k))
        b_spec = pl.BlockSpec((tk, tn), lambda i, j, k: (k, j))
        dims = (((1,), (0,)), ((), ()))
    elif mode == "nt":
        a_spec = pl.BlockSpec((tm, tk), lambda i, j, k: (i, k))
        b_spec = pl.BlockSpec((tn, tk), lambda i, j, k: (j, k))
        dims = (((1,), (1,)), ((), ()))
    else:
        a_spec = pl.BlockSpec((tk, tm), lambda i, j, k: (k, i))
        b_spec = pl.BlockSpec((tk, tn), lambda i, j, k: (k, j))
        dims = (((0,), (0,)), ((), ()))

    def body(a_ref, b_ref, *rest):
        o_ref = rest[-1]

        @pl.when(pl.program_id(2) == 0)
        def _():
            o_ref[...] = jnp.zeros_like(o_ref)

        o_ref[...] += lax.dot_general(a_ref[...].astype(BF16), b_ref[...].astype(BF16), dims,
                                      preferred_element_type=F32)

    deps = [] if dep is None else [dep]
    return pl.pallas_call(
        body, name=name, grid=(M // tm, N // tn, K // tk),
        in_specs=[a_spec, b_spec] + [pl.BlockSpec((8, 128), lambda i, j, k: (0, 0))] * len(deps),
        out_specs=pl.BlockSpec((tm, tn), lambda i, j, k: (i, j)),
        out_shape=jax.ShapeDtypeStruct((M, N), F32),
        compiler_params=_params(("parallel", "parallel", "arbitrary")),
    )(a, b, *deps)


def _row(arr, width=None, cb=0):
    return (arr, arr.shape[1] if width is None else width, cb)


def _rowwise(name, fn, rows, params, out_widths, tm=ROW_TILE, out_dtype=F32):
    R = rows[0][0].shape[0]
    tm = _pick(R, tm, 16 if out_dtype == BF16 else 8)
    n_r, n_p = len(rows), len(params)

    def body(*refs):
        vals = [r[...] for r in refs[:n_r + n_p]]
        for o_ref, val in zip(refs[n_r + n_p:], fn(*vals)):
            o_ref[...] = val.astype(out_dtype)

    in_specs = [pl.BlockSpec((tm, w), lambda i, cb=cb: (i, cb)) for (_, w, cb) in rows]
    in_specs += [pl.BlockSpec(p.shape, lambda i: (0, 0)) for p in params]
    return pl.pallas_call(
        body, name=name, grid=(R // tm,), in_specs=in_specs,
        out_specs=[pl.BlockSpec((tm, w), lambda i: (i, 0)) for w in out_widths],
        out_shape=[jax.ShapeDtypeStruct((R, w), out_dtype) for w in out_widths],
        compiler_params=_params(("parallel",)),
    )(*[r[0] for r in rows], *params)


def _rowwise_bwd(name, fn, rows, params, douts, tm=ROW_TILE):
    R = rows[0][0].shape[0]
    tm = _pick(R, tm)
    n_r, n_p, n_d = len(rows), len(params), len(douts)

    def body(*refs):
        vals = [r[...] for r in refs[:n_r + n_p]]
        cts = tuple(r[...] for r in refs[n_r + n_p:n_r + n_p + n_d])
        grads = jax.vjp(fn, *vals)[1](cts)
        outs = refs[n_r + n_p + n_d:]
        for o_ref, g in zip(outs[:n_r], grads[:n_r]):
            o_ref[...] = g

        @pl.when(pl.program_id(0) == 0)
        def _():
            for o_ref in outs[n_r:]:
                o_ref[...] = jnp.zeros_like(o_ref)

        for o_ref, g in zip(outs[n_r:], grads[n_r:]):
            o_ref[...] += g

    in_specs = [pl.BlockSpec((tm, w), lambda i, cb=cb: (i, cb)) for (_, w, cb) in rows]
    in_specs += [pl.BlockSpec(p.shape, lambda i: (0, 0)) for p in params]
    in_specs += [pl.BlockSpec((tm, w), lambda i, cb=cb: (i, cb)) for (_, w, cb) in douts]
    out_specs = [pl.BlockSpec((tm, w), lambda i: (i, 0)) for (_, w, _) in rows]
    out_specs += [pl.BlockSpec(p.shape, lambda i: (0, 0)) for p in params]
    out_shape = [jax.ShapeDtypeStruct((R, w), F32) for (_, w, _) in rows]
    out_shape += [jax.ShapeDtypeStruct(p.shape, F32) for p in params]
    res = pl.pallas_call(
        body, name=name, grid=(R // tm,), in_specs=in_specs, out_specs=out_specs, out_shape=out_shape,
        compiler_params=_params(("arbitrary",)),
    )(*[r[0] for r in rows], *params, *[d[0] for d in douts])
    return res[:n_r], res[n_r:]


def _softplus(x):
    return jnp.maximum(x, 0.0) + jnp.log(1.0 + jnp.exp(-jnp.abs(x)))


def _silu(x):
    return x * jax.nn.sigmoid(x)


def _rms_fn(h, w):
    return (h * lax.rsqrt(jnp.mean(h * h, axis=-1, keepdims=True) + NORM_EPS) * w,)


def _rw_prep_fn(pr, pr1, pk, pk1, pv, pv1, ps, ps1, mu_r, mu_k, mu_v, mu_s, w0, w2p, a0, a2p, k_k, k_a):
    r = pr + (pr1 - pr) * mu_r
    k = pk + (pk1 - pk) * mu_k
    v = pv + (pv1 - pv) * mu_v
    s = ps + (ps1 - ps) * mu_s
    lora = lambda x, w: jnp.dot(x.astype(BF16), w.astype(BF16), preferred_element_type=F32)
    w_log = -_softplus(-(w0 + lora(jnp.tanh(s), w2p))) - 0.5
    log_decay = -jnp.exp(w_log)
    a = jax.nn.sigmoid(a0 + lora(s, a2p))
    return r, log_decay, k * (1.0 + (a - 1.0) * k_a), v, k * k_k, a


def _conv_fn(u0, u1, u2, u3, w0, w1, w2, w3):
    return (_silu(u0 * w3 + u1 * w2 + u2 * w1 + u3 * w0),)


def _dn_gate_fn(ps2, a_log_n, dt_n):
    beta_n = jax.nn.sigmoid(ps2)
    g_n = -jnp.exp(a_log_n) * _softplus(ps2 + dt_n)
    lane = lax.broadcasted_iota(jnp.int32, (1, DN_HEAD), 1)
    pick = lambda x, j: jnp.broadcast_to(jnp.sum(x * (lane == j).astype(F32), axis=-1, keepdims=True), x.shape)[None]
    beta = jnp.concatenate([pick(beta_n, h) for h in range(DN_HEADS)], axis=0)
    g = jnp.concatenate([pick(g_n, DN_HEADS + h) for h in range(DN_HEADS)], axis=0)
    return beta, g


def _tril_masks(c):
    t = lax.broadcasted_iota(jnp.int32, (c, c), 0)
    s = lax.broadcasted_iota(jnp.int32, (c, c), 1)
    return s <= t, s < t


def _dot(x, y, cx, cy, prec=HIGH):
    nb = x.ndim - 2
    batch = tuple(range(nb))
    return lax.dot_general(x, y, (((cx + nb,), (cy + nb,)), (batch, batch)), precision=prec,
                           preferred_element_type=F32)


def _tri_inv(a):
    c = a.shape[-1]
    eye = (lax.broadcasted_iota(jnp.int32, (c, c), 0) == lax.broadcasted_iota(jnp.int32, (c, c), 1)).astype(F32)
    x = eye - a
    p = a
    n = 2
    while n < c:
        p = _dot(p, p, 1, 0)
        x = x + _dot(x, p, 1, 0)
        n *= 2
    return x


@jax.custom_vjp
def _tri_inv_saved(a, t):
    return t


def _tri_inv_saved_fwd(a, t):
    return t, t


def _tri_inv_saved_bwd(t, dt):
    return -_dot(_dot(t, dt, 0, 0), t, 1, 1), jnp.zeros_like(t)


_tri_inv_saved.defvjp(_tri_inv_saved_fwd, _tri_inv_saved_bwd)


def _pair_masks():
    lane = lax.broadcasted_iota(jnp.int32, (1, 2 * RW_HEAD), 1)
    m0 = (lane < RW_HEAD).astype(F32)
    return m0, 1.0 - m0


def _pair_bd(x):
    m0, m1 = _pair_masks()
    return jnp.concatenate([x * m0, x * m1], axis=-2)


def _pair_mm(x, y):
    return _dot(x, _pair_bd(y), 1, 0)


def _pair_inv(a):
    c = a.shape[-2]
    row = lax.broadcasted_iota(jnp.int32, (c, 2 * RW_HEAD), 0)
    col = lax.broadcasted_iota(jnp.int32, (c, 2 * RW_HEAD), 1) & (RW_HEAD - 1)
    x = (row == col).astype(F32) - a
    p = a
    n = 2
    while n < c:
        p = _pair_mm(p, p)
        x = x + _pair_mm(x, p)
        n *= 2
    return x


@jax.custom_vjp
def _pair_inv_saved(a, t):
    return t


def _pair_inv_saved_fwd(a, t):
    return t, t


def _pair_inv_saved_bwd(t, dt):
    m0, m1 = _pair_masks()
    c = t.shape[-2]
    z = _dot(t, dt, 0, 0)
    x = z[:, :c, :] * m0 + z[:, c:, :] * m1
    return -_dot(x, _pair_bd(t), 1, 1), jnp.zeros_like(t)


_pair_inv_saved.defvjp(_pair_inv_saved_fwd, _pair_inv_saved_bwd)


@jax.custom_vjp
def _use_saved(x, x_saved):
    return x_saved


_use_saved.defvjp(lambda x, x_saved: (x_saved, None), lambda _, g: (g, jnp.zeros_like(g)))


def _rw_chunk_fn(S, r, lw, k2, v, kkp, a, gate, rk, gnw, gnb, saved=None):
    B, C, P = r.shape
    m0, m1 = _pair_masks()
    seg = lambda x: (jnp.sum(x * m0, axis=-1, keepdims=True) * m0 + jnp.sum(x * m1, axis=-1, keepdims=True) * m1)
    mm = lambda x, y: _dot(x.astype(BF16), y.astype(BF16), 1, 0, None)
    nt = lambda x, y: _dot(x, y, 1, 1)
    tn = lambda x, y: _dot(x, y, 0, 0)
    pair_mm = lambda x, y: mm(x, _pair_bd(y))
    t_idx = lax.broadcasted_iota(jnp.int32, (C, P), 0)
    s_idx = lax.broadcasted_iota(jnp.int32, (C, P), 1) & (RW_HEAD - 1)
    incl, strict = s_idx <= t_idx, s_idx < t_idx
    tril = jnp.broadcast_to(_tril_masks(C)[0].astype(F32), (B, C, C))
    kk = kkp * lax.rsqrt(seg(kkp * kkp) + 1e-6)
    b = kk * a
    reuse = (lambda x, i: x) if saved is None else (lambda x, i: _use_saved(x, saved[i]))
    g_incl = reuse(_dot(tril, lw, 1, 0), 1)
    g_excl = g_incl - lw
    inv = jnp.exp(-g_incl)
    alpha, beta, kappa, rho = kk * jnp.exp(g_excl), b * inv, k2 * inv, r * jnp.exp(g_incl)
    ar = jnp.concatenate([alpha, rho], axis=-2)
    scores = reuse(nt(ar, jnp.concatenate([_pair_bd(beta), _pair_bd(kappa)], axis=-2)), 2)
    a_ab = jnp.where(strict, scores[:, :C, :P], 0.0)
    a_ak = jnp.where(strict, scores[:, :C, P:], 0.0)
    r_b = jnp.where(incl, scores[:, C:, :P], 0.0)
    r_k = jnp.where(incl, scores[:, C:, P:], 0.0)
    t_inv = _pair_inv(a_ab) if saved is None else _pair_inv_saved(a_ab, saved[0])
    on_state = reuse(nt(ar, S), 3)
    u = pair_mm(t_inv, -on_state[:, :C, :] - pair_mm(a_ak, v))
    y = on_state[:, C:, :] + mm(jnp.concatenate([r_b, r_k], axis=-1),
                                jnp.concatenate([_pair_bd(u), _pair_bd(v)], axis=-2))
    same_head = ((lax.broadcasted_iota(jnp.int32, (P, P), 0) < RW_HEAD)
                 == (lax.broadcasted_iota(jnp.int32, (P, P), 1) < RW_HEAD))
    fresh = tn(jnp.concatenate([u, v], axis=-2), jnp.concatenate([beta, kappa], axis=-2))
    S_new = jnp.exp(jnp.sum(lw, axis=-2, keepdims=True)) * (S + jnp.where(same_head, fresh, 0.0))
    dev = y - seg(y) * (1.0 / RW_HEAD)
    yn = dev * lax.rsqrt(seg(dev * dev) * (1.0 / RW_HEAD) + RW_GN_EPS) * gnw + gnb
    bonus = seg(r * k2 * rk) * v
    return (yn + bonus) * _silu(gate), S_new, (t_inv, g_incl, scores, on_state)


def _dn_chunk_fn(S, qc, kc, vc, bb, gb, z, nw, saved=None):
    B, C, D = qc.shape
    mm = lambda x, y: _dot(x.astype(BF16), y.astype(BF16), 1, 0, None)
    nt = lambda x, y, p=None: _dot(x, y, 1, 1, p) if p else _dot(x.astype(BF16), y.astype(BF16), 1, 1, None)
    tn = lambda x, y: _dot(x.astype(BF16), y.astype(BF16), 0, 0, None)
    incl, strict = _tril_masks(C)
    q = qc * lax.rsqrt(jnp.sum(qc * qc, axis=-1, keepdims=True) + 1e-6) * (D ** -0.5)
    k = kc * lax.rsqrt(jnp.sum(kc * kc, axis=-1, keepdims=True) + 1e-6)
    kb, vb = k * bb, vc * bb
    G = _dot(jnp.broadcast_to(incl.astype(F32), (B, C, C)), gb, 1, 0, HIGHEST)
    lane = lax.broadcasted_iota(jnp.int32, (C, D), 1)
    e0, e1 = (lane == 0).astype(F32), (lane == 1).astype(F32)
    diff = nt(G * e0 + e1, e0 - G * e1, HIGHEST)
    dmask = jnp.where(incl, jnp.exp(jnp.where(incl, diff, 0.0)), 0.0)
    M = jnp.where(strict, nt(kb, k) * dmask, 0.0)
    T = _tri_inv(M) if saved is None else _tri_inv_saved(M, saved[0])
    eG = jnp.exp(G)
    u = mm(T, vb)
    w = mm(T, kb * eG)
    attn = jnp.where(incl, nt(q, k) * dmask, 0.0)
    v_new = u - mm(w, S)
    o = mm(q * eG, S) + mm(attn, v_new)
    g_last = jnp.sum(gb, axis=-2, keepdims=True)
    S_new = S * jnp.exp(jnp.broadcast_to(g_last, S.shape)) + tn(k * jnp.exp(g_last - G), v_new)
    on = o * lax.rsqrt(jnp.mean(o * o, axis=-1, keepdims=True) + NORM_EPS) * nw
    return on * _silu(z), S_new, (T,)


N_GROUPS = 8
GROUP = 128
HALO = 8


def _groups(x):
    return jnp.concatenate([x[:, g * GROUP:(g + 1) * GROUP][None] for g in range(N_GROUPS)], axis=0)


@functools.partial(jax.custom_vjp, nondiff_argnums=(1,))
def _shift_rows(ext, j):
    return pltpu.roll(ext, j, 0)[HALO:, :]


def _shift_rows_fwd(ext, j):
    return _shift_rows(ext, j), None


def _shift_rows_bwd(j, _, d):
    z = jnp.concatenate([jnp.zeros((HALO, d.shape[1]), d.dtype), d], axis=0)
    return (pltpu.roll(z, z.shape[0] - j, 0),)


_shift_rows.defvjp(_shift_rows_fwd, _shift_rows_bwd)


def _rw_fused_fn(S, h_r, h_k, h_v, h_s, p_r, p_k, p_v, p_s, gate, *pars, saved=None):
    prev = lambda h, x: _shift_rows(jnp.concatenate([h, x], axis=0), 1)
    seq = _rw_prep_fn(p_r, prev(h_r, p_r), p_k, prev(h_k, p_k), p_v, prev(h_v, p_v), p_s, prev(h_s, p_s), *pars[:10])
    return _rw_chunk_fn(S, *[_groups(t) for t in seq], _groups(gate), *[_groups(t) for t in pars[10:]],
                        saved=saved)


def _dn_fused_fn(S, h_q, h_k, h_v, p_q, p_k, p_v, p_s2, z, *pars, saved=None):
    conv = []
    for i, (h, x) in enumerate(((h_q, p_q), (h_k, p_k), (h_v, p_v))):
        ext = jnp.concatenate([h, x], axis=0)
        conv += _conv_fn(x, _shift_rows(ext, 1), _shift_rows(ext, 2), _shift_rows(ext, 3), *pars[4 * i:4 * i + 4])
    beta, g = _dn_gate_fn(p_s2, pars[12], pars[13])
    return _dn_chunk_fn(S, *[_groups(t) for t in conv], beta, g, _groups(z), _groups(pars[14]), saved=saved)


def _chunk_fwd(name, fn, rows, n_halo, pars, saved_shapes):
    Lp = rows[0][0].shape[0]
    C, D, NB = CHUNK, GROUP, N_GROUPS
    nc, n_r, n_p, per = Lp // C, len(rows), len(pars), CHUNK // HALO
    n_s = len(saved_shapes)

    def body(*refs):
        row_refs, halo_refs, par_refs = refs[:n_r], refs[n_r:n_r + n_halo], refs[n_r + n_halo:n_r + n_halo + n_p]
        y_ref, ck_ref = refs[n_r + n_halo + n_p:n_r + n_halo + n_p + 2]
        saved_refs, s_ref = refs[n_r + n_halo + n_p + 2:-1], refs[-1]
        first = pl.program_id(0) == 0

        @pl.when(first)
        def _():
            s_ref[...] = jnp.zeros_like(s_ref)

        S = s_ref[...]
        ck_ref[0] = S
        halos = [jnp.where(first, 0.0, r[...]) for r in halo_refs]
        y, S_new, saved = fn(S, *halos, *[r[...] for r in row_refs], *[r[...] for r in par_refs])
        for g in range(NB):
            y_ref[:, g * D:(g + 1) * D] = y[g].astype(BF16)
        for o_ref, val in zip(saved_refs, saved):
            o_ref[0] = val
        s_ref[...] = S_new

    return pl.pallas_call(
        body, name=name, grid=(nc,),
        in_specs=[pl.BlockSpec((C, w), lambda c, cb=cb: (c, cb)) for (_, w, cb) in rows]
        + [pl.BlockSpec((HALO, w), lambda c, cb=cb: (jnp.maximum(c * per - 1, 0), cb)) for (_, w, cb) in rows[:n_halo]]
        + [pl.BlockSpec(p.shape, lambda c: (0, 0)) for p in pars],
        out_specs=[pl.BlockSpec((C, NB * D), lambda c: (c, 0)), pl.BlockSpec((1, NB, D, D), lambda c: (c, 0, 0, 0))]
        + [pl.BlockSpec((1,) + tuple(shp), lambda c: (c, 0, 0, 0)) for shp in saved_shapes],
        out_shape=[jax.ShapeDtypeStruct((Lp, NB * D), BF16), jax.ShapeDtypeStruct((nc, NB, D, D), F32)]
        + [jax.ShapeDtypeStruct((nc,) + tuple(shp), F32) for shp in saved_shapes],
        scratch_shapes=[pltpu.VMEM((NB, D, D), F32)],
        compiler_params=_params(("arbitrary",)),
    )(*[r[0] for r in rows], *[r[0] for r in rows[:n_halo]], *pars)


def _chunk_bwd(name, fn, rows, n_halo, pars, ckpt, saved, dy, out_blocks, into=None):
    Lp = rows[0][0].shape[0]
    C, D, NB = CHUNK, GROUP, N_GROUPS
    nc, n_r, n_p, per = Lp // C, len(rows), len(pars), CHUNK // HALO
    n_s = len(saved)
    n_in = n_r + n_halo + n_p
    width = sum(w for (_, w, _) in rows)
    n_blocks, my_block = out_blocks
    extra = [] if into is None else [into]

    def body(*refs):
        row_refs, halo_refs, par_refs = refs[:n_r], refs[n_r:n_r + n_halo], refs[n_r + n_halo:n_in]
        ck_ref, saved_refs, dy_ref = refs[n_in], refs[n_in + 1:n_in + 1 + n_s], refs[n_in + 1 + n_s]
        outs = refs[n_in + 2 + n_s + len(extra):]
        drows_ref, dpar_refs = outs[0], outs[1:1 + n_p]
        ds_ref, carry_refs = outs[1 + n_p], outs[2 + n_p:]
        i = pl.program_id(0)

        @pl.when(i == 0)
        def _():
            ds_ref[...] = jnp.zeros_like(ds_ref)
            for o_ref in list(dpar_refs) + list(carry_refs):
                o_ref[...] = jnp.zeros_like(o_ref)

        halos = [jnp.where(i == nc - 1, 0.0, r[...]) for r in halo_refs]
        vals = [ck_ref[0]] + halos + [r[...] for r in row_refs] + [r[...] for r in par_refs]
        stored = tuple(r[0] for r in saved_refs)
        grads = jax.vjp(lambda *v: fn(*v, saved=stored)[:2], *vals)[1]((_groups(dy_ref), ds_ref[...]))
        ds_ref[...] = grads[0]
        d_halos, d_rows, d_pars = grads[1:1 + n_halo], grads[1 + n_halo:1 + n_halo + n_r], grads[1 + n_halo + n_r:]
        col = 0
        for k, g in enumerate(d_rows):
            if k < n_halo:
                g = g + jnp.concatenate([jnp.zeros((C - HALO, g.shape[1]), F32), carry_refs[k][...]], axis=0)
                carry_refs[k][...] = d_halos[k]
            drows_ref[:, col:col + g.shape[1]] = g.astype(BF16)
            col += g.shape[1]
        for o_ref, g in zip(dpar_refs, d_pars):
            o_ref[...] += g

    rev = lambda c: nc - 1 - c
    dy_arr, dy_w, dy_cb = dy
    res = pl.pallas_call(
        body, name=name, grid=(nc,),
        in_specs=[pl.BlockSpec((C, w), lambda c, cb=cb: (rev(c), cb)) for (_, w, cb) in rows]
        + [pl.BlockSpec((HALO, w), lambda c, cb=cb: (jnp.maximum(rev(c) * per - 1, 0), cb))
           for (_, w, cb) in rows[:n_halo]]
        + [pl.BlockSpec(p.shape, lambda c: (0, 0)) for p in pars]
        + [pl.BlockSpec((1, NB, D, D), lambda c: (rev(c), 0, 0, 0))]
        + [pl.BlockSpec((1,) + tuple(t.shape[1:]), lambda c: (rev(c), 0, 0, 0)) for t in saved]
        + [pl.BlockSpec((C, dy_w), lambda c: (rev(c), dy_cb))]
        + [pl.BlockSpec(memory_space=pl.ANY)] * len(extra),
        out_specs=[pl.BlockSpec((C, width), lambda c: (rev(c), my_block))]
        + [pl.BlockSpec(p.shape, lambda c: (0, 0)) for p in pars],
        out_shape=[jax.ShapeDtypeStruct((Lp, n_blocks * width), BF16)]
        + [jax.ShapeDtypeStruct(p.shape, F32) for p in pars],
        scratch_shapes=[pltpu.VMEM((NB, D, D), F32)] + [pltpu.VMEM((HALO, w), F32) for (_, w, _) in rows[:n_halo]],
        input_output_aliases={} if into is None else {n_in + 2 + n_s: 0},
        compiler_params=_params(("arbitrary",)),
    )(*[r[0] for r in rows], *[r[0] for r in rows[:n_halo]], *pars, ckpt, *saved, dy_arr, *extra)
    return res[0], res[1:]


def _loss_head(h, yw, tgt, fw, n_real, tm=2 * ROW_TILE):
    Lp, Dm = h.shape
    tm = _pick(Lp, tm, 16)

    def out_fn(z, fw_):
        return z * lax.rsqrt(jnp.mean(z * z, axis=-1, keepdims=True) + NORM_EPS) * fw_

    def body(h_ref, yw_ref, t_ref, fw_ref, loss_ref, dz_ref, dz16_ref, dfw_ref):
        i = pl.program_id(0)

        @pl.when(i == 0)
        def _():
            loss_ref[...] = jnp.zeros_like(loss_ref)
            dfw_ref[...] = jnp.zeros_like(dfw_ref)

        row = i * tm + lax.broadcasted_iota(jnp.int32, (tm, 1), 0)
        mask = ((row >= N_META) & (row < n_real)).astype(F32)
        z = h_ref[...] + yw_ref[...]
        o, vjp = jax.vjp(out_fn, z, fw_ref[...])
        err = (o - t_ref[...]) * mask
        row_loss = 0.5 * jnp.mean(jnp.square(err), axis=-1, keepdims=True)
        dz, dfw = vjp(err * (1.0 / Dm))
        loss_ref[...] += jnp.sum(row_loss, axis=0, keepdims=True)
        dz_ref[...] = dz
        dz16_ref[...] = dz.astype(BF16)
        dfw_ref[...] += dfw

    row_spec = pl.BlockSpec((tm, Dm), lambda i: (i, 0))
    return pl.pallas_call(
        body, name="loss_head", grid=(Lp // tm,),
        in_specs=[row_spec, row_spec, row_spec, pl.BlockSpec((1, Dm), lambda i: (0, 0))],
        out_specs=[pl.BlockSpec((8, 128), lambda i: (0, 0)), row_spec, row_spec,
                   pl.BlockSpec((1, Dm), lambda i: (0, 0))],
        out_shape=[jax.ShapeDtypeStruct((8, 128), F32), jax.ShapeDtypeStruct((Lp, Dm), F32),
                   jax.ShapeDtypeStruct((Lp, Dm), BF16), jax.ShapeDtypeStruct((1, Dm), F32)],
        compiler_params=_params(("arbitrary",)),
    )(h, yw, tgt, fw)


def _exchange(name, x, masks, slot_kind, per_dest, n_split=1, copy_own=True, other_half=False):
    n = len(masks)
    keep_own = slot_kind is not None and copy_own
    n_slots = {"chip": 4, "core": 2, "dev": 8, None: n}[slot_kind]
    blk_shape = x.shape[1:] if per_dest else x.shape
    if other_half:
        blk_shape = (x.shape[0] // 2,) + tuple(x.shape[1:])
    rows = blk_shape[0] // n_split

    def body(x_ref, o_ref, send_sems, recv_sems, local_sems):
        mx, my, mc = lax.axis_index("x"), lax.axis_index("y"), lax.axis_index("c")
        if other_half:
            x_ref = x_ref.at[pl.ds((1 - mc) * blk_shape[0], blk_shape[0])]

        def slot(k, px, py, pc):
            return {"chip": 2 * px + py, "core": pc, "dev": 4 * px + 2 * py + pc, None: k}[slot_kind]

        def peer(m):
            return (mx + m[0]) % 2, (my + m[1]) % 2, (mc + m[2]) % 2

        def part(ref, j):
            return ref.at[pl.ds(j * rows, rows)]

        own_src = x_ref.at[2 * mx + my] if per_dest else x_ref
        local = []
        if keep_own:
            own_dst = o_ref.at[slot(0, mx, my, mc)]
            local = [pltpu.make_async_copy(part(own_src, j), part(own_dst, j), local_sems.at[j])
                     for j in range(n_split)]
        for cp in local:
            cp.start()
        sends = []
        for k, m in enumerate(masks):
            px, py, pc = peer(m)
            src = x_ref.at[2 * px + py] if per_dest else x_ref
            dst = o_ref.at[slot(k, mx, my, mc)]
            for j in range(n_split):
                sends.append(pltpu.make_async_remote_copy(
                    src_ref=part(src, j), dst_ref=part(dst, j), send_sem=send_sems.at[k * n_split + j],
                    recv_sem=recv_sems.at[k * n_split + j], device_id=(px, py, pc), device_id_type=MESH))
        for cp in sends:
            cp.start()
        for k, m in enumerate(masks):
            px, py, pc = peer(m)
            landed = o_ref.at[slot(k, px, py, pc)]
            for j in range(n_split):
                pltpu.make_async_remote_copy(
                    src_ref=part(own_src, j), dst_ref=part(landed, j), send_sem=send_sems.at[k * n_split + j],
                    recv_sem=recv_sems.at[k * n_split + j], device_id=(px, py, pc), device_id_type=MESH).wait_recv()
        for cp in sends:
            cp.wait_send()
        for cp in local:
            cp.wait()

    return pl.pallas_call(
        body, name=name,
        in_specs=[pl.BlockSpec(memory_space=pl.ANY)], out_specs=pl.BlockSpec(memory_space=pl.ANY),
        out_shape=jax.ShapeDtypeStruct((n_slots,) + tuple(blk_shape), x.dtype),
        scratch_shapes=[pltpu.SemaphoreType.DMA((n * n_split,)), pltpu.SemaphoreType.DMA((n * n_split,)),
                        pltpu.SemaphoreType.DMA((n_split,))],
        compiler_params=pltpu.CompilerParams(has_side_effects=True),
    )(x)


CHIP_MASKS = [(1, 0, 0), (0, 1, 0), (1, 1, 0)]
CORE_MASKS = [(0, 0, 1)]
ALL_MASKS = [(dx, dy, dc) for dx in (0, 1) for dy in (0, 1) for dc in (0, 1) if (dx, dy, dc) != (0, 0, 0)]

HBM_SPEC = pl.BlockSpec(memory_space=pltpu.HBM)
SEM_SPEC = pl.BlockSpec(memory_space=pltpu.SEMAPHORE)
DATAFLOW = pltpu.SideEffectType.DATAFLOW_SIDE_EFFECTING


def _split_copies(x_ref, land_ref, send_sems, recv_sems, masks, slot_kind, per_dest, n_split, with_recvs=True):
    mx, my, mc = lax.axis_index("x"), lax.axis_index("y"), lax.axis_index("c")
    slot = lambda px, py, pc: {"chip": 2 * px + py, "core": pc}[slot_kind]
    rows = (x_ref.shape[1] if per_dest else x_ref.shape[0]) // n_split
    part = lambda ref, j: ref.at[pl.ds(j * rows, rows)]
    sends, recvs = [], []
    for k, m in enumerate(masks):
        px, py, pc = (mx + m[0]) % 2, (my + m[1]) % 2, (mc + m[2]) % 2
        src = x_ref.at[2 * px + py] if per_dest else x_ref
        own_src = x_ref.at[2 * mx + my] if per_dest else x_ref
        for j in range(n_split):
            sems = dict(send_sem=send_sems.at[k * n_split + j], recv_sem=recv_sems.at[k * n_split + j],
                        device_id=(px, py, pc), device_id_type=MESH)
            sends.append(pltpu.make_async_remote_copy(
                src_ref=part(src, j), dst_ref=part(land_ref.at[slot(mx, my, mc)], j), **sems))
            if with_recvs:
                recvs.append(pltpu.make_async_remote_copy(
                    src_ref=part(own_src, j), dst_ref=part(land_ref.at[slot(px, py, pc)], j), **sems))
    return sends, recvs


def _exchange_start(name, x, masks, slot_kind, per_dest, n_split, dep=None):
    n = len(masks) * n_split
    blk_shape = x.shape[1:] if per_dest else x.shape
    land_shape = ({"chip": 4, "core": 2}[slot_kind],) + tuple(blk_shape)
    deps = [] if dep is None else [dep]

    def body(x_ref, land_ref, *rest):
        send_sems, recv_sems, x_thru, land_thru, token = rest[len(deps):]
        for cp in _split_copies(x_ref, land_ref, send_sems, recv_sems, masks, slot_kind, per_dest, n_split, False)[0]:
            cp.start()
        token[...] = jnp.zeros_like(token)

    return pl.pallas_call(
        body, name=name,
        out_shape=(pltpu.SemaphoreType.DMA((n,)), pltpu.SemaphoreType.DMA((n,)), pltpu.HBM(x.shape, x.dtype),
                   pltpu.HBM(land_shape, x.dtype), jax.ShapeDtypeStruct((8, 128), F32)),
        in_specs=(HBM_SPEC, HBM_SPEC) + (pl.BlockSpec(memory_space=pl.ANY),) * len(deps),
        out_specs=(SEM_SPEC, SEM_SPEC, HBM_SPEC, HBM_SPEC, pl.BlockSpec(memory_space=pltpu.VMEM)),
        input_output_aliases={0: 2, 1: 3},
        compiler_params=pltpu.CompilerParams(has_side_effects=DATAFLOW),
    )(pltpu.with_memory_space_constraint(x, pltpu.HBM),
      pltpu.with_memory_space_constraint(lax.empty(land_shape, x.dtype), pltpu.HBM), *deps)


def _exchange_wait(name, started, after, masks, slot_kind, per_dest, n_split):
    send_sems, recv_sems, x_thru, land_thru, _ = started

    def body(x_ref, land_ref, send_sems, recv_sems, after_ref, x_out, land_out):
        sends, recvs = _split_copies(x_ref, land_ref, send_sems, recv_sems, masks, slot_kind, per_dest, n_split)
        for cp in sends:
            cp.wait_send()
        for cp in recvs:
            cp.wait_recv()

    return pl.pallas_call(
        body, name=name,
        out_shape=(pltpu.HBM(x_thru.shape, x_thru.dtype), pltpu.HBM(land_thru.shape, land_thru.dtype)),
        in_specs=(HBM_SPEC, HBM_SPEC, SEM_SPEC, SEM_SPEC, pl.BlockSpec(memory_space=pl.ANY)),
        out_specs=(HBM_SPEC, HBM_SPEC), input_output_aliases={0: 0, 1: 1},
        compiler_params=pltpu.CompilerParams(has_side_effects=DATAFLOW),
    )(x_thru, land_thru, send_sems, recv_sems, after)


def _gather_chips(name, x):
    return _exchange(name, x, CHIP_MASKS, "chip", False)


def _gather_shards(name, shard, n_split, overlap=False, dep=None):
    half = shard.shape[0] // 2
    mine = lax.dynamic_slice_in_dim(shard, lax.axis_index("c") * half, half, axis=0)
    if overlap:
        return _exchange_start(name + "_chips", mine, CHIP_MASKS, "chip", False, n_split, dep)
    by_chip = _exchange(name + "_chips", mine, CHIP_MASKS, "chip", False, n_split, copy_own=False)
    return _gather_tail(name, mine, by_chip)


def _gather_tail(name, mine, by_chip):
    c, chip = lax.axis_index("c"), 2 * lax.axis_index("x") + lax.axis_index("y")
    by_chip = lax.dynamic_update_index_in_dim(by_chip, mine, chip, 0)
    both = _exchange(name + "_cores", by_chip, CORE_MASKS, "core", False, N_CHIPS, copy_own=False)
    return lax.dynamic_update_index_in_dim(both, by_chip, c, 0)


def _gather_finish(name, started, after, n_split):
    mine, by_chip = _exchange_wait(name + "_chips_wait", started, after, CHIP_MASKS, "chip", False, n_split)
    return _gather_tail(name, mine, by_chip)


def _sum_slots(name, x, tr=128):
    S, R, N = x.shape
    tr = _pick(R, tr, 16)

    def body(x_ref, o_ref):
        acc = x_ref[0].astype(F32)
        for s in range(1, S):
            acc = acc + x_ref[s].astype(F32)
        o_ref[...] = acc

    return pl.pallas_call(
        body, name=name, grid=(R // tr,),
        in_specs=[pl.BlockSpec((S, tr, N), lambda i: (0, i, 0))], out_specs=pl.BlockSpec((tr, N), lambda i: (i, 0)),
        out_shape=jax.ShapeDtypeStruct((R, N), F32), compiler_params=_params(("parallel",)),
    )(x)


def _add_to_bf16(name, a, b, tr=128):
    S, R, N = a.shape
    tr = _pick(R, tr, 16)

    def body(a_ref, b_ref, o_ref):
        o_ref[...] = (a_ref[...] + b_ref[...]).astype(BF16)

    spec = pl.BlockSpec((S, tr, N), lambda i: (0, i, 0))
    return pl.pallas_call(
        body, name=name, grid=(R // tr,), in_specs=[spec, spec], out_specs=spec,
        out_shape=jax.ShapeDtypeStruct((S, R, N), BF16), compiler_params=_params(("parallel",)),
    )(a, b)


def _add_slabs_to_bf16(name, full, recv, c, tr=64):
    R, NP = full.shape
    half = R // 2
    nb = half // tr

    def body(c_ref, a_ref, b_ref, o_ref):
        x = a_ref[...] + b_ref[...]
        for s in range(N_CHIPS - 1):
            o_ref[s] = x[:, s * SHARD_COLS:(s + 1) * SHARD_COLS].astype(BF16)
        last = jnp.concatenate([x[:, (N_CHIPS - 1) * SHARD_COLS:GAP_AT], x[:, GAP_AT + GAP:]], axis=1)
        o_ref[N_CHIPS - 1] = last.astype(BF16)

    grid_spec = pltpu.PrefetchScalarGridSpec(
        num_scalar_prefetch=1, grid=(nb,),
        in_specs=[pl.BlockSpec((tr, NP), lambda i, c_ref: (c_ref[0] * nb + i, 0)),
                  pl.BlockSpec((tr, NP), lambda i, c_ref: (i, 0))],
        out_specs=pl.BlockSpec((N_CHIPS, tr, SHARD_COLS), lambda i, c_ref: (0, i, 0)))
    return pl.pallas_call(
        body, name=name, grid_spec=grid_spec,
        out_shape=jax.ShapeDtypeStruct((N_CHIPS, half, SHARD_COLS), BF16), compiler_params=_params(("parallel",)),
    )(jnp.reshape(c, (1,)).astype(jnp.int32), full, recv)


def _adamw(name, gparts, w, m, v, tr=128):
    S, R, N = gparts.shape
    tr = _pick(R, tr)
    c1 = 1.0 / (1.0 - ADAM_B1 ** ADAM_STEP)
    c2 = 1.0 / (1.0 - ADAM_B2 ** ADAM_STEP)

    def body(g_ref, w_ref, m_ref, v_ref, go_ref, d_ref, mo_ref, vo_ref):
        g = g_ref[0]
        for s in range(1, S):
            g = g + g_ref[s]
        m_new = ADAM_B1 * m_ref[...] + (1.0 - ADAM_B1) * g
        v_new = ADAM_B2 * v_ref[...] + (1.0 - ADAM_B2) * jnp.square(g)
        go_ref[...] = g
        mo_ref[...] = m_new
        vo_ref[...] = v_new
        d_ref[...] = -ADAM_LR * ((m_new * c1) / (jnp.sqrt(v_new * c2) + ADAM_EPS) + ADAM_WD * w_ref[...])

    spec = pl.BlockSpec((tr, N), lambda i: (i, 0))
    return pl.pallas_call(
        body, name=name, grid=(R // tr,),
        in_specs=[pl.BlockSpec((S, tr, N), lambda i: (0, i, 0)), spec, spec, spec], out_specs=[spec] * 4,
        out_shape=[jax.ShapeDtypeStruct((R, N), F32)] * 4, compiler_params=_params(("parallel",)),
    )(gparts, w, m, v)


def _reduce_to_shard(name, slabs, n_split, by_columns=False, overlap=False):
    c, chip = lax.axis_index("c"), 2 * lax.axis_index("x") + lax.axis_index("y")
    if by_columns:
        R, N = slabs.shape[0], SHARD_COLS
        from_sibling = _exchange(name + "_sib", slabs, CORE_MASKS, None, False, n_split, other_half=True)[0]
        wire = _add_slabs_to_bf16(name + "_add", slabs, from_sibling, c)
    else:
        _, R, N = slabs.shape
        half = R // 2
        halves = slabs.reshape(N_CHIPS, 2, half, N)
        mine = lax.dynamic_index_in_dim(halves, c, axis=1, keepdims=False)
        theirs = lax.dynamic_index_in_dim(halves, 1 - c, axis=1, keepdims=False)
        from_sibling = _exchange(name + "_sib", theirs, CORE_MASKS, None, False, N_CHIPS)[0]
        wire = _add_to_bf16(name + "_add", mine, from_sibling)
    if overlap:
        return _exchange_start(name + "_chips", wire, CHIP_MASKS, "chip", True, n_split), (R, N)
    got = _exchange(name + "_chips", wire, CHIP_MASKS, "chip", True, n_split, copy_own=False)
    return _reduce_tail(name, wire, got, n_split, R, N)


def _reduce_tail(name, wire, got, n_split, R, N):
    c, chip = lax.axis_index("c"), 2 * lax.axis_index("x") + lax.axis_index("y")
    got = lax.dynamic_update_index_in_dim(got, lax.dynamic_index_in_dim(wire, chip, 0, keepdims=False), chip, 0)
    part = _sum_slots(name + "_sum", got)
    both = _exchange(name + "_cores", part, CORE_MASKS, "core", False, n_split, copy_own=False)
    return lax.dynamic_update_index_in_dim(both, part, c, 0).reshape(1, R, N)


def _reduce_finish(name, started, after, n_split):
    handle, (R, N) = started
    wire, got = _exchange_wait(name + "_chips_wait", handle, after, CHIP_MASKS, "chip", True, n_split)
    return _reduce_tail(name, wire, got, n_split, R, N)


def _pack(pieces, cols, row_mult=8):
    flat = jnp.concatenate([p.reshape(-1) for p in pieces])
    rows = -(-flat.shape[0] // cols)
    rows = -(-rows // row_mult) * row_mult
    return jnp.pad(flat, (0, rows * cols - flat.shape[0])).reshape(rows, cols)


def _unpack(packed, shapes):
    flat = packed.reshape(-1)
    out, off = [], 0
    for shp in shapes:
        n = 1
        for d in shp:
            n *= d
        out.append(flat[off:off + n].reshape(shp))
        off += n
    return out


def _to_my_layout(w):
    z = jnp.zeros((w.shape[0], 112), w.dtype)
    return jnp.concatenate([w[:, 0:3072], w[:, 3200:4224], w[:, 4224:7296], w[:, 7312:8336],
                            w[:, 3072:3200], w[:, 7296:7312], z], axis=1)


def kernel(x, meta_tokens, norm_w, w_in, rw_shift_mu, rw_w0, rw_w2, rw_a0, rw_a2, rw_k_k, rw_k_a, rw_r_k, rw_gn_w, rw_gn_b, dn_conv_w, dn_A_log, dn_dt_bias, dn_norm_w, w_out, final_norm_w, loss_target, m_meta_tokens, m_norm_w, m_w_in, m_rw_shift_mu, m_rw_w0, m_rw_w2, m_rw_a0, m_rw_a2, m_rw_k_k, m_rw_k_a, m_rw_r_k, m_rw_gn_w, m_rw_gn_b, m_dn_conv_w, m_dn_A_log, m_dn_dt_bias, m_dn_norm_w, m_w_out, m_final_norm_w, v_meta_tokens, v_norm_w, v_w_in, v_rw_shift_mu, v_rw_w0, v_rw_w2, v_rw_a0, v_rw_a2, v_rw_k_k, v_rw_k_a, v_rw_r_k, v_rw_gn_w, v_rw_gn_b, v_dn_conv_w, v_dn_A_log, v_dn_dt_bias, v_dn_norm_w, v_w_out, v_final_norm_w):
    S = x.shape[1]
    L = N_META + S
    Lp = -(-L // CHUNK) * CHUNK

    small_shapes = [(RW_LORA, 256), (RW_LORA, 256), (CONV_W, 768), (N_META, 512)]
    small_mine = _pack([rw_w2[0], rw_a2[0], dn_conv_w[0], meta_tokens], 1024)
    small_all = _gather_chips("gather_small", small_mine)
    per_chip = [_unpack(small_all[s], small_shapes) for s in range(N_CHIPS)]
    w2, a2, conv_w, meta = [jnp.concatenate([per_chip[s][i] for s in range(N_CHIPS)], axis=1) for i in range(4)]
    w_in_started = _gather_shards("gather_w_in", w_in[0].astype(BF16), 8, overlap=True, dep=small_all)
    w_out_started = _gather_shards("gather_w_out", w_out[0].astype(BF16), 8, overlap=True, dep=w_in_started[4])

    tail = [jnp.zeros((Lp - L, D_MODEL), F32)] if Lp > L else []
    h = jnp.concatenate([meta + w_in_started[4][:1, :1], x[0]] + tail, axis=0)
    tgt = jnp.concatenate([jnp.zeros((N_META, D_MODEL), F32), loss_target[0]] + tail, axis=0)
    (u,) = _rowwise("rms_in", _rms_fn, [_row(h)], [norm_w], [D_MODEL], tm=2 * ROW_TILE, out_dtype=BF16)
    w_in_all = _gather_finish("gather_w_in", w_in_started, u, 8)
    W_orig = jnp.transpose(w_in_all, (0, 2, 1, 3)).reshape(D_MODEL, IN_COLS)
    W = _to_my_layout(W_orig)
    W_gapped_t = jnp.concatenate([W_orig[:, :GAP_AT], jnp.zeros((D_MODEL, GAP), BF16), W_orig[:, GAP_AT:]], axis=1).T
    p = _mm("in_proj", u, W, "nn", dep=w_out_started[4])

    mu = rw_shift_mu
    zpad = jnp.zeros((RW_LORA, RW_WIDTH), F32)
    rw_params = [mu[:, 0:1024], mu[:, 1024:2048], mu[:, 2048:3072], mu[:, 3072:3200], rw_w0,
                 jnp.concatenate([w2, zpad], axis=0), rw_a0, jnp.concatenate([zpad, a2], axis=0), rw_k_k, rw_k_a]
    rw_rows = [_row(p, 1024, CB_R), _row(p, 1024, CB_K), _row(p, 1024, CB_V), _row(p, 128, CB_S1),
               _row(p, 1024, CB_GATE)]
    rw_pars = rw_params + [rw_r_k, rw_gn_w, rw_gn_b]
    rw_saved_shapes = [(N_GROUPS, CHUNK, GROUP), (N_GROUPS, CHUNK, GROUP), (N_GROUPS, 2 * CHUNK, 2 * GROUP),
                       (N_GROUPS, 2 * CHUNK, GROUP)]
    ya, rw_ck, *rw_t = _chunk_fwd("rw_chunk_fwd", _rw_fused_fn, rw_rows, 4, rw_pars, rw_saved_shapes)

    dn_rows = [_row(p, 1024, CB_DQ), _row(p, 1024, CB_DK), _row(p, 1024, CB_DV), _row(p, 128, CB_S2),
               _row(p, 1024, CB_Z)]
    dn_pars = [conv_w[j:j + 1, 1024 * i:1024 * (i + 1)] for i in range(3) for j in range(CONV_W)]
    narrow = lambda t: jnp.pad(t, ((0, 0), (DN_HEADS, DN_HEAD - 2 * DN_HEADS)))
    dn_pars += [narrow(dn_A_log), narrow(dn_dt_bias), jnp.tile(dn_norm_w, (1, DN_HEADS))]
    yb, dn_ck, *dn_t = _chunk_fwd("dn_chunk_fwd", _dn_fused_fn, dn_rows, 3, dn_pars, [(N_GROUPS, CHUNK, CHUNK)])

    y = jnp.concatenate([ya, yb], axis=1)
    w_out_all = _gather_finish("gather_w_out", w_out_started, y, 8)
    Wo = jnp.transpose(w_out_all, (1, 0, 2, 3)).reshape(D_MODEL, D_MODEL)
    yw = _mm("out_proj", y, Wo, "nn", tn=1024)
    loss_acc, dz, dz16, d_fw = _loss_head(h, yw, tgt, final_norm_w.reshape(1, D_MODEL), L)
    loss = lax.psum(loss_acc[0, 0], ("x", "y", "c"))

    d_wo = _mm("d_w_out", y.T, dz16, "nn", tm=512, tn=1024, tk=Lp)
    wo_started = _reduce_to_shard("rs_w_out", d_wo.reshape(N_CHIPS, D_MODEL // N_CHIPS, D_MODEL), 8, overlap=True)
    dy = _mm("d_out_proj", dz16, Wo.T, "nn", tn=1024, tk=2048, dep=wo_started[0][4])

    dp, d_rw_pars = _chunk_bwd("rw_chunk_bwd", _rw_fused_fn, rw_rows, 4, rw_pars, rw_ck, rw_t, _row(dy, 1024, 0),
                               (2, 0))
    d_prep_pars, d_rw_pars = d_rw_pars[:10], d_rw_pars[10:]
    dp, d_dn_pars = _chunk_bwd("dn_chunk_bwd", _dn_fused_fn, dn_rows, 3, dn_pars, dn_ck, dn_t, _row(dy, 1024, 1),
                               (2, 1), into=dp)
    d_conv_parts = [jnp.concatenate(d_dn_pars[4 * i:4 * i + 4], axis=0) for i in range(3)]
    d_a_log_b, d_dt_b = d_dn_pars[12], d_dn_pars[13]
    d_dn_nw = jnp.sum(d_dn_pars[14].reshape(DN_HEADS, DN_HEAD), axis=0, keepdims=True)
    d_W = _mm("d_w_in", u.T, dp, "nn", tm=1024, tn=768, tk=Lp)
    wi_started = _reduce_to_shard("rs_w_in", d_W, 8, by_columns=True, overlap=True)
    du = _mm("d_in_proj", dp, W_gapped_t, "nn", tn=2048, tk=1408, dep=wi_started[0][4])
    (dh,), (d_norm_w,) = _rowwise_bwd("rms_in_bwd", lambda h_, w_: (_rms_fn(h_, w_)[0], h_), [_row(h)], [norm_w],
                                      [_row(du), _row(dz)])
    grad_x = dh[N_META:L][None]

    d_mu = jnp.concatenate(d_prep_pars[0:4], axis=1)
    d_w2, d_a2 = d_prep_pars[5][:RW_LORA], d_prep_pars[7][RW_LORA:]
    d_conv = jnp.concatenate(d_conv_parts, axis=1)
    d_meta = dh[:N_META]
    head_sum = lambda t: t[:, DN_HEADS:2 * DN_HEADS]
    rep_names = ["norm_w", "rw_shift_mu", "rw_w0", "rw_a0", "rw_k_k", "rw_k_a", "rw_r_k", "rw_gn_w", "rw_gn_b",
                 "dn_A_log", "dn_dt_bias", "dn_norm_w", "final_norm_w"]
    rep_g = [d_norm_w, d_mu, d_prep_pars[4], d_prep_pars[6], d_prep_pars[8], d_prep_pars[9],
             d_rw_pars[0], d_rw_pars[1], d_rw_pars[2],
             head_sum(d_a_log_b), head_sum(d_dt_b), d_dn_nw, d_fw.reshape(D_MODEL)]
    rep_w = [norm_w, rw_shift_mu, rw_w0, rw_a0, rw_k_k, rw_k_a, rw_r_k, rw_gn_w, rw_gn_b, dn_A_log, dn_dt_bias,
             dn_norm_w, final_norm_w]
    rep_m = [m_norm_w, m_rw_shift_mu, m_rw_w0, m_rw_a0, m_rw_k_k, m_rw_k_a, m_rw_r_k, m_rw_gn_w, m_rw_gn_b,
             m_dn_A_log, m_dn_dt_bias, m_dn_norm_w, m_final_norm_w]
    rep_v = [v_norm_w, v_rw_shift_mu, v_rw_w0, v_rw_a0, v_rw_k_k, v_rw_k_a, v_rw_r_k, v_rw_gn_w, v_rw_gn_b,
             v_dn_A_log, v_dn_dt_bias, v_dn_norm_w, v_final_norm_w]
    rep_shapes = [t.shape for t in rep_w]
    rep_all = _exchange("gather_rep_grads", _pack(rep_g, 128), ALL_MASKS, "dev", False)
    rep_out = _adamw("adam_rep", rep_all, _pack(rep_w, 128), _pack(rep_m, 128), _pack(rep_v, 128))
    rep_out = [dict(zip(rep_names, _unpack(t, rep_shapes))) for t in rep_out]

    sm_slabs = jnp.stack([_pack([d_w2[:, 256 * s:256 * (s + 1)], d_a2[:, 256 * s:256 * (s + 1)],
                                 d_conv[:, 768 * s:768 * (s + 1)], d_meta[:, 512 * s:512 * (s + 1)]], 1024, 64)
                          for s in range(N_CHIPS)])
    sm_parts = _reduce_to_shard("rs_small", sm_slabs, 1)
    sm_w = [rw_w2[0], rw_a2[0], dn_conv_w[0], meta_tokens]
    sm_m = [m_rw_w2[0], m_rw_a2[0], m_dn_conv_w[0], m_meta_tokens]
    sm_v = [v_rw_w2[0], v_rw_a2[0], v_dn_conv_w[0], v_meta_tokens]
    sm_out = _adamw("adam_small", sm_parts, _pack(sm_w, 1024, 64), _pack(sm_m, 1024, 64), _pack(sm_v, 1024, 64))
    sm_names = ["rw_w2", "rw_a2", "dn_conv_w", "meta_tokens"]
    sm_full_shapes = [(1, RW_LORA, 256), (1, RW_LORA, 256), (1, CONV_W, 768), (N_META, 512)]
    sm_out = [dict(zip(sm_names, [t.reshape(shp) for t, shp in zip(_unpack(o, small_shapes), sm_full_shapes)]))
              for o in sm_out]

    wo_parts = _reduce_finish("rs_w_out", wo_started, dp, 8)
    wo_out = _adamw("adam_w_out", wo_parts, w_out[0], m_w_out[0], v_w_out[0])
    wi_parts = _reduce_finish("rs_w_in", wi_started, dh, 8)
    wi_out = _adamw("adam_w_in", wi_parts, w_in[0], m_w_in[0], v_w_in[0])

    order = ["meta_tokens", "norm_w", "w_in", "rw_shift_mu", "rw_w0", "rw_w2", "rw_a0", "rw_a2", "rw_k_k", "rw_k_a",
             "rw_r_k", "rw_gn_w", "rw_gn_b", "dn_conv_w", "dn_A_log", "dn_dt_bias", "dn_norm_w", "w_out",
             "final_norm_w"]
    outs = [loss, grad_x]
    for kind in range(4):
        table = dict(rep_out[kind])
        table.update(sm_out[kind])
        table["w_in"] = wi_out[kind][None]
        table["w_out"] = wo_out[kind][None]
        outs += [table[n] for n in order]
    return tuple(outs)
```

```python
import functools

import jax
import jax.numpy as jnp
from jax import lax
from jax.experimental import pallas as pl
from jax.experimental.pallas import tpu as pltpu

F32 = jnp.float32
BF16 = jnp.bfloat16
HIGH = lax.Precision.HIGH
HIGHEST = lax.Precision.HIGHEST
MESH = pl.DeviceIdType.MESH

D_MODEL = 2048
N_META = 16
RW_WIDTH = 1024
RW_HEAD = 64
RW_HEADS = 16
RW_LORA = 64
RW_GN_EPS = 64e-5
DN_WIDTH = 1024
DN_HEAD = 128
DN_HEADS = 8
CONV_W = 4
CHUNK = 64
NORM_EPS = 1e-6
IN_COLS = 8336
N_CHIPS = 4
SHARD_COLS = IN_COLS // N_CHIPS

NP_COLS = 8 * 1024 + 256
GAP_AT = IN_COLS - DN_WIDTH
GAP = NP_COLS - IN_COLS
CB_R, CB_K, CB_V, CB_GATE, CB_DQ, CB_DK, CB_DV, CB_Z = range(8)
CB_S1 = 8192 // 128
CB_S2 = CB_S1 + 1

ADAM_LR = 0.001
ADAM_B1 = 0.9
ADAM_B2 = 0.999
ADAM_EPS = 1e-08
ADAM_WD = 0.01
ADAM_STEP = 10

VMEM_LIMIT_BYTES = 56 * 1024 * 1024
ROW_TILE = 104
MM_ROW_TILE = 832


def _params(sem=None):
    return pltpu.CompilerParams(dimension_semantics=sem, vmem_limit_bytes=VMEM_LIMIT_BYTES)


def _pick(n, target, mult=8):
    best = None
    for d in range(mult, min(n, target) + 1, mult):
        if n % d == 0:
            best = d
    return n if best is None else best


def _mm(name, a, b, mode, tm=MM_ROW_TILE, tn=1408, tk=2048, dep=None):
    if mode == "nn":
        (M, K), (_, N) = a.shape, b.shape
    elif mode == "nt":
        (M, K), (N, _) = a.shape, b.shape
    else:
        (K, M), (_, N) = a.shape, b.shape
    tm = _pick(M, tm, 128 if mode == "tn" else 16)
    tn = _pick(N, tn, 128)
    tk = _pick(K, tk, 8 if mode == "tn" else 128)
    if mode == "nn":
        a_spec = pl.BlockSpec((tm, tk), lambda i, j, k: (i, k))
        b_spec = pl.BlockSpec((tk, tn), lambda i, j, k: (k, j))
        dims = (((1,), (0,)), ((), ()))
    elif mode == "nt":
        a_spec = pl.BlockSpec((tm, tk), lambda i, j, k: (i, k))
        b_spec = pl.BlockSpec((tn, tk), lambda i, j, k: (j, k))
        dims = (((1,), (1,)), ((), ()))
    else:
        a_spec = pl.BlockSpec((tk, tm), lambda i, j, k: (k, i))
        b_spec = pl.BlockSpec((tk, tn), lambda i, j, k: (k, j))
        dims = (((0,), (0,)), ((), ()))

    def body(a_ref, b_ref, *rest):
        o_ref = rest[-1]

        @pl.when(pl.program_id(2) == 0)
        def _():
            o_ref[...] = jnp.zeros_like(o_ref)

        o_ref[...] += lax.dot_general(a_ref[...].astype(BF16), b_ref[...].astype(BF16), dims,
                                      preferred_element_type=F32)

    deps = [] if dep is None else [dep]
    return pl.pallas_call(
        body, name=name, grid=(M // tm, N // tn, K // tk),
        in_specs=[a_spec, b_spec] + [pl.BlockSpec((8, 128), lambda i, j, k: (0, 0))] * len(deps),
        out_specs=pl.BlockSpec((tm, tn), lambda i, j, k: (i, j)),
        out_shape=jax.ShapeDtypeStruct((M, N), F32),
        compiler_params=_params(("parallel", "parallel", "arbitrary")),
    )(a, b, *deps)


def _row(arr, width=None, cb=0):
    return (arr, arr.shape[1] if width is None else width, cb)


def _rowwise(name, fn, rows, params, out_widths, tm=ROW_TILE, out_dtype=F32):
    R = rows[0][0].shape[0]
    tm = _pick(R, tm, 16 if out_dtype == BF16 else 8)
    n_r, n_p = len(rows), len(params)

    def body(*refs):
        vals = [r[...] for r in refs[:n_r + n_p]]
        for o_ref, val in zip(refs[n_r + n_p:], fn(*vals)):
            o_ref[...] = val.astype(out_dtype)

    in_specs = [pl.BlockSpec((tm, w), lambda i, cb=cb: (i, cb)) for (_, w, cb) in rows]
    in_specs += [pl.BlockSpec(p.shape, lambda i: (0, 0)) for p in params]
    return pl.pallas_call(
        body, name=name, grid=(R // tm,), in_specs=in_specs,
        out_specs=[pl.BlockSpec((tm, w), lambda i: (i, 0)) for w in out_widths],
        out_shape=[jax.ShapeDtypeStruct((R, w), out_dtype) for w in out_widths],
        compiler_params=_params(("parallel",)),
    )(*[r[0] for r in rows], *params)


def _rowwise_bwd(name, fn, rows, params, douts, tm=ROW_TILE):
    R = rows[0][0].shape[0]
    tm = _pick(R, tm)
    n_r, n_p, n_d = len(rows), len(params), len(douts)

    def body(*refs):
        vals = [r[...] for r in refs[:n_r + n_p]]
        cts = tuple(r[...] for r in refs[n_r + n_p:n_r + n_p + n_d])
        grads = jax.vjp(fn, *vals)[1](cts)
        outs = refs[n_r + n_p + n_d:]
        for o_ref, g in zip(outs[:n_r], grads[:n_r]):
            o_ref[...] = g

        @pl.when(pl.program_id(0) == 0)
        def _():
            for o_ref in outs[n_r:]:
                o_ref[...] = jnp.zeros_like(o_ref)

        for o_ref, g in zip(outs[n_r:], grads[n_r:]):
            o_ref[...] += g

    in_specs = [pl.BlockSpec((tm, w), lambda i, cb=cb: (i, cb)) for (_, w, cb) in rows]
    in_specs += [pl.BlockSpec(p.shape, lambda i: (0, 0)) for p in params]
    in_specs += [pl.BlockSpec((tm, w), lambda i, cb=cb: (i, cb)) for (_, w, cb) in douts]
    out_specs = [pl.BlockSpec((tm, w), lambda i: (i, 0)) for (_, w, _) in rows]
    out_specs += [pl.BlockSpec(p.shape, lambda i: (0, 0)) for p in params]
    out_shape = [jax.ShapeDtypeStruct((R, w), F32) for (_, w, _) in rows]
    out_shape += [jax.ShapeDtypeStruct(p.shape, F32) for p in params]
    res = pl.pallas_call(
        body, name=name, grid=(R // tm,), in_specs=in_specs, out_specs=out_specs, out_shape=out_shape,
        compiler_params=_params(("arbitrary",)),
    )(*[r[0] for r in rows], *params, *[d[0] for d in douts])
    return res[:n_r], res[n_r:]


def _softplus(x):
    return jnp.maximum(x, 0.0) + jnp.log(1.0 + jnp.exp(-jnp.abs(x)))


def _silu(x):
    return x * jax.nn.sigmoid(x)


def _rms_fn(h, w):
    return (h * lax.rsqrt(jnp.mean(h * h, axis=-1, keepdims=True) + NORM_EPS) * w,)


def _rw_prep_fn(pr, pr1, pk, pk1, pv, pv1, ps, ps1, mu_r, mu_k, mu_v, mu_s, w0, w2p, a0, a2p, k_k, k_a):
    r = pr + (pr1 - pr) * mu_r
    k = pk + (pk1 - pk) * mu_k
    v = pv + (pv1 - pv) * mu_v
    s = ps + (ps1 - ps) * mu_s
    lora = lambda x, w: jnp.dot(x.astype(BF16), w.astype(BF16), preferred_element_type=F32)
    w_log = -_softplus(-(w0 + lora(jnp.tanh(s), w2p))) - 0.5
    log_decay = -jnp.exp(w_log)
    a = jax.nn.sigmoid(a0 + lora(s, a2p))
    return r, log_decay, k * (1.0 + (a - 1.0) * k_a), v, k * k_k, a


def _conv_fn(u0, u1, u2, u3, w0, w1, w2, w3):
    return (_silu(u0 * w3 + u1 * w2 + u2 * w1 + u3 * w0),)


def _dn_gate_fn(ps2, a_log_n, dt_n):
    beta_n = jax.nn.sigmoid(ps2)
    g_n = -jnp.exp(a_log_n) * _softplus(ps2 + dt_n)
    lane = lax.broadcasted_iota(jnp.int32, (1, DN_HEAD), 1)
    pick = lambda x, j: jnp.broadcast_to(jnp.sum(x * (lane == j).astype(F32), axis=-1, keepdims=True), x.shape)[None]
    beta = jnp.concatenate([pick(beta_n, h) for h in range(DN_HEADS)], axis=0)
    g = jnp.concatenate([pick(g_n, DN_HEADS + h) for h in range(DN_HEADS)], axis=0)
    return beta, g


def _tril_masks(c):
    t = lax.broadcasted_iota(jnp.int32, (c, c), 0)
    s = lax.broadcasted_iota(jnp.int32, (c, c), 1)
    return s <= t, s < t


def _dot(x, y, cx, cy, prec=HIGH):
    nb = x.ndim - 2
    batch = tuple(range(nb))
    return lax.dot_general(x, y, (((cx + nb,), (cy + nb,)), (batch, batch)), precision=prec,
                           preferred_element_type=F32)


def _tri_inv(a):
    c = a.shape[-1]
    eye = (lax.broadcasted_iota(jnp.int32, (c, c), 0) == lax.broadcasted_iota(jnp.int32, (c, c), 1)).astype(F32)
    x = eye - a
    p = a
    n = 2
    while n < c:
        p = _dot(p, p, 1, 0)
        x = x + _dot(x, p, 1, 0)
        n *= 2
    return x


@jax.custom_vjp
def _tri_inv_saved(a, t):
    return t


def _tri_inv_saved_fwd(a, t):
    return t, t


def _tri_inv_saved_bwd(t, dt):
    return -_dot(_dot(t, dt, 0, 0), t, 1, 1), jnp.zeros_like(t)


_tri_inv_saved.defvjp(_tri_inv_saved_fwd, _tri_inv_saved_bwd)


def _pair_masks():
    lane = lax.broadcasted_iota(jnp.int32, (1, 2 * RW_HEAD), 1)
    m0 = (lane < RW_HEAD).astype(F32)
    return m0, 1.0 - m0


def _pair_bd(x):
    m0, m1 = _pair_masks()
    return jnp.concatenate([x * m0, x * m1], axis=-2)


def _pair_mm(x, y):
    return _dot(x, _pair_bd(y), 1, 0)


def _pair_inv(a):
    c = a.shape[-2]
    row = lax.broadcasted_iota(jnp.int32, (c, 2 * RW_HEAD), 0)
    col = lax.broadcasted_iota(jnp.int32, (c, 2 * RW_HEAD), 1) & (RW_HEAD - 1)
    x = (row == col).astype(F32) - a
    p = a
    n = 2
    while n < c:
        p = _pair_mm(p, p)
        x = x + _pair_mm(x, p)
        n *= 2
    return x


@jax.custom_vjp
def _pair_inv_saved(a, t):
    return t


def _pair_inv_saved_fwd(a, t):
    return t, t


def _pair_inv_saved_bwd(t, dt):
    m0, m1 = _pair_masks()
    c = t.shape[-2]
    z = _dot(t, dt, 0, 0)
    x = z[:, :c, :] * m0 + z[:, c:, :] * m1
    return -_dot(x, _pair_bd(t), 1, 1), jnp.zeros_like(t)


_pair_inv_saved.defvjp(_pair_inv_saved_fwd, _pair_inv_saved_bwd)


@jax.custom_vjp
def _use_saved(x, x_saved):
    return x_saved


_use_saved.defvjp(lambda x, x_saved: (x_saved, None), lambda _, g: (g, jnp.zeros_like(g)))


def _rw_chunk_fn(S, r, lw, k2, v, kkp, a, gate, rk, gnw, gnb, saved=None):
    B, C, P = r.shape
    m0, m1 = _pair_masks()
    seg = lambda x: (jnp.sum(x * m0, axis=-1, keepdims=True) * m0 + jnp.sum(x * m1, axis=-1, keepdims=True) * m1)
    mm = lambda x, y: _dot(x.astype(BF16), y.astype(BF16), 1, 0, None)
    nt = lambda x, y: _dot(x, y, 1, 1)
    tn = lambda x, y: _dot(x, y, 0, 0)
    pair_mm = lambda x, y: mm(x, _pair_bd(y))
    t_idx = lax.broadcasted_iota(jnp.int32, (C, P), 0)
    s_idx = lax.broadcasted_iota(jnp.int32, (C, P), 1) & (RW_HEAD - 1)
    incl, strict = s_idx <= t_idx, s_idx < t_idx
    tril = jnp.broadcast_to(_tril_masks(C)[0].astype(F32), (B, C, C))
    kk = kkp * lax.rsqrt(seg(kkp * kkp) + 1e-6)
    b = kk * a
    reuse = (lambda x, i: x) if saved is None else (lambda x, i: _use_saved(x, saved[i]))
    g_incl = reuse(_dot(tril, lw, 1, 0), 1)
    g_excl = g_incl - lw
    inv = jnp.exp(-g_incl)
    alpha, beta, kappa, rho = kk * jnp.exp(g_excl), b * inv, k2 * inv, r * jnp.exp(g_incl)
    ar = jnp.concatenate([alpha, rho], axis=-2)
    scores = reuse(nt(ar, jnp.concatenate([_pair_bd(beta), _pair_bd(kappa)], axis=-2)), 2)
    a_ab = jnp.where(strict, scores[:, :C, :P], 0.0)
    a_ak = jnp.where(strict, scores[:, :C, P:], 0.0)
    r_b = jnp.where(incl, scores[:, C:, :P], 0.0)
    r_k = jnp.where(incl, scores[:, C:, P:], 0.0)
    t_inv = _pair_inv(a_ab) if saved is None else _pair_inv_saved(a_ab, saved[0])
    on_state = reuse(nt(ar, S), 3)
    u = pair_mm(t_inv, -on_state[:, :C, :] - pair_mm(a_ak, v))
    y = on_state[:, C:, :] + mm(jnp.concatenate([r_b, r_k], axis=-1),
                                jnp.concatenate([_pair_bd(u), _pair_bd(v)], axis=-2))
    same_head = ((lax.broadcasted_iota(jnp.int32, (P, P), 0) < RW_HEAD)
                 == (lax.broadcasted_iota(jnp.int32, (P, P), 1) < RW_HEAD))
    fresh = tn(jnp.concatenate([u, v], axis=-2), jnp.concatenate([beta, kappa], axis=-2))
    S_new = jnp.exp(jnp.sum(lw, axis=-2, keepdims=True)) * (S + jnp.where(same_head, fresh, 0.0))
    dev = y - seg(y) * (1.0 / RW_HEAD)
    yn = dev * lax.rsqrt(seg(dev * dev) * (1.0 / RW_HEAD) + RW_GN_EPS) * gnw + gnb
    bonus = seg(r * k2 * rk) * v
    return (yn + bonus) * _silu(gate), S_new, (t_inv, g_incl, scores, on_state)


def _dn_chunk_fn(S, qc, kc, vc, bb, gb, z, nw, saved=None):
    B, C, D = qc.shape
    mm = lambda x, y: _dot(x.astype(BF16), y.astype(BF16), 1, 0, None)
    nt = lambda x, y, p=None: _dot(x, y, 1, 1, p) if p else _dot(x.astype(BF16), y.astype(BF16), 1, 1, None)
    tn = lambda x, y: _dot(x.astype(BF16), y.astype(BF16), 0, 0, None)
    incl, strict = _tril_masks(C)
    q = qc * lax.rsqrt(jnp.sum(qc * qc, axis=-1, keepdims=True) + 1e-6) * (D ** -0.5)
    k = kc * lax.rsqrt(jnp.sum(kc * kc, axis=-1, keepdims=True) + 1e-6)
    kb, vb = k * bb, vc * bb
    G = _dot(jnp.broadcast_to(incl.astype(F32), (B, C, C)), gb, 1, 0, HIGHEST)
    lane = lax.broadcasted_iota(jnp.int32, (C, D), 1)
    e0, e1 = (lane == 0).astype(F32), (lane == 1).astype(F32)
    diff = nt(G * e0 + e1, e0 - G * e1, HIGHEST)
    dmask = jnp.where(incl, jnp.exp(jnp.where(incl, diff, 0.0)), 0.0)
    M = jnp.where(strict, nt(kb, k) * dmask, 0.0)
    T = _tri_inv(M) if saved is None else _tri_inv_saved(M, saved[0])
    eG = jnp.exp(G)
    u = mm(T, vb)
    w = mm(T, kb * eG)
    attn = jnp.where(incl, nt(q, k) * dmask, 0.0)
    v_new = u - mm(w, S)
    o = mm(q * eG, S) + mm(attn, v_new)
    g_last = jnp.sum(gb, axis=-2, keepdims=True)
    S_new = S * jnp.exp(jnp.broadcast_to(g_last, S.shape)) + tn(k * jnp.exp(g_last - G), v_new)
    on = o * lax.rsqrt(jnp.mean(o * o, axis=-1, keepdims=True) + NORM_EPS) * nw
    return on * _silu(z), S_new, (T,)


N_GROUPS = 8
GROUP = 128
HALO = 8


def _groups(x):
    return jnp.concatenate([x[:, g * GROUP:(g + 1) * GROUP][None] for g in range(N_GROUPS)], axis=0)


@functools.partial(jax.custom_vjp, nondiff_argnums=(1,))
def _shift_rows(ext, j):
    return pltpu.roll(ext, j, 0)[HALO:, :]


def _shift_rows_fwd(ext, j):
    return _shift_rows(ext, j), None


def _shift_rows_bwd(j, _, d):
    z = jnp.concatenate([jnp.zeros((HALO, d.shape[1]), d.dtype), d], axis=0)
    return (pltpu.roll(z, z.shape[0] - j, 0),)


_shift_rows.defvjp(_shift_rows_fwd, _shift_rows_bwd)


def _rw_fused_fn(S, h_r, h_k, h_v, h_s, p_r, p_k, p_v, p_s, gate, *pars, saved=None):
    prev = lambda h, x: _shift_rows(jnp.concatenate([h, x], axis=0), 1)
    seq = _rw_prep_fn(p_r, prev(h_r, p_r), p_k, prev(h_k, p_k), p_v, prev(h_v, p_v), p_s, prev(h_s, p_s), *pars[:10])
    return _rw_chunk_fn(S, *[_groups(t) for t in seq], _groups(gate), *[_groups(t) for t in pars[10:]],
                        saved=saved)


def _dn_fused_fn(S, h_q, h_k, h_v, p_q, p_k, p_v, p_s2, z, *pars, saved=None):
    conv = []
    for i, (h, x) in enumerate(((h_q, p_q), (h_k, p_k), (h_v, p_v))):
        ext = jnp.concatenate([h, x], axis=0)
        conv += _conv_fn(x, _shift_rows(ext, 1), _shift_rows(ext, 2), _shift_rows(ext, 3), *pars[4 * i:4 * i + 4])
    beta, g = _dn_gate_fn(p_s2, pars[12], pars[13])
    return _dn_chunk_fn(S, *[_groups(t) for t in conv], beta, g, _groups(z), _groups(pars[14]), saved=saved)


def _chunk_fwd(name, fn, rows, n_halo, pars, saved_shapes):
    Lp = rows[0][0].shape[0]
    C, D, NB = CHUNK, GROUP, N_GROUPS
    nc, n_r, n_p, per = Lp // C, len(rows), len(pars), CHUNK // HALO
    n_s = len(saved_shapes)

    def body(*refs):
        row_refs, halo_refs, par_refs = refs[:n_r], refs[n_r:n_r + n_halo], refs[n_r + n_halo:n_r + n_halo + n_p]
        y_ref, ck_ref = refs[n_r + n_halo + n_p:n_r + n_halo + n_p + 2]
        saved_refs, s_ref = refs[n_r + n_halo + n_p + 2:-1], refs[-1]
        first = pl.program_id(0) == 0

        @pl.when(first)
        def _():
            s_ref[...] = jnp.zeros_like(s_ref)

        S = s_ref[...]
        ck_ref[0] = S
        halos = [jnp.where(first, 0.0, r[...]) for r in halo_refs]
        y, S_new, saved = fn(S, *halos, *[r[...] for r in row_refs], *[r[...] for r in par_refs])
        for g in range(NB):
            y_ref[:, g * D:(g + 1) * D] = y[g].astype(BF16)
        for o_ref, val in zip(saved_refs, saved):
            o_ref[0] = val
        s_ref[...] = S_new

    return pl.pallas_call(
        body, name=name, grid=(nc,),
        in_specs=[pl.BlockSpec((C, w), lambda c, cb=cb: (c, cb)) for (_, w, cb) in rows]
        + [pl.BlockSpec((HALO, w), lambda c, cb=cb: (jnp.maximum(c * per - 1, 0), cb)) for (_, w, cb) in rows[:n_halo]]
        + [pl.BlockSpec(p.shape, lambda c: (0, 0)) for p in pars],
        out_specs=[pl.BlockSpec((C, NB * D), lambda c: (c, 0)), pl.BlockSpec((1, NB, D, D), lambda c: (c, 0, 0, 0))]
        + [pl.BlockSpec((1,) + tuple(shp), lambda c: (c, 0, 0, 0)) for shp in saved_shapes],
        out_shape=[jax.ShapeDtypeStruct((Lp, NB * D), BF16), jax.ShapeDtypeStruct((nc, NB, D, D), F32)]
        + [jax.ShapeDtypeStruct((nc,) + tuple(shp), F32) for shp in saved_shapes],
        scratch_shapes=[pltpu.VMEM((NB, D, D), F32)],
        compiler_params=_params(("arbitrary",)),
    )(*[r[0] for r in rows], *[r[0] for r in rows[:n_halo]], *pars)


def _chunk_bwd(name, fn, rows, n_halo, pars, ckpt, saved, dy, out_blocks, into=None):
    Lp = rows[0][0].shape[0]
    C, D, NB = CHUNK, GROUP, N_GROUPS
    nc, n_r, n_p, per = Lp // C, len(rows), len(pars), CHUNK // HALO
    n_s = len(saved)
    n_in = n_r + n_halo + n_p
    width = sum(w for (_, w, _) in rows)
    n_blocks, my_block = out_blocks
    extra = [] if into is None else [into]

    def body(*refs):
        row_refs, halo_refs, par_refs = refs[:n_r], refs[n_r:n_r + n_halo], refs[n_r + n_halo:n_in]
        ck_ref, saved_refs, dy_ref = refs[n_in], refs[n_in + 1:n_in + 1 + n_s], refs[n_in + 1 + n_s]
        outs = refs[n_in + 2 + n_s + len(extra):]
        drows_ref, dpar_refs = outs[0], outs[1:1 + n_p]
        ds_ref, carry_refs = outs[1 + n_p], outs[2 + n_p:]
        i = pl.program_id(0)

        @pl.when(i == 0)
        def _():
            ds_ref[...] = jnp.zeros_like(ds_ref)
            for o_ref in list(dpar_refs) + list(carry_refs):
                o_ref[...] = jnp.zeros_like(o_ref)

        halos = [jnp.where(i == nc - 1, 0.0, r[...]) for r in halo_refs]
        vals = [ck_ref[0]] + halos + [r[...] for r in row_refs] + [r[...] for r in par_refs]
        stored = tuple(r[0] for r in saved_refs)
        grads = jax.vjp(lambda *v: fn(*v, saved=stored)[:2], *vals)[1]((_groups(dy_ref), ds_ref[...]))
        ds_ref[...] = grads[0]
        d_halos, d_rows, d_pars = grads[1:1 + n_halo], grads[1 + n_halo:1 + n_halo + n_r], grads[1 + n_halo + n_r:]
        col = 0
        for k, g in enumerate(d_rows):
            if k < n_halo:
                g = g + jnp.concatenate([jnp.zeros((C - HALO, g.shape[1]), F32), carry_refs[k][...]], axis=0)
                carry_refs[k][...] = d_halos[k]
            drows_ref[:, col:col + g.shape[1]] = g.astype(BF16)
            col += g.shape[1]
        for o_ref, g in zip(dpar_refs, d_pars):
            o_ref[...] += g

    rev = lambda c: nc - 1 - c
    dy_arr, dy_w, dy_cb = dy
    res = pl.pallas_call(
        body, name=name, grid=(nc,),
        in_specs=[pl.BlockSpec((C, w), lambda c, cb=cb: (rev(c), cb)) for (_, w, cb) in rows]
        + [pl.BlockSpec((HALO, w), lambda c, cb=cb: (jnp.maximum(rev(c) * per - 1, 0), cb))
           for (_, w, cb) in rows[:n_halo]]
        + [pl.BlockSpec(p.shape, lambda c: (0, 0)) for p in pars]
        + [pl.BlockSpec((1, NB, D, D), lambda c: (rev(c), 0, 0, 0))]
        + [pl.BlockSpec((1,) + tuple(t.shape[1:]), lambda c: (rev(c), 0, 0, 0)) for t in saved]
        + [pl.BlockSpec((C, dy_w), lambda c: (rev(c), dy_cb))]
        + [pl.BlockSpec(memory_space=pl.ANY)] * len(extra),
        out_specs=[pl.BlockSpec((C, width), lambda c: (rev(c), my_block))]
        + [pl.BlockSpec(p.shape, lambda c: (0, 0)) for p in pars],
        out_shape=[jax.ShapeDtypeStruct((Lp, n_blocks * width), BF16)]
        + [jax.ShapeDtypeStruct(p.shape, F32) for p in pars],
        scratch_shapes=[pltpu.VMEM((NB, D, D), F32)] + [pltpu.VMEM((HALO, w), F32) for (_, w, _) in rows[:n_halo]],
        input_output_aliases={} if into is None else {n_in + 2 + n_s: 0},
        compiler_params=_params(("arbitrary",)),
    )(*[r[0] for r in rows], *[r[0] for r in rows[:n_halo]], *pars, ckpt, *saved, dy_arr, *extra)
    return res[0], res[1:]


def _loss_head(h, yw, tgt, fw, n_real, tm=2 * ROW_TILE):
    Lp, Dm = h.shape
    tm = _pick(Lp, tm, 16)

    def out_fn(z, fw_):
        return z * lax.rsqrt(jnp.mean(z * z, axis=-1, keepdims=True) + NORM_EPS) * fw_

    def body(h_ref, yw_ref, t_ref, fw_ref, loss_ref, dz_ref, dz16_ref, dfw_ref):
        i = pl.program_id(0)

        @pl.when(i == 0)
        def _():
            loss_ref[...] = jnp.zeros_like(loss_ref)
            dfw_ref[...] = jnp.zeros_like(dfw_ref)

        row = i * tm + lax.broadcasted_iota(jnp.int32, (tm, 1), 0)
        mask = ((row >= N_META) & (row < n_real)).astype(F32)
        z = h_ref[...] + yw_ref[...]
        o, vjp = jax.vjp(out_fn, z, fw_ref[...])
        err = (o - t_ref[...]) * mask
        row_loss = 0.5 * jnp.mean(jnp.square(err), axis=-1, keepdims=True)
        dz, dfw = vjp(err * (1.0 / Dm))
        loss_ref[...] += jnp.sum(row_loss, axis=0, keepdims=True)
        dz_ref[...] = dz
        dz16_ref[...] = dz.astype(BF16)
        dfw_ref[...] += dfw

    row_spec = pl.BlockSpec((tm, Dm), lambda i: (i, 0))
    return pl.pallas_call(
        body, name="loss_head", grid=(Lp // tm,),
        in_specs=[row_spec, row_spec, row_spec, pl.BlockSpec((1, Dm), lambda i: (0, 0))],
        out_specs=[pl.BlockSpec((8, 128), lambda i: (0, 0)), row_spec, row_spec,
                   pl.BlockSpec((1, Dm), lambda i: (0, 0))],
        out_shape=[jax.ShapeDtypeStruct((8, 128), F32), jax.ShapeDtypeStruct((Lp, Dm), F32),
                   jax.ShapeDtypeStruct((Lp, Dm), BF16), jax.ShapeDtypeStruct((1, Dm), F32)],
        compiler_params=_params(("arbitrary",)),
    )(h, yw, tgt, fw)


def _exchange(name, x, masks, slot_kind, per_dest, n_split=1, copy_own=True, other_half=False):
    n = len(masks)
    keep_own = slot_kind is not None and copy_own
    n_slots = {"chip": 4, "core": 2, "dev": 8, None: n}[slot_kind]
    blk_shape = x.shape[1:] if per_dest else x.shape
    if other_half:
        blk_shape = (x.shape[0] // 2,) + tuple(x.shape[1:])
    rows = blk_shape[0] // n_split

    def body(x_ref, o_ref, send_sems, recv_sems, local_sems):
        mx, my, mc = lax.axis_index("x"), lax.axis_index("y"), lax.axis_index("c")
        if other_half:
            x_ref = x_ref.at[pl.ds((1 - mc) * blk_shape[0], blk_shape[0])]

        def slot(k, px, py, pc):
            return {"chip": 2 * px + py, "core": pc, "dev": 4 * px + 2 * py + pc, None: k}[slot_kind]

        def peer(m):
            return (mx + m[0]) % 2, (my + m[1]) % 2, (mc + m[2]) % 2

        def part(ref, j):
            return ref.at[pl.ds(j * rows, rows)]

        own_src = x_ref.at[2 * mx + my] if per_dest else x_ref
        local = []
        if keep_own:
            own_dst = o_ref.at[slot(0, mx, my, mc)]
            local = [pltpu.make_async_copy(part(own_src, j), part(own_dst, j), local_sems.at[j])
                     for j in range(n_split)]
        for cp in local:
            cp.start()
        sends = []
        for k, m in enumerate(masks):
            px, py, pc = peer(m)
            src = x_ref.at[2 * px + py] if per_dest else x_ref
            dst = o_ref.at[slot(k, mx, my, mc)]
            for j in range(n_split):
                sends.append(pltpu.make_async_remote_copy(
                    src_ref=part(src, j), dst_ref=part(dst, j), send_sem=send_sems.at[k * n_split + j],
                    recv_sem=recv_sems.at[k * n_split + j], device_id=(px, py, pc), device_id_type=MESH))
        for cp in sends:
            cp.start()
        for k, m in enumerate(masks):
            px, py, pc = peer(m)
            landed = o_ref.at[slot(k, px, py, pc)]
            for j in range(n_split):
                pltpu.make_async_remote_copy(
                    src_ref=part(own_src, j), dst_ref=part(landed, j), send_sem=send_sems.at[k * n_split + j],
                    recv_sem=recv_sems.at[k * n_split + j], device_id=(px, py, pc), device_id_type=MESH).wait_recv()
        for cp in sends:
            cp.wait_send()
        for cp in local:
            cp.wait()

    return pl.pallas_call(
        body, name=name,
        in_specs=[pl.BlockSpec(memory_space=pl.ANY)], out_specs=pl.BlockSpec(memory_space=pl.ANY),
        out_shape=jax.ShapeDtypeStruct((n_slots,) + tuple(blk_shape), x.dtype),
        scratch_shapes=[pltpu.SemaphoreType.DMA((n * n_split,)), pltpu.SemaphoreType.DMA((n * n_split,)),
                        pltpu.SemaphoreType.DMA((n_split,))],
        compiler_params=pltpu.CompilerParams(has_side_effects=True),
    )(x)


CHIP_MASKS = [(1, 0, 0), (0, 1, 0), (1, 1, 0)]
CORE_MASKS = [(0, 0, 1)]
ALL_MASKS = [(dx, dy, dc) for dx in (0, 1) for dy in (0, 1) for dc in (0, 1) if (dx, dy, dc) != (0, 0, 0)]

HBM_SPEC = pl.BlockSpec(memory_space=pltpu.HBM)
SEM_SPEC = pl.BlockSpec(memory_space=pltpu.SEMAPHORE)
DATAFLOW = pltpu.SideEffectType.DATAFLOW_SIDE_EFFECTING


def _split_copies(x_ref, land_ref, send_sems, recv_sems, masks, slot_kind, per_dest, n_split, with_recvs=True):
    mx, my, mc = lax.axis_index("x"), lax.axis_index("y"), lax.axis_index("c")
    slot = lambda px, py, pc: {"chip": 2 * px + py, "core": pc}[slot_kind]
    rows = (x_ref.shape[1] if per_dest else x_ref.shape[0]) // n_split
    part = lambda ref, j: ref.at[pl.ds(j * rows, rows)]
    sends, recvs = [], []
    for k, m in enumerate(masks):
        px, py, pc = (mx + m[0]) % 2, (my + m[1]) % 2, (mc + m[2]) % 2
        src = x_ref.at[2 * px + py] if per_dest else x_ref
        own_src = x_ref.at[2 * mx + my] if per_dest else x_ref
        for j in range(n_split):
            sems = dict(send_sem=send_sems.at[k * n_split + j], recv_sem=recv_sems.at[k * n_split + j],
                        device_id=(px, py, pc), device_id_type=MESH)
            sends.append(pltpu.make_async_remote_copy(
                src_ref=part(src, j), dst_ref=part(land_ref.at[slot(mx, my, mc)], j), **sems))
            if with_recvs:
                recvs.append(pltpu.make_async_remote_copy(
                    src_ref=part(own_src, j), dst_ref=part(land_ref.at[slot(px, py, pc)], j), **sems))
    return sends, recvs


def _exchange_start(name, x, masks, slot_kind, per_dest, n_split, dep=None):
    n = len(masks) * n_split
    blk_shape = x.shape[1:] if per_dest else x.shape
    land_shape = ({"chip": 4, "core": 2}[slot_kind],) + tuple(blk_shape)
    deps = [] if dep is None else [dep]

    def body(x_ref, land_ref, *rest):
        send_sems, recv_sems, x_thru, land_thru, token = rest[len(deps):]
        for cp in _split_copies(x_ref, land_ref, send_sems, recv_sems, masks, slot_kind, per_dest, n_split, False)[0]:
            cp.start()
        token[...] = jnp.zeros_like(token)

    return pl.pallas_call(
        body, name=name,
        out_shape=(pltpu.SemaphoreType.DMA((n,)), pltpu.SemaphoreType.DMA((n,)), pltpu.HBM(x.shape, x.dtype),
                   pltpu.HBM(land_shape, x.dtype), jax.ShapeDtypeStruct((8, 128), F32)),
        in_specs=(HBM_SPEC, HBM_SPEC) + (pl.BlockSpec(memory_space=pl.ANY),) * len(deps),
        out_specs=(SEM_SPEC, SEM_SPEC, HBM_SPEC, HBM_SPEC, pl.BlockSpec(memory_space=pltpu.VMEM)),
        input_output_aliases={0: 2, 1: 3},
        compiler_params=pltpu.CompilerParams(has_side_effects=DATAFLOW),
    )(pltpu.with_memory_space_constraint(x, pltpu.HBM),
      pltpu.with_memory_space_constraint(lax.empty(land_shape, x.dtype), pltpu.HBM), *deps)


def _exchange_wait(name, started, after, masks, slot_kind, per_dest, n_split):
    send_sems, recv_sems, x_thru, land_thru, _ = started

    def body(x_ref, land_ref, send_sems, recv_sems, after_ref, x_out, land_out):
        sends, recvs = _split_copies(x_ref, land_ref, send_sems, recv_sems, masks, slot_kind, per_dest, n_split)
        for cp in sends:
            cp.wait_send()
        for cp in recvs:
            cp.wait_recv()

    return pl.pallas_call(
        body, name=name,
        out_shape=(pltpu.HBM(x_thru.shape, x_thru.dtype), pltpu.HBM(land_thru.shape, land_thru.dtype)),
        in_specs=(HBM_SPEC, HBM_SPEC, SEM_SPEC, SEM_SPEC, pl.BlockSpec(memory_space=pl.ANY)),
        out_specs=(HBM_SPEC, HBM_SPEC), input_output_aliases={0: 0, 1: 1},
        compiler_params=pltpu.CompilerParams(has_side_effects=DATAFLOW),
    )(x_thru, land_thru, send_sems, recv_sems, after)


def _gather_chips(name, x):
    return _exchange(name, x, CHIP_MASKS, "chip", False)


def _gather_shards(name, shard, n_split, overlap=False, dep=None):
    half = shard.shape[0] // 2
    mine = lax.dynamic_slice_in_dim(shard, lax.axis_index("c") * half, half, axis=0)
    if overlap:
        return _exchange_start(name + "_chips", mine, CHIP_MASKS, "chip", False, n_split, dep)
    by_chip = _exchange(name + "_chips", mine, CHIP_MASKS, "chip", False, n_split, copy_own=False)
    return _gather_tail(name, mine, by_chip)


def _gather_tail(name, mine, by_chip):
    c, chip = lax.axis_index("c"), 2 * lax.axis_index("x") + lax.axis_index("y")
    by_chip = lax.dynamic_update_index_in_dim(by_chip, mine, chip, 0)
    both = _exchange(name + "_cores", by_chip, CORE_MASKS, "core", False, N_CHIPS, copy_own=False)
    return lax.dynamic_update_index_in_dim(both, by_chip, c, 0)


def _gather_finish(name, started, after, n_split):
    mine, by_chip = _exchange_wait(name + "_chips_wait", started, after, CHIP_MASKS, "chip", False, n_split)
    return _gather_tail(name, mine, by_chip)


def _sum_slots(name, x, tr=128):
    S, R, N = x.shape
    tr = _pick(R, tr, 16)

    def body(x_ref, o_ref):
        acc = x_ref[0].astype(F32)
        for s in range(1, S):
            acc = acc + x_ref[s].astype(F32)
        o_ref[...] = acc

    return pl.pallas_call(
        body, name=name, grid=(R // tr,),
        in_specs=[pl.BlockSpec((S, tr, N), lambda i: (0, i, 0))], out_specs=pl.BlockSpec((tr, N), lambda i: (i, 0)),
        out_shape=jax.ShapeDtypeStruct((R, N), F32), compiler_params=_params(("parallel",)),
    )(x)


def _add_to_bf16(name, a, b, tr=128):
    S, R, N = a.shape
    tr = _pick(R, tr, 16)

    def body(a_ref, b_ref, o_ref):
        o_ref[...] = (a_ref[...] + b_ref[...]).astype(BF16)

    spec = pl.BlockSpec((S, tr, N), lambda i: (0, i, 0))
    return pl.pallas_call(
        body, name=name, grid=(R // tr,), in_specs=[spec, spec], out_specs=spec,
        out_shape=jax.ShapeDtypeStruct((S, R, N), BF16), compiler_params=_params(("parallel",)),
    )(a, b)


def _add_slabs_to_bf16(name, full, recv, c, tr=64):
    R, NP = full.shape
    half = R // 2
    nb = half // tr

    def body(c_ref, a_ref, b_ref, o_ref):
        x = a_ref[...] + b_ref[...]
        for s in range(N_CHIPS - 1):
            o_ref[s] = x[:, s * SHARD_COLS:(s + 1) * SHARD_COLS].astype(BF16)
        last = jnp.concatenate([x[:, (N_CHIPS - 1) * SHARD_COLS:GAP_AT], x[:, GAP_AT + GAP:]], axis=1)
        o_ref[N_CHIPS - 1] = last.astype(BF16)

    grid_spec = pltpu.PrefetchScalarGridSpec(
        num_scalar_prefetch=1, grid=(nb,),
        in_specs=[pl.BlockSpec((tr, NP), lambda i, c_ref: (c_ref[0] * nb + i, 0)),
                  pl.BlockSpec((tr, NP), lambda i, c_ref: (i, 0))],
        out_specs=pl.BlockSpec((N_CHIPS, tr, SHARD_COLS), lambda i, c_ref: (0, i, 0)))
    return pl.pallas_call(
        body, name=name, grid_spec=grid_spec,
        out_shape=jax.ShapeDtypeStruct((N_CHIPS, half, SHARD_COLS), BF16), compiler_params=_params(("parallel",)),
    )(jnp.reshape(c, (1,)).astype(jnp.int32), full, recv)


def _adamw(name, gparts, w, m, v, tr=128):
    S, R, N = gparts.shape
    tr = _pick(R, tr)
    c1 = 1.0 / (1.0 - ADAM_B1 ** ADAM_STEP)
    c2 = 1.0 / (1.0 - ADAM_B2 ** ADAM_STEP)

    def body(g_ref, w_ref, m_ref, v_ref, go_ref, d_ref, mo_ref, vo_ref):
        g = g_ref[0]
        for s in range(1, S):
            g = g + g_ref[s]
        m_new = ADAM_B1 * m_ref[...] + (1.0 - ADAM_B1) * g
        v_new = ADAM_B2 * v_ref[...] + (1.0 - ADAM_B2) * jnp.square(g)
        go_ref[...] = g
        mo_ref[...] = m_new
        vo_ref[...] = v_new
        d_ref[...] = -ADAM_LR * ((m_new * c1) / (jnp.sqrt(v_new * c2) + ADAM_EPS) + ADAM_WD * w_ref[...])

    spec = pl.BlockSpec((tr, N), lambda i: (i, 0))
    return pl.pallas_call(
        body, name=name, grid=(R // tr,),
        in_specs=[pl.BlockSpec((S, tr, N), lambda i: (0, i, 0)), spec, spec, spec], out_specs=[spec] * 4,
        out_shape=[jax.ShapeDtypeStruct((R, N), F32)] * 4, compiler_params=_params(("parallel",)),
    )(gparts, w, m, v)


def _reduce_to_shard(name, slabs, n_split, by_columns=False, overlap=False):
    c, chip = lax.axis_index("c"), 2 * lax.axis_index("x") + lax.axis_index("y")
    if by_columns:
        R, N = slabs.shape[0], SHARD_COLS
        from_sibling = _exchange(name + "_sib", slabs, CORE_MASKS, None, False, n_split, other_half=True)[0]
        wire = _add_slabs_to_bf16(name + "_add", slabs, from_sibling, c)
    else:
        _, R, N = slabs.shape
        half = R // 2
        halves = slabs.reshape(N_CHIPS, 2, half, N)
        mine = lax.dynamic_index_in_dim(halves, c, axis=1, keepdims=False)
        theirs = lax.dynamic_index_in_dim(halves, 1 - c, axis=1, keepdims=False)
        from_sibling = _exchange(name + "_sib", theirs, CORE_MASKS, None, False, N_CHIPS)[0]
        wire = _add_to_bf16(name + "_add", mine, from_sibling)
    if overlap:
        return _exchange_start(name + "_chips", wire, CHIP_MASKS, "chip", True, n_split), (R, N)
    got = _exchange(name + "_chips", wire, CHIP_MASKS, "chip", True, n_split, copy_own=False)
    return _reduce_tail(name, wire, got, n_split, R, N)


def _reduce_tail(name, wire, got, n_split, R, N):
    c, chip = lax.axis_index("c"), 2 * lax.axis_index("x") + lax.axis_index("y")
    got = lax.dynamic_update_index_in_dim(got, lax.dynamic_index_in_dim(wire, chip, 0, keepdims=False), chip, 0)
    part = _sum_slots(name + "_sum", got)
    both = _exchange(name + "_cores", part, CORE_MASKS, "core", False, n_split, copy_own=False)
    return lax.dynamic_update_index_in_dim(both, part, c, 0).reshape(1, R, N)


def _reduce_finish(name, started, after, n_split):
    handle, (R, N) = started
    wire, got = _exchange_wait(name + "_chips_wait", handle, after, CHIP_MASKS, "chip", True, n_split)
    return _reduce_tail(name, wire, got, n_split, R, N)


def _pack(pieces, cols, row_mult=8):
    flat = jnp.concatenate([p.reshape(-1) for p in pieces])
    rows = -(-flat.shape[0] // cols)
    rows = -(-rows // row_mult) * row_mult
    return jnp.pad(flat, (0, rows * cols - flat.shape[0])).reshape(rows, cols)


def _unpack(packed, shapes):
    flat = packed.reshape(-1)
    out, off = [], 0
    for shp in shapes:
        n = 1
        for d in shp:
            n *= d
        out.append(flat[off:off + n].reshape(shp))
        off += n
    return out


def _to_my_layout(w):
    z = jnp.zeros((w.shape[0], 112), w.dtype)
    return jnp.concatenate([w[:, 0:3072], w[:, 3200:4224], w[:, 4224:7296], w[:, 7312:8336],
                            w[:, 3072:3200], w[:, 7296:7312], z], axis=1)


def kernel(x, meta_tokens, norm_w, w_in, rw_shift_mu, rw_w0, rw_w2, rw_a0, rw_a2, rw_k_k, rw_k_a, rw_r_k, rw_gn_w, rw_gn_b, dn_conv_w, dn_A_log, dn_dt_bias, dn_norm_w, w_out, final_norm_w, loss_target, m_meta_tokens, m_norm_w, m_w_in, m_rw_shift_mu, m_rw_w0, m_rw_w2, m_rw_a0, m_rw_a2, m_rw_k_k, m_rw_k_a, m_rw_r_k, m_rw_gn_w, m_rw_gn_b, m_dn_conv_w, m_dn_A_log, m_dn_dt_bias, m_dn_norm_w, m_w_out, m_final_norm_w, v_meta_tokens, v_norm_w, v_w_in, v_rw_shift_mu, v_rw_w0, v_rw_w2, v_rw_a0, v_rw_a2, v_rw_k_k, v_rw_k_a, v_rw_r_k, v_rw_gn_w, v_rw_gn_b, v_dn_conv_w, v_dn_A_log, v_dn_dt_bias, v_dn_norm_w, v_w_out, v_final_norm_w):
    S = x.shape[1]
    L = N_META + S
    Lp = -(-L // CHUNK) * CHUNK

    small_shapes = [(RW_LORA, 256), (RW_LORA, 256), (CONV_W, 768), (N_META, 512)]
    small_mine = _pack([rw_w2[0], rw_a2[0], dn_conv_w[0], meta_tokens], 1024)
    small_all = _gather_chips("gather_small", small_mine)
    per_chip = [_unpack(small_all[s], small_shapes) for s in range(N_CHIPS)]
    w2, a2, conv_w, meta = [jnp.concatenate([per_chip[s][i] for s in range(N_CHIPS)], axis=1) for i in range(4)]
    w_in_started = _gather_shards("gather_w_in", w_in[0].astype(BF16), 8, overlap=True, dep=small_all)
    w_out_started = _gather_shards("gather_w_out", w_out[0].astype(BF16), 8, overlap=True, dep=w_in_started[4])

    tail = [jnp.zeros((Lp - L, D_MODEL), F32)] if Lp > L else []
    h = jnp.concatenate([meta + w_in_started[4][:1, :1], x[0]] + tail, axis=0)
    tgt = jnp.concatenate([jnp.zeros((N_META, D_MODEL), F32), loss_target[0]] + tail, axis=0)
    (u,) = _rowwise("rms_in", _rms_fn, [_row(h)], [norm_w], [D_MODEL], tm=2 * ROW_TILE, out_dtype=BF16)
    w_in_all = _gather_finish("gather_w_in", w_in_started, u, 8)
    W_orig = jnp.transpose(w_in_all, (0, 2, 1, 3)).reshape(D_MODEL, IN_COLS)
    W = _to_my_layout(W_orig)
    W_gapped = jnp.concatenate([W_orig[:, :GAP_AT], jnp.zeros((D_MODEL, GAP), BF16), W_orig[:, GAP_AT:]], axis=1)
    p = _mm("in_proj", u, W, "nn", dep=w_out_started[4])

    mu = rw_shift_mu
    zpad = jnp.zeros((RW_LORA, RW_WIDTH), F32)
    rw_params = [mu[:, 0:1024], mu[:, 1024:2048], mu[:, 2048:3072], mu[:, 3072:3200], rw_w0,
                 jnp.concatenate([w2, zpad], axis=0), rw_a0, jnp.concatenate([zpad, a2], axis=0), rw_k_k, rw_k_a]
    rw_rows = [_row(p, 1024, CB_R), _row(p, 1024, CB_K), _row(p, 1024, CB_V), _row(p, 128, CB_S1),
               _row(p, 1024, CB_GATE)]
    rw_pars = rw_params + [rw_r_k, rw_gn_w, rw_gn_b]
    rw_saved_shapes = [(N_GROUPS, CHUNK, GROUP), (N_GROUPS, CHUNK, GROUP), (N_GROUPS, 2 * CHUNK, 2 * GROUP),
                       (N_GROUPS, 2 * CHUNK, GROUP)]
    ya, rw_ck, *rw_t = _chunk_fwd("rw_chunk_fwd", _rw_fused_fn, rw_rows, 4, rw_pars, rw_saved_shapes)

    dn_rows = [_row(p, 1024, CB_DQ), _row(p, 1024, CB_DK), _row(p, 1024, CB_DV), _row(p, 128, CB_S2),
               _row(p, 1024, CB_Z)]
    dn_pars = [conv_w[j:j + 1, 1024 * i:1024 * (i + 1)] for i in range(3) for j in range(CONV_W)]
    narrow = lambda t: jnp.pad(t, ((0, 0), (DN_HEADS, DN_HEAD - 2 * DN_HEADS)))
    dn_pars += [narrow(dn_A_log), narrow(dn_dt_bias), jnp.tile(dn_norm_w, (1, DN_HEADS))]
    yb, dn_ck, *dn_t = _chunk_fwd("dn_chunk_fwd", _dn_fused_fn, dn_rows, 3, dn_pars, [(N_GROUPS, CHUNK, CHUNK)])

    y = jnp.concatenate([ya, yb], axis=1)
    w_out_all = _gather_finish("gather_w_out", w_out_started, y, 8)
    Wo = jnp.transpose(w_out_all, (1, 0, 2, 3)).reshape(D_MODEL, D_MODEL)
    yw = _mm("out_proj", y, Wo, "nn", tn=1024)
    loss_acc, dz, dz16, d_fw = _loss_head(h, yw, tgt, final_norm_w.reshape(1, D_MODEL), L)
    loss = lax.psum(loss_acc[0, 0], ("x", "y", "c"))

    d_wo = _mm("d_w_out", y.T, dz16, "nn", tm=512, tn=1024, tk=Lp)
    wo_started = _reduce_to_shard("rs_w_out", d_wo.reshape(N_CHIPS, D_MODEL // N_CHIPS, D_MODEL), 8, overlap=True)
    dy = _mm("d_out_proj", dz16, Wo, "nt", tn=1024, tk=2048, dep=wo_started[0][4])

    dp, d_rw_pars = _chunk_bwd("rw_chunk_bwd", _rw_fused_fn, rw_rows, 4, rw_pars, rw_ck, rw_t, _row(dy, 1024, 0),
                               (2, 0))
    d_prep_pars, d_rw_pars = d_rw_pars[:10], d_rw_pars[10:]
    dp, d_dn_pars = _chunk_bwd("dn_chunk_bwd", _dn_fused_fn, dn_rows, 3, dn_pars, dn_ck, dn_t, _row(dy, 1024, 1),
                               (2, 1), into=dp)
    d_conv_parts = [jnp.concatenate(d_dn_pars[4 * i:4 * i + 4], axis=0) for i in range(3)]
    d_a_log_b, d_dt_b = d_dn_pars[12], d_dn_pars[13]
    d_dn_nw = jnp.sum(d_dn_pars[14].reshape(DN_HEADS, DN_HEAD), axis=0, keepdims=True)
    d_W = _mm("d_w_in", u.T, dp, "nn", tm=1024, tn=768, tk=Lp)
    wi_started = _reduce_to_shard("rs_w_in", d_W, 8, by_columns=True, overlap=True)
    du = _mm("d_in_proj", dp, W_gapped, "nt", tn=2048, tk=1408, dep=wi_started[0][4])
    (dh,), (d_norm_w,) = _rowwise_bwd("rms_in_bwd", lambda h_, w_: (_rms_fn(h_, w_)[0], h_), [_row(h)], [norm_w],
                                      [_row(du), _row(dz)])
    grad_x = dh[N_META:L][None]

    d_mu = jnp.concatenate(d_prep_pars[0:4], axis=1)
    d_w2, d_a2 = d_prep_pars[5][:RW_LORA], d_prep_pars[7][RW_LORA:]
    d_conv = jnp.concatenate(d_conv_parts, axis=1)
    d_meta = dh[:N_META]
    head_sum = lambda t: t[:, DN_HEADS:2 * DN_HEADS]
    rep_names = ["norm_w", "rw_shift_mu", "rw_w0", "rw_a0", "rw_k_k", "rw_k_a", "rw_r_k", "rw_gn_w", "rw_gn_b",
                 "dn_A_log", "dn_dt_bias", "dn_norm_w", "final_norm_w"]
    rep_g = [d_norm_w, d_mu, d_prep_pars[4], d_prep_pars[6], d_prep_pars[8], d_prep_pars[9],
             d_rw_pars[0], d_rw_pars[1], d_rw_pars[2],
             head_sum(d_a_log_b), head_sum(d_dt_b), d_dn_nw, d_fw.reshape(D_MODEL)]
    rep_w = [norm_w, rw_shift_mu, rw_w0, rw_a0, rw_k_k, rw_k_a, rw_r_k, rw_gn_w, rw_gn_b, dn_A_log, dn_dt_bias,
             dn_norm_w, final_norm_w]
    rep_m = [m_norm_w, m_rw_shift_mu, m_rw_w0, m_rw_a0, m_rw_k_k, m_rw_k_a, m_rw_r_k, m_rw_gn_w, m_rw_gn_b,
             m_dn_A_log, m_dn_dt_bias, m_dn_norm_w, m_final_norm_w]
    rep_v = [v_norm_w, v_rw_shift_mu, v_rw_w0, v_rw_a0, v_rw_k_k, v_rw_k_a, v_rw_r_k, v_rw_gn_w, v_rw_gn_b,
             v_dn_A_log, v_dn_dt_bias, v_dn_norm_w, v_final_norm_w]
    rep_shapes = [t.shape for t in rep_w]
    rep_all = _exchange("gather_rep_grads", _pack(rep_g, 128), ALL_MASKS, "dev", False)
    rep_out = _adamw("adam_rep", rep_all, _pack(rep_w, 128), _pack(rep_m, 128), _pack(rep_v, 128))
    rep_out = [dict(zip(rep_names, _unpack(t, rep_shapes))) for t in rep_out]

    sm_slabs = jnp.stack([_pack([d_w2[:, 256 * s:256 * (s + 1)], d_a2[:, 256 * s:256 * (s + 1)],
                                 d_conv[:, 768 * s:768 * (s + 1)], d_meta[:, 512 * s:512 * (s + 1)]], 1024, 64)
                          for s in range(N_CHIPS)])
    sm_parts = _reduce_to_shard("rs_small", sm_slabs, 1)
    sm_w = [rw_w2[0], rw_a2[0], dn_conv_w[0], meta_tokens]
    sm_m = [m_rw_w2[0], m_rw_a2[0], m_dn_conv_w[0], m_meta_tokens]
    sm_v = [v_rw_w2[0], v_rw_a2[0], v_dn_conv_w[0], v_meta_tokens]
    sm_out = _adamw("adam_small", sm_parts, _pack(sm_w, 1024, 64), _pack(sm_m, 1024, 64), _pack(sm_v, 1024, 64))
    sm_names = ["rw_w2", "rw_a2", "dn_conv_w", "meta_tokens"]
    sm_full_shapes = [(1, RW_LORA, 256), (1, RW_LORA, 256), (1, CONV_W, 768), (N_META, 512)]
    sm_out = [dict(zip(sm_names, [t.reshape(shp) for t, shp in zip(_unpack(o, small_shapes), sm_full_shapes)]))
              for o in sm_out]

    wo_parts = _reduce_finish("rs_w_out", wo_started, dp, 8)
    wo_out = _adamw("adam_w_out", wo_parts, w_out[0], m_w_out[0], v_w_out[0])
    wi_parts = _reduce_finish("rs_w_in", wi_started, dh, 8)
    wi_out = _adamw("adam_w_in", wi_parts, w_in[0], m_w_in[0], v_w_in[0])

    order = ["meta_tokens", "norm_w", "w_in", "rw_shift_mu", "rw_w0", "rw_w2", "rw_a0", "rw_a2", "rw_k_k", "rw_k_a",
             "rw_r_k", "rw_gn_w", "rw_gn_b", "dn_conv_w", "dn_A_log", "dn_dt_bias", "dn_norm_w", "w_out",
             "final_norm_w"]
    outs = [loss, grad_x]
    for kind in range(4):
        table = dict(rep_out[kind])
        table.update(sm_out[kind])
        table["w_in"] = wi_out[kind][None]
        table["w_out"] = wo_out[kind][None]
        outs += [table[n] for n in order]
    return tuple(outs)
```

```python
import functools

import jax
import jax.numpy as jnp
from jax import lax
from jax.experimental import pallas as pl
from jax.experimental.pallas import tpu as pltpu

F32 = jnp.float32
BF16 = jnp.bfloat16
HIGH = lax.Precision.HIGH
HIGHEST = lax.Precision.HIGHEST
MESH = pl.DeviceIdType.MESH

D_MODEL = 2048
N_META = 16
RW_WIDTH = 1024
RW_HEAD = 64
RW_HEADS = 16
RW_LORA = 64
RW_GN_EPS = 64e-5
DN_WIDTH = 1024
DN_HEAD = 128
DN_HEADS = 8
CONV_W = 4
CHUNK = 64
NORM_EPS = 1e-6
IN_COLS = 8336
N_CHIPS = 4
SHARD_COLS = IN_COLS // N_CHIPS

NP_COLS = 8 * 1024 + 256
GAP_AT = IN_COLS - DN_WIDTH
GAP = NP_COLS - IN_COLS
OFF_R, OFF_K, OFF_V, OFF_S1, OFF_GATE = 0, 1024, 2048, 3072, 3200
OFF_DQ, OFF_DK, OFF_DV, OFF_S2, OFF_Z = 4224, 5248, 6272, 7296, 7424

ADAM_LR = 0.001
ADAM_B1 = 0.9
ADAM_B2 = 0.999
ADAM_EPS = 1e-08
ADAM_WD = 0.01
ADAM_STEP = 10

VMEM_LIMIT_BYTES = 56 * 1024 * 1024
ROW_TILE = 104
MM_ROW_TILE = 832


def _params(sem=None):
    return pltpu.CompilerParams(dimension_semantics=sem, vmem_limit_bytes=VMEM_LIMIT_BYTES)


def _pick(n, target, mult=8):
    best = None
    for d in range(mult, min(n, target) + 1, mult):
        if n % d == 0:
            best = d
    return n if best is None else best


def _mm(name, a, b, mode, tm=MM_ROW_TILE, tn=1408, tk=2048, dep=None):
    if mode == "nn":
        (M, K), (_, N) = a.shape, b.shape
    elif mode == "nt":
        (M, K), (N, _) = a.shape, b.shape
    else:
        (K, M), (_, N) = a.shape, b.shape
    tm = _pick(M, tm, 128 if mode == "tn" else 16)
    tn = _pick(N, tn, 128)
    tk = _pick(K, tk, 8 if mode == "tn" else 128)
    if mode == "nn":
        a_spec = pl.BlockSpec((tm, tk), lambda i, j, k: (i, k))
        b_spec = pl.BlockSpec((tk, tn), lambda i, j, k: (k, j))
        dims = (((1,), (0,)), ((), ()))
    elif mode == "nt":
        a_spec = pl.BlockSpec((tm, tk), lambda i, j, k: (i, k))
        b_spec = pl.BlockSpec((tn, tk), lambda i, j, k: (j, k))
        dims = (((1,), (1,)), ((), ()))
    else:
        a_spec = pl.BlockSpec((tk, tm), lambda i, j, k: (k, i))
        b_spec = pl.BlockSpec((tk, tn), lambda i, j, k: (k, j))
        dims = (((0,), (0,)), ((), ()))

    def body(a_ref, b_ref, *rest):
        o_ref = rest[-1]

        @pl.when(pl.program_id(2) == 0)
        def _():
            o_ref[...] = jnp.zeros_like(o_ref)

        o_ref[...] += lax.dot_general(a_ref[...].astype(BF16), b_ref[...].astype(BF16), dims,
                                      preferred_element_type=F32)

    deps = [] if dep is None else [dep]
    return pl.pallas_call(
        body, name=name, grid=(M // tm, N // tn, K // tk),
        in_specs=[a_spec, b_spec] + [pl.BlockSpec((8, 128), lambda i, j, k: (0, 0))] * len(deps),
        out_specs=pl.BlockSpec((tm, tn), lambda i, j, k: (i, j)),
        out_shape=jax.ShapeDtypeStruct((M, N), F32),
        compiler_params=_params(("parallel", "parallel", "arbitrary")),
    )(a, b, *deps)


def _row(arr, width=None, cb=0):
    return (arr, arr.shape[1] if width is None else width, cb)


def _rowwise(name, fn, rows, params, out_widths, tm=ROW_TILE, out_dtype=F32):
    R = rows[0][0].shape[0]
    tm = _pick(R, tm, 16 if out_dtype == BF16 else 8)
    n_r, n_p = len(rows), len(params)

    def body(*refs):
        vals = [r[...] for r in refs[:n_r + n_p]]
        for o_ref, val in zip(refs[n_r + n_p:], fn(*vals)):
            o_ref[...] = val.astype(out_dtype)

    in_specs = [pl.BlockSpec((tm, w), lambda i, cb=cb: (i, cb)) for (_, w, cb) in rows]
    in_specs += [pl.BlockSpec(p.shape, lambda i: (0, 0)) for p in params]
    return pl.pallas_call(
        body, name=name, grid=(R // tm,), in_specs=in_specs,
        out_specs=[pl.BlockSpec((tm, w), lambda i: (i, 0)) for w in out_widths],
        out_shape=[jax.ShapeDtypeStruct((R, w), out_dtype) for w in out_widths],
        compiler_params=_params(("parallel",)),
    )(*[r[0] for r in rows], *params)


def _rowwise_bwd(name, fn, rows, params, douts, tm=ROW_TILE):
    R = rows[0][0].shape[0]
    tm = _pick(R, tm)
    n_r, n_p, n_d = len(rows), len(params), len(douts)

    def body(*refs):
        vals = [r[...] for r in refs[:n_r + n_p]]
        cts = tuple(r[...] for r in refs[n_r + n_p:n_r + n_p + n_d])
        grads = jax.vjp(fn, *vals)[1](cts)
        outs = refs[n_r + n_p + n_d:]
        for o_ref, g in zip(outs[:n_r], grads[:n_r]):
            o_ref[...] = g

        @pl.when(pl.program_id(0) == 0)
        def _():
            for o_ref in outs[n_r:]:
                o_ref[...] = jnp.zeros_like(o_ref)

        for o_ref, g in zip(outs[n_r:], grads[n_r:]):
            o_ref[...] += g

    in_specs = [pl.BlockSpec((tm, w), lambda i, cb=cb: (i, cb)) for (_, w, cb) in rows]
    in_specs += [pl.BlockSpec(p.shape, lambda i: (0, 0)) for p in params]
    in_specs += [pl.BlockSpec((tm, w), lambda i, cb=cb: (i, cb)) for (_, w, cb) in douts]
    out_specs = [pl.BlockSpec((tm, w), lambda i: (i, 0)) for (_, w, _) in rows]
    out_specs += [pl.BlockSpec(p.shape, lambda i: (0, 0)) for p in params]
    out_shape = [jax.ShapeDtypeStruct((R, w), F32) for (_, w, _) in rows]
    out_shape += [jax.ShapeDtypeStruct(p.shape, F32) for p in params]
    res = pl.pallas_call(
        body, name=name, grid=(R // tm,), in_specs=in_specs, out_specs=out_specs, out_shape=out_shape,
        compiler_params=_params(("arbitrary",)),
    )(*[r[0] for r in rows], *params, *[d[0] for d in douts])
    return res[:n_r], res[n_r:]


def _softplus(x):
    return jnp.maximum(x, 0.0) + jnp.log(1.0 + jnp.exp(-jnp.abs(x)))


def _silu(x):
    return x * jax.nn.sigmoid(x)


def _rms_fn(h, w):
    return (h * lax.rsqrt(jnp.mean(h * h, axis=-1, keepdims=True) + NORM_EPS) * w,)


def _rw_prep_fn(pr, pr1, pk, pk1, pv, pv1, ps, ps1, mu_r, mu_k, mu_v, mu_s, w0, w2p, a0, a2p, k_k, k_a):
    r = pr + (pr1 - pr) * mu_r
    k = pk + (pk1 - pk) * mu_k
    v = pv + (pv1 - pv) * mu_v
    s = ps + (ps1 - ps) * mu_s
    lora = lambda x, w: jnp.dot(x.astype(BF16), w.astype(BF16), preferred_element_type=F32)
    w_log = -_softplus(-(w0 + lora(jnp.tanh(s), w2p))) - 0.5
    log_decay = -jnp.exp(w_log)
    a = jax.nn.sigmoid(a0 + lora(s, a2p))
    return r, log_decay, k * (1.0 + (a - 1.0) * k_a), v, k * k_k, a


def _conv_fn(u0, u1, u2, u3, w0, w1, w2, w3):
    return (_silu(u0 * w3 + u1 * w2 + u2 * w1 + u3 * w0),)


def _dn_gate_fn(ps2, a_log_n, dt_n):
    beta_n = jax.nn.sigmoid(ps2)
    g_n = -jnp.exp(a_log_n) * _softplus(ps2 + dt_n)
    lane = lax.broadcasted_iota(jnp.int32, (1, DN_HEAD), 1)
    pick = lambda x, j: jnp.broadcast_to(jnp.sum(x * (lane == j).astype(F32), axis=-1, keepdims=True), x.shape)[None]
    beta = jnp.concatenate([pick(beta_n, h) for h in range(DN_HEADS)], axis=0)
    g = jnp.concatenate([pick(g_n, DN_HEADS + h) for h in range(DN_HEADS)], axis=0)
    return beta, g


def _tril_masks(c):
    t = lax.broadcasted_iota(jnp.int32, (c, c), 0)
    s = lax.broadcasted_iota(jnp.int32, (c, c), 1)
    return s <= t, s < t


def _dot(x, y, cx, cy, prec=HIGH):
    nb = x.ndim - 2
    batch = tuple(range(nb))
    return lax.dot_general(x, y, (((cx + nb,), (cy + nb,)), (batch, batch)), precision=prec,
                           preferred_element_type=F32)


def _tri_inv(a):
    c = a.shape[-1]
    eye = (lax.broadcasted_iota(jnp.int32, (c, c), 0) == lax.broadcasted_iota(jnp.int32, (c, c), 1)).astype(F32)
    x = eye - a
    p = a
    n = 2
    while n < c:
        p = _dot(p, p, 1, 0)
        x = x + _dot(x, p, 1, 0)
        n *= 2
    return x


@jax.custom_vjp
def _tri_inv_saved(a, t):
    return t


def _tri_inv_saved_fwd(a, t):
    return t, t


def _tri_inv_saved_bwd(t, dt):
    return -_dot(_dot(t, dt, 0, 0), t, 1, 1), jnp.zeros_like(t)


_tri_inv_saved.defvjp(_tri_inv_saved_fwd, _tri_inv_saved_bwd)


def _pair_masks():
    lane = lax.broadcasted_iota(jnp.int32, (1, 2 * RW_HEAD), 1)
    m0 = (lane < RW_HEAD).astype(F32)
    return m0, 1.0 - m0


def _pair_bd(x):
    m0, m1 = _pair_masks()
    return jnp.concatenate([x * m0, x * m1], axis=-2)


def _pair_mm(x, y):
    return _dot(x, _pair_bd(y), 1, 0)


def _pair_inv(a):
    c = a.shape[-2]
    row = lax.broadcasted_iota(jnp.int32, (c, 2 * RW_HEAD), 0)
    col = lax.broadcasted_iota(jnp.int32, (c, 2 * RW_HEAD), 1) & (RW_HEAD - 1)
    x = (row == col).astype(F32) - a
    p = a
    n = 2
    while n < c:
        p = _pair_mm(p, p)
        x = x + _pair_mm(x, p)
        n *= 2
    return x


@jax.custom_vjp
def _pair_inv_saved(a, t):
    return t


def _pair_inv_saved_fwd(a, t):
    return t, t


def _pair_inv_saved_bwd(t, dt):
    m0, m1 = _pair_masks()
    c = t.shape[-2]
    z = _dot(t, dt, 0, 0)
    x = z[:, :c, :] * m0 + z[:, c:, :] * m1
    return -_dot(x, _pair_bd(t), 1, 1), jnp.zeros_like(t)


_pair_inv_saved.defvjp(_pair_inv_saved_fwd, _pair_inv_saved_bwd)


@jax.custom_vjp
def _use_saved(x, x_saved):
    return x_saved


_use_saved.defvjp(lambda x, x_saved: (x_saved, None), lambda _, g: (g, jnp.zeros_like(g)))


def _rw_chunk_fn(S, r, lw, k2, v, kkp, a, gate, rk, gnw, gnb, saved=None):
    B, C, P = r.shape
    m0, m1 = _pair_masks()
    seg = lambda x: (jnp.sum(x * m0, axis=-1, keepdims=True) * m0 + jnp.sum(x * m1, axis=-1, keepdims=True) * m1)
    mm = lambda x, y: _dot(x.astype(BF16), y.astype(BF16), 1, 0, None)
    nt = lambda x, y: _dot(x, y, 1, 1)
    tn = lambda x, y: _dot(x, y, 0, 0)
    pair_mm = lambda x, y: mm(x, _pair_bd(y))
    t_idx = lax.broadcasted_iota(jnp.int32, (C, P), 0)
    s_idx = lax.broadcasted_iota(jnp.int32, (C, P), 1) & (RW_HEAD - 1)
    incl, strict = s_idx <= t_idx, s_idx < t_idx
    tril = jnp.broadcast_to(_tril_masks(C)[0].astype(F32), (B, C, C))
    kk = kkp * lax.rsqrt(seg(kkp * kkp) + 1e-6)
    b = kk * a
    reuse = (lambda x, i: x) if saved is None else (lambda x, i: _use_saved(x, saved[i]))
    g_incl = reuse(_dot(tril, lw, 1, 0), 1)
    g_excl = g_incl - lw
    inv = jnp.exp(-g_incl)
    alpha, beta, kappa, rho = kk * jnp.exp(g_excl), b * inv, k2 * inv, r * jnp.exp(g_incl)
    ar = jnp.concatenate([alpha, rho], axis=-2)
    scores = reuse(nt(ar, jnp.concatenate([_pair_bd(beta), _pair_bd(kappa)], axis=-2)), 2)
    a_ab = jnp.where(strict, scores[:, :C, :P], 0.0)
    a_ak = jnp.where(strict, scores[:, :C, P:], 0.0)
    r_b = jnp.where(incl, scores[:, C:, :P], 0.0)
    r_k = jnp.where(incl, scores[:, C:, P:], 0.0)
    t_inv = _pair_inv(a_ab) if saved is None else _pair_inv_saved(a_ab, saved[0])
    on_state = reuse(nt(ar, S), 3)
    u = pair_mm(t_inv, -on_state[:, :C, :] - pair_mm(a_ak, v))
    y = on_state[:, C:, :] + mm(jnp.concatenate([r_b, r_k], axis=-1),
                                jnp.concatenate([_pair_bd(u), _pair_bd(v)], axis=-2))
    same_head = ((lax.broadcasted_iota(jnp.int32, (P, P), 0) < RW_HEAD)
                 == (lax.broadcasted_iota(jnp.int32, (P, P), 1) < RW_HEAD))
    fresh = tn(jnp.concatenate([u, v], axis=-2), jnp.concatenate([beta, kappa], axis=-2))
    S_new = jnp.exp(jnp.sum(lw, axis=-2, keepdims=True)) * (S + jnp.where(same_head, fresh, 0.0))
    dev = y - seg(y) * (1.0 / RW_HEAD)
    yn = dev * lax.rsqrt(seg(dev * dev) * (1.0 / RW_HEAD) + RW_GN_EPS) * gnw + gnb
    bonus = seg(r * k2 * rk) * v
    return (yn + bonus) * _silu(gate), S_new, (t_inv, g_incl, scores, on_state)


def _dn_chunk_fn(S, qc, kc, vc, bb, gb, z, nw, saved=None):
    B, C, D = qc.shape
    mm = lambda x, y: _dot(x.astype(BF16), y.astype(BF16), 1, 0, None)
    nt = lambda x, y, p=None: _dot(x, y, 1, 1, p) if p else _dot(x.astype(BF16), y.astype(BF16), 1, 1, None)
    tn = lambda x, y: _dot(x.astype(BF16), y.astype(BF16), 0, 0, None)
    incl, strict = _tril_masks(C)
    q = qc * lax.rsqrt(jnp.sum(qc * qc, axis=-1, keepdims=True) + 1e-6) * (D ** -0.5)
    k = kc * lax.rsqrt(jnp.sum(kc * kc, axis=-1, keepdims=True) + 1e-6)
    kb, vb = k * bb, vc * bb
    G = _dot(jnp.broadcast_to(incl.astype(F32), (B, C, C)), gb, 1, 0, HIGHEST)
    lane = lax.broadcasted_iota(jnp.int32, (C, D), 1)
    e0, e1 = (lane == 0).astype(F32), (lane == 1).astype(F32)
    diff = nt(G * e0 + e1, e0 - G * e1, HIGHEST)
    dmask = jnp.where(incl, jnp.exp(jnp.where(incl, diff, 0.0)), 0.0)
    M = jnp.where(strict, nt(kb, k) * dmask, 0.0)
    T = _tri_inv(M) if saved is None else _tri_inv_saved(M, saved[0])
    eG = jnp.exp(G)
    u = mm(T, vb)
    w = mm(T, kb * eG)
    attn = jnp.where(incl, nt(q, k) * dmask, 0.0)
    v_new = u - mm(w, S)
    o = mm(q * eG, S) + mm(attn, v_new)
    g_last = jnp.sum(gb, axis=-2, keepdims=True)
    S_new = S * jnp.exp(jnp.broadcast_to(g_last, S.shape)) + tn(k * jnp.exp(g_last - G), v_new)
    on = o * lax.rsqrt(jnp.mean(o * o, axis=-1, keepdims=True) + NORM_EPS) * nw
    return on * _silu(z), S_new, (T,)


N_GROUPS = 8
GROUP = 128
HALO = 8


def _groups(x):
    return jnp.concatenate([x[:, g * GROUP:(g + 1) * GROUP][None] for g in range(N_GROUPS)], axis=0)


@functools.partial(jax.custom_vjp, nondiff_argnums=(1,))
def _shift_rows(ext, j):
    return pltpu.roll(ext, j, 0)[HALO:, :]


def _shift_rows_fwd(ext, j):
    return _shift_rows(ext, j), None


def _shift_rows_bwd(j, _, d):
    z = jnp.concatenate([jnp.zeros((HALO, d.shape[1]), d.dtype), d], axis=0)
    return (pltpu.roll(z, z.shape[0] - j, 0),)


_shift_rows.defvjp(_shift_rows_fwd, _shift_rows_bwd)


def _rw_fused_fn(S, h_r, h_k, h_v, h_s, p_r, p_k, p_v, p_s, gate, *pars, saved=None):
    prev = lambda h, x: _shift_rows(jnp.concatenate([h, x], axis=0), 1)
    seq = _rw_prep_fn(p_r, prev(h_r, p_r), p_k, prev(h_k, p_k), p_v, prev(h_v, p_v), p_s, prev(h_s, p_s), *pars[:10])
    return _rw_chunk_fn(S, *[_groups(t) for t in seq], _groups(gate), *[_groups(t) for t in pars[10:]],
                        saved=saved)


def _dn_fused_fn(S, h_q, h_k, h_v, p_q, p_k, p_v, p_s2, z, *pars, saved=None):
    conv = []
    for i, (h, x) in enumerate(((h_q, p_q), (h_k, p_k), (h_v, p_v))):
        ext = jnp.concatenate([h, x], axis=0)
        conv += _conv_fn(x, _shift_rows(ext, 1), _shift_rows(ext, 2), _shift_rows(ext, 3), *pars[4 * i:4 * i + 4])
    beta, g = _dn_gate_fn(p_s2, pars[12], pars[13])
    return _dn_chunk_fn(S, *[_groups(t) for t in conv], beta, g, _groups(z), _groups(pars[14]), saved=saved)


def _chunk_fwd(name, fn, rows, n_halo, pars, saved_shapes):
    Lp = rows[0][0].shape[0]
    C, D, NB = CHUNK, GROUP, N_GROUPS
    nc, n_r, n_p, per = Lp // C, len(rows), len(pars), CHUNK // HALO
    n_s = len(saved_shapes)

    def body(*refs):
        row_refs, halo_refs, par_refs = refs[:n_r], refs[n_r:n_r + n_halo], refs[n_r + n_halo:n_r + n_halo + n_p]
        y_ref, ck_ref = refs[n_r + n_halo + n_p:n_r + n_halo + n_p + 2]
        saved_refs, s_ref = refs[n_r + n_halo + n_p + 2:-1], refs[-1]
        first = pl.program_id(0) == 0

        @pl.when(first)
        def _():
            s_ref[...] = jnp.zeros_like(s_ref)

        S = s_ref[...]
        ck_ref[0] = S
        halos = [jnp.where(first, 0.0, r[...]) for r in halo_refs]
        y, S_new, saved = fn(S, *halos, *[r[...] for r in row_refs], *[r[...] for r in par_refs])
        for g in range(NB):
            y_ref[:, g * D:(g + 1) * D] = y[g].astype(BF16)
        for o_ref, val in zip(saved_refs, saved):
            o_ref[0] = val
        s_ref[...] = S_new

    return pl.pallas_call(
        body, name=name, grid=(nc,),
        in_specs=[pl.BlockSpec((pl.Element(C), pl.Element(w)), lambda c, off=off: (c * C, off)) for (_, w, off) in rows]
        + [pl.BlockSpec((pl.Element(HALO), pl.Element(w)), lambda c, off=off: (pl.multiple_of(jnp.maximum(c * C - HALO, 0), HALO), off))
           for (_, w, off) in rows[:n_halo]]
        + [pl.BlockSpec(p.shape, lambda c: (0, 0)) for p in pars],
        out_specs=[pl.BlockSpec((C, NB * D), lambda c: (c, 0)), pl.BlockSpec((1, NB, D, D), lambda c: (c, 0, 0, 0))]
        + [pl.BlockSpec((1,) + tuple(shp), lambda c: (c, 0, 0, 0)) for shp in saved_shapes],
        out_shape=[jax.ShapeDtypeStruct((Lp, NB * D), BF16), jax.ShapeDtypeStruct((nc, NB, D, D), F32)]
        + [jax.ShapeDtypeStruct((nc,) + tuple(shp), F32) for shp in saved_shapes],
        scratch_shapes=[pltpu.VMEM((NB, D, D), F32)],
        compiler_params=_params(("arbitrary",)),
    )(*[r[0] for r in rows], *[r[0] for r in rows[:n_halo]], *pars)


def _chunk_bwd(name, fn, rows, n_halo, pars, ckpt, saved, dy, out_blocks, into=None):
    Lp = rows[0][0].shape[0]
    C, D, NB = CHUNK, GROUP, N_GROUPS
    nc, n_r, n_p, per = Lp // C, len(rows), len(pars), CHUNK // HALO
    n_s = len(saved)
    n_in = n_r + n_halo + n_p
    width = sum(w for (_, w, _) in rows)
    n_blocks, my_block = out_blocks
    extra = [] if into is None else [into]

    def body(*refs):
        row_refs, halo_refs, par_refs = refs[:n_r], refs[n_r:n_r + n_halo], refs[n_r + n_halo:n_in]
        ck_ref, saved_refs, dy_ref = refs[n_in], refs[n_in + 1:n_in + 1 + n_s], refs[n_in + 1 + n_s]
        outs = refs[n_in + 2 + n_s + len(extra):]
        drows_ref, dpar_refs = outs[0], outs[1:1 + n_p]
        ds_ref, carry_refs = outs[1 + n_p], outs[2 + n_p:]
        i = pl.program_id(0)

        @pl.when(i == 0)
        def _():
            ds_ref[...] = jnp.zeros_like(ds_ref)
            for o_ref in list(dpar_refs) + list(carry_refs):
                o_ref[...] = jnp.zeros_like(o_ref)

        halos = [jnp.where(i == nc - 1, 0.0, r[...]) for r in halo_refs]
        vals = [ck_ref[0]] + halos + [r[...] for r in row_refs] + [r[...] for r in par_refs]
        stored = tuple(r[0] for r in saved_refs)
        grads = jax.vjp(lambda *v: fn(*v, saved=stored)[:2], *vals)[1]((_groups(dy_ref), ds_ref[...]))
        ds_ref[...] = grads[0]
        d_halos, d_rows, d_pars = grads[1:1 + n_halo], grads[1 + n_halo:1 + n_halo + n_r], grads[1 + n_halo + n_r:]
        col = 0
        for k, g in enumerate(d_rows):
            if k < n_halo:
                g = g + jnp.concatenate([jnp.zeros((C - HALO, g.shape[1]), F32), carry_refs[k][...]], axis=0)
                carry_refs[k][...] = d_halos[k]
            drows_ref[:, col:col + g.shape[1]] = g.astype(BF16)
            col += g.shape[1]
        for o_ref, g in zip(dpar_refs, d_pars):
            o_ref[...] += g

    rev = lambda c: nc - 1 - c
    dy_arr, dy_w, dy_cb = dy
    res = pl.pallas_call(
        body, name=name, grid=(nc,),
        in_specs=[pl.BlockSpec((pl.Element(C), pl.Element(w)), lambda c, off=off: (rev(c) * C, off))
                  for (_, w, off) in rows]
        + [pl.BlockSpec((pl.Element(HALO), pl.Element(w)),
                        lambda c, off=off: (pl.multiple_of(jnp.maximum(rev(c) * C - HALO, 0), HALO), off))
           for (_, w, off) in rows[:n_halo]]
        + [pl.BlockSpec(p.shape, lambda c: (0, 0)) for p in pars]
        + [pl.BlockSpec((1, NB, D, D), lambda c: (rev(c), 0, 0, 0))]
        + [pl.BlockSpec((1,) + tuple(t.shape[1:]), lambda c: (rev(c), 0, 0, 0)) for t in saved]
        + [pl.BlockSpec((pl.Element(C), pl.Element(dy_w)), lambda c: (rev(c) * C, dy_cb))]
        + [pl.BlockSpec(memory_space=pl.ANY)] * len(extra),
        out_specs=[pl.BlockSpec((C, width), lambda c: (rev(c), my_block))]
        + [pl.BlockSpec(p.shape, lambda c: (0, 0)) for p in pars],
        out_shape=[jax.ShapeDtypeStruct((Lp, n_blocks * width), BF16)]
        + [jax.ShapeDtypeStruct(p.shape, F32) for p in pars],
        scratch_shapes=[pltpu.VMEM((NB, D, D), F32)] + [pltpu.VMEM((HALO, w), F32) for (_, w, _) in rows[:n_halo]],
        input_output_aliases={} if into is None else {n_in + 2 + n_s: 0},
        compiler_params=_params(("arbitrary",)),
    )(*[r[0] for r in rows], *[r[0] for r in rows[:n_halo]], *pars, ckpt, *saved, dy_arr, *extra)
    return res[0], res[1:]


def _loss_head(h, yw, tgt, fw, n_real, tm=2 * ROW_TILE):
    Lp, Dm = h.shape
    tm = _pick(Lp, tm, 16)

    def out_fn(z, fw_):
        return z * lax.rsqrt(jnp.mean(z * z, axis=-1, keepdims=True) + NORM_EPS) * fw_

    def body(h_ref, yw_ref, t_ref, fw_ref, loss_ref, dz_ref, dz16_ref, dfw_ref):
        i = pl.program_id(0)

        @pl.when(i == 0)
        def _():
            loss_ref[...] = jnp.zeros_like(loss_ref)
            dfw_ref[...] = jnp.zeros_like(dfw_ref)

        row = i * tm + lax.broadcasted_iota(jnp.int32, (tm, 1), 0)
        mask = ((row >= N_META) & (row < n_real)).astype(F32)
        z = h_ref[...] + yw_ref[...]
        o, vjp = jax.vjp(out_fn, z, fw_ref[...])
        err = (o - t_ref[...]) * mask
        row_loss = 0.5 * jnp.mean(jnp.square(err), axis=-1, keepdims=True)
        dz, dfw = vjp(err * (1.0 / Dm))
        loss_ref[...] += jnp.sum(row_loss, axis=0, keepdims=True)
        dz_ref[...] = dz
        dz16_ref[...] = dz.astype(BF16)
        dfw_ref[...] += dfw

    row_spec = pl.BlockSpec((tm, Dm), lambda i: (i, 0))
    return pl.pallas_call(
        body, name="loss_head", grid=(Lp // tm,),
        in_specs=[row_spec, row_spec, row_spec, pl.BlockSpec((1, Dm), lambda i: (0, 0))],
        out_specs=[pl.BlockSpec((8, 128), lambda i: (0, 0)), row_spec, row_spec,
                   pl.BlockSpec((1, Dm), lambda i: (0, 0))],
        out_shape=[jax.ShapeDtypeStruct((8, 128), F32), jax.ShapeDtypeStruct((Lp, Dm), F32),
                   jax.ShapeDtypeStruct((Lp, Dm), BF16), jax.ShapeDtypeStruct((1, Dm), F32)],
        compiler_params=_params(("arbitrary",)),
    )(h, yw, tgt, fw)


def _exchange(name, x, masks, slot_kind, per_dest, n_split=1, copy_own=True, other_half=False):
    n = len(masks)
    keep_own = slot_kind is not None and copy_own
    n_slots = {"chip": 4, "core": 2, "dev": 8, None: n}[slot_kind]
    blk_shape = x.shape[1:] if per_dest else x.shape
    if other_half:
        blk_shape = (x.shape[0] // 2,) + tuple(x.shape[1:])
    rows = blk_shape[0] // n_split

    def body(x_ref, o_ref, send_sems, recv_sems, local_sems):
        mx, my, mc = lax.axis_index("x"), lax.axis_index("y"), lax.axis_index("c")
        if other_half:
            x_ref = x_ref.at[pl.ds((1 - mc) * blk_shape[0], blk_shape[0])]

        def slot(k, px, py, pc):
            return {"chip": 2 * px + py, "core": pc, "dev": 4 * px + 2 * py + pc, None: k}[slot_kind]

        def peer(m):
            return (mx + m[0]) % 2, (my + m[1]) % 2, (mc + m[2]) % 2

        def part(ref, j):
            return ref.at[pl.ds(j * rows, rows)]

        own_src = x_ref.at[2 * mx + my] if per_dest else x_ref
        local = []
        if keep_own:
            own_dst = o_ref.at[slot(0, mx, my, mc)]
            local = [pltpu.make_async_copy(part(own_src, j), part(own_dst, j), local_sems.at[j])
                     for j in range(n_split)]
        for cp in local:
            cp.start()
        sends = []
        for k, m in enumerate(masks):
            px, py, pc = peer(m)
            src = x_ref.at[2 * px + py] if per_dest else x_ref
            dst = o_ref.at[slot(k, mx, my, mc)]
            for j in range(n_split):
                sends.append(pltpu.make_async_remote_copy(
                    src_ref=part(src, j), dst_ref=part(dst, j), send_sem=send_sems.at[k * n_split + j],
                    recv_sem=recv_sems.at[k * n_split + j], device_id=(px, py, pc), device_id_type=MESH))
        for cp in sends:
            cp.start()
        for k, m in enumerate(masks):
            px, py, pc = peer(m)
            landed = o_ref.at[slot(k, px, py, pc)]
            for j in range(n_split):
                pltpu.make_async_remote_copy(
                    src_ref=part(own_src, j), dst_ref=part(landed, j), send_sem=send_sems.at[k * n_split + j],
                    recv_sem=recv_sems.at[k * n_split + j], device_id=(px, py, pc), device_id_type=MESH).wait_recv()
        for cp in sends:
            cp.wait_send()
        for cp in local:
            cp.wait()

    return pl.pallas_call(
        body, name=name,
        in_specs=[pl.BlockSpec(memory_space=pl.ANY)], out_specs=pl.BlockSpec(memory_space=pl.ANY),
        out_shape=jax.ShapeDtypeStruct((n_slots,) + tuple(blk_shape), x.dtype),
        scratch_shapes=[pltpu.SemaphoreType.DMA((n * n_split,)), pltpu.SemaphoreType.DMA((n * n_split,)),
                        pltpu.SemaphoreType.DMA((n_split,))],
        compiler_params=pltpu.CompilerParams(has_side_effects=True),
    )(x)


CHIP_MASKS = [(1, 0, 0), (0, 1, 0), (1, 1, 0)]
CORE_MASKS = [(0, 0, 1)]
ALL_MASKS = [(dx, dy, dc) for dx in (0, 1) for dy in (0, 1) for dc in (0, 1) if (dx, dy, dc) != (0, 0, 0)]

HBM_SPEC = pl.BlockSpec(memory_space=pltpu.HBM)
SEM_SPEC = pl.BlockSpec(memory_space=pltpu.SEMAPHORE)
DATAFLOW = pltpu.SideEffectType.DATAFLOW_SIDE_EFFECTING


def _split_copies(x_ref, land_ref, send_sems, recv_sems, masks, slot_kind, per_dest, n_split, with_recvs=True):
    mx, my, mc = lax.axis_index("x"), lax.axis_index("y"), lax.axis_index("c")
    slot = lambda px, py, pc: {"chip": 2 * px + py, "core": pc}[slot_kind]
    rows = (x_ref.shape[1] if per_dest else x_ref.shape[0]) // n_split
    part = lambda ref, j: ref.at[pl.ds(j * rows, rows)]
    sends, recvs = [], []
    for k, m in enumerate(masks):
        px, py, pc = (mx + m[0]) % 2, (my + m[1]) % 2, (mc + m[2]) % 2
        src = x_ref.at[2 * px + py] if per_dest else x_ref
        own_src = x_ref.at[2 * mx + my] if per_dest else x_ref
        for j in range(n_split):
            sems = dict(send_sem=send_sems.at[k * n_split + j], recv_sem=recv_sems.at[k * n_split + j],
                        device_id=(px, py, pc), device_id_type=MESH)
            sends.append(pltpu.make_async_remote_copy(
                src_ref=part(src, j), dst_ref=part(land_ref.at[slot(mx, my, mc)], j), **sems))
            if with_recvs:
                recvs.append(pltpu.make_async_remote_copy(
                    src_ref=part(own_src, j), dst_ref=part(land_ref.at[slot(px, py, pc)], j), **sems))
    return sends, recvs


def _exchange_start(name, x, masks, slot_kind, per_dest, n_split, dep=None):
    n = len(masks) * n_split
    blk_shape = x.shape[1:] if per_dest else x.shape
    land_shape = ({"chip": 4, "core": 2}[slot_kind],) + tuple(blk_shape)
    deps = [] if dep is None else [dep]

    def body(x_ref, land_ref, *rest):
        send_sems, recv_sems, x_thru, land_thru, token = rest[len(deps):]
        for cp in _split_copies(x_ref, land_ref, send_sems, recv_sems, masks, slot_kind, per_dest, n_split, False)[0]:
            cp.start()
        token[...] = jnp.zeros_like(token)

    return pl.pallas_call(
        body, name=name,
        out_shape=(pltpu.SemaphoreType.DMA((n,)), pltpu.SemaphoreType.DMA((n,)), pltpu.HBM(x.shape, x.dtype),
                   pltpu.HBM(land_shape, x.dtype), jax.ShapeDtypeStruct((8, 128), F32)),
        in_specs=(HBM_SPEC, HBM_SPEC) + (pl.BlockSpec(memory_space=pl.ANY),) * len(deps),
        out_specs=(SEM_SPEC, SEM_SPEC, HBM_SPEC, HBM_SPEC, pl.BlockSpec(memory_space=pltpu.VMEM)),
        input_output_aliases={0: 2, 1: 3},
        compiler_params=pltpu.CompilerParams(has_side_effects=DATAFLOW),
    )(pltpu.with_memory_space_constraint(x, pltpu.HBM),
      pltpu.with_memory_space_constraint(lax.empty(land_shape, x.dtype), pltpu.HBM), *deps)


def _exchange_wait(name, started, after, masks, slot_kind, per_dest, n_split):
    send_sems, recv_sems, x_thru, land_thru, _ = started

    def body(x_ref, land_ref, send_sems, recv_sems, after_ref, x_out, land_out):
        sends, recvs = _split_copies(x_ref, land_ref, send_sems, recv_sems, masks, slot_kind, per_dest, n_split)
        for cp in sends:
            cp.wait_send()
        for cp in recvs:
            cp.wait_recv()

    return pl.pallas_call(
        body, name=name,
        out_shape=(pltpu.HBM(x_thru.shape, x_thru.dtype), pltpu.HBM(land_thru.shape, land_thru.dtype)),
        in_specs=(HBM_SPEC, HBM_SPEC, SEM_SPEC, SEM_SPEC, pl.BlockSpec(memory_space=pl.ANY)),
        out_specs=(HBM_SPEC, HBM_SPEC), input_output_aliases={0: 0, 1: 1},
        compiler_params=pltpu.CompilerParams(has_side_effects=DATAFLOW),
    )(x_thru, land_thru, send_sems, recv_sems, after)


def _gather_chips(name, x):
    return _exchange(name, x, CHIP_MASKS, "chip", False)


def _gather_shards(name, shard, n_split, overlap=False, dep=None):
    half = shard.shape[0] // 2
    mine = lax.dynamic_slice_in_dim(shard, lax.axis_index("c") * half, half, axis=0)
    if overlap:
        return _exchange_start(name + "_chips", mine, CHIP_MASKS, "chip", False, n_split, dep)
    by_chip = _exchange(name + "_chips", mine, CHIP_MASKS, "chip", False, n_split, copy_own=False)
    return _gather_tail(name, mine, by_chip)


def _gather_tail(name, mine, by_chip):
    c, chip = lax.axis_index("c"), 2 * lax.axis_index("x") + lax.axis_index("y")
    by_chip = lax.dynamic_update_index_in_dim(by_chip, mine, chip, 0)
    both = _exchange(name + "_cores", by_chip, CORE_MASKS, "core", False, N_CHIPS, copy_own=False)
    return lax.dynamic_update_index_in_dim(both, by_chip, c, 0)


def _gather_finish(name, started, after, n_split):
    mine, by_chip = _exchange_wait(name + "_chips_wait", started, after, CHIP_MASKS, "chip", False, n_split)
    return _gather_tail(name, mine, by_chip)


def _sum_slots(name, x, tr=128):
    S, R, N = x.shape
    tr = _pick(R, tr, 16)

    def body(x_ref, o_ref):
        acc = x_ref[0].astype(F32)
        for s in range(1, S):
            acc = acc + x_ref[s].astype(F32)
        o_ref[...] = acc

    return pl.pallas_call(
        body, name=name, grid=(R // tr,),
        in_specs=[pl.BlockSpec((S, tr, N), lambda i: (0, i, 0))], out_specs=pl.BlockSpec((tr, N), lambda i: (i, 0)),
        out_shape=jax.ShapeDtypeStruct((R, N), F32), compiler_params=_params(("parallel",)),
    )(x)


def _add_to_bf16(name, a, b, tr=128):
    S, R, N = a.shape
    tr = _pick(R, tr, 16)

    def body(a_ref, b_ref, o_ref):
        o_ref[...] = (a_ref[...] + b_ref[...]).astype(BF16)

    spec = pl.BlockSpec((S, tr, N), lambda i: (0, i, 0))
    return pl.pallas_call(
        body, name=name, grid=(R // tr,), in_specs=[spec, spec], out_specs=spec,
        out_shape=jax.ShapeDtypeStruct((S, R, N), BF16), compiler_params=_params(("parallel",)),
    )(a, b)


def _add_slabs_to_bf16(name, full, recv, c, tr=64):
    R, NP = full.shape
    half = R // 2
    nb = half // tr

    def body(c_ref, a_ref, b_ref, o_ref):
        x = a_ref[...] + b_ref[...]
        for s in range(N_CHIPS - 1):
            o_ref[s] = x[:, s * SHARD_COLS:(s + 1) * SHARD_COLS].astype(BF16)
        last = jnp.concatenate([x[:, (N_CHIPS - 1) * SHARD_COLS:GAP_AT], x[:, GAP_AT + GAP:]], axis=1)
        o_ref[N_CHIPS - 1] = last.astype(BF16)

    grid_spec = pltpu.PrefetchScalarGridSpec(
        num_scalar_prefetch=1, grid=(nb,),
        in_specs=[pl.BlockSpec((tr, NP), lambda i, c_ref: (c_ref[0] * nb + i, 0)),
                  pl.BlockSpec((tr, NP), lambda i, c_ref: (i, 0))],
        out_specs=pl.BlockSpec((N_CHIPS, tr, SHARD_COLS), lambda i, c_ref: (0, i, 0)))
    return pl.pallas_call(
        body, name=name, grid_spec=grid_spec,
        out_shape=jax.ShapeDtypeStruct((N_CHIPS, half, SHARD_COLS), BF16), compiler_params=_params(("parallel",)),
    )(jnp.reshape(c, (1,)).astype(jnp.int32), full, recv)


def _adamw(name, gparts, w, m, v, tr=128):
    S, R, N = gparts.shape
    tr = _pick(R, tr)
    c1 = 1.0 / (1.0 - ADAM_B1 ** ADAM_STEP)
    c2 = 1.0 / (1.0 - ADAM_B2 ** ADAM_STEP)

    def body(g_ref, w_ref, m_ref, v_ref, go_ref, d_ref, mo_ref, vo_ref):
        g = g_ref[0]
        for s in range(1, S):
            g = g + g_ref[s]
        m_new = ADAM_B1 * m_ref[...] + (1.0 - ADAM_B1) * g
        v_new = ADAM_B2 * v_ref[...] + (1.0 - ADAM_B2) * jnp.square(g)
        go_ref[...] = g
        mo_ref[...] = m_new
        vo_ref[...] = v_new
        d_ref[...] = -ADAM_LR * ((m_new * c1) / (jnp.sqrt(v_new * c2) + ADAM_EPS) + ADAM_WD * w_ref[...])

    spec = pl.BlockSpec((tr, N), lambda i: (i, 0))
    return pl.pallas_call(
        body, name=name, grid=(R // tr,),
        in_specs=[pl.BlockSpec((S, tr, N), lambda i: (0, i, 0)), spec, spec, spec], out_specs=[spec] * 4,
        out_shape=[jax.ShapeDtypeStruct((R, N), F32)] * 4, compiler_params=_params(("parallel",)),
    )(gparts, w, m, v)


def _reduce_to_shard(name, slabs, n_split, by_columns=False, overlap=False):
    c, chip = lax.axis_index("c"), 2 * lax.axis_index("x") + lax.axis_index("y")
    if by_columns:
        R, N = slabs.shape[0], SHARD_COLS
        from_sibling = _exchange(name + "_sib", slabs, CORE_MASKS, None, False, n_split, other_half=True)[0]
        wire = _add_slabs_to_bf16(name + "_add", slabs, from_sibling, c)
    else:
        _, R, N = slabs.shape
        half = R // 2
        halves = slabs.reshape(N_CHIPS, 2, half, N)
        mine = lax.dynamic_index_in_dim(halves, c, axis=1, keepdims=False)
        theirs = lax.dynamic_index_in_dim(halves, 1 - c, axis=1, keepdims=False)
        from_sibling = _exchange(name + "_sib", theirs, CORE_MASKS, None, False, N_CHIPS)[0]
        wire = _add_to_bf16(name + "_add", mine, from_sibling)
    if overlap:
        return _exchange_start(name + "_chips", wire, CHIP_MASKS, "chip", True, n_split), (R, N)
    got = _exchange(name + "_chips", wire, CHIP_MASKS, "chip", True, n_split, copy_own=False)
    return _reduce_tail(name, wire, got, n_split, R, N)


def _reduce_tail(name, wire, got, n_split, R, N):
    c, chip = lax.axis_index("c"), 2 * lax.axis_index("x") + lax.axis_index("y")
    got = lax.dynamic_update_index_in_dim(got, lax.dynamic_index_in_dim(wire, chip, 0, keepdims=False), chip, 0)
    part = _sum_slots(name + "_sum", got)
    both = _exchange(name + "_cores", part, CORE_MASKS, "core", False, n_split, copy_own=False)
    return lax.dynamic_update_index_in_dim(both, part, c, 0).reshape(1, R, N)


def _reduce_finish(name, started, after, n_split):
    handle, (R, N) = started
    wire, got = _exchange_wait(name + "_chips_wait", handle, after, CHIP_MASKS, "chip", True, n_split)
    return _reduce_tail(name, wire, got, n_split, R, N)


def _pack(pieces, cols, row_mult=8):
    flat = jnp.concatenate([p.reshape(-1) for p in pieces])
    rows = -(-flat.shape[0] // cols)
    rows = -(-rows // row_mult) * row_mult
    return jnp.pad(flat, (0, rows * cols - flat.shape[0])).reshape(rows, cols)


def _unpack(packed, shapes):
    flat = packed.reshape(-1)
    out, off = [], 0
    for shp in shapes:
        n = 1
        for d in shp:
            n *= d
        out.append(flat[off:off + n].reshape(shp))
        off += n
    return out


def kernel(x, meta_tokens, norm_w, w_in, rw_shift_mu, rw_w0, rw_w2, rw_a0, rw_a2, rw_k_k, rw_k_a, rw_r_k, rw_gn_w, rw_gn_b, dn_conv_w, dn_A_log, dn_dt_bias, dn_norm_w, w_out, final_norm_w, loss_target, m_meta_tokens, m_norm_w, m_w_in, m_rw_shift_mu, m_rw_w0, m_rw_w2, m_rw_a0, m_rw_a2, m_rw_k_k, m_rw_k_a, m_rw_r_k, m_rw_gn_w, m_rw_gn_b, m_dn_conv_w, m_dn_A_log, m_dn_dt_bias, m_dn_norm_w, m_w_out, m_final_norm_w, v_meta_tokens, v_norm_w, v_w_in, v_rw_shift_mu, v_rw_w0, v_rw_w2, v_rw_a0, v_rw_a2, v_rw_k_k, v_rw_k_a, v_rw_r_k, v_rw_gn_w, v_rw_gn_b, v_dn_conv_w, v_dn_A_log, v_dn_dt_bias, v_dn_norm_w, v_w_out, v_final_norm_w):
    S = x.shape[1]
    L = N_META + S
    Lp = -(-L // CHUNK) * CHUNK

    small_shapes = [(RW_LORA, 256), (RW_LORA, 256), (CONV_W, 768), (N_META, 512)]
    small_mine = _pack([rw_w2[0], rw_a2[0], dn_conv_w[0], meta_tokens], 1024)
    small_all = _gather_chips("gather_small", small_mine)
    per_chip = [_unpack(small_all[s], small_shapes) for s in range(N_CHIPS)]
    w2, a2, conv_w, meta = [jnp.concatenate([per_chip[s][i] for s in range(N_CHIPS)], axis=1) for i in range(4)]
    w_in_started = _gather_shards("gather_w_in", w_in[0].astype(BF16), 8, overlap=True, dep=small_all)
    w_out_started = _gather_shards("gather_w_out", w_out[0].astype(BF16), 8, overlap=True, dep=w_in_started[4])

    tail = [jnp.zeros((Lp - L, D_MODEL), F32)] if Lp > L else []
    h = jnp.concatenate([meta + w_in_started[4][:1, :1], x[0]] + tail, axis=0)
    tgt = jnp.concatenate([jnp.zeros((N_META, D_MODEL), F32), loss_target[0]] + tail, axis=0)
    (u,) = _rowwise("rms_in", _rms_fn, [_row(h)], [norm_w], [D_MODEL], tm=2 * ROW_TILE, out_dtype=BF16)
    w_in_all = _gather_finish("gather_w_in", w_in_started, u, 8)
    W_orig = jnp.transpose(w_in_all, (0, 2, 1, 3)).reshape(D_MODEL, IN_COLS)
    W_gapped = jnp.concatenate([W_orig[:, :GAP_AT], jnp.zeros((D_MODEL, GAP), BF16), W_orig[:, GAP_AT:]], axis=1)
    p = _mm("in_proj", u, W_gapped, "nn", dep=w_out_started[4])

    mu = rw_shift_mu
    zpad = jnp.zeros((RW_LORA, RW_WIDTH), F32)
    rw_params = [mu[:, 0:1024], mu[:, 1024:2048], mu[:, 2048:3072], mu[:, 3072:3200], rw_w0,
                 jnp.concatenate([w2, zpad], axis=0), rw_a0, jnp.concatenate([zpad, a2], axis=0), rw_k_k, rw_k_a]
    rw_rows = [_row(p, 1024, OFF_R), _row(p, 1024, OFF_K), _row(p, 1024, OFF_V), _row(p, 128, OFF_S1),
               _row(p, 1024, OFF_GATE)]
    rw_pars = rw_params + [rw_r_k, rw_gn_w, rw_gn_b]
    rw_saved_shapes = [(N_GROUPS, CHUNK, GROUP), (N_GROUPS, CHUNK, GROUP), (N_GROUPS, 2 * CHUNK, 2 * GROUP),
                       (N_GROUPS, 2 * CHUNK, GROUP)]
    ya, rw_ck, *rw_t = _chunk_fwd("rw_chunk_fwd", _rw_fused_fn, rw_rows, 4, rw_pars, rw_saved_shapes)

    dn_rows = [_row(p, 1024, OFF_DQ), _row(p, 1024, OFF_DK), _row(p, 1024, OFF_DV), _row(p, 128, OFF_S2),
               _row(p, 1024, OFF_Z)]
    dn_pars = [conv_w[j:j + 1, 1024 * i:1024 * (i + 1)] for i in range(3) for j in range(CONV_W)]
    narrow = lambda t: jnp.pad(t, ((0, 0), (DN_HEADS, DN_HEAD - 2 * DN_HEADS)))
    dn_pars += [narrow(dn_A_log), narrow(dn_dt_bias), jnp.tile(dn_norm_w, (1, DN_HEADS))]
    yb, dn_ck, *dn_t = _chunk_fwd("dn_chunk_fwd", _dn_fused_fn, dn_rows, 3, dn_pars, [(N_GROUPS, CHUNK, CHUNK)])

    y = jnp.concatenate([ya, yb], axis=1)
    w_out_all = _gather_finish("gather_w_out", w_out_started, y, 8)
    Wo = jnp.transpose(w_out_all, (1, 0, 2, 3)).reshape(D_MODEL, D_MODEL)
    yw = _mm("out_proj", y, Wo, "nn", tn=1024)
    loss_acc, dz, dz16, d_fw = _loss_head(h, yw, tgt, final_norm_w.reshape(1, D_MODEL), L)
    loss = lax.psum(loss_acc[0, 0], ("x", "y", "c"))

    d_wo = _mm("d_w_out", y.T, dz16, "nn", tm=512, tn=1024, tk=Lp)
    wo_started = _reduce_to_shard("rs_w_out", d_wo.reshape(N_CHIPS, D_MODEL // N_CHIPS, D_MODEL), 8, overlap=True)
    dy = _mm("d_out_proj", dz16, Wo, "nt", tn=1024, tk=2048, dep=wo_started[0][4])

    dp, d_rw_pars = _chunk_bwd("rw_chunk_bwd", _rw_fused_fn, rw_rows, 4, rw_pars, rw_ck, rw_t, _row(dy, 1024, 0),
                               (2, 0))
    d_prep_pars, d_rw_pars = d_rw_pars[:10], d_rw_pars[10:]
    dp, d_dn_pars = _chunk_bwd("dn_chunk_bwd", _dn_fused_fn, dn_rows, 3, dn_pars, dn_ck, dn_t, _row(dy, 1024, RW_WIDTH),
                               (2, 1), into=dp)
    d_conv_parts = [jnp.concatenate(d_dn_pars[4 * i:4 * i + 4], axis=0) for i in range(3)]
    d_a_log_b, d_dt_b = d_dn_pars[12], d_dn_pars[13]
    d_dn_nw = jnp.sum(d_dn_pars[14].reshape(DN_HEADS, DN_HEAD), axis=0, keepdims=True)
    d_W = _mm("d_w_in", u.T, dp, "nn", tm=1024, tn=768, tk=Lp)
    wi_started = _reduce_to_shard("rs_w_in", d_W, 8, by_columns=True, overlap=True)
    du = _mm("d_in_proj", dp, W_gapped, "nt", tn=2048, tk=1408, dep=wi_started[0][4])
    (dh,), (d_norm_w,) = _rowwise_bwd("rms_in_bwd", lambda h_, w_: (_rms_fn(h_, w_)[0], h_), [_row(h)], [norm_w],
                                      [_row(du), _row(dz)])
    grad_x = dh[N_META:L][None]

    d_mu = jnp.concatenate(d_prep_pars[0:4], axis=1)
    d_w2, d_a2 = d_prep_pars[5][:RW_LORA], d_prep_pars[7][RW_LORA:]
    d_conv = jnp.concatenate(d_conv_parts, axis=1)
    d_meta = dh[:N_META]
    head_sum = lambda t: t[:, DN_HEADS:2 * DN_HEADS]
    rep_names = ["norm_w", "rw_shift_mu", "rw_w0", "rw_a0", "rw_k_k", "rw_k_a", "rw_r_k", "rw_gn_w", "rw_gn_b",
                 "dn_A_log", "dn_dt_bias", "dn_norm_w", "final_norm_w"]
    rep_g = [d_norm_w, d_mu, d_prep_pars[4], d_prep_pars[6], d_prep_pars[8], d_prep_pars[9],
             d_rw_pars[0], d_rw_pars[1], d_rw_pars[2],
             head_sum(d_a_log_b), head_sum(d_dt_b), d_dn_nw, d_fw.reshape(D_MODEL)]
    rep_w = [norm_w, rw_shift_mu, rw_w0, rw_a0, rw_k_k, rw_k_a, rw_r_k, rw_gn_w, rw_gn_b, dn_A_log, dn_dt_bias,
             dn_norm_w, final_norm_w]
    rep_m = [m_norm_w, m_rw_shift_mu, m_rw_w0, m_rw_a0, m_rw_k_k, m_rw_k_a, m_rw_r_k, m_rw_gn_w, m_rw_gn_b,
             m_dn_A_log, m_dn_dt_bias, m_dn_norm_w, m_final_norm_w]
    rep_v = [v_norm_w, v_rw_shift_mu, v_rw_w0, v_rw_a0, v_rw_k_k, v_rw_k_a, v_rw_r_k, v_rw_gn_w, v_rw_gn_b,
             v_dn_A_log, v_dn_dt_bias, v_dn_norm_w, v_final_norm_w]
    rep_shapes = [t.shape for t in rep_w]
    rep_all = _exchange("gather_rep_grads", _pack(rep_g, 128), ALL_MASKS, "dev", False)
    rep_out = _adamw("adam_rep", rep_all, _pack(rep_w, 128), _pack(rep_m, 128), _pack(rep_v, 128))
    rep_out = [dict(zip(rep_names, _unpack(t, rep_shapes))) for t in rep_out]

    sm_slabs = jnp.stack([_pack([d_w2[:, 256 * s:256 * (s + 1)], d_a2[:, 256 * s:256 * (s + 1)],
                                 d_conv[:, 768 * s:768 * (s + 1)], d_meta[:, 512 * s:512 * (s + 1)]], 1024, 64)
                          for s in range(N_CHIPS)])
    sm_parts = _reduce_to_shard("rs_small", sm_slabs, 1)
    sm_w = [rw_w2[0], rw_a2[0], dn_conv_w[0], meta_tokens]
    sm_m = [m_rw_w2[0], m_rw_a2[0], m_dn_conv_w[0], m_meta_tokens]
    sm_v = [v_rw_w2[0], v_rw_a2[0], v_dn_conv_w[0], v_meta_tokens]
    sm_out = _adamw("adam_small", sm_parts, _pack(sm_w, 1024, 64), _pack(sm_m, 1024, 64), _pack(sm_v, 1024, 64))
    sm_names = ["rw_w2", "rw_a2", "dn_conv_w", "meta_tokens"]
    sm_full_shapes = [(1, RW_LORA, 256), (1, RW_LORA, 256), (1, CONV_W, 768), (N_META, 512)]
    sm_out = [dict(zip(sm_names, [t.reshape(shp) for t, shp in zip(_unpack(o, small_shapes), sm_full_shapes)]))
              for o in sm_out]

    wo_parts = _reduce_finish("rs_w_out", wo_started, dp, 8)
    wo_out = _adamw("adam_w_out", wo_parts, w_out[0], m_w_out[0], v_w_out[0])
    wi_parts = _reduce_finish("rs_w_in", wi_started, dh, 8)
    wi_out = _adamw("adam_w_in", wi_parts, w_in[0], m_w_in[0], v_w_in[0])

    order = ["meta_tokens", "norm_w", "w_in", "rw_shift_mu", "rw_w0", "rw_w2", "rw_a0", "rw_a2", "rw_k_k", "rw_k_a",
             "rw_r_k", "rw_gn_w", "rw_gn_b", "dn_conv_w", "dn_A_log", "dn_dt_bias", "dn_norm_w", "w_out",
             "final_norm_w"]
    outs = [loss, grad_x]
    for kind in range(4):
        table = dict(rep_out[kind])
        table.update(sm_out[kind])
        table["w_in"] = wi_out[kind][None]
        table["w_out"] = wo_out[kind][None]
        outs += [table[n] for n in order]
    return tuple(outs)
```

```python
import functools

import jax
import jax.numpy as jnp
from jax import lax
from jax.experimental import pallas as pl
from jax.experimental.pallas import tpu as pltpu

F32 = jnp.float32
BF16 = jnp.bfloat16
HIGH = lax.Precision.HIGH
HIGHEST = lax.Precision.HIGHEST
MESH = pl.DeviceIdType.MESH

D_MODEL = 2048
N_META = 16
RW_WIDTH = 1024
RW_HEAD = 64
RW_HEADS = 16
RW_LORA = 64
RW_GN_EPS = 64e-5
DN_WIDTH = 1024
DN_HEAD = 128
DN_HEADS = 8
CONV_W = 4
CHUNK = 64
NORM_EPS = 1e-6
IN_COLS = 8336
N_CHIPS = 4
SHARD_COLS = IN_COLS // N_CHIPS

NP_COLS = 8 * 1024 + 256
GAP_AT = IN_COLS - DN_WIDTH
GAP = NP_COLS - IN_COLS
OFF_R, OFF_K, OFF_V, OFF_S1, OFF_GATE = 0, 1024, 2048, 3072, 3200
OFF_DQ, OFF_DK, OFF_DV, OFF_S2, OFF_Z = 4224, 5248, 6272, 7296, 7424

ADAM_LR = 0.001
ADAM_B1 = 0.9
ADAM_B2 = 0.999
ADAM_EPS = 1e-08
ADAM_WD = 0.01
ADAM_STEP = 10

VMEM_LIMIT_BYTES = 56 * 1024 * 1024
ROW_TILE = 104
MM_ROW_TILE = 832


def _params(sem=None):
    return pltpu.CompilerParams(dimension_semantics=sem, vmem_limit_bytes=VMEM_LIMIT_BYTES)


def _pick(n, target, mult=8):
    best = None
    for d in range(mult, min(n, target) + 1, mult):
        if n % d == 0:
            best = d
    return n if best is None else best


def _mm(name, a, b, mode, tm=MM_ROW_TILE, tn=1408, tk=2048, dep=None):
    if mode == "nn":
        (M, K), (_, N) = a.shape, b.shape
    elif mode == "nt":
        (M, K), (N, _) = a.shape, b.shape
    else:
        (K, M), (_, N) = a.shape, b.shape
    tm = _pick(M, tm, 128 if mode == "tn" else 16)
    tn = _pick(N, tn, 128)
    tk = _pick(K, tk, 8 if mode == "tn" else 128)
    if mode == "nn":
        a_spec = pl.BlockSpec((tm, tk), lambda i, j, k: (i, k))
        b_spec = pl.BlockSpec((tk, tn), lambda i, j, k: (k, j))
        dims = (((1,), (0,)), ((), ()))
    elif mode == "nt":
        a_spec = pl.BlockSpec((tm, tk), lambda i, j, k: (i, k))
        b_spec = pl.BlockSpec((tn, tk), lambda i, j, k: (j, k))
        dims = (((1,), (1,)), ((), ()))
    else:
        a_spec = pl.BlockSpec((tk, tm), lambda i, j, k: (k, i))
        b_spec = pl.BlockSpec((tk, tn), lambda i, j, k: (k, j))
        dims = (((0,), (0,)), ((), ()))

    def body(a_ref, b_ref, *rest):
        o_ref = rest[-1]

        @pl.when(pl.program_id(2) == 0)
        def _():
            o_ref[...] = jnp.zeros_like(o_ref)

        o_ref[...] += lax.dot_general(a_ref[...].astype(BF16), b_ref[...].astype(BF16), dims,
                                      preferred_element_type=F32)

    deps = [] if dep is None else [dep]
    return pl.pallas_call(
        body, name=name, grid=(M // tm, N // tn, K // tk),
        in_specs=[a_spec, b_spec] + [pl.BlockSpec((8, 128), lambda i, j, k: (0, 0))] * len(deps),
        out_specs=pl.BlockSpec((tm, tn), lambda i, j, k: (i, j)),
        out_shape=jax.ShapeDtypeStruct((M, N), F32),
        compiler_params=_params(("parallel", "parallel", "arbitrary")),
    )(a, b, *deps)


def _row(arr, width=None, cb=0):
    return (arr, arr.shape[1] if width is None else width, cb)


def _rowwise(name, fn, rows, params, out_widths, tm=ROW_TILE, out_dtype=F32):
    R = rows[0][0].shape[0]
    tm = _pick(R, tm, 16 if out_dtype == BF16 else 8)
    n_r, n_p = len(rows), len(params)

    def body(*refs):
        vals = [r[...] for r in refs[:n_r + n_p]]
        for o_ref, val in zip(refs[n_r + n_p:], fn(*vals)):
            o_ref[...] = val.astype(out_dtype)

    in_specs = [pl.BlockSpec((tm, w), lambda i, cb=cb: (i, cb)) for (_, w, cb) in rows]
    in_specs += [pl.BlockSpec(p.shape, lambda i: (0, 0)) for p in params]
    return pl.pallas_call(
        body, name=name, grid=(R // tm,), in_specs=in_specs,
        out_specs=[pl.BlockSpec((tm, w), lambda i: (i, 0)) for w in out_widths],
        out_shape=[jax.ShapeDtypeStruct((R, w), out_dtype) for w in out_widths],
        compiler_params=_params(("parallel",)),
    )(*[r[0] for r in rows], *params)


def _rowwise_bwd(name, fn, rows, params, douts, tm=ROW_TILE):
    R = rows[0][0].shape[0]
    tm = _pick(R, tm)
    n_r, n_p, n_d = len(rows), len(params), len(douts)

    def body(*refs):
        vals = [r[...] for r in refs[:n_r + n_p]]
        cts = tuple(r[...] for r in refs[n_r + n_p:n_r + n_p + n_d])
        grads = jax.vjp(fn, *vals)[1](cts)
        outs = refs[n_r + n_p + n_d:]
        for o_ref, g in zip(outs[:n_r], grads[:n_r]):
            o_ref[...] = g

        @pl.when(pl.program_id(0) == 0)
        def _():
            for o_ref in outs[n_r:]:
                o_ref[...] = jnp.zeros_like(o_ref)

        for o_ref, g in zip(outs[n_r:], grads[n_r:]):
            o_ref[...] += g

    in_specs = [pl.BlockSpec((tm, w), lambda i, cb=cb: (i, cb)) for (_, w, cb) in rows]
    in_specs += [pl.BlockSpec(p.shape, lambda i: (0, 0)) for p in params]
    in_specs += [pl.BlockSpec((tm, w), lambda i, cb=cb: (i, cb)) for (_, w, cb) in douts]
    out_specs = [pl.BlockSpec((tm, w), lambda i: (i, 0)) for (_, w, _) in rows]
    out_specs += [pl.BlockSpec(p.shape, lambda i: (0, 0)) for p in params]
    out_shape = [jax.ShapeDtypeStruct((R, w), F32) for (_, w, _) in rows]
    out_shape += [jax.ShapeDtypeStruct(p.shape, F32) for p in params]
    res = pl.pallas_call(
        body, name=name, grid=(R // tm,), in_specs=in_specs, out_specs=out_specs, out_shape=out_shape,
        compiler_params=_params(("arbitrary",)),
    )(*[r[0] for r in rows], *params, *[d[0] for d in douts])
    return res[:n_r], res[n_r:]


def _softplus(x):
    return jnp.maximum(x, 0.0) + jnp.log(1.0 + jnp.exp(-jnp.abs(x)))


def _silu(x):
    return x * jax.nn.sigmoid(x)


def _rms_fn(h, w):
    return (h * lax.rsqrt(jnp.mean(h * h, axis=-1, keepdims=True) + NORM_EPS) * w,)


def _rw_prep_fn(pr, pr1, pk, pk1, pv, pv1, ps, ps1, mu_r, mu_k, mu_v, mu_s, w0, w2p, a0, a2p, k_k, k_a):
    r = pr + (pr1 - pr) * mu_r
    k = pk + (pk1 - pk) * mu_k
    v = pv + (pv1 - pv) * mu_v
    s = ps + (ps1 - ps) * mu_s
    lora = lambda x, w: jnp.dot(x.astype(BF16), w.astype(BF16), preferred_element_type=F32)
    w_log = -_softplus(-(w0 + lora(jnp.tanh(s), w2p))) - 0.5
    log_decay = -jnp.exp(w_log)
    a = jax.nn.sigmoid(a0 + lora(s, a2p))
    return r, log_decay, k * (1.0 + (a - 1.0) * k_a), v, k * k_k, a


def _conv_fn(u0, u1, u2, u3, w0, w1, w2, w3):
    return (_silu(u0 * w3 + u1 * w2 + u2 * w1 + u3 * w0),)


def _dn_gate_fn(ps2, a_log_n, dt_n):
    beta_n = jax.nn.sigmoid(ps2)
    g_n = -jnp.exp(a_log_n) * _softplus(ps2 + dt_n)
    lane = lax.broadcasted_iota(jnp.int32, (1, DN_HEAD), 1)
    pick = lambda x, j: jnp.broadcast_to(jnp.sum(x * (lane == j).astype(F32), axis=-1, keepdims=True), x.shape)[None]
    beta = jnp.concatenate([pick(beta_n, h) for h in range(DN_HEADS)], axis=0)
    g = jnp.concatenate([pick(g_n, DN_HEADS + h) for h in range(DN_HEADS)], axis=0)
    return beta, g


def _tril_masks(c):
    t = lax.broadcasted_iota(jnp.int32, (c, c), 0)
    s = lax.broadcasted_iota(jnp.int32, (c, c), 1)
    return s <= t, s < t


def _dot(x, y, cx, cy, prec=HIGH):
    nb = x.ndim - 2
    batch = tuple(range(nb))
    return lax.dot_general(x, y, (((cx + nb,), (cy + nb,)), (batch, batch)), precision=prec,
                           preferred_element_type=F32)


INV_BASE = 8


def _blocked_inv(a, mm, row, col):
    shift = INV_BASE.bit_length() - 1
    d = jnp.where(jnp.right_shift(row, shift) == jnp.right_shift(col, shift), a, 0.0)
    x = (row == col).astype(F32) - d
    p = mm(d, d)
    x = x + mm(x, p)
    x = x + mm(x, mm(p, p))
    size = INV_BASE
    while size < a.shape[-2]:
        same_pair = jnp.right_shift(row, shift + 1) == jnp.right_shift(col, shift + 1)
        lower_left = (jnp.right_shift(row, shift) & 1 == 1) & (jnp.right_shift(col, shift) & 1 == 0)
        x = x - mm(x, mm(jnp.where(same_pair & lower_left, a, 0.0), x))
        size, shift = 2 * size, shift + 1
    return x


def _tri_inv(a):
    c = a.shape[-1]
    row, col = lax.broadcasted_iota(jnp.int32, (c, c), 0), lax.broadcasted_iota(jnp.int32, (c, c), 1)
    return _blocked_inv(a, lambda x, y: _dot(x, y, 1, 0), row, col)


@jax.custom_vjp
def _tri_inv_saved(a, t):
    return t


def _tri_inv_saved_fwd(a, t):
    return t, t


def _tri_inv_saved_bwd(t, dt):
    return -_dot(_dot(t, dt, 0, 0), t, 1, 1), jnp.zeros_like(t)


_tri_inv_saved.defvjp(_tri_inv_saved_fwd, _tri_inv_saved_bwd)


def _pair_masks():
    lane = lax.broadcasted_iota(jnp.int32, (1, 2 * RW_HEAD), 1)
    m0 = (lane < RW_HEAD).astype(F32)
    return m0, 1.0 - m0


def _pair_bd(x):
    m0, m1 = _pair_masks()
    return jnp.concatenate([x * m0, x * m1], axis=-2)


def _pair_mm(x, y):
    return _dot(x, _pair_bd(y), 1, 0)


def _pair_inv(a):
    c = a.shape[-2]
    row = lax.broadcasted_iota(jnp.int32, (c, 2 * RW_HEAD), 0)
    col = lax.broadcasted_iota(jnp.int32, (c, 2 * RW_HEAD), 1) & (RW_HEAD - 1)
    return _blocked_inv(a, _pair_mm, row, col)


@jax.custom_vjp
def _pair_inv_saved(a, t):
    return t


def _pair_inv_saved_fwd(a, t):
    return t, t


def _pair_inv_saved_bwd(t, dt):
    m0, m1 = _pair_masks()
    c = t.shape[-2]
    z = _dot(t, dt, 0, 0)
    x = z[:, :c, :] * m0 + z[:, c:, :] * m1
    return -_dot(x, _pair_bd(t), 1, 1), jnp.zeros_like(t)


_pair_inv_saved.defvjp(_pair_inv_saved_fwd, _pair_inv_saved_bwd)


@jax.custom_vjp
def _use_saved(x, x_saved):
    return x_saved


_use_saved.defvjp(lambda x, x_saved: (x_saved, None), lambda _, g: (g, jnp.zeros_like(g)))


def _rw_chunk_fn(S, r, lw, k2, v, kkp, a, gate, rk, gnw, gnb, saved=None):
    B, C, P = r.shape
    m0, m1 = _pair_masks()
    seg = lambda x: (jnp.sum(x * m0, axis=-1, keepdims=True) * m0 + jnp.sum(x * m1, axis=-1, keepdims=True) * m1)
    mm = lambda x, y: _dot(x.astype(BF16), y.astype(BF16), 1, 0, None)
    nt = lambda x, y: _dot(x, y, 1, 1)
    tn = lambda x, y: _dot(x, y, 0, 0)
    pair_mm = lambda x, y: mm(x, _pair_bd(y))
    t_idx = lax.broadcasted_iota(jnp.int32, (C, P), 0)
    s_idx = lax.broadcasted_iota(jnp.int32, (C, P), 1) & (RW_HEAD - 1)
    incl, strict = s_idx <= t_idx, s_idx < t_idx
    tril = jnp.broadcast_to(_tril_masks(C)[0].astype(F32), (B, C, C))
    kk = kkp * lax.rsqrt(seg(kkp * kkp) + 1e-6)
    b = kk * a
    reuse = (lambda x, i: x) if saved is None else (lambda x, i: _use_saved(x, saved[i]))
    g_incl = reuse(_dot(tril, lw, 1, 0), 1)
    g_excl = g_incl - lw
    inv = jnp.exp(-g_incl)
    alpha, beta, kappa, rho = kk * jnp.exp(g_excl), b * inv, k2 * inv, r * jnp.exp(g_incl)
    ar = jnp.concatenate([alpha, rho], axis=-2)
    scores = reuse(nt(ar, jnp.concatenate([_pair_bd(beta), _pair_bd(kappa)], axis=-2)), 2)
    a_ab = jnp.where(strict, scores[:, :C, :P], 0.0)
    a_ak = jnp.where(strict, scores[:, :C, P:], 0.0)
    r_b = jnp.where(incl, scores[:, C:, :P], 0.0)
    r_k = jnp.where(incl, scores[:, C:, P:], 0.0)
    t_inv = _pair_inv(a_ab) if saved is None else _pair_inv_saved(a_ab, saved[0])
    on_state = reuse(nt(ar, S), 3)
    u = pair_mm(t_inv, -on_state[:, :C, :] - pair_mm(a_ak, v))
    y = on_state[:, C:, :] + mm(jnp.concatenate([r_b, r_k], axis=-1),
                                jnp.concatenate([_pair_bd(u), _pair_bd(v)], axis=-2))
    same_head = ((lax.broadcasted_iota(jnp.int32, (P, P), 0) < RW_HEAD)
                 == (lax.broadcasted_iota(jnp.int32, (P, P), 1) < RW_HEAD))
    fresh = tn(jnp.concatenate([u, v], axis=-2), jnp.concatenate([beta, kappa], axis=-2))
    S_new = jnp.exp(jnp.sum(lw, axis=-2, keepdims=True)) * (S + jnp.where(same_head, fresh, 0.0))
    dev = y - seg(y) * (1.0 / RW_HEAD)
    yn = dev * lax.rsqrt(seg(dev * dev) * (1.0 / RW_HEAD) + RW_GN_EPS) * gnw + gnb
    bonus = seg(r * k2 * rk) * v
    return (yn + bonus) * _silu(gate), S_new, (t_inv, g_incl, scores, on_state)


def _dn_chunk_fn(S, qc, kc, vc, bb, gb, z, nw, saved=None):
    B, C, D = qc.shape
    mm = lambda x, y: _dot(x.astype(BF16), y.astype(BF16), 1, 0, None)
    nt = lambda x, y, p=None: _dot(x, y, 1, 1, p) if p else _dot(x.astype(BF16), y.astype(BF16), 1, 1, None)
    tn = lambda x, y: _dot(x.astype(BF16), y.astype(BF16), 0, 0, None)
    incl, strict = _tril_masks(C)
    q = qc * lax.rsqrt(jnp.sum(qc * qc, axis=-1, keepdims=True) + 1e-6) * (D ** -0.5)
    k = kc * lax.rsqrt(jnp.sum(kc * kc, axis=-1, keepdims=True) + 1e-6)
    kb, vb = k * bb, vc * bb
    G = _dot(jnp.broadcast_to(incl.astype(F32), (B, C, C)), gb, 1, 0, HIGHEST)
    lane = lax.broadcasted_iota(jnp.int32, (C, D), 1)
    e0, e1 = (lane == 0).astype(F32), (lane == 1).astype(F32)
    diff = nt(G * e0 + e1, e0 - G * e1, HIGHEST)
    dmask = jnp.where(incl, jnp.exp(jnp.where(incl, diff, 0.0)), 0.0)
    M = jnp.where(strict, nt(kb, k) * dmask, 0.0)
    T = _tri_inv(M) if saved is None else _tri_inv_saved(M, saved[0])
    eG = jnp.exp(G)
    u = mm(T, vb)
    w = mm(T, kb * eG)
    attn = jnp.where(incl, nt(q, k) * dmask, 0.0)
    v_new = u - mm(w, S)
    o = mm(q * eG, S) + mm(attn, v_new)
    g_last = jnp.sum(gb, axis=-2, keepdims=True)
    S_new = S * jnp.exp(jnp.broadcast_to(g_last, S.shape)) + tn(k * jnp.exp(g_last - G), v_new)
    on = o * lax.rsqrt(jnp.mean(o * o, axis=-1, keepdims=True) + NORM_EPS) * nw
    return on * _silu(z), S_new, (T,)


N_GROUPS = 8
GROUP = 128
HALO = 8


def _groups(x):
    return jnp.concatenate([x[:, g * GROUP:(g + 1) * GROUP][None] for g in range(N_GROUPS)], axis=0)


@functools.partial(jax.custom_vjp, nondiff_argnums=(1,))
def _shift_rows(ext, j):
    return pltpu.roll(ext, j, 0)[HALO:, :]


def _shift_rows_fwd(ext, j):
    return _shift_rows(ext, j), None


def _shift_rows_bwd(j, _, d):
    z = jnp.concatenate([jnp.zeros((HALO, d.shape[1]), d.dtype), d], axis=0)
    return (pltpu.roll(z, z.shape[0] - j, 0),)


_shift_rows.defvjp(_shift_rows_fwd, _shift_rows_bwd)


def _rw_fused_fn(S, h_r, h_k, h_v, h_s, p_r, p_k, p_v, p_s, gate, *pars, saved=None):
    prev = lambda h, x: _shift_rows(jnp.concatenate([h, x], axis=0), 1)
    seq = _rw_prep_fn(p_r, prev(h_r, p_r), p_k, prev(h_k, p_k), p_v, prev(h_v, p_v), p_s, prev(h_s, p_s), *pars[:10])
    return _rw_chunk_fn(S, *[_groups(t) for t in seq], _groups(gate), *[_groups(t) for t in pars[10:]],
                        saved=saved)


def _dn_fused_fn(S, h_q, h_k, h_v, p_q, p_k, p_v, p_s2, z, *pars, saved=None):
    conv = []
    for i, (h, x) in enumerate(((h_q, p_q), (h_k, p_k), (h_v, p_v))):
        ext = jnp.concatenate([h, x], axis=0)
        conv += _conv_fn(x, _shift_rows(ext, 1), _shift_rows(ext, 2), _shift_rows(ext, 3), *pars[4 * i:4 * i + 4])
    beta, g = _dn_gate_fn(p_s2, pars[12], pars[13])
    return _dn_chunk_fn(S, *[_groups(t) for t in conv], beta, g, _groups(z), _groups(pars[14]), saved=saved)


def _chunk_fwd(name, fn, rows, n_halo, pars, saved_shapes):
    Lp = rows[0][0].shape[0]
    C, D, NB = CHUNK, GROUP, N_GROUPS
    nc, n_r, n_p, per = Lp // C, len(rows), len(pars), CHUNK // HALO
    n_s = len(saved_shapes)

    def body(*refs):
        row_refs, halo_refs, par_refs = refs[:n_r], refs[n_r:n_r + n_halo], refs[n_r + n_halo:n_r + n_halo + n_p]
        y_ref, ck_ref = refs[n_r + n_halo + n_p:n_r + n_halo + n_p + 2]
        saved_refs, s_ref = refs[n_r + n_halo + n_p + 2:-1], refs[-1]
        first = pl.program_id(0) == 0

        @pl.when(first)
        def _():
            s_ref[...] = jnp.zeros_like(s_ref)

        S = s_ref[...]
        ck_ref[0] = S
        halos = [jnp.where(first, 0.0, r[...]) for r in halo_refs]
        y, S_new, saved = fn(S, *halos, *[r[...] for r in row_refs], *[r[...] for r in par_refs])
        for g in range(NB):
            y_ref[:, g * D:(g + 1) * D] = y[g].astype(BF16)
        for o_ref, val in zip(saved_refs, saved):
            o_ref[0] = val
        s_ref[...] = S_new

    return pl.pallas_call(
        body, name=name, grid=(nc,),
        in_specs=[pl.BlockSpec((pl.Element(C), pl.Element(w)), lambda c, off=off: (c * C, off)) for (_, w, off) in rows]
        + [pl.BlockSpec((pl.Element(HALO), pl.Element(w)), lambda c, off=off: (pl.multiple_of(jnp.maximum(c * C - HALO, 0), HALO), off))
           for (_, w, off) in rows[:n_halo]]
        + [pl.BlockSpec(p.shape, lambda c: (0, 0)) for p in pars],
        out_specs=[pl.BlockSpec((C, NB * D), lambda c: (c, 0)), pl.BlockSpec((1, NB, D, D), lambda c: (c, 0, 0, 0))]
        + [pl.BlockSpec((1,) + tuple(shp), lambda c: (c, 0, 0, 0)) for shp in saved_shapes],
        out_shape=[jax.ShapeDtypeStruct((Lp, NB * D), BF16), jax.ShapeDtypeStruct((nc, NB, D, D), F32)]
        + [jax.ShapeDtypeStruct((nc,) + tuple(shp), F32) for shp in saved_shapes],
        scratch_shapes=[pltpu.VMEM((NB, D, D), F32)],
        compiler_params=_params(("arbitrary",)),
    )(*[r[0] for r in rows], *[r[0] for r in rows[:n_halo]], *pars)


def _chunk_bwd(name, fn, rows, n_halo, pars, ckpt, saved, dy, out_blocks, into=None):
    Lp = rows[0][0].shape[0]
    C, D, NB = CHUNK, GROUP, N_GROUPS
    nc, n_r, n_p, per = Lp // C, len(rows), len(pars), CHUNK // HALO
    n_s = len(saved)
    n_in = n_r + n_halo + n_p
    width = sum(w for (_, w, _) in rows)
    n_blocks, my_block = out_blocks
    extra = [] if into is None else [into]

    def body(*refs):
        row_refs, halo_refs, par_refs = refs[:n_r], refs[n_r:n_r + n_halo], refs[n_r + n_halo:n_in]
        ck_ref, saved_refs, dy_ref = refs[n_in], refs[n_in + 1:n_in + 1 + n_s], refs[n_in + 1 + n_s]
        outs = refs[n_in + 2 + n_s + len(extra):]
        drows_ref, dpar_refs = outs[0], outs[1:1 + n_p]
        ds_ref, carry_refs = outs[1 + n_p], outs[2 + n_p:]
        i = pl.program_id(0)

        @pl.when(i == 0)
        def _():
            ds_ref[...] = jnp.zeros_like(ds_ref)
            for o_ref in list(dpar_refs) + list(carry_refs):
                o_ref[...] = jnp.zeros_like(o_ref)

        halos = [jnp.where(i == nc - 1, 0.0, r[...]) for r in halo_refs]
        vals = [ck_ref[0]] + halos + [r[...] for r in row_refs] + [r[...] for r in par_refs]
        stored = tuple(r[0] for r in saved_refs)
        grads = jax.vjp(lambda *v: fn(*v, saved=stored)[:2], *vals)[1]((_groups(dy_ref), ds_ref[...]))
        ds_ref[...] = grads[0]
        d_halos, d_rows, d_pars = grads[1:1 + n_halo], grads[1 + n_halo:1 + n_halo + n_r], grads[1 + n_halo + n_r:]
        col = 0
        for k, g in enumerate(d_rows):
            if k < n_halo:
                g = g + jnp.concatenate([jnp.zeros((C - HALO, g.shape[1]), F32), carry_refs[k][...]], axis=0)
                carry_refs[k][...] = d_halos[k]
            drows_ref[:, col:col + g.shape[1]] = g.astype(BF16)
            col += g.shape[1]
        for o_ref, g in zip(dpar_refs, d_pars):
            o_ref[...] += g

    rev = lambda c: nc - 1 - c
    dy_arr, dy_w, dy_cb = dy
    res = pl.pallas_call(
        body, name=name, grid=(nc,),
        in_specs=[pl.BlockSpec((pl.Element(C), pl.Element(w)), lambda c, off=off: (rev(c) * C, off))
                  for (_, w, off) in rows]
        + [pl.BlockSpec((pl.Element(HALO), pl.Element(w)),
                        lambda c, off=off: (pl.multiple_of(jnp.maximum(rev(c) * C - HALO, 0), HALO), off))
           for (_, w, off) in rows[:n_halo]]
        + [pl.BlockSpec(p.shape, lambda c: (0, 0)) for p in pars]
        + [pl.BlockSpec((1, NB, D, D), lambda c: (rev(c), 0, 0, 0))]
        + [pl.BlockSpec((1,) + tuple(t.shape[1:]), lambda c: (rev(c), 0, 0, 0)) for t in saved]
        + [pl.BlockSpec((pl.Element(C), pl.Element(dy_w)), lambda c: (rev(c) * C, dy_cb))]
        + [pl.BlockSpec(memory_space=pl.ANY)] * len(extra),
        out_specs=[pl.BlockSpec((C, width), lambda c: (rev(c), my_block))]
        + [pl.BlockSpec(p.shape, lambda c: (0, 0)) for p in pars],
        out_shape=[jax.ShapeDtypeStruct((Lp, n_blocks * width), BF16)]
        + [jax.ShapeDtypeStruct(p.shape, F32) for p in pars],
        scratch_shapes=[pltpu.VMEM((NB, D, D), F32)] + [pltpu.VMEM((HALO, w), F32) for (_, w, _) in rows[:n_halo]],
        input_output_aliases={} if into is None else {n_in + 2 + n_s: 0},
        compiler_params=_params(("arbitrary",)),
    )(*[r[0] for r in rows], *[r[0] for r in rows[:n_halo]], *pars, ckpt, *saved, dy_arr, *extra)
    return res[0], res[1:]


def _loss_head(h, yw, tgt, fw, n_real, tm=2 * ROW_TILE):
    Lp, Dm = h.shape
    tm = _pick(Lp, tm, 16)

    def out_fn(z, fw_):
        return z * lax.rsqrt(jnp.mean(z * z, axis=-1, keepdims=True) + NORM_EPS) * fw_

    def body(h_ref, yw_ref, t_ref, fw_ref, loss_ref, dz_ref, dz16_ref, dfw_ref):
        i = pl.program_id(0)

        @pl.when(i == 0)
        def _():
            loss_ref[...] = jnp.zeros_like(loss_ref)
            dfw_ref[...] = jnp.zeros_like(dfw_ref)

        row = i * tm + lax.broadcasted_iota(jnp.int32, (tm, 1), 0)
        mask = ((row >= N_META) & (row < n_real)).astype(F32)
        z = h_ref[...] + yw_ref[...]
        o, vjp = jax.vjp(out_fn, z, fw_ref[...])
        err = (o - t_ref[...]) * mask
        row_loss = 0.5 * jnp.mean(jnp.square(err), axis=-1, keepdims=True)
        dz, dfw = vjp(err * (1.0 / Dm))
        loss_ref[...] += jnp.sum(row_loss, axis=0, keepdims=True)
        dz_ref[...] = dz
        dz16_ref[...] = dz.astype(BF16)
        dfw_ref[...] += dfw

    row_spec = pl.BlockSpec((tm, Dm), lambda i: (i, 0))
    return pl.pallas_call(
        body, name="loss_head", grid=(Lp // tm,),
        in_specs=[row_spec, row_spec, row_spec, pl.BlockSpec((1, Dm), lambda i: (0, 0))],
        out_specs=[pl.BlockSpec((8, 128), lambda i: (0, 0)), row_spec, row_spec,
                   pl.BlockSpec((1, Dm), lambda i: (0, 0))],
        out_shape=[jax.ShapeDtypeStruct((8, 128), F32), jax.ShapeDtypeStruct((Lp, Dm), F32),
                   jax.ShapeDtypeStruct((Lp, Dm), BF16), jax.ShapeDtypeStruct((1, Dm), F32)],
        compiler_params=_params(("arbitrary",)),
    )(h, yw, tgt, fw)


def _exchange(name, x, masks, slot_kind, per_dest, n_split=1, copy_own=True, other_half=False):
    n = len(masks)
    keep_own = slot_kind is not None and copy_own
    n_slots = {"chip": 4, "core": 2, "dev": 8, None: n}[slot_kind]
    blk_shape = x.shape[1:] if per_dest else x.shape
    if other_half:
        blk_shape = (x.shape[0] // 2,) + tuple(x.shape[1:])
    rows = blk_shape[0] // n_split

    def body(x_ref, o_ref, send_sems, recv_sems, local_sems):
        mx, my, mc = lax.axis_index("x"), lax.axis_index("y"), lax.axis_index("c")
        if other_half:
            x_ref = x_ref.at[pl.ds((1 - mc) * blk_shape[0], blk_shape[0])]

        def slot(k, px, py, pc):
            return {"chip": 2 * px + py, "core": pc, "dev": 4 * px + 2 * py + pc, None: k}[slot_kind]

        def peer(m):
            return (mx + m[0]) % 2, (my + m[1]) % 2, (mc + m[2]) % 2

        def part(ref, j):
            return ref.at[pl.ds(j * rows, rows)]

        own_src = x_ref.at[2 * mx + my] if per_dest else x_ref
        local = []
        if keep_own:
            own_dst = o_ref.at[slot(0, mx, my, mc)]
            local = [pltpu.make_async_copy(part(own_src, j), part(own_dst, j), local_sems.at[j])
                     for j in range(n_split)]
        for cp in local:
            cp.start()
        sends = []
        for k, m in enumerate(masks):
            px, py, pc = peer(m)
            src = x_ref.at[2 * px + py] if per_dest else x_ref
            dst = o_ref.at[slot(k, mx, my, mc)]
            for j in range(n_split):
                sends.append(pltpu.make_async_remote_copy(
                    src_ref=part(src, j), dst_ref=part(dst, j), send_sem=send_sems.at[k * n_split + j],
                    recv_sem=recv_sems.at[k * n_split + j], device_id=(px, py, pc), device_id_type=MESH))
        for cp in sends:
            cp.start()
        for k, m in enumerate(masks):
            px, py, pc = peer(m)
            landed = o_ref.at[slot(k, px, py, pc)]
            for j in range(n_split):
                pltpu.make_async_remote_copy(
                    src_ref=part(own_src, j), dst_ref=part(landed, j), send_sem=send_sems.at[k * n_split + j],
                    recv_sem=recv_sems.at[k * n_split + j], device_id=(px, py, pc), device_id_type=MESH).wait_recv()
        for cp in sends:
            cp.wait_send()
        for cp in local:
            cp.wait()

    return pl.pallas_call(
        body, name=name,
        in_specs=[pl.BlockSpec(memory_space=pl.ANY)], out_specs=pl.BlockSpec(memory_space=pl.ANY),
        out_shape=jax.ShapeDtypeStruct((n_slots,) + tuple(blk_shape), x.dtype),
        scratch_shapes=[pltpu.SemaphoreType.DMA((n * n_split,)), pltpu.SemaphoreType.DMA((n * n_split,)),
                        pltpu.SemaphoreType.DMA((n_split,))],
        compiler_params=pltpu.CompilerParams(has_side_effects=True),
    )(x)


CHIP_MASKS = [(1, 0, 0), (0, 1, 0), (1, 1, 0)]
CORE_MASKS = [(0, 0, 1)]
ALL_MASKS = [(dx, dy, dc) for dx in (0, 1) for dy in (0, 1) for dc in (0, 1) if (dx, dy, dc) != (0, 0, 0)]

HBM_SPEC = pl.BlockSpec(memory_space=pltpu.HBM)
SEM_SPEC = pl.BlockSpec(memory_space=pltpu.SEMAPHORE)
DATAFLOW = pltpu.SideEffectType.DATAFLOW_SIDE_EFFECTING


def _split_copies(x_ref, land_ref, send_sems, recv_sems, masks, slot_kind, per_dest, n_split, with_recvs=True):
    mx, my, mc = lax.axis_index("x"), lax.axis_index("y"), lax.axis_index("c")
    slot = lambda px, py, pc: {"chip": 2 * px + py, "core": pc}[slot_kind]
    rows = (x_ref.shape[1] if per_dest else x_ref.shape[0]) // n_split
    part = lambda ref, j: ref.at[pl.ds(j * rows, rows)]
    sends, recvs = [], []
    for k, m in enumerate(masks):
        px, py, pc = (mx + m[0]) % 2, (my + m[1]) % 2, (mc + m[2]) % 2
        src = x_ref.at[2 * px + py] if per_dest else x_ref
        own_src = x_ref.at[2 * mx + my] if per_dest else x_ref
        for j in range(n_split):
            sems = dict(send_sem=send_sems.at[k * n_split + j], recv_sem=recv_sems.at[k * n_split + j],
                        device_id=(px, py, pc), device_id_type=MESH)
            sends.append(pltpu.make_async_remote_copy(
                src_ref=part(src, j), dst_ref=part(land_ref.at[slot(mx, my, mc)], j), **sems))
            if with_recvs:
                recvs.append(pltpu.make_async_remote_copy(
                    src_ref=part(own_src, j), dst_ref=part(land_ref.at[slot(px, py, pc)], j), **sems))
    return sends, recvs


def _exchange_start(name, x, masks, slot_kind, per_dest, n_split, dep=None):
    n = len(masks) * n_split
    blk_shape = x.shape[1:] if per_dest else x.shape
    land_shape = ({"chip": 4, "core": 2}[slot_kind],) + tuple(blk_shape)
    deps = [] if dep is None else [dep]

    def body(x_ref, land_ref, *rest):
        send_sems, recv_sems, x_thru, land_thru, token = rest[len(deps):]
        for cp in _split_copies(x_ref, land_ref, send_sems, recv_sems, masks, slot_kind, per_dest, n_split, False)[0]:
            cp.start()
        token[...] = jnp.zeros_like(token)

    return pl.pallas_call(
        body, name=name,
        out_shape=(pltpu.SemaphoreType.DMA((n,)), pltpu.SemaphoreType.DMA((n,)), pltpu.HBM(x.shape, x.dtype),
                   pltpu.HBM(land_shape, x.dtype), jax.ShapeDtypeStruct((8, 128), F32)),
        in_specs=(HBM_SPEC, HBM_SPEC) + (pl.BlockSpec(memory_space=pl.ANY),) * len(deps),
        out_specs=(SEM_SPEC, SEM_SPEC, HBM_SPEC, HBM_SPEC, pl.BlockSpec(memory_space=pltpu.VMEM)),
        input_output_aliases={0: 2, 1: 3},
        compiler_params=pltpu.CompilerParams(has_side_effects=DATAFLOW),
    )(pltpu.with_memory_space_constraint(x, pltpu.HBM),
      pltpu.with_memory_space_constraint(lax.empty(land_shape, x.dtype), pltpu.HBM), *deps)


def _exchange_wait(name, started, after, masks, slot_kind, per_dest, n_split):
    send_sems, recv_sems, x_thru, land_thru, _ = started

    def body(x_ref, land_ref, send_sems, recv_sems, after_ref, x_out, land_out):
        sends, recvs = _split_copies(x_ref, land_ref, send_sems, recv_sems, masks, slot_kind, per_dest, n_split)
        for cp in sends:
            cp.wait_send()
        for cp in recvs:
            cp.wait_recv()

    return pl.pallas_call(
        body, name=name,
        out_shape=(pltpu.HBM(x_thru.shape, x_thru.dtype), pltpu.HBM(land_thru.shape, land_thru.dtype)),
        in_specs=(HBM_SPEC, HBM_SPEC, SEM_SPEC, SEM_SPEC, pl.BlockSpec(memory_space=pl.ANY)),
        out_specs=(HBM_SPEC, HBM_SPEC), input_output_aliases={0: 0, 1: 1},
        compiler_params=pltpu.CompilerParams(has_side_effects=DATAFLOW),
    )(x_thru, land_thru, send_sems, recv_sems, after)


def _gather_chips(name, x):
    return _exchange(name, x, CHIP_MASKS, "chip", False)


def _gather_shards(name, shard, n_split, overlap=False, dep=None):
    half = shard.shape[0] // 2
    mine = lax.dynamic_slice_in_dim(shard, lax.axis_index("c") * half, half, axis=0)
    if overlap:
        return _exchange_start(name + "_chips", mine, CHIP_MASKS, "chip", False, n_split, dep)
    by_chip = _exchange(name + "_chips", mine, CHIP_MASKS, "chip", False, n_split, copy_own=False)
    return _gather_tail(name, mine, by_chip)


def _gather_tail(name, mine, by_chip):
    c, chip = lax.axis_index("c"), 2 * lax.axis_index("x") + lax.axis_index("y")
    by_chip = lax.dynamic_update_index_in_dim(by_chip, mine, chip, 0)
    both = _exchange(name + "_cores", by_chip, CORE_MASKS, "core", False, N_CHIPS, copy_own=False)
    return lax.dynamic_update_index_in_dim(both, by_chip, c, 0)


def _gather_finish(name, started, after, n_split):
    mine, by_chip = _exchange_wait(name + "_chips_wait", started, after, CHIP_MASKS, "chip", False, n_split)
    return _gather_tail(name, mine, by_chip)


def _sum_slots(name, x, tr=128):
    S, R, N = x.shape
    tr = _pick(R, tr, 16)

    def body(x_ref, o_ref):
        acc = x_ref[0].astype(F32)
        for s in range(1, S):
            acc = acc + x_ref[s].astype(F32)
        o_ref[...] = acc

    return pl.pallas_call(
        body, name=name, grid=(R // tr,),
        in_specs=[pl.BlockSpec((S, tr, N), lambda i: (0, i, 0))], out_specs=pl.BlockSpec((tr, N), lambda i: (i, 0)),
        out_shape=jax.ShapeDtypeStruct((R, N), F32), compiler_params=_params(("parallel",)),
    )(x)


def _add_to_bf16(name, a, b, tr=128):
    S, R, N = a.shape
    tr = _pick(R, tr, 16)

    def body(a_ref, b_ref, o_ref):
        o_ref[...] = (a_ref[...] + b_ref[...]).astype(BF16)

    spec = pl.BlockSpec((S, tr, N), lambda i: (0, i, 0))
    return pl.pallas_call(
        body, name=name, grid=(R // tr,), in_specs=[spec, spec], out_specs=spec,
        out_shape=jax.ShapeDtypeStruct((S, R, N), BF16), compiler_params=_params(("parallel",)),
    )(a, b)


def _add_slabs_to_bf16(name, full, recv, c, tr=64):
    R, NP = full.shape
    half = R // 2
    nb = half // tr

    def body(c_ref, a_ref, b_ref, o_ref):
        x = a_ref[...] + b_ref[...]
        for s in range(N_CHIPS - 1):
            o_ref[s] = x[:, s * SHARD_COLS:(s + 1) * SHARD_COLS].astype(BF16)
        last = jnp.concatenate([x[:, (N_CHIPS - 1) * SHARD_COLS:GAP_AT], x[:, GAP_AT + GAP:]], axis=1)
        o_ref[N_CHIPS - 1] = last.astype(BF16)

    grid_spec = pltpu.PrefetchScalarGridSpec(
        num_scalar_prefetch=1, grid=(nb,),
        in_specs=[pl.BlockSpec((tr, NP), lambda i, c_ref: (c_ref[0] * nb + i, 0)),
                  pl.BlockSpec((tr, NP), lambda i, c_ref: (i, 0))],
        out_specs=pl.BlockSpec((N_CHIPS, tr, SHARD_COLS), lambda i, c_ref: (0, i, 0)))
    return pl.pallas_call(
        body, name=name, grid_spec=grid_spec,
        out_shape=jax.ShapeDtypeStruct((N_CHIPS, half, SHARD_COLS), BF16), compiler_params=_params(("parallel",)),
    )(jnp.reshape(c, (1,)).astype(jnp.int32), full, recv)


def _adamw(name, gparts, w, m, v, tr=128):
    S, R, N = gparts.shape
    tr = _pick(R, tr)
    c1 = 1.0 / (1.0 - ADAM_B1 ** ADAM_STEP)
    c2 = 1.0 / (1.0 - ADAM_B2 ** ADAM_STEP)

    def body(g_ref, w_ref, m_ref, v_ref, go_ref, d_ref, mo_ref, vo_ref):
        g = g_ref[0]
        for s in range(1, S):
            g = g + g_ref[s]
        m_new = ADAM_B1 * m_ref[...] + (1.0 - ADAM_B1) * g
        v_new = ADAM_B2 * v_ref[...] + (1.0 - ADAM_B2) * jnp.square(g)
        go_ref[...] = g
        mo_ref[...] = m_new
        vo_ref[...] = v_new
        d_ref[...] = -ADAM_LR * ((m_new * c1) / (jnp.sqrt(v_new * c2) + ADAM_EPS) + ADAM_WD * w_ref[...])

    spec = pl.BlockSpec((tr, N), lambda i: (i, 0))
    return pl.pallas_call(
        body, name=name, grid=(R // tr,),
        in_specs=[pl.BlockSpec((S, tr, N), lambda i: (0, i, 0)), spec, spec, spec], out_specs=[spec] * 4,
        out_shape=[jax.ShapeDtypeStruct((R, N), F32)] * 4, compiler_params=_params(("parallel",)),
    )(gparts, w, m, v)


def _reduce_to_shard(name, slabs, n_split, by_columns=False, overlap=False):
    c, chip = lax.axis_index("c"), 2 * lax.axis_index("x") + lax.axis_index("y")
    if by_columns:
        R, N = slabs.shape[0], SHARD_COLS
        from_sibling = _exchange(name + "_sib", slabs, CORE_MASKS, None, False, n_split, other_half=True)[0]
        wire = _add_slabs_to_bf16(name + "_add", slabs, from_sibling, c)
    else:
        _, R, N = slabs.shape
        half = R // 2
        halves = slabs.reshape(N_CHIPS, 2, half, N)
        mine = lax.dynamic_index_in_dim(halves, c, axis=1, keepdims=False)
        theirs = lax.dynamic_index_in_dim(halves, 1 - c, axis=1, keepdims=False)
        from_sibling = _exchange(name + "_sib", theirs, CORE_MASKS, None, False, N_CHIPS)[0]
        wire = _add_to_bf16(name + "_add", mine, from_sibling)
    if overlap:
        return _exchange_start(name + "_chips", wire, CHIP_MASKS, "chip", True, n_split), (R, N)
    got = _exchange(name + "_chips", wire, CHIP_MASKS, "chip", True, n_split, copy_own=False)
    return _reduce_tail(name, wire, got, n_split, R, N)


def _reduce_tail(name, wire, got, n_split, R, N):
    c, chip = lax.axis_index("c"), 2 * lax.axis_index("x") + lax.axis_index("y")
    got = lax.dynamic_update_index_in_dim(got, lax.dynamic_index_in_dim(wire, chip, 0, keepdims=False), chip, 0)
    part = _sum_slots(name + "_sum", got)
    both = _exchange(name + "_cores", part, CORE_MASKS, "core", False, n_split, copy_own=False)
    return lax.dynamic_update_index_in_dim(both, part, c, 0).reshape(1, R, N)


def _reduce_finish(name, started, after, n_split):
    handle, (R, N) = started
    wire, got = _exchange_wait(name + "_chips_wait", handle, after, CHIP_MASKS, "chip", True, n_split)
    return _reduce_tail(name, wire, got, n_split, R, N)


def _pack(pieces, cols, row_mult=8):
    flat = jnp.concatenate([p.reshape(-1) for p in pieces])
    rows = -(-flat.shape[0] // cols)
    rows = -(-rows // row_mult) * row_mult
    return jnp.pad(flat, (0, rows * cols - flat.shape[0])).reshape(rows, cols)


def _unpack(packed, shapes):
    flat = packed.reshape(-1)
    out, off = [], 0
    for shp in shapes:
        n = 1
        for d in shp:
            n *= d
        out.append(flat[off:off + n].reshape(shp))
        off += n
    return out


def kernel(x, meta_tokens, norm_w, w_in, rw_shift_mu, rw_w0, rw_w2, rw_a0, rw_a2, rw_k_k, rw_k_a, rw_r_k, rw_gn_w, rw_gn_b, dn_conv_w, dn_A_log, dn_dt_bias, dn_norm_w, w_out, final_norm_w, loss_target, m_meta_tokens, m_norm_w, m_w_in, m_rw_shift_mu, m_rw_w0, m_rw_w2, m_rw_a0, m_rw_a2, m_rw_k_k, m_rw_k_a, m_rw_r_k, m_rw_gn_w, m_rw_gn_b, m_dn_conv_w, m_dn_A_log, m_dn_dt_bias, m_dn_norm_w, m_w_out, m_final_norm_w, v_meta_tokens, v_norm_w, v_w_in, v_rw_shift_mu, v_rw_w0, v_rw_w2, v_rw_a0, v_rw_a2, v_rw_k_k, v_rw_k_a, v_rw_r_k, v_rw_gn_w, v_rw_gn_b, v_dn_conv_w, v_dn_A_log, v_dn_dt_bias, v_dn_norm_w, v_w_out, v_final_norm_w):
    S = x.shape[1]
    L = N_META + S
    Lp = -(-L // CHUNK) * CHUNK

    small_shapes = [(RW_LORA, 256), (RW_LORA, 256), (CONV_W, 768), (N_META, 512)]
    small_mine = _pack([rw_w2[0], rw_a2[0], dn_conv_w[0], meta_tokens], 1024)
    small_all = _gather_chips("gather_small", small_mine)
    per_chip = [_unpack(small_all[s], small_shapes) for s in range(N_CHIPS)]
    w2, a2, conv_w, meta = [jnp.concatenate([per_chip[s][i] for s in range(N_CHIPS)], axis=1) for i in range(4)]
    w_in_started = _gather_shards("gather_w_in", w_in[0].astype(BF16), 8, overlap=True, dep=small_all)
    w_out_started = _gather_shards("gather_w_out", w_out[0].astype(BF16), 8, overlap=True, dep=w_in_started[4])

    tail = [jnp.zeros((Lp - L, D_MODEL), F32)] if Lp > L else []
    h = jnp.concatenate([meta + w_in_started[4][:1, :1], x[0]] + tail, axis=0)
    tgt = jnp.concatenate([jnp.zeros((N_META, D_MODEL), F32), loss_target[0]] + tail, axis=0)
    (u,) = _rowwise("rms_in", _rms_fn, [_row(h)], [norm_w], [D_MODEL], tm=2 * ROW_TILE, out_dtype=BF16)
    w_in_all = _gather_finish("gather_w_in", w_in_started, u, 8)
    slabs = [w_in_all[:, s].reshape(D_MODEL, SHARD_COLS) for s in range(N_CHIPS)]
    cut = GAP_AT - (N_CHIPS - 1) * SHARD_COLS
    W_gapped = jnp.concatenate(slabs[:-1] + [slabs[-1][:, :cut], jnp.zeros((D_MODEL, GAP), BF16), slabs[-1][:, cut:]],
                               axis=1)
    p = _mm("in_proj", u, W_gapped, "nn", dep=w_out_started[4])

    mu = rw_shift_mu
    zpad = jnp.zeros((RW_LORA, RW_WIDTH), F32)
    rw_params = [mu[:, 0:1024], mu[:, 1024:2048], mu[:, 2048:3072], mu[:, 3072:3200], rw_w0,
                 jnp.concatenate([w2, zpad], axis=0), rw_a0, jnp.concatenate([zpad, a2], axis=0), rw_k_k, rw_k_a]
    rw_rows = [_row(p, 1024, OFF_R), _row(p, 1024, OFF_K), _row(p, 1024, OFF_V), _row(p, 128, OFF_S1),
               _row(p, 1024, OFF_GATE)]
    rw_pars = rw_params + [rw_r_k, rw_gn_w, rw_gn_b]
    rw_saved_shapes = [(N_GROUPS, CHUNK, GROUP), (N_GROUPS, CHUNK, GROUP), (N_GROUPS, 2 * CHUNK, 2 * GROUP),
                       (N_GROUPS, 2 * CHUNK, GROUP)]
    ya, rw_ck, *rw_t = _chunk_fwd("rw_chunk_fwd", _rw_fused_fn, rw_rows, 4, rw_pars, rw_saved_shapes)

    dn_rows = [_row(p, 1024, OFF_DQ), _row(p, 1024, OFF_DK), _row(p, 1024, OFF_DV), _row(p, 128, OFF_S2),
               _row(p, 1024, OFF_Z)]
    dn_pars = [conv_w[j:j + 1, 1024 * i:1024 * (i + 1)] for i in range(3) for j in range(CONV_W)]
    narrow = lambda t: jnp.pad(t, ((0, 0), (DN_HEADS, DN_HEAD - 2 * DN_HEADS)))
    dn_pars += [narrow(dn_A_log), narrow(dn_dt_bias), jnp.tile(dn_norm_w, (1, DN_HEADS))]
    yb, dn_ck, *dn_t = _chunk_fwd("dn_chunk_fwd", _dn_fused_fn, dn_rows, 3, dn_pars, [(N_GROUPS, CHUNK, CHUNK)])

    y = jnp.concatenate([ya, yb], axis=1)
    w_out_all = _gather_finish("gather_w_out", w_out_started, y, 8)
    Wo = jnp.concatenate([w_out_all[:, s].reshape(D_MODEL // N_CHIPS, D_MODEL) for s in range(N_CHIPS)], axis=0)
    yw = _mm("out_proj", y, Wo, "nn", tn=1024)
    loss_acc, dz, dz16, d_fw = _loss_head(h, yw, tgt, final_norm_w.reshape(1, D_MODEL), L)
    loss = lax.psum(loss_acc[0, 0], ("x", "y", "c"))

    d_wo = _mm("d_w_out", y.T, dz16, "nn", tm=512, tn=1024, tk=Lp)
    wo_started = _reduce_to_shard("rs_w_out", d_wo.reshape(N_CHIPS, D_MODEL // N_CHIPS, D_MODEL), 8, overlap=True)
    dy = _mm("d_out_proj", dz16, Wo, "nt", tn=1024, tk=2048, dep=wo_started[0][4])

    dp, d_rw_pars = _chunk_bwd("rw_chunk_bwd", _rw_fused_fn, rw_rows, 4, rw_pars, rw_ck, rw_t, _row(dy, 1024, 0),
                               (2, 0))
    d_prep_pars, d_rw_pars = d_rw_pars[:10], d_rw_pars[10:]
    dp, d_dn_pars = _chunk_bwd("dn_chunk_bwd", _dn_fused_fn, dn_rows, 3, dn_pars, dn_ck, dn_t, _row(dy, 1024, RW_WIDTH),
                               (2, 1), into=dp)
    d_conv_parts = [jnp.concatenate(d_dn_pars[4 * i:4 * i + 4], axis=0) for i in range(3)]
    d_a_log_b, d_dt_b = d_dn_pars[12], d_dn_pars[13]
    d_dn_nw = jnp.sum(d_dn_pars[14].reshape(DN_HEADS, DN_HEAD), axis=0, keepdims=True)
    d_W = _mm("d_w_in", u.T, dp, "nn", tm=1024, tn=768, tk=Lp)
    wi_started = _reduce_to_shard("rs_w_in", d_W, 8, by_columns=True, overlap=True)
    du = _mm("d_in_proj", dp, W_gapped, "nt", tn=2048, tk=1408, dep=wi_started[0][4])
    (dh,), (d_norm_w,) = _rowwise_bwd("rms_in_bwd", lambda h_, w_: (_rms_fn(h_, w_)[0], h_), [_row(h)], [norm_w],
                                      [_row(du), _row(dz)])
    grad_x = dh[N_META:L][None]

    d_mu = jnp.concatenate(d_prep_pars[0:4], axis=1)
    d_w2, d_a2 = d_prep_pars[5][:RW_LORA], d_prep_pars[7][RW_LORA:]
    d_conv = jnp.concatenate(d_conv_parts, axis=1)
    d_meta = dh[:N_META]
    head_sum = lambda t: t[:, DN_HEADS:2 * DN_HEADS]
    rep_names = ["norm_w", "rw_shift_mu", "rw_w0", "rw_a0", "rw_k_k", "rw_k_a", "rw_r_k", "rw_gn_w", "rw_gn_b",
                 "dn_A_log", "dn_dt_bias", "dn_norm_w", "final_norm_w"]
    rep_g = [d_norm_w, d_mu, d_prep_pars[4], d_prep_pars[6], d_prep_pars[8], d_prep_pars[9],
             d_rw_pars[0], d_rw_pars[1], d_rw_pars[2],
             head_sum(d_a_log_b), head_sum(d_dt_b), d_dn_nw, d_fw.reshape(D_MODEL)]
    rep_w = [norm_w, rw_shift_mu, rw_w0, rw_a0, rw_k_k, rw_k_a, rw_r_k, rw_gn_w, rw_gn_b, dn_A_log, dn_dt_bias,
             dn_norm_w, final_norm_w]
    rep_m = [m_norm_w, m_rw_shift_mu, m_rw_w0, m_rw_a0, m_rw_k_k, m_rw_k_a, m_rw_r_k, m_rw_gn_w, m_rw_gn_b,
             m_dn_A_log, m_dn_dt_bias, m_dn_norm_w, m_final_norm_w]
    rep_v = [v_norm_w, v_rw_shift_mu, v_rw_w0, v_rw_a0, v_rw_k_k, v_rw_k_a, v_rw_r_k, v_rw_gn_w, v_rw_gn_b,
             v_dn_A_log, v_dn_dt_bias, v_dn_norm_w, v_final_norm_w]
    rep_shapes = [t.shape for t in rep_w]
    rep_all = _exchange("gather_rep_grads", _pack(rep_g, 128), ALL_MASKS, "dev", False)
    rep_out = _adamw("adam_rep", rep_all, _pack(rep_w, 128), _pack(rep_m, 128), _pack(rep_v, 128))
    rep_out = [dict(zip(rep_names, _unpack(t, rep_shapes))) for t in rep_out]

    sm_slabs = jnp.stack([_pack([d_w2[:, 256 * s:256 * (s + 1)], d_a2[:, 256 * s:256 * (s + 1)],
                                 d_conv[:, 768 * s:768 * (s + 1)], d_meta[:, 512 * s:512 * (s + 1)]], 1024, 64)
                          for s in range(N_CHIPS)])
    sm_parts = _reduce_to_shard("rs_small", sm_slabs, 1)
    sm_w = [rw_w2[0], rw_a2[0], dn_conv_w[0], meta_tokens]
    sm_m = [m_rw_w2[0], m_rw_a2[0], m_dn_conv_w[0], m_meta_tokens]
    sm_v = [v_rw_w2[0], v_rw_a2[0], v_dn_conv_w[0], v_meta_tokens]
    sm_out = _adamw("adam_small", sm_parts, _pack(sm_w, 1024, 64), _pack(sm_m, 1024, 64), _pack(sm_v, 1024, 64))
    sm_names = ["rw_w2", "rw_a2", "dn_conv_w", "meta_tokens"]
    sm_full_shapes = [(1, RW_LORA, 256), (1, RW_LORA, 256), (1, CONV_W, 768), (N_META, 512)]
    sm_out = [dict(zip(sm_names, [t.reshape(shp) for t, shp in zip(_unpack(o, small_shapes), sm_full_shapes)]))
              for o in sm_out]

    wo_parts = _reduce_finish("rs_w_out", wo_started, dp, 8)
    wo_out = _adamw("adam_w_out", wo_parts, w_out[0], m_w_out[0], v_w_out[0])
    wi_parts = _reduce_finish("rs_w_in", wi_started, dh, 8)
    wi_out = _adamw("adam_w_in", wi_parts, w_in[0], m_w_in[0], v_w_in[0])

    order = ["meta_tokens", "norm_w", "w_in", "rw_shift_mu", "rw_w0", "rw_w2", "rw_a0", "rw_a2", "rw_k_k", "rw_k_a",
             "rw_r_k", "rw_gn_w", "rw_gn_b", "dn_conv_w", "dn_A_log", "dn_dt_bias", "dn_norm_w", "w_out",
             "final_norm_w"]
    outs = [loss, grad_x]
    for kind in range(4):
        table = dict(rep_out[kind])
        table.update(sm_out[kind])
        table["w_in"] = wi_out[kind][None]
        table["w_out"] = wo_out[kind][None]
        outs += [table[n] for n in order]
    return tuple(outs)
```

```python
import functools

import jax
import jax.numpy as jnp
from jax import lax
from jax.experimental import pallas as pl
from jax.experimental.pallas import tpu as pltpu

F32 = jnp.float32
BF16 = jnp.bfloat16
HIGH = lax.Precision.HIGH
HIGHEST = lax.Precision.HIGHEST
MESH = pl.DeviceIdType.MESH

D_MODEL = 2048
N_META = 16
RW_WIDTH = 1024
RW_HEAD = 64
RW_HEADS = 16
RW_LORA = 64
RW_GN_EPS = 64e-5
DN_WIDTH = 1024
DN_HEAD = 128
DN_HEADS = 8
CONV_W = 4
CHUNK = 64
NORM_EPS = 1e-6
IN_COLS = 8336
N_CHIPS = 4
SHARD_COLS = IN_COLS // N_CHIPS

NP_COLS = 8 * 1024 + 256
GAP_AT = IN_COLS - DN_WIDTH
GAP = NP_COLS - IN_COLS
OFF_R, OFF_K, OFF_V, OFF_S1, OFF_GATE = 0, 1024, 2048, 3072, 3200
OFF_DQ, OFF_DK, OFF_DV, OFF_S2, OFF_Z = 4224, 5248, 6272, 7296, 7424

ADAM_LR = 0.001
ADAM_B1 = 0.9
ADAM_B2 = 0.999
ADAM_EPS = 1e-08
ADAM_WD = 0.01
ADAM_STEP = 10

VMEM_LIMIT_BYTES = 56 * 1024 * 1024
ROW_TILE = 104
MM_ROW_TILE = 832


def _params(sem=None):
    return pltpu.CompilerParams(dimension_semantics=sem, vmem_limit_bytes=VMEM_LIMIT_BYTES)


def _pick(n, target, mult=8):
    best = None
    for d in range(mult, min(n, target) + 1, mult):
        if n % d == 0:
            best = d
    return n if best is None else best


def _mm(name, a, b, mode, tm=MM_ROW_TILE, tn=1408, tk=2048, dep=None, n_blocks=None, into=None):
    if mode == "nn":
        (M, K), (_, N) = a.shape, b.shape
    elif mode == "nt":
        (M, K), (N, _) = a.shape, b.shape
    else:
        (K, M), (_, N) = a.shape, b.shape
    tm = _pick(M, tm, 128 if mode == "tn" else 16)
    tn = _pick(N, tn, 128)
    tk = _pick(K, tk, 8 if mode == "tn" else 128)
    j0, nj = (0, N // tn) if n_blocks is None else n_blocks
    if mode == "nn":
        a_spec = pl.BlockSpec((tm, tk), lambda i, j, k: (i, k))
        b_spec = pl.BlockSpec((tk, tn), lambda i, j, k: (k, j + j0))
        dims = (((1,), (0,)), ((), ()))
    elif mode == "nt":
        a_spec = pl.BlockSpec((tm, tk), lambda i, j, k: (i, k))
        b_spec = pl.BlockSpec((tn, tk), lambda i, j, k: (j + j0, k))
        dims = (((1,), (1,)), ((), ()))
    else:
        a_spec = pl.BlockSpec((tk, tm), lambda i, j, k: (k, i))
        b_spec = pl.BlockSpec((tk, tn), lambda i, j, k: (k, j + j0))
        dims = (((0,), (0,)), ((), ()))

    def body(a_ref, b_ref, *rest):
        o_ref = rest[-1]

        @pl.when(pl.program_id(2) == 0)
        def _():
            o_ref[...] = jnp.zeros_like(o_ref)

        o_ref[...] += lax.dot_general(a_ref[...].astype(BF16), b_ref[...].astype(BF16), dims,
                                      preferred_element_type=F32)

    deps = [] if dep is None else [dep]
    prior = [] if into is None else [into]
    return pl.pallas_call(
        body, name=name, grid=(M // tm, nj, K // tk),
        in_specs=[a_spec, b_spec] + [pl.BlockSpec((8, 128), lambda i, j, k: (0, 0))] * len(deps)
        + [pl.BlockSpec(memory_space=pl.ANY)] * len(prior),
        out_specs=pl.BlockSpec((tm, tn), lambda i, j, k: (i, j + j0)),
        out_shape=jax.ShapeDtypeStruct((M, N), F32),
        input_output_aliases={} if into is None else {2 + len(deps): 0},
        compiler_params=_params(("parallel", "parallel", "arbitrary")),
    )(a, b, *deps, *prior)


def _row(arr, width=None, cb=0):
    return (arr, arr.shape[1] if width is None else width, cb)


def _rowwise(name, fn, rows, params, out_widths, tm=ROW_TILE, out_dtype=F32):
    R = rows[0][0].shape[0]
    tm = _pick(R, tm, 16 if out_dtype == BF16 else 8)
    n_r, n_p = len(rows), len(params)

    def body(*refs):
        vals = [r[...] for r in refs[:n_r + n_p]]
        for o_ref, val in zip(refs[n_r + n_p:], fn(*vals)):
            o_ref[...] = val.astype(out_dtype)

    in_specs = [pl.BlockSpec((tm, w), lambda i, cb=cb: (i, cb)) for (_, w, cb) in rows]
    in_specs += [pl.BlockSpec(p.shape, lambda i: (0, 0)) for p in params]
    return pl.pallas_call(
        body, name=name, grid=(R // tm,), in_specs=in_specs,
        out_specs=[pl.BlockSpec((tm, w), lambda i: (i, 0)) for w in out_widths],
        out_shape=[jax.ShapeDtypeStruct((R, w), out_dtype) for w in out_widths],
        compiler_params=_params(("parallel",)),
    )(*[r[0] for r in rows], *params)


def _rowwise_bwd(name, fn, rows, params, douts, tm=ROW_TILE):
    R = rows[0][0].shape[0]
    tm = _pick(R, tm)
    n_r, n_p, n_d = len(rows), len(params), len(douts)

    def body(*refs):
        vals = [r[...] for r in refs[:n_r + n_p]]
        cts = tuple(r[...] for r in refs[n_r + n_p:n_r + n_p + n_d])
        grads = jax.vjp(fn, *vals)[1](cts)
        outs = refs[n_r + n_p + n_d:]
        for o_ref, g in zip(outs[:n_r], grads[:n_r]):
            o_ref[...] = g

        @pl.when(pl.program_id(0) == 0)
        def _():
            for o_ref in outs[n_r:]:
                o_ref[...] = jnp.zeros_like(o_ref)

        for o_ref, g in zip(outs[n_r:], grads[n_r:]):
            o_ref[...] += g

    in_specs = [pl.BlockSpec((tm, w), lambda i, cb=cb: (i, cb)) for (_, w, cb) in rows]
    in_specs += [pl.BlockSpec(p.shape, lambda i: (0, 0)) for p in params]
    in_specs += [pl.BlockSpec((tm, w), lambda i, cb=cb: (i, cb)) for (_, w, cb) in douts]
    out_specs = [pl.BlockSpec((tm, w), lambda i: (i, 0)) for (_, w, _) in rows]
    out_specs += [pl.BlockSpec(p.shape, lambda i: (0, 0)) for p in params]
    out_shape = [jax.ShapeDtypeStruct((R, w), F32) for (_, w, _) in rows]
    out_shape += [jax.ShapeDtypeStruct(p.shape, F32) for p in params]
    res = pl.pallas_call(
        body, name=name, grid=(R // tm,), in_specs=in_specs, out_specs=out_specs, out_shape=out_shape,
        compiler_params=_params(("arbitrary",)),
    )(*[r[0] for r in rows], *params, *[d[0] for d in douts])
    return res[:n_r], res[n_r:]


def _softplus(x):
    return jnp.maximum(x, 0.0) + jnp.log(1.0 + jnp.exp(-jnp.abs(x)))


def _silu(x):
    return x * jax.nn.sigmoid(x)


def _rms_fn(h, w):
    return (h * lax.rsqrt(jnp.mean(h * h, axis=-1, keepdims=True) + NORM_EPS) * w,)


def _rw_prep_fn(pr, pr1, pk, pk1, pv, pv1, ps, ps1, mu_r, mu_k, mu_v, mu_s, w0, w2p, a0, a2p, k_k, k_a):
    r = pr + (pr1 - pr) * mu_r
    k = pk + (pk1 - pk) * mu_k
    v = pv + (pv1 - pv) * mu_v
    s = ps + (ps1 - ps) * mu_s
    lora = lambda x, w: jnp.dot(x.astype(BF16), w.astype(BF16), preferred_element_type=F32)
    w_log = -_softplus(-(w0 + lora(jnp.tanh(s), w2p))) - 0.5
    log_decay = -jnp.exp(w_log)
    a = jax.nn.sigmoid(a0 + lora(s, a2p))
    return r, log_decay, k * (1.0 + (a - 1.0) * k_a), v, k * k_k, a


def _conv_fn(u0, u1, u2, u3, w0, w1, w2, w3):
    return (_silu(u0 * w3 + u1 * w2 + u2 * w1 + u3 * w0),)


def _dn_gate_fn(ps2, a_log_n, dt_n):
    beta_n = jax.nn.sigmoid(ps2)
    g_n = -jnp.exp(a_log_n) * _softplus(ps2 + dt_n)
    lane = lax.broadcasted_iota(jnp.int32, (1, DN_HEAD), 1)
    pick = lambda x, j: jnp.broadcast_to(jnp.sum(x * (lane == j).astype(F32), axis=-1, keepdims=True), x.shape)[None]
    beta = jnp.concatenate([pick(beta_n, h) for h in range(DN_HEADS)], axis=0)
    g = jnp.concatenate([pick(g_n, DN_HEADS + h) for h in range(DN_HEADS)], axis=0)
    return beta, g


def _tril_masks(c):
    t = lax.broadcasted_iota(jnp.int32, (c, c), 0)
    s = lax.broadcasted_iota(jnp.int32, (c, c), 1)
    return s <= t, s < t


def _dot(x, y, cx, cy, prec=HIGH):
    nb = x.ndim - 2
    batch = tuple(range(nb))
    return lax.dot_general(x, y, (((cx + nb,), (cy + nb,)), (batch, batch)), precision=prec,
                           preferred_element_type=F32)


INV_BASE = 8


def _blocked_inv(a, mm, row, col):
    shift = INV_BASE.bit_length() - 1
    d = jnp.where(jnp.right_shift(row, shift) == jnp.right_shift(col, shift), a, 0.0)
    x = (row == col).astype(F32) - d
    p = mm(d, d)
    x = x + mm(x, p)
    x = x + mm(x, mm(p, p))
    size = INV_BASE
    while size < a.shape[-2]:
        same_pair = jnp.right_shift(row, shift + 1) == jnp.right_shift(col, shift + 1)
        lower_left = (jnp.right_shift(row, shift) & 1 == 1) & (jnp.right_shift(col, shift) & 1 == 0)
        x = x - mm(x, mm(jnp.where(same_pair & lower_left, a, 0.0), x))
        size, shift = 2 * size, shift + 1
    return x


def _tri_inv(a):
    c = a.shape[-1]
    row, col = lax.broadcasted_iota(jnp.int32, (c, c), 0), lax.broadcasted_iota(jnp.int32, (c, c), 1)
    return _blocked_inv(a, lambda x, y: _dot(x, y, 1, 0), row, col)


@jax.custom_vjp
def _tri_inv_saved(a, t):
    return t


def _tri_inv_saved_fwd(a, t):
    return t, t


def _tri_inv_saved_bwd(t, dt):
    return -_dot(_dot(t, dt, 0, 0), t, 1, 1), jnp.zeros_like(t)


_tri_inv_saved.defvjp(_tri_inv_saved_fwd, _tri_inv_saved_bwd)


def _pair_masks():
    lane = lax.broadcasted_iota(jnp.int32, (1, 2 * RW_HEAD), 1)
    m0 = (lane < RW_HEAD).astype(F32)
    return m0, 1.0 - m0


def _pair_bd(x):
    m0, m1 = _pair_masks()
    return jnp.concatenate([x * m0, x * m1], axis=-2)


def _pair_mm(x, y):
    return _dot(x, _pair_bd(y), 1, 0)


def _pair_inv(a):
    c = a.shape[-2]
    row = lax.broadcasted_iota(jnp.int32, (c, 2 * RW_HEAD), 0)
    col = lax.broadcasted_iota(jnp.int32, (c, 2 * RW_HEAD), 1) & (RW_HEAD - 1)
    return _blocked_inv(a, _pair_mm, row, col)


@jax.custom_vjp
def _pair_inv_saved(a, t):
    return t


def _pair_inv_saved_fwd(a, t):
    return t, t


def _pair_inv_saved_bwd(t, dt):
    m0, m1 = _pair_masks()
    c = t.shape[-2]
    z = _dot(t, dt, 0, 0)
    x = z[:, :c, :] * m0 + z[:, c:, :] * m1
    return -_dot(x, _pair_bd(t), 1, 1), jnp.zeros_like(t)


_pair_inv_saved.defvjp(_pair_inv_saved_fwd, _pair_inv_saved_bwd)


@jax.custom_vjp
def _use_saved(x, x_saved):
    return x_saved


_use_saved.defvjp(lambda x, x_saved: (x_saved, None), lambda _, g: (g, jnp.zeros_like(g)))


def _rw_chunk_fn(S, r, lw, k2, v, kkp, a, gate, rk, gnw, gnb, saved=None):
    B, C, P = r.shape
    m0, m1 = _pair_masks()
    seg = lambda x: (jnp.sum(x * m0, axis=-1, keepdims=True) * m0 + jnp.sum(x * m1, axis=-1, keepdims=True) * m1)
    mm = lambda x, y: _dot(x.astype(BF16), y.astype(BF16), 1, 0, None)
    nt = lambda x, y: _dot(x, y, 1, 1)
    tn = lambda x, y: _dot(x, y, 0, 0)
    pair_mm = lambda x, y: mm(x, _pair_bd(y))
    t_idx = lax.broadcasted_iota(jnp.int32, (C, P), 0)
    s_idx = lax.broadcasted_iota(jnp.int32, (C, P), 1) & (RW_HEAD - 1)
    incl, strict = s_idx <= t_idx, s_idx < t_idx
    tril = jnp.broadcast_to(_tril_masks(C)[0].astype(F32), (B, C, C))
    kk = kkp * lax.rsqrt(seg(kkp * kkp) + 1e-6)
    b = kk * a
    reuse = (lambda x, i: x) if saved is None else (lambda x, i: _use_saved(x, saved[i]))
    g_incl = reuse(_dot(tril, lw, 1, 0), 1)
    g_excl = g_incl - lw
    inv = jnp.exp(-g_incl)
    alpha, beta, kappa, rho = kk * jnp.exp(g_excl), b * inv, k2 * inv, r * jnp.exp(g_incl)
    ar = jnp.concatenate([alpha, rho], axis=-2)
    scores = reuse(nt(ar, jnp.concatenate([_pair_bd(beta), _pair_bd(kappa)], axis=-2)), 2)
    a_ab = jnp.where(strict, scores[:, :C, :P], 0.0)
    a_ak = jnp.where(strict, scores[:, :C, P:], 0.0)
    r_b = jnp.where(incl, scores[:, C:, :P], 0.0)
    r_k = jnp.where(incl, scores[:, C:, P:], 0.0)
    t_inv = _pair_inv(a_ab) if saved is None else _pair_inv_saved(a_ab, saved[0])
    on_state = reuse(nt(ar, S), 3)
    u = pair_mm(t_inv, -on_state[:, :C, :] - pair_mm(a_ak, v))
    y = on_state[:, C:, :] + mm(jnp.concatenate([r_b, r_k], axis=-1),
                                jnp.concatenate([_pair_bd(u), _pair_bd(v)], axis=-2))
    same_head = ((lax.broadcasted_iota(jnp.int32, (P, P), 0) < RW_HEAD)
                 == (lax.broadcasted_iota(jnp.int32, (P, P), 1) < RW_HEAD))
    fresh = tn(jnp.concatenate([u, v], axis=-2), jnp.concatenate([beta, kappa], axis=-2))
    S_new = jnp.exp(jnp.sum(lw, axis=-2, keepdims=True)) * (S + jnp.where(same_head, fresh, 0.0))
    dev = y - seg(y) * (1.0 / RW_HEAD)
    yn = dev * lax.rsqrt(seg(dev * dev) * (1.0 / RW_HEAD) + RW_GN_EPS) * gnw + gnb
    bonus = seg(r * k2 * rk) * v
    return (yn + bonus) * _silu(gate), S_new, (t_inv, g_incl, scores, on_state)


def _dn_chunk_fn(S, qc, kc, vc, bb, gb, z, nw, saved=None):
    B, C, D = qc.shape
    mm = lambda x, y: _dot(x.astype(BF16), y.astype(BF16), 1, 0, None)
    nt = lambda x, y, p=None: _dot(x, y, 1, 1, p) if p else _dot(x.astype(BF16), y.astype(BF16), 1, 1, None)
    tn = lambda x, y: _dot(x.astype(BF16), y.astype(BF16), 0, 0, None)
    incl, strict = _tril_masks(C)
    q = qc * lax.rsqrt(jnp.sum(qc * qc, axis=-1, keepdims=True) + 1e-6) * (D ** -0.5)
    k = kc * lax.rsqrt(jnp.sum(kc * kc, axis=-1, keepdims=True) + 1e-6)
    kb, vb = k * bb, vc * bb
    G = _dot(jnp.broadcast_to(incl.astype(F32), (B, C, C)), gb, 1, 0, HIGHEST)
    lane = lax.broadcasted_iota(jnp.int32, (C, D), 1)
    e0, e1 = (lane == 0).astype(F32), (lane == 1).astype(F32)
    diff = nt(G * e0 + e1, e0 - G * e1, HIGHEST)
    dmask = jnp.where(incl, jnp.exp(jnp.where(incl, diff, 0.0)), 0.0)
    M = jnp.where(strict, nt(kb, k) * dmask, 0.0)
    T = _tri_inv(M) if saved is None else _tri_inv_saved(M, saved[0])
    eG = jnp.exp(G)
    u = mm(T, vb)
    w = mm(T, kb * eG)
    attn = jnp.where(incl, nt(q, k) * dmask, 0.0)
    v_new = u - mm(w, S)
    o = mm(q * eG, S) + mm(attn, v_new)
    g_last = jnp.sum(gb, axis=-2, keepdims=True)
    S_new = S * jnp.exp(jnp.broadcast_to(g_last, S.shape)) + tn(k * jnp.exp(g_last - G), v_new)
    on = o * lax.rsqrt(jnp.mean(o * o, axis=-1, keepdims=True) + NORM_EPS) * nw
    return on * _silu(z), S_new, (T,)


N_GROUPS = 8
GROUP = 128
HALO = 8


def _groups(x):
    return jnp.concatenate([x[:, g * GROUP:(g + 1) * GROUP][None] for g in range(N_GROUPS)], axis=0)


@functools.partial(jax.custom_vjp, nondiff_argnums=(1,))
def _shift_rows(ext, j):
    return pltpu.roll(ext, j, 0)[HALO:, :]


def _shift_rows_fwd(ext, j):
    return _shift_rows(ext, j), None


def _shift_rows_bwd(j, _, d):
    z = jnp.concatenate([jnp.zeros((HALO, d.shape[1]), d.dtype), d], axis=0)
    return (pltpu.roll(z, z.shape[0] - j, 0),)


_shift_rows.defvjp(_shift_rows_fwd, _shift_rows_bwd)


def _rw_fused_fn(S, h_r, h_k, h_v, h_s, p_r, p_k, p_v, p_s, gate, *pars, saved=None):
    prev = lambda h, x: _shift_rows(jnp.concatenate([h, x], axis=0), 1)
    seq = _rw_prep_fn(p_r, prev(h_r, p_r), p_k, prev(h_k, p_k), p_v, prev(h_v, p_v), p_s, prev(h_s, p_s), *pars[:10])
    return _rw_chunk_fn(S, *[_groups(t) for t in seq], _groups(gate), *[_groups(t) for t in pars[10:]],
                        saved=saved)


def _dn_fused_fn(S, h_q, h_k, h_v, p_q, p_k, p_v, p_s2, z, *pars, saved=None):
    conv = []
    for i, (h, x) in enumerate(((h_q, p_q), (h_k, p_k), (h_v, p_v))):
        ext = jnp.concatenate([h, x], axis=0)
        conv += _conv_fn(x, _shift_rows(ext, 1), _shift_rows(ext, 2), _shift_rows(ext, 3), *pars[4 * i:4 * i + 4])
    beta, g = _dn_gate_fn(p_s2, pars[12], pars[13])
    return _dn_chunk_fn(S, *[_groups(t) for t in conv], beta, g, _groups(z), _groups(pars[14]), saved=saved)


def _chunk_fwd(name, fn, rows, n_halo, pars, saved_shapes):
    Lp = rows[0][0].shape[0]
    C, D, NB = CHUNK, GROUP, N_GROUPS
    nc, n_r, n_p, per = Lp // C, len(rows), len(pars), CHUNK // HALO
    n_s = len(saved_shapes)

    def body(*refs):
        row_refs, halo_refs, par_refs = refs[:n_r], refs[n_r:n_r + n_halo], refs[n_r + n_halo:n_r + n_halo + n_p]
        y_ref, ck_ref = refs[n_r + n_halo + n_p:n_r + n_halo + n_p + 2]
        saved_refs, s_ref = refs[n_r + n_halo + n_p + 2:-1], refs[-1]
        first = pl.program_id(0) == 0

        @pl.when(first)
        def _():
            s_ref[...] = jnp.zeros_like(s_ref)

        S = s_ref[...]
        ck_ref[0] = S
        halos = [jnp.where(first, 0.0, r[...]) for r in halo_refs]
        y, S_new, saved = fn(S, *halos, *[r[...] for r in row_refs], *[r[...] for r in par_refs])
        for g in range(NB):
            y_ref[:, g * D:(g + 1) * D] = y[g].astype(BF16)
        for o_ref, val in zip(saved_refs, saved):
            o_ref[0] = val
        s_ref[...] = S_new

    return pl.pallas_call(
        body, name=name, grid=(nc,),
        in_specs=[pl.BlockSpec((pl.Element(C), pl.Element(w)), lambda c, off=off: (c * C, off)) for (_, w, off) in rows]
        + [pl.BlockSpec((pl.Element(HALO), pl.Element(w)), lambda c, off=off: (pl.multiple_of(jnp.maximum(c * C - HALO, 0), HALO), off))
           for (_, w, off) in rows[:n_halo]]
        + [pl.BlockSpec(p.shape, lambda c: (0, 0)) for p in pars],
        out_specs=[pl.BlockSpec((C, NB * D), lambda c: (c, 0)), pl.BlockSpec((1, NB, D, D), lambda c: (c, 0, 0, 0))]
        + [pl.BlockSpec((1,) + tuple(shp), lambda c: (c, 0, 0, 0)) for shp in saved_shapes],
        out_shape=[jax.ShapeDtypeStruct((Lp, NB * D), BF16), jax.ShapeDtypeStruct((nc, NB, D, D), F32)]
        + [jax.ShapeDtypeStruct((nc,) + tuple(shp), F32) for shp in saved_shapes],
        scratch_shapes=[pltpu.VMEM((NB, D, D), F32)],
        compiler_params=_params(("arbitrary",)),
    )(*[r[0] for r in rows], *[r[0] for r in rows[:n_halo]], *pars)


def _chunk_bwd(name, fn, rows, n_halo, pars, ckpt, saved, dy, out_blocks, into=None):
    Lp = rows[0][0].shape[0]
    C, D, NB = CHUNK, GROUP, N_GROUPS
    nc, n_r, n_p, per = Lp // C, len(rows), len(pars), CHUNK // HALO
    n_s = len(saved)
    n_in = n_r + n_halo + n_p
    width = sum(w for (_, w, _) in rows)
    n_blocks, my_block = out_blocks
    extra = [] if into is None else [into]

    def body(*refs):
        row_refs, halo_refs, par_refs = refs[:n_r], refs[n_r:n_r + n_halo], refs[n_r + n_halo:n_in]
        ck_ref, saved_refs, dy_ref = refs[n_in], refs[n_in + 1:n_in + 1 + n_s], refs[n_in + 1 + n_s]
        outs = refs[n_in + 2 + n_s + len(extra):]
        drows_ref, dpar_refs = outs[0], outs[1:1 + n_p]
        ds_ref, carry_refs = outs[1 + n_p], outs[2 + n_p:]
        i = pl.program_id(0)

        @pl.when(i == 0)
        def _():
            ds_ref[...] = jnp.zeros_like(ds_ref)
            for o_ref in list(dpar_refs) + list(carry_refs):
                o_ref[...] = jnp.zeros_like(o_ref)

        halos = [jnp.where(i == nc - 1, 0.0, r[...]) for r in halo_refs]
        vals = [ck_ref[0]] + halos + [r[...] for r in row_refs] + [r[...] for r in par_refs]
        stored = tuple(r[0] for r in saved_refs)
        grads = jax.vjp(lambda *v: fn(*v, saved=stored)[:2], *vals)[1]((_groups(dy_ref), ds_ref[...]))
        ds_ref[...] = grads[0]
        d_halos, d_rows, d_pars = grads[1:1 + n_halo], grads[1 + n_halo:1 + n_halo + n_r], grads[1 + n_halo + n_r:]
        col = 0
        for k, g in enumerate(d_rows):
            if k < n_halo:
                g = g + jnp.concatenate([jnp.zeros((C - HALO, g.shape[1]), F32), carry_refs[k][...]], axis=0)
                carry_refs[k][...] = d_halos[k]
            drows_ref[:, col:col + g.shape[1]] = g.astype(BF16)
            col += g.shape[1]
        for o_ref, g in zip(dpar_refs, d_pars):
            o_ref[...] += g

    rev = lambda c: nc - 1 - c
    dy_arr, dy_w, dy_cb = dy
    res = pl.pallas_call(
        body, name=name, grid=(nc,),
        in_specs=[pl.BlockSpec((pl.Element(C), pl.Element(w)), lambda c, off=off: (rev(c) * C, off))
                  for (_, w, off) in rows]
        + [pl.BlockSpec((pl.Element(HALO), pl.Element(w)),
                        lambda c, off=off: (pl.multiple_of(jnp.maximum(rev(c) * C - HALO, 0), HALO), off))
           for (_, w, off) in rows[:n_halo]]
        + [pl.BlockSpec(p.shape, lambda c: (0, 0)) for p in pars]
        + [pl.BlockSpec((1, NB, D, D), lambda c: (rev(c), 0, 0, 0))]
        + [pl.BlockSpec((1,) + tuple(t.shape[1:]), lambda c: (rev(c), 0, 0, 0)) for t in saved]
        + [pl.BlockSpec((pl.Element(C), pl.Element(dy_w)), lambda c: (rev(c) * C, dy_cb))]
        + [pl.BlockSpec(memory_space=pl.ANY)] * len(extra),
        out_specs=[pl.BlockSpec((C, width), lambda c: (rev(c), my_block))]
        + [pl.BlockSpec(p.shape, lambda c: (0, 0)) for p in pars],
        out_shape=[jax.ShapeDtypeStruct((Lp, n_blocks * width), BF16)]
        + [jax.ShapeDtypeStruct(p.shape, F32) for p in pars],
        scratch_shapes=[pltpu.VMEM((NB, D, D), F32)] + [pltpu.VMEM((HALO, w), F32) for (_, w, _) in rows[:n_halo]],
        input_output_aliases={} if into is None else {n_in + 2 + n_s: 0},
        compiler_params=_params(("arbitrary",)),
    )(*[r[0] for r in rows], *[r[0] for r in rows[:n_halo]], *pars, ckpt, *saved, dy_arr, *extra)
    return res[0], res[1:]


def _loss_head(h, yw, tgt, fw, n_real, tm=2 * ROW_TILE):
    Lp, Dm = h.shape
    tm = _pick(Lp, tm, 16)

    def out_fn(z, fw_):
        return z * lax.rsqrt(jnp.mean(z * z, axis=-1, keepdims=True) + NORM_EPS) * fw_

    def body(h_ref, yw_ref, t_ref, fw_ref, loss_ref, dz_ref, dz16_ref, dfw_ref):
        i = pl.program_id(0)

        @pl.when(i == 0)
        def _():
            loss_ref[...] = jnp.zeros_like(loss_ref)
            dfw_ref[...] = jnp.zeros_like(dfw_ref)

        row = i * tm + lax.broadcasted_iota(jnp.int32, (tm, 1), 0)
        mask = ((row >= N_META) & (row < n_real)).astype(F32)
        z = h_ref[...] + yw_ref[...]
        o, vjp = jax.vjp(out_fn, z, fw_ref[...])
        err = (o - t_ref[...]) * mask
        row_loss = 0.5 * jnp.mean(jnp.square(err), axis=-1, keepdims=True)
        dz, dfw = vjp(err * (1.0 / Dm))
        loss_ref[...] += jnp.sum(row_loss, axis=0, keepdims=True)
        dz_ref[...] = dz
        dz16_ref[...] = dz.astype(BF16)
        dfw_ref[...] += dfw

    row_spec = pl.BlockSpec((tm, Dm), lambda i: (i, 0))
    return pl.pallas_call(
        body, name="loss_head", grid=(Lp // tm,),
        in_specs=[row_spec, row_spec, row_spec, pl.BlockSpec((1, Dm), lambda i: (0, 0))],
        out_specs=[pl.BlockSpec((8, 128), lambda i: (0, 0)), row_spec, row_spec,
                   pl.BlockSpec((1, Dm), lambda i: (0, 0))],
        out_shape=[jax.ShapeDtypeStruct((8, 128), F32), jax.ShapeDtypeStruct((Lp, Dm), F32),
                   jax.ShapeDtypeStruct((Lp, Dm), BF16), jax.ShapeDtypeStruct((1, Dm), F32)],
        compiler_params=_params(("arbitrary",)),
    )(h, yw, tgt, fw)


def _exchange(name, x, masks, slot_kind, per_dest, n_split=1, copy_own=True):
    n = len(masks)
    keep_own = slot_kind is not None and copy_own
    n_slots = {"chip": 4, "core": 2, "dev": 8, None: n}[slot_kind]
    blk_shape = x.shape[1:] if per_dest else x.shape
    rows = blk_shape[0] // n_split

    def body(x_ref, o_ref, send_sems, recv_sems, local_sems):
        mx, my, mc = lax.axis_index("x"), lax.axis_index("y"), lax.axis_index("c")

        def slot(k, px, py, pc):
            return {"chip": 2 * px + py, "core": pc, "dev": 4 * px + 2 * py + pc, None: k}[slot_kind]

        def peer(m):
            return (mx + m[0]) % 2, (my + m[1]) % 2, (mc + m[2]) % 2

        def part(ref, j):
            return ref.at[pl.ds(j * rows, rows)]

        own_src = x_ref.at[2 * mx + my] if per_dest else x_ref
        local = []
        if keep_own:
            own_dst = o_ref.at[slot(0, mx, my, mc)]
            local = [pltpu.make_async_copy(part(own_src, j), part(own_dst, j), local_sems.at[j])
                     for j in range(n_split)]
        for cp in local:
            cp.start()
        sends = []
        for k, m in enumerate(masks):
            px, py, pc = peer(m)
            src = x_ref.at[2 * px + py] if per_dest else x_ref
            dst = o_ref.at[slot(k, mx, my, mc)]
            for j in range(n_split):
                sends.append(pltpu.make_async_remote_copy(
                    src_ref=part(src, j), dst_ref=part(dst, j), send_sem=send_sems.at[k * n_split + j],
                    recv_sem=recv_sems.at[k * n_split + j], device_id=(px, py, pc), device_id_type=MESH))
        for cp in sends:
            cp.start()
        for k, m in enumerate(masks):
            px, py, pc = peer(m)
            landed = o_ref.at[slot(k, px, py, pc)]
            for j in range(n_split):
                pltpu.make_async_remote_copy(
                    src_ref=part(own_src, j), dst_ref=part(landed, j), send_sem=send_sems.at[k * n_split + j],
                    recv_sem=recv_sems.at[k * n_split + j], device_id=(px, py, pc), device_id_type=MESH).wait_recv()
        for cp in sends:
            cp.wait_send()
        for cp in local:
            cp.wait()

    return pl.pallas_call(
        body, name=name,
        in_specs=[pl.BlockSpec(memory_space=pl.ANY)], out_specs=pl.BlockSpec(memory_space=pl.ANY),
        out_shape=jax.ShapeDtypeStruct((n_slots,) + tuple(blk_shape), x.dtype),
        scratch_shapes=[pltpu.SemaphoreType.DMA((n * n_split,)), pltpu.SemaphoreType.DMA((n * n_split,)),
                        pltpu.SemaphoreType.DMA((n_split,))],
        compiler_params=pltpu.CompilerParams(has_side_effects=True),
    )(x)


CHIP_MASKS = [(1, 0, 0), (0, 1, 0), (1, 1, 0)]
CORE_MASKS = [(0, 0, 1)]
ALL_MASKS = [(dx, dy, dc) for dx in (0, 1) for dy in (0, 1) for dc in (0, 1) if (dx, dy, dc) != (0, 0, 0)]

HBM_SPEC = pl.BlockSpec(memory_space=pltpu.HBM)
SEM_SPEC = pl.BlockSpec(memory_space=pltpu.SEMAPHORE)
DATAFLOW = pltpu.SideEffectType.DATAFLOW_SIDE_EFFECTING


def _split_copies(x_ref, land_ref, send_sems, recv_sems, masks, slot_kind, per_dest, n_split, with_recvs=True):
    mx, my, mc = lax.axis_index("x"), lax.axis_index("y"), lax.axis_index("c")
    if slot_kind == "other_half":
        x_ref = x_ref.at[pl.ds((1 - mc) * land_ref.shape[1], land_ref.shape[1])]
    slot = lambda px, py, pc: {"chip": 2 * px + py, "core": pc, "other_half": 0}[slot_kind]
    rows = (x_ref.shape[1] if per_dest else x_ref.shape[0]) // n_split
    part = lambda ref, j: ref.at[pl.ds(j * rows, rows)]
    sends, recvs = [], []
    for k, m in enumerate(masks):
        px, py, pc = (mx + m[0]) % 2, (my + m[1]) % 2, (mc + m[2]) % 2
        src = x_ref.at[2 * px + py] if per_dest else x_ref
        own_src = x_ref.at[2 * mx + my] if per_dest else x_ref
        for j in range(n_split):
            sems = dict(send_sem=send_sems.at[k * n_split + j], recv_sem=recv_sems.at[k * n_split + j],
                        device_id=(px, py, pc), device_id_type=MESH)
            sends.append(pltpu.make_async_remote_copy(
                src_ref=part(src, j), dst_ref=part(land_ref.at[slot(mx, my, mc)], j), **sems))
            if with_recvs:
                recvs.append(pltpu.make_async_remote_copy(
                    src_ref=part(own_src, j), dst_ref=part(land_ref.at[slot(px, py, pc)], j), **sems))
    return sends, recvs


def _exchange_start(name, x, masks, slot_kind, per_dest, n_split, dep=None):
    n = len(masks) * n_split
    blk_shape = x.shape[1:] if per_dest else x.shape
    if slot_kind == "other_half":
        blk_shape = (x.shape[0] // 2,) + tuple(x.shape[1:])
    land_shape = ({"chip": 4, "core": 2, "other_half": 1}[slot_kind],) + tuple(blk_shape)
    deps = [] if dep is None else [dep]

    def body(x_ref, land_ref, *rest):
        send_sems, recv_sems, x_thru, land_thru, token = rest[len(deps):]
        for cp in _split_copies(x_ref, land_ref, send_sems, recv_sems, masks, slot_kind, per_dest, n_split, False)[0]:
            cp.start()
        token[...] = jnp.zeros_like(token)

    return pl.pallas_call(
        body, name=name,
        out_shape=(pltpu.SemaphoreType.DMA((n,)), pltpu.SemaphoreType.DMA((n,)), pltpu.HBM(x.shape, x.dtype),
                   pltpu.HBM(land_shape, x.dtype), jax.ShapeDtypeStruct((8, 128), F32)),
        in_specs=(HBM_SPEC, HBM_SPEC) + (pl.BlockSpec(memory_space=pl.ANY),) * len(deps),
        out_specs=(SEM_SPEC, SEM_SPEC, HBM_SPEC, HBM_SPEC, pl.BlockSpec(memory_space=pltpu.VMEM)),
        input_output_aliases={0: 2, 1: 3},
        compiler_params=pltpu.CompilerParams(has_side_effects=DATAFLOW),
    )(pltpu.with_memory_space_constraint(x, pltpu.HBM),
      pltpu.with_memory_space_constraint(lax.empty(land_shape, x.dtype), pltpu.HBM), *deps)


def _exchange_wait(name, started, after, masks, slot_kind, per_dest, n_split):
    send_sems, recv_sems, x_thru, land_thru, _ = started

    def body(x_ref, land_ref, send_sems, recv_sems, after_ref, x_out, land_out):
        sends, recvs = _split_copies(x_ref, land_ref, send_sems, recv_sems, masks, slot_kind, per_dest, n_split)
        for cp in sends:
            cp.wait_send()
        for cp in recvs:
            cp.wait_recv()

    return pl.pallas_call(
        body, name=name,
        out_shape=(pltpu.HBM(x_thru.shape, x_thru.dtype), pltpu.HBM(land_thru.shape, land_thru.dtype)),
        in_specs=(HBM_SPEC, HBM_SPEC, SEM_SPEC, SEM_SPEC, pl.BlockSpec(memory_space=pl.ANY)),
        out_specs=(HBM_SPEC, HBM_SPEC), input_output_aliases={0: 0, 1: 1},
        compiler_params=pltpu.CompilerParams(has_side_effects=DATAFLOW),
    )(x_thru, land_thru, send_sems, recv_sems, after)


def _gather_chips(name, x):
    return _exchange(name, x, CHIP_MASKS, "chip", False)


def _gather_shards(name, shard, n_split, overlap=False, dep=None):
    half = shard.shape[0] // 2
    mine = lax.dynamic_slice_in_dim(shard, lax.axis_index("c") * half, half, axis=0)
    if overlap:
        return _exchange_start(name + "_chips", mine, CHIP_MASKS, "chip", False, n_split, dep)
    by_chip = _exchange(name + "_chips", mine, CHIP_MASKS, "chip", False, n_split, copy_own=False)
    return _gather_tail(name, mine, by_chip)


def _gather_tail(name, mine, by_chip):
    c, chip = lax.axis_index("c"), 2 * lax.axis_index("x") + lax.axis_index("y")
    by_chip = lax.dynamic_update_index_in_dim(by_chip, mine, chip, 0)
    both = _exchange(name + "_cores", by_chip, CORE_MASKS, "core", False, N_CHIPS, copy_own=False)
    return lax.dynamic_update_index_in_dim(both, by_chip, c, 0)


def _gather_finish(name, started, after, n_split):
    mine, by_chip = _exchange_wait(name + "_chips_wait", started, after, CHIP_MASKS, "chip", False, n_split)
    return _gather_tail(name, mine, by_chip)


def _sum_slots(name, x, tr=128):
    S, R, N = x.shape
    tr = _pick(R, tr, 16)

    def body(x_ref, o_ref):
        acc = x_ref[0].astype(F32)
        for s in range(1, S):
            acc = acc + x_ref[s].astype(F32)
        o_ref[...] = acc

    return pl.pallas_call(
        body, name=name, grid=(R // tr,),
        in_specs=[pl.BlockSpec((S, tr, N), lambda i: (0, i, 0))], out_specs=pl.BlockSpec((tr, N), lambda i: (i, 0)),
        out_shape=jax.ShapeDtypeStruct((R, N), F32), compiler_params=_params(("parallel",)),
    )(x)


def _add_to_bf16(name, a, b, tr=128):
    S, R, N = a.shape
    tr = _pick(R, tr, 16)

    def body(a_ref, b_ref, o_ref):
        o_ref[...] = (a_ref[...] + b_ref[...]).astype(BF16)

    spec = pl.BlockSpec((S, tr, N), lambda i: (0, i, 0))
    return pl.pallas_call(
        body, name=name, grid=(R // tr,), in_specs=[spec, spec], out_specs=spec,
        out_shape=jax.ShapeDtypeStruct((S, R, N), BF16), compiler_params=_params(("parallel",)),
    )(a, b)


def _add_slabs_to_bf16(name, full, recv, c, tr=64):
    R, NP = full.shape
    half = R // 2
    nb = half // tr

    def body(c_ref, a_ref, b_ref, o_ref):
        x = a_ref[...] + b_ref[...]
        for s in range(N_CHIPS - 1):
            o_ref[s] = x[:, s * SHARD_COLS:(s + 1) * SHARD_COLS].astype(BF16)
        last = jnp.concatenate([x[:, (N_CHIPS - 1) * SHARD_COLS:GAP_AT], x[:, GAP_AT + GAP:]], axis=1)
        o_ref[N_CHIPS - 1] = last.astype(BF16)

    grid_spec = pltpu.PrefetchScalarGridSpec(
        num_scalar_prefetch=1, grid=(nb,),
        in_specs=[pl.BlockSpec((tr, NP), lambda i, c_ref: (c_ref[0] * nb + i, 0)),
                  pl.BlockSpec((tr, NP), lambda i, c_ref: (i, 0))],
        out_specs=pl.BlockSpec((N_CHIPS, tr, SHARD_COLS), lambda i, c_ref: (0, i, 0)))
    return pl.pallas_call(
        body, name=name, grid_spec=grid_spec,
        out_shape=jax.ShapeDtypeStruct((N_CHIPS, half, SHARD_COLS), BF16), compiler_params=_params(("parallel",)),
    )(jnp.reshape(c, (1,)).astype(jnp.int32), full, recv)


def _adamw(name, gparts, w, m, v, tr=128):
    S, R, N = gparts.shape
    tr = _pick(R, tr)
    c1 = 1.0 / (1.0 - ADAM_B1 ** ADAM_STEP)
    c2 = 1.0 / (1.0 - ADAM_B2 ** ADAM_STEP)

    def body(g_ref, w_ref, m_ref, v_ref, go_ref, d_ref, mo_ref, vo_ref):
        g = g_ref[0]
        for s in range(1, S):
            g = g + g_ref[s]
        m_new = ADAM_B1 * m_ref[...] + (1.0 - ADAM_B1) * g
        v_new = ADAM_B2 * v_ref[...] + (1.0 - ADAM_B2) * jnp.square(g)
        go_ref[...] = g
        mo_ref[...] = m_new
        vo_ref[...] = v_new
        d_ref[...] = -ADAM_LR * ((m_new * c1) / (jnp.sqrt(v_new * c2) + ADAM_EPS) + ADAM_WD * w_ref[...])

    spec = pl.BlockSpec((tr, N), lambda i: (i, 0))
    return pl.pallas_call(
        body, name=name, grid=(R // tr,),
        in_specs=[pl.BlockSpec((S, tr, N), lambda i: (0, i, 0)), spec, spec, spec], out_specs=[spec] * 4,
        out_shape=[jax.ShapeDtypeStruct((R, N), F32)] * 4, compiler_params=_params(("parallel",)),
    )(gparts, w, m, v)


def _reduce_to_shard(name, slabs, n_split, overlap=False):
    c = lax.axis_index("c")
    _, R, N = slabs.shape
    half = R // 2
    halves = slabs.reshape(N_CHIPS, 2, half, N)
    mine = lax.dynamic_index_in_dim(halves, c, axis=1, keepdims=False)
    theirs = lax.dynamic_index_in_dim(halves, 1 - c, axis=1, keepdims=False)
    from_sibling = _exchange(name + "_sib", theirs, CORE_MASKS, None, False, N_CHIPS)[0]
    wire = _add_to_bf16(name + "_add", mine, from_sibling)
    if overlap:
        return _exchange_start(name + "_chips", wire, CHIP_MASKS, "chip", True, n_split), (R, N)
    got = _exchange(name + "_chips", wire, CHIP_MASKS, "chip", True, n_split, copy_own=False)
    return _reduce_tail(name, wire, got, n_split, R, N)


def _reduce_tail(name, wire, got, n_split, R, N):
    c, chip = lax.axis_index("c"), 2 * lax.axis_index("x") + lax.axis_index("y")
    got = lax.dynamic_update_index_in_dim(got, lax.dynamic_index_in_dim(wire, chip, 0, keepdims=False), chip, 0)
    part = _sum_slots(name + "_sum", got)
    both = _exchange(name + "_cores", part, CORE_MASKS, "core", False, n_split, copy_own=False)
    return lax.dynamic_update_index_in_dim(both, part, c, 0).reshape(1, R, N)


def _reduce_finish(name, started, after, n_split):
    handle, (R, N) = started
    wire, got = _exchange_wait(name + "_chips_wait", handle, after, CHIP_MASKS, "chip", True, n_split)
    return _reduce_tail(name, wire, got, n_split, R, N)


def _pack(pieces, cols, row_mult=8):
    flat = jnp.concatenate([p.reshape(-1) for p in pieces])
    rows = -(-flat.shape[0] // cols)
    rows = -(-rows // row_mult) * row_mult
    return jnp.pad(flat, (0, rows * cols - flat.shape[0])).reshape(rows, cols)


def _unpack(packed, shapes):
    flat = packed.reshape(-1)
    out, off = [], 0
    for shp in shapes:
        n = 1
        for d in shp:
            n *= d
        out.append(flat[off:off + n].reshape(shp))
        off += n
    return out


def kernel(x, meta_tokens, norm_w, w_in, rw_shift_mu, rw_w0, rw_w2, rw_a0, rw_a2, rw_k_k, rw_k_a, rw_r_k, rw_gn_w, rw_gn_b, dn_conv_w, dn_A_log, dn_dt_bias, dn_norm_w, w_out, final_norm_w, loss_target, m_meta_tokens, m_norm_w, m_w_in, m_rw_shift_mu, m_rw_w0, m_rw_w2, m_rw_a0, m_rw_a2, m_rw_k_k, m_rw_k_a, m_rw_r_k, m_rw_gn_w, m_rw_gn_b, m_dn_conv_w, m_dn_A_log, m_dn_dt_bias, m_dn_norm_w, m_w_out, m_final_norm_w, v_meta_tokens, v_norm_w, v_w_in, v_rw_shift_mu, v_rw_w0, v_rw_w2, v_rw_a0, v_rw_a2, v_rw_k_k, v_rw_k_a, v_rw_r_k, v_rw_gn_w, v_rw_gn_b, v_dn_conv_w, v_dn_A_log, v_dn_dt_bias, v_dn_norm_w, v_w_out, v_final_norm_w):
    S = x.shape[1]
    L = N_META + S
    Lp = -(-L // CHUNK) * CHUNK

    small_shapes = [(RW_LORA, 256), (RW_LORA, 256), (CONV_W, 768), (N_META, 512)]
    small_mine = _pack([rw_w2[0], rw_a2[0], dn_conv_w[0], meta_tokens], 1024)
    small_all = _gather_chips("gather_small", small_mine)
    per_chip = [_unpack(small_all[s], small_shapes) for s in range(N_CHIPS)]
    w2, a2, conv_w, meta = [jnp.concatenate([per_chip[s][i] for s in range(N_CHIPS)], axis=1) for i in range(4)]
    w_in_started = _gather_shards("gather_w_in", w_in[0].astype(BF16), 8, overlap=True, dep=small_all)
    w_out_started = _gather_shards("gather_w_out", w_out[0].astype(BF16), 8, overlap=True, dep=w_in_started[4])

    tail = [jnp.zeros((Lp - L, D_MODEL), F32)] if Lp > L else []
    h = jnp.concatenate([meta + w_in_started[4][:1, :1], x[0]] + tail, axis=0)
    tgt = jnp.concatenate([jnp.zeros((N_META, D_MODEL), F32), loss_target[0]] + tail, axis=0)
    (u,) = _rowwise("rms_in", _rms_fn, [_row(h)], [norm_w], [D_MODEL], tm=2 * ROW_TILE, out_dtype=BF16)
    w_in_all = _gather_finish("gather_w_in", w_in_started, u, 8)
    slabs = [w_in_all[:, s].reshape(D_MODEL, SHARD_COLS) for s in range(N_CHIPS)]
    cut = GAP_AT - (N_CHIPS - 1) * SHARD_COLS
    W_gapped = jnp.concatenate(slabs[:-1] + [slabs[-1][:, :cut], jnp.zeros((D_MODEL, GAP), BF16), slabs[-1][:, cut:]],
                               axis=1)
    p = _mm("in_proj", u, W_gapped, "nn", dep=w_out_started[4])

    mu = rw_shift_mu
    zpad = jnp.zeros((RW_LORA, RW_WIDTH), F32)
    rw_params = [mu[:, 0:1024], mu[:, 1024:2048], mu[:, 2048:3072], mu[:, 3072:3200], rw_w0,
                 jnp.concatenate([w2, zpad], axis=0), rw_a0, jnp.concatenate([zpad, a2], axis=0), rw_k_k, rw_k_a]
    rw_rows = [_row(p, 1024, OFF_R), _row(p, 1024, OFF_K), _row(p, 1024, OFF_V), _row(p, 128, OFF_S1),
               _row(p, 1024, OFF_GATE)]
    rw_pars = rw_params + [rw_r_k, rw_gn_w, rw_gn_b]
    rw_saved_shapes = [(N_GROUPS, CHUNK, GROUP), (N_GROUPS, CHUNK, GROUP), (N_GROUPS, 2 * CHUNK, 2 * GROUP),
                       (N_GROUPS, 2 * CHUNK, GROUP)]
    ya, rw_ck, *rw_t = _chunk_fwd("rw_chunk_fwd", _rw_fused_fn, rw_rows, 4, rw_pars, rw_saved_shapes)

    dn_rows = [_row(p, 1024, OFF_DQ), _row(p, 1024, OFF_DK), _row(p, 1024, OFF_DV), _row(p, 128, OFF_S2),
               _row(p, 1024, OFF_Z)]
    dn_pars = [conv_w[j:j + 1, 1024 * i:1024 * (i + 1)] for i in range(3) for j in range(CONV_W)]
    narrow = lambda t: jnp.pad(t, ((0, 0), (DN_HEADS, DN_HEAD - 2 * DN_HEADS)))
    dn_pars += [narrow(dn_A_log), narrow(dn_dt_bias), jnp.tile(dn_norm_w, (1, DN_HEADS))]
    yb, dn_ck, *dn_t = _chunk_fwd("dn_chunk_fwd", _dn_fused_fn, dn_rows, 3, dn_pars, [(N_GROUPS, CHUNK, CHUNK)])

    y = jnp.concatenate([ya, yb], axis=1)
    w_out_all = _gather_finish("gather_w_out", w_out_started, y, 8)
    Wo = jnp.concatenate([w_out_all[:, s].reshape(D_MODEL // N_CHIPS, D_MODEL) for s in range(N_CHIPS)], axis=0)
    yw = _mm("out_proj", y, Wo, "nn", tn=1024)
    loss_acc, dz, dz16, d_fw = _loss_head(h, yw, tgt, final_norm_w.reshape(1, D_MODEL), L)
    loss = lax.psum(loss_acc[0, 0], ("x", "y", "c"))

    d_wo = _mm("d_w_out", y.T, dz16, "nn", tm=512, tn=1024, tk=Lp)
    wo_started = _reduce_to_shard("rs_w_out", d_wo.reshape(N_CHIPS, D_MODEL // N_CHIPS, D_MODEL), 8, overlap=True)
    dy = _mm("d_out_proj", dz16, Wo, "nt", tn=1024, tk=2048, dep=wo_started[0][4])

    dp, d_rw_pars = _chunk_bwd("rw_chunk_bwd", _rw_fused_fn, rw_rows, 4, rw_pars, rw_ck, rw_t, _row(dy, 1024, 0),
                               (2, 0))
    d_prep_pars, d_rw_pars = d_rw_pars[:10], d_rw_pars[10:]
    dp, d_dn_pars = _chunk_bwd("dn_chunk_bwd", _dn_fused_fn, dn_rows, 3, dn_pars, dn_ck, dn_t, _row(dy, 1024, RW_WIDTH),
                               (2, 1), into=dp)
    d_conv_parts = [jnp.concatenate(d_dn_pars[4 * i:4 * i + 4], axis=0) for i in range(3)]
    d_a_log_b, d_dt_b = d_dn_pars[12], d_dn_pars[13]
    d_dn_nw = jnp.sum(d_dn_pars[14].reshape(DN_HEADS, DN_HEAD), axis=0, keepdims=True)
    d_W = _mm("d_w_in", u.T, dp, "nn", tm=1024, tn=768, tk=Lp)
    sib = _exchange_start("rs_w_in_sib", d_W, CORE_MASKS, "other_half", False, 8)
    du = _mm("d_in_proj_a", dp, W_gapped, "nt", tn=1024, tk=1408, dep=sib[4], n_blocks=(0, 1))
    d_W, from_sibling = _exchange_wait("rs_w_in_sib_wait", sib, du, CORE_MASKS, "other_half", False, 8)
    wire = _add_slabs_to_bf16("rs_w_in_add", d_W, from_sibling[0], lax.axis_index("c"))
    wi_started = (_exchange_start("rs_w_in_chips", wire, CHIP_MASKS, "chip", True, 8), (D_MODEL, SHARD_COLS))
    du = _mm("d_in_proj_b", dp, W_gapped, "nt", tn=1024, tk=1408, dep=wi_started[0][4], n_blocks=(1, 1), into=du)
    (dh,), (d_norm_w,) = _rowwise_bwd("rms_in_bwd", lambda h_, w_: (_rms_fn(h_, w_)[0], h_), [_row(h)], [norm_w],
                                      [_row(du), _row(dz)])
    grad_x = dh[N_META:L][None]

    d_mu = jnp.concatenate(d_prep_pars[0:4], axis=1)
    d_w2, d_a2 = d_prep_pars[5][:RW_LORA], d_prep_pars[7][RW_LORA:]
    d_conv = jnp.concatenate(d_conv_parts, axis=1)
    d_meta = dh[:N_META]
    head_sum = lambda t: t[:, DN_HEADS:2 * DN_HEADS]
    rep_names = ["norm_w", "rw_shift_mu", "rw_w0", "rw_a0", "rw_k_k", "rw_k_a", "rw_r_k", "rw_gn_w", "rw_gn_b",
                 "dn_A_log", "dn_dt_bias", "dn_norm_w", "final_norm_w"]
    rep_g = [d_norm_w, d_mu, d_prep_pars[4], d_prep_pars[6], d_prep_pars[8], d_prep_pars[9],
             d_rw_pars[0], d_rw_pars[1], d_rw_pars[2],
             head_sum(d_a_log_b), head_sum(d_dt_b), d_dn_nw, d_fw.reshape(D_MODEL)]
    rep_w = [norm_w, rw_shift_mu, rw_w0, rw_a0, rw_k_k, rw_k_a, rw_r_k, rw_gn_w, rw_gn_b, dn_A_log, dn_dt_bias,
             dn_norm_w, final_norm_w]
    rep_m = [m_norm_w, m_rw_shift_mu, m_rw_w0, m_rw_a0, m_rw_k_k, m_rw_k_a, m_rw_r_k, m_rw_gn_w, m_rw_gn_b,
             m_dn_A_log, m_dn_dt_bias, m_dn_norm_w, m_final_norm_w]
    rep_v = [v_norm_w, v_rw_shift_mu, v_rw_w0, v_rw_a0, v_rw_k_k, v_rw_k_a, v_rw_r_k, v_rw_gn_w, v_rw_gn_b,
             v_dn_A_log, v_dn_dt_bias, v_dn_norm_w, v_final_norm_w]
    rep_shapes = [t.shape for t in rep_w]
    rep_all = _exchange("gather_rep_grads", _pack(rep_g, 128), ALL_MASKS, "dev", False)
    rep_out = _adamw("adam_rep", rep_all, _pack(rep_w, 128), _pack(rep_m, 128), _pack(rep_v, 128))
    rep_out = [dict(zip(rep_names, _unpack(t, rep_shapes))) for t in rep_out]

    sm_slabs = jnp.stack([_pack([d_w2[:, 256 * s:256 * (s + 1)], d_a2[:, 256 * s:256 * (s + 1)],
                                 d_conv[:, 768 * s:768 * (s + 1)], d_meta[:, 512 * s:512 * (s + 1)]], 1024, 64)
                          for s in range(N_CHIPS)])
    sm_parts = _reduce_to_shard("rs_small", sm_slabs, 1)
    sm_w = [rw_w2[0], rw_a2[0], dn_conv_w[0], meta_tokens]
    sm_m = [m_rw_w2[0], m_rw_a2[0], m_dn_conv_w[0], m_meta_tokens]
    sm_v = [v_rw_w2[0], v_rw_a2[0], v_dn_conv_w[0], v_meta_tokens]
    sm_out = _adamw("adam_small", sm_parts, _pack(sm_w, 1024, 64), _pack(sm_m, 1024, 64), _pack(sm_v, 1024, 64))
    sm_names = ["rw_w2", "rw_a2", "dn_conv_w", "meta_tokens"]
    sm_full_shapes = [(1, RW_LORA, 256), (1, RW_LORA, 256), (1, CONV_W, 768), (N_META, 512)]
    sm_out = [dict(zip(sm_names, [t.reshape(shp) for t, shp in zip(_unpack(o, small_shapes), sm_full_shapes)]))
              for o in sm_out]

    wo_parts = _reduce_finish("rs_w_out", wo_started, dp, 8)
    wo_out = _adamw("adam_w_out", wo_parts, w_out[0], m_w_out[0], v_w_out[0])
    wi_parts = _reduce_finish("rs_w_in", wi_started, dh, 8)
    wi_out = _adamw("adam_w_in", wi_parts, w_in[0], m_w_in[0], v_w_in[0])

    order = ["meta_tokens", "norm_w", "w_in", "rw_shift_mu", "rw_w0", "rw_w2", "rw_a0", "rw_a2", "rw_k_k", "rw_k_a",
             "rw_r_k", "rw_gn_w", "rw_gn_b", "dn_conv_w", "dn_A_log", "dn_dt_bias", "dn_norm_w", "w_out",
             "final_norm_w"]
    outs = [loss, grad_x]
    for kind in range(4):
        table = dict(rep_out[kind])
        table.update(sm_out[kind])
        table["w_in"] = wi_out[kind][None]
        table["w_out"] = wo_out[kind][None]
        outs += [table[n] for n in order]
    return tuple(outs)
```

```python
import functools

import jax
import jax.numpy as jnp
from jax import lax
from jax.experimental import pallas as pl
from jax.experimental.pallas import tpu as pltpu

F32 = jnp.float32
BF16 = jnp.bfloat16
HIGH = lax.Precision.HIGH
HIGHEST = lax.Precision.HIGHEST
MESH = pl.DeviceIdType.MESH

D_MODEL = 2048
N_META = 16
RW_WIDTH = 1024
RW_HEAD = 64
RW_HEADS = 16
RW_LORA = 64
RW_GN_EPS = 64e-5
DN_WIDTH = 1024
DN_HEAD = 128
DN_HEADS = 8
CONV_W = 4
CHUNK = 64
NORM_EPS = 1e-6
IN_COLS = 8336
N_CHIPS = 4
SHARD_COLS = IN_COLS // N_CHIPS

NP_COLS = 8 * 1024 + 256
GAP_AT = IN_COLS - DN_WIDTH
GAP = NP_COLS - IN_COLS
OFF_R, OFF_K, OFF_V, OFF_S1, OFF_GATE = 0, 1024, 2048, 3072, 3200
OFF_DQ, OFF_DK, OFF_DV, OFF_S2, OFF_Z = 4224, 5248, 6272, 7296, 7424

ADAM_LR = 0.001
ADAM_B1 = 0.9
ADAM_B2 = 0.999
ADAM_EPS = 1e-08
ADAM_WD = 0.01
ADAM_STEP = 10

VMEM_LIMIT_BYTES = 56 * 1024 * 1024
ROW_TILE = 104
MM_ROW_TILE = 832


def _params(sem=None):
    return pltpu.CompilerParams(dimension_semantics=sem, vmem_limit_bytes=VMEM_LIMIT_BYTES)


def _pick(n, target, mult=8):
    best = None
    for d in range(mult, min(n, target) + 1, mult):
        if n % d == 0:
            best = d
    return n if best is None else best


def _mm(name, a, b, mode, tm=MM_ROW_TILE, tn=1408, tk=2048, dep=None):
    if mode == "nn":
        (M, K), (_, N) = a.shape, b.shape
    elif mode == "nt":
        (M, K), (N, _) = a.shape, b.shape
    else:
        (K, M), (_, N) = a.shape, b.shape
    tm = _pick(M, tm, 128 if mode == "tn" else 16)
    tn = _pick(N, tn, 128)
    tk = _pick(K, tk, 8 if mode == "tn" else 128)
    if mode == "nn":
        a_spec = pl.BlockSpec((tm, tk), lambda i, j, k: (i, k))
        b_spec = pl.BlockSpec((tk, tn), lambda i, j, k: (k, j))
        dims = (((1,), (0,)), ((), ()))
    elif mode == "nt":
        a_spec = pl.BlockSpec((tm, tk), lambda i, j, k: (i, k))
        b_spec = pl.BlockSpec((tn, tk), lambda i, j, k: (j, k))
        dims = (((1,), (1,)), ((), ()))
    else:
        a_spec = pl.BlockSpec((tk, tm), lambda i, j, k: (k, i))
        b_spec = pl.BlockSpec((tk, tn), lambda i, j, k: (k, j))
        dims = (((0,), (0,)), ((), ()))

    def body(a_ref, b_ref, *rest):
        o_ref = rest[-1]

        @pl.when(pl.program_id(2) == 0)
        def _():
            o_ref[...] = jnp.zeros_like(o_ref)

        o_ref[...] += lax.dot_general(a_ref[...].astype(BF16), b_ref[...].astype(BF16), dims,
                                      preferred_element_type=F32)

    deps = [] if dep is None else [dep]
    return pl.pallas_call(
        body, name=name, grid=(M // tm, N // tn, K // tk),
        in_specs=[a_spec, b_spec] + [pl.BlockSpec((8, 128), lambda i, j, k: (0, 0))] * len(deps),
        out_specs=pl.BlockSpec((tm, tn), lambda i, j, k: (i, j)),
        out_shape=jax.ShapeDtypeStruct((M, N), F32),
        compiler_params=_params(("parallel", "parallel", "arbitrary")),
    )(a, b, *deps)


def _row(arr, width=None, cb=0):
    return (arr, arr.shape[1] if width is None else width, cb)


def _rowwise(name, fn, rows, params, out_widths, tm=ROW_TILE, out_dtype=F32):
    R = rows[0][0].shape[0]
    tm = _pick(R, tm, 16 if out_dtype == BF16 else 8)
    n_r, n_p = len(rows), len(params)

    def body(*refs):
        vals = [r[...] for r in refs[:n_r + n_p]]
        for o_ref, val in zip(refs[n_r + n_p:], fn(*vals)):
            o_ref[...] = val.astype(out_dtype)

    in_specs = [pl.BlockSpec((tm, w), lambda i, cb=cb: (i, cb)) for (_, w, cb) in rows]
    in_specs += [pl.BlockSpec(p.shape, lambda i: (0, 0)) for p in params]
    return pl.pallas_call(
        body, name=name, grid=(R // tm,), in_specs=in_specs,
        out_specs=[pl.BlockSpec((tm, w), lambda i: (i, 0)) for w in out_widths],
        out_shape=[jax.ShapeDtypeStruct((R, w), out_dtype) for w in out_widths],
        compiler_params=_params(("parallel",)),
    )(*[r[0] for r in rows], *params)


def _rowwise_bwd(name, fn, rows, params, douts, tm=ROW_TILE):
    R = rows[0][0].shape[0]
    tm = _pick(R, tm)
    n_r, n_p, n_d = len(rows), len(params), len(douts)

    def body(*refs):
        vals = [r[...] for r in refs[:n_r + n_p]]
        cts = tuple(r[...] for r in refs[n_r + n_p:n_r + n_p + n_d])
        grads = jax.vjp(fn, *vals)[1](cts)
        outs = refs[n_r + n_p + n_d:]
        for o_ref, g in zip(outs[:n_r], grads[:n_r]):
            o_ref[...] = g

        @pl.when(pl.program_id(0) == 0)
        def _():
            for o_ref in outs[n_r:]:
                o_ref[...] = jnp.zeros_like(o_ref)

        for o_ref, g in zip(outs[n_r:], grads[n_r:]):
            o_ref[...] += g

    in_specs = [pl.BlockSpec((tm, w), lambda i, cb=cb: (i, cb)) for (_, w, cb) in rows]
    in_specs += [pl.BlockSpec(p.shape, lambda i: (0, 0)) for p in params]
    in_specs += [pl.BlockSpec((tm, w), lambda i, cb=cb: (i, cb)) for (_, w, cb) in douts]
    out_specs = [pl.BlockSpec((tm, w), lambda i: (i, 0)) for (_, w, _) in rows]
    out_specs += [pl.BlockSpec(p.shape, lambda i: (0, 0)) for p in params]
    out_shape = [jax.ShapeDtypeStruct((R, w), F32) for (_, w, _) in rows]
    out_shape += [jax.ShapeDtypeStruct(p.shape, F32) for p in params]
    res = pl.pallas_call(
        body, name=name, grid=(R // tm,), in_specs=in_specs, out_specs=out_specs, out_shape=out_shape,
        compiler_params=_params(("arbitrary",)),
    )(*[r[0] for r in rows], *params, *[d[0] for d in douts])
    return res[:n_r], res[n_r:]


def _softplus(x):
    return jnp.maximum(x, 0.0) + jnp.log(1.0 + jnp.exp(-jnp.abs(x)))


def _silu(x):
    return x * jax.nn.sigmoid(x)


def _rms_fn(h, w):
    return (h * lax.rsqrt(jnp.mean(h * h, axis=-1, keepdims=True) + NORM_EPS) * w,)


def _rw_prep_fn(pr, pr1, pk, pk1, pv, pv1, ps, ps1, mu_r, mu_k, mu_v, mu_s, w0, w2p, a0, a2p, k_k, k_a):
    r = pr + (pr1 - pr) * mu_r
    k = pk + (pk1 - pk) * mu_k
    v = pv + (pv1 - pv) * mu_v
    s = ps + (ps1 - ps) * mu_s
    lora = lambda x, w: jnp.dot(x.astype(BF16), w.astype(BF16), preferred_element_type=F32)
    w_log = -_softplus(-(w0 + lora(jnp.tanh(s), w2p))) - 0.5
    log_decay = -jnp.exp(w_log)
    a = jax.nn.sigmoid(a0 + lora(s, a2p))
    return r, log_decay, k * (1.0 + (a - 1.0) * k_a), v, k * k_k, a


def _conv_fn(u0, u1, u2, u3, w0, w1, w2, w3):
    return (_silu(u0 * w3 + u1 * w2 + u2 * w1 + u3 * w0),)


def _dn_gate_fn(ps2, a_log_n, dt_n):
    beta_n = jax.nn.sigmoid(ps2)
    g_n = -jnp.exp(a_log_n) * _softplus(ps2 + dt_n)
    lane = lax.broadcasted_iota(jnp.int32, (1, DN_HEAD), 1)
    pick = lambda x, j: jnp.broadcast_to(jnp.sum(x * (lane == j).astype(F32), axis=-1, keepdims=True), x.shape)[None]
    beta = jnp.concatenate([pick(beta_n, h) for h in range(DN_HEADS)], axis=0)
    g = jnp.concatenate([pick(g_n, DN_HEADS + h) for h in range(DN_HEADS)], axis=0)
    return beta, g


def _tril_masks(c):
    t = lax.broadcasted_iota(jnp.int32, (c, c), 0)
    s = lax.broadcasted_iota(jnp.int32, (c, c), 1)
    return s <= t, s < t


def _dot(x, y, cx, cy, prec=HIGH):
    nb = x.ndim - 2
    batch = tuple(range(nb))
    return lax.dot_general(x, y, (((cx + nb,), (cy + nb,)), (batch, batch)), precision=prec,
                           preferred_element_type=F32)


INV_BASE = 8


def _blocked_inv(a, mm, row, col, mm_merge=None):
    mm_merge = mm if mm_merge is None else mm_merge
    shift = INV_BASE.bit_length() - 1
    d = jnp.where(jnp.right_shift(row, shift) == jnp.right_shift(col, shift), a, 0.0)
    x = (row == col).astype(F32) - d
    p = mm(d, d)
    x = x + mm(x, p)
    x = x + mm(x, mm(p, p))
    size = INV_BASE
    while size < a.shape[-2]:
        same_pair = jnp.right_shift(row, shift + 1) == jnp.right_shift(col, shift + 1)
        lower_left = (jnp.right_shift(row, shift) & 1 == 1) & (jnp.right_shift(col, shift) & 1 == 0)
        x = x - mm_merge(x, mm_merge(jnp.where(same_pair & lower_left, a, 0.0), x))
        size, shift = 2 * size, shift + 1
    return x


def _tri_inv(a):
    c = a.shape[-1]
    row, col = lax.broadcasted_iota(jnp.int32, (c, c), 0), lax.broadcasted_iota(jnp.int32, (c, c), 1)
    return _blocked_inv(a, lambda x, y: _dot(x, y, 1, 0), row, col,
                        lambda x, y: _dot(x.astype(BF16), y.astype(BF16), 1, 0, None))


@jax.custom_vjp
def _tri_inv_saved(a, t):
    return t


def _tri_inv_saved_fwd(a, t):
    return t, t


def _tri_inv_saved_bwd(t, dt):
    return -_dot(_dot(t, dt, 0, 0), t, 1, 1), jnp.zeros_like(t)


_tri_inv_saved.defvjp(_tri_inv_saved_fwd, _tri_inv_saved_bwd)


def _pair_masks():
    lane = lax.broadcasted_iota(jnp.int32, (1, 2 * RW_HEAD), 1)
    m0 = (lane < RW_HEAD).astype(F32)
    return m0, 1.0 - m0


def _pair_bd(x):
    m0, m1 = _pair_masks()
    return jnp.concatenate([x * m0, x * m1], axis=-2)


def _pair_mm(x, y):
    return _dot(x, _pair_bd(y), 1, 0)


def _pair_inv(a):
    c = a.shape[-2]
    row = lax.broadcasted_iota(jnp.int32, (c, 2 * RW_HEAD), 0)
    col = lax.broadcasted_iota(jnp.int32, (c, 2 * RW_HEAD), 1) & (RW_HEAD - 1)
    return _blocked_inv(a, _pair_mm, row, col,
                        lambda x, y: _dot(x.astype(BF16), _pair_bd(y).astype(BF16), 1, 0, None))


@jax.custom_vjp
def _pair_inv_saved(a, t):
    return t


def _pair_inv_saved_fwd(a, t):
    return t, t


def _pair_inv_saved_bwd(t, dt):
    m0, m1 = _pair_masks()
    c = t.shape[-2]
    z = _dot(t, dt, 0, 0)
    x = z[:, :c, :] * m0 + z[:, c:, :] * m1
    return -_dot(x, _pair_bd(t), 1, 1), jnp.zeros_like(t)


_pair_inv_saved.defvjp(_pair_inv_saved_fwd, _pair_inv_saved_bwd)


@jax.custom_vjp
def _use_saved(x, x_saved):
    return x_saved


_use_saved.defvjp(lambda x, x_saved: (x_saved, None), lambda _, g: (g, jnp.zeros_like(g)))


def _rw_chunk_fn(S, r, lw, k2, v, kkp, a, gate, rk, gnw, gnb, saved=None):
    B, C, P = r.shape
    m0, m1 = _pair_masks()
    seg = lambda x: (jnp.sum(x * m0, axis=-1, keepdims=True) * m0 + jnp.sum(x * m1, axis=-1, keepdims=True) * m1)
    mm = lambda x, y: _dot(x.astype(BF16), y.astype(BF16), 1, 0, None)
    nt = lambda x, y: _dot(x, y, 1, 1)
    tn = lambda x, y: _dot(x, y, 0, 0)
    pair_mm = lambda x, y: mm(x, _pair_bd(y))
    t_idx = lax.broadcasted_iota(jnp.int32, (C, P), 0)
    s_idx = lax.broadcasted_iota(jnp.int32, (C, P), 1) & (RW_HEAD - 1)
    incl, strict = s_idx <= t_idx, s_idx < t_idx
    tril = jnp.broadcast_to(_tril_masks(C)[0].astype(F32), (B, C, C))
    kk = kkp * lax.rsqrt(seg(kkp * kkp) + 1e-6)
    b = kk * a
    reuse = (lambda x, i: x) if saved is None else (lambda x, i: _use_saved(x, saved[i]))
    g_incl = reuse(_dot(tril, lw, 1, 0), 1)
    g_excl = g_incl - lw
    inv = jnp.exp(-g_incl)
    alpha, beta, kappa, rho = kk * jnp.exp(g_excl), b * inv, k2 * inv, r * jnp.exp(g_incl)
    ar = jnp.concatenate([alpha, rho], axis=-2)
    scores = reuse(nt(ar, jnp.concatenate([_pair_bd(beta), _pair_bd(kappa)], axis=-2)), 2)
    a_ab = jnp.where(strict, scores[:, :C, :P], 0.0)
    a_ak = jnp.where(strict, scores[:, :C, P:], 0.0)
    r_b = jnp.where(incl, scores[:, C:, :P], 0.0)
    r_k = jnp.where(incl, scores[:, C:, P:], 0.0)
    t_inv = _pair_inv(a_ab) if saved is None else _pair_inv_saved(a_ab, saved[0])
    on_state = reuse(nt(ar, S), 3)
    u = pair_mm(t_inv, -on_state[:, :C, :] - pair_mm(a_ak, v))
    y = on_state[:, C:, :] + mm(jnp.concatenate([r_b, r_k], axis=-1),
                                jnp.concatenate([_pair_bd(u), _pair_bd(v)], axis=-2))
    same_head = ((lax.broadcasted_iota(jnp.int32, (P, P), 0) < RW_HEAD)
                 == (lax.broadcasted_iota(jnp.int32, (P, P), 1) < RW_HEAD))
    fresh = tn(jnp.concatenate([u, v], axis=-2), jnp.concatenate([beta, kappa], axis=-2))
    S_new = jnp.exp(jnp.sum(lw, axis=-2, keepdims=True)) * (S + jnp.where(same_head, fresh, 0.0))
    dev = y - seg(y) * (1.0 / RW_HEAD)
    yn = dev * lax.rsqrt(seg(dev * dev) * (1.0 / RW_HEAD) + RW_GN_EPS) * gnw + gnb
    bonus = seg(r * k2 * rk) * v
    return (yn + bonus) * _silu(gate), S_new, (t_inv, g_incl, scores, on_state)


def _dn_chunk_fn(S, qc, kc, vc, bb, gb, z, nw, saved=None):
    B, C, D = qc.shape
    mm = lambda x, y: _dot(x.astype(BF16), y.astype(BF16), 1, 0, None)
    nt = lambda x, y, p=None: _dot(x, y, 1, 1, p) if p else _dot(x.astype(BF16), y.astype(BF16), 1, 1, None)
    tn = lambda x, y: _dot(x.astype(BF16), y.astype(BF16), 0, 0, None)
    incl, strict = _tril_masks(C)
    q = qc * lax.rsqrt(jnp.sum(qc * qc, axis=-1, keepdims=True) + 1e-6) * (D ** -0.5)
    k = kc * lax.rsqrt(jnp.sum(kc * kc, axis=-1, keepdims=True) + 1e-6)
    kb, vb = k * bb, vc * bb
    G = _dot(jnp.broadcast_to(incl.astype(F32), (B, C, C)), gb, 1, 0, HIGHEST)
    G = G if saved is None else _use_saved(G, saved[1])
    lane = lax.broadcasted_iota(jnp.int32, (C, D), 1)
    e0, e1 = (lane == 0).astype(F32), (lane == 1).astype(F32)
    diff = nt(G * e0 + e1, e0 - G * e1, HIGHEST)
    dmask = jnp.where(incl, jnp.exp(jnp.where(incl, diff, 0.0)), 0.0)
    M = jnp.where(strict, nt(kb, k) * dmask, 0.0)
    T = _tri_inv(M) if saved is None else _tri_inv_saved(M, saved[0])
    eG = jnp.exp(G)
    u = mm(T, vb)
    w = mm(T, kb * eG)
    attn = jnp.where(incl, nt(q, k) * dmask, 0.0)
    v_new = u - mm(w, S)
    o = mm(q * eG, S) + mm(attn, v_new)
    g_last = jnp.sum(gb, axis=-2, keepdims=True)
    S_new = S * jnp.exp(jnp.broadcast_to(g_last, S.shape)) + tn(k * jnp.exp(g_last - G), v_new)
    on = o * lax.rsqrt(jnp.mean(o * o, axis=-1, keepdims=True) + NORM_EPS) * nw
    return on * _silu(z), S_new, (T, G)


N_GROUPS = 8
GROUP = 128
HALO = 8


def _groups(x):
    return jnp.concatenate([x[:, g * GROUP:(g + 1) * GROUP][None] for g in range(N_GROUPS)], axis=0)


@functools.partial(jax.custom_vjp, nondiff_argnums=(1,))
def _shift_rows(ext, j):
    return pltpu.roll(ext, j, 0)[HALO:, :]


def _shift_rows_fwd(ext, j):
    return _shift_rows(ext, j), None


def _shift_rows_bwd(j, _, d):
    z = jnp.concatenate([jnp.zeros((HALO, d.shape[1]), d.dtype), d], axis=0)
    return (pltpu.roll(z, z.shape[0] - j, 0),)


_shift_rows.defvjp(_shift_rows_fwd, _shift_rows_bwd)


def _rw_fused_fn(S, h_r, h_k, h_v, h_s, p_r, p_k, p_v, p_s, gate, *pars, saved=None):
    prev = lambda h, x: _shift_rows(jnp.concatenate([h, x], axis=0), 1)
    seq = _rw_prep_fn(p_r, prev(h_r, p_r), p_k, prev(h_k, p_k), p_v, prev(h_v, p_v), p_s, prev(h_s, p_s), *pars[:10])
    return _rw_chunk_fn(S, *[_groups(t) for t in seq], _groups(gate), *[_groups(t) for t in pars[10:]],
                        saved=saved)


def _dn_fused_fn(S, h_q, h_k, h_v, p_q, p_k, p_v, p_s2, z, *pars, saved=None):
    conv = []
    for i, (h, x) in enumerate(((h_q, p_q), (h_k, p_k), (h_v, p_v))):
        ext = jnp.concatenate([h, x], axis=0)
        conv += _conv_fn(x, _shift_rows(ext, 1), _shift_rows(ext, 2), _shift_rows(ext, 3), *pars[4 * i:4 * i + 4])
    beta, g = _dn_gate_fn(p_s2, pars[12], pars[13])
    return _dn_chunk_fn(S, *[_groups(t) for t in conv], beta, g, _groups(z), _groups(pars[14]), saved=saved)


def _chunk_fwd(name, fn, rows, n_halo, pars, saved_shapes):
    Lp = rows[0][0].shape[0]
    C, D, NB = CHUNK, GROUP, N_GROUPS
    nc, n_r, n_p, per = Lp // C, len(rows), len(pars), CHUNK // HALO
    n_s = len(saved_shapes)

    def body(*refs):
        row_refs, halo_refs, par_refs = refs[:n_r], refs[n_r:n_r + n_halo], refs[n_r + n_halo:n_r + n_halo + n_p]
        y_ref, ck_ref = refs[n_r + n_halo + n_p:n_r + n_halo + n_p + 2]
        saved_refs, s_ref = refs[n_r + n_halo + n_p + 2:-1], refs[-1]
        first = pl.program_id(0) == 0

        @pl.when(first)
        def _():
            s_ref[...] = jnp.zeros_like(s_ref)

        S = s_ref[...]
        ck_ref[0] = S
        halos = [jnp.where(first, 0.0, r[...]) for r in halo_refs]
        y, S_new, saved = fn(S, *halos, *[r[...] for r in row_refs], *[r[...] for r in par_refs])
        for g in range(NB):
            y_ref[:, g * D:(g + 1) * D] = y[g].astype(BF16)
        for o_ref, val in zip(saved_refs, saved):
            o_ref[0] = val
        s_ref[...] = S_new

    return pl.pallas_call(
        body, name=name, grid=(nc,),
        in_specs=[pl.BlockSpec((pl.Element(C), pl.Element(w)), lambda c, off=off: (c * C, off)) for (_, w, off) in rows]
        + [pl.BlockSpec((pl.Element(HALO), pl.Element(w)), lambda c, off=off: (pl.multiple_of(jnp.maximum(c * C - HALO, 0), HALO), off))
           for (_, w, off) in rows[:n_halo]]
        + [pl.BlockSpec(p.shape, lambda c: (0, 0)) for p in pars],
        out_specs=[pl.BlockSpec((C, NB * D), lambda c: (c, 0)), pl.BlockSpec((1, NB, D, D), lambda c: (c, 0, 0, 0))]
        + [pl.BlockSpec((1,) + tuple(shp), lambda c: (c, 0, 0, 0)) for shp in saved_shapes],
        out_shape=[jax.ShapeDtypeStruct((Lp, NB * D), BF16), jax.ShapeDtypeStruct((nc, NB, D, D), F32)]
        + [jax.ShapeDtypeStruct((nc,) + tuple(shp), F32) for shp in saved_shapes],
        scratch_shapes=[pltpu.VMEM((NB, D, D), F32)],
        compiler_params=_params(("arbitrary",)),
    )(*[r[0] for r in rows], *[r[0] for r in rows[:n_halo]], *pars)


def _chunk_bwd(name, fn, rows, n_halo, pars, ckpt, saved, dy, out_blocks, into=None):
    Lp = rows[0][0].shape[0]
    C, D, NB = CHUNK, GROUP, N_GROUPS
    nc, n_r, n_p, per = Lp // C, len(rows), len(pars), CHUNK // HALO
    n_s = len(saved)
    n_in = n_r + n_halo + n_p
    width = sum(w for (_, w, _) in rows)
    n_blocks, my_block = out_blocks
    extra = [] if into is None else [into]

    def body(*refs):
        row_refs, halo_refs, par_refs = refs[:n_r], refs[n_r:n_r + n_halo], refs[n_r + n_halo:n_in]
        ck_ref, saved_refs, dy_ref = refs[n_in], refs[n_in + 1:n_in + 1 + n_s], refs[n_in + 1 + n_s]
        outs = refs[n_in + 2 + n_s + len(extra):]
        drows_ref, dpar_refs = outs[0], outs[1:1 + n_p]
        ds_ref, carry_refs = outs[1 + n_p], outs[2 + n_p:]
        i = pl.program_id(0)

        @pl.when(i == 0)
        def _():
            ds_ref[...] = jnp.zeros_like(ds_ref)
            for o_ref in list(dpar_refs) + list(carry_refs):
                o_ref[...] = jnp.zeros_like(o_ref)

        halos = [jnp.where(i == nc - 1, 0.0, r[...]) for r in halo_refs]
        vals = [ck_ref[0]] + halos + [r[...] for r in row_refs] + [r[...] for r in par_refs]
        stored = tuple(r[0] for r in saved_refs)
        grads = jax.vjp(lambda *v: fn(*v, saved=stored)[:2], *vals)[1]((_groups(dy_ref), ds_ref[...]))
        ds_ref[...] = grads[0]
        d_halos, d_rows, d_pars = grads[1:1 + n_halo], grads[1 + n_halo:1 + n_halo + n_r], grads[1 + n_halo + n_r:]
        col = 0
        for k, g in enumerate(d_rows):
            if k < n_halo:
                g = g + jnp.concatenate([jnp.zeros((C - HALO, g.shape[1]), F32), carry_refs[k][...]], axis=0)
                carry_refs[k][...] = d_halos[k]
            drows_ref[:, col:col + g.shape[1]] = g.astype(BF16)
            col += g.shape[1]
        for o_ref, g in zip(dpar_refs, d_pars):
            o_ref[...] += g

    rev = lambda c: nc - 1 - c
    dy_arr, dy_w, dy_cb = dy
    res = pl.pallas_call(
        body, name=name, grid=(nc,),
        in_specs=[pl.BlockSpec((pl.Element(C), pl.Element(w)), lambda c, off=off: (rev(c) * C, off))
                  for (_, w, off) in rows]
        + [pl.BlockSpec((pl.Element(HALO), pl.Element(w)),
                        lambda c, off=off: (pl.multiple_of(jnp.maximum(rev(c) * C - HALO, 0), HALO), off))
           for (_, w, off) in rows[:n_halo]]
        + [pl.BlockSpec(p.shape, lambda c: (0, 0)) for p in pars]
        + [pl.BlockSpec((1, NB, D, D), lambda c: (rev(c), 0, 0, 0))]
        + [pl.BlockSpec((1,) + tuple(t.shape[1:]), lambda c: (rev(c), 0, 0, 0)) for t in saved]
        + [pl.BlockSpec((pl.Element(C), pl.Element(dy_w)), lambda c: (rev(c) * C, dy_cb))]
        + [pl.BlockSpec(memory_space=pl.ANY)] * len(extra),
        out_specs=[pl.BlockSpec((C, width), lambda c: (rev(c), my_block))]
        + [pl.BlockSpec(p.shape, lambda c: (0, 0)) for p in pars],
        out_shape=[jax.ShapeDtypeStruct((Lp, n_blocks * width), BF16)]
        + [jax.ShapeDtypeStruct(p.shape, F32) for p in pars],
        scratch_shapes=[pltpu.VMEM((NB, D, D), F32)] + [pltpu.VMEM((HALO, w), F32) for (_, w, _) in rows[:n_halo]],
        input_output_aliases={} if into is None else {n_in + 2 + n_s: 0},
        compiler_params=_params(("arbitrary",)),
    )(*[r[0] for r in rows], *[r[0] for r in rows[:n_halo]], *pars, ckpt, *saved, dy_arr, *extra)
    return res[0], res[1:]


def _loss_head(h, yw, tgt, fw, n_real, tm=2 * ROW_TILE):
    Lp, Dm = h.shape
    tm = _pick(Lp, tm, 16)

    def out_fn(z, fw_):
        return z * lax.rsqrt(jnp.mean(z * z, axis=-1, keepdims=True) + NORM_EPS) * fw_

    def body(h_ref, yw_ref, t_ref, fw_ref, loss_ref, dz_ref, dz16_ref, dfw_ref):
        i = pl.program_id(0)

        @pl.when(i == 0)
        def _():
            loss_ref[...] = jnp.zeros_like(loss_ref)
            dfw_ref[...] = jnp.zeros_like(dfw_ref)

        row = i * tm + lax.broadcasted_iota(jnp.int32, (tm, 1), 0)
        mask = ((row >= N_META) & (row < n_real)).astype(F32)
        z = h_ref[...] + yw_ref[...]
        o, vjp = jax.vjp(out_fn, z, fw_ref[...])
        err = (o - t_ref[...]) * mask
        row_loss = 0.5 * jnp.mean(jnp.square(err), axis=-1, keepdims=True)
        dz, dfw = vjp(err * (1.0 / Dm))
        loss_ref[...] += jnp.sum(row_loss, axis=0, keepdims=True)
        dz_ref[...] = dz
        dz16_ref[...] = dz.astype(BF16)
        dfw_ref[...] += dfw

    row_spec = pl.BlockSpec((tm, Dm), lambda i: (i, 0))
    return pl.pallas_call(
        body, name="loss_head", grid=(Lp // tm,),
        in_specs=[row_spec, row_spec, row_spec, pl.BlockSpec((1, Dm), lambda i: (0, 0))],
        out_specs=[pl.BlockSpec((8, 128), lambda i: (0, 0)), row_spec, row_spec,
                   pl.BlockSpec((1, Dm), lambda i: (0, 0))],
        out_shape=[jax.ShapeDtypeStruct((8, 128), F32), jax.ShapeDtypeStruct((Lp, Dm), F32),
                   jax.ShapeDtypeStruct((Lp, Dm), BF16), jax.ShapeDtypeStruct((1, Dm), F32)],
        compiler_params=_params(("arbitrary",)),
    )(h, yw, tgt, fw)


def _exchange(name, x, masks, slot_kind, per_dest, n_split=1, copy_own=True, other_half=False):
    n = len(masks)
    keep_own = slot_kind is not None and copy_own
    n_slots = {"chip": 4, "core": 2, "dev": 8, None: n}[slot_kind]
    blk_shape = x.shape[1:] if per_dest else x.shape
    if other_half:
        blk_shape = (x.shape[0] // 2,) + tuple(x.shape[1:])
    rows = blk_shape[0] // n_split

    def body(x_ref, o_ref, send_sems, recv_sems, local_sems):
        mx, my, mc = lax.axis_index("x"), lax.axis_index("y"), lax.axis_index("c")
        if other_half:
            x_ref = x_ref.at[pl.ds((1 - mc) * blk_shape[0], blk_shape[0])]

        def slot(k, px, py, pc):
            return {"chip": 2 * px + py, "core": pc, "dev": 4 * px + 2 * py + pc, None: k}[slot_kind]

        def peer(m):
            return (mx + m[0]) % 2, (my + m[1]) % 2, (mc + m[2]) % 2

        def part(ref, j):
            return ref.at[pl.ds(j * rows, rows)]

        own_src = x_ref.at[2 * mx + my] if per_dest else x_ref
        local = []
        if keep_own:
            own_dst = o_ref.at[slot(0, mx, my, mc)]
            local = [pltpu.make_async_copy(part(own_src, j), part(own_dst, j), local_sems.at[j])
                     for j in range(n_split)]
        for cp in local:
            cp.start()
        sends = []
        for k, m in enumerate(masks):
            px, py, pc = peer(m)
            src = x_ref.at[2 * px + py] if per_dest else x_ref
            dst = o_ref.at[slot(k, mx, my, mc)]
            for j in range(n_split):
                sends.append(pltpu.make_async_remote_copy(
                    src_ref=part(src, j), dst_ref=part(dst, j), send_sem=send_sems.at[k * n_split + j],
                    recv_sem=recv_sems.at[k * n_split + j], device_id=(px, py, pc), device_id_type=MESH))
        for cp in sends:
            cp.start()
        for k, m in enumerate(masks):
            px, py, pc = peer(m)
            landed = o_ref.at[slot(k, px, py, pc)]
            for j in range(n_split):
                pltpu.make_async_remote_copy(
                    src_ref=part(own_src, j), dst_ref=part(landed, j), send_sem=send_sems.at[k * n_split + j],
                    recv_sem=recv_sems.at[k * n_split + j], device_id=(px, py, pc), device_id_type=MESH).wait_recv()
        for cp in sends:
            cp.wait_send()
        for cp in local:
            cp.wait()

    return pl.pallas_call(
        body, name=name,
        in_specs=[pl.BlockSpec(memory_space=pl.ANY)], out_specs=pl.BlockSpec(memory_space=pl.ANY),
        out_shape=jax.ShapeDtypeStruct((n_slots,) + tuple(blk_shape), x.dtype),
        scratch_shapes=[pltpu.SemaphoreType.DMA((n * n_split,)), pltpu.SemaphoreType.DMA((n * n_split,)),
                        pltpu.SemaphoreType.DMA((n_split,))],
        compiler_params=pltpu.CompilerParams(has_side_effects=True),
    )(x)


CHIP_MASKS = [(1, 0, 0), (0, 1, 0), (1, 1, 0)]
CORE_MASKS = [(0, 0, 1)]
ALL_MASKS = [(dx, dy, dc) for dx in (0, 1) for dy in (0, 1) for dc in (0, 1) if (dx, dy, dc) != (0, 0, 0)]

HBM_SPEC = pl.BlockSpec(memory_space=pltpu.HBM)
SEM_SPEC = pl.BlockSpec(memory_space=pltpu.SEMAPHORE)
DATAFLOW = pltpu.SideEffectType.DATAFLOW_SIDE_EFFECTING


def _split_copies(x_ref, land_ref, send_sems, recv_sems, masks, slot_kind, per_dest, n_split, with_recvs=True):
    mx, my, mc = lax.axis_index("x"), lax.axis_index("y"), lax.axis_index("c")
    slot = lambda px, py, pc: {"chip": 2 * px + py, "core": pc}[slot_kind]
    rows = (x_ref.shape[1] if per_dest else x_ref.shape[0]) // n_split
    part = lambda ref, j: ref.at[pl.ds(j * rows, rows)]
    sends, recvs = [], []
    for k, m in enumerate(masks):
        px, py, pc = (mx + m[0]) % 2, (my + m[1]) % 2, (mc + m[2]) % 2
        src = x_ref.at[2 * px + py] if per_dest else x_ref
        own_src = x_ref.at[2 * mx + my] if per_dest else x_ref
        for j in range(n_split):
            sems = dict(send_sem=send_sems.at[k * n_split + j], recv_sem=recv_sems.at[k * n_split + j],
                        device_id=(px, py, pc), device_id_type=MESH)
            sends.append(pltpu.make_async_remote_copy(
                src_ref=part(src, j), dst_ref=part(land_ref.at[slot(mx, my, mc)], j), **sems))
            if with_recvs:
                recvs.append(pltpu.make_async_remote_copy(
                    src_ref=part(own_src, j), dst_ref=part(land_ref.at[slot(px, py, pc)], j), **sems))
    return sends, recvs


def _exchange_start(name, x, masks, slot_kind, per_dest, n_split, dep=None):
    n = len(masks) * n_split
    blk_shape = x.shape[1:] if per_dest else x.shape
    land_shape = ({"chip": 4, "core": 2}[slot_kind],) + tuple(blk_shape)
    deps = [] if dep is None else [dep]

    def body(x_ref, land_ref, *rest):
        send_sems, recv_sems, x_thru, land_thru, token = rest[len(deps):]
        for cp in _split_copies(x_ref, land_ref, send_sems, recv_sems, masks, slot_kind, per_dest, n_split, False)[0]:
            cp.start()
        token[...] = jnp.zeros_like(token)

    return pl.pallas_call(
        body, name=name,
        out_shape=(pltpu.SemaphoreType.DMA((n,)), pltpu.SemaphoreType.DMA((n,)), pltpu.HBM(x.shape, x.dtype),
                   pltpu.HBM(land_shape, x.dtype), jax.ShapeDtypeStruct((8, 128), F32)),
        in_specs=(HBM_SPEC, HBM_SPEC) + (pl.BlockSpec(memory_space=pl.ANY),) * len(deps),
        out_specs=(SEM_SPEC, SEM_SPEC, HBM_SPEC, HBM_SPEC, pl.BlockSpec(memory_space=pltpu.VMEM)),
        input_output_aliases={0: 2, 1: 3},
        compiler_params=pltpu.CompilerParams(has_side_effects=DATAFLOW),
    )(pltpu.with_memory_space_constraint(x, pltpu.HBM),
      pltpu.with_memory_space_constraint(lax.empty(land_shape, x.dtype), pltpu.HBM), *deps)


def _exchange_wait(name, started, after, masks, slot_kind, per_dest, n_split):
    send_sems, recv_sems, x_thru, land_thru, _ = started

    def body(x_ref, land_ref, send_sems, recv_sems, after_ref, x_out, land_out):
        sends, recvs = _split_copies(x_ref, land_ref, send_sems, recv_sems, masks, slot_kind, per_dest, n_split)
        for cp in sends:
            cp.wait_send()
        for cp in recvs:
            cp.wait_recv()

    return pl.pallas_call(
        body, name=name,
        out_shape=(pltpu.HBM(x_thru.shape, x_thru.dtype), pltpu.HBM(land_thru.shape, land_thru.dtype)),
        in_specs=(HBM_SPEC, HBM_SPEC, SEM_SPEC, SEM_SPEC, pl.BlockSpec(memory_space=pl.ANY)),
        out_specs=(HBM_SPEC, HBM_SPEC), input_output_aliases={0: 0, 1: 1},
        compiler_params=pltpu.CompilerParams(has_side_effects=DATAFLOW),
    )(x_thru, land_thru, send_sems, recv_sems, after)


def _gather_chips(name, x):
    return _exchange(name, x, CHIP_MASKS, "chip", False)


def _gather_shards(name, shard, n_split, overlap=False, dep=None):
    half = shard.shape[0] // 2
    mine = lax.dynamic_slice_in_dim(shard, lax.axis_index("c") * half, half, axis=0)
    if overlap:
        return _exchange_start(name + "_chips", mine, CHIP_MASKS, "chip", False, n_split, dep)
    by_chip = _exchange(name + "_chips", mine, CHIP_MASKS, "chip", False, n_split, copy_own=False)
    return _gather_tail(name, mine, by_chip)


def _gather_tail(name, mine, by_chip):
    c, chip = lax.axis_index("c"), 2 * lax.axis_index("x") + lax.axis_index("y")
    by_chip = lax.dynamic_update_index_in_dim(by_chip, mine, chip, 0)
    both = _exchange(name + "_cores", by_chip, CORE_MASKS, "core", False, N_CHIPS, copy_own=False)
    return lax.dynamic_update_index_in_dim(both, by_chip, c, 0)


def _gather_finish(name, started, after, n_split):
    mine, by_chip = _exchange_wait(name + "_chips_wait", started, after, CHIP_MASKS, "chip", False, n_split)
    return _gather_tail(name, mine, by_chip)


def _sum_slots(name, x, tr=128):
    S, R, N = x.shape
    tr = _pick(R, tr, 16)

    def body(x_ref, o_ref):
        acc = x_ref[0].astype(F32)
        for s in range(1, S):
            acc = acc + x_ref[s].astype(F32)
        o_ref[...] = acc

    return pl.pallas_call(
        body, name=name, grid=(R // tr,),
        in_specs=[pl.BlockSpec((S, tr, N), lambda i: (0, i, 0))], out_specs=pl.BlockSpec((tr, N), lambda i: (i, 0)),
        out_shape=jax.ShapeDtypeStruct((R, N), F32), compiler_params=_params(("parallel",)),
    )(x)


def _add_to_bf16(name, a, b, tr=128):
    S, R, N = a.shape
    tr = _pick(R, tr, 16)

    def body(a_ref, b_ref, o_ref):
        o_ref[...] = (a_ref[...] + b_ref[...]).astype(BF16)

    spec = pl.BlockSpec((S, tr, N), lambda i: (0, i, 0))
    return pl.pallas_call(
        body, name=name, grid=(R // tr,), in_specs=[spec, spec], out_specs=spec,
        out_shape=jax.ShapeDtypeStruct((S, R, N), BF16), compiler_params=_params(("parallel",)),
    )(a, b)


def _add_slabs_to_bf16(name, full, recv, c, tr=64):
    R, NP = full.shape
    half = R // 2
    nb = half // tr

    def body(c_ref, a_ref, b_ref, o_ref):
        x = a_ref[...] + b_ref[...]
        for s in range(N_CHIPS - 1):
            o_ref[s] = x[:, s * SHARD_COLS:(s + 1) * SHARD_COLS].astype(BF16)
        last = jnp.concatenate([x[:, (N_CHIPS - 1) * SHARD_COLS:GAP_AT], x[:, GAP_AT + GAP:]], axis=1)
        o_ref[N_CHIPS - 1] = last.astype(BF16)

    grid_spec = pltpu.PrefetchScalarGridSpec(
        num_scalar_prefetch=1, grid=(nb,),
        in_specs=[pl.BlockSpec((tr, NP), lambda i, c_ref: (c_ref[0] * nb + i, 0)),
                  pl.BlockSpec((tr, NP), lambda i, c_ref: (i, 0))],
        out_specs=pl.BlockSpec((N_CHIPS, tr, SHARD_COLS), lambda i, c_ref: (0, i, 0)))
    return pl.pallas_call(
        body, name=name, grid_spec=grid_spec,
        out_shape=jax.ShapeDtypeStruct((N_CHIPS, half, SHARD_COLS), BF16), compiler_params=_params(("parallel",)),
    )(jnp.reshape(c, (1,)).astype(jnp.int32), full, recv)


def _adamw(name, gparts, w, m, v, tr=128):
    S, R, N = gparts.shape
    tr = _pick(R, tr)
    c1 = 1.0 / (1.0 - ADAM_B1 ** ADAM_STEP)
    c2 = 1.0 / (1.0 - ADAM_B2 ** ADAM_STEP)

    def body(g_ref, w_ref, m_ref, v_ref, go_ref, d_ref, mo_ref, vo_ref):
        g = g_ref[0]
        for s in range(1, S):
            g = g + g_ref[s]
        m_new = ADAM_B1 * m_ref[...] + (1.0 - ADAM_B1) * g
        v_new = ADAM_B2 * v_ref[...] + (1.0 - ADAM_B2) * jnp.square(g)
        go_ref[...] = g
        mo_ref[...] = m_new
        vo_ref[...] = v_new
        d_ref[...] = -ADAM_LR * ((m_new * c1) / (jnp.sqrt(v_new * c2) + ADAM_EPS) + ADAM_WD * w_ref[...])

    spec = pl.BlockSpec((tr, N), lambda i: (i, 0))
    return pl.pallas_call(
        body, name=name, grid=(R // tr,),
        in_specs=[pl.BlockSpec((S, tr, N), lambda i: (0, i, 0)), spec, spec, spec], out_specs=[spec] * 4,
        out_shape=[jax.ShapeDtypeStruct((R, N), F32)] * 4, compiler_params=_params(("parallel",)),
    )(gparts, w, m, v)


def _reduce_to_shard(name, slabs, n_split, by_columns=False, overlap=False):
    c, chip = lax.axis_index("c"), 2 * lax.axis_index("x") + lax.axis_index("y")
    if by_columns:
        R, N = slabs.shape[0], SHARD_COLS
        from_sibling = _exchange(name + "_sib", slabs, CORE_MASKS, None, False, n_split, other_half=True)[0]
        wire = _add_slabs_to_bf16(name + "_add", slabs, from_sibling, c)
    else:
        _, R, N = slabs.shape
        half = R // 2
        halves = slabs.reshape(N_CHIPS, 2, half, N)
        mine = lax.dynamic_index_in_dim(halves, c, axis=1, keepdims=False)
        theirs = lax.dynamic_index_in_dim(halves, 1 - c, axis=1, keepdims=False)
        from_sibling = _exchange(name + "_sib", theirs, CORE_MASKS, None, False, N_CHIPS)[0]
        wire = _add_to_bf16(name + "_add", mine, from_sibling)
    if overlap:
        return _exchange_start(name + "_chips", wire, CHIP_MASKS, "chip", True, n_split), (R, N)
    got = _exchange(name + "_chips", wire, CHIP_MASKS, "chip", True, n_split, copy_own=False)
    return _reduce_tail(name, wire, got, n_split, R, N)


def _reduce_tail(name, wire, got, n_split, R, N):
    c, chip = lax.axis_index("c"), 2 * lax.axis_index("x") + lax.axis_index("y")
    got = lax.dynamic_update_index_in_dim(got, lax.dynamic_index_in_dim(wire, chip, 0, keepdims=False), chip, 0)
    part = _sum_slots(name + "_sum", got)
    both = _exchange(name + "_cores", part, CORE_MASKS, "core", False, n_split, copy_own=False)
    return lax.dynamic_update_index_in_dim(both, part, c, 0).reshape(1, R, N)


def _reduce_finish(name, started, after, n_split):
    handle, (R, N) = started
    wire, got = _exchange_wait(name + "_chips_wait", handle, after, CHIP_MASKS, "chip", True, n_split)
    return _reduce_tail(name, wire, got, n_split, R, N)


def _pack(pieces, cols, row_mult=8):
    flat = jnp.concatenate([p.reshape(-1) for p in pieces])
    rows = -(-flat.shape[0] // cols)
    rows = -(-rows // row_mult) * row_mult
    return jnp.pad(flat, (0, rows * cols - flat.shape[0])).reshape(rows, cols)


def _unpack(packed, shapes):
    flat = packed.reshape(-1)
    out, off = [], 0
    for shp in shapes:
        n = 1
        for d in shp:
            n *= d
        out.append(flat[off:off + n].reshape(shp))
        off += n
    return out


def kernel(x, meta_tokens, norm_w, w_in, rw_shift_mu, rw_w0, rw_w2, rw_a0, rw_a2, rw_k_k, rw_k_a, rw_r_k, rw_gn_w, rw_gn_b, dn_conv_w, dn_A_log, dn_dt_bias, dn_norm_w, w_out, final_norm_w, loss_target, m_meta_tokens, m_norm_w, m_w_in, m_rw_shift_mu, m_rw_w0, m_rw_w2, m_rw_a0, m_rw_a2, m_rw_k_k, m_rw_k_a, m_rw_r_k, m_rw_gn_w, m_rw_gn_b, m_dn_conv_w, m_dn_A_log, m_dn_dt_bias, m_dn_norm_w, m_w_out, m_final_norm_w, v_meta_tokens, v_norm_w, v_w_in, v_rw_shift_mu, v_rw_w0, v_rw_w2, v_rw_a0, v_rw_a2, v_rw_k_k, v_rw_k_a, v_rw_r_k, v_rw_gn_w, v_rw_gn_b, v_dn_conv_w, v_dn_A_log, v_dn_dt_bias, v_dn_norm_w, v_w_out, v_final_norm_w):
    S = x.shape[1]
    L = N_META + S
    Lp = -(-L // CHUNK) * CHUNK

    small_shapes = [(RW_LORA, 256), (RW_LORA, 256), (CONV_W, 768), (N_META, 512)]
    small_mine = _pack([rw_w2[0], rw_a2[0], dn_conv_w[0], meta_tokens], 1024)
    small_all = _gather_chips("gather_small", small_mine)
    per_chip = [_unpack(small_all[s], small_shapes) for s in range(N_CHIPS)]
    w2, a2, conv_w, meta = [jnp.concatenate([per_chip[s][i] for s in range(N_CHIPS)], axis=1) for i in range(4)]
    w_in_started = _gather_shards("gather_w_in", w_in[0].astype(BF16), 8, overlap=True, dep=small_all)
    w_out_started = _gather_shards("gather_w_out", w_out[0].astype(BF16), 8, overlap=True, dep=w_in_started[4])

    tail = [jnp.zeros((Lp - L, D_MODEL), F32)] if Lp > L else []
    h = jnp.concatenate([meta + w_in_started[4][:1, :1], x[0]] + tail, axis=0)
    tgt = jnp.concatenate([jnp.zeros((N_META, D_MODEL), F32), loss_target[0]] + tail, axis=0)
    (u,) = _rowwise("rms_in", _rms_fn, [_row(h)], [norm_w], [D_MODEL], tm=2 * ROW_TILE, out_dtype=BF16)
    w_in_all = _gather_finish("gather_w_in", w_in_started, u, 8)
    slabs = [w_in_all[:, s].reshape(D_MODEL, SHARD_COLS) for s in range(N_CHIPS)]
    cut = GAP_AT - (N_CHIPS - 1) * SHARD_COLS
    W_gapped = jnp.concatenate(slabs[:-1] + [slabs[-1][:, :cut], jnp.zeros((D_MODEL, GAP), BF16), slabs[-1][:, cut:]],
                               axis=1)
    p = _mm("in_proj", u, W_gapped, "nn", dep=w_out_started[4])

    mu = rw_shift_mu
    zpad = jnp.zeros((RW_LORA, RW_WIDTH), F32)
    rw_params = [mu[:, 0:1024], mu[:, 1024:2048], mu[:, 2048:3072], mu[:, 3072:3200], rw_w0,
                 jnp.concatenate([w2, zpad], axis=0), rw_a0, jnp.concatenate([zpad, a2], axis=0), rw_k_k, rw_k_a]
    rw_rows = [_row(p, 1024, OFF_R), _row(p, 1024, OFF_K), _row(p, 1024, OFF_V), _row(p, 128, OFF_S1),
               _row(p, 1024, OFF_GATE)]
    rw_pars = rw_params + [rw_r_k, rw_gn_w, rw_gn_b]
    rw_saved_shapes = [(N_GROUPS, CHUNK, GROUP), (N_GROUPS, CHUNK, GROUP), (N_GROUPS, 2 * CHUNK, 2 * GROUP),
                       (N_GROUPS, 2 * CHUNK, GROUP)]
    ya, rw_ck, *rw_t = _chunk_fwd("rw_chunk_fwd", _rw_fused_fn, rw_rows, 4, rw_pars, rw_saved_shapes)

    dn_rows = [_row(p, 1024, OFF_DQ), _row(p, 1024, OFF_DK), _row(p, 1024, OFF_DV), _row(p, 128, OFF_S2),
               _row(p, 1024, OFF_Z)]
    dn_pars = [conv_w[j:j + 1, 1024 * i:1024 * (i + 1)] for i in range(3) for j in range(CONV_W)]
    narrow = lambda t: jnp.pad(t, ((0, 0), (DN_HEADS, DN_HEAD - 2 * DN_HEADS)))
    dn_pars += [narrow(dn_A_log), narrow(dn_dt_bias), jnp.tile(dn_norm_w, (1, DN_HEADS))]
    yb, dn_ck, *dn_t = _chunk_fwd("dn_chunk_fwd", _dn_fused_fn, dn_rows, 3, dn_pars,
                                  [(N_GROUPS, CHUNK, CHUNK), (N_GROUPS, CHUNK, GROUP)])

    y = jnp.concatenate([ya, yb], axis=1)
    w_out_all = _gather_finish("gather_w_out", w_out_started, y, 8)
    Wo = jnp.concatenate([w_out_all[:, s].reshape(D_MODEL // N_CHIPS, D_MODEL) for s in range(N_CHIPS)], axis=0)
    yw = _mm("out_proj", y, Wo, "nn", tn=1024)
    loss_acc, dz, dz16, d_fw = _loss_head(h, yw, tgt, final_norm_w.reshape(1, D_MODEL), L)
    loss = lax.psum(loss_acc[0, 0], ("x", "y", "c"))

    d_wo = _mm("d_w_out", y.T, dz16, "nn", tm=512, tn=1024, tk=Lp)
    wo_started = _reduce_to_shard("rs_w_out", d_wo.reshape(N_CHIPS, D_MODEL // N_CHIPS, D_MODEL), 8, overlap=True)
    dy = _mm("d_out_proj", dz16, Wo, "nt", tn=1024, tk=2048, dep=wo_started[0][4])

    dp, d_rw_pars = _chunk_bwd("rw_chunk_bwd", _rw_fused_fn, rw_rows, 4, rw_pars, rw_ck, rw_t, _row(dy, 1024, 0),
                               (2, 0))
    d_prep_pars, d_rw_pars = d_rw_pars[:10], d_rw_pars[10:]
    dp, d_dn_pars = _chunk_bwd("dn_chunk_bwd", _dn_fused_fn, dn_rows, 3, dn_pars, dn_ck, dn_t, _row(dy, 1024, RW_WIDTH),
                               (2, 1), into=dp)
    d_conv_parts = [jnp.concatenate(d_dn_pars[4 * i:4 * i + 4], axis=0) for i in range(3)]
    d_a_log_b, d_dt_b = d_dn_pars[12], d_dn_pars[13]
    d_dn_nw = jnp.sum(d_dn_pars[14].reshape(DN_HEADS, DN_HEAD), axis=0, keepdims=True)
    d_W = _mm("d_w_in", u.T, dp, "nn", tm=1024, tn=768, tk=Lp)
    wi_started = _reduce_to_shard("rs_w_in", d_W, 8, by_columns=True, overlap=True)
    du = _mm("d_in_proj", dp, W_gapped, "nt", tn=2048, tk=1408, dep=wi_started[0][4])
    (dh,), (d_norm_w,) = _rowwise_bwd("rms_in_bwd", lambda h_, w_: (_rms_fn(h_, w_)[0], h_), [_row(h)], [norm_w],
                                      [_row(du), _row(dz)])
    grad_x = dh[N_META:L][None]

    d_mu = jnp.concatenate(d_prep_pars[0:4], axis=1)
    d_w2, d_a2 = d_prep_pars[5][:RW_LORA], d_prep_pars[7][RW_LORA:]
    d_conv = jnp.concatenate(d_conv_parts, axis=1)
    d_meta = dh[:N_META]
    head_sum = lambda t: t[:, DN_HEADS:2 * DN_HEADS]
    rep_names = ["norm_w", "rw_shift_mu", "rw_w0", "rw_a0", "rw_k_k", "rw_k_a", "rw_r_k", "rw_gn_w", "rw_gn_b",
                 "dn_A_log", "dn_dt_bias", "dn_norm_w", "final_norm_w"]
    rep_g = [d_norm_w, d_mu, d_prep_pars[4], d_prep_pars[6], d_prep_pars[8], d_prep_pars[9],
             d_rw_pars[0], d_rw_pars[1], d_rw_pars[2],
             head_sum(d_a_log_b), head_sum(d_dt_b), d_dn_nw, d_fw.reshape(D_MODEL)]
    rep_w = [norm_w, rw_shift_mu, rw_w0, rw_a0, rw_k_k, rw_k_a, rw_r_k, rw_gn_w, rw_gn_b, dn_A_log, dn_dt_bias,
             dn_norm_w, final_norm_w]
    rep_m = [m_norm_w, m_rw_shift_mu, m_rw_w0, m_rw_a0, m_rw_k_k, m_rw_k_a, m_rw_r_k, m_rw_gn_w, m_rw_gn_b,
             m_dn_A_log, m_dn_dt_bias, m_dn_norm_w, m_final_norm_w]
    rep_v = [v_norm_w, v_rw_shift_mu, v_rw_w0, v_rw_a0, v_rw_k_k, v_rw_k_a, v_rw_r_k, v_rw_gn_w, v_rw_gn_b,
             v_dn_A_log, v_dn_dt_bias, v_dn_norm_w, v_final_norm_w]
    rep_shapes = [t.shape for t in rep_w]
    rep_all = _exchange("gather_rep_grads", _pack(rep_g, 128), ALL_MASKS, "dev", False)
    rep_out = _adamw("adam_rep", rep_all, _pack(rep_w, 128), _pack(rep_m, 128), _pack(rep_v, 128))
    rep_out = [dict(zip(rep_names, _unpack(t, rep_shapes))) for t in rep_out]

    sm_slabs = jnp.stack([_pack([d_w2[:, 256 * s:256 * (s + 1)], d_a2[:, 256 * s:256 * (s + 1)],
                                 d_conv[:, 768 * s:768 * (s + 1)], d_meta[:, 512 * s:512 * (s + 1)]], 1024, 64)
                          for s in range(N_CHIPS)])
    sm_parts = _reduce_to_shard("rs_small", sm_slabs, 1)
    sm_w = [rw_w2[0], rw_a2[0], dn_conv_w[0], meta_tokens]
    sm_m = [m_rw_w2[0], m_rw_a2[0], m_dn_conv_w[0], m_meta_tokens]
    sm_v = [v_rw_w2[0], v_rw_a2[0], v_dn_conv_w[0], v_meta_tokens]
    sm_out = _adamw("adam_small", sm_parts, _pack(sm_w, 1024, 64), _pack(sm_m, 1024, 64), _pack(sm_v, 1024, 64))
    sm_names = ["rw_w2", "rw_a2", "dn_conv_w", "meta_tokens"]
    sm_full_shapes = [(1, RW_LORA, 256), (1, RW_LORA, 256), (1, CONV_W, 768), (N_META, 512)]
    sm_out = [dict(zip(sm_names, [t.reshape(shp) for t, shp in zip(_unpack(o, small_shapes), sm_full_shapes)]))
              for o in sm_out]

    wo_parts = _reduce_finish("rs_w_out", wo_started, dp, 8)
    wo_out = _adamw("adam_w_out", wo_parts, w_out[0], m_w_out[0], v_w_out[0])
    wi_parts = _reduce_finish("rs_w_in", wi_started, dh, 8)
    wi_out = _adamw("adam_w_in", wi_parts, w_in[0], m_w_in[0], v_w_in[0])

    order = ["meta_tokens", "norm_w", "w_in", "rw_shift_mu", "rw_w0", "rw_w2", "rw_a0", "rw_a2", "rw_k_k", "rw_k_a",
             "rw_r_k", "rw_gn_w", "rw_gn_b", "dn_conv_w", "dn_A_log", "dn_dt_bias", "dn_norm_w", "w_out",
             "final_norm_w"]
    outs = [loss, grad_x]
    for kind in range(4):
        table = dict(rep_out[kind])
        table.update(sm_out[kind])
        table["w_in"] = wi_out[kind][None]
        table["w_out"] = wo_out[kind][None]
        outs += [table[n] for n in order]
    return tuple(outs)
```

```python
import functools

import jax
import jax.numpy as jnp
from jax import lax
from jax.experimental import pallas as pl
from jax.experimental.pallas import tpu as pltpu

F32 = jnp.float32
BF16 = jnp.bfloat16
HIGH = lax.Precision.HIGH
HIGHEST = lax.Precision.HIGHEST
MESH = pl.DeviceIdType.MESH

D_MODEL = 2048
N_META = 16
RW_WIDTH = 1024
RW_HEAD = 64
RW_HEADS = 16
RW_LORA = 64
RW_GN_EPS = 64e-5
DN_WIDTH = 1024
DN_HEAD = 128
DN_HEADS = 8
CONV_W = 4
CHUNK = 64
NORM_EPS = 1e-6
IN_COLS = 8336
N_CHIPS = 4
SHARD_COLS = IN_COLS // N_CHIPS

NP_COLS = 8 * 1024 + 256
GAP_AT = IN_COLS - DN_WIDTH
GAP = NP_COLS - IN_COLS
OFF_R, OFF_K, OFF_V, OFF_S1, OFF_GATE = 0, 1024, 2048, 3072, 3200
OFF_DQ, OFF_DK, OFF_DV, OFF_S2, OFF_Z = 4224, 5248, 6272, 7296, 7424

ADAM_LR = 0.001
ADAM_B1 = 0.9
ADAM_B2 = 0.999
ADAM_EPS = 1e-08
ADAM_WD = 0.01
ADAM_STEP = 10

VMEM_LIMIT_BYTES = 56 * 1024 * 1024
ROW_TILE = 104
MM_ROW_TILE = 832


def _params(sem=None):
    return pltpu.CompilerParams(dimension_semantics=sem, vmem_limit_bytes=VMEM_LIMIT_BYTES)


def _pick(n, target, mult=8):
    best = None
    for d in range(mult, min(n, target) + 1, mult):
        if n % d == 0:
            best = d
    return n if best is None else best


def _mm(name, a, b, mode, tm=MM_ROW_TILE, tn=1408, tk=2048, dep=None):
    if mode == "nn":
        (M, K), (_, N) = a.shape, b.shape
    elif mode == "nt":
        (M, K), (N, _) = a.shape, b.shape
    else:
        (K, M), (_, N) = a.shape, b.shape
    tm = _pick(M, tm, 128 if mode == "tn" else 16)
    tn = _pick(N, tn, 128)
    tk = _pick(K, tk, 8 if mode == "tn" else 128)
    if mode == "nn":
        a_spec = pl.BlockSpec((tm, tk), lambda i, j, k: (i, k))
        b_spec = pl.BlockSpec((tk, tn), lambda i, j, k: (k, j))
        dims = (((1,), (0,)), ((), ()))
    elif mode == "nt":
        a_spec = pl.BlockSpec((tm, tk), lambda i, j, k: (i, k))
        b_spec = pl.BlockSpec((tn, tk), lambda i, j, k: (j, k))
        dims = (((1,), (1,)), ((), ()))
    else:
        a_spec = pl.BlockSpec((tk, tm), lambda i, j, k: (k, i))
        b_spec = pl.BlockSpec((tk, tn), lambda i, j, k: (k, j))
        dims = (((0,), (0,)), ((), ()))

    def body(a_ref, b_ref, *rest):
        o_ref = rest[-1]

        @pl.when(pl.program_id(2) == 0)
        def _():
            o_ref[...] = jnp.zeros_like(o_ref)

        o_ref[...] += lax.dot_general(a_ref[...].astype(BF16), b_ref[...].astype(BF16), dims,
                                      preferred_element_type=F32)

    deps = [] if dep is None else [dep]
    return pl.pallas_call(
        body, name=name, grid=(M // tm, N // tn, K // tk),
        in_specs=[a_spec, b_spec] + [pl.BlockSpec((8, 128), lambda i, j, k: (0, 0))] * len(deps),
        out_specs=pl.BlockSpec((tm, tn), lambda i, j, k: (i, j)),
        out_shape=jax.ShapeDtypeStruct((M, N), F32),
        compiler_params=_params(("parallel", "parallel", "arbitrary")),
    )(a, b, *deps)


def _row(arr, width=None, cb=0):
    return (arr, arr.shape[1] if width is None else width, cb)


def _rowwise(name, fn, rows, params, out_widths, tm=ROW_TILE, out_dtype=F32):
    R = rows[0][0].shape[0]
    tm = _pick(R, tm, 16 if out_dtype == BF16 else 8)
    n_r, n_p = len(rows), len(params)

    def body(*refs):
        vals = [r[...] for r in refs[:n_r + n_p]]
        for o_ref, val in zip(refs[n_r + n_p:], fn(*vals)):
            o_ref[...] = val.astype(out_dtype)

    in_specs = [pl.BlockSpec((tm, w), lambda i, cb=cb: (i, cb)) for (_, w, cb) in rows]
    in_specs += [pl.BlockSpec(p.shape, lambda i: (0, 0)) for p in params]
    return pl.pallas_call(
        body, name=name, grid=(R // tm,), in_specs=in_specs,
        out_specs=[pl.BlockSpec((tm, w), lambda i: (i, 0)) for w in out_widths],
        out_shape=[jax.ShapeDtypeStruct((R, w), out_dtype) for w in out_widths],
        compiler_params=_params(("parallel",)),
    )(*[r[0] for r in rows], *params)


def _rowwise_bwd(name, fn, rows, params, douts, tm=ROW_TILE):
    R = rows[0][0].shape[0]
    tm = _pick(R, tm)
    n_r, n_p, n_d = len(rows), len(params), len(douts)

    def body(*refs):
        vals = [r[...] for r in refs[:n_r + n_p]]
        cts = tuple(r[...] for r in refs[n_r + n_p:n_r + n_p + n_d])
        grads = jax.vjp(fn, *vals)[1](cts)
        outs = refs[n_r + n_p + n_d:]
        for o_ref, g in zip(outs[:n_r], grads[:n_r]):
            o_ref[...] = g

        @pl.when(pl.program_id(0) == 0)
        def _():
            for o_ref in outs[n_r:]:
                o_ref[...] = jnp.zeros_like(o_ref)

        for o_ref, g in zip(outs[n_r:], grads[n_r:]):
            o_ref[...] += g

    in_specs = [pl.BlockSpec((tm, w), lambda i, cb=cb: (i, cb)) for (_, w, cb) in rows]
    in_specs += [pl.BlockSpec(p.shape, lambda i: (0, 0)) for p in params]
    in_specs += [pl.BlockSpec((tm, w), lambda i, cb=cb: (i, cb)) for (_, w, cb) in douts]
    out_specs = [pl.BlockSpec((tm, w), lambda i: (i, 0)) for (_, w, _) in rows]
    out_specs += [pl.BlockSpec(p.shape, lambda i: (0, 0)) for p in params]
    out_shape = [jax.ShapeDtypeStruct((R, w), F32) for (_, w, _) in rows]
    out_shape += [jax.ShapeDtypeStruct(p.shape, F32) for p in params]
    res = pl.pallas_call(
        body, name=name, grid=(R // tm,), in_specs=in_specs, out_specs=out_specs, out_shape=out_shape,
        compiler_params=_params(("arbitrary",)),
    )(*[r[0] for r in rows], *params, *[d[0] for d in douts])
    return res[:n_r], res[n_r:]


def _softplus(x):
    return jnp.maximum(x, 0.0) + jnp.log(1.0 + jnp.exp(-jnp.abs(x)))


def _silu(x):
    return x * jax.nn.sigmoid(x)


def _rms_fn(h, w):
    return (h * lax.rsqrt(jnp.mean(h * h, axis=-1, keepdims=True) + NORM_EPS) * w,)


def _rw_prep_fn(pr, pr1, pk, pk1, pv, pv1, ps, ps1, mu_r, mu_k, mu_v, mu_s, w0, w2p, a0, a2p, k_k, k_a):
    r = pr + (pr1 - pr) * mu_r
    k = pk + (pk1 - pk) * mu_k
    v = pv + (pv1 - pv) * mu_v
    s = ps + (ps1 - ps) * mu_s
    lora = lambda x, w: jnp.dot(x.astype(BF16), w.astype(BF16), preferred_element_type=F32)
    w_log = -_softplus(-(w0 + lora(jnp.tanh(s), w2p))) - 0.5
    log_decay = -jnp.exp(w_log)
    a = jax.nn.sigmoid(a0 + lora(s, a2p))
    return r, log_decay, k * (1.0 + (a - 1.0) * k_a), v, k * k_k, a


def _conv_fn(u0, u1, u2, u3, w0, w1, w2, w3):
    return (_silu(u0 * w3 + u1 * w2 + u2 * w1 + u3 * w0),)


def _dn_gate_fn(ps2, a_log_n, dt_n):
    beta_n = jax.nn.sigmoid(ps2)
    g_n = -jnp.exp(a_log_n) * _softplus(ps2 + dt_n)
    lane = lax.broadcasted_iota(jnp.int32, (1, DN_HEAD), 1)
    pick = lambda x, j: jnp.broadcast_to(jnp.sum(x * (lane == j).astype(F32), axis=-1, keepdims=True), x.shape)[None]
    beta = jnp.concatenate([pick(beta_n, h) for h in range(DN_HEADS)], axis=0)
    g = jnp.concatenate([pick(g_n, DN_HEADS + h) for h in range(DN_HEADS)], axis=0)
    return beta, g


def _tril_masks(c):
    t = lax.broadcasted_iota(jnp.int32, (c, c), 0)
    s = lax.broadcasted_iota(jnp.int32, (c, c), 1)
    return s <= t, s < t


def _dot(x, y, cx, cy, prec=HIGH):
    nb = x.ndim - 2
    batch = tuple(range(nb))
    return lax.dot_general(x, y, (((cx + nb,), (cy + nb,)), (batch, batch)), precision=prec,
                           preferred_element_type=F32)


INV_BASE = 8


def _blocked_inv(a, mm, row, col, mm_merge=None):
    mm_merge = mm if mm_merge is None else mm_merge
    shift = INV_BASE.bit_length() - 1
    d = jnp.where(jnp.right_shift(row, shift) == jnp.right_shift(col, shift), a, 0.0)
    x = (row == col).astype(F32) - d
    p = mm(d, d)
    x = x + mm(x, p)
    x = x + mm(x, mm(p, p))
    size = INV_BASE
    while size < a.shape[-2]:
        same_pair = jnp.right_shift(row, shift + 1) == jnp.right_shift(col, shift + 1)
        lower_left = (jnp.right_shift(row, shift) & 1 == 1) & (jnp.right_shift(col, shift) & 1 == 0)
        x = x - mm_merge(x, mm_merge(jnp.where(same_pair & lower_left, a, 0.0), x))
        size, shift = 2 * size, shift + 1
    return x


def _tri_inv(a):
    c = a.shape[-1]
    row, col = lax.broadcasted_iota(jnp.int32, (c, c), 0), lax.broadcasted_iota(jnp.int32, (c, c), 1)
    return _blocked_inv(a, lambda x, y: _dot(x, y, 1, 0), row, col,
                        lambda x, y: _dot(x.astype(BF16), y.astype(BF16), 1, 0, None))


@jax.custom_vjp
def _tri_inv_saved(a, t):
    return t


def _tri_inv_saved_fwd(a, t):
    return t, t


def _tri_inv_saved_bwd(t, dt):
    return -_dot(_dot(t, dt, 0, 0), t, 1, 1), jnp.zeros_like(t)


_tri_inv_saved.defvjp(_tri_inv_saved_fwd, _tri_inv_saved_bwd)


def _pair_masks():
    lane = lax.broadcasted_iota(jnp.int32, (1, 2 * RW_HEAD), 1)
    m0 = (lane < RW_HEAD).astype(F32)
    return m0, 1.0 - m0


def _pair_bd(x):
    m0, m1 = _pair_masks()
    return jnp.concatenate([x * m0, x * m1], axis=-2)


def _pair_mm(x, y):
    return _dot(x, _pair_bd(y), 1, 0)


def _pair_inv(a):
    c = a.shape[-2]
    row = lax.broadcasted_iota(jnp.int32, (c, 2 * RW_HEAD), 0)
    col = lax.broadcasted_iota(jnp.int32, (c, 2 * RW_HEAD), 1) & (RW_HEAD - 1)
    return _blocked_inv(a, _pair_mm, row, col,
                        lambda x, y: _dot(x.astype(BF16), _pair_bd(y).astype(BF16), 1, 0, None))


@jax.custom_vjp
def _pair_inv_saved(a, t):
    return t


def _pair_inv_saved_fwd(a, t):
    return t, t


def _pair_inv_saved_bwd(t, dt):
    m0, m1 = _pair_masks()
    c = t.shape[-2]
    z = _dot(t, dt, 0, 0)
    x = z[:, :c, :] * m0 + z[:, c:, :] * m1
    return -_dot(x, _pair_bd(t), 1, 1), jnp.zeros_like(t)


_pair_inv_saved.defvjp(_pair_inv_saved_fwd, _pair_inv_saved_bwd)


@jax.custom_vjp
def _use_saved(x, x_saved):
    return x_saved


_use_saved.defvjp(lambda x, x_saved: (x_saved, None), lambda _, g: (g, jnp.zeros_like(g)))


def _rw_chunk_fn(S, r, lw, k2, v, kkp, a, gate, rk, gnw, gnb, saved=None):
    B, C, P = r.shape
    m0, m1 = _pair_masks()
    seg = lambda x: (jnp.sum(x * m0, axis=-1, keepdims=True) * m0 + jnp.sum(x * m1, axis=-1, keepdims=True) * m1)
    mm = lambda x, y: _dot(x.astype(BF16), y.astype(BF16), 1, 0, None)
    nt = lambda x, y: _dot(x, y, 1, 1)
    tn = lambda x, y: _dot(x, y, 0, 0)
    pair_mm = lambda x, y: mm(x, _pair_bd(y))
    t_idx = lax.broadcasted_iota(jnp.int32, (C, P), 0)
    s_idx = lax.broadcasted_iota(jnp.int32, (C, P), 1) & (RW_HEAD - 1)
    incl, strict = s_idx <= t_idx, s_idx < t_idx
    tril = jnp.broadcast_to(_tril_masks(C)[0].astype(F32), (B, C, C))
    kk = kkp * lax.rsqrt(seg(kkp * kkp) + 1e-6)
    b = kk * a
    reuse = (lambda x, i: x) if saved is None else (lambda x, i: _use_saved(x, saved[i]))
    g_incl = reuse(_dot(tril, lw, 1, 0), 1)
    g_excl = g_incl - lw
    inv = jnp.exp(-g_incl)
    alpha, beta, kappa, rho = kk * jnp.exp(g_excl), b * inv, k2 * inv, r * jnp.exp(g_incl)
    ar = jnp.concatenate([alpha, rho], axis=-2)
    scores = reuse(nt(ar, jnp.concatenate([_pair_bd(beta), _pair_bd(kappa)], axis=-2)), 2)
    a_ab = jnp.where(strict, scores[:, :C, :P], 0.0)
    a_ak = jnp.where(strict, scores[:, :C, P:], 0.0)
    r_b = jnp.where(incl, scores[:, C:, :P], 0.0)
    r_k = jnp.where(incl, scores[:, C:, P:], 0.0)
    t_inv = _pair_inv(a_ab) if saved is None else _pair_inv_saved(a_ab, saved[0])
    on_state = reuse(nt(ar, S), 3)
    u = pair_mm(t_inv, -on_state[:, :C, :] - pair_mm(a_ak, v))
    y = on_state[:, C:, :] + mm(jnp.concatenate([r_b, r_k], axis=-1),
                                jnp.concatenate([_pair_bd(u), _pair_bd(v)], axis=-2))
    same_head = ((lax.broadcasted_iota(jnp.int32, (P, P), 0) < RW_HEAD)
                 == (lax.broadcasted_iota(jnp.int32, (P, P), 1) < RW_HEAD))
    fresh = tn(jnp.concatenate([u, v], axis=-2), jnp.concatenate([beta, kappa], axis=-2))
    S_new = jnp.exp(jnp.sum(lw, axis=-2, keepdims=True)) * (S + jnp.where(same_head, fresh, 0.0))
    dev = y - seg(y) * (1.0 / RW_HEAD)
    yn = dev * lax.rsqrt(seg(dev * dev) * (1.0 / RW_HEAD) + RW_GN_EPS) * gnw + gnb
    bonus = seg(r * k2 * rk) * v
    return (yn + bonus) * _silu(gate), S_new, (t_inv, g_incl, scores, on_state)


def _dn_chunk_fn(S, qc, kc, vc, bb, gb, z, nw, saved=None):
    B, C, D = qc.shape
    mm = lambda x, y: _dot(x.astype(BF16), y.astype(BF16), 1, 0, None)
    nt = lambda x, y, p=None: _dot(x, y, 1, 1, p) if p else _dot(x.astype(BF16), y.astype(BF16), 1, 1, None)
    tn = lambda x, y: _dot(x.astype(BF16), y.astype(BF16), 0, 0, None)
    incl, strict = _tril_masks(C)
    q = qc * lax.rsqrt(jnp.sum(qc * qc, axis=-1, keepdims=True) + 1e-6) * (D ** -0.5)
    k = kc * lax.rsqrt(jnp.sum(kc * kc, axis=-1, keepdims=True) + 1e-6)
    kb, vb = k * bb, vc * bb
    G = _dot(jnp.broadcast_to(incl.astype(F32), (B, C, C)), gb, 1, 0, HIGHEST)
    G = G if saved is None else _use_saved(G, saved[1])
    lane = lax.broadcasted_iota(jnp.int32, (C, D), 1)
    e0, e1 = (lane == 0).astype(F32), (lane == 1).astype(F32)
    diff = nt(G * e0 + e1, e0 - G * e1, HIGHEST)
    dmask = jnp.where(incl, jnp.exp(jnp.where(incl, diff, 0.0)), 0.0)
    M = jnp.where(strict, nt(kb, k) * dmask, 0.0)
    T = _tri_inv(M) if saved is None else _tri_inv_saved(M, saved[0])
    eG = jnp.exp(G)
    u = mm(T, vb)
    w = mm(T, kb * eG)
    attn = jnp.where(incl, nt(q, k) * dmask, 0.0)
    v_new = u - mm(w, S)
    o = mm(q * eG, S) + mm(attn, v_new)
    g_last = jnp.sum(gb, axis=-2, keepdims=True)
    S_new = S * jnp.exp(jnp.broadcast_to(g_last, S.shape)) + tn(k * jnp.exp(g_last - G), v_new)
    on = o * lax.rsqrt(jnp.mean(o * o, axis=-1, keepdims=True) + NORM_EPS) * nw
    return on * _silu(z), S_new, (T, G)


N_GROUPS = 8
GROUP = 128
HALO = 8


def _groups(x):
    return jnp.concatenate([x[:, g * GROUP:(g + 1) * GROUP][None] for g in range(N_GROUPS)], axis=0)


@functools.partial(jax.custom_vjp, nondiff_argnums=(1,))
def _shift_rows(ext, j):
    return pltpu.roll(ext, j, 0)[HALO:, :]


def _shift_rows_fwd(ext, j):
    return _shift_rows(ext, j), None


def _shift_rows_bwd(j, _, d):
    z = jnp.concatenate([jnp.zeros((HALO, d.shape[1]), d.dtype), d], axis=0)
    return (pltpu.roll(z, z.shape[0] - j, 0),)


_shift_rows.defvjp(_shift_rows_fwd, _shift_rows_bwd)


def _rw_fused_fn(S, h_r, h_k, h_v, h_s, p_r, p_k, p_v, p_s, gate, *pars, saved=None):
    prev = lambda h, x: _shift_rows(jnp.concatenate([h, x], axis=0), 1)
    seq = _rw_prep_fn(p_r, prev(h_r, p_r), p_k, prev(h_k, p_k), p_v, prev(h_v, p_v), p_s, prev(h_s, p_s), *pars[:10])
    return _rw_chunk_fn(S, *[_groups(t) for t in seq], _groups(gate), *[_groups(t) for t in pars[10:]],
                        saved=saved)


def _dn_fused_fn(S, h_q, h_k, h_v, p_q, p_k, p_v, p_s2, z, *pars, saved=None):
    conv = []
    for i, (h, x) in enumerate(((h_q, p_q), (h_k, p_k), (h_v, p_v))):
        ext = jnp.concatenate([h, x], axis=0)
        conv += _conv_fn(x, _shift_rows(ext, 1), _shift_rows(ext, 2), _shift_rows(ext, 3), *pars[4 * i:4 * i + 4])
    beta, g = _dn_gate_fn(p_s2, pars[12], pars[13])
    return _dn_chunk_fn(S, *[_groups(t) for t in conv], beta, g, _groups(z), _groups(pars[14]), saved=saved)


def _chunk_fwd(name, fn, rows, n_halo, pars, saved_shapes):
    Lp = rows[0][0].shape[0]
    C, D, NB = CHUNK, GROUP, N_GROUPS
    nc, n_r, n_p, per = Lp // C, len(rows), len(pars), CHUNK // HALO
    n_s = len(saved_shapes)

    def body(*refs):
        row_refs, halo_refs, par_refs = refs[:n_r], refs[n_r:n_r + n_halo], refs[n_r + n_halo:n_r + n_halo + n_p]
        y_ref, ck_ref = refs[n_r + n_halo + n_p:n_r + n_halo + n_p + 2]
        saved_refs, s_ref = refs[n_r + n_halo + n_p + 2:-1], refs[-1]
        first = pl.program_id(0) == 0

        @pl.when(first)
        def _():
            s_ref[...] = jnp.zeros_like(s_ref)

        S = s_ref[...]
        ck_ref[0] = S
        halos = [jnp.where(first, 0.0, r[...]) for r in halo_refs]
        y, S_new, saved = fn(S, *halos, *[r[...] for r in row_refs], *[r[...] for r in par_refs])
        for g in range(NB):
            y_ref[:, g * D:(g + 1) * D] = y[g].astype(BF16)
        for o_ref, val in zip(saved_refs, saved):
            o_ref[0] = val
        s_ref[...] = S_new

    return pl.pallas_call(
        body, name=name, grid=(nc,),
        in_specs=[pl.BlockSpec((pl.Element(C), pl.Element(w)), lambda c, off=off: (c * C, off)) for (_, w, off) in rows]
        + [pl.BlockSpec((pl.Element(HALO), pl.Element(w)), lambda c, off=off: (pl.multiple_of(jnp.maximum(c * C - HALO, 0), HALO), off))
           for (_, w, off) in rows[:n_halo]]
        + [pl.BlockSpec(p.shape, lambda c: (0, 0)) for p in pars],
        out_specs=[pl.BlockSpec((C, NB * D), lambda c: (c, 0)), pl.BlockSpec((1, NB, D, D), lambda c: (c, 0, 0, 0))]
        + [pl.BlockSpec((1,) + tuple(shp), lambda c: (c, 0, 0, 0)) for shp in saved_shapes],
        out_shape=[jax.ShapeDtypeStruct((Lp, NB * D), BF16), jax.ShapeDtypeStruct((nc, NB, D, D), F32)]
        + [jax.ShapeDtypeStruct((nc,) + tuple(shp), F32) for shp in saved_shapes],
        scratch_shapes=[pltpu.VMEM((NB, D, D), F32)],
        compiler_params=_params(("arbitrary",)),
    )(*[r[0] for r in rows], *[r[0] for r in rows[:n_halo]], *pars)


def _chunk_bwd(name, fn, rows, n_halo, pars, ckpt, saved, dy, out_blocks, into=None):
    Lp = rows[0][0].shape[0]
    C, D, NB = CHUNK, GROUP, N_GROUPS
    nc, n_r, n_p, per = Lp // C, len(rows), len(pars), CHUNK // HALO
    n_s = len(saved)
    n_in = n_r + n_halo + n_p
    width = sum(w for (_, w, _) in rows)
    n_blocks, my_block = out_blocks
    extra = [] if into is None else [into]

    def body(*refs):
        row_refs, halo_refs, par_refs = refs[:n_r], refs[n_r:n_r + n_halo], refs[n_r + n_halo:n_in]
        ck_ref, saved_refs, dy_ref = refs[n_in], refs[n_in + 1:n_in + 1 + n_s], refs[n_in + 1 + n_s]
        outs = refs[n_in + 2 + n_s + len(extra):]
        drows_ref, dpar_refs = outs[0], outs[1:1 + n_p]
        ds_ref, carry_refs = outs[1 + n_p], outs[2 + n_p:]
        i = pl.program_id(0)

        @pl.when(i == 0)
        def _():
            ds_ref[...] = jnp.zeros_like(ds_ref)
            for o_ref in list(dpar_refs) + list(carry_refs):
                o_ref[...] = jnp.zeros_like(o_ref)

        halos = [jnp.where(i == nc - 1, 0.0, r[...]) for r in halo_refs]
        vals = [ck_ref[0]] + halos + [r[...] for r in row_refs] + [r[...] for r in par_refs]
        stored = tuple(r[0] for r in saved_refs)
        grads = jax.vjp(lambda *v: fn(*v, saved=stored)[:2], *vals)[1]((_groups(dy_ref), ds_ref[...]))
        ds_ref[...] = grads[0]
        d_halos, d_rows, d_pars = grads[1:1 + n_halo], grads[1 + n_halo:1 + n_halo + n_r], grads[1 + n_halo + n_r:]
        col = 0
        for k, g in enumerate(d_rows):
            if k < n_halo:
                g = g + jnp.concatenate([jnp.zeros((C - HALO, g.shape[1]), F32), carry_refs[k][...]], axis=0)
                carry_refs[k][...] = d_halos[k]
            drows_ref[:, col:col + g.shape[1]] = g.astype(BF16)
            col += g.shape[1]
        for o_ref, g in zip(dpar_refs, d_pars):
            o_ref[...] += g

    rev = lambda c: nc - 1 - c
    dy_arr, dy_w, dy_cb = dy
    res = pl.pallas_call(
        body, name=name, grid=(nc,),
        in_specs=[pl.BlockSpec((pl.Element(C), pl.Element(w)), lambda c, off=off: (rev(c) * C, off))
                  for (_, w, off) in rows]
        + [pl.BlockSpec((pl.Element(HALO), pl.Element(w)),
                        lambda c, off=off: (pl.multiple_of(jnp.maximum(rev(c) * C - HALO, 0), HALO), off))
           for (_, w, off) in rows[:n_halo]]
        + [pl.BlockSpec(p.shape, lambda c: (0, 0)) for p in pars]
        + [pl.BlockSpec((1, NB, D, D), lambda c: (rev(c), 0, 0, 0))]
        + [pl.BlockSpec((1,) + tuple(t.shape[1:]), lambda c: (rev(c), 0, 0, 0)) for t in saved]
        + [pl.BlockSpec((pl.Element(C), pl.Element(dy_w)), lambda c: (rev(c) * C, dy_cb))]
        + [pl.BlockSpec(memory_space=pl.ANY)] * len(extra),
        out_specs=[pl.BlockSpec((C, width), lambda c: (rev(c), my_block))]
        + [pl.BlockSpec(p.shape, lambda c: (0, 0)) for p in pars],
        out_shape=[jax.ShapeDtypeStruct((Lp, n_blocks * width), BF16)]
        + [jax.ShapeDtypeStruct(p.shape, F32) for p in pars],
        scratch_shapes=[pltpu.VMEM((NB, D, D), F32)] + [pltpu.VMEM((HALO, w), F32) for (_, w, _) in rows[:n_halo]],
        input_output_aliases={} if into is None else {n_in + 2 + n_s: 0},
        compiler_params=_params(("arbitrary",)),
    )(*[r[0] for r in rows], *[r[0] for r in rows[:n_halo]], *pars, ckpt, *saved, dy_arr, *extra)
    return res[0], res[1:]


def _row_specs(rows, n_halo, row_of):
    C = CHUNK
    specs = [pl.BlockSpec((pl.Element(C), pl.Element(w)), lambda c, off=off: (row_of(c) * C, off))
             for (_, w, off) in rows]
    specs += [pl.BlockSpec((pl.Element(HALO), pl.Element(w)),
                           lambda c, off=off: (pl.multiple_of(jnp.maximum(row_of(c) * C - HALO, 0), HALO), off))
              for (_, w, off) in rows[:n_halo]]
    return specs


def _mixers_fwd(name, parts):
    Lp = parts[0][1][0][0].shape[0]
    C, D, NB = CHUNK, GROUP, N_GROUPS
    nc, width = Lp // C, NB * D
    n_in = [len(rows) + n_halo + len(pars) for (_, rows, n_halo, pars, _) in parts]
    n_out = [1 + len(shapes) for (*_, shapes) in parts]

    def body(*refs):
        ins, y_ref = refs[:sum(n_in)], refs[sum(n_in)]
        outs, states = refs[sum(n_in) + 1:sum(n_in) + 1 + sum(n_out)], refs[sum(n_in) + 1 + sum(n_out):]
        first = pl.program_id(0) == 0

        @pl.when(first)
        def _():
            for s_ref in states:
                s_ref[...] = jnp.zeros_like(s_ref)

        i_at = o_at = 0
        for k, (fn, rows, n_halo, pars, shapes) in enumerate(parts):
            n_r = len(rows)
            row_refs, halo_refs = ins[i_at:i_at + n_r], ins[i_at + n_r:i_at + n_r + n_halo]
            par_refs = ins[i_at + n_r + n_halo:i_at + n_in[k]]
            ck_ref, saved_refs = outs[o_at], outs[o_at + 1:o_at + n_out[k]]
            S = states[k][...]
            ck_ref[0] = S
            halos = [jnp.where(first, 0.0, r[...]) for r in halo_refs]
            y, S_new, saved = fn(S, *halos, *[r[...] for r in row_refs], *[r[...] for r in par_refs])
            for g in range(NB):
                y_ref[:, k * width + g * D:k * width + (g + 1) * D] = y[g].astype(BF16)
            for o_ref, val in zip(saved_refs, saved):
                o_ref[0] = val
            states[k][...] = S_new
            i_at, o_at = i_at + n_in[k], o_at + n_out[k]

    in_specs, args, out_specs, out_shape = [], [], [], []
    for (_, rows, n_halo, pars, shapes) in parts:
        in_specs += _row_specs(rows, n_halo, lambda c: c) + [pl.BlockSpec(p.shape, lambda c: (0, 0)) for p in pars]
        args += [r[0] for r in rows] + [r[0] for r in rows[:n_halo]] + list(pars)
        for shp in [(NB, D, D)] + list(shapes):
            out_specs.append(pl.BlockSpec((1,) + tuple(shp), lambda c: (c, 0, 0, 0)))
            out_shape.append(jax.ShapeDtypeStruct((nc,) + tuple(shp), F32))
    res = pl.pallas_call(
        body, name=name, grid=(nc,), in_specs=in_specs,
        out_specs=[pl.BlockSpec((C, width * len(parts)), lambda c: (c, 0))] + out_specs,
        out_shape=[jax.ShapeDtypeStruct((Lp, width * len(parts)), BF16)] + out_shape,
        scratch_shapes=[pltpu.VMEM((NB, D, D), F32)] * len(parts),
        compiler_params=_params(("arbitrary",)),
    )(*args)
    per_part, at = [], 1
    for n in n_out:
        per_part.append(list(res[at:at + n]))
        at += n
    return res[0], per_part


def _mixers_bwd(name, parts, kept, dy):
    Lp = parts[0][1][0][0].shape[0]
    C, D, NB = CHUNK, GROUP, N_GROUPS
    nc = Lp // C
    n_in = [len(rows) + n_halo + len(pars) + len(kept[k]) for k, (_, rows, n_halo, pars, _) in enumerate(parts)]
    widths = [sum(w for (_, w, _) in rows) for (_, rows, *_) in parts]
    n_par = [len(pars) for (_, _, _, pars, _) in parts]
    n_scr = [1 + n_halo for (_, _, n_halo, _, _) in parts]
    rev = lambda c: nc - 1 - c

    def body(*refs):
        ins, dy_ref = refs[:sum(n_in)], refs[sum(n_in)]
        drows_ref = refs[sum(n_in) + 1]
        dpar_all = refs[sum(n_in) + 2:sum(n_in) + 2 + sum(n_par)]
        scratch = refs[sum(n_in) + 2 + sum(n_par):]
        i = pl.program_id(0)

        @pl.when(i == 0)
        def _():
            for o_ref in list(dpar_all) + list(scratch):
                o_ref[...] = jnp.zeros_like(o_ref)

        i_at = p_at = s_at = col = 0
        for k, (fn, rows, n_halo, pars, _) in enumerate(parts):
            n_r = len(rows)
            row_refs, halo_refs = ins[i_at:i_at + n_r], ins[i_at + n_r:i_at + n_r + n_halo]
            par_refs = ins[i_at + n_r + n_halo:i_at + n_r + n_halo + n_par[k]]
            kept_refs = ins[i_at + n_r + n_halo + n_par[k]:i_at + n_in[k]]
            dpar_refs = dpar_all[p_at:p_at + n_par[k]]
            ds_ref, carry_refs = scratch[s_at], scratch[s_at + 1:s_at + n_scr[k]]
            halos = [jnp.where(i == nc - 1, 0.0, r[...]) for r in halo_refs]
            vals = [kept_refs[0][0]] + halos + [r[...] for r in row_refs] + [r[...] for r in par_refs]
            stored = tuple(r[0] for r in kept_refs[1:])
            d_y = _groups(dy_ref[:, k * NB * D:(k + 1) * NB * D])
            grads = jax.vjp(lambda *v, fn=fn, stored=stored: fn(*v, saved=stored)[:2], *vals)[1]((d_y, ds_ref[...]))
            ds_ref[...] = grads[0]
            d_halos, d_rows = grads[1:1 + n_halo], grads[1 + n_halo:1 + n_halo + n_r]
            for j, g in enumerate(d_rows):
                if j < n_halo:
                    g = g + jnp.concatenate([jnp.zeros((C - HALO, g.shape[1]), F32), carry_refs[j][...]], axis=0)
                    carry_refs[j][...] = d_halos[j]
                drows_ref[:, col:col + g.shape[1]] = g.astype(BF16)
                col += g.shape[1]
            for o_ref, g in zip(dpar_refs, grads[1 + n_halo + n_r:]):
                o_ref[...] += g
            i_at, p_at, s_at = i_at + n_in[k], p_at + n_par[k], s_at + n_scr[k]

    in_specs, args, out_specs, out_shape, scratch_shapes = [], [], [], [], []
    for k, (_, rows, n_halo, pars, _) in enumerate(parts):
        in_specs += _row_specs(rows, n_halo, rev) + [pl.BlockSpec(p.shape, lambda c: (0, 0)) for p in pars]
        in_specs += [pl.BlockSpec((1,) + tuple(t.shape[1:]), lambda c: (rev(c), 0, 0, 0)) for t in kept[k]]
        args += [r[0] for r in rows] + [r[0] for r in rows[:n_halo]] + list(pars) + list(kept[k])
        out_specs += [pl.BlockSpec(p.shape, lambda c: (0, 0)) for p in pars]
        out_shape += [jax.ShapeDtypeStruct(p.shape, F32) for p in pars]
        scratch_shapes += [pltpu.VMEM((NB, D, D), F32)] + [pltpu.VMEM((HALO, w), F32) for (_, w, _) in rows[:n_halo]]
    res = pl.pallas_call(
        body, name=name, grid=(nc,),
        in_specs=in_specs + [pl.BlockSpec((C, dy.shape[1]), lambda c: (rev(c), 0))],
        out_specs=[pl.BlockSpec((C, sum(widths)), lambda c: (rev(c), 0))] + out_specs,
        out_shape=[jax.ShapeDtypeStruct((Lp, sum(widths)), BF16)] + out_shape,
        scratch_shapes=scratch_shapes, compiler_params=_params(("arbitrary",)),
    )(*args, dy)
    d_pars, at = [], 1
    for n in n_par:
        d_pars.append(list(res[at:at + n]))
        at += n
    return res[0], d_pars


def _loss_head(h, yw, tgt, fw, n_real, tm=2 * ROW_TILE):
    Lp, Dm = h.shape
    tm = _pick(Lp, tm, 16)

    def out_fn(z, fw_):
        return z * lax.rsqrt(jnp.mean(z * z, axis=-1, keepdims=True) + NORM_EPS) * fw_

    def body(h_ref, yw_ref, t_ref, fw_ref, loss_ref, dz_ref, dz16_ref, dfw_ref):
        i = pl.program_id(0)

        @pl.when(i == 0)
        def _():
            loss_ref[...] = jnp.zeros_like(loss_ref)
            dfw_ref[...] = jnp.zeros_like(dfw_ref)

        row = i * tm + lax.broadcasted_iota(jnp.int32, (tm, 1), 0)
        mask = ((row >= N_META) & (row < n_real)).astype(F32)
        z = h_ref[...] + yw_ref[...]
        o, vjp = jax.vjp(out_fn, z, fw_ref[...])
        err = (o - t_ref[...]) * mask
        row_loss = 0.5 * jnp.mean(jnp.square(err), axis=-1, keepdims=True)
        dz, dfw = vjp(err * (1.0 / Dm))
        loss_ref[...] += jnp.sum(row_loss, axis=0, keepdims=True)
        dz_ref[...] = dz
        dz16_ref[...] = dz.astype(BF16)
        dfw_ref[...] += dfw

    row_spec = pl.BlockSpec((tm, Dm), lambda i: (i, 0))
    return pl.pallas_call(
        body, name="loss_head", grid=(Lp // tm,),
        in_specs=[row_spec, row_spec, row_spec, pl.BlockSpec((1, Dm), lambda i: (0, 0))],
        out_specs=[pl.BlockSpec((8, 128), lambda i: (0, 0)), row_spec, row_spec,
                   pl.BlockSpec((1, Dm), lambda i: (0, 0))],
        out_shape=[jax.ShapeDtypeStruct((8, 128), F32), jax.ShapeDtypeStruct((Lp, Dm), F32),
                   jax.ShapeDtypeStruct((Lp, Dm), BF16), jax.ShapeDtypeStruct((1, Dm), F32)],
        compiler_params=_params(("arbitrary",)),
    )(h, yw, tgt, fw)


def _exchange(name, x, masks, slot_kind, per_dest, n_split=1, copy_own=True, other_half=False):
    n = len(masks)
    keep_own = slot_kind is not None and copy_own
    n_slots = {"chip": 4, "core": 2, "dev": 8, None: n}[slot_kind]
    blk_shape = x.shape[1:] if per_dest else x.shape
    if other_half:
        blk_shape = (x.shape[0] // 2,) + tuple(x.shape[1:])
    rows = blk_shape[0] // n_split

    def body(x_ref, o_ref, send_sems, recv_sems, local_sems):
        mx, my, mc = lax.axis_index("x"), lax.axis_index("y"), lax.axis_index("c")
        if other_half:
            x_ref = x_ref.at[pl.ds((1 - mc) * blk_shape[0], blk_shape[0])]

        def slot(k, px, py, pc):
            return {"chip": 2 * px + py, "core": pc, "dev": 4 * px + 2 * py + pc, None: k}[slot_kind]

        def peer(m):
            return (mx + m[0]) % 2, (my + m[1]) % 2, (mc + m[2]) % 2

        def part(ref, j):
            return ref.at[pl.ds(j * rows, rows)]

        own_src = x_ref.at[2 * mx + my] if per_dest else x_ref
        local = []
        if keep_own:
            own_dst = o_ref.at[slot(0, mx, my, mc)]
            local = [pltpu.make_async_copy(part(own_src, j), part(own_dst, j), local_sems.at[j])
                     for j in range(n_split)]
        for cp in local:
            cp.start()
        sends = []
        for k, m in enumerate(masks):
            px, py, pc = peer(m)
            src = x_ref.at[2 * px + py] if per_dest else x_ref
            dst = o_ref.at[slot(k, mx, my, mc)]
            for j in range(n_split):
                sends.append(pltpu.make_async_remote_copy(
                    src_ref=part(src, j), dst_ref=part(dst, j), send_sem=send_sems.at[k * n_split + j],
                    recv_sem=recv_sems.at[k * n_split + j], device_id=(px, py, pc), device_id_type=MESH))
        for cp in sends:
            cp.start()
        for k, m in enumerate(masks):
            px, py, pc = peer(m)
            landed = o_ref.at[slot(k, px, py, pc)]
            for j in range(n_split):
                pltpu.make_async_remote_copy(
                    src_ref=part(own_src, j), dst_ref=part(landed, j), send_sem=send_sems.at[k * n_split + j],
                    recv_sem=recv_sems.at[k * n_split + j], device_id=(px, py, pc), device_id_type=MESH).wait_recv()
        for cp in sends:
            cp.wait_send()
        for cp in local:
            cp.wait()

    return pl.pallas_call(
        body, name=name,
        in_specs=[pl.BlockSpec(memory_space=pl.ANY)], out_specs=pl.BlockSpec(memory_space=pl.ANY),
        out_shape=jax.ShapeDtypeStruct((n_slots,) + tuple(blk_shape), x.dtype),
        scratch_shapes=[pltpu.SemaphoreType.DMA((n * n_split,)), pltpu.SemaphoreType.DMA((n * n_split,)),
                        pltpu.SemaphoreType.DMA((n_split,))],
        compiler_params=pltpu.CompilerParams(has_side_effects=True),
    )(x)


CHIP_MASKS = [(1, 0, 0), (0, 1, 0), (1, 1, 0)]
CORE_MASKS = [(0, 0, 1)]
ALL_MASKS = [(dx, dy, dc) for dx in (0, 1) for dy in (0, 1) for dc in (0, 1) if (dx, dy, dc) != (0, 0, 0)]

HBM_SPEC = pl.BlockSpec(memory_space=pltpu.HBM)
SEM_SPEC = pl.BlockSpec(memory_space=pltpu.SEMAPHORE)
DATAFLOW = pltpu.SideEffectType.DATAFLOW_SIDE_EFFECTING


def _split_copies(x_ref, land_ref, send_sems, recv_sems, masks, slot_kind, per_dest, n_split, with_recvs=True):
    mx, my, mc = lax.axis_index("x"), lax.axis_index("y"), lax.axis_index("c")
    slot = lambda px, py, pc: {"chip": 2 * px + py, "core": pc}[slot_kind]
    rows = (x_ref.shape[1] if per_dest else x_ref.shape[0]) // n_split
    part = lambda ref, j: ref.at[pl.ds(j * rows, rows)]
    sends, recvs = [], []
    for k, m in enumerate(masks):
        px, py, pc = (mx + m[0]) % 2, (my + m[1]) % 2, (mc + m[2]) % 2
        src = x_ref.at[2 * px + py] if per_dest else x_ref
        own_src = x_ref.at[2 * mx + my] if per_dest else x_ref
        for j in range(n_split):
            sems = dict(send_sem=send_sems.at[k * n_split + j], recv_sem=recv_sems.at[k * n_split + j],
                        device_id=(px, py, pc), device_id_type=MESH)
            sends.append(pltpu.make_async_remote_copy(
                src_ref=part(src, j), dst_ref=part(land_ref.at[slot(mx, my, mc)], j), **sems))
            if with_recvs:
                recvs.append(pltpu.make_async_remote_copy(
                    src_ref=part(own_src, j), dst_ref=part(land_ref.at[slot(px, py, pc)], j), **sems))
    return sends, recvs


def _exchange_start(name, x, masks, slot_kind, per_dest, n_split, dep=None):
    n = len(masks) * n_split
    blk_shape = x.shape[1:] if per_dest else x.shape
    land_shape = ({"chip": 4, "core": 2}[slot_kind],) + tuple(blk_shape)
    deps = [] if dep is None else [dep]

    def body(x_ref, land_ref, *rest):
        send_sems, recv_sems, x_thru, land_thru, token = rest[len(deps):]
        for cp in _split_copies(x_ref, land_ref, send_sems, recv_sems, masks, slot_kind, per_dest, n_split, False)[0]:
            cp.start()
        token[...] = jnp.zeros_like(token)

    return pl.pallas_call(
        body, name=name,
        out_shape=(pltpu.SemaphoreType.DMA((n,)), pltpu.SemaphoreType.DMA((n,)), pltpu.HBM(x.shape, x.dtype),
                   pltpu.HBM(land_shape, x.dtype), jax.ShapeDtypeStruct((8, 128), F32)),
        in_specs=(HBM_SPEC, HBM_SPEC) + (pl.BlockSpec(memory_space=pl.ANY),) * len(deps),
        out_specs=(SEM_SPEC, SEM_SPEC, HBM_SPEC, HBM_SPEC, pl.BlockSpec(memory_space=pltpu.VMEM)),
        input_output_aliases={0: 2, 1: 3},
        compiler_params=pltpu.CompilerParams(has_side_effects=DATAFLOW),
    )(pltpu.with_memory_space_constraint(x, pltpu.HBM),
      pltpu.with_memory_space_constraint(lax.empty(land_shape, x.dtype), pltpu.HBM), *deps)


def _exchange_wait(name, started, after, masks, slot_kind, per_dest, n_split):
    send_sems, recv_sems, x_thru, land_thru, _ = started

    def body(x_ref, land_ref, send_sems, recv_sems, after_ref, x_out, land_out):
        sends, recvs = _split_copies(x_ref, land_ref, send_sems, recv_sems, masks, slot_kind, per_dest, n_split)
        for cp in sends:
            cp.wait_send()
        for cp in recvs:
            cp.wait_recv()

    return pl.pallas_call(
        body, name=name,
        out_shape=(pltpu.HBM(x_thru.shape, x_thru.dtype), pltpu.HBM(land_thru.shape, land_thru.dtype)),
        in_specs=(HBM_SPEC, HBM_SPEC, SEM_SPEC, SEM_SPEC, pl.BlockSpec(memory_space=pl.ANY)),
        out_specs=(HBM_SPEC, HBM_SPEC), input_output_aliases={0: 0, 1: 1},
        compiler_params=pltpu.CompilerParams(has_side_effects=DATAFLOW),
    )(x_thru, land_thru, send_sems, recv_sems, after)


def _gather_chips(name, x):
    return _exchange(name, x, CHIP_MASKS, "chip", False)


def _gather_shards(name, shard, n_split, overlap=False, dep=None):
    half = shard.shape[0] // 2
    mine = lax.dynamic_slice_in_dim(shard, lax.axis_index("c") * half, half, axis=0)
    if overlap:
        return _exchange_start(name + "_chips", mine, CHIP_MASKS, "chip", False, n_split, dep)
    by_chip = _exchange(name + "_chips", mine, CHIP_MASKS, "chip", False, n_split, copy_own=False)
    return _gather_tail(name, mine, by_chip)


def _gather_tail(name, mine, by_chip):
    c, chip = lax.axis_index("c"), 2 * lax.axis_index("x") + lax.axis_index("y")
    by_chip = lax.dynamic_update_index_in_dim(by_chip, mine, chip, 0)
    both = _exchange(name + "_cores", by_chip, CORE_MASKS, "core", False, N_CHIPS, copy_own=False)
    return lax.dynamic_update_index_in_dim(both, by_chip, c, 0)


def _gather_finish(name, started, after, n_split):
    mine, by_chip = _exchange_wait(name + "_chips_wait", started, after, CHIP_MASKS, "chip", False, n_split)
    return _gather_tail(name, mine, by_chip)


def _sum_slots(name, x, tr=128):
    S, R, N = x.shape
    tr = _pick(R, tr, 16)

    def body(x_ref, o_ref):
        acc = x_ref[0].astype(F32)
        for s in range(1, S):
            acc = acc + x_ref[s].astype(F32)
        o_ref[...] = acc

    return pl.pallas_call(
        body, name=name, grid=(R // tr,),
        in_specs=[pl.BlockSpec((S, tr, N), lambda i: (0, i, 0))], out_specs=pl.BlockSpec((tr, N), lambda i: (i, 0)),
        out_shape=jax.ShapeDtypeStruct((R, N), F32), compiler_params=_params(("parallel",)),
    )(x)


def _add_to_bf16(name, a, b, tr=128):
    S, R, N = a.shape
    tr = _pick(R, tr, 16)

    def body(a_ref, b_ref, o_ref):
        o_ref[...] = (a_ref[...] + b_ref[...]).astype(BF16)

    spec = pl.BlockSpec((S, tr, N), lambda i: (0, i, 0))
    return pl.pallas_call(
        body, name=name, grid=(R // tr,), in_specs=[spec, spec], out_specs=spec,
        out_shape=jax.ShapeDtypeStruct((S, R, N), BF16), compiler_params=_params(("parallel",)),
    )(a, b)


def _add_slabs_to_bf16(name, full, recv, c, tr=64):
    R, NP = full.shape
    half = R // 2
    nb = half // tr

    def body(c_ref, a_ref, b_ref, o_ref):
        x = a_ref[...] + b_ref[...]
        for s in range(N_CHIPS - 1):
            o_ref[s] = x[:, s * SHARD_COLS:(s + 1) * SHARD_COLS].astype(BF16)
        last = jnp.concatenate([x[:, (N_CHIPS - 1) * SHARD_COLS:GAP_AT], x[:, GAP_AT + GAP:]], axis=1)
        o_ref[N_CHIPS - 1] = last.astype(BF16)

    grid_spec = pltpu.PrefetchScalarGridSpec(
        num_scalar_prefetch=1, grid=(nb,),
        in_specs=[pl.BlockSpec((tr, NP), lambda i, c_ref: (c_ref[0] * nb + i, 0)),
                  pl.BlockSpec((tr, NP), lambda i, c_ref: (i, 0))],
        out_specs=pl.BlockSpec((N_CHIPS, tr, SHARD_COLS), lambda i, c_ref: (0, i, 0)))
    return pl.pallas_call(
        body, name=name, grid_spec=grid_spec,
        out_shape=jax.ShapeDtypeStruct((N_CHIPS, half, SHARD_COLS), BF16), compiler_params=_params(("parallel",)),
    )(jnp.reshape(c, (1,)).astype(jnp.int32), full, recv)


def _adamw(name, gparts, w, m, v, tr=128):
    S, R, N = gparts.shape
    tr = _pick(R, tr)
    c1 = 1.0 / (1.0 - ADAM_B1 ** ADAM_STEP)
    c2 = 1.0 / (1.0 - ADAM_B2 ** ADAM_STEP)

    def body(g_ref, w_ref, m_ref, v_ref, go_ref, d_ref, mo_ref, vo_ref):
        g = g_ref[0]
        for s in range(1, S):
            g = g + g_ref[s]
        m_new = ADAM_B1 * m_ref[...] + (1.0 - ADAM_B1) * g
        v_new = ADAM_B2 * v_ref[...] + (1.0 - ADAM_B2) * jnp.square(g)
        go_ref[...] = g
        mo_ref[...] = m_new
        vo_ref[...] = v_new
        d_ref[...] = -ADAM_LR * ((m_new * c1) / (jnp.sqrt(v_new * c2) + ADAM_EPS) + ADAM_WD * w_ref[...])

    spec = pl.BlockSpec((tr, N), lambda i: (i, 0))
    return pl.pallas_call(
        body, name=name, grid=(R // tr,),
        in_specs=[pl.BlockSpec((S, tr, N), lambda i: (0, i, 0)), spec, spec, spec], out_specs=[spec] * 4,
        out_shape=[jax.ShapeDtypeStruct((R, N), F32)] * 4, compiler_params=_params(("parallel",)),
    )(gparts, w, m, v)


def _reduce_to_shard(name, slabs, n_split, by_columns=False, overlap=False):
    c, chip = lax.axis_index("c"), 2 * lax.axis_index("x") + lax.axis_index("y")
    if by_columns:
        R, N = slabs.shape[0], SHARD_COLS
        from_sibling = _exchange(name + "_sib", slabs, CORE_MASKS, None, False, n_split, other_half=True)[0]
        wire = _add_slabs_to_bf16(name + "_add", slabs, from_sibling, c)
    else:
        _, R, N = slabs.shape
        half = R // 2
        halves = slabs.reshape(N_CHIPS, 2, half, N)
        mine = lax.dynamic_index_in_dim(halves, c, axis=1, keepdims=False)
        theirs = lax.dynamic_index_in_dim(halves, 1 - c, axis=1, keepdims=False)
        from_sibling = _exchange(name + "_sib", theirs, CORE_MASKS, None, False, N_CHIPS)[0]
        wire = _add_to_bf16(name + "_add", mine, from_sibling)
    if overlap:
        return _exchange_start(name + "_chips", wire, CHIP_MASKS, "chip", True, n_split), (R, N)
    got = _exchange(name + "_chips", wire, CHIP_MASKS, "chip", True, n_split, copy_own=False)
    return _reduce_tail(name, wire, got, n_split, R, N)


def _reduce_tail(name, wire, got, n_split, R, N):
    c, chip = lax.axis_index("c"), 2 * lax.axis_index("x") + lax.axis_index("y")
    got = lax.dynamic_update_index_in_dim(got, lax.dynamic_index_in_dim(wire, chip, 0, keepdims=False), chip, 0)
    part = _sum_slots(name + "_sum", got)
    both = _exchange(name + "_cores", part, CORE_MASKS, "core", False, n_split, copy_own=False)
    return lax.dynamic_update_index_in_dim(both, part, c, 0).reshape(1, R, N)


def _reduce_finish(name, started, after, n_split):
    handle, (R, N) = started
    wire, got = _exchange_wait(name + "_chips_wait", handle, after, CHIP_MASKS, "chip", True, n_split)
    return _reduce_tail(name, wire, got, n_split, R, N)


def _pack(pieces, cols, row_mult=8):
    flat = jnp.concatenate([p.reshape(-1) for p in pieces])
    rows = -(-flat.shape[0] // cols)
    rows = -(-rows // row_mult) * row_mult
    return jnp.pad(flat, (0, rows * cols - flat.shape[0])).reshape(rows, cols)


def _unpack(packed, shapes):
    flat = packed.reshape(-1)
    out, off = [], 0
    for shp in shapes:
        n = 1
        for d in shp:
            n *= d
        out.append(flat[off:off + n].reshape(shp))
        off += n
    return out


def kernel(x, meta_tokens, norm_w, w_in, rw_shift_mu, rw_w0, rw_w2, rw_a0, rw_a2, rw_k_k, rw_k_a, rw_r_k, rw_gn_w, rw_gn_b, dn_conv_w, dn_A_log, dn_dt_bias, dn_norm_w, w_out, final_norm_w, loss_target, m_meta_tokens, m_norm_w, m_w_in, m_rw_shift_mu, m_rw_w0, m_rw_w2, m_rw_a0, m_rw_a2, m_rw_k_k, m_rw_k_a, m_rw_r_k, m_rw_gn_w, m_rw_gn_b, m_dn_conv_w, m_dn_A_log, m_dn_dt_bias, m_dn_norm_w, m_w_out, m_final_norm_w, v_meta_tokens, v_norm_w, v_w_in, v_rw_shift_mu, v_rw_w0, v_rw_w2, v_rw_a0, v_rw_a2, v_rw_k_k, v_rw_k_a, v_rw_r_k, v_rw_gn_w, v_rw_gn_b, v_dn_conv_w, v_dn_A_log, v_dn_dt_bias, v_dn_norm_w, v_w_out, v_final_norm_w):
    S = x.shape[1]
    L = N_META + S
    Lp = -(-L // CHUNK) * CHUNK

    small_shapes = [(RW_LORA, 256), (RW_LORA, 256), (CONV_W, 768), (N_META, 512)]
    small_mine = _pack([rw_w2[0], rw_a2[0], dn_conv_w[0], meta_tokens], 1024)
    small_all = _gather_chips("gather_small", small_mine)
    per_chip = [_unpack(small_all[s], small_shapes) for s in range(N_CHIPS)]
    w2, a2, conv_w, meta = [jnp.concatenate([per_chip[s][i] for s in range(N_CHIPS)], axis=1) for i in range(4)]
    w_in_started = _gather_shards("gather_w_in", w_in[0].astype(BF16), 8, overlap=True, dep=small_all)
    w_out_started = _gather_shards("gather_w_out", w_out[0].astype(BF16), 8, overlap=True, dep=w_in_started[4])

    tail = [jnp.zeros((Lp - L, D_MODEL), F32)] if Lp > L else []
    h = jnp.concatenate([meta + w_in_started[4][:1, :1], x[0]] + tail, axis=0)
    tgt = jnp.concatenate([jnp.zeros((N_META, D_MODEL), F32), loss_target[0]] + tail, axis=0)
    (u,) = _rowwise("rms_in", _rms_fn, [_row(h)], [norm_w], [D_MODEL], tm=2 * ROW_TILE, out_dtype=BF16)
    w_in_all = _gather_finish("gather_w_in", w_in_started, u, 8)
    slabs = [w_in_all[:, s].reshape(D_MODEL, SHARD_COLS) for s in range(N_CHIPS)]
    cut = GAP_AT - (N_CHIPS - 1) * SHARD_COLS
    W_gapped = jnp.concatenate(slabs[:-1] + [slabs[-1][:, :cut], jnp.zeros((D_MODEL, GAP), BF16), slabs[-1][:, cut:]],
                               axis=1)
    p = _mm("in_proj", u, W_gapped, "nn", dep=w_out_started[4])

    mu = rw_shift_mu
    zpad = jnp.zeros((RW_LORA, RW_WIDTH), F32)
    rw_params = [mu[:, 0:1024], mu[:, 1024:2048], mu[:, 2048:3072], mu[:, 3072:3200], rw_w0,
                 jnp.concatenate([w2, zpad], axis=0), rw_a0, jnp.concatenate([zpad, a2], axis=0), rw_k_k, rw_k_a]
    rw_rows = [_row(p, 1024, OFF_R), _row(p, 1024, OFF_K), _row(p, 1024, OFF_V), _row(p, 128, OFF_S1),
               _row(p, 1024, OFF_GATE)]
    rw_pars = rw_params + [rw_r_k, rw_gn_w, rw_gn_b]
    rw_saved_shapes = [(N_GROUPS, CHUNK, GROUP), (N_GROUPS, CHUNK, GROUP), (N_GROUPS, 2 * CHUNK, 2 * GROUP),
                       (N_GROUPS, 2 * CHUNK, GROUP)]

    dn_rows = [_row(p, 1024, OFF_DQ), _row(p, 1024, OFF_DK), _row(p, 1024, OFF_DV), _row(p, 128, OFF_S2),
               _row(p, 1024, OFF_Z)]
    dn_pars = [conv_w[j:j + 1, 1024 * i:1024 * (i + 1)] for i in range(3) for j in range(CONV_W)]
    narrow = lambda t: jnp.pad(t, ((0, 0), (DN_HEADS, DN_HEAD - 2 * DN_HEADS)))
    dn_pars += [narrow(dn_A_log), narrow(dn_dt_bias), jnp.tile(dn_norm_w, (1, DN_HEADS))]
    dn_saved_shapes = [(N_GROUPS, CHUNK, CHUNK), (N_GROUPS, CHUNK, GROUP)]
    mixer_parts = [(_rw_fused_fn, rw_rows, 4, rw_pars, rw_saved_shapes),
                   (_dn_fused_fn, dn_rows, 3, dn_pars, dn_saved_shapes)]
    y, mixer_kept = _mixers_fwd("mixers_fwd", mixer_parts)
    w_out_all = _gather_finish("gather_w_out", w_out_started, y, 8)
    Wo = jnp.concatenate([w_out_all[:, s].reshape(D_MODEL // N_CHIPS, D_MODEL) for s in range(N_CHIPS)], axis=0)
    yw = _mm("out_proj", y, Wo, "nn", tn=1024)
    loss_acc, dz, dz16, d_fw = _loss_head(h, yw, tgt, final_norm_w.reshape(1, D_MODEL), L)
    loss = lax.psum(loss_acc[0, 0], ("x", "y", "c"))

    d_wo = _mm("d_w_out", y.T, dz16, "nn", tm=512, tn=1024, tk=Lp)
    wo_started = _reduce_to_shard("rs_w_out", d_wo.reshape(N_CHIPS, D_MODEL // N_CHIPS, D_MODEL), 8, overlap=True)
    dy = _mm("d_out_proj", dz16, Wo, "nt", tn=1024, tk=2048, dep=wo_started[0][4])

    dp, (d_rw_pars, d_dn_pars) = _mixers_bwd("mixers_bwd", mixer_parts, mixer_kept, dy)
    d_prep_pars, d_rw_pars = d_rw_pars[:10], d_rw_pars[10:]
    d_conv_parts = [jnp.concatenate(d_dn_pars[4 * i:4 * i + 4], axis=0) for i in range(3)]
    d_a_log_b, d_dt_b = d_dn_pars[12], d_dn_pars[13]
    d_dn_nw = jnp.sum(d_dn_pars[14].reshape(DN_HEADS, DN_HEAD), axis=0, keepdims=True)
    d_W = _mm("d_w_in", u.T, dp, "nn", tm=1024, tn=768, tk=Lp)
    wi_started = _reduce_to_shard("rs_w_in", d_W, 8, by_columns=True, overlap=True)
    du = _mm("d_in_proj", dp, W_gapped, "nt", tn=2048, tk=1408, dep=wi_started[0][4])
    (dh,), (d_norm_w,) = _rowwise_bwd("rms_in_bwd", lambda h_, w_: (_rms_fn(h_, w_)[0], h_), [_row(h)], [norm_w],
                                      [_row(du), _row(dz)])
    grad_x = dh[N_META:L][None]

    d_mu = jnp.concatenate(d_prep_pars[0:4], axis=1)
    d_w2, d_a2 = d_prep_pars[5][:RW_LORA], d_prep_pars[7][RW_LORA:]
    d_conv = jnp.concatenate(d_conv_parts, axis=1)
    d_meta = dh[:N_META]
    head_sum = lambda t: t[:, DN_HEADS:2 * DN_HEADS]
    rep_names = ["norm_w", "rw_shift_mu", "rw_w0", "rw_a0", "rw_k_k", "rw_k_a", "rw_r_k", "rw_gn_w", "rw_gn_b",
                 "dn_A_log", "dn_dt_bias", "dn_norm_w", "final_norm_w"]
    rep_g = [d_norm_w, d_mu, d_prep_pars[4], d_prep_pars[6], d_prep_pars[8], d_prep_pars[9],
             d_rw_pars[0], d_rw_pars[1], d_rw_pars[2],
             head_sum(d_a_log_b), head_sum(d_dt_b), d_dn_nw, d_fw.reshape(D_MODEL)]
    rep_w = [norm_w, rw_shift_mu, rw_w0, rw_a0, rw_k_k, rw_k_a, rw_r_k, rw_gn_w, rw_gn_b, dn_A_log, dn_dt_bias,
             dn_norm_w, final_norm_w]
    rep_m = [m_norm_w, m_rw_shift_mu, m_rw_w0, m_rw_a0, m_rw_k_k, m_rw_k_a, m_rw_r_k, m_rw_gn_w, m_rw_gn_b,
             m_dn_A_log, m_dn_dt_bias, m_dn_norm_w, m_final_norm_w]
    rep_v = [v_norm_w, v_rw_shift_mu, v_rw_w0, v_rw_a0, v_rw_k_k, v_rw_k_a, v_rw_r_k, v_rw_gn_w, v_rw_gn_b,
             v_dn_A_log, v_dn_dt_bias, v_dn_norm_w, v_final_norm_w]
    rep_shapes = [t.shape for t in rep_w]
    rep_all = _exchange("gather_rep_grads", _pack(rep_g, 128), ALL_MASKS, "dev", False)
    rep_out = _adamw("adam_rep", rep_all, _pack(rep_w, 128), _pack(rep_m, 128), _pack(rep_v, 128))
    rep_out = [dict(zip(rep_names, _unpack(t, rep_shapes))) for t in rep_out]

    sm_slabs = jnp.stack([_pack([d_w2[:, 256 * s:256 * (s + 1)], d_a2[:, 256 * s:256 * (s + 1)],
                                 d_conv[:, 768 * s:768 * (s + 1)], d_meta[:, 512 * s:512 * (s + 1)]], 1024, 64)
                          for s in range(N_CHIPS)])
    sm_parts = _reduce_to_shard("rs_small", sm_slabs, 1)
    sm_w = [rw_w2[0], rw_a2[0], dn_conv_w[0], meta_tokens]
    sm_m = [m_rw_w2[0], m_rw_a2[0], m_dn_conv_w[0], m_meta_tokens]
    sm_v = [v_rw_w2[0], v_rw_a2[0], v_dn_conv_w[0], v_meta_tokens]
    sm_out = _adamw("adam_small", sm_parts, _pack(sm_w, 1024, 64), _pack(sm_m, 1024, 64), _pack(sm_v, 1024, 64))
    sm_names = ["rw_w2", "rw_a2", "dn_conv_w", "meta_tokens"]
    sm_full_shapes = [(1, RW_LORA, 256), (1, RW_LORA, 256), (1, CONV_W, 768), (N_META, 512)]
    sm_out = [dict(zip(sm_names, [t.reshape(shp) for t, shp in zip(_unpack(o, small_shapes), sm_full_shapes)]))
              for o in sm_out]

    wo_parts = _reduce_finish("rs_w_out", wo_started, dp, 8)
    wo_out = _adamw("adam_w_out", wo_parts, w_out[0], m_w_out[0], v_w_out[0])
    wi_parts = _reduce_finish("rs_w_in", wi_started, dh, 8)
    wi_out = _adamw("adam_w_in", wi_parts, w_in[0], m_w_in[0], v_w_in[0])

    order = ["meta_tokens", "norm_w", "w_in", "rw_shift_mu", "rw_w0", "rw_w2", "rw_a0", "rw_a2", "rw_k_k", "rw_k_a",
             "rw_r_k", "rw_gn_w", "rw_gn_b", "dn_conv_w", "dn_A_log", "dn_dt_bias", "dn_norm_w", "w_out",
             "final_norm_w"]
    outs = [loss, grad_x]
    for kind in range(4):
        table = dict(rep_out[kind])
        table.update(sm_out[kind])
        table["w_in"] = wi_out[kind][None]
        table["w_out"] = wo_out[kind][None]
        outs += [table[n] for n in order]
    return tuple(outs)
```

```python
import functools

import jax
import jax.numpy as jnp
from jax import lax
from jax.experimental import pallas as pl
from jax.experimental.pallas import tpu as pltpu

F32 = jnp.float32
BF16 = jnp.bfloat16
HIGH = lax.Precision.HIGH
HIGHEST = lax.Precision.HIGHEST
MESH = pl.DeviceIdType.MESH

D_MODEL = 2048
N_META = 16
RW_WIDTH = 1024
RW_HEAD = 64
RW_HEADS = 16
RW_LORA = 64
RW_GN_EPS = 64e-5
DN_WIDTH = 1024
DN_HEAD = 128
DN_HEADS = 8
CONV_W = 4
CHUNK = 64
NORM_EPS = 1e-6
IN_COLS = 8336
N_CHIPS = 4
SHARD_COLS = IN_COLS // N_CHIPS

NP_COLS = 8 * 1024 + 256
GAP_AT = IN_COLS - DN_WIDTH
GAP = NP_COLS - IN_COLS
OFF_R, OFF_K, OFF_V, OFF_S1, OFF_GATE = 0, 1024, 2048, 3072, 3200
OFF_DQ, OFF_DK, OFF_DV, OFF_S2, OFF_Z = 4224, 5248, 6272, 7296, 7424

ADAM_LR = 0.001
ADAM_B1 = 0.9
ADAM_B2 = 0.999
ADAM_EPS = 1e-08
ADAM_WD = 0.01
ADAM_STEP = 10

VMEM_LIMIT_BYTES = 56 * 1024 * 1024
ROW_TILE = 104
MM_ROW_TILE = 832


def _params(sem=None):
    return pltpu.CompilerParams(dimension_semantics=sem, vmem_limit_bytes=VMEM_LIMIT_BYTES)


def _pick(n, target, mult=8):
    best = None
    for d in range(mult, min(n, target) + 1, mult):
        if n % d == 0:
            best = d
    return n if best is None else best


def _mm(name, a, b, mode, tm=MM_ROW_TILE, tn=1408, tk=2048, dep=None):
    if mode == "nn":
        (M, K), (_, N) = a.shape, b.shape
    elif mode == "nt":
        (M, K), (N, _) = a.shape, b.shape
    else:
        (K, M), (_, N) = a.shape, b.shape
    tm = _pick(M, tm, 128 if mode == "tn" else 16)
    tn = _pick(N, tn, 128)
    tk = _pick(K, tk, 8 if mode == "tn" else 128)
    if mode == "nn":
        a_spec = pl.BlockSpec((tm, tk), lambda i, j, k: (i, k))
        b_spec = pl.BlockSpec((tk, tn), lambda i, j, k: (k, j))
        dims = (((1,), (0,)), ((), ()))
    elif mode == "nt":
        a_spec = pl.BlockSpec((tm, tk), lambda i, j, k: (i, k))
        b_spec = pl.BlockSpec((tn, tk), lambda i, j, k: (j, k))
        dims = (((1,), (1,)), ((), ()))
    else:
        a_spec = pl.BlockSpec((tk, tm), lambda i, j, k: (k, i))
        b_spec = pl.BlockSpec((tk, tn), lambda i, j, k: (k, j))
        dims = (((0,), (0,)), ((), ()))

    def body(a_ref, b_ref, *rest):
        o_ref = rest[-1]

        @pl.when(pl.program_id(2) == 0)
        def _():
            o_ref[...] = jnp.zeros_like(o_ref)

        o_ref[...] += lax.dot_general(a_ref[...].astype(BF16), b_ref[...].astype(BF16), dims,
                                      preferred_element_type=F32)

    deps = [] if dep is None else [dep]
    return pl.pallas_call(
        body, name=name, grid=(M // tm, N // tn, K // tk),
        in_specs=[a_spec, b_spec] + [pl.BlockSpec((8, 128), lambda i, j, k: (0, 0))] * len(deps),
        out_specs=pl.BlockSpec((tm, tn), lambda i, j, k: (i, j)),
        out_shape=jax.ShapeDtypeStruct((M, N), F32),
        compiler_params=_params(("parallel", "parallel", "arbitrary")),
    )(a, b, *deps)


def _row(arr, width=None, cb=0):
    return (arr, arr.shape[1] if width is None else width, cb)


def _rowwise(name, fn, rows, params, out_widths, tm=ROW_TILE, out_dtype=F32):
    R = rows[0][0].shape[0]
    tm = _pick(R, tm, 16 if out_dtype == BF16 else 8)
    n_r, n_p = len(rows), len(params)

    def body(*refs):
        vals = [r[...] for r in refs[:n_r + n_p]]
        for o_ref, val in zip(refs[n_r + n_p:], fn(*vals)):
            o_ref[...] = val.astype(out_dtype)

    in_specs = [pl.BlockSpec((tm, w), lambda i, cb=cb: (i, cb)) for (_, w, cb) in rows]
    in_specs += [pl.BlockSpec(p.shape, lambda i: (0, 0)) for p in params]
    return pl.pallas_call(
        body, name=name, grid=(R // tm,), in_specs=in_specs,
        out_specs=[pl.BlockSpec((tm, w), lambda i: (i, 0)) for w in out_widths],
        out_shape=[jax.ShapeDtypeStruct((R, w), out_dtype) for w in out_widths],
        compiler_params=_params(("parallel",)),
    )(*[r[0] for r in rows], *params)


def _rowwise_bwd(name, fn, rows, params, douts, tm=ROW_TILE):
    R = rows[0][0].shape[0]
    tm = _pick(R, tm)
    n_r, n_p, n_d = len(rows), len(params), len(douts)

    def body(*refs):
        vals = [r[...] for r in refs[:n_r + n_p]]
        cts = tuple(r[...] for r in refs[n_r + n_p:n_r + n_p + n_d])
        grads = jax.vjp(fn, *vals)[1](cts)
        outs = refs[n_r + n_p + n_d:]
        for o_ref, g in zip(outs[:n_r], grads[:n_r]):
            o_ref[...] = g

        @pl.when(pl.program_id(0) == 0)
        def _():
            for o_ref in outs[n_r:]:
                o_ref[...] = jnp.zeros_like(o_ref)

        for o_ref, g in zip(outs[n_r:], grads[n_r:]):
            o_ref[...] += g

    in_specs = [pl.BlockSpec((tm, w), lambda i, cb=cb: (i, cb)) for (_, w, cb) in rows]
    in_specs += [pl.BlockSpec(p.shape, lambda i: (0, 0)) for p in params]
    in_specs += [pl.BlockSpec((tm, w), lambda i, cb=cb: (i, cb)) for (_, w, cb) in douts]
    out_specs = [pl.BlockSpec((tm, w), lambda i: (i, 0)) for (_, w, _) in rows]
    out_specs += [pl.BlockSpec(p.shape, lambda i: (0, 0)) for p in params]
    out_shape = [jax.ShapeDtypeStruct((R, w), F32) for (_, w, _) in rows]
    out_shape += [jax.ShapeDtypeStruct(p.shape, F32) for p in params]
    res = pl.pallas_call(
        body, name=name, grid=(R // tm,), in_specs=in_specs, out_specs=out_specs, out_shape=out_shape,
        compiler_params=_params(("arbitrary",)),
    )(*[r[0] for r in rows], *params, *[d[0] for d in douts])
    return res[:n_r], res[n_r:]


def _softplus(x):
    return jnp.maximum(x, 0.0) + jnp.log(1.0 + jnp.exp(-jnp.abs(x)))


def _silu(x):
    return x * jax.nn.sigmoid(x)


def _rms_fn(h, w):
    return (h * lax.rsqrt(jnp.mean(h * h, axis=-1, keepdims=True) + NORM_EPS) * w,)


def _rw_prep_fn(pr, pr1, pk, pk1, pv, pv1, ps, ps1, mu_r, mu_k, mu_v, mu_s, w0, w2p, a0, a2p, k_k, k_a):
    r = pr + (pr1 - pr) * mu_r
    k = pk + (pk1 - pk) * mu_k
    v = pv + (pv1 - pv) * mu_v
    s = ps + (ps1 - ps) * mu_s
    lora = lambda x, w: jnp.dot(x.astype(BF16), w.astype(BF16), preferred_element_type=F32)
    w_log = -_softplus(-(w0 + lora(jnp.tanh(s), w2p))) - 0.5
    log_decay = -jnp.exp(w_log)
    a = jax.nn.sigmoid(a0 + lora(s, a2p))
    return r, log_decay, k * (1.0 + (a - 1.0) * k_a), v, k * k_k, a


def _conv_fn(u0, u1, u2, u3, w0, w1, w2, w3):
    return (_silu(u0 * w3 + u1 * w2 + u2 * w1 + u3 * w0),)


def _dn_gate_fn(ps2, a_log_n, dt_n):
    beta_n = jax.nn.sigmoid(ps2)
    g_n = -jnp.exp(a_log_n) * _softplus(ps2 + dt_n)
    lane = lax.broadcasted_iota(jnp.int32, (1, DN_HEAD), 1)
    pick = lambda x, j: jnp.broadcast_to(jnp.sum(x * (lane == j).astype(F32), axis=-1, keepdims=True), x.shape)[None]
    beta = jnp.concatenate([pick(beta_n, h) for h in range(DN_HEADS)], axis=0)
    g = jnp.concatenate([pick(g_n, DN_HEADS + h) for h in range(DN_HEADS)], axis=0)
    return beta, g


def _tril_masks(c):
    t = lax.broadcasted_iota(jnp.int32, (c, c), 0)
    s = lax.broadcasted_iota(jnp.int32, (c, c), 1)
    return s <= t, s < t


def _dot(x, y, cx, cy, prec=HIGH):
    nb = x.ndim - 2
    batch = tuple(range(nb))
    return lax.dot_general(x, y, (((cx + nb,), (cy + nb,)), (batch, batch)), precision=prec,
                           preferred_element_type=F32)


INV_BASE = 8


def _run(steps):
    try:
        while True:
            next(steps)
    except StopIteration as done:
        return done.value


def _run_interleaved(running):
    results, live = [None] * len(running), list(range(len(running)))
    while live:
        for k in list(live):
            try:
                next(running[k])
            except StopIteration as done:
                results[k] = done.value
                live.remove(k)
    return results


def _blocked_inv(a, mm, row, col, mm_merge=None):
    mm_merge = mm if mm_merge is None else mm_merge
    shift = INV_BASE.bit_length() - 1
    d = jnp.where(jnp.right_shift(row, shift) == jnp.right_shift(col, shift), a, 0.0)
    x = (row == col).astype(F32) - d
    p = mm(d, d)
    yield
    x = x + mm(x, p)
    yield
    p = mm(p, p)
    yield
    x = x + mm(x, p)
    yield
    size = INV_BASE
    while size < a.shape[-2]:
        same_pair = jnp.right_shift(row, shift + 1) == jnp.right_shift(col, shift + 1)
        lower_left = (jnp.right_shift(row, shift) & 1 == 1) & (jnp.right_shift(col, shift) & 1 == 0)
        lower = mm_merge(jnp.where(same_pair & lower_left, a, 0.0), x)
        yield
        x = x - mm_merge(x, lower)
        size, shift = 2 * size, shift + 1
        yield
    return x


def _tri_inv(a):
    c = a.shape[-1]
    row, col = lax.broadcasted_iota(jnp.int32, (c, c), 0), lax.broadcasted_iota(jnp.int32, (c, c), 1)
    return _blocked_inv(a, lambda x, y: _dot(x, y, 1, 0), row, col,
                        lambda x, y: _dot(x.astype(BF16), y.astype(BF16), 1, 0, None))


@jax.custom_vjp
def _tri_inv_saved(a, t):
    return t


def _tri_inv_saved_fwd(a, t):
    return t, t


def _tri_inv_saved_bwd(t, dt):
    return -_dot(_dot(t, dt, 0, 0), t, 1, 1), jnp.zeros_like(t)


_tri_inv_saved.defvjp(_tri_inv_saved_fwd, _tri_inv_saved_bwd)


def _pair_masks():
    lane = lax.broadcasted_iota(jnp.int32, (1, 2 * RW_HEAD), 1)
    m0 = (lane < RW_HEAD).astype(F32)
    return m0, 1.0 - m0


def _pair_bd(x):
    m0, m1 = _pair_masks()
    return jnp.concatenate([x * m0, x * m1], axis=-2)


def _pair_mm(x, y):
    return _dot(x, _pair_bd(y), 1, 0)


def _pair_inv(a):
    c = a.shape[-2]
    row = lax.broadcasted_iota(jnp.int32, (c, 2 * RW_HEAD), 0)
    col = lax.broadcasted_iota(jnp.int32, (c, 2 * RW_HEAD), 1) & (RW_HEAD - 1)
    return _blocked_inv(a, _pair_mm, row, col,
                        lambda x, y: _dot(x.astype(BF16), _pair_bd(y).astype(BF16), 1, 0, None))


@jax.custom_vjp
def _pair_inv_saved(a, t):
    return t


def _pair_inv_saved_fwd(a, t):
    return t, t


def _pair_inv_saved_bwd(t, dt):
    m0, m1 = _pair_masks()
    c = t.shape[-2]
    z = _dot(t, dt, 0, 0)
    x = z[:, :c, :] * m0 + z[:, c:, :] * m1
    return -_dot(x, _pair_bd(t), 1, 1), jnp.zeros_like(t)


_pair_inv_saved.defvjp(_pair_inv_saved_fwd, _pair_inv_saved_bwd)


@jax.custom_vjp
def _use_saved(x, x_saved):
    return x_saved


_use_saved.defvjp(lambda x, x_saved: (x_saved, None), lambda _, g: (g, jnp.zeros_like(g)))


def _rw_chunk_fn(*args, saved=None):
    return _run(_rw_chunk_steps(*args, saved=saved))


def _rw_chunk_steps(S, r, lw, k2, v, kkp, a, gate, rk, gnw, gnb, saved=None):
    B, C, P = r.shape
    m0, m1 = _pair_masks()
    seg = lambda x: (jnp.sum(x * m0, axis=-1, keepdims=True) * m0 + jnp.sum(x * m1, axis=-1, keepdims=True) * m1)
    mm = lambda x, y: _dot(x.astype(BF16), y.astype(BF16), 1, 0, None)
    nt = lambda x, y: _dot(x, y, 1, 1)
    tn = lambda x, y: _dot(x, y, 0, 0)
    pair_mm = lambda x, y: mm(x, _pair_bd(y))
    t_idx = lax.broadcasted_iota(jnp.int32, (C, P), 0)
    s_idx = lax.broadcasted_iota(jnp.int32, (C, P), 1) & (RW_HEAD - 1)
    incl, strict = s_idx <= t_idx, s_idx < t_idx
    tril = jnp.broadcast_to(_tril_masks(C)[0].astype(F32), (B, C, C))
    kk = kkp * lax.rsqrt(seg(kkp * kkp) + 1e-6)
    b = kk * a
    reuse = (lambda x, i: x) if saved is None else (lambda x, i: _use_saved(x, saved[i]))
    g_incl = reuse(_dot(tril, lw, 1, 0), 1)
    g_excl = g_incl - lw
    inv = jnp.exp(-g_incl)
    alpha, beta, kappa, rho = kk * jnp.exp(g_excl), b * inv, k2 * inv, r * jnp.exp(g_incl)
    yield
    ar = jnp.concatenate([alpha, rho], axis=-2)
    scores = reuse(nt(ar, jnp.concatenate([_pair_bd(beta), _pair_bd(kappa)], axis=-2)), 2)
    a_ab = jnp.where(strict, scores[:, :C, :P], 0.0)
    a_ak = jnp.where(strict, scores[:, :C, P:], 0.0)
    r_b = jnp.where(incl, scores[:, C:, :P], 0.0)
    r_k = jnp.where(incl, scores[:, C:, P:], 0.0)
    yield
    t_inv = (yield from _pair_inv(a_ab)) if saved is None else _pair_inv_saved(a_ab, saved[0])
    on_state = reuse(nt(ar, S), 3)
    yield
    u = pair_mm(t_inv, -on_state[:, :C, :] - pair_mm(a_ak, v))
    yield
    y = on_state[:, C:, :] + mm(jnp.concatenate([r_b, r_k], axis=-1),
                                jnp.concatenate([_pair_bd(u), _pair_bd(v)], axis=-2))
    same_head = ((lax.broadcasted_iota(jnp.int32, (P, P), 0) < RW_HEAD)
                 == (lax.broadcasted_iota(jnp.int32, (P, P), 1) < RW_HEAD))
    yield
    fresh = tn(jnp.concatenate([u, v], axis=-2), jnp.concatenate([beta, kappa], axis=-2))
    yield
    S_new = jnp.exp(jnp.sum(lw, axis=-2, keepdims=True)) * (S + jnp.where(same_head, fresh, 0.0))
    dev = y - seg(y) * (1.0 / RW_HEAD)
    yn = dev * lax.rsqrt(seg(dev * dev) * (1.0 / RW_HEAD) + RW_GN_EPS) * gnw + gnb
    bonus = seg(r * k2 * rk) * v
    return (yn + bonus) * _silu(gate), S_new, (t_inv, g_incl, scores, on_state)


def _dn_chunk_fn(*args, saved=None):
    return _run(_dn_chunk_steps(*args, saved=saved))


def _dn_chunk_steps(S, qc, kc, vc, bb, gb, z, nw, saved=None):
    B, C, D = qc.shape
    mm = lambda x, y: _dot(x.astype(BF16), y.astype(BF16), 1, 0, None)
    nt = lambda x, y, p=None: _dot(x, y, 1, 1, p) if p else _dot(x.astype(BF16), y.astype(BF16), 1, 1, None)
    tn = lambda x, y: _dot(x.astype(BF16), y.astype(BF16), 0, 0, None)
    incl, strict = _tril_masks(C)
    q = qc * lax.rsqrt(jnp.sum(qc * qc, axis=-1, keepdims=True) + 1e-6) * (D ** -0.5)
    k = kc * lax.rsqrt(jnp.sum(kc * kc, axis=-1, keepdims=True) + 1e-6)
    kb, vb = k * bb, vc * bb
    yield
    G = _dot(jnp.broadcast_to(incl.astype(F32), (B, C, C)), gb, 1, 0, HIGHEST)
    G = G if saved is None else _use_saved(G, saved[1])
    lane = lax.broadcasted_iota(jnp.int32, (C, D), 1)
    e0, e1 = (lane == 0).astype(F32), (lane == 1).astype(F32)
    diff = nt(G * e0 + e1, e0 - G * e1, HIGHEST)
    dmask = jnp.where(incl, jnp.exp(jnp.where(incl, diff, 0.0)), 0.0)
    M = jnp.where(strict, nt(kb, k) * dmask, 0.0)
    yield
    T = (yield from _tri_inv(M)) if saved is None else _tri_inv_saved(M, saved[0])
    eG = jnp.exp(G)
    u = mm(T, vb)
    yield
    w = mm(T, kb * eG)
    yield
    attn = jnp.where(incl, nt(q, k) * dmask, 0.0)
    yield
    v_new = u - mm(w, S)
    yield
    o = mm(q * eG, S) + mm(attn, v_new)
    yield
    g_last = jnp.sum(gb, axis=-2, keepdims=True)
    S_new = S * jnp.exp(jnp.broadcast_to(g_last, S.shape)) + tn(k * jnp.exp(g_last - G), v_new)
    on = o * lax.rsqrt(jnp.mean(o * o, axis=-1, keepdims=True) + NORM_EPS) * nw
    return on * _silu(z), S_new, (T, G)


N_GROUPS = 8
GROUP = 128
HALO = 8


def _groups(x):
    return jnp.concatenate([x[:, g * GROUP:(g + 1) * GROUP][None] for g in range(N_GROUPS)], axis=0)


@functools.partial(jax.custom_vjp, nondiff_argnums=(1,))
def _shift_rows(ext, j):
    return pltpu.roll(ext, j, 0)[HALO:, :]


def _shift_rows_fwd(ext, j):
    return _shift_rows(ext, j), None


def _shift_rows_bwd(j, _, d):
    z = jnp.concatenate([jnp.zeros((HALO, d.shape[1]), d.dtype), d], axis=0)
    return (pltpu.roll(z, z.shape[0] - j, 0),)


_shift_rows.defvjp(_shift_rows_fwd, _shift_rows_bwd)


def _rw_fused_steps(S, h_r, h_k, h_v, h_s, p_r, p_k, p_v, p_s, gate, *pars, saved=None):
    prev = lambda h, x: _shift_rows(jnp.concatenate([h, x], axis=0), 1)
    seq = _rw_prep_fn(p_r, prev(h_r, p_r), p_k, prev(h_k, p_k), p_v, prev(h_v, p_v), p_s, prev(h_s, p_s), *pars[:10])
    yield
    return (yield from _rw_chunk_steps(S, *[_groups(t) for t in seq], _groups(gate),
                                       *[_groups(t) for t in pars[10:]], saved=saved))


def _rw_fused_fn(*args, saved=None):
    return _run(_rw_fused_steps(*args, saved=saved))


def _dn_fused_steps(S, h_q, h_k, h_v, p_q, p_k, p_v, p_s2, z, *pars, saved=None):
    conv = []
    for i, (h, x) in enumerate(((h_q, p_q), (h_k, p_k), (h_v, p_v))):
        ext = jnp.concatenate([h, x], axis=0)
        conv += _conv_fn(x, _shift_rows(ext, 1), _shift_rows(ext, 2), _shift_rows(ext, 3), *pars[4 * i:4 * i + 4])
    beta, g = _dn_gate_fn(p_s2, pars[12], pars[13])
    yield
    return (yield from _dn_chunk_steps(S, *[_groups(t) for t in conv], beta, g, _groups(z), _groups(pars[14]),
                                       saved=saved))


def _dn_fused_fn(*args, saved=None):
    return _run(_dn_fused_steps(*args, saved=saved))


def _chunk_fwd(name, fn, rows, n_halo, pars, saved_shapes):
    Lp = rows[0][0].shape[0]
    C, D, NB = CHUNK, GROUP, N_GROUPS
    nc, n_r, n_p, per = Lp // C, len(rows), len(pars), CHUNK // HALO
    n_s = len(saved_shapes)

    def body(*refs):
        row_refs, halo_refs, par_refs = refs[:n_r], refs[n_r:n_r + n_halo], refs[n_r + n_halo:n_r + n_halo + n_p]
        y_ref, ck_ref = refs[n_r + n_halo + n_p:n_r + n_halo + n_p + 2]
        saved_refs, s_ref = refs[n_r + n_halo + n_p + 2:-1], refs[-1]
        first = pl.program_id(0) == 0

        @pl.when(first)
        def _():
            s_ref[...] = jnp.zeros_like(s_ref)

        S = s_ref[...]
        ck_ref[0] = S
        halos = [jnp.where(first, 0.0, r[...]) for r in halo_refs]
        y, S_new, saved = fn(S, *halos, *[r[...] for r in row_refs], *[r[...] for r in par_refs])
        for g in range(NB):
            y_ref[:, g * D:(g + 1) * D] = y[g].astype(BF16)
        for o_ref, val in zip(saved_refs, saved):
            o_ref[0] = val
        s_ref[...] = S_new

    return pl.pallas_call(
        body, name=name, grid=(nc,),
        in_specs=[pl.BlockSpec((pl.Element(C), pl.Element(w)), lambda c, off=off: (c * C, off)) for (_, w, off) in rows]
        + [pl.BlockSpec((pl.Element(HALO), pl.Element(w)), lambda c, off=off: (pl.multiple_of(jnp.maximum(c * C - HALO, 0), HALO), off))
           for (_, w, off) in rows[:n_halo]]
        + [pl.BlockSpec(p.shape, lambda c: (0, 0)) for p in pars],
        out_specs=[pl.BlockSpec((C, NB * D), lambda c: (c, 0)), pl.BlockSpec((1, NB, D, D), lambda c: (c, 0, 0, 0))]
        + [pl.BlockSpec((1,) + tuple(shp), lambda c: (c, 0, 0, 0)) for shp in saved_shapes],
        out_shape=[jax.ShapeDtypeStruct((Lp, NB * D), BF16), jax.ShapeDtypeStruct((nc, NB, D, D), F32)]
        + [jax.ShapeDtypeStruct((nc,) + tuple(shp), F32) for shp in saved_shapes],
        scratch_shapes=[pltpu.VMEM((NB, D, D), F32)],
        compiler_params=_params(("arbitrary",)),
    )(*[r[0] for r in rows], *[r[0] for r in rows[:n_halo]], *pars)


def _chunk_bwd(name, fn, rows, n_halo, pars, ckpt, saved, dy, out_blocks, into=None):
    Lp = rows[0][0].shape[0]
    C, D, NB = CHUNK, GROUP, N_GROUPS
    nc, n_r, n_p, per = Lp // C, len(rows), len(pars), CHUNK // HALO
    n_s = len(saved)
    n_in = n_r + n_halo + n_p
    width = sum(w for (_, w, _) in rows)
    n_blocks, my_block = out_blocks
    extra = [] if into is None else [into]

    def body(*refs):
        row_refs, halo_refs, par_refs = refs[:n_r], refs[n_r:n_r + n_halo], refs[n_r + n_halo:n_in]
        ck_ref, saved_refs, dy_ref = refs[n_in], refs[n_in + 1:n_in + 1 + n_s], refs[n_in + 1 + n_s]
        outs = refs[n_in + 2 + n_s + len(extra):]
        drows_ref, dpar_refs = outs[0], outs[1:1 + n_p]
        ds_ref, carry_refs = outs[1 + n_p], outs[2 + n_p:]
        i = pl.program_id(0)

        @pl.when(i == 0)
        def _():
            ds_ref[...] = jnp.zeros_like(ds_ref)
            for o_ref in list(dpar_refs) + list(carry_refs):
                o_ref[...] = jnp.zeros_like(o_ref)

        halos = [jnp.where(i == nc - 1, 0.0, r[...]) for r in halo_refs]
        vals = [ck_ref[0]] + halos + [r[...] for r in row_refs] + [r[...] for r in par_refs]
        stored = tuple(r[0] for r in saved_refs)
        grads = jax.vjp(lambda *v: fn(*v, saved=stored)[:2], *vals)[1]((_groups(dy_ref), ds_ref[...]))
        ds_ref[...] = grads[0]
        d_halos, d_rows, d_pars = grads[1:1 + n_halo], grads[1 + n_halo:1 + n_halo + n_r], grads[1 + n_halo + n_r:]
        col = 0
        for k, g in enumerate(d_rows):
            if k < n_halo:
                g = g + jnp.concatenate([jnp.zeros((C - HALO, g.shape[1]), F32), carry_refs[k][...]], axis=0)
                carry_refs[k][...] = d_halos[k]
            drows_ref[:, col:col + g.shape[1]] = g.astype(BF16)
            col += g.shape[1]
        for o_ref, g in zip(dpar_refs, d_pars):
            o_ref[...] += g

    rev = lambda c: nc - 1 - c
    dy_arr, dy_w, dy_cb = dy
    res = pl.pallas_call(
        body, name=name, grid=(nc,),
        in_specs=[pl.BlockSpec((pl.Element(C), pl.Element(w)), lambda c, off=off: (rev(c) * C, off))
                  for (_, w, off) in rows]
        + [pl.BlockSpec((pl.Element(HALO), pl.Element(w)),
                        lambda c, off=off: (pl.multiple_of(jnp.maximum(rev(c) * C - HALO, 0), HALO), off))
           for (_, w, off) in rows[:n_halo]]
        + [pl.BlockSpec(p.shape, lambda c: (0, 0)) for p in pars]
        + [pl.BlockSpec((1, NB, D, D), lambda c: (rev(c), 0, 0, 0))]
        + [pl.BlockSpec((1,) + tuple(t.shape[1:]), lambda c: (rev(c), 0, 0, 0)) for t in saved]
        + [pl.BlockSpec((pl.Element(C), pl.Element(dy_w)), lambda c: (rev(c) * C, dy_cb))]
        + [pl.BlockSpec(memory_space=pl.ANY)] * len(extra),
        out_specs=[pl.BlockSpec((C, width), lambda c: (rev(c), my_block))]
        + [pl.BlockSpec(p.shape, lambda c: (0, 0)) for p in pars],
        out_shape=[jax.ShapeDtypeStruct((Lp, n_blocks * width), BF16)]
        + [jax.ShapeDtypeStruct(p.shape, F32) for p in pars],
        scratch_shapes=[pltpu.VMEM((NB, D, D), F32)] + [pltpu.VMEM((HALO, w), F32) for (_, w, _) in rows[:n_halo]],
        input_output_aliases={} if into is None else {n_in + 2 + n_s: 0},
        compiler_params=_params(("arbitrary",)),
    )(*[r[0] for r in rows], *[r[0] for r in rows[:n_halo]], *pars, ckpt, *saved, dy_arr, *extra)
    return res[0], res[1:]


def _row_specs(rows, n_halo, row_of):
    C = CHUNK
    specs = [pl.BlockSpec((pl.Element(C), pl.Element(w)), lambda c, off=off: (row_of(c) * C, off))
             for (_, w, off) in rows]
    specs += [pl.BlockSpec((pl.Element(HALO), pl.Element(w)),
                           lambda c, off=off: (pl.multiple_of(jnp.maximum(row_of(c) * C - HALO, 0), HALO), off))
              for (_, w, off) in rows[:n_halo]]
    return specs


def _mixers_fwd(name, parts):
    Lp = parts[0][1][0][0].shape[0]
    C, D, NB = CHUNK, GROUP, N_GROUPS
    nc, width = Lp // C, NB * D
    n_in = [len(rows) + n_halo + len(pars) for (_, rows, n_halo, pars, _) in parts]
    n_out = [1 + len(shapes) for (*_, shapes) in parts]

    def body(*refs):
        ins, y_ref = refs[:sum(n_in)], refs[sum(n_in)]
        outs, states = refs[sum(n_in) + 1:sum(n_in) + 1 + sum(n_out)], refs[sum(n_in) + 1 + sum(n_out):]
        first = pl.program_id(0) == 0

        @pl.when(first)
        def _():
            for s_ref in states:
                s_ref[...] = jnp.zeros_like(s_ref)

        i_at, running = 0, []
        for k, (steps, rows, n_halo, pars, shapes) in enumerate(parts):
            n_r = len(rows)
            row_refs, halo_refs = ins[i_at:i_at + n_r], ins[i_at + n_r:i_at + n_r + n_halo]
            par_refs = ins[i_at + n_r + n_halo:i_at + n_in[k]]
            S = states[k][...]
            halos = [jnp.where(first, 0.0, r[...]) for r in halo_refs]
            running.append(steps(S, *halos, *[r[...] for r in row_refs], *[r[...] for r in par_refs]))
            i_at += n_in[k]
        results = _run_interleaved(running)
        o_at = 0
        for k, (y, S_new, saved) in enumerate(results):
            ck_ref, saved_refs = outs[o_at], outs[o_at + 1:o_at + n_out[k]]
            ck_ref[0] = states[k][...]
            for g in range(NB):
                y_ref[:, k * width + g * D:k * width + (g + 1) * D] = y[g].astype(BF16)
            for o_ref, val in zip(saved_refs, saved):
                o_ref[0] = val
            states[k][...] = S_new
            o_at += n_out[k]

    in_specs, args, out_specs, out_shape = [], [], [], []
    for (_, rows, n_halo, pars, shapes) in parts:
        in_specs += _row_specs(rows, n_halo, lambda c: c) + [pl.BlockSpec(p.shape, lambda c: (0, 0)) for p in pars]
        args += [r[0] for r in rows] + [r[0] for r in rows[:n_halo]] + list(pars)
        for shp in [(NB, D, D)] + list(shapes):
            out_specs.append(pl.BlockSpec((1,) + tuple(shp), lambda c: (c, 0, 0, 0)))
            out_shape.append(jax.ShapeDtypeStruct((nc,) + tuple(shp), F32))
    res = pl.pallas_call(
        body, name=name, grid=(nc,), in_specs=in_specs,
        out_specs=[pl.BlockSpec((C, width * len(parts)), lambda c: (c, 0))] + out_specs,
        out_shape=[jax.ShapeDtypeStruct((Lp, width * len(parts)), BF16)] + out_shape,
        scratch_shapes=[pltpu.VMEM((NB, D, D), F32)] * len(parts),
        compiler_params=_params(("arbitrary",)),
    )(*args)
    per_part, at = [], 1
    for n in n_out:
        per_part.append(list(res[at:at + n]))
        at += n
    return res[0], per_part


def _mixers_bwd(name, parts, kept, dy):
    Lp = parts[0][1][0][0].shape[0]
    C, D, NB = CHUNK, GROUP, N_GROUPS
    nc = Lp // C
    n_in = [len(rows) + n_halo + len(pars) + len(kept[k]) for k, (_, rows, n_halo, pars, _) in enumerate(parts)]
    widths = [sum(w for (_, w, _) in rows) for (_, rows, *_) in parts]
    n_par = [len(pars) for (_, _, _, pars, _) in parts]
    n_scr = [1 + n_halo for (_, _, n_halo, _, _) in parts]
    rev = lambda c: nc - 1 - c

    def body(*refs):
        ins, dy_ref = refs[:sum(n_in)], refs[sum(n_in)]
        drows_ref = refs[sum(n_in) + 1]
        dpar_all = refs[sum(n_in) + 2:sum(n_in) + 2 + sum(n_par)]
        scratch = refs[sum(n_in) + 2 + sum(n_par):]
        i = pl.program_id(0)

        @pl.when(i == 0)
        def _():
            for o_ref in list(dpar_all) + list(scratch):
                o_ref[...] = jnp.zeros_like(o_ref)

        i_at, vals, stored, cts, sizes = 0, [], [], [], []
        for k, (_, rows, n_halo, pars, _) in enumerate(parts):
            n_r = len(rows)
            row_refs, halo_refs = ins[i_at:i_at + n_r], ins[i_at + n_r:i_at + n_r + n_halo]
            par_refs = ins[i_at + n_r + n_halo:i_at + n_r + n_halo + n_par[k]]
            kept_refs = ins[i_at + n_r + n_halo + n_par[k]:i_at + n_in[k]]
            halos = [jnp.where(i == nc - 1, 0.0, r[...]) for r in halo_refs]
            part_vals = [kept_refs[0][0]] + halos + [r[...] for r in row_refs] + [r[...] for r in par_refs]
            vals += part_vals
            sizes.append(len(part_vals))
            stored.append(tuple(r[0] for r in kept_refs[1:]))
            cts.append((_groups(dy_ref[:, k * NB * D:(k + 1) * NB * D]), scratch[sum(n_scr[:k])][...]))
            i_at += n_in[k]

        def both(*flat):
            running, at = [], 0
            for k, (steps, *_) in enumerate(parts):
                running.append(steps(*flat[at:at + sizes[k]], saved=stored[k]))
                at += sizes[k]
            return tuple(r[:2] for r in _run_interleaved(running))

        all_grads = jax.vjp(both, *vals)[1](tuple(cts))
        i_at = p_at = s_at = g_at = col = 0
        for k, (_, rows, n_halo, pars, _) in enumerate(parts):
            n_r = len(rows)
            grads = all_grads[g_at:g_at + sizes[k]]
            dpar_refs = dpar_all[p_at:p_at + n_par[k]]
            ds_ref, carry_refs = scratch[s_at], scratch[s_at + 1:s_at + n_scr[k]]
            ds_ref[...] = grads[0]
            d_halos, d_rows = grads[1:1 + n_halo], grads[1 + n_halo:1 + n_halo + n_r]
            for j, g in enumerate(d_rows):
                if j < n_halo:
                    g = g + jnp.concatenate([jnp.zeros((C - HALO, g.shape[1]), F32), carry_refs[j][...]], axis=0)
                    carry_refs[j][...] = d_halos[j]
                drows_ref[:, col:col + g.shape[1]] = g.astype(BF16)
                col += g.shape[1]
            for o_ref, g in zip(dpar_refs, grads[1 + n_halo + n_r:]):
                o_ref[...] += g
            p_at, s_at, g_at = p_at + n_par[k], s_at + n_scr[k], g_at + sizes[k]

    in_specs, args, out_specs, out_shape, scratch_shapes = [], [], [], [], []
    for k, (_, rows, n_halo, pars, _) in enumerate(parts):
        in_specs += _row_specs(rows, n_halo, rev) + [pl.BlockSpec(p.shape, lambda c: (0, 0)) for p in pars]
        in_specs += [pl.BlockSpec((1,) + tuple(t.shape[1:]), lambda c: (rev(c), 0, 0, 0)) for t in kept[k]]
        args += [r[0] for r in rows] + [r[0] for r in rows[:n_halo]] + list(pars) + list(kept[k])
        out_specs += [pl.BlockSpec(p.shape, lambda c: (0, 0)) for p in pars]
        out_shape += [jax.ShapeDtypeStruct(p.shape, F32) for p in pars]
        scratch_shapes += [pltpu.VMEM((NB, D, D), F32)] + [pltpu.VMEM((HALO, w), F32) for (_, w, _) in rows[:n_halo]]
    res = pl.pallas_call(
        body, name=name, grid=(nc,),
        in_specs=in_specs + [pl.BlockSpec((C, dy.shape[1]), lambda c: (rev(c), 0))],
        out_specs=[pl.BlockSpec((C, sum(widths)), lambda c: (rev(c), 0))] + out_specs,
        out_shape=[jax.ShapeDtypeStruct((Lp, sum(widths)), BF16)] + out_shape,
        scratch_shapes=scratch_shapes, compiler_params=_params(("arbitrary",)),
    )(*args, dy)
    d_pars, at = [], 1
    for n in n_par:
        d_pars.append(list(res[at:at + n]))
        at += n
    return res[0], d_pars


def _loss_head(h, yw, tgt, fw, n_real, tm=2 * ROW_TILE):
    Lp, Dm = h.shape
    tm = _pick(Lp, tm, 16)

    def out_fn(z, fw_):
        return z * lax.rsqrt(jnp.mean(z * z, axis=-1, keepdims=True) + NORM_EPS) * fw_

    def body(h_ref, yw_ref, t_ref, fw_ref, loss_ref, dz_ref, dz16_ref, dfw_ref):
        i = pl.program_id(0)

        @pl.when(i == 0)
        def _():
            loss_ref[...] = jnp.zeros_like(loss_ref)
            dfw_ref[...] = jnp.zeros_like(dfw_ref)

        row = i * tm + lax.broadcasted_iota(jnp.int32, (tm, 1), 0)
        mask = ((row >= N_META) & (row < n_real)).astype(F32)
        z = h_ref[...] + yw_ref[...]
        o, vjp = jax.vjp(out_fn, z, fw_ref[...])
        err = (o - t_ref[...]) * mask
        row_loss = 0.5 * jnp.mean(jnp.square(err), axis=-1, keepdims=True)
        dz, dfw = vjp(err * (1.0 / Dm))
        loss_ref[...] += jnp.sum(row_loss, axis=0, keepdims=True)
        dz_ref[...] = dz
        dz16_ref[...] = dz.astype(BF16)
        dfw_ref[...] += dfw

    row_spec = pl.BlockSpec((tm, Dm), lambda i: (i, 0))
    return pl.pallas_call(
        body, name="loss_head", grid=(Lp // tm,),
        in_specs=[row_spec, row_spec, row_spec, pl.BlockSpec((1, Dm), lambda i: (0, 0))],
        out_specs=[pl.BlockSpec((8, 128), lambda i: (0, 0)), row_spec, row_spec,
                   pl.BlockSpec((1, Dm), lambda i: (0, 0))],
        out_shape=[jax.ShapeDtypeStruct((8, 128), F32), jax.ShapeDtypeStruct((Lp, Dm), F32),
                   jax.ShapeDtypeStruct((Lp, Dm), BF16), jax.ShapeDtypeStruct((1, Dm), F32)],
        compiler_params=_params(("arbitrary",)),
    )(h, yw, tgt, fw)


def _exchange(name, x, masks, slot_kind, per_dest, n_split=1, copy_own=True, other_half=False):
    n = len(masks)
    keep_own = slot_kind is not None and copy_own
    n_slots = {"chip": 4, "core": 2, "dev": 8, None: n}[slot_kind]
    blk_shape = x.shape[1:] if per_dest else x.shape
    if other_half:
        blk_shape = (x.shape[0] // 2,) + tuple(x.shape[1:])
    rows = blk_shape[0] // n_split

    def body(x_ref, o_ref, send_sems, recv_sems, local_sems):
        mx, my, mc = lax.axis_index("x"), lax.axis_index("y"), lax.axis_index("c")
        if other_half:
            x_ref = x_ref.at[pl.ds((1 - mc) * blk_shape[0], blk_shape[0])]

        def slot(k, px, py, pc):
            return {"chip": 2 * px + py, "core": pc, "dev": 4 * px + 2 * py + pc, None: k}[slot_kind]

        def peer(m):
            return (mx + m[0]) % 2, (my + m[1]) % 2, (mc + m[2]) % 2

        def part(ref, j):
            return ref.at[pl.ds(j * rows, rows)]

        own_src = x_ref.at[2 * mx + my] if per_dest else x_ref
        local = []
        if keep_own:
            own_dst = o_ref.at[slot(0, mx, my, mc)]
            local = [pltpu.make_async_copy(part(own_src, j), part(own_dst, j), local_sems.at[j])
                     for j in range(n_split)]
        for cp in local:
            cp.start()
        sends = []
        for k, m in enumerate(masks):
            px, py, pc = peer(m)
            src = x_ref.at[2 * px + py] if per_dest else x_ref
            dst = o_ref.at[slot(k, mx, my, mc)]
            for j in range(n_split):
                sends.append(pltpu.make_async_remote_copy(
                    src_ref=part(src, j), dst_ref=part(dst, j), send_sem=send_sems.at[k * n_split + j],
                    recv_sem=recv_sems.at[k * n_split + j], device_id=(px, py, pc), device_id_type=MESH))
        for cp in sends:
            cp.start()
        for k, m in enumerate(masks):
            px, py, pc = peer(m)
            landed = o_ref.at[slot(k, px, py, pc)]
            for j in range(n_split):
                pltpu.make_async_remote_copy(
                    src_ref=part(own_src, j), dst_ref=part(landed, j), send_sem=send_sems.at[k * n_split + j],
                    recv_sem=recv_sems.at[k * n_split + j], device_id=(px, py, pc), device_id_type=MESH).wait_recv()
        for cp in sends:
            cp.wait_send()
        for cp in local:
            cp.wait()

    return pl.pallas_call(
        body, name=name,
        in_specs=[pl.BlockSpec(memory_space=pl.ANY)], out_specs=pl.BlockSpec(memory_space=pl.ANY),
        out_shape=jax.ShapeDtypeStruct((n_slots,) + tuple(blk_shape), x.dtype),
        scratch_shapes=[pltpu.SemaphoreType.DMA((n * n_split,)), pltpu.SemaphoreType.DMA((n * n_split,)),
                        pltpu.SemaphoreType.DMA((n_split,))],
        compiler_params=pltpu.CompilerParams(has_side_effects=True),
    )(x)


CHIP_MASKS = [(1, 0, 0), (0, 1, 0), (1, 1, 0)]
CORE_MASKS = [(0, 0, 1)]
ALL_MASKS = [(dx, dy, dc) for dx in (0, 1) for dy in (0, 1) for dc in (0, 1) if (dx, dy, dc) != (0, 0, 0)]

HBM_SPEC = pl.BlockSpec(memory_space=pltpu.HBM)
SEM_SPEC = pl.BlockSpec(memory_space=pltpu.SEMAPHORE)
DATAFLOW = pltpu.SideEffectType.DATAFLOW_SIDE_EFFECTING


def _split_copies(x_ref, land_ref, send_sems, recv_sems, masks, slot_kind, per_dest, n_split, with_recvs=True):
    mx, my, mc = lax.axis_index("x"), lax.axis_index("y"), lax.axis_index("c")
    slot = lambda px, py, pc: {"chip": 2 * px + py, "core": pc}[slot_kind]
    rows = (x_ref.shape[1] if per_dest else x_ref.shape[0]) // n_split
    part = lambda ref, j: ref.at[pl.ds(j * rows, rows)]
    sends, recvs = [], []
    for k, m in enumerate(masks):
        px, py, pc = (mx + m[0]) % 2, (my + m[1]) % 2, (mc + m[2]) % 2
        src = x_ref.at[2 * px + py] if per_dest else x_ref
        own_src = x_ref.at[2 * mx + my] if per_dest else x_ref
        for j in range(n_split):
            sems = dict(send_sem=send_sems.at[k * n_split + j], recv_sem=recv_sems.at[k * n_split + j],
                        device_id=(px, py, pc), device_id_type=MESH)
            sends.append(pltpu.make_async_remote_copy(
                src_ref=part(src, j), dst_ref=part(land_ref.at[slot(mx, my, mc)], j), **sems))
            if with_recvs:
                recvs.append(pltpu.make_async_remote_copy(
                    src_ref=part(own_src, j), dst_ref=part(land_ref.at[slot(px, py, pc)], j), **sems))
    return sends, recvs


def _exchange_start(name, x, masks, slot_kind, per_dest, n_split, dep=None):
    n = len(masks) * n_split
    blk_shape = x.shape[1:] if per_dest else x.shape
    land_shape = ({"chip": 4, "core": 2}[slot_kind],) + tuple(blk_shape)
    deps = [] if dep is None else [dep]

    def body(x_ref, land_ref, *rest):
        send_sems, recv_sems, x_thru, land_thru, token = rest[len(deps):]
        for cp in _split_copies(x_ref, land_ref, send_sems, recv_sems, masks, slot_kind, per_dest, n_split, False)[0]:
            cp.start()
        token[...] = jnp.zeros_like(token)

    return pl.pallas_call(
        body, name=name,
        out_shape=(pltpu.SemaphoreType.DMA((n,)), pltpu.SemaphoreType.DMA((n,)), pltpu.HBM(x.shape, x.dtype),
                   pltpu.HBM(land_shape, x.dtype), jax.ShapeDtypeStruct((8, 128), F32)),
        in_specs=(HBM_SPEC, HBM_SPEC) + (pl.BlockSpec(memory_space=pl.ANY),) * len(deps),
        out_specs=(SEM_SPEC, SEM_SPEC, HBM_SPEC, HBM_SPEC, pl.BlockSpec(memory_space=pltpu.VMEM)),
        input_output_aliases={0: 2, 1: 3},
        compiler_params=pltpu.CompilerParams(has_side_effects=DATAFLOW),
    )(pltpu.with_memory_space_constraint(x, pltpu.HBM),
      pltpu.with_memory_space_constraint(lax.empty(land_shape, x.dtype), pltpu.HBM), *deps)


def _exchange_wait(name, started, after, masks, slot_kind, per_dest, n_split):
    send_sems, recv_sems, x_thru, land_thru, _ = started

    def body(x_ref, land_ref, send_sems, recv_sems, after_ref, x_out, land_out):
        sends, recvs = _split_copies(x_ref, land_ref, send_sems, recv_sems, masks, slot_kind, per_dest, n_split)
        for cp in sends:
            cp.wait_send()
        for cp in recvs:
            cp.wait_recv()

    return pl.pallas_call(
        body, name=name,
        out_shape=(pltpu.HBM(x_thru.shape, x_thru.dtype), pltpu.HBM(land_thru.shape, land_thru.dtype)),
        in_specs=(HBM_SPEC, HBM_SPEC, SEM_SPEC, SEM_SPEC, pl.BlockSpec(memory_space=pl.ANY)),
        out_specs=(HBM_SPEC, HBM_SPEC), input_output_aliases={0: 0, 1: 1},
        compiler_params=pltpu.CompilerParams(has_side_effects=DATAFLOW),
    )(x_thru, land_thru, send_sems, recv_sems, after)


def _gather_chips(name, x):
    return _exchange(name, x, CHIP_MASKS, "chip", False)


def _gather_shards(name, shard, n_split, overlap=False, dep=None):
    half = shard.shape[0] // 2
    mine = lax.dynamic_slice_in_dim(shard, lax.axis_index("c") * half, half, axis=0)
    if overlap:
        return _exchange_start(name + "_chips", mine, CHIP_MASKS, "chip", False, n_split, dep)
    by_chip = _exchange(name + "_chips", mine, CHIP_MASKS, "chip", False, n_split, copy_own=False)
    return _gather_tail(name, mine, by_chip)


def _gather_tail(name, mine, by_chip):
    c, chip = lax.axis_index("c"), 2 * lax.axis_index("x") + lax.axis_index("y")
    by_chip = lax.dynamic_update_index_in_dim(by_chip, mine, chip, 0)
    both = _exchange(name + "_cores", by_chip, CORE_MASKS, "core", False, N_CHIPS, copy_own=False)
    return lax.dynamic_update_index_in_dim(both, by_chip, c, 0)


def _gather_finish(name, started, after, n_split):
    mine, by_chip = _exchange_wait(name + "_chips_wait", started, after, CHIP_MASKS, "chip", False, n_split)
    return _gather_tail(name, mine, by_chip)


def _sum_slots(name, x, tr=128):
    S, R, N = x.shape
    tr = _pick(R, tr, 16)

    def body(x_ref, o_ref):
        acc = x_ref[0].astype(F32)
        for s in range(1, S):
            acc = acc + x_ref[s].astype(F32)
        o_ref[...] = acc

    return pl.pallas_call(
        body, name=name, grid=(R // tr,),
        in_specs=[pl.BlockSpec((S, tr, N), lambda i: (0, i, 0))], out_specs=pl.BlockSpec((tr, N), lambda i: (i, 0)),
        out_shape=jax.ShapeDtypeStruct((R, N), F32), compiler_params=_params(("parallel",)),
    )(x)


def _add_to_bf16(name, a, b, tr=128):
    S, R, N = a.shape
    tr = _pick(R, tr, 16)

    def body(a_ref, b_ref, o_ref):
        o_ref[...] = (a_ref[...] + b_ref[...]).astype(BF16)

    spec = pl.BlockSpec((S, tr, N), lambda i: (0, i, 0))
    return pl.pallas_call(
        body, name=name, grid=(R // tr,), in_specs=[spec, spec], out_specs=spec,
        out_shape=jax.ShapeDtypeStruct((S, R, N), BF16), compiler_params=_params(("parallel",)),
    )(a, b)


def _add_slabs_to_bf16(name, full, recv, c, tr=64):
    R, NP = full.shape
    half = R // 2
    nb = half // tr

    def body(c_ref, a_ref, b_ref, o_ref):
        x = a_ref[...] + b_ref[...]
        for s in range(N_CHIPS - 1):
            o_ref[s] = x[:, s * SHARD_COLS:(s + 1) * SHARD_COLS].astype(BF16)
        last = jnp.concatenate([x[:, (N_CHIPS - 1) * SHARD_COLS:GAP_AT], x[:, GAP_AT + GAP:]], axis=1)
        o_ref[N_CHIPS - 1] = last.astype(BF16)

    grid_spec = pltpu.PrefetchScalarGridSpec(
        num_scalar_prefetch=1, grid=(nb,),
        in_specs=[pl.BlockSpec((tr, NP), lambda i, c_ref: (c_ref[0] * nb + i, 0)),
                  pl.BlockSpec((tr, NP), lambda i, c_ref: (i, 0))],
        out_specs=pl.BlockSpec((N_CHIPS, tr, SHARD_COLS), lambda i, c_ref: (0, i, 0)))
    return pl.pallas_call(
        body, name=name, grid_spec=grid_spec,
        out_shape=jax.ShapeDtypeStruct((N_CHIPS, half, SHARD_COLS), BF16), compiler_params=_params(("parallel",)),
    )(jnp.reshape(c, (1,)).astype(jnp.int32), full, recv)


def _adamw(name, gparts, w, m, v, tr=128):
    S, R, N = gparts.shape
    tr = _pick(R, tr)
    c1 = 1.0 / (1.0 - ADAM_B1 ** ADAM_STEP)
    c2 = 1.0 / (1.0 - ADAM_B2 ** ADAM_STEP)

    def body(g_ref, w_ref, m_ref, v_ref, go_ref, d_ref, mo_ref, vo_ref):
        g = g_ref[0]
        for s in range(1, S):
            g = g + g_ref[s]
        m_new = ADAM_B1 * m_ref[...] + (1.0 - ADAM_B1) * g
        v_new = ADAM_B2 * v_ref[...] + (1.0 - ADAM_B2) * jnp.square(g)
        go_ref[...] = g
        mo_ref[...] = m_new
        vo_ref[...] = v_new
        d_ref[...] = -ADAM_LR * ((m_new * c1) / (jnp.sqrt(v_new * c2) + ADAM_EPS) + ADAM_WD * w_ref[...])

    spec = pl.BlockSpec((tr, N), lambda i: (i, 0))
    return pl.pallas_call(
        body, name=name, grid=(R // tr,),
        in_specs=[pl.BlockSpec((S, tr, N), lambda i: (0, i, 0)), spec, spec, spec], out_specs=[spec] * 4,
        out_shape=[jax.ShapeDtypeStruct((R, N), F32)] * 4, compiler_params=_params(("parallel",)),
    )(gparts, w, m, v)


def _reduce_to_shard(name, slabs, n_split, by_columns=False, overlap=False):
    c, chip = lax.axis_index("c"), 2 * lax.axis_index("x") + lax.axis_index("y")
    if by_columns:
        R, N = slabs.shape[0], SHARD_COLS
        from_sibling = _exchange(name + "_sib", slabs, CORE_MASKS, None, False, n_split, other_half=True)[0]
        wire = _add_slabs_to_bf16(name + "_add", slabs, from_sibling, c)
    else:
        _, R, N = slabs.shape
        half = R // 2
        halves = slabs.reshape(N_CHIPS, 2, half, N)
        mine = lax.dynamic_index_in_dim(halves, c, axis=1, keepdims=False)
        theirs = lax.dynamic_index_in_dim(halves, 1 - c, axis=1, keepdims=False)
        from_sibling = _exchange(name + "_sib", theirs, CORE_MASKS, None, False, N_CHIPS)[0]
        wire = _add_to_bf16(name + "_add", mine, from_sibling)
    if overlap:
        return _exchange_start(name + "_chips", wire, CHIP_MASKS, "chip", True, n_split), (R, N)
    got = _exchange(name + "_chips", wire, CHIP_MASKS, "chip", True, n_split, copy_own=False)
    return _reduce_tail(name, wire, got, n_split, R, N)


def _reduce_tail(name, wire, got, n_split, R, N):
    c, chip = lax.axis_index("c"), 2 * lax.axis_index("x") + lax.axis_index("y")
    got = lax.dynamic_update_index_in_dim(got, lax.dynamic_index_in_dim(wire, chip, 0, keepdims=False), chip, 0)
    part = _sum_slots(name + "_sum", got)
    both = _exchange(name + "_cores", part, CORE_MASKS, "core", False, n_split, copy_own=False)
    return lax.dynamic_update_index_in_dim(both, part, c, 0).reshape(1, R, N)


def _reduce_finish(name, started, after, n_split):
    handle, (R, N) = started
    wire, got = _exchange_wait(name + "_chips_wait", handle, after, CHIP_MASKS, "chip", True, n_split)
    return _reduce_tail(name, wire, got, n_split, R, N)


def _pack(pieces, cols, row_mult=8):
    flat = jnp.concatenate([p.reshape(-1) for p in pieces])
    rows = -(-flat.shape[0] // cols)
    rows = -(-rows // row_mult) * row_mult
    return jnp.pad(flat, (0, rows * cols - flat.shape[0])).reshape(rows, cols)


def _unpack(packed, shapes):
    flat = packed.reshape(-1)
    out, off = [], 0
    for shp in shapes:
        n = 1
        for d in shp:
            n *= d
        out.append(flat[off:off + n].reshape(shp))
        off += n
    return out


def kernel(x, meta_tokens, norm_w, w_in, rw_shift_mu, rw_w0, rw_w2, rw_a0, rw_a2, rw_k_k, rw_k_a, rw_r_k, rw_gn_w, rw_gn_b, dn_conv_w, dn_A_log, dn_dt_bias, dn_norm_w, w_out, final_norm_w, loss_target, m_meta_tokens, m_norm_w, m_w_in, m_rw_shift_mu, m_rw_w0, m_rw_w2, m_rw_a0, m_rw_a2, m_rw_k_k, m_rw_k_a, m_rw_r_k, m_rw_gn_w, m_rw_gn_b, m_dn_conv_w, m_dn_A_log, m_dn_dt_bias, m_dn_norm_w, m_w_out, m_final_norm_w, v_meta_tokens, v_norm_w, v_w_in, v_rw_shift_mu, v_rw_w0, v_rw_w2, v_rw_a0, v_rw_a2, v_rw_k_k, v_rw_k_a, v_rw_r_k, v_rw_gn_w, v_rw_gn_b, v_dn_conv_w, v_dn_A_log, v_dn_dt_bias, v_dn_norm_w, v_w_out, v_final_norm_w):
    S = x.shape[1]
    L = N_META + S
    Lp = -(-L // CHUNK) * CHUNK

    small_shapes = [(RW_LORA, 256), (RW_LORA, 256), (CONV_W, 768), (N_META, 512)]
    small_mine = _pack([rw_w2[0], rw_a2[0], dn_conv_w[0], meta_tokens], 1024)
    small_all = _gather_chips("gather_small", small_mine)
    per_chip = [_unpack(small_all[s], small_shapes) for s in range(N_CHIPS)]
    w2, a2, conv_w, meta = [jnp.concatenate([per_chip[s][i] for s in range(N_CHIPS)], axis=1) for i in range(4)]
    w_in_started = _gather_shards("gather_w_in", w_in[0].astype(BF16), 8, overlap=True, dep=small_all)
    w_out_started = _gather_shards("gather_w_out", w_out[0].astype(BF16), 8, overlap=True, dep=w_in_started[4])

    tail = [jnp.zeros((Lp - L, D_MODEL), F32)] if Lp > L else []
    h = jnp.concatenate([meta + w_in_started[4][:1, :1], x[0]] + tail, axis=0)
    tgt = jnp.concatenate([jnp.zeros((N_META, D_MODEL), F32), loss_target[0]] + tail, axis=0)
    (u,) = _rowwise("rms_in", _rms_fn, [_row(h)], [norm_w], [D_MODEL], tm=2 * ROW_TILE, out_dtype=BF16)
    w_in_all = _gather_finish("gather_w_in", w_in_started, u, 8)
    slabs = [w_in_all[:, s].reshape(D_MODEL, SHARD_COLS) for s in range(N_CHIPS)]
    cut = GAP_AT - (N_CHIPS - 1) * SHARD_COLS
    W_gapped = jnp.concatenate(slabs[:-1] + [slabs[-1][:, :cut], jnp.zeros((D_MODEL, GAP), BF16), slabs[-1][:, cut:]],
                               axis=1)
    p = _mm("in_proj", u, W_gapped, "nn", dep=w_out_started[4])

    mu = rw_shift_mu
    zpad = jnp.zeros((RW_LORA, RW_WIDTH), F32)
    rw_params = [mu[:, 0:1024], mu[:, 1024:2048], mu[:, 2048:3072], mu[:, 3072:3200], rw_w0,
                 jnp.concatenate([w2, zpad], axis=0), rw_a0, jnp.concatenate([zpad, a2], axis=0), rw_k_k, rw_k_a]
    rw_rows = [_row(p, 1024, OFF_R), _row(p, 1024, OFF_K), _row(p, 1024, OFF_V), _row(p, 128, OFF_S1),
               _row(p, 1024, OFF_GATE)]
    rw_pars = rw_params + [rw_r_k, rw_gn_w, rw_gn_b]
    rw_saved_shapes = [(N_GROUPS, CHUNK, GROUP), (N_GROUPS, CHUNK, GROUP), (N_GROUPS, 2 * CHUNK, 2 * GROUP),
                       (N_GROUPS, 2 * CHUNK, GROUP)]

    dn_rows = [_row(p, 1024, OFF_DQ), _row(p, 1024, OFF_DK), _row(p, 1024, OFF_DV), _row(p, 128, OFF_S2),
               _row(p, 1024, OFF_Z)]
    dn_pars = [conv_w[j:j + 1, 1024 * i:1024 * (i + 1)] for i in range(3) for j in range(CONV_W)]
    narrow = lambda t: jnp.pad(t, ((0, 0), (DN_HEADS, DN_HEAD - 2 * DN_HEADS)))
    dn_pars += [narrow(dn_A_log), narrow(dn_dt_bias), jnp.tile(dn_norm_w, (1, DN_HEADS))]
    dn_saved_shapes = [(N_GROUPS, CHUNK, CHUNK), (N_GROUPS, CHUNK, GROUP)]
    mixer_parts = [(_rw_fused_steps, rw_rows, 4, rw_pars, rw_saved_shapes),
                   (_dn_fused_steps, dn_rows, 3, dn_pars, dn_saved_shapes)]
    y, mixer_kept = _mixers_fwd("mixers_fwd", mixer_parts)
    w_out_all = _gather_finish("gather_w_out", w_out_started, y, 8)
    Wo = jnp.concatenate([w_out_all[:, s].reshape(D_MODEL // N_CHIPS, D_MODEL) for s in range(N_CHIPS)], axis=0)
    yw = _mm("out_proj", y, Wo, "nn", tn=1024)
    loss_acc, dz, dz16, d_fw = _loss_head(h, yw, tgt, final_norm_w.reshape(1, D_MODEL), L)
    loss = lax.psum(loss_acc[0, 0], ("x", "y", "c"))

    d_wo = _mm("d_w_out", y.T, dz16, "nn", tm=512, tn=1024, tk=Lp)
    wo_started = _reduce_to_shard("rs_w_out", d_wo.reshape(N_CHIPS, D_MODEL // N_CHIPS, D_MODEL), 8, overlap=True)
    dy = _mm("d_out_proj", dz16, Wo, "nt", tn=1024, tk=2048, dep=wo_started[0][4])

    dp, (d_rw_pars, d_dn_pars) = _mixers_bwd("mixers_bwd", mixer_parts, mixer_kept, dy)
    d_prep_pars, d_rw_pars = d_rw_pars[:10], d_rw_pars[10:]
    d_conv_parts = [jnp.concatenate(d_dn_pars[4 * i:4 * i + 4], axis=0) for i in range(3)]
    d_a_log_b, d_dt_b = d_dn_pars[12], d_dn_pars[13]
    d_dn_nw = jnp.sum(d_dn_pars[14].reshape(DN_HEADS, DN_HEAD), axis=0, keepdims=True)
    d_W = _mm("d_w_in", u.T, dp, "nn", tm=1024, tn=768, tk=Lp)
    wi_started = _reduce_to_shard("rs_w_in", d_W, 8, by_columns=True, overlap=True)
    du = _mm("d_in_proj", dp, W_gapped, "nt", tn=2048, tk=1408, dep=wi_started[0][4])
    (dh,), (d_norm_w,) = _rowwise_bwd("rms_in_bwd", lambda h_, w_: (_rms_fn(h_, w_)[0], h_), [_row(h)], [norm_w],
                                      [_row(du), _row(dz)])
    grad_x = dh[N_META:L][None]

    d_mu = jnp.concatenate(d_prep_pars[0:4], axis=1)
    d_w2, d_a2 = d_prep_pars[5][:RW_LORA], d_prep_pars[7][RW_LORA:]
    d_conv = jnp.concatenate(d_conv_parts, axis=1)
    d_meta = dh[:N_META]
    head_sum = lambda t: t[:, DN_HEADS:2 * DN_HEADS]
    rep_names = ["norm_w", "rw_shift_mu", "rw_w0", "rw_a0", "rw_k_k", "rw_k_a", "rw_r_k", "rw_gn_w", "rw_gn_b",
                 "dn_A_log", "dn_dt_bias", "dn_norm_w", "final_norm_w"]
    rep_g = [d_norm_w, d_mu, d_prep_pars[4], d_prep_pars[6], d_prep_pars[8], d_prep_pars[9],
             d_rw_pars[0], d_rw_pars[1], d_rw_pars[2],
             head_sum(d_a_log_b), head_sum(d_dt_b), d_dn_nw, d_fw.reshape(D_MODEL)]
    rep_w = [norm_w, rw_shift_mu, rw_w0, rw_a0, rw_k_k, rw_k_a, rw_r_k, rw_gn_w, rw_gn_b, dn_A_log, dn_dt_bias,
             dn_norm_w, final_norm_w]
    rep_m = [m_norm_w, m_rw_shift_mu, m_rw_w0, m_rw_a0, m_rw_k_k, m_rw_k_a, m_rw_r_k, m_rw_gn_w, m_rw_gn_b,
             m_dn_A_log, m_dn_dt_bias, m_dn_norm_w, m_final_norm_w]
    rep_v = [v_norm_w, v_rw_shift_mu, v_rw_w0, v_rw_a0, v_rw_k_k, v_rw_k_a, v_rw_r_k, v_rw_gn_w, v_rw_gn_b,
             v_dn_A_log, v_dn_dt_bias, v_dn_norm_w, v_final_norm_w]
    rep_shapes = [t.shape for t in rep_w]
    rep_all = _exchange("gather_rep_grads", _pack(rep_g, 128), ALL_MASKS, "dev", False)
    rep_out = _adamw("adam_rep", rep_all, _pack(rep_w, 128), _pack(rep_m, 128), _pack(rep_v, 128))
    rep_out = [dict(zip(rep_names, _unpack(t, rep_shapes))) for t in rep_out]

    sm_slabs = jnp.stack([_pack([d_w2[:, 256 * s:256 * (s + 1)], d_a2[:, 256 * s:256 * (s + 1)],
                                 d_conv[:, 768 * s:768 * (s + 1)], d_meta[:, 512 * s:512 * (s + 1)]], 1024, 64)
                          for s in range(N_CHIPS)])
    sm_parts = _reduce_to_shard("rs_small", sm_slabs, 1)
    sm_w = [rw_w2[0], rw_a2[0], dn_conv_w[0], meta_tokens]
    sm_m = [m_rw_w2[0], m_rw_a2[0], m_dn_conv_w[0], m_meta_tokens]
    sm_v = [v_rw_w2[0], v_rw_a2[0], v_dn_conv_w[0], v_meta_tokens]
    sm_out = _adamw("adam_small", sm_parts, _pack(sm_w, 1024, 64), _pack(sm_m, 1024, 64), _pack(sm_v, 1024, 64))
    sm_names = ["rw_w2", "rw_a2", "dn_conv_w", "meta_tokens"]
    sm_full_shapes = [(1, RW_LORA, 256), (1, RW_LORA, 256), (1, CONV_W, 768), (N_META, 512)]
    sm_out = [dict(zip(sm_names, [t.reshape(shp) for t, shp in zip(_unpack(o, small_shapes), sm_full_shapes)]))
              for o in sm_out]

    wo_parts = _reduce_finish("rs_w_out", wo_started, dp, 8)
    wo_out = _adamw("adam_w_out", wo_parts, w_out[0], m_w_out[0], v_w_out[0])
    wi_parts = _reduce_finish("rs_w_in", wi_started, dh, 8)
    wi_out = _adamw("adam_w_in", wi_parts, w_in[0], m_w_in[0], v_w_in[0])

    order = ["meta_tokens", "norm_w", "w_in", "rw_shift_mu", "rw_w0", "rw_w2", "rw_a0", "rw_a2", "rw_k_k", "rw_k_a",
             "rw_r_k", "rw_gn_w", "rw_gn_b", "dn_conv_w", "dn_A_log", "dn_dt_bias", "dn_norm_w", "w_out",
             "final_norm_w"]
    outs = [loss, grad_x]
    for kind in range(4):
        table = dict(rep_out[kind])
        table.update(sm_out[kind])
        table["w_in"] = wi_out[kind][None]
        table["w_out"] = wo_out[kind][None]
        outs += [table[n] for n in order]
    return tuple(outs)
```

```python
import functools

import jax
import jax.numpy as jnp
from jax import lax
from jax.experimental import pallas as pl
from jax.experimental.pallas import tpu as pltpu

F32 = jnp.float32
BF16 = jnp.bfloat16
HIGH = lax.Precision.HIGH
HIGHEST = lax.Precision.HIGHEST
MESH = pl.DeviceIdType.MESH

D_MODEL = 2048
N_META = 16
RW_WIDTH = 1024
RW_HEAD = 64
RW_HEADS = 16
RW_LORA = 64
RW_GN_EPS = 64e-5
DN_WIDTH = 1024
DN_HEAD = 128
DN_HEADS = 8
CONV_W = 4
CHUNK = 64
NORM_EPS = 1e-6
IN_COLS = 8336
N_CHIPS = 4
SHARD_COLS = IN_COLS // N_CHIPS

NP_COLS = 8 * 1024 + 256
GAP_AT = IN_COLS - DN_WIDTH
GAP = NP_COLS - IN_COLS
OFF_R, OFF_K, OFF_V, OFF_S1, OFF_GATE = 0, 1024, 2048, 3072, 3200
OFF_DQ, OFF_DK, OFF_DV, OFF_S2, OFF_Z = 4224, 5248, 6272, 7296, 7424

ADAM_LR = 0.001
ADAM_B1 = 0.9
ADAM_B2 = 0.999
ADAM_EPS = 1e-08
ADAM_WD = 0.01
ADAM_STEP = 10

VMEM_LIMIT_BYTES = 56 * 1024 * 1024
ROW_TILE = 104
MM_ROW_TILE = 832


def _params(sem=None):
    return pltpu.CompilerParams(dimension_semantics=sem, vmem_limit_bytes=VMEM_LIMIT_BYTES)


def _pick(n, target, mult=8):
    best = None
    for d in range(mult, min(n, target) + 1, mult):
        if n % d == 0:
            best = d
    return n if best is None else best


def _mm(name, a, b, mode, tm=MM_ROW_TILE, tn=1408, tk=2048, dep=None):
    if mode == "nn":
        (M, K), (_, N) = a.shape, b.shape
    elif mode == "nt":
        (M, K), (N, _) = a.shape, b.shape
    else:
        (K, M), (_, N) = a.shape, b.shape
    tm = _pick(M, tm, 128 if mode == "tn" else 16)
    tn = _pick(N, tn, 128)
    tk = _pick(K, tk, 8 if mode == "tn" else 128)
    if mode == "nn":
        a_spec = pl.BlockSpec((tm, tk), lambda i, j, k: (i, k))
        b_spec = pl.BlockSpec((tk, tn), lambda i, j, k: (k, j))
        dims = (((1,), (0,)), ((), ()))
    elif mode == "nt":
        a_spec = pl.BlockSpec((tm, tk), lambda i, j, k: (i, k))
        b_spec = pl.BlockSpec((tn, tk), lambda i, j, k: (j, k))
        dims = (((1,), (1,)), ((), ()))
    else:
        a_spec = pl.BlockSpec((tk, tm), lambda i, j, k: (k, i))
        b_spec = pl.BlockSpec((tk, tn), lambda i, j, k: (k, j))
        dims = (((0,), (0,)), ((), ()))

    def body(a_ref, b_ref, *rest):
        o_ref = rest[-1]

        @pl.when(pl.program_id(2) == 0)
        def _():
            o_ref[...] = jnp.zeros_like(o_ref)

        o_ref[...] += lax.dot_general(a_ref[...].astype(BF16), b_ref[...].astype(BF16), dims,
                                      preferred_element_type=F32)

    deps = [] if dep is None else [dep]
    return pl.pallas_call(
        body, name=name, grid=(M // tm, N // tn, K // tk),
        in_specs=[a_spec, b_spec] + [pl.BlockSpec((8, 128), lambda i, j, k: (0, 0))] * len(deps),
        out_specs=pl.BlockSpec((tm, tn), lambda i, j, k: (i, j)),
        out_shape=jax.ShapeDtypeStruct((M, N), F32),
        compiler_params=_params(("parallel", "parallel", "arbitrary")),
    )(a, b, *deps)


def _row(arr, width=None, cb=0):
    return (arr, arr.shape[1] if width is None else width, cb)


def _rowwise(name, fn, rows, params, out_widths, tm=ROW_TILE, out_dtype=F32):
    R = rows[0][0].shape[0]
    tm = _pick(R, tm, 16 if out_dtype == BF16 else 8)
    n_r, n_p = len(rows), len(params)

    def body(*refs):
        vals = [r[...] for r in refs[:n_r + n_p]]
        for o_ref, val in zip(refs[n_r + n_p:], fn(*vals)):
            o_ref[...] = val.astype(out_dtype)

    in_specs = [pl.BlockSpec((tm, w), lambda i, cb=cb: (i, cb)) for (_, w, cb) in rows]
    in_specs += [pl.BlockSpec(p.shape, lambda i: (0, 0)) for p in params]
    return pl.pallas_call(
        body, name=name, grid=(R // tm,), in_specs=in_specs,
        out_specs=[pl.BlockSpec((tm, w), lambda i: (i, 0)) for w in out_widths],
        out_shape=[jax.ShapeDtypeStruct((R, w), out_dtype) for w in out_widths],
        compiler_params=_params(("parallel",)),
    )(*[r[0] for r in rows], *params)


def _rowwise_bwd(name, fn, rows, params, douts, tm=ROW_TILE, row0=0, n_rows=None):
    R = rows[0][0].shape[0] - row0 if n_rows is None else n_rows
    tm = _pick(R, tm)
    assert row0 % 8 == 0 and row0 + R <= rows[0][0].shape[0]

    def in_spec(w, cb):
        if row0 == 0:
            return pl.BlockSpec((tm, w), lambda i: (i, cb))
        return pl.BlockSpec((pl.Element(tm), pl.Element(w)), lambda i: (pl.multiple_of(row0 + i * tm, 8), cb * w))

    n_r, n_p, n_d = len(rows), len(params), len(douts)

    def body(*refs):
        vals = [r[...] for r in refs[:n_r + n_p]]
        cts = tuple(r[...] for r in refs[n_r + n_p:n_r + n_p + n_d])
        grads = jax.vjp(fn, *vals)[1](cts)
        outs = refs[n_r + n_p + n_d:]
        for o_ref, g in zip(outs[:n_r], grads[:n_r]):
            o_ref[...] = g

        @pl.when(pl.program_id(0) == 0)
        def _():
            for o_ref in outs[n_r:]:
                o_ref[...] = jnp.zeros_like(o_ref)

        for o_ref, g in zip(outs[n_r:], grads[n_r:]):
            o_ref[...] += g

    in_specs = [in_spec(w, cb) for (_, w, cb) in rows]
    in_specs += [pl.BlockSpec(p.shape, lambda i: (0, 0)) for p in params]
    in_specs += [in_spec(w, cb) for (_, w, cb) in douts]
    out_specs = [pl.BlockSpec((tm, w), lambda i: (i, 0)) for (_, w, _) in rows]
    out_specs += [pl.BlockSpec(p.shape, lambda i: (0, 0)) for p in params]
    out_shape = [jax.ShapeDtypeStruct((R, w), F32) for (_, w, _) in rows]
    out_shape += [jax.ShapeDtypeStruct(p.shape, F32) for p in params]
    res = pl.pallas_call(
        body, name=name, grid=(R // tm,), in_specs=in_specs, out_specs=out_specs, out_shape=out_shape,
        compiler_params=_params(("arbitrary",)),
    )(*[r[0] for r in rows], *params, *[d[0] for d in douts])
    return res[:n_r], res[n_r:]


def _softplus(x):
    return jnp.maximum(x, 0.0) + jnp.log(1.0 + jnp.exp(-jnp.abs(x)))


def _silu(x):
    return x * jax.nn.sigmoid(x)


def _rms_fn(h, w):
    return (h * lax.rsqrt(jnp.mean(h * h, axis=-1, keepdims=True) + NORM_EPS) * w,)


def _rw_prep_fn(pr, pr1, pk, pk1, pv, pv1, ps, ps1, mu_r, mu_k, mu_v, mu_s, w0, w2p, a0, a2p, k_k, k_a):
    r = pr + (pr1 - pr) * mu_r
    k = pk + (pk1 - pk) * mu_k
    v = pv + (pv1 - pv) * mu_v
    s = ps + (ps1 - ps) * mu_s
    lora = lambda x, w: jnp.dot(x.astype(BF16), w.astype(BF16), preferred_element_type=F32)
    w_log = -_softplus(-(w0 + lora(jnp.tanh(s), w2p))) - 0.5
    log_decay = -jnp.exp(w_log)
    a = jax.nn.sigmoid(a0 + lora(s, a2p))
    return r, log_decay, k * (1.0 + (a - 1.0) * k_a), v, k * k_k, a


def _conv_fn(u0, u1, u2, u3, w0, w1, w2, w3):
    return (_silu(u0 * w3 + u1 * w2 + u2 * w1 + u3 * w0),)


def _dn_gate_fn(ps2, a_log_n, dt_n):
    beta_n = jax.nn.sigmoid(ps2)
    g_n = -jnp.exp(a_log_n) * _softplus(ps2 + dt_n)
    lane = lax.broadcasted_iota(jnp.int32, (1, DN_HEAD), 1)
    pick = lambda x, j: jnp.broadcast_to(jnp.sum(x * (lane == j).astype(F32), axis=-1, keepdims=True), x.shape)[None]
    beta = jnp.concatenate([pick(beta_n, h) for h in range(DN_HEADS)], axis=0)
    g = jnp.concatenate([pick(g_n, DN_HEADS + h) for h in range(DN_HEADS)], axis=0)
    return beta, g


def _tril_masks(c):
    t = lax.broadcasted_iota(jnp.int32, (c, c), 0)
    s = lax.broadcasted_iota(jnp.int32, (c, c), 1)
    return s <= t, s < t


def _dot(x, y, cx, cy, prec=HIGH):
    nb = x.ndim - 2
    batch = tuple(range(nb))
    return lax.dot_general(x, y, (((cx + nb,), (cy + nb,)), (batch, batch)), precision=prec,
                           preferred_element_type=F32)


INV_BASE = 8


def _run_interleaved(running):
    results, live = [None] * len(running), list(range(len(running)))
    while live:
        for k in list(live):
            try:
                next(running[k])
            except StopIteration as done:
                results[k] = done.value
                live.remove(k)
    return results


def _blocked_inv(a, mm, row, col, mm_merge=None):
    mm_merge = mm if mm_merge is None else mm_merge
    shift = INV_BASE.bit_length() - 1
    d = jnp.where(jnp.right_shift(row, shift) == jnp.right_shift(col, shift), a, 0.0)
    x = (row == col).astype(F32) - d
    p = mm(d, d)
    yield
    x = x + mm(x, p)
    yield
    p = mm(p, p)
    yield
    x = x + mm(x, p)
    yield
    size = INV_BASE
    while size < a.shape[-2]:
        same_pair = jnp.right_shift(row, shift + 1) == jnp.right_shift(col, shift + 1)
        lower_left = (jnp.right_shift(row, shift) & 1 == 1) & (jnp.right_shift(col, shift) & 1 == 0)
        lower = mm_merge(jnp.where(same_pair & lower_left, a, 0.0), x)
        yield
        x = x - mm_merge(x, lower)
        size, shift = 2 * size, shift + 1
        yield
    return x


def _tri_inv(a):
    c = a.shape[-1]
    row, col = lax.broadcasted_iota(jnp.int32, (c, c), 0), lax.broadcasted_iota(jnp.int32, (c, c), 1)
    return _blocked_inv(a, lambda x, y: _dot(x, y, 1, 0), row, col,
                        lambda x, y: _dot(x.astype(BF16), y.astype(BF16), 1, 0, None))


@jax.custom_vjp
def _tri_inv_saved(a, t):
    return t


def _tri_inv_saved_fwd(a, t):
    return t, t


def _tri_inv_saved_bwd(t, dt):
    return -_dot(_dot(t, dt, 0, 0), t, 1, 1), jnp.zeros_like(t)


_tri_inv_saved.defvjp(_tri_inv_saved_fwd, _tri_inv_saved_bwd)


def _pair_masks():
    lane = lax.broadcasted_iota(jnp.int32, (1, 2 * RW_HEAD), 1)
    m0 = (lane < RW_HEAD).astype(F32)
    return m0, 1.0 - m0


def _pair_bd(x):
    m0, m1 = _pair_masks()
    return jnp.concatenate([x * m0, x * m1], axis=-2)


def _pair_mm(x, y):
    return _dot(x, _pair_bd(y), 1, 0)


def _pair_inv(a):
    c = a.shape[-2]
    row = lax.broadcasted_iota(jnp.int32, (c, 2 * RW_HEAD), 0)
    col = lax.broadcasted_iota(jnp.int32, (c, 2 * RW_HEAD), 1) & (RW_HEAD - 1)
    return _blocked_inv(a, _pair_mm, row, col,
                        lambda x, y: _dot(x.astype(BF16), _pair_bd(y).astype(BF16), 1, 0, None))


@jax.custom_vjp
def _pair_inv_saved(a, t):
    return t


def _pair_inv_saved_fwd(a, t):
    return t, t


def _pair_inv_saved_bwd(t, dt):
    m0, m1 = _pair_masks()
    c = t.shape[-2]
    z = _dot(t, dt, 0, 0)
    x = z[:, :c, :] * m0 + z[:, c:, :] * m1
    return -_dot(x, _pair_bd(t), 1, 1), jnp.zeros_like(t)


_pair_inv_saved.defvjp(_pair_inv_saved_fwd, _pair_inv_saved_bwd)


@jax.custom_vjp
def _use_saved(x, x_saved):
    return x_saved


_use_saved.defvjp(lambda x, x_saved: (x_saved, None), lambda _, g: (g, jnp.zeros_like(g)))


def _rw_chunk_steps(S, r, lw, k2, v, kkp, a, gate, rk, gnw, gnb, saved=None):
    B, C, P = r.shape
    m0, m1 = _pair_masks()
    seg = lambda x: (jnp.sum(x * m0, axis=-1, keepdims=True) * m0 + jnp.sum(x * m1, axis=-1, keepdims=True) * m1)
    mm = lambda x, y: _dot(x.astype(BF16), y.astype(BF16), 1, 0, None)
    nt = lambda x, y: _dot(x, y, 1, 1)
    tn = lambda x, y: _dot(x, y, 0, 0)
    pair_mm = lambda x, y: mm(x, _pair_bd(y))
    t_idx = lax.broadcasted_iota(jnp.int32, (C, P), 0)
    s_idx = lax.broadcasted_iota(jnp.int32, (C, P), 1) & (RW_HEAD - 1)
    incl, strict = s_idx <= t_idx, s_idx < t_idx
    tril = jnp.broadcast_to(_tril_masks(C)[0].astype(F32), (B, C, C))
    kk = kkp * lax.rsqrt(seg(kkp * kkp) + 1e-6)
    b = kk * a
    reuse = (lambda x, i: x) if saved is None else (lambda x, i: _use_saved(x, saved[i]))
    g_incl = reuse(_dot(tril, lw, 1, 0), 1)
    g_excl = g_incl - lw
    inv = jnp.exp(-g_incl)
    alpha, beta, kappa, rho = kk * jnp.exp(g_excl), b * inv, k2 * inv, r * jnp.exp(g_incl)
    yield
    ar = jnp.concatenate([alpha, rho], axis=-2)
    scores = reuse(nt(ar, jnp.concatenate([_pair_bd(beta), _pair_bd(kappa)], axis=-2)), 2)
    a_ab = jnp.where(strict, scores[:, :C, :P], 0.0)
    a_ak = jnp.where(strict, scores[:, :C, P:], 0.0)
    r_b = jnp.where(incl, scores[:, C:, :P], 0.0)
    r_k = jnp.where(incl, scores[:, C:, P:], 0.0)
    yield
    t_inv = (yield from _pair_inv(a_ab)) if saved is None else _pair_inv_saved(a_ab, saved[0])
    on_state = reuse(nt(ar, S), 3)
    yield
    u = pair_mm(t_inv, -on_state[:, :C, :] - pair_mm(a_ak, v))
    yield
    y = on_state[:, C:, :] + mm(jnp.concatenate([r_b, r_k], axis=-1),
                                jnp.concatenate([_pair_bd(u), _pair_bd(v)], axis=-2))
    same_head = ((lax.broadcasted_iota(jnp.int32, (P, P), 0) < RW_HEAD)
                 == (lax.broadcasted_iota(jnp.int32, (P, P), 1) < RW_HEAD))
    yield
    fresh = tn(jnp.concatenate([u, v], axis=-2), jnp.concatenate([beta, kappa], axis=-2))
    yield
    S_new = jnp.exp(jnp.sum(lw, axis=-2, keepdims=True)) * (S + jnp.where(same_head, fresh, 0.0))
    dev = y - seg(y) * (1.0 / RW_HEAD)
    yn = dev * lax.rsqrt(seg(dev * dev) * (1.0 / RW_HEAD) + RW_GN_EPS) * gnw + gnb
    bonus = seg(r * k2 * rk) * v
    return (yn + bonus) * _silu(gate), S_new, (t_inv, g_incl, scores, on_state)


def _dn_chunk_steps(S, qc, kc, vc, bb, gb, z, nw, saved=None):
    B, C, D = qc.shape
    mm = lambda x, y: _dot(x.astype(BF16), y.astype(BF16), 1, 0, None)
    nt = lambda x, y, p=None: _dot(x, y, 1, 1, p) if p else _dot(x.astype(BF16), y.astype(BF16), 1, 1, None)
    tn = lambda x, y: _dot(x.astype(BF16), y.astype(BF16), 0, 0, None)
    incl, strict = _tril_masks(C)
    q = qc * lax.rsqrt(jnp.sum(qc * qc, axis=-1, keepdims=True) + 1e-6) * (D ** -0.5)
    k = kc * lax.rsqrt(jnp.sum(kc * kc, axis=-1, keepdims=True) + 1e-6)
    kb, vb = k * bb, vc * bb
    yield
    G = _dot(jnp.broadcast_to(incl.astype(F32), (B, C, C)), gb, 1, 0, HIGHEST)
    G = G if saved is None else _use_saved(G, saved[1])
    lane = lax.broadcasted_iota(jnp.int32, (C, D), 1)
    e0, e1 = (lane == 0).astype(F32), (lane == 1).astype(F32)
    diff = nt(G * e0 + e1, e0 - G * e1, HIGHEST)
    dmask = jnp.where(incl, jnp.exp(jnp.where(incl, diff, 0.0)), 0.0)
    M = jnp.where(strict, nt(kb, k) * dmask, 0.0)
    yield
    T = (yield from _tri_inv(M)) if saved is None else _tri_inv_saved(M, saved[0])
    eG = jnp.exp(G)
    u = mm(T, vb)
    yield
    w = mm(T, kb * eG)
    yield
    attn = jnp.where(incl, nt(q, k) * dmask, 0.0)
    yield
    v_new = u - mm(w, S)
    yield
    o = mm(q * eG, S) + mm(attn, v_new)
    yield
    g_last = jnp.sum(gb, axis=-2, keepdims=True)
    S_new = S * jnp.exp(jnp.broadcast_to(g_last, S.shape)) + tn(k * jnp.exp(g_last - G), v_new)
    on = o * lax.rsqrt(jnp.mean(o * o, axis=-1, keepdims=True) + NORM_EPS) * nw
    return on * _silu(z), S_new, (T, G)


N_GROUPS = 8
GROUP = 128
HALO = 8


def _groups(x):
    return jnp.concatenate([x[:, g * GROUP:(g + 1) * GROUP][None] for g in range(N_GROUPS)], axis=0)


@functools.partial(jax.custom_vjp, nondiff_argnums=(1,))
def _shift_rows(ext, j):
    return pltpu.roll(ext, j, 0)[HALO:, :]


def _shift_rows_fwd(ext, j):
    return _shift_rows(ext, j), None


def _shift_rows_bwd(j, _, d):
    z = jnp.concatenate([jnp.zeros((HALO, d.shape[1]), d.dtype), d], axis=0)
    return (pltpu.roll(z, z.shape[0] - j, 0),)


_shift_rows.defvjp(_shift_rows_fwd, _shift_rows_bwd)


def _rw_fused_steps(S, h_r, h_k, h_v, h_s, p_r, p_k, p_v, p_s, gate, *pars, saved=None):
    prev = lambda h, x: _shift_rows(jnp.concatenate([h, x], axis=0), 1)
    seq = _rw_prep_fn(p_r, prev(h_r, p_r), p_k, prev(h_k, p_k), p_v, prev(h_v, p_v), p_s, prev(h_s, p_s), *pars[:10])
    yield
    return (yield from _rw_chunk_steps(S, *[_groups(t) for t in seq], _groups(gate),
                                       *[_groups(t) for t in pars[10:]], saved=saved))


def _dn_fused_steps(S, h_q, h_k, h_v, p_q, p_k, p_v, p_s2, z, *pars, saved=None):
    conv = []
    for i, (h, x) in enumerate(((h_q, p_q), (h_k, p_k), (h_v, p_v))):
        ext = jnp.concatenate([h, x], axis=0)
        conv += _conv_fn(x, _shift_rows(ext, 1), _shift_rows(ext, 2), _shift_rows(ext, 3), *pars[4 * i:4 * i + 4])
    beta, g = _dn_gate_fn(p_s2, pars[12], pars[13])
    yield
    return (yield from _dn_chunk_steps(S, *[_groups(t) for t in conv], beta, g, _groups(z), _groups(pars[14]),
                                       saved=saved))


def _row_specs(rows, n_halo, row_of):
    C = CHUNK
    specs = [pl.BlockSpec((pl.Element(C), pl.Element(w)), lambda c, off=off: (row_of(c) * C, off))
             for (_, w, off) in rows]
    specs += [pl.BlockSpec((pl.Element(HALO), pl.Element(w)),
                           lambda c, off=off: (pl.multiple_of(jnp.maximum(row_of(c) * C - HALO, 0), HALO), off))
              for (_, w, off) in rows[:n_halo]]
    return specs


def _mixers_fwd(name, parts):
    Lp = parts[0][1][0][0].shape[0]
    C, D, NB = CHUNK, GROUP, N_GROUPS
    nc, width = Lp // C, NB * D
    n_in = [len(rows) + n_halo + len(pars) for (_, rows, n_halo, pars, _) in parts]
    n_out = [1 + len(shapes) for (*_, shapes) in parts]

    def body(*refs):
        ins, y_ref = refs[:sum(n_in)], refs[sum(n_in)]
        outs, states = refs[sum(n_in) + 1:sum(n_in) + 1 + sum(n_out)], refs[sum(n_in) + 1 + sum(n_out):]
        first = pl.program_id(0) == 0

        @pl.when(first)
        def _():
            for s_ref in states:
                s_ref[...] = jnp.zeros_like(s_ref)

        i_at, running = 0, []
        for k, (steps, rows, n_halo, pars, shapes) in enumerate(parts):
            n_r = len(rows)
            row_refs, halo_refs = ins[i_at:i_at + n_r], ins[i_at + n_r:i_at + n_r + n_halo]
            par_refs = ins[i_at + n_r + n_halo:i_at + n_in[k]]
            S = states[k][...]
            halos = [jnp.where(first, 0.0, r[...]) for r in halo_refs]
            running.append(steps(S, *halos, *[r[...] for r in row_refs], *[r[...] for r in par_refs]))
            i_at += n_in[k]
        results = _run_interleaved(running)
        o_at = 0
        for k, (y, S_new, saved) in enumerate(results):
            ck_ref, saved_refs = outs[o_at], outs[o_at + 1:o_at + n_out[k]]
            ck_ref[0] = states[k][...]
            for g in range(NB):
                y_ref[:, k * width + g * D:k * width + (g + 1) * D] = y[g].astype(BF16)
            for o_ref, val in zip(saved_refs, saved):
                o_ref[0] = val
            states[k][...] = S_new
            o_at += n_out[k]

    in_specs, args, out_specs, out_shape = [], [], [], []
    for (_, rows, n_halo, pars, shapes) in parts:
        in_specs += _row_specs(rows, n_halo, lambda c: c) + [pl.BlockSpec(p.shape, lambda c: (0, 0)) for p in pars]
        args += [r[0] for r in rows] + [r[0] for r in rows[:n_halo]] + list(pars)
        for shp in [(NB, D, D)] + list(shapes):
            out_specs.append(pl.BlockSpec((1,) + tuple(shp), lambda c: (c, 0, 0, 0)))
            out_shape.append(jax.ShapeDtypeStruct((nc,) + tuple(shp), F32))
    res = pl.pallas_call(
        body, name=name, grid=(nc,), in_specs=in_specs,
        out_specs=[pl.BlockSpec((C, width * len(parts)), lambda c: (c, 0))] + out_specs,
        out_shape=[jax.ShapeDtypeStruct((Lp, width * len(parts)), BF16)] + out_shape,
        scratch_shapes=[pltpu.VMEM((NB, D, D), F32)] * len(parts),
        compiler_params=_params(("arbitrary",)),
    )(*args)
    per_part, at = [], 1
    for n in n_out:
        per_part.append(list(res[at:at + n]))
        at += n
    return res[0], per_part


def _mixers_bwd(name, parts, kept, dy):
    Lp = parts[0][1][0][0].shape[0]
    C, D, NB = CHUNK, GROUP, N_GROUPS
    nc = Lp // C
    n_in = [len(rows) + n_halo + len(pars) + len(kept[k]) for k, (_, rows, n_halo, pars, _) in enumerate(parts)]
    widths = [sum(w for (_, w, _) in rows) for (_, rows, *_) in parts]
    n_par = [len(pars) for (_, _, _, pars, _) in parts]
    n_scr = [1 + n_halo for (_, _, n_halo, _, _) in parts]
    rev = lambda c: nc - 1 - c

    def body(*refs):
        ins, dy_ref = refs[:sum(n_in)], refs[sum(n_in)]
        drows_ref = refs[sum(n_in) + 1]
        dpar_all = refs[sum(n_in) + 2:sum(n_in) + 2 + sum(n_par)]
        scratch = refs[sum(n_in) + 2 + sum(n_par):]
        i = pl.program_id(0)

        @pl.when(i == 0)
        def _():
            for o_ref in list(dpar_all) + list(scratch):
                o_ref[...] = jnp.zeros_like(o_ref)

        i_at, vals, stored, cts, sizes = 0, [], [], [], []
        for k, (_, rows, n_halo, pars, _) in enumerate(parts):
            n_r = len(rows)
            row_refs, halo_refs = ins[i_at:i_at + n_r], ins[i_at + n_r:i_at + n_r + n_halo]
            par_refs = ins[i_at + n_r + n_halo:i_at + n_r + n_halo + n_par[k]]
            kept_refs = ins[i_at + n_r + n_halo + n_par[k]:i_at + n_in[k]]
            halos = [jnp.where(i == nc - 1, 0.0, r[...]) for r in halo_refs]
            part_vals = [kept_refs[0][0]] + halos + [r[...] for r in row_refs] + [r[...] for r in par_refs]
            vals += part_vals
            sizes.append(len(part_vals))
            stored.append(tuple(r[0] for r in kept_refs[1:]))
            cts.append((_groups(dy_ref[:, k * NB * D:(k + 1) * NB * D]), scratch[sum(n_scr[:k])][...]))
            i_at += n_in[k]

        def both(*flat):
            running, at = [], 0
            for k, (steps, *_) in enumerate(parts):
                running.append(steps(*flat[at:at + sizes[k]], saved=stored[k]))
                at += sizes[k]
            return tuple(r[:2] for r in _run_interleaved(running))

        all_grads = jax.vjp(both, *vals)[1](tuple(cts))
        i_at = p_at = s_at = g_at = col = 0
        for k, (_, rows, n_halo, pars, _) in enumerate(parts):
            n_r = len(rows)
            grads = all_grads[g_at:g_at + sizes[k]]
            dpar_refs = dpar_all[p_at:p_at + n_par[k]]
            ds_ref, carry_refs = scratch[s_at], scratch[s_at + 1:s_at + n_scr[k]]
            ds_ref[...] = grads[0]
            d_halos, d_rows = grads[1:1 + n_halo], grads[1 + n_halo:1 + n_halo + n_r]
            for j, g in enumerate(d_rows):
                if j < n_halo:
                    g = g + jnp.concatenate([jnp.zeros((C - HALO, g.shape[1]), F32), carry_refs[j][...]], axis=0)
                    carry_refs[j][...] = d_halos[j]
                drows_ref[:, col:col + g.shape[1]] = g.astype(BF16)
                col += g.shape[1]
            for o_ref, g in zip(dpar_refs, grads[1 + n_halo + n_r:]):
                o_ref[...] += g
            p_at, s_at, g_at = p_at + n_par[k], s_at + n_scr[k], g_at + sizes[k]

    in_specs, args, out_specs, out_shape, scratch_shapes = [], [], [], [], []
    for k, (_, rows, n_halo, pars, _) in enumerate(parts):
        in_specs += _row_specs(rows, n_halo, rev) + [pl.BlockSpec(p.shape, lambda c: (0, 0)) for p in pars]
        in_specs += [pl.BlockSpec((1,) + tuple(t.shape[1:]), lambda c: (rev(c), 0, 0, 0)) for t in kept[k]]
        args += [r[0] for r in rows] + [r[0] for r in rows[:n_halo]] + list(pars) + list(kept[k])
        out_specs += [pl.BlockSpec(p.shape, lambda c: (0, 0)) for p in pars]
        out_shape += [jax.ShapeDtypeStruct(p.shape, F32) for p in pars]
        scratch_shapes += [pltpu.VMEM((NB, D, D), F32)] + [pltpu.VMEM((HALO, w), F32) for (_, w, _) in rows[:n_halo]]
    res = pl.pallas_call(
        body, name=name, grid=(nc,),
        in_specs=in_specs + [pl.BlockSpec((C, dy.shape[1]), lambda c: (rev(c), 0))],
        out_specs=[pl.BlockSpec((C, sum(widths)), lambda c: (rev(c), 0))] + out_specs,
        out_shape=[jax.ShapeDtypeStruct((Lp, sum(widths)), BF16)] + out_shape,
        scratch_shapes=scratch_shapes, compiler_params=_params(("arbitrary",)),
    )(*args, dy)
    d_pars, at = [], 1
    for n in n_par:
        d_pars.append(list(res[at:at + n]))
        at += n
    return res[0], d_pars


def _loss_head(h, yw, target, fw, tm=2 * ROW_TILE):
    Lp, Dm = h.shape
    S = target.shape[0]
    n_real = N_META + S
    tm = _pick(Lp, tm, 16)
    n_tiles = Lp // tm
    assert n_tiles >= 2 and N_META < tm <= S and Lp - n_real < tm and (S - tm) % 8 == 0 and N_META % 8 == 0

    def out_fn(z, fw_):
        return z * lax.rsqrt(jnp.mean(z * z, axis=-1, keepdims=True) + NORM_EPS) * fw_

    def body(h_ref, yw_ref, t_ref, fw_ref, loss_ref, dz_ref, dz16_ref, dfw_ref):
        i = pl.program_id(0)

        @pl.when(i == 0)
        def _():
            loss_ref[...] = jnp.zeros_like(loss_ref)
            dfw_ref[...] = jnp.zeros_like(dfw_ref)

        row = i * tm + lax.broadcasted_iota(jnp.int32, (tm, 1), 0)
        mask = ((row >= N_META) & (row < n_real)).astype(F32)
        z = h_ref[...] + yw_ref[...]
        o, vjp = jax.vjp(out_fn, z, fw_ref[...])
        t = t_ref[...]
        t = jnp.where(i == 0, pltpu.roll(t, N_META, 0), t)
        if Lp > n_real:
            t = jnp.where(i == n_tiles - 1, pltpu.roll(t, tm - (Lp - n_real), 0), t)
        err = (o - t) * mask
        row_loss = 0.5 * jnp.mean(jnp.square(err), axis=-1, keepdims=True)
        dz, dfw = vjp(err * (1.0 / Dm))
        loss_ref[...] += jnp.sum(row_loss, axis=0, keepdims=True)
        dz_ref[...] = dz
        dz16_ref[...] = dz.astype(BF16)
        dfw_ref[...] += dfw

    row_spec = pl.BlockSpec((tm, Dm), lambda i: (i, 0))
    target_spec = pl.BlockSpec((pl.Element(tm), pl.Element(Dm)),
                               lambda i: (pl.multiple_of(jnp.clip(i * tm - N_META, 0, S - tm), 8), 0))
    return pl.pallas_call(
        body, name="loss_head", grid=(n_tiles,),
        in_specs=[row_spec, row_spec, target_spec, pl.BlockSpec((1, Dm), lambda i: (0, 0))],
        out_specs=[pl.BlockSpec((8, 128), lambda i: (0, 0)), row_spec, row_spec,
                   pl.BlockSpec((1, Dm), lambda i: (0, 0))],
        out_shape=[jax.ShapeDtypeStruct((8, 128), F32), jax.ShapeDtypeStruct((Lp, Dm), F32),
                   jax.ShapeDtypeStruct((Lp, Dm), BF16), jax.ShapeDtypeStruct((1, Dm), F32)],
        compiler_params=_params(("arbitrary",)),
    )(h, yw, target, fw)


def _exchange(name, x, masks, slot_kind, per_dest, n_split=1, copy_own=True, other_half=False):
    n = len(masks)
    keep_own = slot_kind is not None and copy_own
    n_slots = {"chip": 4, "core": 2, "dev": 8, None: n}[slot_kind]
    blk_shape = x.shape[1:] if per_dest else x.shape
    if other_half:
        blk_shape = (x.shape[0] // 2,) + tuple(x.shape[1:])
    rows = blk_shape[0] // n_split

    def body(x_ref, o_ref, send_sems, recv_sems, local_sems):
        mx, my, mc = lax.axis_index("x"), lax.axis_index("y"), lax.axis_index("c")
        if other_half:
            x_ref = x_ref.at[pl.ds((1 - mc) * blk_shape[0], blk_shape[0])]

        def slot(k, px, py, pc):
            return {"chip": 2 * px + py, "core": pc, "dev": 4 * px + 2 * py + pc, None: k}[slot_kind]

        def peer(m):
            return (mx + m[0]) % 2, (my + m[1]) % 2, (mc + m[2]) % 2

        def part(ref, j):
            return ref.at[pl.ds(j * rows, rows)]

        own_src = x_ref.at[2 * mx + my] if per_dest else x_ref
        local = []
        if keep_own:
            own_dst = o_ref.at[slot(0, mx, my, mc)]
            local = [pltpu.make_async_copy(part(own_src, j), part(own_dst, j), local_sems.at[j])
                     for j in range(n_split)]
        for cp in local:
            cp.start()
        sends = []
        for k, m in enumerate(masks):
            px, py, pc = peer(m)
            src = x_ref.at[2 * px + py] if per_dest else x_ref
            dst = o_ref.at[slot(k, mx, my, mc)]
            for j in range(n_split):
                sends.append(pltpu.make_async_remote_copy(
                    src_ref=part(src, j), dst_ref=part(dst, j), send_sem=send_sems.at[k * n_split + j],
                    recv_sem=recv_sems.at[k * n_split + j], device_id=(px, py, pc), device_id_type=MESH))
        for cp in sends:
            cp.start()
        for k, m in enumerate(masks):
            px, py, pc = peer(m)
            landed = o_ref.at[slot(k, px, py, pc)]
            for j in range(n_split):
                pltpu.make_async_remote_copy(
                    src_ref=part(own_src, j), dst_ref=part(landed, j), send_sem=send_sems.at[k * n_split + j],
                    recv_sem=recv_sems.at[k * n_split + j], device_id=(px, py, pc), device_id_type=MESH).wait_recv()
        for cp in sends:
            cp.wait_send()
        for cp in local:
            cp.wait()

    return pl.pallas_call(
        body, name=name,
        in_specs=[pl.BlockSpec(memory_space=pl.ANY)], out_specs=pl.BlockSpec(memory_space=pl.ANY),
        out_shape=jax.ShapeDtypeStruct((n_slots,) + tuple(blk_shape), x.dtype),
        scratch_shapes=[pltpu.SemaphoreType.DMA((n * n_split,)), pltpu.SemaphoreType.DMA((n * n_split,)),
                        pltpu.SemaphoreType.DMA((n_split,))],
        compiler_params=pltpu.CompilerParams(has_side_effects=True),
    )(x)


CHIP_MASKS = [(1, 0, 0), (0, 1, 0), (1, 1, 0)]
CORE_MASKS = [(0, 0, 1)]
ALL_MASKS = [(dx, dy, dc) for dx in (0, 1) for dy in (0, 1) for dc in (0, 1) if (dx, dy, dc) != (0, 0, 0)]

HBM_SPEC = pl.BlockSpec(memory_space=pltpu.HBM)
SEM_SPEC = pl.BlockSpec(memory_space=pltpu.SEMAPHORE)
DATAFLOW = pltpu.SideEffectType.DATAFLOW_SIDE_EFFECTING


def _split_copies(x_ref, land_ref, send_sems, recv_sems, masks, slot_kind, per_dest, n_split, with_recvs=True):
    mx, my, mc = lax.axis_index("x"), lax.axis_index("y"), lax.axis_index("c")
    slot = lambda px, py, pc: {"chip": 2 * px + py, "core": pc}[slot_kind]
    rows = (x_ref.shape[1] if per_dest else x_ref.shape[0]) // n_split
    part = lambda ref, j: ref.at[pl.ds(j * rows, rows)]
    sends, recvs = [], []
    for k, m in enumerate(masks):
        px, py, pc = (mx + m[0]) % 2, (my + m[1]) % 2, (mc + m[2]) % 2
        src = x_ref.at[2 * px + py] if per_dest else x_ref
        own_src = x_ref.at[2 * mx + my] if per_dest else x_ref
        for j in range(n_split):
            sems = dict(send_sem=send_sems.at[k * n_split + j], recv_sem=recv_sems.at[k * n_split + j],
                        device_id=(px, py, pc), device_id_type=MESH)
            sends.append(pltpu.make_async_remote_copy(
                src_ref=part(src, j), dst_ref=part(land_ref.at[slot(mx, my, mc)], j), **sems))
            if with_recvs:
                recvs.append(pltpu.make_async_remote_copy(
                    src_ref=part(own_src, j), dst_ref=part(land_ref.at[slot(px, py, pc)], j), **sems))
    return sends, recvs


def _exchange_start(name, x, masks, slot_kind, per_dest, n_split, dep=None):
    n = len(masks) * n_split
    blk_shape = x.shape[1:] if per_dest else x.shape
    land_shape = ({"chip": 4, "core": 2}[slot_kind],) + tuple(blk_shape)
    deps = [] if dep is None else [dep]

    def body(x_ref, land_ref, *rest):
        send_sems, recv_sems, x_thru, land_thru, token = rest[len(deps):]
        for cp in _split_copies(x_ref, land_ref, send_sems, recv_sems, masks, slot_kind, per_dest, n_split, False)[0]:
            cp.start()
        token[...] = jnp.zeros_like(token)

    return pl.pallas_call(
        body, name=name,
        out_shape=(pltpu.SemaphoreType.DMA((n,)), pltpu.SemaphoreType.DMA((n,)), pltpu.HBM(x.shape, x.dtype),
                   pltpu.HBM(land_shape, x.dtype), jax.ShapeDtypeStruct((8, 128), F32)),
        in_specs=(HBM_SPEC, HBM_SPEC) + (pl.BlockSpec(memory_space=pl.ANY),) * len(deps),
        out_specs=(SEM_SPEC, SEM_SPEC, HBM_SPEC, HBM_SPEC, pl.BlockSpec(memory_space=pltpu.VMEM)),
        input_output_aliases={0: 2, 1: 3},
        compiler_params=pltpu.CompilerParams(has_side_effects=DATAFLOW),
    )(pltpu.with_memory_space_constraint(x, pltpu.HBM),
      pltpu.with_memory_space_constraint(lax.empty(land_shape, x.dtype), pltpu.HBM), *deps)


def _exchange_wait(name, started, after, masks, slot_kind, per_dest, n_split):
    send_sems, recv_sems, x_thru, land_thru, _ = started

    def body(x_ref, land_ref, send_sems, recv_sems, after_ref, x_out, land_out):
        sends, recvs = _split_copies(x_ref, land_ref, send_sems, recv_sems, masks, slot_kind, per_dest, n_split)
        for cp in sends:
            cp.wait_send()
        for cp in recvs:
            cp.wait_recv()

    return pl.pallas_call(
        body, name=name,
        out_shape=(pltpu.HBM(x_thru.shape, x_thru.dtype), pltpu.HBM(land_thru.shape, land_thru.dtype)),
        in_specs=(HBM_SPEC, HBM_SPEC, SEM_SPEC, SEM_SPEC, pl.BlockSpec(memory_space=pl.ANY)),
        out_specs=(HBM_SPEC, HBM_SPEC), input_output_aliases={0: 0, 1: 1},
        compiler_params=pltpu.CompilerParams(has_side_effects=DATAFLOW),
    )(x_thru, land_thru, send_sems, recv_sems, after)


def _gather_chips(name, x):
    return _exchange(name, x, CHIP_MASKS, "chip", False)


def _gather_shards(name, shard, n_split, overlap=False, dep=None):
    half = shard.shape[0] // 2
    mine = lax.dynamic_slice_in_dim(shard, lax.axis_index("c") * half, half, axis=0)
    if overlap:
        return _exchange_start(name + "_chips", mine, CHIP_MASKS, "chip", False, n_split, dep)
    by_chip = _exchange(name + "_chips", mine, CHIP_MASKS, "chip", False, n_split, copy_own=False)
    return _gather_tail(name, mine, by_chip)


def _gather_tail(name, mine, by_chip):
    c, chip = lax.axis_index("c"), 2 * lax.axis_index("x") + lax.axis_index("y")
    by_chip = lax.dynamic_update_index_in_dim(by_chip, mine, chip, 0)
    both = _exchange(name + "_cores", by_chip, CORE_MASKS, "core", False, N_CHIPS, copy_own=False)
    return lax.dynamic_update_index_in_dim(both, by_chip, c, 0)


def _gather_finish(name, started, after, n_split):
    mine, by_chip = _exchange_wait(name + "_chips_wait", started, after, CHIP_MASKS, "chip", False, n_split)
    return _gather_tail(name, mine, by_chip)


def _sum_slots(name, x, tr=128):
    S, R, N = x.shape
    tr = _pick(R, tr, 16)

    def body(x_ref, o_ref):
        acc = x_ref[0].astype(F32)
        for s in range(1, S):
            acc = acc + x_ref[s].astype(F32)
        o_ref[...] = acc

    return pl.pallas_call(
        body, name=name, grid=(R // tr,),
        in_specs=[pl.BlockSpec((S, tr, N), lambda i: (0, i, 0))], out_specs=pl.BlockSpec((tr, N), lambda i: (i, 0)),
        out_shape=jax.ShapeDtypeStruct((R, N), F32), compiler_params=_params(("parallel",)),
    )(x)


def _add_to_bf16(name, a, b, tr=128):
    S, R, N = a.shape
    tr = _pick(R, tr, 16)

    def body(a_ref, b_ref, o_ref):
        o_ref[...] = (a_ref[...] + b_ref[...]).astype(BF16)

    spec = pl.BlockSpec((S, tr, N), lambda i: (0, i, 0))
    return pl.pallas_call(
        body, name=name, grid=(R // tr,), in_specs=[spec, spec], out_specs=spec,
        out_shape=jax.ShapeDtypeStruct((S, R, N), BF16), compiler_params=_params(("parallel",)),
    )(a, b)


def _add_slabs_to_bf16(name, full, recv, c, tr=64):
    R, NP = full.shape
    half = R // 2
    nb = half // tr

    def body(c_ref, a_ref, b_ref, o_ref):
        x = a_ref[...] + b_ref[...]
        for s in range(N_CHIPS - 1):
            o_ref[s] = x[:, s * SHARD_COLS:(s + 1) * SHARD_COLS].astype(BF16)
        last = jnp.concatenate([x[:, (N_CHIPS - 1) * SHARD_COLS:GAP_AT], x[:, GAP_AT + GAP:]], axis=1)
        o_ref[N_CHIPS - 1] = last.astype(BF16)

    grid_spec = pltpu.PrefetchScalarGridSpec(
        num_scalar_prefetch=1, grid=(nb,),
        in_specs=[pl.BlockSpec((tr, NP), lambda i, c_ref: (c_ref[0] * nb + i, 0)),
                  pl.BlockSpec((tr, NP), lambda i, c_ref: (i, 0))],
        out_specs=pl.BlockSpec((N_CHIPS, tr, SHARD_COLS), lambda i, c_ref: (0, i, 0)))
    return pl.pallas_call(
        body, name=name, grid_spec=grid_spec,
        out_shape=jax.ShapeDtypeStruct((N_CHIPS, half, SHARD_COLS), BF16), compiler_params=_params(("parallel",)),
    )(jnp.reshape(c, (1,)).astype(jnp.int32), full, recv)


def _adamw(name, gparts, w, m, v, tr=128):
    S, R, N = gparts.shape
    tr = _pick(R, tr)
    c1 = 1.0 / (1.0 - ADAM_B1 ** ADAM_STEP)
    c2 = 1.0 / (1.0 - ADAM_B2 ** ADAM_STEP)

    def body(g_ref, w_ref, m_ref, v_ref, go_ref, d_ref, mo_ref, vo_ref):
        g = g_ref[0]
        for s in range(1, S):
            g = g + g_ref[s]
        m_new = ADAM_B1 * m_ref[...] + (1.0 - ADAM_B1) * g
        v_new = ADAM_B2 * v_ref[...] + (1.0 - ADAM_B2) * jnp.square(g)
        go_ref[...] = g
        mo_ref[...] = m_new
        vo_ref[...] = v_new
        d_ref[...] = -ADAM_LR * ((m_new * c1) / (jnp.sqrt(v_new * c2) + ADAM_EPS) + ADAM_WD * w_ref[...])

    spec = pl.BlockSpec((tr, N), lambda i: (i, 0))
    return pl.pallas_call(
        body, name=name, grid=(R // tr,),
        in_specs=[pl.BlockSpec((S, tr, N), lambda i: (0, i, 0)), spec, spec, spec], out_specs=[spec] * 4,
        out_shape=[jax.ShapeDtypeStruct((R, N), F32)] * 4, compiler_params=_params(("parallel",)),
    )(gparts, w, m, v)


def _reduce_to_shard(name, slabs, n_split, by_columns=False, overlap=False):
    c, chip = lax.axis_index("c"), 2 * lax.axis_index("x") + lax.axis_index("y")
    if by_columns:
        R, N = slabs.shape[0], SHARD_COLS
        from_sibling = _exchange(name + "_sib", slabs, CORE_MASKS, None, False, n_split, other_half=True)[0]
        wire = _add_slabs_to_bf16(name + "_add", slabs, from_sibling, c)
    else:
        _, R, N = slabs.shape
        half = R // 2
        halves = slabs.reshape(N_CHIPS, 2, half, N)
        mine = lax.dynamic_index_in_dim(halves, c, axis=1, keepdims=False)
        theirs = lax.dynamic_index_in_dim(halves, 1 - c, axis=1, keepdims=False)
        from_sibling = _exchange(name + "_sib", theirs, CORE_MASKS, None, False, N_CHIPS)[0]
        wire = _add_to_bf16(name + "_add", mine, from_sibling)
    if overlap:
        return _exchange_start(name + "_chips", wire, CHIP_MASKS, "chip", True, n_split), (R, N)
    got = _exchange(name + "_chips", wire, CHIP_MASKS, "chip", True, n_split, copy_own=False)
    return _reduce_tail(name, wire, got, n_split, R, N)


def _reduce_tail(name, wire, got, n_split, R, N):
    c, chip = lax.axis_index("c"), 2 * lax.axis_index("x") + lax.axis_index("y")
    got = lax.dynamic_update_index_in_dim(got, lax.dynamic_index_in_dim(wire, chip, 0, keepdims=False), chip, 0)
    part = _sum_slots(name + "_sum", got)
    both = _exchange(name + "_cores", part, CORE_MASKS, "core", False, n_split, copy_own=False)
    return lax.dynamic_update_index_in_dim(both, part, c, 0).reshape(1, R, N)


def _reduce_finish(name, started, after, n_split):
    handle, (R, N) = started
    wire, got = _exchange_wait(name + "_chips_wait", handle, after, CHIP_MASKS, "chip", True, n_split)
    return _reduce_tail(name, wire, got, n_split, R, N)


def _pack(pieces, cols, row_mult=8):
    flat = jnp.concatenate([p.reshape(-1) for p in pieces])
    rows = -(-flat.shape[0] // cols)
    rows = -(-rows // row_mult) * row_mult
    return jnp.pad(flat, (0, rows * cols - flat.shape[0])).reshape(rows, cols)


def _unpack(packed, shapes):
    flat = packed.reshape(-1)
    out, off = [], 0
    for shp in shapes:
        n = 1
        for d in shp:
            n *= d
        out.append(flat[off:off + n].reshape(shp))
        off += n
    return out


def kernel(x, meta_tokens, norm_w, w_in, rw_shift_mu, rw_w0, rw_w2, rw_a0, rw_a2, rw_k_k, rw_k_a, rw_r_k, rw_gn_w, rw_gn_b, dn_conv_w, dn_A_log, dn_dt_bias, dn_norm_w, w_out, final_norm_w, loss_target, m_meta_tokens, m_norm_w, m_w_in, m_rw_shift_mu, m_rw_w0, m_rw_w2, m_rw_a0, m_rw_a2, m_rw_k_k, m_rw_k_a, m_rw_r_k, m_rw_gn_w, m_rw_gn_b, m_dn_conv_w, m_dn_A_log, m_dn_dt_bias, m_dn_norm_w, m_w_out, m_final_norm_w, v_meta_tokens, v_norm_w, v_w_in, v_rw_shift_mu, v_rw_w0, v_rw_w2, v_rw_a0, v_rw_a2, v_rw_k_k, v_rw_k_a, v_rw_r_k, v_rw_gn_w, v_rw_gn_b, v_dn_conv_w, v_dn_A_log, v_dn_dt_bias, v_dn_norm_w, v_w_out, v_final_norm_w):
    S = x.shape[1]
    L = N_META + S
    Lp = -(-L // CHUNK) * CHUNK

    small_shapes = [(RW_LORA, 256), (RW_LORA, 256), (CONV_W, 768), (N_META, 512)]
    small_mine = _pack([rw_w2[0], rw_a2[0], dn_conv_w[0], meta_tokens], 1024)
    small_all = _gather_chips("gather_small", small_mine)
    per_chip = [_unpack(small_all[s], small_shapes) for s in range(N_CHIPS)]
    w2, a2, conv_w, meta = [jnp.concatenate([per_chip[s][i] for s in range(N_CHIPS)], axis=1) for i in range(4)]
    w_in_started = _gather_shards("gather_w_in", w_in[0].astype(BF16), 8, overlap=True, dep=small_all)
    w_out_started = _gather_shards("gather_w_out", w_out[0].astype(BF16), 8, overlap=True, dep=w_in_started[4])

    tail = [jnp.zeros((Lp - L, D_MODEL), F32)] if Lp > L else []
    h = jnp.concatenate([meta + w_in_started[4][:1, :1], x[0]] + tail, axis=0)
    (u,) = _rowwise("rms_in", _rms_fn, [_row(h)], [norm_w], [D_MODEL], tm=2 * ROW_TILE, out_dtype=BF16)
    w_in_all = _gather_finish("gather_w_in", w_in_started, u, 8)
    slabs = [w_in_all[:, s].reshape(D_MODEL, SHARD_COLS) for s in range(N_CHIPS)]
    cut = GAP_AT - (N_CHIPS - 1) * SHARD_COLS
    W_gapped = jnp.concatenate(slabs[:-1] + [slabs[-1][:, :cut], jnp.zeros((D_MODEL, GAP), BF16), slabs[-1][:, cut:]],
                               axis=1)
    p = _mm("in_proj", u, W_gapped, "nn", dep=w_out_started[4])

    mu = rw_shift_mu
    zpad = jnp.zeros((RW_LORA, RW_WIDTH), F32)
    rw_params = [mu[:, 0:1024], mu[:, 1024:2048], mu[:, 2048:3072], mu[:, 3072:3200], rw_w0,
                 jnp.concatenate([w2, zpad], axis=0), rw_a0, jnp.concatenate([zpad, a2], axis=0), rw_k_k, rw_k_a]
    rw_rows = [_row(p, 1024, OFF_R), _row(p, 1024, OFF_K), _row(p, 1024, OFF_V), _row(p, 128, OFF_S1),
               _row(p, 1024, OFF_GATE)]
    rw_pars = rw_params + [rw_r_k, rw_gn_w, rw_gn_b]
    rw_saved_shapes = [(N_GROUPS, CHUNK, GROUP), (N_GROUPS, CHUNK, GROUP), (N_GROUPS, 2 * CHUNK, 2 * GROUP),
                       (N_GROUPS, 2 * CHUNK, GROUP)]

    dn_rows = [_row(p, 1024, OFF_DQ), _row(p, 1024, OFF_DK), _row(p, 1024, OFF_DV), _row(p, 128, OFF_S2),
               _row(p, 1024, OFF_Z)]
    dn_pars = [conv_w[j:j + 1, 1024 * i:1024 * (i + 1)] for i in range(3) for j in range(CONV_W)]
    narrow = lambda t: jnp.pad(t, ((0, 0), (DN_HEADS, DN_HEAD - 2 * DN_HEADS)))
    dn_pars += [narrow(dn_A_log), narrow(dn_dt_bias), jnp.tile(dn_norm_w, (1, DN_HEADS))]
    dn_saved_shapes = [(N_GROUPS, CHUNK, CHUNK), (N_GROUPS, CHUNK, GROUP)]
    mixer_parts = [(_rw_fused_steps, rw_rows, 4, rw_pars, rw_saved_shapes),
                   (_dn_fused_steps, dn_rows, 3, dn_pars, dn_saved_shapes)]
    y, mixer_kept = _mixers_fwd("mixers_fwd", mixer_parts)
    w_out_all = _gather_finish("gather_w_out", w_out_started, y, 8)
    Wo = jnp.concatenate([w_out_all[:, s].reshape(D_MODEL // N_CHIPS, D_MODEL) for s in range(N_CHIPS)], axis=0)
    yw = _mm("out_proj", y, Wo, "nn", tn=1024)
    loss_acc, dz, dz16, d_fw = _loss_head(h, yw, loss_target[0], final_norm_w.reshape(1, D_MODEL))
    loss = lax.psum(loss_acc[0, 0], ("x", "y", "c"))

    d_wo = _mm("d_w_out", y.T, dz16, "nn", tm=512, tn=1024, tk=Lp)
    wo_started = _reduce_to_shard("rs_w_out", d_wo.reshape(N_CHIPS, D_MODEL // N_CHIPS, D_MODEL), 8, overlap=True)
    dy = _mm("d_out_proj", dz16, Wo, "nt", tn=1024, tk=2048, dep=wo_started[0][4])

    dp, (d_rw_pars, d_dn_pars) = _mixers_bwd("mixers_bwd", mixer_parts, mixer_kept, dy)
    d_prep_pars, d_rw_pars = d_rw_pars[:10], d_rw_pars[10:]
    d_conv_parts = [jnp.concatenate(d_dn_pars[4 * i:4 * i + 4], axis=0) for i in range(3)]
    d_a_log_b, d_dt_b = d_dn_pars[12], d_dn_pars[13]
    d_dn_nw = jnp.sum(d_dn_pars[14].reshape(DN_HEADS, DN_HEAD), axis=0, keepdims=True)
    d_W = _mm("d_w_in", u.T, dp, "nn", tm=1024, tn=768, tk=Lp)
    wi_started = _reduce_to_shard("rs_w_in", d_W, 8, by_columns=True, overlap=True)
    du = _mm("d_in_proj", dp, W_gapped, "nt", tn=2048, tk=1408, dep=wi_started[0][4])
    norm_res = lambda h_, w_: (_rms_fn(h_, w_)[0], h_)
    (dx,), (d_norm_w_x,) = _rowwise_bwd("rms_in_bwd", norm_res, [_row(h)], [norm_w], [_row(du), _row(dz)],
                                        tm=2 * GROUP, row0=N_META, n_rows=S)
    (d_meta,), (d_norm_w_meta,) = _rowwise_bwd("rms_meta_bwd", norm_res, [_row(h)], [norm_w], [_row(du), _row(dz)],
                                               n_rows=N_META)
    d_norm_w = d_norm_w_x + d_norm_w_meta
    grad_x = dx[None]

    d_mu = jnp.concatenate(d_prep_pars[0:4], axis=1)
    d_w2, d_a2 = d_prep_pars[5][:RW_LORA], d_prep_pars[7][RW_LORA:]
    d_conv = jnp.concatenate(d_conv_parts, axis=1)
    head_sum = lambda t: t[:, DN_HEADS:2 * DN_HEADS]
    rep_names = ["norm_w", "rw_shift_mu", "rw_w0", "rw_a0", "rw_k_k", "rw_k_a", "rw_r_k", "rw_gn_w", "rw_gn_b",
                 "dn_A_log", "dn_dt_bias", "dn_norm_w", "final_norm_w"]
    rep_g = [d_norm_w, d_mu, d_prep_pars[4], d_prep_pars[6], d_prep_pars[8], d_prep_pars[9],
             d_rw_pars[0], d_rw_pars[1], d_rw_pars[2],
             head_sum(d_a_log_b), head_sum(d_dt_b), d_dn_nw, d_fw.reshape(D_MODEL)]
    rep_w = [norm_w, rw_shift_mu, rw_w0, rw_a0, rw_k_k, rw_k_a, rw_r_k, rw_gn_w, rw_gn_b, dn_A_log, dn_dt_bias,
             dn_norm_w, final_norm_w]
    rep_m = [m_norm_w, m_rw_shift_mu, m_rw_w0, m_rw_a0, m_rw_k_k, m_rw_k_a, m_rw_r_k, m_rw_gn_w, m_rw_gn_b,
             m_dn_A_log, m_dn_dt_bias, m_dn_norm_w, m_final_norm_w]
    rep_v = [v_norm_w, v_rw_shift_mu, v_rw_w0, v_rw_a0, v_rw_k_k, v_rw_k_a, v_rw_r_k, v_rw_gn_w, v_rw_gn_b,
             v_dn_A_log, v_dn_dt_bias, v_dn_norm_w, v_final_norm_w]
    rep_shapes = [t.shape for t in rep_w]
    rep_all = _exchange("gather_rep_grads", _pack(rep_g, 128), ALL_MASKS, "dev", False)
    rep_out = _adamw("adam_rep", rep_all, _pack(rep_w, 128), _pack(rep_m, 128), _pack(rep_v, 128))
    rep_out = [dict(zip(rep_names, _unpack(t, rep_shapes))) for t in rep_out]

    sm_slabs = jnp.stack([_pack([d_w2[:, 256 * s:256 * (s + 1)], d_a2[:, 256 * s:256 * (s + 1)],
                                 d_conv[:, 768 * s:768 * (s + 1)], d_meta[:, 512 * s:512 * (s + 1)]], 1024, 64)
                          for s in range(N_CHIPS)])
    sm_parts = _reduce_to_shard("rs_small", sm_slabs, 1)
    sm_w = [rw_w2[0], rw_a2[0], dn_conv_w[0], meta_tokens]
    sm_m = [m_rw_w2[0], m_rw_a2[0], m_dn_conv_w[0], m_meta_tokens]
    sm_v = [v_rw_w2[0], v_rw_a2[0], v_dn_conv_w[0], v_meta_tokens]
    sm_out = _adamw("adam_small", sm_parts, _pack(sm_w, 1024, 64), _pack(sm_m, 1024, 64), _pack(sm_v, 1024, 64))
    sm_names = ["rw_w2", "rw_a2", "dn_conv_w", "meta_tokens"]
    sm_full_shapes = [(1, RW_LORA, 256), (1, RW_LORA, 256), (1, CONV_W, 768), (N_META, 512)]
    sm_out = [dict(zip(sm_names, [t.reshape(shp) for t, shp in zip(_unpack(o, small_shapes), sm_full_shapes)]))
              for o in sm_out]

    wo_parts = _reduce_finish("rs_w_out", wo_started, dp, 8)
    wo_out = _adamw("adam_w_out", wo_parts, w_out[0], m_w_out[0], v_w_out[0])
    wi_parts = _reduce_finish("rs_w_in", wi_started, dx, 8)
    wi_out = _adamw("adam_w_in", wi_parts, w_in[0], m_w_in[0], v_w_in[0])

    order = ["meta_tokens", "norm_w", "w_in", "rw_shift_mu", "rw_w0", "rw_w2", "rw_a0", "rw_a2", "rw_k_k", "rw_k_a",
             "rw_r_k", "rw_gn_w", "rw_gn_b", "dn_conv_w", "dn_A_log", "dn_dt_bias", "dn_norm_w", "w_out",
             "final_norm_w"]
    outs = [loss, grad_x]
    for kind in range(4):
        table = dict(rep_out[kind])
        table.update(sm_out[kind])
        table["w_in"] = wi_out[kind][None]
        table["w_out"] = wo_out[kind][None]
        outs += [table[n] for n in order]
    return tuple(outs)
```

```python
import functools

import jax
import jax.numpy as jnp
from jax import lax
from jax.experimental import pallas as pl
from jax.experimental.pallas import tpu as pltpu

F32 = jnp.float32
BF16 = jnp.bfloat16
HIGH = lax.Precision.HIGH
HIGHEST = lax.Precision.HIGHEST
MESH = pl.DeviceIdType.MESH

D_MODEL = 2048
N_META = 16
RW_WIDTH = 1024
RW_HEAD = 64
RW_HEADS = 16
RW_LORA = 64
RW_GN_EPS = 64e-5
DN_WIDTH = 1024
DN_HEAD = 128
DN_HEADS = 8
CONV_W = 4
CHUNK = 64
NORM_EPS = 1e-6
IN_COLS = 8336
N_CHIPS = 4
SHARD_COLS = IN_COLS // N_CHIPS

NP_COLS = 8 * 1024 + 256
GAP_AT = IN_COLS - DN_WIDTH
GAP = NP_COLS - IN_COLS
OFF_R, OFF_K, OFF_V, OFF_S1, OFF_GATE = 0, 1024, 2048, 3072, 3200
OFF_DQ, OFF_DK, OFF_DV, OFF_S2, OFF_Z = 4224, 5248, 6272, 7296, 7424

ADAM_LR = 0.001
ADAM_B1 = 0.9
ADAM_B2 = 0.999
ADAM_EPS = 1e-08
ADAM_WD = 0.01
ADAM_STEP = 10

VMEM_LIMIT_BYTES = 56 * 1024 * 1024
ROW_TILE = 104
MM_ROW_TILE = 832


def _params(sem=None):
    return pltpu.CompilerParams(dimension_semantics=sem, vmem_limit_bytes=VMEM_LIMIT_BYTES)


def _pick(n, target, mult=8):
    best = None
    for d in range(mult, min(n, target) + 1, mult):
        if n % d == 0:
            best = d
    return n if best is None else best


def _mm(name, a, b, mode, tm=MM_ROW_TILE, tn=1408, tk=2048, dep=None):
    if mode == "nn":
        (M, K), (_, N) = a.shape, b.shape
    elif mode == "nt":
        (M, K), (N, _) = a.shape, b.shape
    else:
        (K, M), (_, N) = a.shape, b.shape
    tm = _pick(M, tm, 128 if mode == "tn" else 16)
    tn = _pick(N, tn, 128)
    tk = _pick(K, tk, 8 if mode == "tn" else 128)
    if mode == "nn":
        a_spec = pl.BlockSpec((tm, tk), lambda i, j, k: (i, k))
        b_spec = pl.BlockSpec((tk, tn), lambda i, j, k: (k, j))
        dims = (((1,), (0,)), ((), ()))
    elif mode == "nt":
        a_spec = pl.BlockSpec((tm, tk), lambda i, j, k: (i, k))
        b_spec = pl.BlockSpec((tn, tk), lambda i, j, k: (j, k))
        dims = (((1,), (1,)), ((), ()))
    else:
        a_spec = pl.BlockSpec((tk, tm), lambda i, j, k: (k, i))
        b_spec = pl.BlockSpec((tk, tn), lambda i, j, k: (k, j))
        dims = (((0,), (0,)), ((), ()))

    def body(a_ref, b_ref, *rest):
        o_ref = rest[-1]

        @pl.when(pl.program_id(2) == 0)
        def _():
            o_ref[...] = jnp.zeros_like(o_ref)

        o_ref[...] += lax.dot_general(a_ref[...].astype(BF16), b_ref[...].astype(BF16), dims,
                                      preferred_element_type=F32)

    deps = [] if dep is None else [dep]
    return pl.pallas_call(
        body, name=name, grid=(M // tm, N // tn, K // tk),
        in_specs=[a_spec, b_spec] + [pl.BlockSpec((8, 128), lambda i, j, k: (0, 0))] * len(deps),
        out_specs=pl.BlockSpec((tm, tn), lambda i, j, k: (i, j)),
        out_shape=jax.ShapeDtypeStruct((M, N), F32),
        compiler_params=_params(("parallel", "parallel", "arbitrary")),
    )(a, b, *deps)


def _row(arr, width=None, cb=0):
    return (arr, arr.shape[1] if width is None else width, cb)


def _rowwise(name, fn, rows, params, out_widths, tm=ROW_TILE, out_dtype=F32):
    R = rows[0][0].shape[0]
    tm = _pick(R, tm, 16 if out_dtype == BF16 else 8)
    n_r, n_p = len(rows), len(params)

    def body(*refs):
        vals = [r[...] for r in refs[:n_r + n_p]]
        for o_ref, val in zip(refs[n_r + n_p:], fn(*vals)):
            o_ref[...] = val.astype(out_dtype)

    in_specs = [pl.BlockSpec((tm, w), lambda i, cb=cb: (i, cb)) for (_, w, cb) in rows]
    in_specs += [pl.BlockSpec(p.shape, lambda i: (0, 0)) for p in params]
    return pl.pallas_call(
        body, name=name, grid=(R // tm,), in_specs=in_specs,
        out_specs=[pl.BlockSpec((tm, w), lambda i: (i, 0)) for w in out_widths],
        out_shape=[jax.ShapeDtypeStruct((R, w), out_dtype) for w in out_widths],
        compiler_params=_params(("parallel",)),
    )(*[r[0] for r in rows], *params)


def _rowwise_bwd(name, fn, rows, params, douts, tm=ROW_TILE, row0=0, n_rows=None):
    R = rows[0][0].shape[0] - row0 if n_rows is None else n_rows
    tm = _pick(R, tm)
    assert row0 % 8 == 0 and row0 + R <= rows[0][0].shape[0]

    def in_spec(w, cb):
        if row0 == 0:
            return pl.BlockSpec((tm, w), lambda i: (i, cb))
        return pl.BlockSpec((pl.Element(tm), pl.Element(w)), lambda i: (pl.multiple_of(row0 + i * tm, 8), cb * w))

    n_r, n_p, n_d = len(rows), len(params), len(douts)

    def body(*refs):
        vals = [r[...] for r in refs[:n_r + n_p]]
        cts = tuple(r[...] for r in refs[n_r + n_p:n_r + n_p + n_d])
        grads = jax.vjp(fn, *vals)[1](cts)
        outs = refs[n_r + n_p + n_d:]
        for o_ref, g in zip(outs[:n_r], grads[:n_r]):
            o_ref[...] = g

        @pl.when(pl.program_id(0) == 0)
        def _():
            for o_ref in outs[n_r:]:
                o_ref[...] = jnp.zeros_like(o_ref)

        for o_ref, g in zip(outs[n_r:], grads[n_r:]):
            o_ref[...] += g

    in_specs = [in_spec(w, cb) for (_, w, cb) in rows]
    in_specs += [pl.BlockSpec(p.shape, lambda i: (0, 0)) for p in params]
    in_specs += [in_spec(w, cb) for (_, w, cb) in douts]
    out_specs = [pl.BlockSpec((tm, w), lambda i: (i, 0)) for (_, w, _) in rows]
    out_specs += [pl.BlockSpec(p.shape, lambda i: (0, 0)) for p in params]
    out_shape = [jax.ShapeDtypeStruct((R, w), F32) for (_, w, _) in rows]
    out_shape += [jax.ShapeDtypeStruct(p.shape, F32) for p in params]
    res = pl.pallas_call(
        body, name=name, grid=(R // tm,), in_specs=in_specs, out_specs=out_specs, out_shape=out_shape,
        compiler_params=_params(("arbitrary",)),
    )(*[r[0] for r in rows], *params, *[d[0] for d in douts])
    return res[:n_r], res[n_r:]


def _softplus(x):
    return jnp.maximum(x, 0.0) + jnp.log(1.0 + jnp.exp(-jnp.abs(x)))


def _silu(x):
    return x * jax.nn.sigmoid(x)


def _rms_fn(h, w):
    return (h * lax.rsqrt(jnp.mean(h * h, axis=-1, keepdims=True) + NORM_EPS) * w,)


def _rw_prep_fn(pr, pr1, pk, pk1, pv, pv1, ps, ps1, mu_r, mu_k, mu_v, mu_s, w0, w2p, a0, a2p, k_k, k_a):
    r = pr + (pr1 - pr) * mu_r
    k = pk + (pk1 - pk) * mu_k
    v = pv + (pv1 - pv) * mu_v
    s = ps + (ps1 - ps) * mu_s
    lora = lambda x, w: jnp.dot(x.astype(BF16), w.astype(BF16), preferred_element_type=F32)
    w_log = -_softplus(-(w0 + lora(jnp.tanh(s), w2p))) - 0.5
    log_decay = -jnp.exp(w_log)
    a = jax.nn.sigmoid(a0 + lora(s, a2p))
    return r, log_decay, k * (1.0 + (a - 1.0) * k_a), v, k * k_k, a


def _conv_fn(u0, u1, u2, u3, w0, w1, w2, w3):
    return (_silu(u0 * w3 + u1 * w2 + u2 * w1 + u3 * w0),)


def _dn_gate_fn(ps2, a_log_n, dt_n):
    beta_n = jax.nn.sigmoid(ps2)
    g_n = -jnp.exp(a_log_n) * _softplus(ps2 + dt_n)
    lane = lax.broadcasted_iota(jnp.int32, (1, DN_HEAD), 1)
    pick = lambda x, j: jnp.broadcast_to(jnp.sum(x * (lane == j).astype(F32), axis=-1, keepdims=True), x.shape)[None]
    beta = jnp.concatenate([pick(beta_n, h) for h in range(DN_HEADS)], axis=0)
    g = jnp.concatenate([pick(g_n, DN_HEADS + h) for h in range(DN_HEADS)], axis=0)
    return beta, g


def _tril_masks(c):
    t = lax.broadcasted_iota(jnp.int32, (c, c), 0)
    s = lax.broadcasted_iota(jnp.int32, (c, c), 1)
    return s <= t, s < t


def _dot(x, y, cx, cy, prec=HIGH):
    nb = x.ndim - 2
    batch = tuple(range(nb))
    return lax.dot_general(x, y, (((cx + nb,), (cy + nb,)), (batch, batch)), precision=prec,
                           preferred_element_type=F32)


INV_BASE = 8


def _run_interleaved(running):
    results, live = [None] * len(running), list(range(len(running)))
    while live:
        for k in list(live):
            try:
                next(running[k])
            except StopIteration as done:
                results[k] = done.value
                live.remove(k)
    return results


def _blocked_inv(a, mm, row, col, mm_merge=None):
    mm_merge = mm if mm_merge is None else mm_merge
    shift = INV_BASE.bit_length() - 1
    d = jnp.where(jnp.right_shift(row, shift) == jnp.right_shift(col, shift), a, 0.0)
    x = (row == col).astype(F32) - d
    p = mm(d, d)
    yield
    x = x + mm(x, p)
    yield
    p = mm(p, p)
    yield
    x = x + mm(x, p)
    yield
    size = INV_BASE
    while size < a.shape[-2]:
        same_pair = jnp.right_shift(row, shift + 1) == jnp.right_shift(col, shift + 1)
        lower_left = (jnp.right_shift(row, shift) & 1 == 1) & (jnp.right_shift(col, shift) & 1 == 0)
        lower = mm_merge(jnp.where(same_pair & lower_left, a, 0.0), x)
        yield
        x = x - mm_merge(x, lower)
        size, shift = 2 * size, shift + 1
        yield
    return x


def _tri_inv(a):
    c = a.shape[-1]
    row, col = lax.broadcasted_iota(jnp.int32, (c, c), 0), lax.broadcasted_iota(jnp.int32, (c, c), 1)
    return _blocked_inv(a, lambda x, y: _dot(x, y, 1, 0), row, col,
                        lambda x, y: _dot(x.astype(BF16), y.astype(BF16), 1, 0, None))


@jax.custom_vjp
def _tri_inv_saved(a, t):
    return t


def _tri_inv_saved_fwd(a, t):
    return t, t


def _tri_inv_saved_bwd(t, dt):
    return -_dot(_dot(t, dt, 0, 0), t, 1, 1), jnp.zeros_like(t)


_tri_inv_saved.defvjp(_tri_inv_saved_fwd, _tri_inv_saved_bwd)


def _pair_masks():
    lane = lax.broadcasted_iota(jnp.int32, (1, 2 * RW_HEAD), 1)
    m0 = (lane < RW_HEAD).astype(F32)
    return m0, 1.0 - m0


def _pair_bd(x):
    m0, m1 = _pair_masks()
    return jnp.concatenate([x * m0, x * m1], axis=-2)


def _pair_mm(x, y):
    return _dot(x, _pair_bd(y), 1, 0)


def _pair_inv(a):
    c = a.shape[-2]
    row = lax.broadcasted_iota(jnp.int32, (c, 2 * RW_HEAD), 0)
    col = lax.broadcasted_iota(jnp.int32, (c, 2 * RW_HEAD), 1) & (RW_HEAD - 1)
    return _blocked_inv(a, _pair_mm, row, col,
                        lambda x, y: _dot(x.astype(BF16), _pair_bd(y).astype(BF16), 1, 0, None))


@jax.custom_vjp
def _pair_inv_saved(a, t):
    return t


def _pair_inv_saved_fwd(a, t):
    return t, t


def _pair_inv_saved_bwd(t, dt):
    m0, m1 = _pair_masks()
    c = t.shape[-2]
    z = _dot(t, dt, 0, 0)
    x = z[:, :c, :] * m0 + z[:, c:, :] * m1
    return -_dot(x, _pair_bd(t), 1, 1), jnp.zeros_like(t)


_pair_inv_saved.defvjp(_pair_inv_saved_fwd, _pair_inv_saved_bwd)


@jax.custom_vjp
def _use_saved(x, x_saved):
    return x_saved


_use_saved.defvjp(lambda x, x_saved: (x_saved, None), lambda _, g: (g, jnp.zeros_like(g)))


def _rw_chunk_steps(S, r, lw, k2, v, kkp, a, gate, rk, gnw, gnb, saved=None):
    B, C, P = r.shape
    m0, m1 = _pair_masks()
    seg = lambda x: (jnp.sum(x * m0, axis=-1, keepdims=True) * m0 + jnp.sum(x * m1, axis=-1, keepdims=True) * m1)
    mm = lambda x, y: _dot(x.astype(BF16), y.astype(BF16), 1, 0, None)
    nt = lambda x, y: _dot(x, y, 1, 1)
    tn = lambda x, y: _dot(x, y, 0, 0)
    pair_mm = lambda x, y: mm(x, _pair_bd(y))
    t_idx = lax.broadcasted_iota(jnp.int32, (C, P), 0)
    s_idx = lax.broadcasted_iota(jnp.int32, (C, P), 1) & (RW_HEAD - 1)
    incl, strict = s_idx <= t_idx, s_idx < t_idx
    tril = jnp.broadcast_to(_tril_masks(C)[0].astype(F32), (B, C, C))
    kk = kkp * lax.rsqrt(seg(kkp * kkp) + 1e-6)
    b = kk * a
    reuse = (lambda x, i: x) if saved is None else (lambda x, i: _use_saved(x, saved[i]))
    g_incl = reuse(_dot(tril, lw, 1, 0), 1)
    g_excl = g_incl - lw
    inv = jnp.exp(-g_incl)
    alpha, beta, kappa, rho = kk * jnp.exp(g_excl), b * inv, k2 * inv, r * jnp.exp(g_incl)
    yield
    ar = jnp.concatenate([alpha, rho], axis=-2)
    scores = reuse(nt(ar, jnp.concatenate([_pair_bd(beta), _pair_bd(kappa)], axis=-2)), 2)
    a_ab = jnp.where(strict, scores[:, :C, :P], 0.0)
    a_ak = jnp.where(strict, scores[:, :C, P:], 0.0)
    r_b = jnp.where(incl, scores[:, C:, :P], 0.0)
    r_k = jnp.where(incl, scores[:, C:, P:], 0.0)
    yield
    t_inv = (yield from _pair_inv(a_ab)) if saved is None else _pair_inv_saved(a_ab, saved[0])
    on_state = reuse(nt(ar, S), 3)
    yield
    u = pair_mm(t_inv, -on_state[:, :C, :] - pair_mm(a_ak, v))
    yield
    y = on_state[:, C:, :] + mm(jnp.concatenate([r_b, r_k], axis=-1),
                                jnp.concatenate([_pair_bd(u), _pair_bd(v)], axis=-2))
    same_head = ((lax.broadcasted_iota(jnp.int32, (P, P), 0) < RW_HEAD)
                 == (lax.broadcasted_iota(jnp.int32, (P, P), 1) < RW_HEAD))
    yield
    fresh = tn(jnp.concatenate([u, v], axis=-2), jnp.concatenate([beta, kappa], axis=-2))
    yield
    S_new = jnp.exp(jnp.sum(lw, axis=-2, keepdims=True)) * (S + jnp.where(same_head, fresh, 0.0))
    dev = y - seg(y) * (1.0 / RW_HEAD)
    yn = dev * lax.rsqrt(seg(dev * dev) * (1.0 / RW_HEAD) + RW_GN_EPS) * gnw + gnb
    bonus = seg(r * k2 * rk) * v
    return (yn + bonus) * _silu(gate), S_new, (t_inv, g_incl, scores, on_state)


def _dn_chunk_steps(S, qc, kc, vc, bb, gb, z, nw, saved=None):
    B, C, D = qc.shape
    mm = lambda x, y: _dot(x.astype(BF16), y.astype(BF16), 1, 0, None)
    nt = lambda x, y, p=None: _dot(x, y, 1, 1, p) if p else _dot(x.astype(BF16), y.astype(BF16), 1, 1, None)
    tn = lambda x, y: _dot(x.astype(BF16), y.astype(BF16), 0, 0, None)
    incl, strict = _tril_masks(C)
    q = qc * lax.rsqrt(jnp.sum(qc * qc, axis=-1, keepdims=True) + 1e-6) * (D ** -0.5)
    k = kc * lax.rsqrt(jnp.sum(kc * kc, axis=-1, keepdims=True) + 1e-6)
    kb, vb = k * bb, vc * bb
    yield
    G = _dot(jnp.broadcast_to(incl.astype(F32), (B, C, C)), gb, 1, 0, HIGHEST)
    G = G if saved is None else _use_saved(G, saved[1])
    lane = lax.broadcasted_iota(jnp.int32, (C, D), 1)
    e0, e1 = (lane == 0).astype(F32), (lane == 1).astype(F32)
    diff = nt(G * e0 + e1, e0 - G * e1, HIGHEST)
    dmask = jnp.where(incl, jnp.exp(jnp.where(incl, diff, 0.0)), 0.0)
    M = jnp.where(strict, nt(kb, k) * dmask, 0.0)
    yield
    T = (yield from _tri_inv(M)) if saved is None else _tri_inv_saved(M, saved[0])
    eG = jnp.exp(G)
    u = mm(T, vb)
    yield
    w = mm(T, kb * eG)
    yield
    attn = jnp.where(incl, nt(q, k) * dmask, 0.0)
    yield
    v_new = u - mm(w, S)
    yield
    o = mm(q * eG, S) + mm(attn, v_new)
    yield
    g_last = jnp.sum(gb, axis=-2, keepdims=True)
    S_new = S * jnp.exp(jnp.broadcast_to(g_last, S.shape)) + tn(k * jnp.exp(g_last - G), v_new)
    on = o * lax.rsqrt(jnp.mean(o * o, axis=-1, keepdims=True) + NORM_EPS) * nw
    return on * _silu(z), S_new, (T, G)


N_GROUPS = 8
GROUP = 128
HALO = 8


def _groups(x):
    return jnp.concatenate([x[:, g * GROUP:(g + 1) * GROUP][None] for g in range(N_GROUPS)], axis=0)


@functools.partial(jax.custom_vjp, nondiff_argnums=(1,))
def _shift_rows(ext, j):
    return pltpu.roll(ext, j, 0)[HALO:, :]


def _shift_rows_fwd(ext, j):
    return _shift_rows(ext, j), None


def _shift_rows_bwd(j, _, d):
    z = jnp.concatenate([jnp.zeros((HALO, d.shape[1]), d.dtype), d], axis=0)
    return (pltpu.roll(z, z.shape[0] - j, 0),)


_shift_rows.defvjp(_shift_rows_fwd, _shift_rows_bwd)


def _rw_fused_steps(S, h_r, h_k, h_v, h_s, p_r, p_k, p_v, p_s, gate, *pars, saved=None):
    prev = lambda h, x: _shift_rows(jnp.concatenate([h, x], axis=0), 1)
    seq = _rw_prep_fn(p_r, prev(h_r, p_r), p_k, prev(h_k, p_k), p_v, prev(h_v, p_v), p_s, prev(h_s, p_s), *pars[:10])
    yield
    return (yield from _rw_chunk_steps(S, *[_groups(t) for t in seq], _groups(gate),
                                       *[_groups(t) for t in pars[10:]], saved=saved))


def _dn_fused_steps(S, h_q, h_k, h_v, p_q, p_k, p_v, p_s2, z, *pars, saved=None):
    conv = []
    for i, (h, x) in enumerate(((h_q, p_q), (h_k, p_k), (h_v, p_v))):
        ext = jnp.concatenate([h, x], axis=0)
        conv += _conv_fn(x, _shift_rows(ext, 1), _shift_rows(ext, 2), _shift_rows(ext, 3), *pars[4 * i:4 * i + 4])
    beta, g = _dn_gate_fn(p_s2, pars[12], pars[13])
    yield
    return (yield from _dn_chunk_steps(S, *[_groups(t) for t in conv], beta, g, _groups(z), _groups(pars[14]),
                                       saved=saved))


def _row_specs(rows, n_halo, row_of):
    C = CHUNK
    specs = [pl.BlockSpec((pl.Element(C), pl.Element(w)), lambda c, off=off: (row_of(c) * C, off))
             for (_, w, off) in rows]
    specs += [pl.BlockSpec((pl.Element(HALO), pl.Element(w)),
                           lambda c, off=off: (pl.multiple_of(jnp.maximum(row_of(c) * C - HALO, 0), HALO), off))
              for (_, w, off) in rows[:n_halo]]
    return specs


def _mixers_fwd(name, parts):
    Lp = parts[0][1][0][0].shape[0]
    C, D, NB = CHUNK, GROUP, N_GROUPS
    nc, width = Lp // C, NB * D
    n_in = [len(rows) + n_halo + len(pars) for (_, rows, n_halo, pars, _) in parts]
    n_out = [1 + len(shapes) for (*_, shapes) in parts]

    def body(*refs):
        ins, y_ref = refs[:sum(n_in)], refs[sum(n_in)]
        outs, states = refs[sum(n_in) + 1:sum(n_in) + 1 + sum(n_out)], refs[sum(n_in) + 1 + sum(n_out):]
        first = pl.program_id(0) == 0

        @pl.when(first)
        def _():
            for s_ref in states:
                s_ref[...] = jnp.zeros_like(s_ref)

        i_at, running = 0, []
        for k, (steps, rows, n_halo, pars, shapes) in enumerate(parts):
            n_r = len(rows)
            row_refs, halo_refs = ins[i_at:i_at + n_r], ins[i_at + n_r:i_at + n_r + n_halo]
            par_refs = ins[i_at + n_r + n_halo:i_at + n_in[k]]
            S = states[k][...]
            halos = [jnp.where(first, 0.0, r[...]) for r in halo_refs]
            running.append(steps(S, *halos, *[r[...] for r in row_refs], *[r[...] for r in par_refs]))
            i_at += n_in[k]
        results = _run_interleaved(running)
        o_at = 0
        for k, (y, S_new, saved) in enumerate(results):
            ck_ref, saved_refs = outs[o_at], outs[o_at + 1:o_at + n_out[k]]
            ck_ref[0] = states[k][...]
            for g in range(NB):
                y_ref[:, k * width + g * D:k * width + (g + 1) * D] = y[g].astype(BF16)
            for o_ref, val in zip(saved_refs, saved):
                o_ref[0] = val
            states[k][...] = S_new
            o_at += n_out[k]

    in_specs, args, out_specs, out_shape = [], [], [], []
    for (_, rows, n_halo, pars, shapes) in parts:
        in_specs += _row_specs(rows, n_halo, lambda c: c) + [pl.BlockSpec(p.shape, lambda c: (0, 0)) for p in pars]
        args += [r[0] for r in rows] + [r[0] for r in rows[:n_halo]] + list(pars)
        for shp in [(NB, D, D)] + list(shapes):
            out_specs.append(pl.BlockSpec((1,) + tuple(shp), lambda c: (c, 0, 0, 0)))
            out_shape.append(jax.ShapeDtypeStruct((nc,) + tuple(shp), F32))
    res = pl.pallas_call(
        body, name=name, grid=(nc,), in_specs=in_specs,
        out_specs=[pl.BlockSpec((C, width * len(parts)), lambda c: (c, 0))] + out_specs,
        out_shape=[jax.ShapeDtypeStruct((Lp, width * len(parts)), BF16)] + out_shape,
        scratch_shapes=[pltpu.VMEM((NB, D, D), F32)] * len(parts),
        compiler_params=_params(("arbitrary",)),
    )(*args)
    per_part, at = [], 1
    for n in n_out:
        per_part.append(list(res[at:at + n]))
        at += n
    return res[0], per_part


def _mixers_bwd(name, parts, kept, dy):
    Lp = parts[0][1][0][0].shape[0]
    C, D, NB = CHUNK, GROUP, N_GROUPS
    nc = Lp // C
    n_in = [len(rows) + n_halo + len(pars) + len(kept[k]) for k, (_, rows, n_halo, pars, _) in enumerate(parts)]
    widths = [sum(w for (_, w, _) in rows) for (_, rows, *_) in parts]
    n_par = [len(pars) for (_, _, _, pars, _) in parts]
    n_scr = [1 + n_halo for (_, _, n_halo, _, _) in parts]
    rev = lambda c: nc - 1 - c

    def body(*refs):
        ins, dy_ref = refs[:sum(n_in)], refs[sum(n_in)]
        drows_ref = refs[sum(n_in) + 1]
        dpar_all = refs[sum(n_in) + 2:sum(n_in) + 2 + sum(n_par)]
        scratch = refs[sum(n_in) + 2 + sum(n_par):]
        i = pl.program_id(0)

        @pl.when(i == 0)
        def _():
            for o_ref in list(dpar_all) + list(scratch):
                o_ref[...] = jnp.zeros_like(o_ref)

        i_at, vals, stored, cts, sizes = 0, [], [], [], []
        for k, (_, rows, n_halo, pars, _) in enumerate(parts):
            n_r = len(rows)
            row_refs, halo_refs = ins[i_at:i_at + n_r], ins[i_at + n_r:i_at + n_r + n_halo]
            par_refs = ins[i_at + n_r + n_halo:i_at + n_r + n_halo + n_par[k]]
            kept_refs = ins[i_at + n_r + n_halo + n_par[k]:i_at + n_in[k]]
            halos = [jnp.where(i == nc - 1, 0.0, r[...]) for r in halo_refs]
            part_vals = [kept_refs[0][0]] + halos + [r[...] for r in row_refs] + [r[...] for r in par_refs]
            vals += part_vals
            sizes.append(len(part_vals))
            stored.append(tuple(r[0] for r in kept_refs[1:]))
            cts.append((_groups(dy_ref[:, k * NB * D:(k + 1) * NB * D]), scratch[sum(n_scr[:k])][...]))
            i_at += n_in[k]

        def both(*flat):
            running, at = [], 0
            for k, (steps, *_) in enumerate(parts):
                running.append(steps(*flat[at:at + sizes[k]], saved=stored[k]))
                at += sizes[k]
            return tuple(r[:2] for r in _run_interleaved(running))

        all_grads = jax.vjp(both, *vals)[1](tuple(cts))
        i_at = p_at = s_at = g_at = col = 0
        for k, (_, rows, n_halo, pars, _) in enumerate(parts):
            n_r = len(rows)
            grads = all_grads[g_at:g_at + sizes[k]]
            dpar_refs = dpar_all[p_at:p_at + n_par[k]]
            ds_ref, carry_refs = scratch[s_at], scratch[s_at + 1:s_at + n_scr[k]]
            ds_ref[...] = grads[0]
            d_halos, d_rows = grads[1:1 + n_halo], grads[1 + n_halo:1 + n_halo + n_r]
            for j, g in enumerate(d_rows):
                if j < n_halo:
                    g = g + jnp.concatenate([jnp.zeros((C - HALO, g.shape[1]), F32), carry_refs[j][...]], axis=0)
                    carry_refs[j][...] = d_halos[j]
                drows_ref[:, col:col + g.shape[1]] = g.astype(BF16)
                col += g.shape[1]
            for o_ref, g in zip(dpar_refs, grads[1 + n_halo + n_r:]):
                o_ref[...] += g
            p_at, s_at, g_at = p_at + n_par[k], s_at + n_scr[k], g_at + sizes[k]

    in_specs, args, out_specs, out_shape, scratch_shapes = [], [], [], [], []
    for k, (_, rows, n_halo, pars, _) in enumerate(parts):
        in_specs += _row_specs(rows, n_halo, rev) + [pl.BlockSpec(p.shape, lambda c: (0, 0)) for p in pars]
        in_specs += [pl.BlockSpec((1,) + tuple(t.shape[1:]), lambda c: (rev(c), 0, 0, 0)) for t in kept[k]]
        args += [r[0] for r in rows] + [r[0] for r in rows[:n_halo]] + list(pars) + list(kept[k])
        out_specs += [pl.BlockSpec(p.shape, lambda c: (0, 0)) for p in pars]
        out_shape += [jax.ShapeDtypeStruct(p.shape, F32) for p in pars]
        scratch_shapes += [pltpu.VMEM((NB, D, D), F32)] + [pltpu.VMEM((HALO, w), F32) for (_, w, _) in rows[:n_halo]]
    res = pl.pallas_call(
        body, name=name, grid=(nc,),
        in_specs=in_specs + [pl.BlockSpec((C, dy.shape[1]), lambda c: (rev(c), 0))],
        out_specs=[pl.BlockSpec((C, sum(widths)), lambda c: (rev(c), 0))] + out_specs,
        out_shape=[jax.ShapeDtypeStruct((Lp, sum(widths)), BF16)] + out_shape,
        scratch_shapes=scratch_shapes, compiler_params=_params(("arbitrary",)),
    )(*args, dy)
    d_pars, at = [], 1
    for n in n_par:
        d_pars.append(list(res[at:at + n]))
        at += n
    return res[0], d_pars


def _loss_head(h, yw, target, fw, tm=2 * ROW_TILE):
    Lp, Dm = h.shape
    S = target.shape[0]
    n_real = N_META + S
    tm = _pick(Lp, tm, 16)
    n_tiles = Lp // tm
    assert n_tiles >= 2 and N_META < tm <= S and Lp - n_real < tm and (S - tm) % 8 == 0 and N_META % 8 == 0

    def out_fn(z, fw_):
        return z * lax.rsqrt(jnp.mean(z * z, axis=-1, keepdims=True) + NORM_EPS) * fw_

    def body(h_ref, yw_ref, t_ref, fw_ref, loss_ref, dz_ref, dz16_ref, dfw_ref):
        i = pl.program_id(0)

        @pl.when(i == 0)
        def _():
            loss_ref[...] = jnp.zeros_like(loss_ref)
            dfw_ref[...] = jnp.zeros_like(dfw_ref)

        row = i * tm + lax.broadcasted_iota(jnp.int32, (tm, 1), 0)
        mask = ((row >= N_META) & (row < n_real)).astype(F32)
        z = h_ref[...] + yw_ref[...]
        o, vjp = jax.vjp(out_fn, z, fw_ref[...])
        t = t_ref[...]
        t = jnp.where(i == 0, pltpu.roll(t, N_META, 0), t)
        if Lp > n_real:
            t = jnp.where(i == n_tiles - 1, pltpu.roll(t, tm - (Lp - n_real), 0), t)
        err = (o - t) * mask
        row_loss = 0.5 * jnp.mean(jnp.square(err), axis=-1, keepdims=True)
        dz, dfw = vjp(err * (1.0 / Dm))
        loss_ref[...] += jnp.sum(row_loss, axis=0, keepdims=True)
        dz_ref[...] = dz
        dz16_ref[...] = dz.astype(BF16)
        dfw_ref[...] += dfw

    row_spec = pl.BlockSpec((tm, Dm), lambda i: (i, 0))
    target_spec = pl.BlockSpec((pl.Element(tm), pl.Element(Dm)),
                               lambda i: (pl.multiple_of(jnp.clip(i * tm - N_META, 0, S - tm), 8), 0))
    return pl.pallas_call(
        body, name="loss_head", grid=(n_tiles,),
        in_specs=[row_spec, row_spec, target_spec, pl.BlockSpec((1, Dm), lambda i: (0, 0))],
        out_specs=[pl.BlockSpec((8, 128), lambda i: (0, 0)), row_spec, row_spec,
                   pl.BlockSpec((1, Dm), lambda i: (0, 0))],
        out_shape=[jax.ShapeDtypeStruct((8, 128), F32), jax.ShapeDtypeStruct((Lp, Dm), F32),
                   jax.ShapeDtypeStruct((Lp, Dm), BF16), jax.ShapeDtypeStruct((1, Dm), F32)],
        compiler_params=_params(("arbitrary",)),
    )(h, yw, target, fw)


def _exchange(name, x, masks, slot_kind, per_dest, n_split=1, copy_own=True, other_half=False):
    n = len(masks)
    keep_own = slot_kind is not None and copy_own
    n_slots = {"chip": 4, "core": 2, "dev": 8, None: n}[slot_kind]
    blk_shape = x.shape[1:] if per_dest else x.shape
    if other_half:
        blk_shape = (x.shape[0] // 2,) + tuple(x.shape[1:])
    rows = blk_shape[0] // n_split

    def body(x_ref, o_ref, send_sems, recv_sems, local_sems):
        mx, my, mc = lax.axis_index("x"), lax.axis_index("y"), lax.axis_index("c")
        if other_half:
            x_ref = x_ref.at[pl.ds((1 - mc) * blk_shape[0], blk_shape[0])]

        def slot(k, px, py, pc):
            return {"chip": 2 * px + py, "core": pc, "dev": 4 * px + 2 * py + pc, None: k}[slot_kind]

        def peer(m):
            return (mx + m[0]) % 2, (my + m[1]) % 2, (mc + m[2]) % 2

        def part(ref, j):
            return ref.at[pl.ds(j * rows, rows)]

        own_src = x_ref.at[2 * mx + my] if per_dest else x_ref
        local = []
        if keep_own:
            own_dst = o_ref.at[slot(0, mx, my, mc)]
            local = [pltpu.make_async_copy(part(own_src, j), part(own_dst, j), local_sems.at[j])
                     for j in range(n_split)]
        for cp in local:
            cp.start()
        sends = []
        for k, m in enumerate(masks):
            px, py, pc = peer(m)
            src = x_ref.at[2 * px + py] if per_dest else x_ref
            dst = o_ref.at[slot(k, mx, my, mc)]
            for j in range(n_split):
                sends.append(pltpu.make_async_remote_copy(
                    src_ref=part(src, j), dst_ref=part(dst, j), send_sem=send_sems.at[k * n_split + j],
                    recv_sem=recv_sems.at[k * n_split + j], device_id=(px, py, pc), device_id_type=MESH))
        for cp in sends:
            cp.start()
        for k, m in enumerate(masks):
            px, py, pc = peer(m)
            landed = o_ref.at[slot(k, px, py, pc)]
            for j in range(n_split):
                pltpu.make_async_remote_copy(
                    src_ref=part(own_src, j), dst_ref=part(landed, j), send_sem=send_sems.at[k * n_split + j],
                    recv_sem=recv_sems.at[k * n_split + j], device_id=(px, py, pc), device_id_type=MESH).wait_recv()
        for cp in sends:
            cp.wait_send()
        for cp in local:
            cp.wait()

    return pl.pallas_call(
        body, name=name,
        in_specs=[pl.BlockSpec(memory_space=pl.ANY)], out_specs=pl.BlockSpec(memory_space=pl.ANY),
        out_shape=jax.ShapeDtypeStruct((n_slots,) + tuple(blk_shape), x.dtype),
        scratch_shapes=[pltpu.SemaphoreType.DMA((n * n_split,)), pltpu.SemaphoreType.DMA((n * n_split,)),
                        pltpu.SemaphoreType.DMA((n_split,))],
        compiler_params=pltpu.CompilerParams(has_side_effects=True),
    )(x)


CHIP_MASKS = [(1, 0, 0), (0, 1, 0), (1, 1, 0)]
CORE_MASKS = [(0, 0, 1)]
ALL_MASKS = [(dx, dy, dc) for dx in (0, 1) for dy in (0, 1) for dc in (0, 1) if (dx, dy, dc) != (0, 0, 0)]

HBM_SPEC = pl.BlockSpec(memory_space=pltpu.HBM)
SEM_SPEC = pl.BlockSpec(memory_space=pltpu.SEMAPHORE)
DATAFLOW = pltpu.SideEffectType.DATAFLOW_SIDE_EFFECTING


def _split_copies(x_ref, land_ref, send_sems, recv_sems, masks, slot_kind, per_dest, n_split, with_recvs=True):
    mx, my, mc = lax.axis_index("x"), lax.axis_index("y"), lax.axis_index("c")
    slot = lambda px, py, pc: {"chip": 2 * px + py, "core": pc}[slot_kind]
    rows = (x_ref.shape[1] if per_dest else x_ref.shape[0]) // n_split
    part = lambda ref, j: ref.at[pl.ds(j * rows, rows)]
    sends, recvs = [], []
    for k, m in enumerate(masks):
        px, py, pc = (mx + m[0]) % 2, (my + m[1]) % 2, (mc + m[2]) % 2
        src = x_ref.at[2 * px + py] if per_dest else x_ref
        own_src = x_ref.at[2 * mx + my] if per_dest else x_ref
        for j in range(n_split):
            sems = dict(send_sem=send_sems.at[k * n_split + j], recv_sem=recv_sems.at[k * n_split + j],
                        device_id=(px, py, pc), device_id_type=MESH)
            sends.append(pltpu.make_async_remote_copy(
                src_ref=part(src, j), dst_ref=part(land_ref.at[slot(mx, my, mc)], j), **sems))
            if with_recvs:
                recvs.append(pltpu.make_async_remote_copy(
                    src_ref=part(own_src, j), dst_ref=part(land_ref.at[slot(px, py, pc)], j), **sems))
    return sends, recvs


def _exchange_start(name, x, masks, slot_kind, per_dest, n_split, dep=None):
    n = len(masks) * n_split
    blk_shape = x.shape[1:] if per_dest else x.shape
    land_shape = ({"chip": 4, "core": 2}[slot_kind],) + tuple(blk_shape)
    deps = [] if dep is None else [dep]

    def body(x_ref, land_ref, *rest):
        send_sems, recv_sems, x_thru, land_thru, token = rest[len(deps):]
        for cp in _split_copies(x_ref, land_ref, send_sems, recv_sems, masks, slot_kind, per_dest, n_split, False)[0]:
            cp.start()
        token[...] = jnp.zeros_like(token)

    return pl.pallas_call(
        body, name=name,
        out_shape=(pltpu.SemaphoreType.DMA((n,)), pltpu.SemaphoreType.DMA((n,)), pltpu.HBM(x.shape, x.dtype),
                   pltpu.HBM(land_shape, x.dtype), jax.ShapeDtypeStruct((8, 128), F32)),
        in_specs=(HBM_SPEC, HBM_SPEC) + (pl.BlockSpec(memory_space=pl.ANY),) * len(deps),
        out_specs=(SEM_SPEC, SEM_SPEC, HBM_SPEC, HBM_SPEC, pl.BlockSpec(memory_space=pltpu.VMEM)),
        input_output_aliases={0: 2, 1: 3},
        compiler_params=pltpu.CompilerParams(has_side_effects=DATAFLOW),
    )(pltpu.with_memory_space_constraint(x, pltpu.HBM),
      pltpu.with_memory_space_constraint(lax.empty(land_shape, x.dtype), pltpu.HBM), *deps)


def _exchange_wait(name, started, after, masks, slot_kind, per_dest, n_split):
    send_sems, recv_sems, x_thru, land_thru, _ = started
    after = list(after) if isinstance(after, (list, tuple)) else [after]

    def body(x_ref, land_ref, send_sems, recv_sems, *after_and_out_refs):
        sends, recvs = _split_copies(x_ref, land_ref, send_sems, recv_sems, masks, slot_kind, per_dest, n_split)
        for cp in sends:
            cp.wait_send()
        for cp in recvs:
            cp.wait_recv()

    return pl.pallas_call(
        body, name=name,
        out_shape=(pltpu.HBM(x_thru.shape, x_thru.dtype), pltpu.HBM(land_thru.shape, land_thru.dtype)),
        in_specs=(HBM_SPEC, HBM_SPEC, SEM_SPEC, SEM_SPEC) + (pl.BlockSpec(memory_space=pl.ANY),) * len(after),
        out_specs=(HBM_SPEC, HBM_SPEC), input_output_aliases={0: 0, 1: 1},
        compiler_params=pltpu.CompilerParams(has_side_effects=DATAFLOW),
    )(x_thru, land_thru, send_sems, recv_sems, *after)


def _gather_chips(name, x):
    return _exchange(name, x, CHIP_MASKS, "chip", False)


def _gather_shards(name, shard, n_split, overlap=False, dep=None):
    half = shard.shape[0] // 2
    mine = lax.dynamic_slice_in_dim(shard, lax.axis_index("c") * half, half, axis=0)
    if overlap:
        return _exchange_start(name + "_chips", mine, CHIP_MASKS, "chip", False, n_split, dep)
    by_chip = _exchange(name + "_chips", mine, CHIP_MASKS, "chip", False, n_split, copy_own=False)
    return _gather_tail(name, mine, by_chip)


def _gather_tail(name, mine, by_chip):
    c, chip = lax.axis_index("c"), 2 * lax.axis_index("x") + lax.axis_index("y")
    by_chip = lax.dynamic_update_index_in_dim(by_chip, mine, chip, 0)
    both = _exchange(name + "_cores", by_chip, CORE_MASKS, "core", False, N_CHIPS, copy_own=False)
    return lax.dynamic_update_index_in_dim(both, by_chip, c, 0)


def _gather_finish(name, started, after, n_split):
    mine, by_chip = _exchange_wait(name + "_chips_wait", started, after, CHIP_MASKS, "chip", False, n_split)
    return _gather_tail(name, mine, by_chip)


def _sum_slots(name, x, tr=128):
    S, R, N = x.shape
    tr = _pick(R, tr, 16)

    def body(x_ref, o_ref):
        acc = x_ref[0].astype(F32)
        for s in range(1, S):
            acc = acc + x_ref[s].astype(F32)
        o_ref[...] = acc

    return pl.pallas_call(
        body, name=name, grid=(R // tr,),
        in_specs=[pl.BlockSpec((S, tr, N), lambda i: (0, i, 0))], out_specs=pl.BlockSpec((tr, N), lambda i: (i, 0)),
        out_shape=jax.ShapeDtypeStruct((R, N), F32), compiler_params=_params(("parallel",)),
    )(x)


def _add_to_bf16(name, a, b, tr=128):
    S, R, N = a.shape
    tr = _pick(R, tr, 16)

    def body(a_ref, b_ref, o_ref):
        o_ref[...] = (a_ref[...] + b_ref[...]).astype(BF16)

    spec = pl.BlockSpec((S, tr, N), lambda i: (0, i, 0))
    return pl.pallas_call(
        body, name=name, grid=(R // tr,), in_specs=[spec, spec], out_specs=spec,
        out_shape=jax.ShapeDtypeStruct((S, R, N), BF16), compiler_params=_params(("parallel",)),
    )(a, b)


def _add_slabs_to_bf16(name, full, recv, c, tr=64):
    R, NP = full.shape
    half = R // 2
    nb = half // tr

    def body(c_ref, a_ref, b_ref, o_ref):
        x = a_ref[...] + b_ref[...]
        for s in range(N_CHIPS - 1):
            o_ref[s] = x[:, s * SHARD_COLS:(s + 1) * SHARD_COLS].astype(BF16)
        last = jnp.concatenate([x[:, (N_CHIPS - 1) * SHARD_COLS:GAP_AT], x[:, GAP_AT + GAP:]], axis=1)
        o_ref[N_CHIPS - 1] = last.astype(BF16)

    grid_spec = pltpu.PrefetchScalarGridSpec(
        num_scalar_prefetch=1, grid=(nb,),
        in_specs=[pl.BlockSpec((tr, NP), lambda i, c_ref: (c_ref[0] * nb + i, 0)),
                  pl.BlockSpec((tr, NP), lambda i, c_ref: (i, 0))],
        out_specs=pl.BlockSpec((N_CHIPS, tr, SHARD_COLS), lambda i, c_ref: (0, i, 0)))
    return pl.pallas_call(
        body, name=name, grid_spec=grid_spec,
        out_shape=jax.ShapeDtypeStruct((N_CHIPS, half, SHARD_COLS), BF16), compiler_params=_params(("parallel",)),
    )(jnp.reshape(c, (1,)).astype(jnp.int32), full, recv)


def _adamw(name, gparts, w, m, v, tr=128):
    S, R, N = gparts.shape
    tr = _pick(R, tr)
    c1 = 1.0 / (1.0 - ADAM_B1 ** ADAM_STEP)
    c2 = 1.0 / (1.0 - ADAM_B2 ** ADAM_STEP)

    def body(g_ref, w_ref, m_ref, v_ref, go_ref, d_ref, mo_ref, vo_ref):
        g = g_ref[0]
        for s in range(1, S):
            g = g + g_ref[s]
        m_new = ADAM_B1 * m_ref[...] + (1.0 - ADAM_B1) * g
        v_new = ADAM_B2 * v_ref[...] + (1.0 - ADAM_B2) * jnp.square(g)
        go_ref[...] = g
        mo_ref[...] = m_new
        vo_ref[...] = v_new
        d_ref[...] = -ADAM_LR * ((m_new * c1) / (jnp.sqrt(v_new * c2) + ADAM_EPS) + ADAM_WD * w_ref[...])

    spec = pl.BlockSpec((tr, N), lambda i: (i, 0))
    return pl.pallas_call(
        body, name=name, grid=(R // tr,),
        in_specs=[pl.BlockSpec((S, tr, N), lambda i: (0, i, 0)), spec, spec, spec], out_specs=[spec] * 4,
        out_shape=[jax.ShapeDtypeStruct((R, N), F32)] * 4, compiler_params=_params(("parallel",)),
    )(gparts, w, m, v)


def _reduce_to_shard(name, slabs, n_split, by_columns=False, overlap=False):
    c, chip = lax.axis_index("c"), 2 * lax.axis_index("x") + lax.axis_index("y")
    if by_columns:
        R, N = slabs.shape[0], SHARD_COLS
        from_sibling = _exchange(name + "_sib", slabs, CORE_MASKS, None, False, n_split, other_half=True)[0]
        wire = _add_slabs_to_bf16(name + "_add", slabs, from_sibling, c)
    else:
        _, R, N = slabs.shape
        half = R // 2
        halves = slabs.reshape(N_CHIPS, 2, half, N)
        mine = lax.dynamic_index_in_dim(halves, c, axis=1, keepdims=False)
        theirs = lax.dynamic_index_in_dim(halves, 1 - c, axis=1, keepdims=False)
        from_sibling = _exchange(name + "_sib", theirs, CORE_MASKS, None, False, N_CHIPS)[0]
        wire = _add_to_bf16(name + "_add", mine, from_sibling)
    if overlap:
        return _exchange_start(name + "_chips", wire, CHIP_MASKS, "chip", True, n_split), (R, N)
    got = _exchange(name + "_chips", wire, CHIP_MASKS, "chip", True, n_split, copy_own=False)
    return _reduce_tail(name, wire, got, n_split, R, N)


def _reduce_tail(name, wire, got, n_split, R, N):
    c, chip = lax.axis_index("c"), 2 * lax.axis_index("x") + lax.axis_index("y")
    got = lax.dynamic_update_index_in_dim(got, lax.dynamic_index_in_dim(wire, chip, 0, keepdims=False), chip, 0)
    part = _sum_slots(name + "_sum", got)
    both = _exchange(name + "_cores", part, CORE_MASKS, "core", False, n_split, copy_own=False)
    return lax.dynamic_update_index_in_dim(both, part, c, 0).reshape(1, R, N)


def _reduce_finish(name, started, after, n_split):
    handle, (R, N) = started
    wire, got = _exchange_wait(name + "_chips_wait", handle, after, CHIP_MASKS, "chip", True, n_split)
    return _reduce_tail(name, wire, got, n_split, R, N)


def _pack(pieces, cols, row_mult=8):
    flat = jnp.concatenate([p.reshape(-1) for p in pieces])
    rows = -(-flat.shape[0] // cols)
    rows = -(-rows // row_mult) * row_mult
    return jnp.pad(flat, (0, rows * cols - flat.shape[0])).reshape(rows, cols)


def _unpack(packed, shapes):
    flat = packed.reshape(-1)
    out, off = [], 0
    for shp in shapes:
        n = 1
        for d in shp:
            n *= d
        out.append(flat[off:off + n].reshape(shp))
        off += n
    return out


def kernel(x, meta_tokens, norm_w, w_in, rw_shift_mu, rw_w0, rw_w2, rw_a0, rw_a2, rw_k_k, rw_k_a, rw_r_k, rw_gn_w, rw_gn_b, dn_conv_w, dn_A_log, dn_dt_bias, dn_norm_w, w_out, final_norm_w, loss_target, m_meta_tokens, m_norm_w, m_w_in, m_rw_shift_mu, m_rw_w0, m_rw_w2, m_rw_a0, m_rw_a2, m_rw_k_k, m_rw_k_a, m_rw_r_k, m_rw_gn_w, m_rw_gn_b, m_dn_conv_w, m_dn_A_log, m_dn_dt_bias, m_dn_norm_w, m_w_out, m_final_norm_w, v_meta_tokens, v_norm_w, v_w_in, v_rw_shift_mu, v_rw_w0, v_rw_w2, v_rw_a0, v_rw_a2, v_rw_k_k, v_rw_k_a, v_rw_r_k, v_rw_gn_w, v_rw_gn_b, v_dn_conv_w, v_dn_A_log, v_dn_dt_bias, v_dn_norm_w, v_w_out, v_final_norm_w):
    S = x.shape[1]
    L = N_META + S
    Lp = -(-L // CHUNK) * CHUNK

    small_shapes = [(RW_LORA, 256), (RW_LORA, 256), (CONV_W, 768), (N_META, 512)]
    small_mine = _pack([rw_w2[0], rw_a2[0], dn_conv_w[0], meta_tokens], 1024)
    small_all = _gather_chips("gather_small", small_mine)
    per_chip = [_unpack(small_all[s], small_shapes) for s in range(N_CHIPS)]
    w2, a2, conv_w, meta = [jnp.concatenate([per_chip[s][i] for s in range(N_CHIPS)], axis=1) for i in range(4)]
    w_in_started = _gather_shards("gather_w_in", w_in[0].astype(BF16), 8, overlap=True, dep=small_all)
    w_out_started = _gather_shards("gather_w_out", w_out[0].astype(BF16), 8, overlap=True, dep=w_in_started[4])

    tail = [jnp.zeros((Lp - L, D_MODEL), F32)] if Lp > L else []
    h = jnp.concatenate([meta + w_in_started[4][:1, :1], x[0]] + tail, axis=0)
    (u,) = _rowwise("rms_in", _rms_fn, [_row(h)], [norm_w], [D_MODEL], tm=2 * ROW_TILE, out_dtype=BF16)
    u_t = u.T
    m_in, v_in = m_w_in[0] + w_in_started[4][:1, :1], v_w_in[0] + w_in_started[4][:1, :1]
    w_in_all = _gather_finish("gather_w_in", w_in_started, [u, u_t, m_in, v_in], 8)
    slabs = [w_in_all[:, s].reshape(D_MODEL, SHARD_COLS) for s in range(N_CHIPS)]
    cut = GAP_AT - (N_CHIPS - 1) * SHARD_COLS
    W_gapped = jnp.concatenate(slabs[:-1] + [slabs[-1][:, :cut], jnp.zeros((D_MODEL, GAP), BF16), slabs[-1][:, cut:]],
                               axis=1)
    p = _mm("in_proj", u, W_gapped, "nn", dep=w_out_started[4])

    mu = rw_shift_mu
    zpad = jnp.zeros((RW_LORA, RW_WIDTH), F32)
    rw_params = [mu[:, 0:1024], mu[:, 1024:2048], mu[:, 2048:3072], mu[:, 3072:3200], rw_w0,
                 jnp.concatenate([w2, zpad], axis=0), rw_a0, jnp.concatenate([zpad, a2], axis=0), rw_k_k, rw_k_a]
    rw_rows = [_row(p, 1024, OFF_R), _row(p, 1024, OFF_K), _row(p, 1024, OFF_V), _row(p, 128, OFF_S1),
               _row(p, 1024, OFF_GATE)]
    rw_pars = rw_params + [rw_r_k, rw_gn_w, rw_gn_b]
    rw_saved_shapes = [(N_GROUPS, CHUNK, GROUP), (N_GROUPS, CHUNK, GROUP), (N_GROUPS, 2 * CHUNK, 2 * GROUP),
                       (N_GROUPS, 2 * CHUNK, GROUP)]

    dn_rows = [_row(p, 1024, OFF_DQ), _row(p, 1024, OFF_DK), _row(p, 1024, OFF_DV), _row(p, 128, OFF_S2),
               _row(p, 1024, OFF_Z)]
    dn_pars = [conv_w[j:j + 1, 1024 * i:1024 * (i + 1)] for i in range(3) for j in range(CONV_W)]
    narrow = lambda t: jnp.pad(t, ((0, 0), (DN_HEADS, DN_HEAD - 2 * DN_HEADS)))
    dn_pars += [narrow(dn_A_log), narrow(dn_dt_bias), jnp.tile(dn_norm_w, (1, DN_HEADS))]
    dn_saved_shapes = [(N_GROUPS, CHUNK, CHUNK), (N_GROUPS, CHUNK, GROUP)]
    mixer_parts = [(_rw_fused_steps, rw_rows, 4, rw_pars, rw_saved_shapes),
                   (_dn_fused_steps, dn_rows, 3, dn_pars, dn_saved_shapes)]
    y, mixer_kept = _mixers_fwd("mixers_fwd", mixer_parts)
    w_out_all = _gather_finish("gather_w_out", w_out_started, y, 8)
    Wo = jnp.concatenate([w_out_all[:, s].reshape(D_MODEL // N_CHIPS, D_MODEL) for s in range(N_CHIPS)], axis=0)
    yw = _mm("out_proj", y, Wo, "nn", tn=1024)
    loss_acc, dz, dz16, d_fw = _loss_head(h, yw, loss_target[0], final_norm_w.reshape(1, D_MODEL))
    loss = lax.psum(loss_acc[0, 0], ("x", "y", "c"))

    d_wo = _mm("d_w_out", y.T, dz16, "nn", tm=512, tn=1024, tk=Lp)
    wo_started = _reduce_to_shard("rs_w_out", d_wo.reshape(N_CHIPS, D_MODEL // N_CHIPS, D_MODEL), 8, overlap=True)
    dy = _mm("d_out_proj", dz16, Wo, "nt", tn=1024, tk=2048, dep=wo_started[0][4])

    dp, (d_rw_pars, d_dn_pars) = _mixers_bwd("mixers_bwd", mixer_parts, mixer_kept, dy)
    d_prep_pars, d_rw_pars = d_rw_pars[:10], d_rw_pars[10:]
    d_conv_parts = [jnp.concatenate(d_dn_pars[4 * i:4 * i + 4], axis=0) for i in range(3)]
    d_a_log_b, d_dt_b = d_dn_pars[12], d_dn_pars[13]
    d_dn_nw = jnp.sum(d_dn_pars[14].reshape(DN_HEADS, DN_HEAD), axis=0, keepdims=True)
    d_W = _mm("d_w_in", u_t, dp, "nn", tm=1024, tn=768, tk=Lp)
    wi_started = _reduce_to_shard("rs_w_in", d_W, 8, by_columns=True, overlap=True)
    du = _mm("d_in_proj", dp, W_gapped, "nt", tn=2048, tk=1408, dep=wi_started[0][4])
    norm_res = lambda h_, w_: (_rms_fn(h_, w_)[0], h_)
    (dx,), (d_norm_w_x,) = _rowwise_bwd("rms_in_bwd", norm_res, [_row(h)], [norm_w], [_row(du), _row(dz)],
                                        tm=2 * GROUP, row0=N_META, n_rows=S)
    (d_meta,), (d_norm_w_meta,) = _rowwise_bwd("rms_meta_bwd", norm_res, [_row(h)], [norm_w], [_row(du), _row(dz)],
                                               n_rows=N_META)
    d_norm_w = d_norm_w_x + d_norm_w_meta
    grad_x = dx[None]

    d_mu = jnp.concatenate(d_prep_pars[0:4], axis=1)
    d_w2, d_a2 = d_prep_pars[5][:RW_LORA], d_prep_pars[7][RW_LORA:]
    d_conv = jnp.concatenate(d_conv_parts, axis=1)
    head_sum = lambda t: t[:, DN_HEADS:2 * DN_HEADS]
    rep_names = ["norm_w", "rw_shift_mu", "rw_w0", "rw_a0", "rw_k_k", "rw_k_a", "rw_r_k", "rw_gn_w", "rw_gn_b",
                 "dn_A_log", "dn_dt_bias", "dn_norm_w", "final_norm_w"]
    rep_g = [d_norm_w, d_mu, d_prep_pars[4], d_prep_pars[6], d_prep_pars[8], d_prep_pars[9],
             d_rw_pars[0], d_rw_pars[1], d_rw_pars[2],
             head_sum(d_a_log_b), head_sum(d_dt_b), d_dn_nw, d_fw.reshape(D_MODEL)]
    rep_w = [norm_w, rw_shift_mu, rw_w0, rw_a0, rw_k_k, rw_k_a, rw_r_k, rw_gn_w, rw_gn_b, dn_A_log, dn_dt_bias,
             dn_norm_w, final_norm_w]
    rep_m = [m_norm_w, m_rw_shift_mu, m_rw_w0, m_rw_a0, m_rw_k_k, m_rw_k_a, m_rw_r_k, m_rw_gn_w, m_rw_gn_b,
             m_dn_A_log, m_dn_dt_bias, m_dn_norm_w, m_final_norm_w]
    rep_v = [v_norm_w, v_rw_shift_mu, v_rw_w0, v_rw_a0, v_rw_k_k, v_rw_k_a, v_rw_r_k, v_rw_gn_w, v_rw_gn_b,
             v_dn_A_log, v_dn_dt_bias, v_dn_norm_w, v_final_norm_w]
    rep_shapes = [t.shape for t in rep_w]
    rep_all = _exchange("gather_rep_grads", _pack(rep_g, 128), ALL_MASKS, "dev", False)
    rep_out = _adamw("adam_rep", rep_all, _pack(rep_w, 128), _pack(rep_m, 128), _pack(rep_v, 128))
    rep_out = [dict(zip(rep_names, _unpack(t, rep_shapes))) for t in rep_out]

    sm_slabs = jnp.stack([_pack([d_w2[:, 256 * s:256 * (s + 1)], d_a2[:, 256 * s:256 * (s + 1)],
                                 d_conv[:, 768 * s:768 * (s + 1)], d_meta[:, 512 * s:512 * (s + 1)]], 1024, 64)
                          for s in range(N_CHIPS)])
    sm_parts = _reduce_to_shard("rs_small", sm_slabs, 1)
    sm_w = [rw_w2[0], rw_a2[0], dn_conv_w[0], meta_tokens]
    sm_m = [m_rw_w2[0], m_rw_a2[0], m_dn_conv_w[0], m_meta_tokens]
    sm_v = [v_rw_w2[0], v_rw_a2[0], v_dn_conv_w[0], v_meta_tokens]
    sm_out = _adamw("adam_small", sm_parts, _pack(sm_w, 1024, 64), _pack(sm_m, 1024, 64), _pack(sm_v, 1024, 64))
    sm_names = ["rw_w2", "rw_a2", "dn_conv_w", "meta_tokens"]
    sm_full_shapes = [(1, RW_LORA, 256), (1, RW_LORA, 256), (1, CONV_W, 768), (N_META, 512)]
    sm_out = [dict(zip(sm_names, [t.reshape(shp) for t, shp in zip(_unpack(o, small_shapes), sm_full_shapes)]))
              for o in sm_out]

    wo_parts = _reduce_finish("rs_w_out", wo_started, dp, 8)
    wo_out = _adamw("adam_w_out", wo_parts, w_out[0], m_w_out[0], v_w_out[0])
    wi_parts = _reduce_finish("rs_w_in", wi_started, dx, 8)
    wi_out = _adamw("adam_w_in", wi_parts, w_in[0], m_in, v_in)

    order = ["meta_tokens", "norm_w", "w_in", "rw_shift_mu", "rw_w0", "rw_w2", "rw_a0", "rw_a2", "rw_k_k", "rw_k_a",
             "rw_r_k", "rw_gn_w", "rw_gn_b", "dn_conv_w", "dn_A_log", "dn_dt_bias", "dn_norm_w", "w_out",
             "final_norm_w"]
    outs = [loss, grad_x]
    for kind in range(4):
        table = dict(rep_out[kind])
        table.update(sm_out[kind])
        table["w_in"] = wi_out[kind][None]
        table["w_out"] = wo_out[kind][None]
        outs += [table[n] for n in order]
    return tuple(outs)
```

```python
import functools

import jax
import jax.numpy as jnp
from jax import lax
from jax.experimental import pallas as pl
from jax.experimental.pallas import tpu as pltpu

F32 = jnp.float32
BF16 = jnp.bfloat16
HIGH = lax.Precision.HIGH
HIGHEST = lax.Precision.HIGHEST
MESH = pl.DeviceIdType.MESH

D_MODEL = 2048
N_META = 16
RW_WIDTH = 1024
RW_HEAD = 64
RW_HEADS = 16
RW_LORA = 64
RW_GN_EPS = 64e-5
DN_WIDTH = 1024
DN_HEAD = 128
DN_HEADS = 8
CONV_W = 4
CHUNK = 64
NORM_EPS = 1e-6
IN_COLS = 8336
N_CHIPS = 4
SHARD_COLS = IN_COLS // N_CHIPS

NP_COLS = 8 * 1024 + 256
GAP_AT = IN_COLS - DN_WIDTH
GAP = NP_COLS - IN_COLS
OFF_R, OFF_K, OFF_V, OFF_S1, OFF_GATE = 0, 1024, 2048, 3072, 3200
OFF_DQ, OFF_DK, OFF_DV, OFF_S2, OFF_Z = 4224, 5248, 6272, 7296, 7424

ADAM_LR = 0.001
ADAM_B1 = 0.9
ADAM_B2 = 0.999
ADAM_EPS = 1e-08
ADAM_WD = 0.01
ADAM_STEP = 10

VMEM_LIMIT_BYTES = 56 * 1024 * 1024
ROW_TILE = 104
MM_ROW_TILE = 832


def _params(sem=None):
    return pltpu.CompilerParams(dimension_semantics=sem, vmem_limit_bytes=VMEM_LIMIT_BYTES)


def _pick(n, target, mult=8):
    best = None
    for d in range(mult, min(n, target) + 1, mult):
        if n % d == 0:
            best = d
    return n if best is None else best


def _mm(name, a, b, mode, tm=MM_ROW_TILE, tn=1408, tk=2048, dep=None):
    if mode == "nn":
        (M, K), (_, N) = a.shape, b.shape
    elif mode == "nt":
        (M, K), (N, _) = a.shape, b.shape
    else:
        (K, M), (_, N) = a.shape, b.shape
    tm = _pick(M, tm, 128 if mode == "tn" else 16)
    tn = _pick(N, tn, 128)
    tk = _pick(K, tk, 8 if mode == "tn" else 128)
    if mode == "nn":
        a_spec = pl.BlockSpec((tm, tk), lambda i, j, k: (i, k))
        b_spec = pl.BlockSpec((tk, tn), lambda i, j, k: (k, j))
        dims = (((1,), (0,)), ((), ()))
    elif mode == "nt":
        a_spec = pl.BlockSpec((tm, tk), lambda i, j, k: (i, k))
        b_spec = pl.BlockSpec((tn, tk), lambda i, j, k: (j, k))
        dims = (((1,), (1,)), ((), ()))
    else:
        a_spec = pl.BlockSpec((tk, tm), lambda i, j, k: (k, i))
        b_spec = pl.BlockSpec((tk, tn), lambda i, j, k: (k, j))
        dims = (((0,), (0,)), ((), ()))

    def body(a_ref, b_ref, *rest):
        o_ref = rest[-1]

        @pl.when(pl.program_id(2) == 0)
        def _():
            o_ref[...] = jnp.zeros_like(o_ref)

        o_ref[...] += lax.dot_general(a_ref[...].astype(BF16), b_ref[...].astype(BF16), dims,
                                      preferred_element_type=F32)

    deps = [] if dep is None else [dep]
    return pl.pallas_call(
        body, name=name, grid=(M // tm, N // tn, K // tk),
        in_specs=[a_spec, b_spec] + [pl.BlockSpec((8, 128), lambda i, j, k: (0, 0))] * len(deps),
        out_specs=pl.BlockSpec((tm, tn), lambda i, j, k: (i, j)),
        out_shape=jax.ShapeDtypeStruct((M, N), F32),
        compiler_params=_params(("parallel", "parallel", "arbitrary")),
    )(a, b, *deps)


def _row(arr, width=None, cb=0):
    return (arr, arr.shape[1] if width is None else width, cb)


def _rowwise(name, fn, rows, params, out_widths, tm=ROW_TILE, out_dtype=F32):
    R = rows[0][0].shape[0]
    tm = _pick(R, tm, 16 if out_dtype == BF16 else 8)
    n_r, n_p = len(rows), len(params)

    def body(*refs):
        vals = [r[...] for r in refs[:n_r + n_p]]
        for o_ref, val in zip(refs[n_r + n_p:], fn(*vals)):
            o_ref[...] = val.astype(out_dtype)

    in_specs = [pl.BlockSpec((tm, w), lambda i, cb=cb: (i, cb)) for (_, w, cb) in rows]
    in_specs += [pl.BlockSpec(p.shape, lambda i: (0, 0)) for p in params]
    return pl.pallas_call(
        body, name=name, grid=(R // tm,), in_specs=in_specs,
        out_specs=[pl.BlockSpec((tm, w), lambda i: (i, 0)) for w in out_widths],
        out_shape=[jax.ShapeDtypeStruct((R, w), out_dtype) for w in out_widths],
        compiler_params=_params(("parallel",)),
    )(*[r[0] for r in rows], *params)


def _rowwise_bwd(name, fn, rows, params, douts, tm=ROW_TILE, row0=0, n_rows=None):
    R = rows[0][0].shape[0] - row0 if n_rows is None else n_rows
    tm = _pick(R, tm)
    assert row0 % 8 == 0 and row0 + R <= rows[0][0].shape[0]

    def in_spec(w, cb):
        if row0 == 0:
            return pl.BlockSpec((tm, w), lambda i: (i, cb))
        return pl.BlockSpec((pl.Element(tm), pl.Element(w)), lambda i: (pl.multiple_of(row0 + i * tm, 8), cb * w))

    n_r, n_p, n_d = len(rows), len(params), len(douts)

    def body(*refs):
        vals = [r[...] for r in refs[:n_r + n_p]]
        cts = tuple(r[...] for r in refs[n_r + n_p:n_r + n_p + n_d])
        grads = jax.vjp(fn, *vals)[1](cts)
        outs = refs[n_r + n_p + n_d:]
        for o_ref, g in zip(outs[:n_r], grads[:n_r]):
            o_ref[...] = g

        @pl.when(pl.program_id(0) == 0)
        def _():
            for o_ref in outs[n_r:]:
                o_ref[...] = jnp.zeros_like(o_ref)

        for o_ref, g in zip(outs[n_r:], grads[n_r:]):
            o_ref[...] += g

    in_specs = [in_spec(w, cb) for (_, w, cb) in rows]
    in_specs += [pl.BlockSpec(p.shape, lambda i: (0, 0)) for p in params]
    in_specs += [in_spec(w, cb) for (_, w, cb) in douts]
    out_specs = [pl.BlockSpec((tm, w), lambda i: (i, 0)) for (_, w, _) in rows]
    out_specs += [pl.BlockSpec(p.shape, lambda i: (0, 0)) for p in params]
    out_shape = [jax.ShapeDtypeStruct((R, w), F32) for (_, w, _) in rows]
    out_shape += [jax.ShapeDtypeStruct(p.shape, F32) for p in params]
    res = pl.pallas_call(
        body, name=name, grid=(R // tm,), in_specs=in_specs, out_specs=out_specs, out_shape=out_shape,
        compiler_params=_params(("arbitrary",)),
    )(*[r[0] for r in rows], *params, *[d[0] for d in douts])
    return res[:n_r], res[n_r:]


def _softplus(x):
    return jnp.maximum(x, 0.0) + jnp.log(1.0 + jnp.exp(-jnp.abs(x)))


def _silu(x):
    return x * jax.nn.sigmoid(x)


def _rms_fn(h, w):
    return (h * lax.rsqrt(jnp.mean(h * h, axis=-1, keepdims=True) + NORM_EPS) * w,)


def _rw_prep_fn(pr, pr1, pk, pk1, pv, pv1, ps, ps1, mu_r, mu_k, mu_v, mu_s, w0, w2p, a0, a2p, k_k, k_a):
    r = pr + (pr1 - pr) * mu_r
    k = pk + (pk1 - pk) * mu_k
    v = pv + (pv1 - pv) * mu_v
    s = ps + (ps1 - ps) * mu_s
    lora = lambda x, w: jnp.dot(x.astype(BF16), w.astype(BF16), preferred_element_type=F32)
    w_log = -_softplus(-(w0 + lora(jnp.tanh(s), w2p))) - 0.5
    log_decay = -jnp.exp(w_log)
    a = jax.nn.sigmoid(a0 + lora(s, a2p))
    return r, log_decay, k * (1.0 + (a - 1.0) * k_a), v, k * k_k, a


def _conv_fn(u0, u1, u2, u3, w0, w1, w2, w3):
    return (_silu(u0 * w3 + u1 * w2 + u2 * w1 + u3 * w0),)


def _dn_gate_fn(ps2, a_log_n, dt_n):
    beta_n = jax.nn.sigmoid(ps2)
    g_n = -jnp.exp(a_log_n) * _softplus(ps2 + dt_n)
    lane = lax.broadcasted_iota(jnp.int32, (1, DN_HEAD), 1)
    pick = lambda x, j: jnp.broadcast_to(jnp.sum(x * (lane == j).astype(F32), axis=-1, keepdims=True), x.shape)[None]
    beta = jnp.concatenate([pick(beta_n, h) for h in range(DN_HEADS)], axis=0)
    g = jnp.concatenate([pick(g_n, DN_HEADS + h) for h in range(DN_HEADS)], axis=0)
    return beta, g


def _tril_masks(c):
    t = lax.broadcasted_iota(jnp.int32, (c, c), 0)
    s = lax.broadcasted_iota(jnp.int32, (c, c), 1)
    return s <= t, s < t


def _dot(x, y, cx, cy, prec=HIGH):
    nb = x.ndim - 2
    batch = tuple(range(nb))
    return lax.dot_general(x, y, (((cx + nb,), (cy + nb,)), (batch, batch)), precision=prec,
                           preferred_element_type=F32)


INV_BASE = 8


def _run_interleaved(running):
    results, live = [None] * len(running), list(range(len(running)))
    while live:
        for k in list(live):
            try:
                next(running[k])
            except StopIteration as done:
                results[k] = done.value
                live.remove(k)
    return results


def _blocked_inv(a, mm, row, col, mm_merge=None):
    mm_merge = mm if mm_merge is None else mm_merge
    shift = INV_BASE.bit_length() - 1
    d = jnp.where(jnp.right_shift(row, shift) == jnp.right_shift(col, shift), a, 0.0)
    x = (row == col).astype(F32) - d
    p = mm(d, d)
    yield
    x = x + mm(x, p)
    yield
    p = mm(p, p)
    yield
    x = x + mm(x, p)
    yield
    size = INV_BASE
    while size < a.shape[-2]:
        same_pair = jnp.right_shift(row, shift + 1) == jnp.right_shift(col, shift + 1)
        lower_left = (jnp.right_shift(row, shift) & 1 == 1) & (jnp.right_shift(col, shift) & 1 == 0)
        lower = mm_merge(jnp.where(same_pair & lower_left, a, 0.0), x)
        yield
        x = x - mm_merge(x, lower)
        size, shift = 2 * size, shift + 1
        yield
    return x


def _tri_inv(a):
    c = a.shape[-1]
    row, col = lax.broadcasted_iota(jnp.int32, (c, c), 0), lax.broadcasted_iota(jnp.int32, (c, c), 1)
    return _blocked_inv(a, lambda x, y: _dot(x, y, 1, 0), row, col,
                        lambda x, y: _dot(x.astype(BF16), y.astype(BF16), 1, 0, None))


@jax.custom_vjp
def _tri_inv_saved(a, t):
    return t


def _tri_inv_saved_fwd(a, t):
    return t, t


def _tri_inv_saved_bwd(t, dt):
    return -_dot(_dot(t, dt, 0, 0), t, 1, 1), jnp.zeros_like(t)


_tri_inv_saved.defvjp(_tri_inv_saved_fwd, _tri_inv_saved_bwd)


def _pair_masks():
    lane = lax.broadcasted_iota(jnp.int32, (1, 2 * RW_HEAD), 1)
    m0 = (lane < RW_HEAD).astype(F32)
    return m0, 1.0 - m0


def _pair_bd(x):
    m0, m1 = _pair_masks()
    return jnp.concatenate([x * m0, x * m1], axis=-2)


def _pair_mm(x, y):
    return _dot(x, _pair_bd(y), 1, 0)


def _pair_inv(a):
    c = a.shape[-2]
    row = lax.broadcasted_iota(jnp.int32, (c, 2 * RW_HEAD), 0)
    col = lax.broadcasted_iota(jnp.int32, (c, 2 * RW_HEAD), 1) & (RW_HEAD - 1)
    return _blocked_inv(a, _pair_mm, row, col,
                        lambda x, y: _dot(x.astype(BF16), _pair_bd(y).astype(BF16), 1, 0, None))


@jax.custom_vjp
def _pair_inv_saved(a, t):
    return t


def _pair_inv_saved_fwd(a, t):
    return t, t


def _pair_inv_saved_bwd(t, dt):
    m0, m1 = _pair_masks()
    c = t.shape[-2]
    z = _dot(t, dt, 0, 0)
    x = z[:, :c, :] * m0 + z[:, c:, :] * m1
    return -_dot(x, _pair_bd(t), 1, 1), jnp.zeros_like(t)


_pair_inv_saved.defvjp(_pair_inv_saved_fwd, _pair_inv_saved_bwd)


@jax.custom_vjp
def _use_saved(x, x_saved):
    return x_saved


_use_saved.defvjp(lambda x, x_saved: (x_saved, None), lambda _, g: (g, jnp.zeros_like(g)))


def _rw_chunk_steps(S, r, lw, k2, v, kkp, a, gate, rk, gnw, gnb, saved=None):
    B, C, P = r.shape
    m0, m1 = _pair_masks()
    seg = lambda x: (jnp.sum(x * m0, axis=-1, keepdims=True) * m0 + jnp.sum(x * m1, axis=-1, keepdims=True) * m1)
    mm = lambda x, y: _dot(x.astype(BF16), y.astype(BF16), 1, 0, None)
    nt = lambda x, y: _dot(x, y, 1, 1)
    tn = lambda x, y: _dot(x, y, 0, 0)
    pair_mm = lambda x, y: mm(x, _pair_bd(y))
    t_idx = lax.broadcasted_iota(jnp.int32, (C, P), 0)
    s_idx = lax.broadcasted_iota(jnp.int32, (C, P), 1) & (RW_HEAD - 1)
    incl, strict = s_idx <= t_idx, s_idx < t_idx
    tril = jnp.broadcast_to(_tril_masks(C)[0].astype(F32), (B, C, C))
    kk = kkp * lax.rsqrt(seg(kkp * kkp) + 1e-6)
    b = kk * a
    reuse = (lambda x, i: x) if saved is None else (lambda x, i: _use_saved(x, saved[i]))
    g_incl = reuse(_dot(tril, lw, 1, 0), 1)
    g_excl = g_incl - lw
    inv = jnp.exp(-g_incl)
    alpha, beta, kappa, rho = kk * jnp.exp(g_excl), b * inv, k2 * inv, r * jnp.exp(g_incl)
    yield
    ar = jnp.concatenate([alpha, rho], axis=-2)
    scores = reuse(nt(ar, jnp.concatenate([_pair_bd(beta), _pair_bd(kappa)], axis=-2)), 2)
    a_ab = jnp.where(strict, scores[:, :C, :P], 0.0)
    a_ak = jnp.where(strict, scores[:, :C, P:], 0.0)
    r_b = jnp.where(incl, scores[:, C:, :P], 0.0)
    r_k = jnp.where(incl, scores[:, C:, P:], 0.0)
    yield
    t_inv = (yield from _pair_inv(a_ab)) if saved is None else _pair_inv_saved(a_ab, saved[0])
    on_state = reuse(nt(ar, S), 3)
    yield
    u = pair_mm(t_inv, -on_state[:, :C, :] - pair_mm(a_ak, v))
    yield
    y = on_state[:, C:, :] + mm(jnp.concatenate([r_b, r_k], axis=-1),
                                jnp.concatenate([_pair_bd(u), _pair_bd(v)], axis=-2))
    same_head = ((lax.broadcasted_iota(jnp.int32, (P, P), 0) < RW_HEAD)
                 == (lax.broadcasted_iota(jnp.int32, (P, P), 1) < RW_HEAD))
    yield
    fresh = tn(jnp.concatenate([u, v], axis=-2), jnp.concatenate([beta, kappa], axis=-2))
    yield
    S_new = jnp.exp(jnp.sum(lw, axis=-2, keepdims=True)) * (S + jnp.where(same_head, fresh, 0.0))
    dev = y - seg(y) * (1.0 / RW_HEAD)
    yn = dev * lax.rsqrt(seg(dev * dev) * (1.0 / RW_HEAD) + RW_GN_EPS) * gnw + gnb
    bonus = seg(r * k2 * rk) * v
    return (yn + bonus) * _silu(gate), S_new, (t_inv, g_incl, scores, on_state)


def _dn_chunk_steps(S, qc, kc, vc, bb, gb, z, nw, saved=None):
    B, C, D = qc.shape
    mm = lambda x, y: _dot(x.astype(BF16), y.astype(BF16), 1, 0, None)
    nt = lambda x, y, p=None: _dot(x, y, 1, 1, p) if p else _dot(x.astype(BF16), y.astype(BF16), 1, 1, None)
    tn = lambda x, y: _dot(x.astype(BF16), y.astype(BF16), 0, 0, None)
    incl, strict = _tril_masks(C)
    q = qc * lax.rsqrt(jnp.sum(qc * qc, axis=-1, keepdims=True) + 1e-6) * (D ** -0.5)
    k = kc * lax.rsqrt(jnp.sum(kc * kc, axis=-1, keepdims=True) + 1e-6)
    kb, vb = k * bb, vc * bb
    yield
    G = _dot(jnp.broadcast_to(incl.astype(F32), (B, C, C)), gb, 1, 0, HIGHEST)
    G = G if saved is None else _use_saved(G, saved[1])
    lane = lax.broadcasted_iota(jnp.int32, (C, D), 1)
    e0, e1 = (lane == 0).astype(F32), (lane == 1).astype(F32)
    diff = nt(G * e0 + e1, e0 - G * e1, HIGHEST)
    dmask = jnp.where(incl, jnp.exp(jnp.where(incl, diff, 0.0)), 0.0)
    M = jnp.where(strict, nt(kb, k) * dmask, 0.0)
    yield
    T = (yield from _tri_inv(M)) if saved is None else _tri_inv_saved(M, saved[0])
    eG = jnp.exp(G)
    u = mm(T, vb)
    yield
    w = mm(T, kb * eG)
    yield
    attn = jnp.where(incl, nt(q, k) * dmask, 0.0)
    yield
    v_new = u - mm(w, S)
    yield
    o = mm(q * eG, S) + mm(attn, v_new)
    yield
    g_last = jnp.sum(gb, axis=-2, keepdims=True)
    S_new = S * jnp.exp(jnp.broadcast_to(g_last, S.shape)) + tn(k * jnp.exp(g_last - G), v_new)
    on = o * lax.rsqrt(jnp.mean(o * o, axis=-1, keepdims=True) + NORM_EPS) * nw
    return on * _silu(z), S_new, (T, G)


N_GROUPS = 8
GROUP = 128
HALO = 8


def _groups(x):
    return jnp.concatenate([x[:, g * GROUP:(g + 1) * GROUP][None] for g in range(N_GROUPS)], axis=0)


@functools.partial(jax.custom_vjp, nondiff_argnums=(1,))
def _shift_rows(ext, j):
    return pltpu.roll(ext, j, 0)[HALO:, :]


def _shift_rows_fwd(ext, j):
    return _shift_rows(ext, j), None


def _shift_rows_bwd(j, _, d):
    z = jnp.concatenate([jnp.zeros((HALO, d.shape[1]), d.dtype), d], axis=0)
    return (pltpu.roll(z, z.shape[0] - j, 0),)


_shift_rows.defvjp(_shift_rows_fwd, _shift_rows_bwd)


def _rw_fused_steps(S, h_r, h_k, h_v, h_s, p_r, p_k, p_v, p_s, gate, *pars, saved=None):
    prev = lambda h, x: _shift_rows(jnp.concatenate([h, x], axis=0), 1)
    seq = _rw_prep_fn(p_r, prev(h_r, p_r), p_k, prev(h_k, p_k), p_v, prev(h_v, p_v), p_s, prev(h_s, p_s), *pars[:10])
    yield
    return (yield from _rw_chunk_steps(S, *[_groups(t) for t in seq], _groups(gate),
                                       *[_groups(t) for t in pars[10:]], saved=saved))


def _dn_fused_steps(S, h_q, h_k, h_v, p_q, p_k, p_v, p_s2, z, *pars, saved=None):
    conv = []
    for i, (h, x) in enumerate(((h_q, p_q), (h_k, p_k), (h_v, p_v))):
        ext = jnp.concatenate([h, x], axis=0)
        conv += _conv_fn(x, _shift_rows(ext, 1), _shift_rows(ext, 2), _shift_rows(ext, 3), *pars[4 * i:4 * i + 4])
    beta, g = _dn_gate_fn(p_s2, pars[12], pars[13])
    yield
    return (yield from _dn_chunk_steps(S, *[_groups(t) for t in conv], beta, g, _groups(z), _groups(pars[14]),
                                       saved=saved))


def _row_specs(rows, n_halo, row_of):
    C = CHUNK
    specs = [pl.BlockSpec((pl.Element(C), pl.Element(w)), lambda c, off=off: (row_of(c) * C, off))
             for (_, w, off) in rows]
    specs += [pl.BlockSpec((pl.Element(HALO), pl.Element(w)),
                           lambda c, off=off: (pl.multiple_of(jnp.maximum(row_of(c) * C - HALO, 0), HALO), off))
              for (_, w, off) in rows[:n_halo]]
    return specs


def _mixers_fwd(name, parts):
    Lp = parts[0][1][0][0].shape[0]
    C, D, NB = CHUNK, GROUP, N_GROUPS
    nc, width = Lp // C, NB * D
    n_in = [len(rows) + n_halo + len(pars) for (_, rows, n_halo, pars, _) in parts]
    n_out = [1 + len(shapes) for (*_, shapes) in parts]

    def body(*refs):
        ins, y_ref = refs[:sum(n_in)], refs[sum(n_in)]
        outs, states = refs[sum(n_in) + 1:sum(n_in) + 1 + sum(n_out)], refs[sum(n_in) + 1 + sum(n_out):]
        first = pl.program_id(0) == 0

        @pl.when(first)
        def _():
            for s_ref in states:
                s_ref[...] = jnp.zeros_like(s_ref)

        i_at, running = 0, []
        for k, (steps, rows, n_halo, pars, shapes) in enumerate(parts):
            n_r = len(rows)
            row_refs, halo_refs = ins[i_at:i_at + n_r], ins[i_at + n_r:i_at + n_r + n_halo]
            par_refs = ins[i_at + n_r + n_halo:i_at + n_in[k]]
            S = states[k][...]
            halos = [jnp.where(first, 0.0, r[...]) for r in halo_refs]
            running.append(steps(S, *halos, *[r[...] for r in row_refs], *[r[...] for r in par_refs]))
            i_at += n_in[k]
        results = _run_interleaved(running)
        o_at = 0
        for k, (y, S_new, saved) in enumerate(results):
            ck_ref, saved_refs = outs[o_at], outs[o_at + 1:o_at + n_out[k]]
            ck_ref[0] = states[k][...]
            for g in range(NB):
                y_ref[:, k * width + g * D:k * width + (g + 1) * D] = y[g].astype(BF16)
            for o_ref, val in zip(saved_refs, saved):
                o_ref[0] = val
            states[k][...] = S_new
            o_at += n_out[k]

    in_specs, args, out_specs, out_shape = [], [], [], []
    for (_, rows, n_halo, pars, shapes) in parts:
        in_specs += _row_specs(rows, n_halo, lambda c: c) + [pl.BlockSpec(p.shape, lambda c: (0, 0)) for p in pars]
        args += [r[0] for r in rows] + [r[0] for r in rows[:n_halo]] + list(pars)
        for shp in [(NB, D, D)] + list(shapes):
            out_specs.append(pl.BlockSpec((1,) + tuple(shp), lambda c: (c, 0, 0, 0)))
            out_shape.append(jax.ShapeDtypeStruct((nc,) + tuple(shp), F32))
    res = pl.pallas_call(
        body, name=name, grid=(nc,), in_specs=in_specs,
        out_specs=[pl.BlockSpec((C, width * len(parts)), lambda c: (c, 0))] + out_specs,
        out_shape=[jax.ShapeDtypeStruct((Lp, width * len(parts)), BF16)] + out_shape,
        scratch_shapes=[pltpu.VMEM((NB, D, D), F32)] * len(parts),
        compiler_params=_params(("arbitrary",)),
    )(*args)
    per_part, at = [], 1
    for n in n_out:
        per_part.append(list(res[at:at + n]))
        at += n
    return res[0], per_part


def _mixers_bwd(name, parts, kept, dy):
    Lp = parts[0][1][0][0].shape[0]
    C, D, NB = CHUNK, GROUP, N_GROUPS
    nc = Lp // C
    n_in = [len(rows) + n_halo + len(pars) + len(kept[k]) for k, (_, rows, n_halo, pars, _) in enumerate(parts)]
    widths = [sum(w for (_, w, _) in rows) for (_, rows, *_) in parts]
    n_par = [len(pars) for (_, _, _, pars, _) in parts]
    n_scr = [1 + n_halo for (_, _, n_halo, _, _) in parts]
    rev = lambda c: nc - 1 - c

    def body(*refs):
        ins, dy_ref = refs[:sum(n_in)], refs[sum(n_in)]
        drows_ref = refs[sum(n_in) + 1]
        dpar_all = refs[sum(n_in) + 2:sum(n_in) + 2 + sum(n_par)]
        scratch = refs[sum(n_in) + 2 + sum(n_par):]
        i = pl.program_id(0)

        @pl.when(i == 0)
        def _():
            for o_ref in list(dpar_all) + list(scratch):
                o_ref[...] = jnp.zeros_like(o_ref)

        i_at, vals, stored, cts, sizes = 0, [], [], [], []
        for k, (_, rows, n_halo, pars, _) in enumerate(parts):
            n_r = len(rows)
            row_refs, halo_refs = ins[i_at:i_at + n_r], ins[i_at + n_r:i_at + n_r + n_halo]
            par_refs = ins[i_at + n_r + n_halo:i_at + n_r + n_halo + n_par[k]]
            kept_refs = ins[i_at + n_r + n_halo + n_par[k]:i_at + n_in[k]]
            halos = [jnp.where(i == nc - 1, 0.0, r[...]) for r in halo_refs]
            part_vals = [kept_refs[0][0]] + halos + [r[...] for r in row_refs] + [r[...] for r in par_refs]
            vals += part_vals
            sizes.append(len(part_vals))
            stored.append(tuple(r[0] for r in kept_refs[1:]))
            cts.append((_groups(dy_ref[:, k * NB * D:(k + 1) * NB * D]), scratch[sum(n_scr[:k])][...]))
            i_at += n_in[k]

        def both(*flat):
            running, at = [], 0
            for k, (steps, *_) in enumerate(parts):
                running.append(steps(*flat[at:at + sizes[k]], saved=stored[k]))
                at += sizes[k]
            return tuple(r[:2] for r in _run_interleaved(running))

        all_grads = jax.vjp(both, *vals)[1](tuple(cts))
        i_at = p_at = s_at = g_at = col = 0
        for k, (_, rows, n_halo, pars, _) in enumerate(parts):
            n_r = len(rows)
            grads = all_grads[g_at:g_at + sizes[k]]
            dpar_refs = dpar_all[p_at:p_at + n_par[k]]
            ds_ref, carry_refs = scratch[s_at], scratch[s_at + 1:s_at + n_scr[k]]
            ds_ref[...] = grads[0]
            d_halos, d_rows = grads[1:1 + n_halo], grads[1 + n_halo:1 + n_halo + n_r]
            for j, g in enumerate(d_rows):
                if j < n_halo:
                    g = g + jnp.concatenate([jnp.zeros((C - HALO, g.shape[1]), F32), carry_refs[j][...]], axis=0)
                    carry_refs[j][...] = d_halos[j]
                drows_ref[:, col:col + g.shape[1]] = g.astype(BF16)
                col += g.shape[1]
            for o_ref, g in zip(dpar_refs, grads[1 + n_halo + n_r:]):
                o_ref[...] += g
            p_at, s_at, g_at = p_at + n_par[k], s_at + n_scr[k], g_at + sizes[k]

    in_specs, args, out_specs, out_shape, scratch_shapes = [], [], [], [], []
    for k, (_, rows, n_halo, pars, _) in enumerate(parts):
        in_specs += _row_specs(rows, n_halo, rev) + [pl.BlockSpec(p.shape, lambda c: (0, 0)) for p in pars]
        in_specs += [pl.BlockSpec((1,) + tuple(t.shape[1:]), lambda c: (rev(c), 0, 0, 0)) for t in kept[k]]
        args += [r[0] for r in rows] + [r[0] for r in rows[:n_halo]] + list(pars) + list(kept[k])
        out_specs += [pl.BlockSpec(p.shape, lambda c: (0, 0)) for p in pars]
        out_shape += [jax.ShapeDtypeStruct(p.shape, F32) for p in pars]
        scratch_shapes += [pltpu.VMEM((NB, D, D), F32)] + [pltpu.VMEM((HALO, w), F32) for (_, w, _) in rows[:n_halo]]
    res = pl.pallas_call(
        body, name=name, grid=(nc,),
        in_specs=in_specs + [pl.BlockSpec((C, dy.shape[1]), lambda c: (rev(c), 0))],
        out_specs=[pl.BlockSpec((C, sum(widths)), lambda c: (rev(c), 0))] + out_specs,
        out_shape=[jax.ShapeDtypeStruct((Lp, sum(widths)), BF16)] + out_shape,
        scratch_shapes=scratch_shapes, compiler_params=_params(("arbitrary",)),
    )(*args, dy)
    d_pars, at = [], 1
    for n in n_par:
        d_pars.append(list(res[at:at + n]))
        at += n
    return res[0], d_pars


def _loss_head(h, yw, target, fw, tm=2 * ROW_TILE):
    Lp, Dm = h.shape
    S = target.shape[0]
    n_real = N_META + S
    tm = _pick(Lp, tm, 16)
    n_tiles = Lp // tm
    assert n_tiles >= 2 and N_META < tm <= S and Lp - n_real < tm and (S - tm) % 8 == 0 and N_META % 8 == 0

    def out_fn(z, fw_):
        return z * lax.rsqrt(jnp.mean(z * z, axis=-1, keepdims=True) + NORM_EPS) * fw_

    def body(h_ref, yw_ref, t_ref, fw_ref, loss_ref, dz_ref, dz16_ref, dfw_ref):
        i = pl.program_id(0)

        @pl.when(i == 0)
        def _():
            loss_ref[...] = jnp.zeros_like(loss_ref)
            dfw_ref[...] = jnp.zeros_like(dfw_ref)

        row = i * tm + lax.broadcasted_iota(jnp.int32, (tm, 1), 0)
        mask = ((row >= N_META) & (row < n_real)).astype(F32)
        z = h_ref[...] + yw_ref[...]
        o, vjp = jax.vjp(out_fn, z, fw_ref[...])
        t = t_ref[...]
        t = jnp.where(i == 0, pltpu.roll(t, N_META, 0), t)
        if Lp > n_real:
            t = jnp.where(i == n_tiles - 1, pltpu.roll(t, tm - (Lp - n_real), 0), t)
        err = (o - t) * mask
        row_loss = 0.5 * jnp.mean(jnp.square(err), axis=-1, keepdims=True)
        dz, dfw = vjp(err * (1.0 / Dm))
        loss_ref[...] += jnp.sum(row_loss, axis=0, keepdims=True)
        dz_ref[...] = dz
        dz16_ref[...] = dz.astype(BF16)
        dfw_ref[...] += dfw

    row_spec = pl.BlockSpec((tm, Dm), lambda i: (i, 0))
    target_spec = pl.BlockSpec((pl.Element(tm), pl.Element(Dm)),
                               lambda i: (pl.multiple_of(jnp.clip(i * tm - N_META, 0, S - tm), 8), 0))
    return pl.pallas_call(
        body, name="loss_head", grid=(n_tiles,),
        in_specs=[row_spec, row_spec, target_spec, pl.BlockSpec((1, Dm), lambda i: (0, 0))],
        out_specs=[pl.BlockSpec((8, 128), lambda i: (0, 0)), row_spec, row_spec,
                   pl.BlockSpec((1, Dm), lambda i: (0, 0))],
        out_shape=[jax.ShapeDtypeStruct((8, 128), F32), jax.ShapeDtypeStruct((Lp, Dm), F32),
                   jax.ShapeDtypeStruct((Lp, Dm), BF16), jax.ShapeDtypeStruct((1, Dm), F32)],
        compiler_params=_params(("arbitrary",)),
    )(h, yw, target, fw)


def _exchange(name, x, masks, slot_kind, per_dest, n_split=1, copy_own=True, other_half=False):
    n = len(masks)
    keep_own = slot_kind is not None and copy_own
    n_slots = {"chip": 4, "core": 2, "dev": 8, None: n}[slot_kind]
    blk_shape = x.shape[1:] if per_dest else x.shape
    if other_half:
        blk_shape = (x.shape[0] // 2,) + tuple(x.shape[1:])
    rows = blk_shape[0] // n_split

    def body(x_ref, o_ref, send_sems, recv_sems, local_sems):
        mx, my, mc = lax.axis_index("x"), lax.axis_index("y"), lax.axis_index("c")
        if other_half:
            x_ref = x_ref.at[pl.ds((1 - mc) * blk_shape[0], blk_shape[0])]

        def slot(k, px, py, pc):
            return {"chip": 2 * px + py, "core": pc, "dev": 4 * px + 2 * py + pc, None: k}[slot_kind]

        def peer(m):
            return (mx + m[0]) % 2, (my + m[1]) % 2, (mc + m[2]) % 2

        def part(ref, j):
            return ref.at[pl.ds(j * rows, rows)]

        own_src = x_ref.at[2 * mx + my] if per_dest else x_ref
        local = []
        if keep_own:
            own_dst = o_ref.at[slot(0, mx, my, mc)]
            local = [pltpu.make_async_copy(part(own_src, j), part(own_dst, j), local_sems.at[j])
                     for j in range(n_split)]
        for cp in local:
            cp.start()
        sends = []
        for k, m in enumerate(masks):
            px, py, pc = peer(m)
            src = x_ref.at[2 * px + py] if per_dest else x_ref
            dst = o_ref.at[slot(k, mx, my, mc)]
            for j in range(n_split):
                sends.append(pltpu.make_async_remote_copy(
                    src_ref=part(src, j), dst_ref=part(dst, j), send_sem=send_sems.at[k * n_split + j],
                    recv_sem=recv_sems.at[k * n_split + j], device_id=(px, py, pc), device_id_type=MESH))
        for cp in sends:
            cp.start()
        for k, m in enumerate(masks):
            px, py, pc = peer(m)
            landed = o_ref.at[slot(k, px, py, pc)]
            for j in range(n_split):
                pltpu.make_async_remote_copy(
                    src_ref=part(own_src, j), dst_ref=part(landed, j), send_sem=send_sems.at[k * n_split + j],
                    recv_sem=recv_sems.at[k * n_split + j], device_id=(px, py, pc), device_id_type=MESH).wait_recv()
        for cp in sends:
            cp.wait_send()
        for cp in local:
            cp.wait()

    return pl.pallas_call(
        body, name=name,
        in_specs=[pl.BlockSpec(memory_space=pl.ANY)], out_specs=pl.BlockSpec(memory_space=pl.ANY),
        out_shape=jax.ShapeDtypeStruct((n_slots,) + tuple(blk_shape), x.dtype),
        scratch_shapes=[pltpu.SemaphoreType.DMA((n * n_split,)), pltpu.SemaphoreType.DMA((n * n_split,)),
                        pltpu.SemaphoreType.DMA((n_split,))],
        compiler_params=pltpu.CompilerParams(has_side_effects=True),
    )(x)


CHIP_MASKS = [(1, 0, 0), (0, 1, 0), (1, 1, 0)]
CORE_MASKS = [(0, 0, 1)]
ALL_MASKS = [(dx, dy, dc) for dx in (0, 1) for dy in (0, 1) for dc in (0, 1) if (dx, dy, dc) != (0, 0, 0)]

HBM_SPEC = pl.BlockSpec(memory_space=pltpu.HBM)
SEM_SPEC = pl.BlockSpec(memory_space=pltpu.SEMAPHORE)
DATAFLOW = pltpu.SideEffectType.DATAFLOW_SIDE_EFFECTING


def _split_copies(x_ref, land_ref, send_sems, recv_sems, masks, slot_kind, per_dest, n_split, with_recvs=True):
    mx, my, mc = lax.axis_index("x"), lax.axis_index("y"), lax.axis_index("c")
    slot = lambda px, py, pc: {"chip": 2 * px + py, "core": pc}[slot_kind]
    rows = (x_ref.shape[1] if per_dest else x_ref.shape[0]) // n_split
    part = lambda ref, j: ref.at[pl.ds(j * rows, rows)]
    sends, recvs = [], []
    for k, m in enumerate(masks):
        px, py, pc = (mx + m[0]) % 2, (my + m[1]) % 2, (mc + m[2]) % 2
        src = x_ref.at[2 * px + py] if per_dest else x_ref
        own_src = x_ref.at[2 * mx + my] if per_dest else x_ref
        for j in range(n_split):
            sems = dict(send_sem=send_sems.at[k * n_split + j], recv_sem=recv_sems.at[k * n_split + j],
                        device_id=(px, py, pc), device_id_type=MESH)
            sends.append(pltpu.make_async_remote_copy(
                src_ref=part(src, j), dst_ref=part(land_ref.at[slot(mx, my, mc)], j), **sems))
            if with_recvs:
                recvs.append(pltpu.make_async_remote_copy(
                    src_ref=part(own_src, j), dst_ref=part(land_ref.at[slot(px, py, pc)], j), **sems))
    return sends, recvs


def _exchange_start(name, x, masks, slot_kind, per_dest, n_split, dep=None):
    n = len(masks) * n_split
    blk_shape = x.shape[1:] if per_dest else x.shape
    land_shape = ({"chip": 4, "core": 2}[slot_kind],) + tuple(blk_shape)
    deps = [] if dep is None else [dep]

    def body(x_ref, land_ref, *rest):
        send_sems, recv_sems, x_thru, land_thru, token = rest[len(deps):]
        for cp in _split_copies(x_ref, land_ref, send_sems, recv_sems, masks, slot_kind, per_dest, n_split, False)[0]:
            cp.start()
        token[...] = jnp.zeros_like(token)

    return pl.pallas_call(
        body, name=name,
        out_shape=(pltpu.SemaphoreType.DMA((n,)), pltpu.SemaphoreType.DMA((n,)), pltpu.HBM(x.shape, x.dtype),
                   pltpu.HBM(land_shape, x.dtype), jax.ShapeDtypeStruct((8, 128), F32)),
        in_specs=(HBM_SPEC, HBM_SPEC) + (pl.BlockSpec(memory_space=pl.ANY),) * len(deps),
        out_specs=(SEM_SPEC, SEM_SPEC, HBM_SPEC, HBM_SPEC, pl.BlockSpec(memory_space=pltpu.VMEM)),
        input_output_aliases={0: 2, 1: 3},
        compiler_params=pltpu.CompilerParams(has_side_effects=DATAFLOW),
    )(pltpu.with_memory_space_constraint(x, pltpu.HBM),
      pltpu.with_memory_space_constraint(lax.empty(land_shape, x.dtype), pltpu.HBM), *deps)


def _exchange_wait(name, started, after, masks, slot_kind, per_dest, n_split):
    send_sems, recv_sems, x_thru, land_thru, _ = started
    after = list(after) if isinstance(after, (list, tuple)) else [after]

    def body(x_ref, land_ref, send_sems, recv_sems, *after_and_out_refs):
        sends, recvs = _split_copies(x_ref, land_ref, send_sems, recv_sems, masks, slot_kind, per_dest, n_split)
        for cp in sends:
            cp.wait_send()
        for cp in recvs:
            cp.wait_recv()

    return pl.pallas_call(
        body, name=name,
        out_shape=(pltpu.HBM(x_thru.shape, x_thru.dtype), pltpu.HBM(land_thru.shape, land_thru.dtype)),
        in_specs=(HBM_SPEC, HBM_SPEC, SEM_SPEC, SEM_SPEC) + (pl.BlockSpec(memory_space=pl.ANY),) * len(after),
        out_specs=(HBM_SPEC, HBM_SPEC), input_output_aliases={0: 0, 1: 1},
        compiler_params=pltpu.CompilerParams(has_side_effects=DATAFLOW),
    )(x_thru, land_thru, send_sems, recv_sems, *after)


def _gather_chips(name, x):
    return _exchange(name, x, CHIP_MASKS, "chip", False)


def _gather_shards(name, shard, n_split, overlap=False, dep=None):
    half = shard.shape[0] // 2
    mine = lax.dynamic_slice_in_dim(shard, lax.axis_index("c") * half, half, axis=0)
    if overlap:
        return _exchange_start(name + "_chips", mine, CHIP_MASKS, "chip", False, n_split, dep)
    by_chip = _exchange(name + "_chips", mine, CHIP_MASKS, "chip", False, n_split, copy_own=False)
    return _gather_tail(name, mine, by_chip)


def _gather_tail(name, mine, by_chip, arrange=None):
    c, chip = lax.axis_index("c"), 2 * lax.axis_index("x") + lax.axis_index("y")
    by_chip = lax.dynamic_update_index_in_dim(by_chip, mine, chip, 0)
    if arrange is not None:
        by_chip = arrange(by_chip)
    both = _exchange(name + "_cores", by_chip, CORE_MASKS, "core", False, N_CHIPS, copy_own=False)
    return lax.dynamic_update_index_in_dim(both, by_chip, c, 0)


def _gather_finish(name, started, after, n_split, arrange=None):
    mine, by_chip = _exchange_wait(name + "_chips_wait", started, after, CHIP_MASKS, "chip", False, n_split)
    return _gather_tail(name, mine, by_chip, arrange)


def _sum_slots(name, x, tr=128):
    S, R, N = x.shape
    tr = _pick(R, tr, 16)

    def body(x_ref, o_ref):
        acc = x_ref[0].astype(F32)
        for s in range(1, S):
            acc = acc + x_ref[s].astype(F32)
        o_ref[...] = acc

    return pl.pallas_call(
        body, name=name, grid=(R // tr,),
        in_specs=[pl.BlockSpec((S, tr, N), lambda i: (0, i, 0))], out_specs=pl.BlockSpec((tr, N), lambda i: (i, 0)),
        out_shape=jax.ShapeDtypeStruct((R, N), F32), compiler_params=_params(("parallel",)),
    )(x)


def _add_to_bf16(name, a, b, tr=128):
    S, R, N = a.shape
    tr = _pick(R, tr, 16)

    def body(a_ref, b_ref, o_ref):
        o_ref[...] = (a_ref[...] + b_ref[...]).astype(BF16)

    spec = pl.BlockSpec((S, tr, N), lambda i: (0, i, 0))
    return pl.pallas_call(
        body, name=name, grid=(R // tr,), in_specs=[spec, spec], out_specs=spec,
        out_shape=jax.ShapeDtypeStruct((S, R, N), BF16), compiler_params=_params(("parallel",)),
    )(a, b)


def _add_slabs_to_bf16(name, full, recv, c, tr=64):
    R, NP = full.shape
    half = R // 2
    nb = half // tr

    def body(c_ref, a_ref, b_ref, o_ref):
        x = a_ref[...] + b_ref[...]
        for s in range(N_CHIPS - 1):
            o_ref[s] = x[:, s * SHARD_COLS:(s + 1) * SHARD_COLS].astype(BF16)
        last = jnp.concatenate([x[:, (N_CHIPS - 1) * SHARD_COLS:GAP_AT], x[:, GAP_AT + GAP:]], axis=1)
        o_ref[N_CHIPS - 1] = last.astype(BF16)

    grid_spec = pltpu.PrefetchScalarGridSpec(
        num_scalar_prefetch=1, grid=(nb,),
        in_specs=[pl.BlockSpec((tr, NP), lambda i, c_ref: (c_ref[0] * nb + i, 0)),
                  pl.BlockSpec((tr, NP), lambda i, c_ref: (i, 0))],
        out_specs=pl.BlockSpec((N_CHIPS, tr, SHARD_COLS), lambda i, c_ref: (0, i, 0)))
    return pl.pallas_call(
        body, name=name, grid_spec=grid_spec,
        out_shape=jax.ShapeDtypeStruct((N_CHIPS, half, SHARD_COLS), BF16), compiler_params=_params(("parallel",)),
    )(jnp.reshape(c, (1,)).astype(jnp.int32), full, recv)


def _adamw(name, gparts, w, m, v, tr=128):
    S, R, N = gparts.shape
    tr = _pick(R, tr)
    c1 = 1.0 / (1.0 - ADAM_B1 ** ADAM_STEP)
    c2 = 1.0 / (1.0 - ADAM_B2 ** ADAM_STEP)

    def body(g_ref, w_ref, m_ref, v_ref, go_ref, d_ref, mo_ref, vo_ref):
        g = g_ref[0]
        for s in range(1, S):
            g = g + g_ref[s]
        m_new = ADAM_B1 * m_ref[...] + (1.0 - ADAM_B1) * g
        v_new = ADAM_B2 * v_ref[...] + (1.0 - ADAM_B2) * jnp.square(g)
        go_ref[...] = g
        mo_ref[...] = m_new
        vo_ref[...] = v_new
        d_ref[...] = -ADAM_LR * ((m_new * c1) / (jnp.sqrt(v_new * c2) + ADAM_EPS) + ADAM_WD * w_ref[...])

    spec = pl.BlockSpec((tr, N), lambda i: (i, 0))
    return pl.pallas_call(
        body, name=name, grid=(R // tr,),
        in_specs=[pl.BlockSpec((S, tr, N), lambda i: (0, i, 0)), spec, spec, spec], out_specs=[spec] * 4,
        out_shape=[jax.ShapeDtypeStruct((R, N), F32)] * 4, compiler_params=_params(("parallel",)),
    )(gparts, w, m, v)


def _reduce_to_shard(name, slabs, n_split, by_columns=False, overlap=False):
    c, chip = lax.axis_index("c"), 2 * lax.axis_index("x") + lax.axis_index("y")
    if by_columns:
        R, N = slabs.shape[0], SHARD_COLS
        from_sibling = _exchange(name + "_sib", slabs, CORE_MASKS, None, False, n_split, other_half=True)[0]
        wire = _add_slabs_to_bf16(name + "_add", slabs, from_sibling, c)
    else:
        _, R, N = slabs.shape
        half = R // 2
        halves = slabs.reshape(N_CHIPS, 2, half, N)
        mine = lax.dynamic_index_in_dim(halves, c, axis=1, keepdims=False)
        theirs = lax.dynamic_index_in_dim(halves, 1 - c, axis=1, keepdims=False)
        from_sibling = _exchange(name + "_sib", theirs, CORE_MASKS, None, False, N_CHIPS)[0]
        wire = _add_to_bf16(name + "_add", mine, from_sibling)
    if overlap:
        return _exchange_start(name + "_chips", wire, CHIP_MASKS, "chip", True, n_split), (R, N)
    got = _exchange(name + "_chips", wire, CHIP_MASKS, "chip", True, n_split, copy_own=False)
    return _reduce_tail(name, wire, got, n_split, R, N)


def _reduce_tail(name, wire, got, n_split, R, N):
    c, chip = lax.axis_index("c"), 2 * lax.axis_index("x") + lax.axis_index("y")
    got = lax.dynamic_update_index_in_dim(got, lax.dynamic_index_in_dim(wire, chip, 0, keepdims=False), chip, 0)
    part = _sum_slots(name + "_sum", got)
    both = _exchange(name + "_cores", part, CORE_MASKS, "core", False, n_split, copy_own=False)
    return lax.dynamic_update_index_in_dim(both, part, c, 0).reshape(1, R, N)


def _reduce_finish(name, started, after, n_split):
    handle, (R, N) = started
    wire, got = _exchange_wait(name + "_chips_wait", handle, after, CHIP_MASKS, "chip", True, n_split)
    return _reduce_tail(name, wire, got, n_split, R, N)


def _pack(pieces, cols, row_mult=8):
    flat = jnp.concatenate([p.reshape(-1) for p in pieces])
    rows = -(-flat.shape[0] // cols)
    rows = -(-rows // row_mult) * row_mult
    return jnp.pad(flat, (0, rows * cols - flat.shape[0])).reshape(rows, cols)


def _unpack(packed, shapes):
    flat = packed.reshape(-1)
    out, off = [], 0
    for shp in shapes:
        n = 1
        for d in shp:
            n *= d
        out.append(flat[off:off + n].reshape(shp))
        off += n
    return out


def kernel(x, meta_tokens, norm_w, w_in, rw_shift_mu, rw_w0, rw_w2, rw_a0, rw_a2, rw_k_k, rw_k_a, rw_r_k, rw_gn_w, rw_gn_b, dn_conv_w, dn_A_log, dn_dt_bias, dn_norm_w, w_out, final_norm_w, loss_target, m_meta_tokens, m_norm_w, m_w_in, m_rw_shift_mu, m_rw_w0, m_rw_w2, m_rw_a0, m_rw_a2, m_rw_k_k, m_rw_k_a, m_rw_r_k, m_rw_gn_w, m_rw_gn_b, m_dn_conv_w, m_dn_A_log, m_dn_dt_bias, m_dn_norm_w, m_w_out, m_final_norm_w, v_meta_tokens, v_norm_w, v_w_in, v_rw_shift_mu, v_rw_w0, v_rw_w2, v_rw_a0, v_rw_a2, v_rw_k_k, v_rw_k_a, v_rw_r_k, v_rw_gn_w, v_rw_gn_b, v_dn_conv_w, v_dn_A_log, v_dn_dt_bias, v_dn_norm_w, v_w_out, v_final_norm_w):
    S = x.shape[1]
    L = N_META + S
    Lp = -(-L // CHUNK) * CHUNK

    small_shapes = [(RW_LORA, 256), (RW_LORA, 256), (CONV_W, 768), (N_META, 512)]
    small_mine = _pack([rw_w2[0], rw_a2[0], dn_conv_w[0], meta_tokens], 1024)
    small_all = _gather_chips("gather_small", small_mine)
    per_chip = [_unpack(small_all[s], small_shapes) for s in range(N_CHIPS)]
    w2, a2, conv_w, meta = [jnp.concatenate([per_chip[s][i] for s in range(N_CHIPS)], axis=1) for i in range(4)]
    w_in_started = _gather_shards("gather_w_in", w_in[0].astype(BF16), 8, overlap=True, dep=small_all)
    w_out_started = _gather_shards("gather_w_out", w_out[0].astype(BF16), 8, overlap=True, dep=w_in_started[4])

    tail = [jnp.zeros((Lp - L, D_MODEL), F32)] if Lp > L else []
    h = jnp.concatenate([meta + w_in_started[4][:1, :1], x[0]] + tail, axis=0)
    (u,) = _rowwise("rms_in", _rms_fn, [_row(h)], [norm_w], [D_MODEL], tm=2 * ROW_TILE, out_dtype=BF16)
    u_t = u.T
    m_in, v_in = m_w_in[0] + w_in_started[4][:1, :1], v_w_in[0] + w_in_started[4][:1, :1]
    cut = GAP_AT - (N_CHIPS - 1) * SHARD_COLS

    def gapped(by_chip):
        last = by_chip[N_CHIPS - 1]
        return jnp.concatenate([by_chip[s] for s in range(N_CHIPS - 1)]
                               + [last[:, :cut], jnp.zeros((last.shape[0], GAP), BF16), last[:, cut:]], axis=1)

    W_gapped = _gather_finish("gather_w_in", w_in_started, [u, u_t, m_in, v_in], 8, arrange=gapped)
    W_gapped = W_gapped.reshape(D_MODEL, NP_COLS)
    p = _mm("in_proj", u, W_gapped, "nn", dep=w_out_started[4])

    mu = rw_shift_mu
    zpad = jnp.zeros((RW_LORA, RW_WIDTH), F32)
    rw_params = [mu[:, 0:1024], mu[:, 1024:2048], mu[:, 2048:3072], mu[:, 3072:3200], rw_w0,
                 jnp.concatenate([w2, zpad], axis=0), rw_a0, jnp.concatenate([zpad, a2], axis=0), rw_k_k, rw_k_a]
    rw_rows = [_row(p, 1024, OFF_R), _row(p, 1024, OFF_K), _row(p, 1024, OFF_V), _row(p, 128, OFF_S1),
               _row(p, 1024, OFF_GATE)]
    rw_pars = rw_params + [rw_r_k, rw_gn_w, rw_gn_b]
    rw_saved_shapes = [(N_GROUPS, CHUNK, GROUP), (N_GROUPS, CHUNK, GROUP), (N_GROUPS, 2 * CHUNK, 2 * GROUP),
                       (N_GROUPS, 2 * CHUNK, GROUP)]

    dn_rows = [_row(p, 1024, OFF_DQ), _row(p, 1024, OFF_DK), _row(p, 1024, OFF_DV), _row(p, 128, OFF_S2),
               _row(p, 1024, OFF_Z)]
    dn_pars = [conv_w[j:j + 1, 1024 * i:1024 * (i + 1)] for i in range(3) for j in range(CONV_W)]
    narrow = lambda t: jnp.pad(t, ((0, 0), (DN_HEADS, DN_HEAD - 2 * DN_HEADS)))
    dn_pars += [narrow(dn_A_log), narrow(dn_dt_bias), jnp.tile(dn_norm_w, (1, DN_HEADS))]
    dn_saved_shapes = [(N_GROUPS, CHUNK, CHUNK), (N_GROUPS, CHUNK, GROUP)]
    mixer_parts = [(_rw_fused_steps, rw_rows, 4, rw_pars, rw_saved_shapes),
                   (_dn_fused_steps, dn_rows, 3, dn_pars, dn_saved_shapes)]
    y, mixer_kept = _mixers_fwd("mixers_fwd", mixer_parts)
    w_out_all = _gather_finish("gather_w_out", w_out_started, y, 8)
    Wo = jnp.concatenate([w_out_all[:, s].reshape(D_MODEL // N_CHIPS, D_MODEL) for s in range(N_CHIPS)], axis=0)
    yw = _mm("out_proj", y, Wo, "nn", tn=1024)
    loss_acc, dz, dz16, d_fw = _loss_head(h, yw, loss_target[0], final_norm_w.reshape(1, D_MODEL))
    loss = lax.psum(loss_acc[0, 0], ("x", "y", "c"))

    d_wo = _mm("d_w_out", y.T, dz16, "nn", tm=512, tn=1024, tk=Lp)
    wo_started = _reduce_to_shard("rs_w_out", d_wo.reshape(N_CHIPS, D_MODEL // N_CHIPS, D_MODEL), 8, overlap=True)
    dy = _mm("d_out_proj", dz16, Wo, "nt", tn=1024, tk=2048, dep=wo_started[0][4])

    dp, (d_rw_pars, d_dn_pars) = _mixers_bwd("mixers_bwd", mixer_parts, mixer_kept, dy)
    d_prep_pars, d_rw_pars = d_rw_pars[:10], d_rw_pars[10:]
    d_conv_parts = [jnp.concatenate(d_dn_pars[4 * i:4 * i + 4], axis=0) for i in range(3)]
    d_a_log_b, d_dt_b = d_dn_pars[12], d_dn_pars[13]
    d_dn_nw = jnp.sum(d_dn_pars[14].reshape(DN_HEADS, DN_HEAD), axis=0, keepdims=True)
    d_W = _mm("d_w_in", u_t, dp, "nn", tm=1024, tn=768, tk=Lp)
    wi_started = _reduce_to_shard("rs_w_in", d_W, 8, by_columns=True, overlap=True)
    du = _mm("d_in_proj", dp, W_gapped, "nt", tn=2048, tk=1408, dep=wi_started[0][4])
    norm_res = lambda h_, w_: (_rms_fn(h_, w_)[0], h_)
    (dx,), (d_norm_w_x,) = _rowwise_bwd("rms_in_bwd", norm_res, [_row(h)], [norm_w], [_row(du), _row(dz)],
                                        tm=2 * GROUP, row0=N_META, n_rows=S)
    (d_meta,), (d_norm_w_meta,) = _rowwise_bwd("rms_meta_bwd", norm_res, [_row(h)], [norm_w], [_row(du), _row(dz)],
                                               n_rows=N_META)
    d_norm_w = d_norm_w_x + d_norm_w_meta
    grad_x = dx[None]

    d_mu = jnp.concatenate(d_prep_pars[0:4], axis=1)
    d_w2, d_a2 = d_prep_pars[5][:RW_LORA], d_prep_pars[7][RW_LORA:]
    d_conv = jnp.concatenate(d_conv_parts, axis=1)
    head_sum = lambda t: t[:, DN_HEADS:2 * DN_HEADS]
    rep_names = ["norm_w", "rw_shift_mu", "rw_w0", "rw_a0", "rw_k_k", "rw_k_a", "rw_r_k", "rw_gn_w", "rw_gn_b",
                 "dn_A_log", "dn_dt_bias", "dn_norm_w", "final_norm_w"]
    rep_g = [d_norm_w, d_mu, d_prep_pars[4], d_prep_pars[6], d_prep_pars[8], d_prep_pars[9],
             d_rw_pars[0], d_rw_pars[1], d_rw_pars[2],
             head_sum(d_a_log_b), head_sum(d_dt_b), d_dn_nw, d_fw.reshape(D_MODEL)]
    rep_w = [norm_w, rw_shift_mu, rw_w0, rw_a0, rw_k_k, rw_k_a, rw_r_k, rw_gn_w, rw_gn_b, dn_A_log, dn_dt_bias,
             dn_norm_w, final_norm_w]
    rep_m = [m_norm_w, m_rw_shift_mu, m_rw_w0, m_rw_a0, m_rw_k_k, m_rw_k_a, m_rw_r_k, m_rw_gn_w, m_rw_gn_b,
             m_dn_A_log, m_dn_dt_bias, m_dn_norm_w, m_final_norm_w]
    rep_v = [v_norm_w, v_rw_shift_mu, v_rw_w0, v_rw_a0, v_rw_k_k, v_rw_k_a, v_rw_r_k, v_rw_gn_w, v_rw_gn_b,
             v_dn_A_log, v_dn_dt_bias, v_dn_norm_w, v_final_norm_w]
    rep_shapes = [t.shape for t in rep_w]
    rep_all = _exchange("gather_rep_grads", _pack(rep_g, 128), ALL_MASKS, "dev", False)
    rep_out = _adamw("adam_rep", rep_all, _pack(rep_w, 128), _pack(rep_m, 128), _pack(rep_v, 128))
    rep_out = [dict(zip(rep_names, _unpack(t, rep_shapes))) for t in rep_out]

    sm_slabs = jnp.stack([_pack([d_w2[:, 256 * s:256 * (s + 1)], d_a2[:, 256 * s:256 * (s + 1)],
                                 d_conv[:, 768 * s:768 * (s + 1)], d_meta[:, 512 * s:512 * (s + 1)]], 1024, 64)
                          for s in range(N_CHIPS)])
    sm_parts = _reduce_to_shard("rs_small", sm_slabs, 1)
    sm_w = [rw_w2[0], rw_a2[0], dn_conv_w[0], meta_tokens]
    sm_m = [m_rw_w2[0], m_rw_a2[0], m_dn_conv_w[0], m_meta_tokens]
    sm_v = [v_rw_w2[0], v_rw_a2[0], v_dn_conv_w[0], v_meta_tokens]
    sm_out = _adamw("adam_small", sm_parts, _pack(sm_w, 1024, 64), _pack(sm_m, 1024, 64), _pack(sm_v, 1024, 64))
    sm_names = ["rw_w2", "rw_a2", "dn_conv_w", "meta_tokens"]
    sm_full_shapes = [(1, RW_LORA, 256), (1, RW_LORA, 256), (1, CONV_W, 768), (N_META, 512)]
    sm_out = [dict(zip(sm_names, [t.reshape(shp) for t, shp in zip(_unpack(o, small_shapes), sm_full_shapes)]))
              for o in sm_out]

    wo_parts = _reduce_finish("rs_w_out", wo_started, dp, 8)
    wo_out = _adamw("adam_w_out", wo_parts, w_out[0], m_w_out[0], v_w_out[0])
    wi_parts = _reduce_finish("rs_w_in", wi_started, dx, 8)
    wi_out = _adamw("adam_w_in", wi_parts, w_in[0], m_in, v_in)

    order = ["meta_tokens", "norm_w", "w_in", "rw_shift_mu", "rw_w0", "rw_w2", "rw_a0", "rw_a2", "rw_k_k", "rw_k_a",
             "rw_r_k", "rw_gn_w", "rw_gn_b", "dn_conv_w", "dn_A_log", "dn_dt_bias", "dn_norm_w", "w_out",
             "final_norm_w"]
    outs = [loss, grad_x]
    for kind in range(4):
        table = dict(rep_out[kind])
        table.update(sm_out[kind])
        table["w_in"] = wi_out[kind][None]
        table["w_out"] = wo_out[kind][None]
        outs += [table[n] for n in order]
    return tuple(outs)
```

```python
import functools

import jax
import jax.numpy as jnp
from jax import lax
from jax.experimental import pallas as pl
from jax.experimental.pallas import tpu as pltpu

F32 = jnp.float32
BF16 = jnp.bfloat16
HIGH = lax.Precision.HIGH
HIGHEST = lax.Precision.HIGHEST
MESH = pl.DeviceIdType.MESH

D_MODEL = 2048
N_META = 16
RW_WIDTH = 1024
RW_HEAD = 64
RW_HEADS = 16
RW_LORA = 64
RW_GN_EPS = 64e-5
DN_WIDTH = 1024
DN_HEAD = 128
DN_HEADS = 8
CONV_W = 4
CHUNK = 64
NORM_EPS = 1e-6
IN_COLS = 8336
N_CHIPS = 4
SHARD_COLS = IN_COLS // N_CHIPS

NP_COLS = 8 * 1024 + 256
GAP_AT = IN_COLS - DN_WIDTH
GAP = NP_COLS - IN_COLS
OFF_R, OFF_K, OFF_V, OFF_S1, OFF_GATE = 0, 1024, 2048, 3072, 3200
OFF_DQ, OFF_DK, OFF_DV, OFF_S2, OFF_Z = 4224, 5248, 6272, 7296, 7424

ADAM_LR = 0.001
ADAM_B1 = 0.9
ADAM_B2 = 0.999
ADAM_EPS = 1e-08
ADAM_WD = 0.01
ADAM_STEP = 10

VMEM_LIMIT_BYTES = 56 * 1024 * 1024
ROW_TILE = 104
MM_ROW_TILE = 832


def _params(sem=None):
    return pltpu.CompilerParams(dimension_semantics=sem, vmem_limit_bytes=VMEM_LIMIT_BYTES)


def _pick(n, target, mult=8):
    best = None
    for d in range(mult, min(n, target) + 1, mult):
        if n % d == 0:
            best = d
    return n if best is None else best


def _mm(name, a, b, mode, tm=MM_ROW_TILE, tn=1408, tk=2048, dep=None):
    if mode == "nn":
        (M, K), (_, N) = a.shape, b.shape
    elif mode == "nt":
        (M, K), (N, _) = a.shape, b.shape
    else:
        (K, M), (_, N) = a.shape, b.shape
    tm = _pick(M, tm, 128 if mode == "tn" else 16)
    tn = _pick(N, tn, 128)
    tk = _pick(K, tk, 8 if mode == "tn" else 128)
    if mode == "nn":
        a_spec = pl.BlockSpec((tm, tk), lambda i, j, k: (i, k))
        b_spec = pl.BlockSpec((tk, tn), lambda i, j, k: (k, j))
        dims = (((1,), (0,)), ((), ()))
    elif mode == "nt":
        a_spec = pl.BlockSpec((tm, tk), lambda i, j, k: (i, k))
        b_spec = pl.BlockSpec((tn, tk), lambda i, j, k: (j, k))
        dims = (((1,), (1,)), ((), ()))
    else:
        a_spec = pl.BlockSpec((tk, tm), lambda i, j, k: (k, i))
        b_spec = pl.BlockSpec((tk, tn), lambda i, j, k: (k, j))
        dims = (((0,), (0,)), ((), ()))

    def body(a_ref, b_ref, *rest):
        o_ref = rest[-1]

        @pl.when(pl.program_id(2) == 0)
        def _():
            o_ref[...] = jnp.zeros_like(o_ref)

        o_ref[...] += lax.dot_general(a_ref[...].astype(BF16), b_ref[...].astype(BF16), dims,
                                      preferred_element_type=F32)

    deps = [] if dep is None else [dep]
    return pl.pallas_call(
        body, name=name, grid=(M // tm, N // tn, K // tk),
        in_specs=[a_spec, b_spec] + [pl.BlockSpec((8, 128), lambda i, j, k: (0, 0))] * len(deps),
        out_specs=pl.BlockSpec((tm, tn), lambda i, j, k: (i, j)),
        out_shape=jax.ShapeDtypeStruct((M, N), F32),
        compiler_params=_params(("parallel", "parallel", "arbitrary")),
    )(a, b, *deps)


def _row(arr, width=None, cb=0):
    return (arr, arr.shape[1] if width is None else width, cb)


def _rowwise(name, fn, rows, params, out_widths, tm=ROW_TILE, out_dtype=F32):
    R = rows[0][0].shape[0]
    tm = _pick(R, tm, 16 if out_dtype == BF16 else 8)
    n_r, n_p = len(rows), len(params)

    def body(*refs):
        vals = [r[...] for r in refs[:n_r + n_p]]
        for o_ref, val in zip(refs[n_r + n_p:], fn(*vals)):
            o_ref[...] = val.astype(out_dtype)

    in_specs = [pl.BlockSpec((tm, w), lambda i, cb=cb: (i, cb)) for (_, w, cb) in rows]
    in_specs += [pl.BlockSpec(p.shape, lambda i: (0, 0)) for p in params]
    return pl.pallas_call(
        body, name=name, grid=(R // tm,), in_specs=in_specs,
        out_specs=[pl.BlockSpec((tm, w), lambda i: (i, 0)) for w in out_widths],
        out_shape=[jax.ShapeDtypeStruct((R, w), out_dtype) for w in out_widths],
        compiler_params=_params(("parallel",)),
    )(*[r[0] for r in rows], *params)


def _rowwise_bwd(name, fn, rows, params, douts, tm=ROW_TILE, row0=0, n_rows=None):
    R = rows[0][0].shape[0] - row0 if n_rows is None else n_rows
    tm = _pick(R, tm)
    assert row0 % 8 == 0 and row0 + R <= rows[0][0].shape[0]

    def in_spec(w, cb):
        if row0 == 0:
            return pl.BlockSpec((tm, w), lambda i: (i, cb))
        return pl.BlockSpec((pl.Element(tm), pl.Element(w)), lambda i: (pl.multiple_of(row0 + i * tm, 8), cb * w))

    n_r, n_p, n_d = len(rows), len(params), len(douts)

    def body(*refs):
        vals = [r[...] for r in refs[:n_r + n_p]]
        cts = tuple(r[...] for r in refs[n_r + n_p:n_r + n_p + n_d])
        grads = jax.vjp(fn, *vals)[1](cts)
        outs = refs[n_r + n_p + n_d:]
        for o_ref, g in zip(outs[:n_r], grads[:n_r]):
            o_ref[...] = g

        @pl.when(pl.program_id(0) == 0)
        def _():
            for o_ref in outs[n_r:]:
                o_ref[...] = jnp.zeros_like(o_ref)

        for o_ref, g in zip(outs[n_r:], grads[n_r:]):
            o_ref[...] += g

    in_specs = [in_spec(w, cb) for (_, w, cb) in rows]
    in_specs += [pl.BlockSpec(p.shape, lambda i: (0, 0)) for p in params]
    in_specs += [in_spec(w, cb) for (_, w, cb) in douts]
    out_specs = [pl.BlockSpec((tm, w), lambda i: (i, 0)) for (_, w, _) in rows]
    out_specs += [pl.BlockSpec(p.shape, lambda i: (0, 0)) for p in params]
    out_shape = [jax.ShapeDtypeStruct((R, w), F32) for (_, w, _) in rows]
    out_shape += [jax.ShapeDtypeStruct(p.shape, F32) for p in params]
    res = pl.pallas_call(
        body, name=name, grid=(R // tm,), in_specs=in_specs, out_specs=out_specs, out_shape=out_shape,
        compiler_params=_params(("arbitrary",)),
    )(*[r[0] for r in rows], *params, *[d[0] for d in douts])
    return res[:n_r], res[n_r:]


def _softplus(x):
    return jnp.maximum(x, 0.0) + jnp.log(1.0 + jnp.exp(-jnp.abs(x)))


def _silu(x):
    return x * jax.nn.sigmoid(x)


def _rms_fn(h, w):
    return (h * lax.rsqrt(jnp.mean(h * h, axis=-1, keepdims=True) + NORM_EPS) * w,)


def _rw_prep_fn(pr, pr1, pk, pk1, pv, pv1, ps, ps1, mu_r, mu_k, mu_v, mu_s, w0, w2p, a0, a2p, k_k, k_a):
    r = pr + (pr1 - pr) * mu_r
    k = pk + (pk1 - pk) * mu_k
    v = pv + (pv1 - pv) * mu_v
    s = ps + (ps1 - ps) * mu_s
    lora = lambda x, w: jnp.dot(x.astype(BF16), w.astype(BF16), preferred_element_type=F32)
    w_log = -_softplus(-(w0 + lora(jnp.tanh(s), w2p))) - 0.5
    log_decay = -jnp.exp(w_log)
    a = jax.nn.sigmoid(a0 + lora(s, a2p))
    return r, log_decay, k * (1.0 + (a - 1.0) * k_a), v, k * k_k, a


def _conv_fn(u0, u1, u2, u3, w0, w1, w2, w3):
    return (_silu(u0 * w3 + u1 * w2 + u2 * w1 + u3 * w0),)


def _dn_gate_fn(ps2, a_log_n, dt_n):
    beta_n = jax.nn.sigmoid(ps2)
    g_n = -jnp.exp(a_log_n) * _softplus(ps2 + dt_n)
    lane = lax.broadcasted_iota(jnp.int32, (1, DN_HEAD), 1)
    pick = lambda x, j: jnp.broadcast_to(jnp.sum(x * (lane == j).astype(F32), axis=-1, keepdims=True), x.shape)[None]
    beta = jnp.concatenate([pick(beta_n, h) for h in range(DN_HEADS)], axis=0)
    g = jnp.concatenate([pick(g_n, DN_HEADS + h) for h in range(DN_HEADS)], axis=0)
    return beta, g


def _tril_masks(c):
    t = lax.broadcasted_iota(jnp.int32, (c, c), 0)
    s = lax.broadcasted_iota(jnp.int32, (c, c), 1)
    return s <= t, s < t


def _dot(x, y, cx, cy, prec=HIGH):
    nb = x.ndim - 2
    batch = tuple(range(nb))
    return lax.dot_general(x, y, (((cx + nb,), (cy + nb,)), (batch, batch)), precision=prec,
                           preferred_element_type=F32)


INV_BASE = 8


def _run_interleaved(running):
    results, live = [None] * len(running), list(range(len(running)))
    while live:
        for k in list(live):
            try:
                next(running[k])
            except StopIteration as done:
                results[k] = done.value
                live.remove(k)
    return results


def _blocked_inv(a, mm, row, col, mm_merge=None):
    mm_merge = mm if mm_merge is None else mm_merge
    shift = INV_BASE.bit_length() - 1
    d = jnp.where(jnp.right_shift(row, shift) == jnp.right_shift(col, shift), a, 0.0)
    x = (row == col).astype(F32) - d
    p = mm(d, d)
    yield
    x = x + mm(x, p)
    yield
    p = mm(p, p)
    yield
    x = x + mm(x, p)
    yield
    size = INV_BASE
    while size < a.shape[-2]:
        same_pair = jnp.right_shift(row, shift + 1) == jnp.right_shift(col, shift + 1)
        lower_left = (jnp.right_shift(row, shift) & 1 == 1) & (jnp.right_shift(col, shift) & 1 == 0)
        lower = mm_merge(jnp.where(same_pair & lower_left, a, 0.0), x)
        yield
        x = x - mm_merge(x, lower)
        size, shift = 2 * size, shift + 1
        yield
    return x


def _tri_inv(a):
    c = a.shape[-1]
    row, col = lax.broadcasted_iota(jnp.int32, (c, c), 0), lax.broadcasted_iota(jnp.int32, (c, c), 1)
    return _blocked_inv(a, lambda x, y: _dot(x, y, 1, 0), row, col,
                        lambda x, y: _dot(x.astype(BF16), y.astype(BF16), 1, 0, None))


@jax.custom_vjp
def _tri_inv_saved(a, t):
    return t


def _tri_inv_saved_fwd(a, t):
    return t, t


def _tri_inv_saved_bwd(t, dt):
    return -_dot(_dot(t, dt, 0, 0), t, 1, 1), jnp.zeros_like(t)


_tri_inv_saved.defvjp(_tri_inv_saved_fwd, _tri_inv_saved_bwd)


def _pair_masks():
    lane = lax.broadcasted_iota(jnp.int32, (1, 2 * RW_HEAD), 1)
    m0 = (lane < RW_HEAD).astype(F32)
    return m0, 1.0 - m0


def _pair_bd(x):
    m0, m1 = _pair_masks()
    return jnp.concatenate([x * m0, x * m1], axis=-2)


def _pair_mm(x, y):
    return _dot(x, _pair_bd(y), 1, 0)


def _pair_inv(a):
    c = a.shape[-2]
    row = lax.broadcasted_iota(jnp.int32, (c, 2 * RW_HEAD), 0)
    col = lax.broadcasted_iota(jnp.int32, (c, 2 * RW_HEAD), 1) & (RW_HEAD - 1)
    return _blocked_inv(a, _pair_mm, row, col,
                        lambda x, y: _dot(x.astype(BF16), _pair_bd(y).astype(BF16), 1, 0, None))


@jax.custom_vjp
def _pair_inv_saved(a, t):
    return t


def _pair_inv_saved_fwd(a, t):
    return t, t


def _pair_inv_saved_bwd(t, dt):
    m0, m1 = _pair_masks()
    c = t.shape[-2]
    z = _dot(t, dt, 0, 0)
    x = z[:, :c, :] * m0 + z[:, c:, :] * m1
    return -_dot(x, _pair_bd(t), 1, 1), jnp.zeros_like(t)


_pair_inv_saved.defvjp(_pair_inv_saved_fwd, _pair_inv_saved_bwd)


@jax.custom_vjp
def _use_saved(x, x_saved):
    return x_saved


_use_saved.defvjp(lambda x, x_saved: (x_saved, None), lambda _, g: (g, jnp.zeros_like(g)))


def _rw_chunk_steps(S, r, lw, k2, v, kkp, a, gate, rk, gnw, gnb, saved=None):
    B, C, P = r.shape
    m0, m1 = _pair_masks()
    seg = lambda x: (jnp.sum(x * m0, axis=-1, keepdims=True) * m0 + jnp.sum(x * m1, axis=-1, keepdims=True) * m1)
    mm = lambda x, y: _dot(x.astype(BF16), y.astype(BF16), 1, 0, None)
    nt = lambda x, y: _dot(x, y, 1, 1)
    tn = lambda x, y: _dot(x, y, 0, 0)
    pair_mm = lambda x, y: mm(x, _pair_bd(y))
    t_idx = lax.broadcasted_iota(jnp.int32, (C, P), 0)
    s_idx = lax.broadcasted_iota(jnp.int32, (C, P), 1) & (RW_HEAD - 1)
    incl, strict = s_idx <= t_idx, s_idx < t_idx
    tril = jnp.broadcast_to(_tril_masks(C)[0].astype(F32), (B, C, C))
    kk = kkp * lax.rsqrt(seg(kkp * kkp) + 1e-6)
    b = kk * a
    reuse = (lambda x, i: x) if saved is None else (lambda x, i: _use_saved(x, saved[i]))
    g_incl = reuse(_dot(tril, lw, 1, 0), 1)
    g_excl = g_incl - lw
    inv = jnp.exp(-g_incl)
    alpha, beta, kappa, rho = kk * jnp.exp(g_excl), b * inv, k2 * inv, r * jnp.exp(g_incl)
    yield
    ar = jnp.concatenate([alpha, rho], axis=-2)
    scores = reuse(nt(ar, jnp.concatenate([_pair_bd(beta), _pair_bd(kappa)], axis=-2)), 2)
    a_ab = jnp.where(strict, scores[:, :C, :P], 0.0)
    a_ak = jnp.where(strict, scores[:, :C, P:], 0.0)
    r_b = jnp.where(incl, scores[:, C:, :P], 0.0)
    r_k = jnp.where(incl, scores[:, C:, P:], 0.0)
    yield
    t_inv = (yield from _pair_inv(a_ab)) if saved is None else _pair_inv_saved(a_ab, saved[0])
    on_state = reuse(nt(ar, S), 3)
    yield
    u = pair_mm(t_inv, -on_state[:, :C, :] - pair_mm(a_ak, v))
    yield
    y = on_state[:, C:, :] + mm(jnp.concatenate([r_b, r_k], axis=-1),
                                jnp.concatenate([_pair_bd(u), _pair_bd(v)], axis=-2))
    same_head = ((lax.broadcasted_iota(jnp.int32, (P, P), 0) < RW_HEAD)
                 == (lax.broadcasted_iota(jnp.int32, (P, P), 1) < RW_HEAD))
    yield
    fresh = tn(jnp.concatenate([u, v], axis=-2), jnp.concatenate([beta, kappa], axis=-2))
    yield
    S_new = jnp.exp(jnp.sum(lw, axis=-2, keepdims=True)) * (S + jnp.where(same_head, fresh, 0.0))
    dev = y - seg(y) * (1.0 / RW_HEAD)
    yn = dev * lax.rsqrt(seg(dev * dev) * (1.0 / RW_HEAD) + RW_GN_EPS) * gnw + gnb
    bonus = seg(r * k2 * rk) * v
    return (yn + bonus) * _silu(gate), S_new, (t_inv, g_incl, scores, on_state)


def _dn_chunk_steps(S, qc, kc, vc, bb, gb, z, nw, saved=None):
    B, C, D = qc.shape
    mm = lambda x, y: _dot(x.astype(BF16), y.astype(BF16), 1, 0, None)
    nt = lambda x, y, p=None: _dot(x, y, 1, 1, p) if p else _dot(x.astype(BF16), y.astype(BF16), 1, 1, None)
    tn = lambda x, y: _dot(x.astype(BF16), y.astype(BF16), 0, 0, None)
    incl, strict = _tril_masks(C)
    q = qc * lax.rsqrt(jnp.sum(qc * qc, axis=-1, keepdims=True) + 1e-6) * (D ** -0.5)
    k = kc * lax.rsqrt(jnp.sum(kc * kc, axis=-1, keepdims=True) + 1e-6)
    kb, vb = k * bb, vc * bb
    yield
    G = _dot(jnp.broadcast_to(incl.astype(F32), (B, C, C)), gb, 1, 0, HIGHEST)
    G = G if saved is None else _use_saved(G, saved[1])
    lane = lax.broadcasted_iota(jnp.int32, (C, D), 1)
    e0, e1 = (lane == 0).astype(F32), (lane == 1).astype(F32)
    diff = nt(G * e0 + e1, e0 - G * e1, HIGHEST)
    dmask = jnp.where(incl, jnp.exp(jnp.where(incl, diff, 0.0)), 0.0)
    M = jnp.where(strict, nt(kb, k) * dmask, 0.0)
    yield
    T = (yield from _tri_inv(M)) if saved is None else _tri_inv_saved(M, saved[0])
    eG = jnp.exp(G)
    u = mm(T, vb)
    yield
    w = mm(T, kb * eG)
    yield
    attn = jnp.where(incl, nt(q, k) * dmask, 0.0)
    yield
    v_new = u - mm(w, S)
    yield
    o = mm(q * eG, S) + mm(attn, v_new)
    yield
    g_last = jnp.sum(gb, axis=-2, keepdims=True)
    S_new = S * jnp.exp(jnp.broadcast_to(g_last, S.shape)) + tn(k * jnp.exp(g_last - G), v_new)
    on = o * lax.rsqrt(jnp.mean(o * o, axis=-1, keepdims=True) + NORM_EPS) * nw
    return on * _silu(z), S_new, (T, G)


N_GROUPS = 8
GROUP = 128
HALO = 8


def _groups(x):
    return jnp.concatenate([x[:, g * GROUP:(g + 1) * GROUP][None] for g in range(N_GROUPS)], axis=0)


@functools.partial(jax.custom_vjp, nondiff_argnums=(1,))
def _shift_rows(ext, j):
    return pltpu.roll(ext, j, 0)[HALO:, :]


def _shift_rows_fwd(ext, j):
    return _shift_rows(ext, j), None


def _shift_rows_bwd(j, _, d):
    z = jnp.concatenate([jnp.zeros((HALO, d.shape[1]), d.dtype), d], axis=0)
    return (pltpu.roll(z, z.shape[0] - j, 0),)


_shift_rows.defvjp(_shift_rows_fwd, _shift_rows_bwd)


def _rw_fused_steps(S, h_r, h_k, h_v, h_s, p_r, p_k, p_v, p_s, gate, *pars, saved=None):
    prev = lambda h, x: _shift_rows(jnp.concatenate([h, x], axis=0), 1)
    seq = _rw_prep_fn(p_r, prev(h_r, p_r), p_k, prev(h_k, p_k), p_v, prev(h_v, p_v), p_s, prev(h_s, p_s), *pars[:10])
    yield
    return (yield from _rw_chunk_steps(S, *[_groups(t) for t in seq], _groups(gate),
                                       *[_groups(t) for t in pars[10:]], saved=saved))


def _dn_fused_steps(S, h_q, h_k, h_v, p_q, p_k, p_v, p_s2, z, *pars, saved=None):
    conv = []
    for i, (h, x) in enumerate(((h_q, p_q), (h_k, p_k), (h_v, p_v))):
        ext = jnp.concatenate([h, x], axis=0)
        conv += _conv_fn(x, _shift_rows(ext, 1), _shift_rows(ext, 2), _shift_rows(ext, 3), *pars[4 * i:4 * i + 4])
    beta, g = _dn_gate_fn(p_s2, pars[12], pars[13])
    yield
    return (yield from _dn_chunk_steps(S, *[_groups(t) for t in conv], beta, g, _groups(z), _groups(pars[14]),
                                       saved=saved))


def _row_specs(rows, n_halo, row_of):
    C = CHUNK
    specs = [pl.BlockSpec((pl.Element(C), pl.Element(w)), lambda c, off=off: (row_of(c) * C, off))
             for (_, w, off) in rows]
    specs += [pl.BlockSpec((pl.Element(HALO), pl.Element(w)),
                           lambda c, off=off: (pl.multiple_of(jnp.maximum(row_of(c) * C - HALO, 0), HALO), off))
              for (_, w, off) in rows[:n_halo]]
    return specs


def _mixers_fwd(name, parts):
    Lp = parts[0][1][0][0].shape[0]
    C, D, NB = CHUNK, GROUP, N_GROUPS
    nc, width = Lp // C, NB * D
    n_in = [len(rows) + n_halo + len(pars) for (_, rows, n_halo, pars, _) in parts]
    n_out = [1 + len(shapes) for (*_, shapes) in parts]

    def body(*refs):
        ins, y_ref = refs[:sum(n_in)], refs[sum(n_in)]
        outs, states = refs[sum(n_in) + 1:sum(n_in) + 1 + sum(n_out)], refs[sum(n_in) + 1 + sum(n_out):]
        first = pl.program_id(0) == 0

        @pl.when(first)
        def _():
            for s_ref in states:
                s_ref[...] = jnp.zeros_like(s_ref)

        i_at, running = 0, []
        for k, (steps, rows, n_halo, pars, shapes) in enumerate(parts):
            n_r = len(rows)
            row_refs, halo_refs = ins[i_at:i_at + n_r], ins[i_at + n_r:i_at + n_r + n_halo]
            par_refs = ins[i_at + n_r + n_halo:i_at + n_in[k]]
            S = states[k][...]
            halos = [jnp.where(first, 0.0, r[...]) for r in halo_refs]
            running.append(steps(S, *halos, *[r[...] for r in row_refs], *[r[...] for r in par_refs]))
            i_at += n_in[k]
        results = _run_interleaved(running)
        o_at = 0
        for k, (y, S_new, saved) in enumerate(results):
            ck_ref, saved_refs = outs[o_at], outs[o_at + 1:o_at + n_out[k]]
            ck_ref[0] = states[k][...]
            for g in range(NB):
                y_ref[:, k * width + g * D:k * width + (g + 1) * D] = y[g].astype(BF16)
            for o_ref, val in zip(saved_refs, saved):
                o_ref[0] = val
            states[k][...] = S_new
            o_at += n_out[k]

    in_specs, args, out_specs, out_shape = [], [], [], []
    for (_, rows, n_halo, pars, shapes) in parts:
        in_specs += _row_specs(rows, n_halo, lambda c: c) + [pl.BlockSpec(p.shape, lambda c: (0, 0)) for p in pars]
        args += [r[0] for r in rows] + [r[0] for r in rows[:n_halo]] + list(pars)
        for shp in [(NB, D, D)] + list(shapes):
            out_specs.append(pl.BlockSpec((1,) + tuple(shp), lambda c: (c, 0, 0, 0)))
            out_shape.append(jax.ShapeDtypeStruct((nc,) + tuple(shp), F32))
    res = pl.pallas_call(
        body, name=name, grid=(nc,), in_specs=in_specs,
        out_specs=[pl.BlockSpec((C, width * len(parts)), lambda c: (c, 0))] + out_specs,
        out_shape=[jax.ShapeDtypeStruct((Lp, width * len(parts)), BF16)] + out_shape,
        scratch_shapes=[pltpu.VMEM((NB, D, D), F32)] * len(parts),
        compiler_params=_params(("arbitrary",)),
    )(*args)
    per_part, at = [], 1
    for n in n_out:
        per_part.append(list(res[at:at + n]))
        at += n
    return res[0], per_part


def _mixers_bwd(name, parts, kept, dy):
    Lp = parts[0][1][0][0].shape[0]
    C, D, NB = CHUNK, GROUP, N_GROUPS
    nc = Lp // C
    n_in = [len(rows) + n_halo + len(pars) + len(kept[k]) for k, (_, rows, n_halo, pars, _) in enumerate(parts)]
    widths = [sum(w for (_, w, _) in rows) for (_, rows, *_) in parts]
    n_par = [len(pars) for (_, _, _, pars, _) in parts]
    n_scr = [1 + n_halo for (_, _, n_halo, _, _) in parts]
    rev = lambda c: nc - 1 - c

    def body(*refs):
        ins, dy_ref = refs[:sum(n_in)], refs[sum(n_in)]
        drows_ref = refs[sum(n_in) + 1]
        dpar_all = refs[sum(n_in) + 2:sum(n_in) + 2 + sum(n_par)]
        scratch = refs[sum(n_in) + 2 + sum(n_par):]
        i = pl.program_id(0)

        @pl.when(i == 0)
        def _():
            for o_ref in list(dpar_all) + list(scratch):
                o_ref[...] = jnp.zeros_like(o_ref)

        i_at, vals, stored, cts, sizes = 0, [], [], [], []
        for k, (_, rows, n_halo, pars, _) in enumerate(parts):
            n_r = len(rows)
            row_refs, halo_refs = ins[i_at:i_at + n_r], ins[i_at + n_r:i_at + n_r + n_halo]
            par_refs = ins[i_at + n_r + n_halo:i_at + n_r + n_halo + n_par[k]]
            kept_refs = ins[i_at + n_r + n_halo + n_par[k]:i_at + n_in[k]]
            halos = [jnp.where(i == nc - 1, 0.0, r[...]) for r in halo_refs]
            part_vals = [kept_refs[0][0]] + halos + [r[...] for r in row_refs] + [r[...] for r in par_refs]
            vals += part_vals
            sizes.append(len(part_vals))
            stored.append(tuple(r[0] for r in kept_refs[1:]))
            cts.append((_groups(dy_ref[:, k * NB * D:(k + 1) * NB * D]), scratch[sum(n_scr[:k])][...]))
            i_at += n_in[k]

        def both(*flat):
            running, at = [], 0
            for k, (steps, *_) in enumerate(parts):
                running.append(steps(*flat[at:at + sizes[k]], saved=stored[k]))
                at += sizes[k]
            return tuple(r[:2] for r in _run_interleaved(running))

        all_grads = jax.vjp(both, *vals)[1](tuple(cts))
        i_at = p_at = s_at = g_at = col = 0
        for k, (_, rows, n_halo, pars, _) in enumerate(parts):
            n_r = len(rows)
            grads = all_grads[g_at:g_at + sizes[k]]
            dpar_refs = dpar_all[p_at:p_at + n_par[k]]
            ds_ref, carry_refs = scratch[s_at], scratch[s_at + 1:s_at + n_scr[k]]
            ds_ref[...] = grads[0]
            d_halos, d_rows = grads[1:1 + n_halo], grads[1 + n_halo:1 + n_halo + n_r]
            for j, g in enumerate(d_rows):
                if j < n_halo:
                    g = g + jnp.concatenate([jnp.zeros((C - HALO, g.shape[1]), F32), carry_refs[j][...]], axis=0)
                    carry_refs[j][...] = d_halos[j]
                drows_ref[:, col:col + g.shape[1]] = g.astype(BF16)
                col += g.shape[1]
            for o_ref, g in zip(dpar_refs, grads[1 + n_halo + n_r:]):
                o_ref[...] += g
            p_at, s_at, g_at = p_at + n_par[k], s_at + n_scr[k], g_at + sizes[k]

    in_specs, args, out_specs, out_shape, scratch_shapes = [], [], [], [], []
    for k, (_, rows, n_halo, pars, _) in enumerate(parts):
        in_specs += _row_specs(rows, n_halo, rev) + [pl.BlockSpec(p.shape, lambda c: (0, 0)) for p in pars]
        in_specs += [pl.BlockSpec((1,) + tuple(t.shape[1:]), lambda c: (rev(c), 0, 0, 0)) for t in kept[k]]
        args += [r[0] for r in rows] + [r[0] for r in rows[:n_halo]] + list(pars) + list(kept[k])
        out_specs += [pl.BlockSpec(p.shape, lambda c: (0, 0)) for p in pars]
        out_shape += [jax.ShapeDtypeStruct(p.shape, F32) for p in pars]
        scratch_shapes += [pltpu.VMEM((NB, D, D), F32)] + [pltpu.VMEM((HALO, w), F32) for (_, w, _) in rows[:n_halo]]
    res = pl.pallas_call(
        body, name=name, grid=(nc,),
        in_specs=in_specs + [pl.BlockSpec((C, dy.shape[1]), lambda c: (rev(c), 0))],
        out_specs=[pl.BlockSpec((C, sum(widths)), lambda c: (rev(c), 0))] + out_specs,
        out_shape=[jax.ShapeDtypeStruct((Lp, sum(widths)), BF16)] + out_shape,
        scratch_shapes=scratch_shapes, compiler_params=_params(("arbitrary",)),
    )(*args, dy)
    d_pars, at = [], 1
    for n in n_par:
        d_pars.append(list(res[at:at + n]))
        at += n
    return res[0], d_pars


def _loss_head(h, y, wo, target, fw, tm=2 * ROW_TILE):
    Lp, Dm = h.shape
    Dk = y.shape[1]
    S = target.shape[0]
    n_real = N_META + S
    tm = _pick(Lp, tm, 16)
    n_tiles = Lp // tm
    assert n_tiles >= 2 and N_META < tm <= S and Lp - n_real < tm and (S - tm) % 8 == 0 and N_META % 8 == 0

    def out_fn(z, fw_):
        return z * lax.rsqrt(jnp.mean(z * z, axis=-1, keepdims=True) + NORM_EPS) * fw_

    def body(h_ref, y_ref, wo_ref, t_ref, fw_ref, loss_ref, dz_ref, dz16_ref, dfw_ref):
        i = pl.program_id(0)

        @pl.when(i == 0)
        def _():
            loss_ref[...] = jnp.zeros_like(loss_ref)
            dfw_ref[...] = jnp.zeros_like(dfw_ref)

        row = i * tm + lax.broadcasted_iota(jnp.int32, (tm, 1), 0)
        mask = ((row >= N_META) & (row < n_real)).astype(F32)
        z = h_ref[...] + jnp.dot(y_ref[...], wo_ref[...], preferred_element_type=F32)
        o, vjp = jax.vjp(out_fn, z, fw_ref[...])
        t = t_ref[...]
        t = jnp.where(i == 0, pltpu.roll(t, N_META, 0), t)
        if Lp > n_real:
            t = jnp.where(i == n_tiles - 1, pltpu.roll(t, tm - (Lp - n_real), 0), t)
        err = (o - t) * mask
        row_loss = 0.5 * jnp.mean(jnp.square(err), axis=-1, keepdims=True)
        dz, dfw = vjp(err * (1.0 / Dm))
        loss_ref[...] += jnp.sum(row_loss, axis=0, keepdims=True)
        dz_ref[...] = dz
        dz16_ref[...] = dz.astype(BF16)
        dfw_ref[...] += dfw

    row_spec = pl.BlockSpec((tm, Dm), lambda i: (i, 0))
    target_spec = pl.BlockSpec((pl.Element(tm), pl.Element(Dm)),
                               lambda i: (pl.multiple_of(jnp.clip(i * tm - N_META, 0, S - tm), 8), 0))
    return pl.pallas_call(
        body, name="loss_head", grid=(n_tiles,),
        in_specs=[row_spec, pl.BlockSpec((tm, Dk), lambda i: (i, 0)), pl.BlockSpec((Dk, Dm), lambda i: (0, 0)),
                  target_spec, pl.BlockSpec((1, Dm), lambda i: (0, 0))],
        out_specs=[pl.BlockSpec((8, 128), lambda i: (0, 0)), row_spec, row_spec,
                   pl.BlockSpec((1, Dm), lambda i: (0, 0))],
        out_shape=[jax.ShapeDtypeStruct((8, 128), F32), jax.ShapeDtypeStruct((Lp, Dm), F32),
                   jax.ShapeDtypeStruct((Lp, Dm), BF16), jax.ShapeDtypeStruct((1, Dm), F32)],
        compiler_params=_params(("arbitrary",)),
    )(h, y, wo, target, fw)


def _exchange(name, x, masks, slot_kind, per_dest, n_split=1, copy_own=True, other_half=False):
    n = len(masks)
    keep_own = slot_kind is not None and copy_own
    n_slots = {"chip": 4, "core": 2, "dev": 8, None: n}[slot_kind]
    blk_shape = x.shape[1:] if per_dest else x.shape
    if other_half:
        blk_shape = (x.shape[0] // 2,) + tuple(x.shape[1:])
    rows = blk_shape[0] // n_split

    def body(x_ref, o_ref, send_sems, recv_sems, local_sems):
        mx, my, mc = lax.axis_index("x"), lax.axis_index("y"), lax.axis_index("c")
        if other_half:
            x_ref = x_ref.at[pl.ds((1 - mc) * blk_shape[0], blk_shape[0])]

        def slot(k, px, py, pc):
            return {"chip": 2 * px + py, "core": pc, "dev": 4 * px + 2 * py + pc, None: k}[slot_kind]

        def peer(m):
            return (mx + m[0]) % 2, (my + m[1]) % 2, (mc + m[2]) % 2

        def part(ref, j):
            return ref.at[pl.ds(j * rows, rows)]

        own_src = x_ref.at[2 * mx + my] if per_dest else x_ref
        local = []
        if keep_own:
            own_dst = o_ref.at[slot(0, mx, my, mc)]
            local = [pltpu.make_async_copy(part(own_src, j), part(own_dst, j), local_sems.at[j])
                     for j in range(n_split)]
        for cp in local:
            cp.start()
        sends = []
        for k, m in enumerate(masks):
            px, py, pc = peer(m)
            src = x_ref.at[2 * px + py] if per_dest else x_ref
            dst = o_ref.at[slot(k, mx, my, mc)]
            for j in range(n_split):
                sends.append(pltpu.make_async_remote_copy(
                    src_ref=part(src, j), dst_ref=part(dst, j), send_sem=send_sems.at[k * n_split + j],
                    recv_sem=recv_sems.at[k * n_split + j], device_id=(px, py, pc), device_id_type=MESH))
        for cp in sends:
            cp.start()
        for k, m in enumerate(masks):
            px, py, pc = peer(m)
            landed = o_ref.at[slot(k, px, py, pc)]
            for j in range(n_split):
                pltpu.make_async_remote_copy(
                    src_ref=part(own_src, j), dst_ref=part(landed, j), send_sem=send_sems.at[k * n_split + j],
                    recv_sem=recv_sems.at[k * n_split + j], device_id=(px, py, pc), device_id_type=MESH).wait_recv()
        for cp in sends:
            cp.wait_send()
        for cp in local:
            cp.wait()

    return pl.pallas_call(
        body, name=name,
        in_specs=[pl.BlockSpec(memory_space=pl.ANY)], out_specs=pl.BlockSpec(memory_space=pl.ANY),
        out_shape=jax.ShapeDtypeStruct((n_slots,) + tuple(blk_shape), x.dtype),
        scratch_shapes=[pltpu.SemaphoreType.DMA((n * n_split,)), pltpu.SemaphoreType.DMA((n * n_split,)),
                        pltpu.SemaphoreType.DMA((n_split,))],
        compiler_params=pltpu.CompilerParams(has_side_effects=True),
    )(x)


CHIP_MASKS = [(1, 0, 0), (0, 1, 0), (1, 1, 0)]
CORE_MASKS = [(0, 0, 1)]
ALL_MASKS = [(dx, dy, dc) for dx in (0, 1) for dy in (0, 1) for dc in (0, 1) if (dx, dy, dc) != (0, 0, 0)]

HBM_SPEC = pl.BlockSpec(memory_space=pltpu.HBM)
SEM_SPEC = pl.BlockSpec(memory_space=pltpu.SEMAPHORE)
DATAFLOW = pltpu.SideEffectType.DATAFLOW_SIDE_EFFECTING


def _split_copies(x_ref, land_ref, send_sems, recv_sems, masks, slot_kind, per_dest, n_split, with_recvs=True):
    mx, my, mc = lax.axis_index("x"), lax.axis_index("y"), lax.axis_index("c")
    slot = lambda px, py, pc: {"chip": 2 * px + py, "core": pc}[slot_kind]
    rows = (x_ref.shape[1] if per_dest else x_ref.shape[0]) // n_split
    part = lambda ref, j: ref.at[pl.ds(j * rows, rows)]
    sends, recvs = [], []
    for k, m in enumerate(masks):
        px, py, pc = (mx + m[0]) % 2, (my + m[1]) % 2, (mc + m[2]) % 2
        src = x_ref.at[2 * px + py] if per_dest else x_ref
        own_src = x_ref.at[2 * mx + my] if per_dest else x_ref
        for j in range(n_split):
            sems = dict(send_sem=send_sems.at[k * n_split + j], recv_sem=recv_sems.at[k * n_split + j],
                        device_id=(px, py, pc), device_id_type=MESH)
            sends.append(pltpu.make_async_remote_copy(
                src_ref=part(src, j), dst_ref=part(land_ref.at[slot(mx, my, mc)], j), **sems))
            if with_recvs:
                recvs.append(pltpu.make_async_remote_copy(
                    src_ref=part(own_src, j), dst_ref=part(land_ref.at[slot(px, py, pc)], j), **sems))
    return sends, recvs


def _exchange_start(name, x, masks, slot_kind, per_dest, n_split, dep=None):
    n = len(masks) * n_split
    blk_shape = x.shape[1:] if per_dest else x.shape
    land_shape = ({"chip": 4, "core": 2}[slot_kind],) + tuple(blk_shape)
    deps = [] if dep is None else [dep]

    def body(x_ref, land_ref, *rest):
        send_sems, recv_sems, x_thru, land_thru, token = rest[len(deps):]
        for cp in _split_copies(x_ref, land_ref, send_sems, recv_sems, masks, slot_kind, per_dest, n_split, False)[0]:
            cp.start()
        token[...] = jnp.zeros_like(token)

    return pl.pallas_call(
        body, name=name,
        out_shape=(pltpu.SemaphoreType.DMA((n,)), pltpu.SemaphoreType.DMA((n,)), pltpu.HBM(x.shape, x.dtype),
                   pltpu.HBM(land_shape, x.dtype), jax.ShapeDtypeStruct((8, 128), F32)),
        in_specs=(HBM_SPEC, HBM_SPEC) + (pl.BlockSpec(memory_space=pl.ANY),) * len(deps),
        out_specs=(SEM_SPEC, SEM_SPEC, HBM_SPEC, HBM_SPEC, pl.BlockSpec(memory_space=pltpu.VMEM)),
        input_output_aliases={0: 2, 1: 3},
        compiler_params=pltpu.CompilerParams(has_side_effects=DATAFLOW),
    )(pltpu.with_memory_space_constraint(x, pltpu.HBM),
      pltpu.with_memory_space_constraint(lax.empty(land_shape, x.dtype), pltpu.HBM), *deps)


def _exchange_wait(name, started, after, masks, slot_kind, per_dest, n_split):
    send_sems, recv_sems, x_thru, land_thru, _ = started
    after = list(after) if isinstance(after, (list, tuple)) else [after]

    def body(x_ref, land_ref, send_sems, recv_sems, *after_and_out_refs):
        sends, recvs = _split_copies(x_ref, land_ref, send_sems, recv_sems, masks, slot_kind, per_dest, n_split)
        for cp in sends:
            cp.wait_send()
        for cp in recvs:
            cp.wait_recv()

    return pl.pallas_call(
        body, name=name,
        out_shape=(pltpu.HBM(x_thru.shape, x_thru.dtype), pltpu.HBM(land_thru.shape, land_thru.dtype)),
        in_specs=(HBM_SPEC, HBM_SPEC, SEM_SPEC, SEM_SPEC) + (pl.BlockSpec(memory_space=pl.ANY),) * len(after),
        out_specs=(HBM_SPEC, HBM_SPEC), input_output_aliases={0: 0, 1: 1},
        compiler_params=pltpu.CompilerParams(has_side_effects=DATAFLOW),
    )(x_thru, land_thru, send_sems, recv_sems, *after)


def _gather_chips(name, x):
    return _exchange(name, x, CHIP_MASKS, "chip", False)


def _gather_shards(name, shard, n_split, overlap=False, dep=None):
    half = shard.shape[0] // 2
    mine = lax.dynamic_slice_in_dim(shard, lax.axis_index("c") * half, half, axis=0)
    if overlap:
        return _exchange_start(name + "_chips", mine, CHIP_MASKS, "chip", False, n_split, dep)
    by_chip = _exchange(name + "_chips", mine, CHIP_MASKS, "chip", False, n_split, copy_own=False)
    return _gather_tail(name, mine, by_chip)


def _gather_tail(name, mine, by_chip, arrange=None):
    c, chip = lax.axis_index("c"), 2 * lax.axis_index("x") + lax.axis_index("y")
    by_chip = lax.dynamic_update_index_in_dim(by_chip, mine, chip, 0)
    if arrange is not None:
        by_chip = arrange(by_chip)
    both = _exchange(name + "_cores", by_chip, CORE_MASKS, "core", False, N_CHIPS, copy_own=False)
    return lax.dynamic_update_index_in_dim(both, by_chip, c, 0)


def _gather_finish(name, started, after, n_split, arrange=None):
    mine, by_chip = _exchange_wait(name + "_chips_wait", started, after, CHIP_MASKS, "chip", False, n_split)
    return _gather_tail(name, mine, by_chip, arrange)


def _sum_slots(name, x, tr=128):
    S, R, N = x.shape
    tr = _pick(R, tr, 16)

    def body(x_ref, o_ref):
        acc = x_ref[0].astype(F32)
        for s in range(1, S):
            acc = acc + x_ref[s].astype(F32)
        o_ref[...] = acc

    return pl.pallas_call(
        body, name=name, grid=(R // tr,),
        in_specs=[pl.BlockSpec((S, tr, N), lambda i: (0, i, 0))], out_specs=pl.BlockSpec((tr, N), lambda i: (i, 0)),
        out_shape=jax.ShapeDtypeStruct((R, N), F32), compiler_params=_params(("parallel",)),
    )(x)


def _add_to_bf16(name, a, b, tr=128):
    S, R, N = a.shape
    tr = _pick(R, tr, 16)

    def body(a_ref, b_ref, o_ref):
        o_ref[...] = (a_ref[...] + b_ref[...]).astype(BF16)

    spec = pl.BlockSpec((S, tr, N), lambda i: (0, i, 0))
    return pl.pallas_call(
        body, name=name, grid=(R // tr,), in_specs=[spec, spec], out_specs=spec,
        out_shape=jax.ShapeDtypeStruct((S, R, N), BF16), compiler_params=_params(("parallel",)),
    )(a, b)


def _add_slabs_to_bf16(name, full, recv, c, tr=64):
    R, NP = full.shape
    half = R // 2
    nb = half // tr

    def body(c_ref, a_ref, b_ref, o_ref):
        x = a_ref[...] + b_ref[...]
        for s in range(N_CHIPS - 1):
            o_ref[s] = x[:, s * SHARD_COLS:(s + 1) * SHARD_COLS].astype(BF16)
        last = jnp.concatenate([x[:, (N_CHIPS - 1) * SHARD_COLS:GAP_AT], x[:, GAP_AT + GAP:]], axis=1)
        o_ref[N_CHIPS - 1] = last.astype(BF16)

    grid_spec = pltpu.PrefetchScalarGridSpec(
        num_scalar_prefetch=1, grid=(nb,),
        in_specs=[pl.BlockSpec((tr, NP), lambda i, c_ref: (c_ref[0] * nb + i, 0)),
                  pl.BlockSpec((tr, NP), lambda i, c_ref: (i, 0))],
        out_specs=pl.BlockSpec((N_CHIPS, tr, SHARD_COLS), lambda i, c_ref: (0, i, 0)))
    return pl.pallas_call(
        body, name=name, grid_spec=grid_spec,
        out_shape=jax.ShapeDtypeStruct((N_CHIPS, half, SHARD_COLS), BF16), compiler_params=_params(("parallel",)),
    )(jnp.reshape(c, (1,)).astype(jnp.int32), full, recv)


def _adamw(name, gparts, w, m, v, tr=128):
    S, R, N = gparts.shape
    tr = _pick(R, tr)
    c1 = 1.0 / (1.0 - ADAM_B1 ** ADAM_STEP)
    c2 = 1.0 / (1.0 - ADAM_B2 ** ADAM_STEP)

    def body(g_ref, w_ref, m_ref, v_ref, go_ref, d_ref, mo_ref, vo_ref):
        g = g_ref[0]
        for s in range(1, S):
            g = g + g_ref[s]
        m_new = ADAM_B1 * m_ref[...] + (1.0 - ADAM_B1) * g
        v_new = ADAM_B2 * v_ref[...] + (1.0 - ADAM_B2) * jnp.square(g)
        go_ref[...] = g
        mo_ref[...] = m_new
        vo_ref[...] = v_new
        d_ref[...] = -ADAM_LR * ((m_new * c1) / (jnp.sqrt(v_new * c2) + ADAM_EPS) + ADAM_WD * w_ref[...])

    spec = pl.BlockSpec((tr, N), lambda i: (i, 0))
    return pl.pallas_call(
        body, name=name, grid=(R // tr,),
        in_specs=[pl.BlockSpec((S, tr, N), lambda i: (0, i, 0)), spec, spec, spec], out_specs=[spec] * 4,
        out_shape=[jax.ShapeDtypeStruct((R, N), F32)] * 4, compiler_params=_params(("parallel",)),
    )(gparts, w, m, v)


def _reduce_to_shard(name, slabs, n_split, by_columns=False, overlap=False):
    c, chip = lax.axis_index("c"), 2 * lax.axis_index("x") + lax.axis_index("y")
    if by_columns:
        R, N = slabs.shape[0], SHARD_COLS
        from_sibling = _exchange(name + "_sib", slabs, CORE_MASKS, None, False, n_split, other_half=True)[0]
        wire = _add_slabs_to_bf16(name + "_add", slabs, from_sibling, c)
    else:
        _, R, N = slabs.shape
        half = R // 2
        halves = slabs.reshape(N_CHIPS, 2, half, N)
        mine = lax.dynamic_index_in_dim(halves, c, axis=1, keepdims=False)
        theirs = lax.dynamic_index_in_dim(halves, 1 - c, axis=1, keepdims=False)
        from_sibling = _exchange(name + "_sib", theirs, CORE_MASKS, None, False, N_CHIPS)[0]
        wire = _add_to_bf16(name + "_add", mine, from_sibling)
    if overlap:
        return _exchange_start(name + "_chips", wire, CHIP_MASKS, "chip", True, n_split), (R, N)
    got = _exchange(name + "_chips", wire, CHIP_MASKS, "chip", True, n_split, copy_own=False)
    return _reduce_tail(name, wire, got, n_split, R, N)


def _reduce_tail(name, wire, got, n_split, R, N):
    c, chip = lax.axis_index("c"), 2 * lax.axis_index("x") + lax.axis_index("y")
    got = lax.dynamic_update_index_in_dim(got, lax.dynamic_index_in_dim(wire, chip, 0, keepdims=False), chip, 0)
    part = _sum_slots(name + "_sum", got)
    both = _exchange(name + "_cores", part, CORE_MASKS, "core", False, n_split, copy_own=False)
    return lax.dynamic_update_index_in_dim(both, part, c, 0).reshape(1, R, N)


def _reduce_finish(name, started, after, n_split):
    handle, (R, N) = started
    wire, got = _exchange_wait(name + "_chips_wait", handle, after, CHIP_MASKS, "chip", True, n_split)
    return _reduce_tail(name, wire, got, n_split, R, N)


def _pack(pieces, cols, row_mult=8):
    flat = jnp.concatenate([p.reshape(-1) for p in pieces])
    rows = -(-flat.shape[0] // cols)
    rows = -(-rows // row_mult) * row_mult
    return jnp.pad(flat, (0, rows * cols - flat.shape[0])).reshape(rows, cols)


def _unpack(packed, shapes):
    flat = packed.reshape(-1)
    out, off = [], 0
    for shp in shapes:
        n = 1
        for d in shp:
            n *= d
        out.append(flat[off:off + n].reshape(shp))
        off += n
    return out


def kernel(x, meta_tokens, norm_w, w_in, rw_shift_mu, rw_w0, rw_w2, rw_a0, rw_a2, rw_k_k, rw_k_a, rw_r_k, rw_gn_w, rw_gn_b, dn_conv_w, dn_A_log, dn_dt_bias, dn_norm_w, w_out, final_norm_w, loss_target, m_meta_tokens, m_norm_w, m_w_in, m_rw_shift_mu, m_rw_w0, m_rw_w2, m_rw_a0, m_rw_a2, m_rw_k_k, m_rw_k_a, m_rw_r_k, m_rw_gn_w, m_rw_gn_b, m_dn_conv_w, m_dn_A_log, m_dn_dt_bias, m_dn_norm_w, m_w_out, m_final_norm_w, v_meta_tokens, v_norm_w, v_w_in, v_rw_shift_mu, v_rw_w0, v_rw_w2, v_rw_a0, v_rw_a2, v_rw_k_k, v_rw_k_a, v_rw_r_k, v_rw_gn_w, v_rw_gn_b, v_dn_conv_w, v_dn_A_log, v_dn_dt_bias, v_dn_norm_w, v_w_out, v_final_norm_w):
    S = x.shape[1]
    L = N_META + S
    Lp = -(-L // CHUNK) * CHUNK

    small_shapes = [(RW_LORA, 256), (RW_LORA, 256), (CONV_W, 768), (N_META, 512)]
    small_mine = _pack([rw_w2[0], rw_a2[0], dn_conv_w[0], meta_tokens], 1024)
    small_all = _gather_chips("gather_small", small_mine)
    per_chip = [_unpack(small_all[s], small_shapes) for s in range(N_CHIPS)]
    w2, a2, conv_w, meta = [jnp.concatenate([per_chip[s][i] for s in range(N_CHIPS)], axis=1) for i in range(4)]
    w_in_started = _gather_shards("gather_w_in", w_in[0].astype(BF16), 8, overlap=True, dep=small_all)
    w_out_started = _gather_shards("gather_w_out", w_out[0].astype(BF16), 8, overlap=True, dep=w_in_started[4])

    tail = [jnp.zeros((Lp - L, D_MODEL), F32)] if Lp > L else []
    h = jnp.concatenate([meta + w_in_started[4][:1, :1], x[0]] + tail, axis=0)
    (u,) = _rowwise("rms_in", _rms_fn, [_row(h)], [norm_w], [D_MODEL], tm=2 * ROW_TILE, out_dtype=BF16)
    u_t = u.T
    m_in, v_in = m_w_in[0] + w_in_started[4][:1, :1], v_w_in[0] + w_in_started[4][:1, :1]
    cut = GAP_AT - (N_CHIPS - 1) * SHARD_COLS

    def gapped(by_chip):
        last = by_chip[N_CHIPS - 1]
        return jnp.concatenate([by_chip[s] for s in range(N_CHIPS - 1)]
                               + [last[:, :cut], jnp.zeros((last.shape[0], GAP), BF16), last[:, cut:]], axis=1)

    W_gapped = _gather_finish("gather_w_in", w_in_started, [u, u_t, m_in, v_in], 8, arrange=gapped)
    W_gapped = W_gapped.reshape(D_MODEL, NP_COLS)
    p = _mm("in_proj", u, W_gapped, "nn", dep=w_out_started[4])

    mu = rw_shift_mu
    zpad = jnp.zeros((RW_LORA, RW_WIDTH), F32)
    rw_params = [mu[:, 0:1024], mu[:, 1024:2048], mu[:, 2048:3072], mu[:, 3072:3200], rw_w0,
                 jnp.concatenate([w2, zpad], axis=0), rw_a0, jnp.concatenate([zpad, a2], axis=0), rw_k_k, rw_k_a]
    rw_rows = [_row(p, 1024, OFF_R), _row(p, 1024, OFF_K), _row(p, 1024, OFF_V), _row(p, 128, OFF_S1),
               _row(p, 1024, OFF_GATE)]
    rw_pars = rw_params + [rw_r_k, rw_gn_w, rw_gn_b]
    rw_saved_shapes = [(N_GROUPS, CHUNK, GROUP), (N_GROUPS, CHUNK, GROUP), (N_GROUPS, 2 * CHUNK, 2 * GROUP),
                       (N_GROUPS, 2 * CHUNK, GROUP)]

    dn_rows = [_row(p, 1024, OFF_DQ), _row(p, 1024, OFF_DK), _row(p, 1024, OFF_DV), _row(p, 128, OFF_S2),
               _row(p, 1024, OFF_Z)]
    dn_pars = [conv_w[j:j + 1, 1024 * i:1024 * (i + 1)] for i in range(3) for j in range(CONV_W)]
    narrow = lambda t: jnp.pad(t, ((0, 0), (DN_HEADS, DN_HEAD - 2 * DN_HEADS)))
    dn_pars += [narrow(dn_A_log), narrow(dn_dt_bias), jnp.tile(dn_norm_w, (1, DN_HEADS))]
    dn_saved_shapes = [(N_GROUPS, CHUNK, CHUNK), (N_GROUPS, CHUNK, GROUP)]
    mixer_parts = [(_rw_fused_steps, rw_rows, 4, rw_pars, rw_saved_shapes),
                   (_dn_fused_steps, dn_rows, 3, dn_pars, dn_saved_shapes)]
    y, mixer_kept = _mixers_fwd("mixers_fwd", mixer_parts)
    w_out_all = _gather_finish("gather_w_out", w_out_started, y, 8)
    Wo = jnp.concatenate([w_out_all[:, s].reshape(D_MODEL // N_CHIPS, D_MODEL) for s in range(N_CHIPS)], axis=0)
    loss_acc, dz, dz16, d_fw = _loss_head(h, y, Wo, loss_target[0], final_norm_w.reshape(1, D_MODEL))
    loss = lax.psum(loss_acc[0, 0], ("x", "y", "c"))

    d_wo = _mm("d_w_out", y.T, dz16, "nn", tm=512, tn=1024, tk=Lp)
    wo_started = _reduce_to_shard("rs_w_out", d_wo.reshape(N_CHIPS, D_MODEL // N_CHIPS, D_MODEL), 8, overlap=True)
    dy = _mm("d_out_proj", dz16, Wo, "nt", tn=1024, tk=2048, dep=wo_started[0][4])

    dp, (d_rw_pars, d_dn_pars) = _mixers_bwd("mixers_bwd", mixer_parts, mixer_kept, dy)
    d_prep_pars, d_rw_pars = d_rw_pars[:10], d_rw_pars[10:]
    d_conv_parts = [jnp.concatenate(d_dn_pars[4 * i:4 * i + 4], axis=0) for i in range(3)]
    d_a_log_b, d_dt_b = d_dn_pars[12], d_dn_pars[13]
    d_dn_nw = jnp.sum(d_dn_pars[14].reshape(DN_HEADS, DN_HEAD), axis=0, keepdims=True)
    d_W = _mm("d_w_in", u_t, dp, "nn", tm=1024, tn=768, tk=Lp)
    wi_started = _reduce_to_shard("rs_w_in", d_W, 8, by_columns=True, overlap=True)
    du = _mm("d_in_proj", dp, W_gapped, "nt", tn=2048, tk=1408, dep=wi_started[0][4])
    norm_res = lambda h_, w_: (_rms_fn(h_, w_)[0], h_)
    (dx,), (d_norm_w_x,) = _rowwise_bwd("rms_in_bwd", norm_res, [_row(h)], [norm_w], [_row(du), _row(dz)],
                                        tm=2 * GROUP, row0=N_META, n_rows=S)
    (d_meta,), (d_norm_w_meta,) = _rowwise_bwd("rms_meta_bwd", norm_res, [_row(h)], [norm_w], [_row(du), _row(dz)],
                                               n_rows=N_META)
    d_norm_w = d_norm_w_x + d_norm_w_meta
    grad_x = dx[None]

    d_mu = jnp.concatenate(d_prep_pars[0:4], axis=1)
    d_w2, d_a2 = d_prep_pars[5][:RW_LORA], d_prep_pars[7][RW_LORA:]
    d_conv = jnp.concatenate(d_conv_parts, axis=1)
    head_sum = lambda t: t[:, DN_HEADS:2 * DN_HEADS]
    rep_names = ["norm_w", "rw_shift_mu", "rw_w0", "rw_a0", "rw_k_k", "rw_k_a", "rw_r_k", "rw_gn_w", "rw_gn_b",
                 "dn_A_log", "dn_dt_bias", "dn_norm_w", "final_norm_w"]
    rep_g = [d_norm_w, d_mu, d_prep_pars[4], d_prep_pars[6], d_prep_pars[8], d_prep_pars[9],
             d_rw_pars[0], d_rw_pars[1], d_rw_pars[2],
             head_sum(d_a_log_b), head_sum(d_dt_b), d_dn_nw, d_fw.reshape(D_MODEL)]
    rep_w = [norm_w, rw_shift_mu, rw_w0, rw_a0, rw_k_k, rw_k_a, rw_r_k, rw_gn_w, rw_gn_b, dn_A_log, dn_dt_bias,
             dn_norm_w, final_norm_w]
    rep_m = [m_norm_w, m_rw_shift_mu, m_rw_w0, m_rw_a0, m_rw_k_k, m_rw_k_a, m_rw_r_k, m_rw_gn_w, m_rw_gn_b,
             m_dn_A_log, m_dn_dt_bias, m_dn_norm_w, m_final_norm_w]
    rep_v = [v_norm_w, v_rw_shift_mu, v_rw_w0, v_rw_a0, v_rw_k_k, v_rw_k_a, v_rw_r_k, v_rw_gn_w, v_rw_gn_b,
             v_dn_A_log, v_dn_dt_bias, v_dn_norm_w, v_final_norm_w]
    rep_shapes = [t.shape for t in rep_w]
    rep_all = _exchange("gather_rep_grads", _pack(rep_g, 128), ALL_MASKS, "dev", False)
    rep_out = _adamw("adam_rep", rep_all, _pack(rep_w, 128), _pack(rep_m, 128), _pack(rep_v, 128))
    rep_out = [dict(zip(rep_names, _unpack(t, rep_shapes))) for t in rep_out]

    sm_slabs = jnp.stack([_pack([d_w2[:, 256 * s:256 * (s + 1)], d_a2[:, 256 * s:256 * (s + 1)],
                                 d_conv[:, 768 * s:768 * (s + 1)], d_meta[:, 512 * s:512 * (s + 1)]], 1024, 64)
                          for s in range(N_CHIPS)])
    sm_parts = _reduce_to_shard("rs_small", sm_slabs, 1)
    sm_w = [rw_w2[0], rw_a2[0], dn_conv_w[0], meta_tokens]
    sm_m = [m_rw_w2[0], m_rw_a2[0], m_dn_conv_w[0], m_meta_tokens]
    sm_v = [v_rw_w2[0], v_rw_a2[0], v_dn_conv_w[0], v_meta_tokens]
    sm_out = _adamw("adam_small", sm_parts, _pack(sm_w, 1024, 64), _pack(sm_m, 1024, 64), _pack(sm_v, 1024, 64))
    sm_names = ["rw_w2", "rw_a2", "dn_conv_w", "meta_tokens"]
    sm_full_shapes = [(1, RW_LORA, 256), (1, RW_LORA, 256), (1, CONV_W, 768), (N_META, 512)]
    sm_out = [dict(zip(sm_names, [t.reshape(shp) for t, shp in zip(_unpack(o, small_shapes), sm_full_shapes)]))
              for o in sm_out]

    wo_parts = _reduce_finish("rs_w_out", wo_started, dp, 8)
    wo_out = _adamw("adam_w_out", wo_parts, w_out[0], m_w_out[0], v_w_out[0])
    wi_parts = _reduce_finish("rs_w_in", wi_started, dx, 8)
    wi_out = _adamw("adam_w_in", wi_parts, w_in[0], m_in, v_in)

    order = ["meta_tokens", "norm_w", "w_in", "rw_shift_mu", "rw_w0", "rw_w2", "rw_a0", "rw_a2", "rw_k_k", "rw_k_a",
             "rw_r_k", "rw_gn_w", "rw_gn_b", "dn_conv_w", "dn_A_log", "dn_dt_bias", "dn_norm_w", "w_out",
             "final_norm_w"]
    outs = [loss, grad_x]
    for kind in range(4):
        table = dict(rep_out[kind])
        table.update(sm_out[kind])
        table["w_in"] = wi_out[kind][None]
        table["w_out"] = wo_out[kind][None]
        outs += [table[n] for n in order]
    return tuple(outs)
```

```python
import functools

import jax
import jax.numpy as jnp
from jax import lax
from jax.experimental import pallas as pl
from jax.experimental.pallas import tpu as pltpu

F32 = jnp.float32
BF16 = jnp.bfloat16
HIGH = lax.Precision.HIGH
HIGHEST = lax.Precision.HIGHEST
MESH = pl.DeviceIdType.MESH

D_MODEL = 2048
N_META = 16
RW_WIDTH = 1024
RW_HEAD = 64
RW_HEADS = 16
RW_LORA = 64
RW_GN_EPS = 64e-5
DN_WIDTH = 1024
DN_HEAD = 128
DN_HEADS = 8
CONV_W = 4
CHUNK = 64
NORM_EPS = 1e-6
IN_COLS = 8336
N_CHIPS = 4
SHARD_COLS = IN_COLS // N_CHIPS

NP_COLS = 8 * 1024 + 256
GAP_AT = IN_COLS - DN_WIDTH
GAP = NP_COLS - IN_COLS
OFF_R, OFF_K, OFF_V, OFF_S1, OFF_GATE = 0, 1024, 2048, 3072, 3200
OFF_DQ, OFF_DK, OFF_DV, OFF_S2, OFF_Z = 4224, 5248, 6272, 7296, 7424

ADAM_LR = 0.001
ADAM_B1 = 0.9
ADAM_B2 = 0.999
ADAM_EPS = 1e-08
ADAM_WD = 0.01
ADAM_STEP = 10

VMEM_LIMIT_BYTES = 56 * 1024 * 1024
ROW_TILE = 104
MM_ROW_TILE = 832


def _params(sem=None):
    return pltpu.CompilerParams(dimension_semantics=sem, vmem_limit_bytes=VMEM_LIMIT_BYTES)


def _pick(n, target, mult=8):
    best = None
    for d in range(mult, min(n, target) + 1, mult):
        if n % d == 0:
            best = d
    return n if best is None else best


def _mm(name, a, b, mode, tm=MM_ROW_TILE, tn=1408, tk=2048, dep=None):
    if mode == "nn":
        (M, K), (_, N) = a.shape, b.shape
    elif mode == "nt":
        (M, K), (N, _) = a.shape, b.shape
    else:
        (K, M), (_, N) = a.shape, b.shape
    tm = _pick(M, tm, 128 if mode == "tn" else 16)
    tn = _pick(N, tn, 128)
    tk = _pick(K, tk, 8 if mode == "tn" else 128)
    if mode == "nn":
        a_spec = pl.BlockSpec((tm, tk), lambda i, j, k: (i, k))
        b_spec = pl.BlockSpec((tk, tn), lambda i, j, k: (k, j))
        dims = (((1,), (0,)), ((), ()))
    elif mode == "nt":
        a_spec = pl.BlockSpec((tm, tk), lambda i, j, k: (i, k))
        b_spec = pl.BlockSpec((tn, tk), lambda i, j, k: (j, k))
        dims = (((1,), (1,)), ((), ()))
    else:
        a_spec = pl.BlockSpec((tk, tm), lambda i, j, k: (k, i))
        b_spec = pl.BlockSpec((tk, tn), lambda i, j, k: (k, j))
        dims = (((0,), (0,)), ((), ()))

    def body(a_ref, b_ref, *rest):
        o_ref = rest[-1]

        @pl.when(pl.program_id(2) == 0)
        def _():
            o_ref[...] = jnp.zeros_like(o_ref)

        o_ref[...] += lax.dot_general(a_ref[...].astype(BF16), b_ref[...].astype(BF16), dims,
                                      preferred_element_type=F32)

    deps = [] if dep is None else [dep]
    return pl.pallas_call(
        body, name=name, grid=(M // tm, N // tn, K // tk),
        in_specs=[a_spec, b_spec] + [pl.BlockSpec((8, 128), lambda i, j, k: (0, 0))] * len(deps),
        out_specs=pl.BlockSpec((tm, tn), lambda i, j, k: (i, j)),
        out_shape=jax.ShapeDtypeStruct((M, N), F32),
        compiler_params=_params(("parallel", "parallel", "arbitrary")),
    )(a, b, *deps)


def _row(arr, width=None, cb=0):
    return (arr, arr.shape[1] if width is None else width, cb)


def _rowwise(name, fn, rows, params, out_widths, tm=ROW_TILE, out_dtype=F32):
    R = rows[0][0].shape[0]
    tm = _pick(R, tm, 16 if out_dtype == BF16 else 8)
    n_r, n_p = len(rows), len(params)

    def body(*refs):
        vals = [r[...] for r in refs[:n_r + n_p]]
        for o_ref, val in zip(refs[n_r + n_p:], fn(*vals)):
            o_ref[...] = val.astype(out_dtype)

    in_specs = [pl.BlockSpec((tm, w), lambda i, cb=cb: (i, cb)) for (_, w, cb) in rows]
    in_specs += [pl.BlockSpec(p.shape, lambda i: (0, 0)) for p in params]
    return pl.pallas_call(
        body, name=name, grid=(R // tm,), in_specs=in_specs,
        out_specs=[pl.BlockSpec((tm, w), lambda i: (i, 0)) for w in out_widths],
        out_shape=[jax.ShapeDtypeStruct((R, w), out_dtype) for w in out_widths],
        compiler_params=_params(("parallel",)),
    )(*[r[0] for r in rows], *params)


def _rowwise_bwd(name, fn, rows, params, douts, tm=ROW_TILE, row0=0, n_rows=None):
    R = rows[0][0].shape[0] - row0 if n_rows is None else n_rows
    tm = _pick(R, tm)
    assert row0 % 8 == 0 and row0 + R <= rows[0][0].shape[0]

    def in_spec(w, cb):
        if row0 == 0:
            return pl.BlockSpec((tm, w), lambda i: (i, cb))
        return pl.BlockSpec((pl.Element(tm), pl.Element(w)), lambda i: (pl.multiple_of(row0 + i * tm, 8), cb * w))

    n_r, n_p, n_d = len(rows), len(params), len(douts)

    def body(*refs):
        vals = [r[...] for r in refs[:n_r + n_p]]
        cts = tuple(r[...] for r in refs[n_r + n_p:n_r + n_p + n_d])
        grads = jax.vjp(fn, *vals)[1](cts)
        outs = refs[n_r + n_p + n_d:]
        for o_ref, g in zip(outs[:n_r], grads[:n_r]):
            o_ref[...] = g

        @pl.when(pl.program_id(0) == 0)
        def _():
            for o_ref in outs[n_r:]:
                o_ref[...] = jnp.zeros_like(o_ref)

        for o_ref, g in zip(outs[n_r:], grads[n_r:]):
            o_ref[...] += g

    in_specs = [in_spec(w, cb) for (_, w, cb) in rows]
    in_specs += [pl.BlockSpec(p.shape, lambda i: (0, 0)) for p in params]
    in_specs += [in_spec(w, cb) for (_, w, cb) in douts]
    out_specs = [pl.BlockSpec((tm, w), lambda i: (i, 0)) for (_, w, _) in rows]
    out_specs += [pl.BlockSpec(p.shape, lambda i: (0, 0)) for p in params]
    out_shape = [jax.ShapeDtypeStruct((R, w), F32) for (_, w, _) in rows]
    out_shape += [jax.ShapeDtypeStruct(p.shape, F32) for p in params]
    res = pl.pallas_call(
        body, name=name, grid=(R // tm,), in_specs=in_specs, out_specs=out_specs, out_shape=out_shape,
        compiler_params=_params(("arbitrary",)),
    )(*[r[0] for r in rows], *params, *[d[0] for d in douts])
    return res[:n_r], res[n_r:]


def _softplus(x):
    return jnp.maximum(x, 0.0) + jnp.log(1.0 + jnp.exp(-jnp.abs(x)))


def _silu(x):
    return x * jax.nn.sigmoid(x)


def _rms_fn(h, w):
    return (h * lax.rsqrt(jnp.mean(h * h, axis=-1, keepdims=True) + NORM_EPS) * w,)


def _rw_prep_fn(pr, pr1, pk, pk1, pv, pv1, ps, ps1, mu_r, mu_k, mu_v, mu_s, w0, w2p, a0, a2p, k_k, k_a):
    r = pr + (pr1 - pr) * mu_r
    k = pk + (pk1 - pk) * mu_k
    v = pv + (pv1 - pv) * mu_v
    s = ps + (ps1 - ps) * mu_s
    lora = lambda x, w: jnp.dot(x.astype(BF16), w.astype(BF16), preferred_element_type=F32)
    w_log = -_softplus(-(w0 + lora(jnp.tanh(s), w2p))) - 0.5
    log_decay = -jnp.exp(w_log)
    a = jax.nn.sigmoid(a0 + lora(s, a2p))
    return r, log_decay, k * (1.0 + (a - 1.0) * k_a), v, k * k_k, a


def _conv_fn(u0, u1, u2, u3, w0, w1, w2, w3):
    return (_silu(u0 * w3 + u1 * w2 + u2 * w1 + u3 * w0),)


def _dn_gate_fn(ps2, a_log_n, dt_n):
    beta_n = jax.nn.sigmoid(ps2)
    g_n = -jnp.exp(a_log_n) * _softplus(ps2 + dt_n)
    lane = lax.broadcasted_iota(jnp.int32, (1, DN_HEAD), 1)
    pick = lambda x, j: jnp.broadcast_to(jnp.sum(x * (lane == j).astype(F32), axis=-1, keepdims=True), x.shape)[None]
    beta = jnp.concatenate([pick(beta_n, h) for h in range(DN_HEADS)], axis=0)
    g = jnp.concatenate([pick(g_n, DN_HEADS + h) for h in range(DN_HEADS)], axis=0)
    return beta, g


def _tril_masks(c):
    t = lax.broadcasted_iota(jnp.int32, (c, c), 0)
    s = lax.broadcasted_iota(jnp.int32, (c, c), 1)
    return s <= t, s < t


def _dot(x, y, cx, cy, prec=HIGH):
    nb = x.ndim - 2
    batch = tuple(range(nb))
    return lax.dot_general(x, y, (((cx + nb,), (cy + nb,)), (batch, batch)), precision=prec,
                           preferred_element_type=F32)


INV_BASE = 8


def _run_interleaved(running):
    results, live = [None] * len(running), list(range(len(running)))
    while live:
        for k in list(live):
            try:
                next(running[k])
            except StopIteration as done:
                results[k] = done.value
                live.remove(k)
    return results


def _blocked_inv(a, mm, row, col, mm_merge=None):
    mm_merge = mm if mm_merge is None else mm_merge
    shift = INV_BASE.bit_length() - 1
    d = jnp.where(jnp.right_shift(row, shift) == jnp.right_shift(col, shift), a, 0.0)
    x = (row == col).astype(F32) - d
    p = mm(d, d)
    yield
    x = x + mm(x, p)
    yield
    p = mm(p, p)
    yield
    x = x + mm(x, p)
    yield
    size = INV_BASE
    while size < a.shape[-2]:
        same_pair = jnp.right_shift(row, shift + 1) == jnp.right_shift(col, shift + 1)
        lower_left = (jnp.right_shift(row, shift) & 1 == 1) & (jnp.right_shift(col, shift) & 1 == 0)
        lower = mm_merge(jnp.where(same_pair & lower_left, a, 0.0), x)
        yield
        x = x - mm_merge(x, lower)
        size, shift = 2 * size, shift + 1
        yield
    return x


def _tri_inv(a):
    c = a.shape[-1]
    row, col = lax.broadcasted_iota(jnp.int32, (c, c), 0), lax.broadcasted_iota(jnp.int32, (c, c), 1)
    return _blocked_inv(a, lambda x, y: _dot(x, y, 1, 0), row, col,
                        lambda x, y: _dot(x.astype(BF16), y.astype(BF16), 1, 0, None))


@jax.custom_vjp
def _tri_inv_saved(a, t):
    return t


def _tri_inv_saved_fwd(a, t):
    return t, t


def _tri_inv_saved_bwd(t, dt):
    return -_dot(_dot(t, dt, 0, 0), t, 1, 1), jnp.zeros_like(t)


_tri_inv_saved.defvjp(_tri_inv_saved_fwd, _tri_inv_saved_bwd)


def _pair_masks():
    lane = lax.broadcasted_iota(jnp.int32, (1, 2 * RW_HEAD), 1)
    m0 = (lane < RW_HEAD).astype(F32)
    return m0, 1.0 - m0


def _pair_bd(x):
    m0, m1 = _pair_masks()
    return jnp.concatenate([x * m0, x * m1], axis=-2)


def _pair_mm(x, y):
    return _dot(x, _pair_bd(y), 1, 0)


def _pair_inv(a):
    c = a.shape[-2]
    row = lax.broadcasted_iota(jnp.int32, (c, 2 * RW_HEAD), 0)
    col = lax.broadcasted_iota(jnp.int32, (c, 2 * RW_HEAD), 1) & (RW_HEAD - 1)
    return _blocked_inv(a, _pair_mm, row, col,
                        lambda x, y: _dot(x.astype(BF16), _pair_bd(y).astype(BF16), 1, 0, None))


@jax.custom_vjp
def _pair_inv_saved(a, t):
    return t


def _pair_inv_saved_fwd(a, t):
    return t, t


def _pair_inv_saved_bwd(t, dt):
    m0, m1 = _pair_masks()
    c = t.shape[-2]
    z = _dot(t, dt, 0, 0)
    x = z[:, :c, :] * m0 + z[:, c:, :] * m1
    return -_dot(x, _pair_bd(t), 1, 1), jnp.zeros_like(t)


_pair_inv_saved.defvjp(_pair_inv_saved_fwd, _pair_inv_saved_bwd)


@jax.custom_vjp
def _use_saved(x, x_saved):
    return x_saved


_use_saved.defvjp(lambda x, x_saved: (x_saved, None), lambda _, g: (g, jnp.zeros_like(g)))


def _rw_chunk_steps(S, r, lw, k2, v, kkp, a, gate, rk, gnw, gnb, saved=None):
    B, C, P = r.shape
    m0, m1 = _pair_masks()
    seg = lambda x: (jnp.sum(x * m0, axis=-1, keepdims=True) * m0 + jnp.sum(x * m1, axis=-1, keepdims=True) * m1)
    mm = lambda x, y: _dot(x.astype(BF16), y.astype(BF16), 1, 0, None)
    nt = lambda x, y: _dot(x, y, 1, 1)
    tn = lambda x, y: _dot(x, y, 0, 0)
    pair_mm = lambda x, y: mm(x, _pair_bd(y))
    t_idx = lax.broadcasted_iota(jnp.int32, (C, P), 0)
    s_idx = lax.broadcasted_iota(jnp.int32, (C, P), 1) & (RW_HEAD - 1)
    incl, strict = s_idx <= t_idx, s_idx < t_idx
    tril = jnp.broadcast_to(_tril_masks(C)[0].astype(F32), (B, C, C))
    kk = kkp * lax.rsqrt(seg(kkp * kkp) + 1e-6)
    b = kk * a
    reuse = (lambda x, i: x) if saved is None else (lambda x, i: _use_saved(x, saved[i]))
    g_incl = reuse(_dot(tril, lw, 1, 0), 1)
    g_excl = g_incl - lw
    inv = jnp.exp(-g_incl)
    alpha, beta, kappa, rho = kk * jnp.exp(g_excl), b * inv, k2 * inv, r * jnp.exp(g_incl)
    yield
    ar = jnp.concatenate([alpha, rho], axis=-2)
    scores = reuse(nt(ar, jnp.concatenate([_pair_bd(beta), _pair_bd(kappa)], axis=-2)), 2)
    a_ab = jnp.where(strict, scores[:, :C, :P], 0.0)
    a_ak = jnp.where(strict, scores[:, :C, P:], 0.0)
    r_b = jnp.where(incl, scores[:, C:, :P], 0.0)
    r_k = jnp.where(incl, scores[:, C:, P:], 0.0)
    yield
    t_inv = (yield from _pair_inv(a_ab)) if saved is None else _pair_inv_saved(a_ab, saved[0])
    on_state = reuse(nt(ar, S), 3)
    yield
    u = pair_mm(t_inv, -on_state[:, :C, :] - pair_mm(a_ak, v))
    yield
    y = on_state[:, C:, :] + mm(jnp.concatenate([r_b, r_k], axis=-1),
                                jnp.concatenate([_pair_bd(u), _pair_bd(v)], axis=-2))
    same_head = ((lax.broadcasted_iota(jnp.int32, (P, P), 0) < RW_HEAD)
                 == (lax.broadcasted_iota(jnp.int32, (P, P), 1) < RW_HEAD))
    yield
    fresh = tn(jnp.concatenate([u, v], axis=-2), jnp.concatenate([beta, kappa], axis=-2))
    yield
    S_new = jnp.exp(jnp.sum(lw, axis=-2, keepdims=True)) * (S + jnp.where(same_head, fresh, 0.0))
    dev = y - seg(y) * (1.0 / RW_HEAD)
    yn = dev * lax.rsqrt(seg(dev * dev) * (1.0 / RW_HEAD) + RW_GN_EPS) * gnw + gnb
    bonus = seg(r * k2 * rk) * v
    return (yn + bonus) * _silu(gate), S_new, (t_inv, g_incl, scores, on_state)


def _dn_chunk_steps(S, qc, kc, vc, bb, gb, z, nw, saved=None):
    B, C, D = qc.shape
    mm = lambda x, y: _dot(x.astype(BF16), y.astype(BF16), 1, 0, None)
    nt = lambda x, y, p=None: _dot(x, y, 1, 1, p) if p else _dot(x.astype(BF16), y.astype(BF16), 1, 1, None)
    tn = lambda x, y: _dot(x.astype(BF16), y.astype(BF16), 0, 0, None)
    incl, strict = _tril_masks(C)
    q = qc * lax.rsqrt(jnp.sum(qc * qc, axis=-1, keepdims=True) + 1e-6) * (D ** -0.5)
    k = kc * lax.rsqrt(jnp.sum(kc * kc, axis=-1, keepdims=True) + 1e-6)
    kb, vb = k * bb, vc * bb
    yield
    G = _dot(jnp.broadcast_to(incl.astype(F32), (B, C, C)), gb, 1, 0, HIGHEST)
    G = G if saved is None else _use_saved(G, saved[1])
    lane = lax.broadcasted_iota(jnp.int32, (C, D), 1)
    e0, e1 = (lane == 0).astype(F32), (lane == 1).astype(F32)
    diff = nt(G * e0 + e1, e0 - G * e1, HIGHEST)
    dmask = jnp.where(incl, jnp.exp(jnp.where(incl, diff, 0.0)), 0.0)
    M = jnp.where(strict, nt(kb, k) * dmask, 0.0)
    yield
    T = (yield from _tri_inv(M)) if saved is None else _tri_inv_saved(M, saved[0])
    eG = jnp.exp(G)
    u = mm(T, vb)
    yield
    w = mm(T, kb * eG)
    yield
    attn = jnp.where(incl, nt(q, k) * dmask, 0.0)
    yield
    v_new = u - mm(w, S)
    yield
    o = mm(q * eG, S) + mm(attn, v_new)
    yield
    g_last = jnp.sum(gb, axis=-2, keepdims=True)
    S_new = S * jnp.exp(jnp.broadcast_to(g_last, S.shape)) + tn(k * jnp.exp(g_last - G), v_new)
    on = o * lax.rsqrt(jnp.mean(o * o, axis=-1, keepdims=True) + NORM_EPS) * nw
    return on * _silu(z), S_new, (T, G)


N_GROUPS = 8
GROUP = 128
HALO = 8


def _groups(x):
    return jnp.concatenate([x[:, g * GROUP:(g + 1) * GROUP][None] for g in range(N_GROUPS)], axis=0)


@functools.partial(jax.custom_vjp, nondiff_argnums=(1,))
def _shift_rows(ext, j):
    return pltpu.roll(ext, j, 0)[HALO:, :]


def _shift_rows_fwd(ext, j):
    return _shift_rows(ext, j), None


def _shift_rows_bwd(j, _, d):
    z = jnp.concatenate([jnp.zeros((HALO, d.shape[1]), d.dtype), d], axis=0)
    return (pltpu.roll(z, z.shape[0] - j, 0),)


_shift_rows.defvjp(_shift_rows_fwd, _shift_rows_bwd)


def _rw_fused_steps(S, h_r, h_k, h_v, h_s, p_r, p_k, p_v, p_s, gate, *pars, saved=None):
    prev = lambda h, x: _shift_rows(jnp.concatenate([h, x], axis=0), 1)
    seq = _rw_prep_fn(p_r, prev(h_r, p_r), p_k, prev(h_k, p_k), p_v, prev(h_v, p_v), p_s, prev(h_s, p_s), *pars[:10])
    yield
    return (yield from _rw_chunk_steps(S, *[_groups(t) for t in seq], _groups(gate),
                                       *[_groups(t) for t in pars[10:]], saved=saved))


def _dn_fused_steps(S, h_q, h_k, h_v, p_q, p_k, p_v, p_s2, z, *pars, saved=None):
    conv = []
    for i, (h, x) in enumerate(((h_q, p_q), (h_k, p_k), (h_v, p_v))):
        ext = jnp.concatenate([h, x], axis=0)
        conv += _conv_fn(x, _shift_rows(ext, 1), _shift_rows(ext, 2), _shift_rows(ext, 3), *pars[4 * i:4 * i + 4])
    beta, g = _dn_gate_fn(p_s2, pars[12], pars[13])
    yield
    return (yield from _dn_chunk_steps(S, *[_groups(t) for t in conv], beta, g, _groups(z), _groups(pars[14]),
                                       saved=saved))


def _row_specs(rows, n_halo, row_of):
    C = CHUNK
    specs = [pl.BlockSpec((pl.Element(C), pl.Element(w)), lambda c, off=off: (row_of(c) * C, off))
             for (_, w, off) in rows]
    specs += [pl.BlockSpec((pl.Element(HALO), pl.Element(w)),
                           lambda c, off=off: (pl.multiple_of(jnp.maximum(row_of(c) * C - HALO, 0), HALO), off))
              for (_, w, off) in rows[:n_halo]]
    return specs


def _mixers_fwd(name, parts):
    Lp = parts[0][1][0][0].shape[0]
    C, D, NB = CHUNK, GROUP, N_GROUPS
    nc, width = Lp // C, NB * D
    n_in = [len(rows) + n_halo + len(pars) for (_, rows, n_halo, pars, _) in parts]
    n_out = [1 + len(shapes) for (*_, shapes) in parts]

    def body(*refs):
        ins, y_ref = refs[:sum(n_in)], refs[sum(n_in)]
        outs, states = refs[sum(n_in) + 1:sum(n_in) + 1 + sum(n_out)], refs[sum(n_in) + 1 + sum(n_out):]
        first = pl.program_id(0) == 0

        @pl.when(first)
        def _():
            for s_ref in states:
                s_ref[...] = jnp.zeros_like(s_ref)

        i_at, running = 0, []
        for k, (steps, rows, n_halo, pars, shapes) in enumerate(parts):
            n_r = len(rows)
            row_refs, halo_refs = ins[i_at:i_at + n_r], ins[i_at + n_r:i_at + n_r + n_halo]
            par_refs = ins[i_at + n_r + n_halo:i_at + n_in[k]]
            S = states[k][...]
            halos = [jnp.where(first, 0.0, r[...]) for r in halo_refs]
            running.append(steps(S, *halos, *[r[...] for r in row_refs], *[r[...] for r in par_refs]))
            i_at += n_in[k]
        results = _run_interleaved(running)
        o_at = 0
        for k, (y, S_new, saved) in enumerate(results):
            ck_ref, saved_refs = outs[o_at], outs[o_at + 1:o_at + n_out[k]]
            ck_ref[0] = states[k][...]
            for g in range(NB):
                y_ref[:, k * width + g * D:k * width + (g + 1) * D] = y[g].astype(BF16)
            for o_ref, val in zip(saved_refs, saved):
                o_ref[0] = val
            states[k][...] = S_new
            o_at += n_out[k]

    in_specs, args, out_specs, out_shape = [], [], [], []
    for (_, rows, n_halo, pars, shapes) in parts:
        in_specs += _row_specs(rows, n_halo, lambda c: c) + [pl.BlockSpec(p.shape, lambda c: (0, 0)) for p in pars]
        args += [r[0] for r in rows] + [r[0] for r in rows[:n_halo]] + list(pars)
        for shp in [(NB, D, D)] + list(shapes):
            out_specs.append(pl.BlockSpec((1,) + tuple(shp), lambda c: (c, 0, 0, 0)))
            out_shape.append(jax.ShapeDtypeStruct((nc,) + tuple(shp), F32))
    res = pl.pallas_call(
        body, name=name, grid=(nc,), in_specs=in_specs,
        out_specs=[pl.BlockSpec((C, width * len(parts)), lambda c: (c, 0))] + out_specs,
        out_shape=[jax.ShapeDtypeStruct((Lp, width * len(parts)), BF16)] + out_shape,
        scratch_shapes=[pltpu.VMEM((NB, D, D), F32)] * len(parts),
        compiler_params=_params(("arbitrary",)),
    )(*args)
    per_part, at = [], 1
    for n in n_out:
        per_part.append(list(res[at:at + n]))
        at += n
    return res[0], per_part


def _mixers_bwd(name, parts, kept, dy):
    Lp = parts[0][1][0][0].shape[0]
    C, D, NB = CHUNK, GROUP, N_GROUPS
    nc = Lp // C
    n_in = [len(rows) + n_halo + len(pars) + len(kept[k]) for k, (_, rows, n_halo, pars, _) in enumerate(parts)]
    widths = [sum(w for (_, w, _) in rows) for (_, rows, *_) in parts]
    n_par = [len(pars) for (_, _, _, pars, _) in parts]
    n_scr = [1 + n_halo for (_, _, n_halo, _, _) in parts]
    rev = lambda c: nc - 1 - c

    def body(*refs):
        ins, dy_ref = refs[:sum(n_in)], refs[sum(n_in)]
        drows_ref = refs[sum(n_in) + 1]
        dpar_all = refs[sum(n_in) + 2:sum(n_in) + 2 + sum(n_par)]
        scratch = refs[sum(n_in) + 2 + sum(n_par):]
        i = pl.program_id(0)

        @pl.when(i == 0)
        def _():
            for o_ref in list(dpar_all) + list(scratch):
                o_ref[...] = jnp.zeros_like(o_ref)

        i_at, vals, stored, cts, sizes = 0, [], [], [], []
        for k, (_, rows, n_halo, pars, _) in enumerate(parts):
            n_r = len(rows)
            row_refs, halo_refs = ins[i_at:i_at + n_r], ins[i_at + n_r:i_at + n_r + n_halo]
            par_refs = ins[i_at + n_r + n_halo:i_at + n_r + n_halo + n_par[k]]
            kept_refs = ins[i_at + n_r + n_halo + n_par[k]:i_at + n_in[k]]
            halos = [jnp.where(i == nc - 1, 0.0, r[...]) for r in halo_refs]
            part_vals = [kept_refs[0][0]] + halos + [r[...] for r in row_refs] + [r[...] for r in par_refs]
            vals += part_vals
            sizes.append(len(part_vals))
            stored.append(tuple(r[0] for r in kept_refs[1:]))
            cts.append((_groups(dy_ref[:, k * NB * D:(k + 1) * NB * D]), scratch[sum(n_scr[:k])][...]))
            i_at += n_in[k]

        def both(*flat):
            running, at = [], 0
            for k, (steps, *_) in enumerate(parts):
                running.append(steps(*flat[at:at + sizes[k]], saved=stored[k]))
                at += sizes[k]
            return tuple(r[:2] for r in _run_interleaved(running))

        all_grads = jax.vjp(both, *vals)[1](tuple(cts))
        i_at = p_at = s_at = g_at = col = 0
        for k, (_, rows, n_halo, pars, _) in enumerate(parts):
            n_r = len(rows)
            grads = all_grads[g_at:g_at + sizes[k]]
            dpar_refs = dpar_all[p_at:p_at + n_par[k]]
            ds_ref, carry_refs = scratch[s_at], scratch[s_at + 1:s_at + n_scr[k]]
            ds_ref[...] = grads[0]
            d_halos, d_rows = grads[1:1 + n_halo], grads[1 + n_halo:1 + n_halo + n_r]
            for j, g in enumerate(d_rows):
                if j < n_halo:
                    g = g + jnp.concatenate([jnp.zeros((C - HALO, g.shape[1]), F32), carry_refs[j][...]], axis=0)
                    carry_refs[j][...] = d_halos[j]
                drows_ref[:, col:col + g.shape[1]] = g.astype(BF16)
                col += g.shape[1]
            for o_ref, g in zip(dpar_refs, grads[1 + n_halo + n_r:]):
                o_ref[...] += g
            p_at, s_at, g_at = p_at + n_par[k], s_at + n_scr[k], g_at + sizes[k]

    in_specs, args, out_specs, out_shape, scratch_shapes = [], [], [], [], []
    for k, (_, rows, n_halo, pars, _) in enumerate(parts):
        in_specs += _row_specs(rows, n_halo, rev) + [pl.BlockSpec(p.shape, lambda c: (0, 0)) for p in pars]
        in_specs += [pl.BlockSpec((1,) + tuple(t.shape[1:]), lambda c: (rev(c), 0, 0, 0)) for t in kept[k]]
        args += [r[0] for r in rows] + [r[0] for r in rows[:n_halo]] + list(pars) + list(kept[k])
        out_specs += [pl.BlockSpec(p.shape, lambda c: (0, 0)) for p in pars]
        out_shape += [jax.ShapeDtypeStruct(p.shape, F32) for p in pars]
        scratch_shapes += [pltpu.VMEM((NB, D, D), F32)] + [pltpu.VMEM((HALO, w), F32) for (_, w, _) in rows[:n_halo]]
    res = pl.pallas_call(
        body, name=name, grid=(nc,),
        in_specs=in_specs + [pl.BlockSpec((C, dy.shape[1]), lambda c: (rev(c), 0))],
        out_specs=[pl.BlockSpec((C, sum(widths)), lambda c: (rev(c), 0))] + out_specs,
        out_shape=[jax.ShapeDtypeStruct((Lp, sum(widths)), BF16)] + out_shape,
        scratch_shapes=scratch_shapes, compiler_params=_params(("arbitrary",)),
    )(*args, dy)
    d_pars, at = [], 1
    for n in n_par:
        d_pars.append(list(res[at:at + n]))
        at += n
    return res[0], d_pars


def _loss_head(h, y, wo, target, fw, tm=2 * ROW_TILE):
    Lp, Dm = h.shape
    Dk = y.shape[1]
    S = target.shape[0]
    n_real = N_META + S
    tm = _pick(Lp, tm, 16)
    n_tiles = Lp // tm
    assert n_tiles >= 2 and N_META < tm <= S and Lp - n_real < tm and (S - tm) % 8 == 0 and N_META % 8 == 0

    def out_fn(z, fw_):
        return z * lax.rsqrt(jnp.mean(z * z, axis=-1, keepdims=True) + NORM_EPS) * fw_

    def body(h_ref, y_ref, wo_ref, t_ref, fw_ref, loss_ref, dz_ref, dz16_ref, dfw_ref, dy_ref):
        i = pl.program_id(0)

        @pl.when(i == 0)
        def _():
            loss_ref[...] = jnp.zeros_like(loss_ref)
            dfw_ref[...] = jnp.zeros_like(dfw_ref)

        row = i * tm + lax.broadcasted_iota(jnp.int32, (tm, 1), 0)
        mask = ((row >= N_META) & (row < n_real)).astype(F32)
        z = h_ref[...] + jnp.dot(y_ref[...], wo_ref[...], preferred_element_type=F32)
        o, vjp = jax.vjp(out_fn, z, fw_ref[...])
        t = t_ref[...]
        t = jnp.where(i == 0, pltpu.roll(t, N_META, 0), t)
        if Lp > n_real:
            t = jnp.where(i == n_tiles - 1, pltpu.roll(t, tm - (Lp - n_real), 0), t)
        err = (o - t) * mask
        row_loss = 0.5 * jnp.mean(jnp.square(err), axis=-1, keepdims=True)
        dz, dfw = vjp(err * (1.0 / Dm))
        loss_ref[...] += jnp.sum(row_loss, axis=0, keepdims=True)
        dz_ref[...] = dz
        dz16_ref[...] = dz.astype(BF16)
        dfw_ref[...] += dfw
        dy_ref[...] = lax.dot_general(dz.astype(BF16), wo_ref[...], (((1,), (1,)), ((), ())),
                                      preferred_element_type=F32)

    row_spec = pl.BlockSpec((tm, Dm), lambda i: (i, 0))
    target_spec = pl.BlockSpec((pl.Element(tm), pl.Element(Dm)),
                               lambda i: (pl.multiple_of(jnp.clip(i * tm - N_META, 0, S - tm), 8), 0))
    return pl.pallas_call(
        body, name="loss_head", grid=(n_tiles,),
        in_specs=[row_spec, pl.BlockSpec((tm, Dk), lambda i: (i, 0)), pl.BlockSpec((Dk, Dm), lambda i: (0, 0)),
                  target_spec, pl.BlockSpec((1, Dm), lambda i: (0, 0))],
        out_specs=[pl.BlockSpec((8, 128), lambda i: (0, 0)), row_spec, row_spec,
                   pl.BlockSpec((1, Dm), lambda i: (0, 0)), pl.BlockSpec((tm, Dk), lambda i: (i, 0))],
        out_shape=[jax.ShapeDtypeStruct((8, 128), F32), jax.ShapeDtypeStruct((Lp, Dm), F32),
                   jax.ShapeDtypeStruct((Lp, Dm), BF16), jax.ShapeDtypeStruct((1, Dm), F32),
                   jax.ShapeDtypeStruct((Lp, Dk), F32)],
        compiler_params=_params(("arbitrary",)),
    )(h, y, wo, target, fw)


def _exchange(name, x, masks, slot_kind, per_dest, n_split=1, copy_own=True, other_half=False):
    n = len(masks)
    keep_own = slot_kind is not None and copy_own
    n_slots = {"chip": 4, "core": 2, "dev": 8, None: n}[slot_kind]
    blk_shape = x.shape[1:] if per_dest else x.shape
    if other_half:
        blk_shape = (x.shape[0] // 2,) + tuple(x.shape[1:])
    rows = blk_shape[0] // n_split

    def body(x_ref, o_ref, send_sems, recv_sems, local_sems):
        mx, my, mc = lax.axis_index("x"), lax.axis_index("y"), lax.axis_index("c")
        if other_half:
            x_ref = x_ref.at[pl.ds((1 - mc) * blk_shape[0], blk_shape[0])]

        def slot(k, px, py, pc):
            return {"chip": 2 * px + py, "core": pc, "dev": 4 * px + 2 * py + pc, None: k}[slot_kind]

        def peer(m):
            return (mx + m[0]) % 2, (my + m[1]) % 2, (mc + m[2]) % 2

        def part(ref, j):
            return ref.at[pl.ds(j * rows, rows)]

        own_src = x_ref.at[2 * mx + my] if per_dest else x_ref
        local = []
        if keep_own:
            own_dst = o_ref.at[slot(0, mx, my, mc)]
            local = [pltpu.make_async_copy(part(own_src, j), part(own_dst, j), local_sems.at[j])
                     for j in range(n_split)]
        for cp in local:
            cp.start()
        sends = []
        for k, m in enumerate(masks):
            px, py, pc = peer(m)
            src = x_ref.at[2 * px + py] if per_dest else x_ref
            dst = o_ref.at[slot(k, mx, my, mc)]
            for j in range(n_split):
                sends.append(pltpu.make_async_remote_copy(
                    src_ref=part(src, j), dst_ref=part(dst, j), send_sem=send_sems.at[k * n_split + j],
                    recv_sem=recv_sems.at[k * n_split + j], device_id=(px, py, pc), device_id_type=MESH))
        for cp in sends:
            cp.start()
        for k, m in enumerate(masks):
            px, py, pc = peer(m)
            landed = o_ref.at[slot(k, px, py, pc)]
            for j in range(n_split):
                pltpu.make_async_remote_copy(
                    src_ref=part(own_src, j), dst_ref=part(landed, j), send_sem=send_sems.at[k * n_split + j],
                    recv_sem=recv_sems.at[k * n_split + j], device_id=(px, py, pc), device_id_type=MESH).wait_recv()
        for cp in sends:
            cp.wait_send()
        for cp in local:
            cp.wait()

    return pl.pallas_call(
        body, name=name,
        in_specs=[pl.BlockSpec(memory_space=pl.ANY)], out_specs=pl.BlockSpec(memory_space=pl.ANY),
        out_shape=jax.ShapeDtypeStruct((n_slots,) + tuple(blk_shape), x.dtype),
        scratch_shapes=[pltpu.SemaphoreType.DMA((n * n_split,)), pltpu.SemaphoreType.DMA((n * n_split,)),
                        pltpu.SemaphoreType.DMA((n_split,))],
        compiler_params=pltpu.CompilerParams(has_side_effects=True),
    )(x)


CHIP_MASKS = [(1, 0, 0), (0, 1, 0), (1, 1, 0)]
CORE_MASKS = [(0, 0, 1)]
ALL_MASKS = [(dx, dy, dc) for dx in (0, 1) for dy in (0, 1) for dc in (0, 1) if (dx, dy, dc) != (0, 0, 0)]

HBM_SPEC = pl.BlockSpec(memory_space=pltpu.HBM)
SEM_SPEC = pl.BlockSpec(memory_space=pltpu.SEMAPHORE)
DATAFLOW = pltpu.SideEffectType.DATAFLOW_SIDE_EFFECTING


def _split_copies(x_ref, land_ref, send_sems, recv_sems, masks, slot_kind, per_dest, n_split, with_recvs=True):
    mx, my, mc = lax.axis_index("x"), lax.axis_index("y"), lax.axis_index("c")
    slot = lambda px, py, pc: {"chip": 2 * px + py, "core": pc}[slot_kind]
    rows = (x_ref.shape[1] if per_dest else x_ref.shape[0]) // n_split
    part = lambda ref, j: ref.at[pl.ds(j * rows, rows)]
    sends, recvs = [], []
    for k, m in enumerate(masks):
        px, py, pc = (mx + m[0]) % 2, (my + m[1]) % 2, (mc + m[2]) % 2
        src = x_ref.at[2 * px + py] if per_dest else x_ref
        own_src = x_ref.at[2 * mx + my] if per_dest else x_ref
        for j in range(n_split):
            sems = dict(send_sem=send_sems.at[k * n_split + j], recv_sem=recv_sems.at[k * n_split + j],
                        device_id=(px, py, pc), device_id_type=MESH)
            sends.append(pltpu.make_async_remote_copy(
                src_ref=part(src, j), dst_ref=part(land_ref.at[slot(mx, my, mc)], j), **sems))
            if with_recvs:
                recvs.append(pltpu.make_async_remote_copy(
                    src_ref=part(own_src, j), dst_ref=part(land_ref.at[slot(px, py, pc)], j), **sems))
    return sends, recvs


def _exchange_start(name, x, masks, slot_kind, per_dest, n_split, dep=None):
    n = len(masks) * n_split
    blk_shape = x.shape[1:] if per_dest else x.shape
    land_shape = ({"chip": 4, "core": 2}[slot_kind],) + tuple(blk_shape)
    deps = [] if dep is None else [dep]

    def body(x_ref, land_ref, *rest):
        send_sems, recv_sems, x_thru, land_thru, token = rest[len(deps):]
        for cp in _split_copies(x_ref, land_ref, send_sems, recv_sems, masks, slot_kind, per_dest, n_split, False)[0]:
            cp.start()
        token[...] = jnp.zeros_like(token)

    return pl.pallas_call(
        body, name=name,
        out_shape=(pltpu.SemaphoreType.DMA((n,)), pltpu.SemaphoreType.DMA((n,)), pltpu.HBM(x.shape, x.dtype),
                   pltpu.HBM(land_shape, x.dtype), jax.ShapeDtypeStruct((8, 128), F32)),
        in_specs=(HBM_SPEC, HBM_SPEC) + (pl.BlockSpec(memory_space=pl.ANY),) * len(deps),
        out_specs=(SEM_SPEC, SEM_SPEC, HBM_SPEC, HBM_SPEC, pl.BlockSpec(memory_space=pltpu.VMEM)),
        input_output_aliases={0: 2, 1: 3},
        compiler_params=pltpu.CompilerParams(has_side_effects=DATAFLOW),
    )(pltpu.with_memory_space_constraint(x, pltpu.HBM),
      pltpu.with_memory_space_constraint(lax.empty(land_shape, x.dtype), pltpu.HBM), *deps)


def _exchange_wait(name, started, after, masks, slot_kind, per_dest, n_split):
    send_sems, recv_sems, x_thru, land_thru, _ = started
    after = list(after) if isinstance(after, (list, tuple)) else [after]

    def body(x_ref, land_ref, send_sems, recv_sems, *after_and_out_refs):
        sends, recvs = _split_copies(x_ref, land_ref, send_sems, recv_sems, masks, slot_kind, per_dest, n_split)
        for cp in sends:
            cp.wait_send()
        for cp in recvs:
            cp.wait_recv()

    return pl.pallas_call(
        body, name=name,
        out_shape=(pltpu.HBM(x_thru.shape, x_thru.dtype), pltpu.HBM(land_thru.shape, land_thru.dtype)),
        in_specs=(HBM_SPEC, HBM_SPEC, SEM_SPEC, SEM_SPEC) + (pl.BlockSpec(memory_space=pl.ANY),) * len(after),
        out_specs=(HBM_SPEC, HBM_SPEC), input_output_aliases={0: 0, 1: 1},
        compiler_params=pltpu.CompilerParams(has_side_effects=DATAFLOW),
    )(x_thru, land_thru, send_sems, recv_sems, *after)


def _gather_chips(name, x):
    return _exchange(name, x, CHIP_MASKS, "chip", False)


def _gather_shards(name, shard, n_split, overlap=False, dep=None):
    half = shard.shape[0] // 2
    mine = lax.dynamic_slice_in_dim(shard, lax.axis_index("c") * half, half, axis=0)
    if overlap:
        return _exchange_start(name + "_chips", mine, CHIP_MASKS, "chip", False, n_split, dep)
    by_chip = _exchange(name + "_chips", mine, CHIP_MASKS, "chip", False, n_split, copy_own=False)
    return _gather_tail(name, mine, by_chip)


def _gather_tail(name, mine, by_chip, arrange=None):
    c, chip = lax.axis_index("c"), 2 * lax.axis_index("x") + lax.axis_index("y")
    by_chip = lax.dynamic_update_index_in_dim(by_chip, mine, chip, 0)
    if arrange is not None:
        by_chip = arrange(by_chip)
    both = _exchange(name + "_cores", by_chip, CORE_MASKS, "core", False, N_CHIPS, copy_own=False)
    return lax.dynamic_update_index_in_dim(both, by_chip, c, 0)


def _gather_finish(name, started, after, n_split, arrange=None):
    mine, by_chip = _exchange_wait(name + "_chips_wait", started, after, CHIP_MASKS, "chip", False, n_split)
    return _gather_tail(name, mine, by_chip, arrange)


def _sum_slots(name, x, tr=128):
    S, R, N = x.shape
    tr = _pick(R, tr, 16)

    def body(x_ref, o_ref):
        acc = x_ref[0].astype(F32)
        for s in range(1, S):
            acc = acc + x_ref[s].astype(F32)
        o_ref[...] = acc

    return pl.pallas_call(
        body, name=name, grid=(R // tr,),
        in_specs=[pl.BlockSpec((S, tr, N), lambda i: (0, i, 0))], out_specs=pl.BlockSpec((tr, N), lambda i: (i, 0)),
        out_shape=jax.ShapeDtypeStruct((R, N), F32), compiler_params=_params(("parallel",)),
    )(x)


def _add_to_bf16(name, a, b, tr=128):
    S, R, N = a.shape
    tr = _pick(R, tr, 16)

    def body(a_ref, b_ref, o_ref):
        o_ref[...] = (a_ref[...] + b_ref[...]).astype(BF16)

    spec = pl.BlockSpec((S, tr, N), lambda i: (0, i, 0))
    return pl.pallas_call(
        body, name=name, grid=(R // tr,), in_specs=[spec, spec], out_specs=spec,
        out_shape=jax.ShapeDtypeStruct((S, R, N), BF16), compiler_params=_params(("parallel",)),
    )(a, b)


def _add_slabs_to_bf16(name, full, recv, c, tr=64):
    R, NP = full.shape
    half = R // 2
    nb = half // tr

    def body(c_ref, a_ref, b_ref, o_ref):
        x = a_ref[...] + b_ref[...]
        for s in range(N_CHIPS - 1):
            o_ref[s] = x[:, s * SHARD_COLS:(s + 1) * SHARD_COLS].astype(BF16)
        last = jnp.concatenate([x[:, (N_CHIPS - 1) * SHARD_COLS:GAP_AT], x[:, GAP_AT + GAP:]], axis=1)
        o_ref[N_CHIPS - 1] = last.astype(BF16)

    grid_spec = pltpu.PrefetchScalarGridSpec(
        num_scalar_prefetch=1, grid=(nb,),
        in_specs=[pl.BlockSpec((tr, NP), lambda i, c_ref: (c_ref[0] * nb + i, 0)),
                  pl.BlockSpec((tr, NP), lambda i, c_ref: (i, 0))],
        out_specs=pl.BlockSpec((N_CHIPS, tr, SHARD_COLS), lambda i, c_ref: (0, i, 0)))
    return pl.pallas_call(
        body, name=name, grid_spec=grid_spec,
        out_shape=jax.ShapeDtypeStruct((N_CHIPS, half, SHARD_COLS), BF16), compiler_params=_params(("parallel",)),
    )(jnp.reshape(c, (1,)).astype(jnp.int32), full, recv)


def _adamw(name, gparts, w, m, v, tr=128):
    S, R, N = gparts.shape
    tr = _pick(R, tr)
    c1 = 1.0 / (1.0 - ADAM_B1 ** ADAM_STEP)
    c2 = 1.0 / (1.0 - ADAM_B2 ** ADAM_STEP)

    def body(g_ref, w_ref, m_ref, v_ref, go_ref, d_ref, mo_ref, vo_ref):
        g = g_ref[0]
        for s in range(1, S):
            g = g + g_ref[s]
        m_new = ADAM_B1 * m_ref[...] + (1.0 - ADAM_B1) * g
        v_new = ADAM_B2 * v_ref[...] + (1.0 - ADAM_B2) * jnp.square(g)
        go_ref[...] = g
        mo_ref[...] = m_new
        vo_ref[...] = v_new
        d_ref[...] = -ADAM_LR * ((m_new * c1) / (jnp.sqrt(v_new * c2) + ADAM_EPS) + ADAM_WD * w_ref[...])

    spec = pl.BlockSpec((tr, N), lambda i: (i, 0))
    return pl.pallas_call(
        body, name=name, grid=(R // tr,),
        in_specs=[pl.BlockSpec((S, tr, N), lambda i: (0, i, 0)), spec, spec, spec], out_specs=[spec] * 4,
        out_shape=[jax.ShapeDtypeStruct((R, N), F32)] * 4, compiler_params=_params(("parallel",)),
    )(gparts, w, m, v)


def _reduce_to_shard(name, slabs, n_split, by_columns=False, overlap=False):
    c, chip = lax.axis_index("c"), 2 * lax.axis_index("x") + lax.axis_index("y")
    if by_columns:
        R, N = slabs.shape[0], SHARD_COLS
        from_sibling = _exchange(name + "_sib", slabs, CORE_MASKS, None, False, n_split, other_half=True)[0]
        wire = _add_slabs_to_bf16(name + "_add", slabs, from_sibling, c)
    else:
        _, R, N = slabs.shape
        half = R // 2
        halves = slabs.reshape(N_CHIPS, 2, half, N)
        mine = lax.dynamic_index_in_dim(halves, c, axis=1, keepdims=False)
        theirs = lax.dynamic_index_in_dim(halves, 1 - c, axis=1, keepdims=False)
        from_sibling = _exchange(name + "_sib", theirs, CORE_MASKS, None, False, N_CHIPS)[0]
        wire = _add_to_bf16(name + "_add", mine, from_sibling)
    if overlap:
        return _exchange_start(name + "_chips", wire, CHIP_MASKS, "chip", True, n_split), (R, N)
    got = _exchange(name + "_chips", wire, CHIP_MASKS, "chip", True, n_split, copy_own=False)
    return _reduce_tail(name, wire, got, n_split, R, N)


def _reduce_tail(name, wire, got, n_split, R, N):
    c, chip = lax.axis_index("c"), 2 * lax.axis_index("x") + lax.axis_index("y")
    got = lax.dynamic_update_index_in_dim(got, lax.dynamic_index_in_dim(wire, chip, 0, keepdims=False), chip, 0)
    part = _sum_slots(name + "_sum", got)
    both = _exchange(name + "_cores", part, CORE_MASKS, "core", False, n_split, copy_own=False)
    return lax.dynamic_update_index_in_dim(both, part, c, 0).reshape(1, R, N)


def _reduce_finish(name, started, after, n_split):
    handle, (R, N) = started
    wire, got = _exchange_wait(name + "_chips_wait", handle, after, CHIP_MASKS, "chip", True, n_split)
    return _reduce_tail(name, wire, got, n_split, R, N)


def _pack(pieces, cols, row_mult=8):
    flat = jnp.concatenate([p.reshape(-1) for p in pieces])
    rows = -(-flat.shape[0] // cols)
    rows = -(-rows // row_mult) * row_mult
    return jnp.pad(flat, (0, rows * cols - flat.shape[0])).reshape(rows, cols)


def _unpack(packed, shapes):
    flat = packed.reshape(-1)
    out, off = [], 0
    for shp in shapes:
        n = 1
        for d in shp:
            n *= d
        out.append(flat[off:off + n].reshape(shp))
        off += n
    return out


def kernel(x, meta_tokens, norm_w, w_in, rw_shift_mu, rw_w0, rw_w2, rw_a0, rw_a2, rw_k_k, rw_k_a, rw_r_k, rw_gn_w, rw_gn_b, dn_conv_w, dn_A_log, dn_dt_bias, dn_norm_w, w_out, final_norm_w, loss_target, m_meta_tokens, m_norm_w, m_w_in, m_rw_shift_mu, m_rw_w0, m_rw_w2, m_rw_a0, m_rw_a2, m_rw_k_k, m_rw_k_a, m_rw_r_k, m_rw_gn_w, m_rw_gn_b, m_dn_conv_w, m_dn_A_log, m_dn_dt_bias, m_dn_norm_w, m_w_out, m_final_norm_w, v_meta_tokens, v_norm_w, v_w_in, v_rw_shift_mu, v_rw_w0, v_rw_w2, v_rw_a0, v_rw_a2, v_rw_k_k, v_rw_k_a, v_rw_r_k, v_rw_gn_w, v_rw_gn_b, v_dn_conv_w, v_dn_A_log, v_dn_dt_bias, v_dn_norm_w, v_w_out, v_final_norm_w):
    S = x.shape[1]
    L = N_META + S
    Lp = -(-L // CHUNK) * CHUNK

    small_shapes = [(RW_LORA, 256), (RW_LORA, 256), (CONV_W, 768), (N_META, 512)]
    small_mine = _pack([rw_w2[0], rw_a2[0], dn_conv_w[0], meta_tokens], 1024)
    small_all = _gather_chips("gather_small", small_mine)
    per_chip = [_unpack(small_all[s], small_shapes) for s in range(N_CHIPS)]
    w2, a2, conv_w, meta = [jnp.concatenate([per_chip[s][i] for s in range(N_CHIPS)], axis=1) for i in range(4)]
    w_in_started = _gather_shards("gather_w_in", w_in[0].astype(BF16), 8, overlap=True, dep=small_all)
    w_out_started = _gather_shards("gather_w_out", w_out[0].astype(BF16), 8, overlap=True, dep=w_in_started[4])

    tail = [jnp.zeros((Lp - L, D_MODEL), F32)] if Lp > L else []
    h = jnp.concatenate([meta + w_in_started[4][:1, :1], x[0]] + tail, axis=0)
    (u,) = _rowwise("rms_in", _rms_fn, [_row(h)], [norm_w], [D_MODEL], tm=2 * ROW_TILE, out_dtype=BF16)
    u_t = u.T
    m_in, v_in = m_w_in[0] + w_in_started[4][:1, :1], v_w_in[0] + w_in_started[4][:1, :1]
    cut = GAP_AT - (N_CHIPS - 1) * SHARD_COLS

    def gapped(by_chip):
        last = by_chip[N_CHIPS - 1]
        return jnp.concatenate([by_chip[s] for s in range(N_CHIPS - 1)]
                               + [last[:, :cut], jnp.zeros((last.shape[0], GAP), BF16), last[:, cut:]], axis=1)

    W_gapped = _gather_finish("gather_w_in", w_in_started, [u, u_t, m_in, v_in], 8, arrange=gapped)
    W_gapped = W_gapped.reshape(D_MODEL, NP_COLS)
    p = _mm("in_proj", u, W_gapped, "nn", dep=w_out_started[4])

    mu = rw_shift_mu
    zpad = jnp.zeros((RW_LORA, RW_WIDTH), F32)
    rw_params = [mu[:, 0:1024], mu[:, 1024:2048], mu[:, 2048:3072], mu[:, 3072:3200], rw_w0,
                 jnp.concatenate([w2, zpad], axis=0), rw_a0, jnp.concatenate([zpad, a2], axis=0), rw_k_k, rw_k_a]
    rw_rows = [_row(p, 1024, OFF_R), _row(p, 1024, OFF_K), _row(p, 1024, OFF_V), _row(p, 128, OFF_S1),
               _row(p, 1024, OFF_GATE)]
    rw_pars = rw_params + [rw_r_k, rw_gn_w, rw_gn_b]
    rw_saved_shapes = [(N_GROUPS, CHUNK, GROUP), (N_GROUPS, CHUNK, GROUP), (N_GROUPS, 2 * CHUNK, 2 * GROUP),
                       (N_GROUPS, 2 * CHUNK, GROUP)]

    dn_rows = [_row(p, 1024, OFF_DQ), _row(p, 1024, OFF_DK), _row(p, 1024, OFF_DV), _row(p, 128, OFF_S2),
               _row(p, 1024, OFF_Z)]
    dn_pars = [conv_w[j:j + 1, 1024 * i:1024 * (i + 1)] for i in range(3) for j in range(CONV_W)]
    narrow = lambda t: jnp.pad(t, ((0, 0), (DN_HEADS, DN_HEAD - 2 * DN_HEADS)))
    dn_pars += [narrow(dn_A_log), narrow(dn_dt_bias), jnp.tile(dn_norm_w, (1, DN_HEADS))]
    dn_saved_shapes = [(N_GROUPS, CHUNK, CHUNK), (N_GROUPS, CHUNK, GROUP)]
    mixer_parts = [(_rw_fused_steps, rw_rows, 4, rw_pars, rw_saved_shapes),
                   (_dn_fused_steps, dn_rows, 3, dn_pars, dn_saved_shapes)]
    y, mixer_kept = _mixers_fwd("mixers_fwd", mixer_parts)
    w_out_all = _gather_finish("gather_w_out", w_out_started, y, 8)
    Wo = jnp.concatenate([w_out_all[:, s].reshape(D_MODEL // N_CHIPS, D_MODEL) for s in range(N_CHIPS)], axis=0)
    loss_acc, dz, dz16, d_fw, dy = _loss_head(h, y, Wo, loss_target[0], final_norm_w.reshape(1, D_MODEL))
    loss = lax.psum(loss_acc[0, 0], ("x", "y", "c"))

    d_wo = _mm("d_w_out", y.T, dz16, "nn", tm=512, tn=1024, tk=Lp)
    wo_started = _reduce_to_shard("rs_w_out", d_wo.reshape(N_CHIPS, D_MODEL // N_CHIPS, D_MODEL), 8, overlap=True)
    rw_pars_after = rw_pars[:-1] + [rw_pars[-1] + wo_started[0][4][:1, :1]]
    bwd_parts = [(_rw_fused_steps, rw_rows, 4, rw_pars_after, rw_saved_shapes), mixer_parts[1]]

    dp, (d_rw_pars, d_dn_pars) = _mixers_bwd("mixers_bwd", bwd_parts, mixer_kept, dy)
    d_prep_pars, d_rw_pars = d_rw_pars[:10], d_rw_pars[10:]
    d_conv_parts = [jnp.concatenate(d_dn_pars[4 * i:4 * i + 4], axis=0) for i in range(3)]
    d_a_log_b, d_dt_b = d_dn_pars[12], d_dn_pars[13]
    d_dn_nw = jnp.sum(d_dn_pars[14].reshape(DN_HEADS, DN_HEAD), axis=0, keepdims=True)
    d_W = _mm("d_w_in", u_t, dp, "nn", tm=1024, tn=768, tk=Lp)
    wi_started = _reduce_to_shard("rs_w_in", d_W, 8, by_columns=True, overlap=True)
    du = _mm("d_in_proj", dp, W_gapped, "nt", tn=2048, tk=1408, dep=wi_started[0][4])
    norm_res = lambda h_, w_: (_rms_fn(h_, w_)[0], h_)
    (dx,), (d_norm_w_x,) = _rowwise_bwd("rms_in_bwd", norm_res, [_row(h)], [norm_w], [_row(du), _row(dz)],
                                        tm=2 * GROUP, row0=N_META, n_rows=S)
    (d_meta,), (d_norm_w_meta,) = _rowwise_bwd("rms_meta_bwd", norm_res, [_row(h)], [norm_w], [_row(du), _row(dz)],
                                               n_rows=N_META)
    d_norm_w = d_norm_w_x + d_norm_w_meta
    grad_x = dx[None]

    d_mu = jnp.concatenate(d_prep_pars[0:4], axis=1)
    d_w2, d_a2 = d_prep_pars[5][:RW_LORA], d_prep_pars[7][RW_LORA:]
    d_conv = jnp.concatenate(d_conv_parts, axis=1)
    head_sum = lambda t: t[:, DN_HEADS:2 * DN_HEADS]
    rep_names = ["norm_w", "rw_shift_mu", "rw_w0", "rw_a0", "rw_k_k", "rw_k_a", "rw_r_k", "rw_gn_w", "rw_gn_b",
                 "dn_A_log", "dn_dt_bias", "dn_norm_w", "final_norm_w"]
    rep_g = [d_norm_w, d_mu, d_prep_pars[4], d_prep_pars[6], d_prep_pars[8], d_prep_pars[9],
             d_rw_pars[0], d_rw_pars[1], d_rw_pars[2],
             head_sum(d_a_log_b), head_sum(d_dt_b), d_dn_nw, d_fw.reshape(D_MODEL)]
    rep_w = [norm_w, rw_shift_mu, rw_w0, rw_a0, rw_k_k, rw_k_a, rw_r_k, rw_gn_w, rw_gn_b, dn_A_log, dn_dt_bias,
             dn_norm_w, final_norm_w]
    rep_m = [m_norm_w, m_rw_shift_mu, m_rw_w0, m_rw_a0, m_rw_k_k, m_rw_k_a, m_rw_r_k, m_rw_gn_w, m_rw_gn_b,
             m_dn_A_log, m_dn_dt_bias, m_dn_norm_w, m_final_norm_w]
    rep_v = [v_norm_w, v_rw_shift_mu, v_rw_w0, v_rw_a0, v_rw_k_k, v_rw_k_a, v_rw_r_k, v_rw_gn_w, v_rw_gn_b,
             v_dn_A_log, v_dn_dt_bias, v_dn_norm_w, v_final_norm_w]
    rep_shapes = [t.shape for t in rep_w]
    rep_all = _exchange("gather_rep_grads", _pack(rep_g, 128), ALL_MASKS, "dev", False)
    rep_out = _adamw("adam_rep", rep_all, _pack(rep_w, 128), _pack(rep_m, 128), _pack(rep_v, 128))
    rep_out = [dict(zip(rep_names, _unpack(t, rep_shapes))) for t in rep_out]

    sm_slabs = jnp.stack([_pack([d_w2[:, 256 * s:256 * (s + 1)], d_a2[:, 256 * s:256 * (s + 1)],
                                 d_conv[:, 768 * s:768 * (s + 1)], d_meta[:, 512 * s:512 * (s + 1)]], 1024, 64)
                          for s in range(N_CHIPS)])
    sm_parts = _reduce_to_shard("rs_small", sm_slabs, 1)
    sm_w = [rw_w2[0], rw_a2[0], dn_conv_w[0], meta_tokens]
    sm_m = [m_rw_w2[0], m_rw_a2[0], m_dn_conv_w[0], m_meta_tokens]
    sm_v = [v_rw_w2[0], v_rw_a2[0], v_dn_conv_w[0], v_meta_tokens]
    sm_out = _adamw("adam_small", sm_parts, _pack(sm_w, 1024, 64), _pack(sm_m, 1024, 64), _pack(sm_v, 1024, 64))
    sm_names = ["rw_w2", "rw_a2", "dn_conv_w", "meta_tokens"]
    sm_full_shapes = [(1, RW_LORA, 256), (1, RW_LORA, 256), (1, CONV_W, 768), (N_META, 512)]
    sm_out = [dict(zip(sm_names, [t.reshape(shp) for t, shp in zip(_unpack(o, small_shapes), sm_full_shapes)]))
              for o in sm_out]

    wo_parts = _reduce_finish("rs_w_out", wo_started, dp, 8)
    wo_out = _adamw("adam_w_out", wo_parts, w_out[0], m_w_out[0], v_w_out[0])
    wi_parts = _reduce_finish("rs_w_in", wi_started, dx, 8)
    wi_out = _adamw("adam_w_in", wi_parts, w_in[0], m_in, v_in)

    order = ["meta_tokens", "norm_w", "w_in", "rw_shift_mu", "rw_w0", "rw_w2", "rw_a0", "rw_a2", "rw_k_k", "rw_k_a",
             "rw_r_k", "rw_gn_w", "rw_gn_b", "dn_conv_w", "dn_A_log", "dn_dt_bias", "dn_norm_w", "w_out",
             "final_norm_w"]
    outs = [loss, grad_x]
    for kind in range(4):
        table = dict(rep_out[kind])
        table.update(sm_out[kind])
        table["w_in"] = wi_out[kind][None]
        table["w_out"] = wo_out[kind][None]
        outs += [table[n] for n in order]
    return tuple(outs)
```

```python
import functools

import jax
import jax.numpy as jnp
from jax import lax
from jax.experimental import pallas as pl
from jax.experimental.pallas import tpu as pltpu

F32 = jnp.float32
BF16 = jnp.bfloat16
HIGH = lax.Precision.HIGH
HIGHEST = lax.Precision.HIGHEST
MESH = pl.DeviceIdType.MESH

D_MODEL = 2048
N_META = 16
RW_WIDTH = 1024
RW_HEAD = 64
RW_HEADS = 16
RW_LORA = 64
RW_GN_EPS = 64e-5
DN_WIDTH = 1024
DN_HEAD = 128
DN_HEADS = 8
CONV_W = 4
CHUNK = 64
NORM_EPS = 1e-6
IN_COLS = 8336
N_CHIPS = 4
SHARD_COLS = IN_COLS // N_CHIPS

NP_COLS = 8 * 1024 + 256
GAP_AT = IN_COLS - DN_WIDTH
GAP = NP_COLS - IN_COLS
OFF_R, OFF_K, OFF_V, OFF_S1, OFF_GATE = 0, 1024, 2048, 3072, 3200
OFF_DQ, OFF_DK, OFF_DV, OFF_S2, OFF_Z = 4224, 5248, 6272, 7296, 7424

ADAM_LR = 0.001
ADAM_B1 = 0.9
ADAM_B2 = 0.999
ADAM_EPS = 1e-08
ADAM_WD = 0.01
ADAM_STEP = 10

VMEM_LIMIT_BYTES = 56 * 1024 * 1024
ROW_TILE = 104
MM_ROW_TILE = 832


def _params(sem=None):
    return pltpu.CompilerParams(dimension_semantics=sem, vmem_limit_bytes=VMEM_LIMIT_BYTES)


def _pick(n, target, mult=8):
    best = None
    for d in range(mult, min(n, target) + 1, mult):
        if n % d == 0:
            best = d
    return n if best is None else best


def _mm(name, a, b, mode, tm=MM_ROW_TILE, tn=1408, tk=2048, dep=None):
    if mode == "nn":
        (M, K), (_, N) = a.shape, b.shape
    elif mode == "nt":
        (M, K), (N, _) = a.shape, b.shape
    else:
        (K, M), (_, N) = a.shape, b.shape
    tm = _pick(M, tm, 128 if mode == "tn" else 16)
    tn = _pick(N, tn, 128)
    tk = _pick(K, tk, 8 if mode == "tn" else 128)
    if mode == "nn":
        a_spec = pl.BlockSpec((tm, tk), lambda i, j, k: (i, k))
        b_spec = pl.BlockSpec((tk, tn), lambda i, j, k: (k, j))
        dims = (((1,), (0,)), ((), ()))
    elif mode == "nt":
        a_spec = pl.BlockSpec((tm, tk), lambda i, j, k: (i, k))
        b_spec = pl.BlockSpec((tn, tk), lambda i, j, k: (j, k))
        dims = (((1,), (1,)), ((), ()))
    else:
        a_spec = pl.BlockSpec((tk, tm), lambda i, j, k: (k, i))
        b_spec = pl.BlockSpec((tk, tn), lambda i, j, k: (k, j))
        dims = (((0,), (0,)), ((), ()))

    def body(a_ref, b_ref, *rest):
        o_ref = rest[-1]

        @pl.when(pl.program_id(2) == 0)
        def _():
            o_ref[...] = jnp.zeros_like(o_ref)

        o_ref[...] += lax.dot_general(a_ref[...].astype(BF16), b_ref[...].astype(BF16), dims,
                                      preferred_element_type=F32)

    deps = [] if dep is None else [dep]
    return pl.pallas_call(
        body, name=name, grid=(M // tm, N // tn, K // tk),
        in_specs=[a_spec, b_spec] + [pl.BlockSpec((8, 128), lambda i, j, k: (0, 0))] * len(deps),
        out_specs=pl.BlockSpec((tm, tn), lambda i, j, k: (i, j)),
        out_shape=jax.ShapeDtypeStruct((M, N), F32),
        compiler_params=_params(("parallel", "parallel", "arbitrary")),
    )(a, b, *deps)


def _row(arr, width=None, cb=0):
    return (arr, arr.shape[1] if width is None else width, cb)


def _rowwise(name, fn, rows, params, out_widths, tm=ROW_TILE, out_dtype=F32):
    R = rows[0][0].shape[0]
    tm = _pick(R, tm, 16 if out_dtype == BF16 else 8)
    n_r, n_p = len(rows), len(params)

    def body(*refs):
        vals = [r[...] for r in refs[:n_r + n_p]]
        for o_ref, val in zip(refs[n_r + n_p:], fn(*vals)):
            o_ref[...] = val.astype(out_dtype)

    in_specs = [pl.BlockSpec((tm, w), lambda i, cb=cb: (i, cb)) for (_, w, cb) in rows]
    in_specs += [pl.BlockSpec(p.shape, lambda i: (0, 0)) for p in params]
    return pl.pallas_call(
        body, name=name, grid=(R // tm,), in_specs=in_specs,
        out_specs=[pl.BlockSpec((tm, w), lambda i: (i, 0)) for w in out_widths],
        out_shape=[jax.ShapeDtypeStruct((R, w), out_dtype) for w in out_widths],
        compiler_params=_params(("parallel",)),
    )(*[r[0] for r in rows], *params)


def _rowwise_bwd(name, fn, rows, params, douts, tm=ROW_TILE, row0=0, n_rows=None):
    R = rows[0][0].shape[0] - row0 if n_rows is None else n_rows
    tm = _pick(R, tm)
    assert row0 % 8 == 0 and row0 + R <= rows[0][0].shape[0]

    def in_spec(w, cb):
        if row0 == 0:
            return pl.BlockSpec((tm, w), lambda i: (i, cb))
        return pl.BlockSpec((pl.Element(tm), pl.Element(w)), lambda i: (pl.multiple_of(row0 + i * tm, 8), cb * w))

    n_r, n_p, n_d = len(rows), len(params), len(douts)

    def body(*refs):
        vals = [r[...] for r in refs[:n_r + n_p]]
        cts = tuple(r[...] for r in refs[n_r + n_p:n_r + n_p + n_d])
        grads = jax.vjp(fn, *vals)[1](cts)
        outs = refs[n_r + n_p + n_d:]
        for o_ref, g in zip(outs[:n_r], grads[:n_r]):
            o_ref[...] = g

        @pl.when(pl.program_id(0) == 0)
        def _():
            for o_ref in outs[n_r:]:
                o_ref[...] = jnp.zeros_like(o_ref)

        for o_ref, g in zip(outs[n_r:], grads[n_r:]):
            o_ref[...] += g

    in_specs = [in_spec(w, cb) for (_, w, cb) in rows]
    in_specs += [pl.BlockSpec(p.shape, lambda i: (0, 0)) for p in params]
    in_specs += [in_spec(w, cb) for (_, w, cb) in douts]
    out_specs = [pl.BlockSpec((tm, w), lambda i: (i, 0)) for (_, w, _) in rows]
    out_specs += [pl.BlockSpec(p.shape, lambda i: (0, 0)) for p in params]
    out_shape = [jax.ShapeDtypeStruct((R, w), F32) for (_, w, _) in rows]
    out_shape += [jax.ShapeDtypeStruct(p.shape, F32) for p in params]
    res = pl.pallas_call(
        body, name=name, grid=(R // tm,), in_specs=in_specs, out_specs=out_specs, out_shape=out_shape,
        compiler_params=_params(("arbitrary",)),
    )(*[r[0] for r in rows], *params, *[d[0] for d in douts])
    return res[:n_r], res[n_r:]


def _softplus(x):
    return jnp.maximum(x, 0.0) + jnp.log(1.0 + jnp.exp(-jnp.abs(x)))


def _silu(x):
    return x * jax.nn.sigmoid(x)


def _rms_fn(h, w):
    return (h * lax.rsqrt(jnp.mean(h * h, axis=-1, keepdims=True) + NORM_EPS) * w,)


def _rw_prep_fn(pr, pr1, pk, pk1, pv, pv1, ps, ps1, mu_r, mu_k, mu_v, mu_s, w0, w2p, a0, a2p, k_k, k_a):
    r = pr + (pr1 - pr) * mu_r
    k = pk + (pk1 - pk) * mu_k
    v = pv + (pv1 - pv) * mu_v
    s = ps + (ps1 - ps) * mu_s
    lora = lambda x, w: jnp.dot(x.astype(BF16), w.astype(BF16), preferred_element_type=F32)
    w_log = -_softplus(-(w0 + lora(jnp.tanh(s), w2p))) - 0.5
    log_decay = -jnp.exp(w_log)
    a = jax.nn.sigmoid(a0 + lora(s, a2p))
    return r, log_decay, k * (1.0 + (a - 1.0) * k_a), v, k * k_k, a


def _conv_fn(u0, u1, u2, u3, w0, w1, w2, w3):
    return (_silu(u0 * w3 + u1 * w2 + u2 * w1 + u3 * w0),)


def _dn_gate_fn(ps2, a_log_n, dt_n):
    beta_n = jax.nn.sigmoid(ps2)
    g_n = -jnp.exp(a_log_n) * _softplus(ps2 + dt_n)
    lane = lax.broadcasted_iota(jnp.int32, (1, DN_HEAD), 1)
    pick = lambda x, j: jnp.broadcast_to(jnp.sum(x * (lane == j).astype(F32), axis=-1, keepdims=True), x.shape)[None]
    beta = jnp.concatenate([pick(beta_n, h) for h in range(DN_HEADS)], axis=0)
    g = jnp.concatenate([pick(g_n, DN_HEADS + h) for h in range(DN_HEADS)], axis=0)
    return beta, g


def _tril_masks(c):
    t = lax.broadcasted_iota(jnp.int32, (c, c), 0)
    s = lax.broadcasted_iota(jnp.int32, (c, c), 1)
    return s <= t, s < t


def _dot(x, y, cx, cy, prec=HIGH):
    nb = x.ndim - 2
    batch = tuple(range(nb))
    return lax.dot_general(x, y, (((cx + nb,), (cy + nb,)), (batch, batch)), precision=prec,
                           preferred_element_type=F32)


INV_BASE = 8


def _run_interleaved(running):
    results, live = [None] * len(running), list(range(len(running)))
    while live:
        for k in list(live):
            try:
                next(running[k])
            except StopIteration as done:
                results[k] = done.value
                live.remove(k)
    return results


def _blocked_inv(a, mm, row, col, mm_merge=None):
    mm_merge = mm if mm_merge is None else mm_merge
    shift = INV_BASE.bit_length() - 1
    d = jnp.where(jnp.right_shift(row, shift) == jnp.right_shift(col, shift), a, 0.0)
    x = (row == col).astype(F32) - d
    p = mm(d, d)
    yield
    x = x + mm(x, p)
    yield
    p = mm(p, p)
    yield
    x = x + mm(x, p)
    yield
    size = INV_BASE
    while size < a.shape[-2]:
        same_pair = jnp.right_shift(row, shift + 1) == jnp.right_shift(col, shift + 1)
        lower_left = (jnp.right_shift(row, shift) & 1 == 1) & (jnp.right_shift(col, shift) & 1 == 0)
        lower = mm_merge(jnp.where(same_pair & lower_left, a, 0.0), x)
        yield
        x = x - mm_merge(x, lower)
        size, shift = 2 * size, shift + 1
        yield
    return x


def _tri_inv(a):
    c = a.shape[-1]
    row, col = lax.broadcasted_iota(jnp.int32, (c, c), 0), lax.broadcasted_iota(jnp.int32, (c, c), 1)
    return _blocked_inv(a, lambda x, y: _dot(x, y, 1, 0), row, col,
                        lambda x, y: _dot(x.astype(BF16), y.astype(BF16), 1, 0, None))


@jax.custom_vjp
def _tri_inv_saved(a, t):
    return t


def _tri_inv_saved_fwd(a, t):
    return t, t


def _tri_inv_saved_bwd(t, dt):
    return -_dot(_dot(t, dt, 0, 0), t, 1, 1), jnp.zeros_like(t)


_tri_inv_saved.defvjp(_tri_inv_saved_fwd, _tri_inv_saved_bwd)


def _pair_masks():
    lane = lax.broadcasted_iota(jnp.int32, (1, 2 * RW_HEAD), 1)
    m0 = (lane < RW_HEAD).astype(F32)
    return m0, 1.0 - m0


def _pair_bd(x):
    m0, m1 = _pair_masks()
    return jnp.concatenate([x * m0, x * m1], axis=-2)


def _pair_mm(x, y):
    return _dot(x, _pair_bd(y), 1, 0)


def _pair_inv(a):
    c = a.shape[-2]
    row = lax.broadcasted_iota(jnp.int32, (c, 2 * RW_HEAD), 0)
    col = lax.broadcasted_iota(jnp.int32, (c, 2 * RW_HEAD), 1) & (RW_HEAD - 1)
    return _blocked_inv(a, _pair_mm, row, col,
                        lambda x, y: _dot(x.astype(BF16), _pair_bd(y).astype(BF16), 1, 0, None))


@jax.custom_vjp
def _pair_inv_saved(a, t):
    return t


def _pair_inv_saved_fwd(a, t):
    return t, t


def _pair_inv_saved_bwd(t, dt):
    m0, m1 = _pair_masks()
    c = t.shape[-2]
    z = _dot(t, dt, 0, 0)
    x = z[:, :c, :] * m0 + z[:, c:, :] * m1
    return -_dot(x, _pair_bd(t), 1, 1), jnp.zeros_like(t)


_pair_inv_saved.defvjp(_pair_inv_saved_fwd, _pair_inv_saved_bwd)


@jax.custom_vjp
def _use_saved(x, x_saved):
    return x_saved


_use_saved.defvjp(lambda x, x_saved: (x_saved, None), lambda _, g: (g, jnp.zeros_like(g)))


def _rw_chunk_steps(S, r, lw, k2, v, kkp, a, gate, rk, gnw, gnb, saved=None):
    B, C, P = r.shape
    m0, m1 = _pair_masks()
    seg = lambda x: (jnp.sum(x * m0, axis=-1, keepdims=True) * m0 + jnp.sum(x * m1, axis=-1, keepdims=True) * m1)
    mm = lambda x, y: _dot(x.astype(BF16), y.astype(BF16), 1, 0, None)
    nt = lambda x, y: _dot(x, y, 1, 1)
    tn = lambda x, y: _dot(x, y, 0, 0)
    pair_mm = lambda x, y: mm(x, _pair_bd(y))
    t_idx = lax.broadcasted_iota(jnp.int32, (C, P), 0)
    s_idx = lax.broadcasted_iota(jnp.int32, (C, P), 1) & (RW_HEAD - 1)
    incl, strict = s_idx <= t_idx, s_idx < t_idx
    tril = jnp.broadcast_to(_tril_masks(C)[0].astype(F32), (B, C, C))
    kk = kkp * lax.rsqrt(seg(kkp * kkp) + 1e-6)
    b = kk * a
    reuse = (lambda x, i: x) if saved is None else (lambda x, i: _use_saved(x, saved[i]))
    g_incl = reuse(_dot(tril, lw, 1, 0), 1)
    g_excl = g_incl - lw
    inv = jnp.exp(-g_incl)
    alpha, beta, kappa, rho = kk * jnp.exp(g_excl), b * inv, k2 * inv, r * jnp.exp(g_incl)
    yield
    ar = jnp.concatenate([alpha, rho], axis=-2)
    scores = reuse(nt(ar, jnp.concatenate([_pair_bd(beta), _pair_bd(kappa)], axis=-2)), 2)
    a_ab = jnp.where(strict, scores[:, :C, :P], 0.0)
    a_ak = jnp.where(strict, scores[:, :C, P:], 0.0)
    r_b = jnp.where(incl, scores[:, C:, :P], 0.0)
    r_k = jnp.where(incl, scores[:, C:, P:], 0.0)
    yield
    t_inv = (yield from _pair_inv(a_ab)) if saved is None else _pair_inv_saved(a_ab, saved[0])
    on_state = reuse(nt(ar, S), 3)
    yield
    u = pair_mm(t_inv, -on_state[:, :C, :] - pair_mm(a_ak, v))
    yield
    y = on_state[:, C:, :] + mm(jnp.concatenate([r_b, r_k], axis=-1),
                                jnp.concatenate([_pair_bd(u), _pair_bd(v)], axis=-2))
    same_head = ((lax.broadcasted_iota(jnp.int32, (P, P), 0) < RW_HEAD)
                 == (lax.broadcasted_iota(jnp.int32, (P, P), 1) < RW_HEAD))
    yield
    fresh = tn(jnp.concatenate([u, v], axis=-2), jnp.concatenate([beta, kappa], axis=-2))
    yield
    S_new = jnp.exp(jnp.sum(lw, axis=-2, keepdims=True)) * (S + jnp.where(same_head, fresh, 0.0))
    dev = y - seg(y) * (1.0 / RW_HEAD)
    yn = dev * lax.rsqrt(seg(dev * dev) * (1.0 / RW_HEAD) + RW_GN_EPS) * gnw + gnb
    bonus = seg(r * k2 * rk) * v
    return (yn + bonus) * _silu(gate), S_new, (t_inv, g_incl, scores, on_state)


def _dn_chunk_steps(S, qc, kc, vc, bb, gb, z, nw, saved=None):
    B, C, D = qc.shape
    mm = lambda x, y: _dot(x.astype(BF16), y.astype(BF16), 1, 0, None)
    nt = lambda x, y, p=None: _dot(x, y, 1, 1, p) if p else _dot(x.astype(BF16), y.astype(BF16), 1, 1, None)
    tn = lambda x, y: _dot(x.astype(BF16), y.astype(BF16), 0, 0, None)
    incl, strict = _tril_masks(C)
    q = qc * lax.rsqrt(jnp.sum(qc * qc, axis=-1, keepdims=True) + 1e-6) * (D ** -0.5)
    k = kc * lax.rsqrt(jnp.sum(kc * kc, axis=-1, keepdims=True) + 1e-6)
    kb, vb = k * bb, vc * bb
    yield
    G = _dot(jnp.broadcast_to(incl.astype(F32), (B, C, C)), gb, 1, 0, HIGHEST)
    G = G if saved is None else _use_saved(G, saved[1])
    lane = lax.broadcasted_iota(jnp.int32, (C, D), 1)
    e0, e1 = (lane == 0).astype(F32), (lane == 1).astype(F32)
    diff = nt(G * e0 + e1, e0 - G * e1, HIGHEST)
    dmask = jnp.where(incl, jnp.exp(jnp.where(incl, diff, 0.0)), 0.0)
    M = jnp.where(strict, nt(kb, k) * dmask, 0.0)
    yield
    T = (yield from _tri_inv(M)) if saved is None else _tri_inv_saved(M, saved[0])
    eG = jnp.exp(G)
    u = mm(T, vb)
    yield
    w = mm(T, kb * eG)
    yield
    attn = jnp.where(incl, nt(q, k) * dmask, 0.0)
    yield
    v_new = u - mm(w, S)
    yield
    o = mm(q * eG, S) + mm(attn, v_new)
    yield
    g_last = jnp.sum(gb, axis=-2, keepdims=True)
    S_new = S * jnp.exp(jnp.broadcast_to(g_last, S.shape)) + tn(k * jnp.exp(g_last - G), v_new)
    on = o * lax.rsqrt(jnp.mean(o * o, axis=-1, keepdims=True) + NORM_EPS) * nw
    return on * _silu(z), S_new, (T, G)


N_GROUPS = 8
GROUP = 128
HALO = 8


def _groups(x):
    return jnp.concatenate([x[:, g * GROUP:(g + 1) * GROUP][None] for g in range(N_GROUPS)], axis=0)


@functools.partial(jax.custom_vjp, nondiff_argnums=(1,))
def _shift_rows(ext, j):
    return pltpu.roll(ext, j, 0)[HALO:, :]


def _shift_rows_fwd(ext, j):
    return _shift_rows(ext, j), None


def _shift_rows_bwd(j, _, d):
    z = jnp.concatenate([jnp.zeros((HALO, d.shape[1]), d.dtype), d], axis=0)
    return (pltpu.roll(z, z.shape[0] - j, 0),)


_shift_rows.defvjp(_shift_rows_fwd, _shift_rows_bwd)


def _rw_fused_steps(S, h_r, h_k, h_v, h_s, p_r, p_k, p_v, p_s, gate, *pars, saved=None):
    prev = lambda h, x: _shift_rows(jnp.concatenate([h, x], axis=0), 1)
    seq = _rw_prep_fn(p_r, prev(h_r, p_r), p_k, prev(h_k, p_k), p_v, prev(h_v, p_v), p_s, prev(h_s, p_s), *pars[:10])
    yield
    return (yield from _rw_chunk_steps(S, *[_groups(t) for t in seq], _groups(gate),
                                       *[_groups(t) for t in pars[10:]], saved=saved))


def _dn_fused_steps(S, h_q, h_k, h_v, p_q, p_k, p_v, p_s2, z, *pars, saved=None):
    conv = []
    for i, (h, x) in enumerate(((h_q, p_q), (h_k, p_k), (h_v, p_v))):
        ext = jnp.concatenate([h, x], axis=0)
        conv += _conv_fn(x, _shift_rows(ext, 1), _shift_rows(ext, 2), _shift_rows(ext, 3), *pars[4 * i:4 * i + 4])
    beta, g = _dn_gate_fn(p_s2, pars[12], pars[13])
    yield
    return (yield from _dn_chunk_steps(S, *[_groups(t) for t in conv], beta, g, _groups(z), _groups(pars[14]),
                                       saved=saved))


def _row_specs(rows, n_halo, row_of):
    C = CHUNK
    specs = [pl.BlockSpec((pl.Element(C), pl.Element(w)), lambda c, off=off: (row_of(c) * C, off))
             for (_, w, off) in rows]
    specs += [pl.BlockSpec((pl.Element(HALO), pl.Element(w)),
                           lambda c, off=off: (pl.multiple_of(jnp.maximum(row_of(c) * C - HALO, 0), HALO), off))
              for (_, w, off) in rows[:n_halo]]
    return specs


def _mixers_fwd(name, parts):
    Lp = parts[0][1][0][0].shape[0]
    C, D, NB = CHUNK, GROUP, N_GROUPS
    nc, width = Lp // C, NB * D
    n_in = [len(rows) + n_halo + len(pars) for (_, rows, n_halo, pars, _) in parts]
    n_out = [1 + len(shapes) for (*_, shapes) in parts]

    def body(*refs):
        ins, y_ref = refs[:sum(n_in)], refs[sum(n_in)]
        outs, states = refs[sum(n_in) + 1:sum(n_in) + 1 + sum(n_out)], refs[sum(n_in) + 1 + sum(n_out):]
        first = pl.program_id(0) == 0

        @pl.when(first)
        def _():
            for s_ref in states:
                s_ref[...] = jnp.zeros_like(s_ref)

        i_at, running = 0, []
        for k, (steps, rows, n_halo, pars, shapes) in enumerate(parts):
            n_r = len(rows)
            row_refs, halo_refs = ins[i_at:i_at + n_r], ins[i_at + n_r:i_at + n_r + n_halo]
            par_refs = ins[i_at + n_r + n_halo:i_at + n_in[k]]
            S = states[k][...]
            halos = [jnp.where(first, 0.0, r[...]) for r in halo_refs]
            running.append(steps(S, *halos, *[r[...] for r in row_refs], *[r[...] for r in par_refs]))
            i_at += n_in[k]
        results = _run_interleaved(running)
        o_at = 0
        for k, (y, S_new, saved) in enumerate(results):
            ck_ref, saved_refs = outs[o_at], outs[o_at + 1:o_at + n_out[k]]
            ck_ref[0] = states[k][...]
            for g in range(NB):
                y_ref[:, k * width + g * D:k * width + (g + 1) * D] = y[g].astype(BF16)
            for o_ref, val in zip(saved_refs, saved):
                o_ref[0] = val
            states[k][...] = S_new
            o_at += n_out[k]

    in_specs, args, out_specs, out_shape = [], [], [], []
    for (_, rows, n_halo, pars, shapes) in parts:
        in_specs += _row_specs(rows, n_halo, lambda c: c) + [pl.BlockSpec(p.shape, lambda c: (0, 0)) for p in pars]
        args += [r[0] for r in rows] + [r[0] for r in rows[:n_halo]] + list(pars)
        for shp in [(NB, D, D)] + list(shapes):
            out_specs.append(pl.BlockSpec((1,) + tuple(shp), lambda c: (c, 0, 0, 0)))
            out_shape.append(jax.ShapeDtypeStruct((nc,) + tuple(shp), F32))
    res = pl.pallas_call(
        body, name=name, grid=(nc,), in_specs=in_specs,
        out_specs=[pl.BlockSpec((C, width * len(parts)), lambda c: (c, 0))] + out_specs,
        out_shape=[jax.ShapeDtypeStruct((Lp, width * len(parts)), BF16)] + out_shape,
        scratch_shapes=[pltpu.VMEM((NB, D, D), F32)] * len(parts),
        compiler_params=_params(("arbitrary",)),
    )(*args)
    per_part, at = [], 1
    for n in n_out:
        per_part.append(list(res[at:at + n]))
        at += n
    return res[0], per_part


def _mixers_bwd(name, parts, kept, dy):
    Lp = parts[0][1][0][0].shape[0]
    C, D, NB = CHUNK, GROUP, N_GROUPS
    nc = Lp // C
    n_in = [len(rows) + n_halo + len(pars) + len(kept[k]) for k, (_, rows, n_halo, pars, _) in enumerate(parts)]
    widths = [sum(w for (_, w, _) in rows) for (_, rows, *_) in parts]
    n_par = [len(pars) for (_, _, _, pars, _) in parts]
    n_scr = [1 + n_halo for (_, _, n_halo, _, _) in parts]
    rev = lambda c: nc - 1 - c

    def body(*refs):
        ins, dy_ref = refs[:sum(n_in)], refs[sum(n_in)]
        drows_ref = refs[sum(n_in) + 1]
        dpar_all = refs[sum(n_in) + 2:sum(n_in) + 2 + sum(n_par)]
        scratch = refs[sum(n_in) + 2 + sum(n_par):]
        i = pl.program_id(0)

        @pl.when(i == 0)
        def _():
            for o_ref in list(dpar_all) + list(scratch):
                o_ref[...] = jnp.zeros_like(o_ref)

        i_at, vals, stored, cts, sizes = 0, [], [], [], []
        for k, (_, rows, n_halo, pars, _) in enumerate(parts):
            n_r = len(rows)
            row_refs, halo_refs = ins[i_at:i_at + n_r], ins[i_at + n_r:i_at + n_r + n_halo]
            par_refs = ins[i_at + n_r + n_halo:i_at + n_r + n_halo + n_par[k]]
            kept_refs = ins[i_at + n_r + n_halo + n_par[k]:i_at + n_in[k]]
            halos = [jnp.where(i == nc - 1, 0.0, r[...]) for r in halo_refs]
            part_vals = [kept_refs[0][0]] + halos + [r[...] for r in row_refs] + [r[...] for r in par_refs]
            vals += part_vals
            sizes.append(len(part_vals))
            stored.append(tuple(r[0] for r in kept_refs[1:]))
            cts.append((_groups(dy_ref[:, k * NB * D:(k + 1) * NB * D]), scratch[sum(n_scr[:k])][...]))
            i_at += n_in[k]

        def both(*flat):
            running, at = [], 0
            for k, (steps, *_) in enumerate(parts):
                running.append(steps(*flat[at:at + sizes[k]], saved=stored[k]))
                at += sizes[k]
            return tuple(r[:2] for r in _run_interleaved(running))

        all_grads = jax.vjp(both, *vals)[1](tuple(cts))
        i_at = p_at = s_at = g_at = col = 0
        for k, (_, rows, n_halo, pars, _) in enumerate(parts):
            n_r = len(rows)
            grads = all_grads[g_at:g_at + sizes[k]]
            dpar_refs = dpar_all[p_at:p_at + n_par[k]]
            ds_ref, carry_refs = scratch[s_at], scratch[s_at + 1:s_at + n_scr[k]]
            ds_ref[...] = grads[0]
            d_halos, d_rows = grads[1:1 + n_halo], grads[1 + n_halo:1 + n_halo + n_r]
            for j, g in enumerate(d_rows):
                if j < n_halo:
                    g = g + jnp.concatenate([jnp.zeros((C - HALO, g.shape[1]), F32), carry_refs[j][...]], axis=0)
                    carry_refs[j][...] = d_halos[j]
                drows_ref[:, col:col + g.shape[1]] = g.astype(BF16)
                col += g.shape[1]
            for o_ref, g in zip(dpar_refs, grads[1 + n_halo + n_r:]):
                o_ref[...] += g
            p_at, s_at, g_at = p_at + n_par[k], s_at + n_scr[k], g_at + sizes[k]

    in_specs, args, out_specs, out_shape, scratch_shapes = [], [], [], [], []
    for k, (_, rows, n_halo, pars, _) in enumerate(parts):
        in_specs += _row_specs(rows, n_halo, rev) + [pl.BlockSpec(p.shape, lambda c: (0, 0)) for p in pars]
        in_specs += [pl.BlockSpec((1,) + tuple(t.shape[1:]), lambda c: (rev(c), 0, 0, 0)) for t in kept[k]]
        args += [r[0] for r in rows] + [r[0] for r in rows[:n_halo]] + list(pars) + list(kept[k])
        out_specs += [pl.BlockSpec(p.shape, lambda c: (0, 0)) for p in pars]
        out_shape += [jax.ShapeDtypeStruct(p.shape, F32) for p in pars]
        scratch_shapes += [pltpu.VMEM((NB, D, D), F32)] + [pltpu.VMEM((HALO, w), F32) for (_, w, _) in rows[:n_halo]]
    res = pl.pallas_call(
        body, name=name, grid=(nc,),
        in_specs=in_specs + [pl.BlockSpec((C, dy.shape[1]), lambda c: (rev(c), 0))],
        out_specs=[pl.BlockSpec((C, sum(widths)), lambda c: (rev(c), 0))] + out_specs,
        out_shape=[jax.ShapeDtypeStruct((Lp, sum(widths)), BF16)] + out_shape,
        scratch_shapes=scratch_shapes, compiler_params=_params(("arbitrary",)),
    )(*args, dy)
    d_pars, at = [], 1
    for n in n_par:
        d_pars.append(list(res[at:at + n]))
        at += n
    return res[0], d_pars


def _loss_head(h, y, wo, target, fw, tm=2 * ROW_TILE):
    Lp, Dm = h.shape
    Dk = y.shape[1]
    S = target.shape[0]
    n_real = N_META + S
    tm = _pick(Lp, tm, 16)
    n_tiles = Lp // tm
    assert n_tiles >= 2 and N_META < tm <= S and Lp - n_real < tm and (S - tm) % 8 == 0 and N_META % 8 == 0

    def out_fn(z, fw_):
        return z * lax.rsqrt(jnp.mean(z * z, axis=-1, keepdims=True) + NORM_EPS) * fw_

    def body(h_ref, y_ref, wo_ref, t_ref, fw_ref, loss_ref, dz_ref, dz16_ref, dfw_ref):
        i = pl.program_id(0)

        @pl.when(i == 0)
        def _():
            loss_ref[...] = jnp.zeros_like(loss_ref)
            dfw_ref[...] = jnp.zeros_like(dfw_ref)

        row = i * tm + lax.broadcasted_iota(jnp.int32, (tm, 1), 0)
        mask = ((row >= N_META) & (row < n_real)).astype(F32)
        z = h_ref[...] + jnp.dot(y_ref[...], wo_ref[...], preferred_element_type=F32)
        o, vjp = jax.vjp(out_fn, z, fw_ref[...])
        t = t_ref[...]
        t = jnp.where(i == 0, pltpu.roll(t, N_META, 0), t)
        if Lp > n_real:
            t = jnp.where(i == n_tiles - 1, pltpu.roll(t, tm - (Lp - n_real), 0), t)
        err = (o - t) * mask
        row_loss = 0.5 * jnp.mean(jnp.square(err), axis=-1, keepdims=True)
        dz, dfw = vjp(err * (1.0 / Dm))
        loss_ref[...] += jnp.sum(row_loss, axis=0, keepdims=True)
        dz_ref[...] = dz
        dz16_ref[...] = dz.astype(BF16)
        dfw_ref[...] += dfw

    row_spec = pl.BlockSpec((tm, Dm), lambda i: (i, 0))
    target_spec = pl.BlockSpec((pl.Element(tm), pl.Element(Dm)),
                               lambda i: (pl.multiple_of(jnp.clip(i * tm - N_META, 0, S - tm), 8), 0))
    return pl.pallas_call(
        body, name="loss_head", grid=(n_tiles,),
        in_specs=[row_spec, pl.BlockSpec((tm, Dk), lambda i: (i, 0)), pl.BlockSpec((Dk, Dm), lambda i: (0, 0)),
                  target_spec, pl.BlockSpec((1, Dm), lambda i: (0, 0))],
        out_specs=[pl.BlockSpec((8, 128), lambda i: (0, 0)), row_spec, row_spec,
                   pl.BlockSpec((1, Dm), lambda i: (0, 0))],
        out_shape=[jax.ShapeDtypeStruct((8, 128), F32), jax.ShapeDtypeStruct((Lp, Dm), F32),
                   jax.ShapeDtypeStruct((Lp, Dm), BF16), jax.ShapeDtypeStruct((1, Dm), F32)],
        compiler_params=_params(("arbitrary",)),
    )(h, y, wo, target, fw)


def _exchange(name, x, masks, slot_kind, per_dest, n_split=1, copy_own=True, other_half=False):
    n = len(masks)
    keep_own = slot_kind is not None and copy_own
    n_slots = {"chip": 4, "core": 2, "dev": 8, None: n}[slot_kind]
    blk_shape = x.shape[1:] if per_dest else x.shape
    if other_half:
        blk_shape = (x.shape[0] // 2,) + tuple(x.shape[1:])
    rows = blk_shape[0] // n_split

    def body(x_ref, o_ref, send_sems, recv_sems, local_sems):
        mx, my, mc = lax.axis_index("x"), lax.axis_index("y"), lax.axis_index("c")
        if other_half:
            x_ref = x_ref.at[pl.ds((1 - mc) * blk_shape[0], blk_shape[0])]

        def slot(k, px, py, pc):
            return {"chip": 2 * px + py, "core": pc, "dev": 4 * px + 2 * py + pc, None: k}[slot_kind]

        def peer(m):
            return (mx + m[0]) % 2, (my + m[1]) % 2, (mc + m[2]) % 2

        def part(ref, j):
            return ref.at[pl.ds(j * rows, rows)]

        own_src = x_ref.at[2 * mx + my] if per_dest else x_ref
        local = []
        if keep_own:
            own_dst = o_ref.at[slot(0, mx, my, mc)]
            local = [pltpu.make_async_copy(part(own_src, j), part(own_dst, j), local_sems.at[j])
                     for j in range(n_split)]
        for cp in local:
            cp.start()
        sends = []
        for k, m in enumerate(masks):
            px, py, pc = peer(m)
            src = x_ref.at[2 * px + py] if per_dest else x_ref
            dst = o_ref.at[slot(k, mx, my, mc)]
            for j in range(n_split):
                sends.append(pltpu.make_async_remote_copy(
                    src_ref=part(src, j), dst_ref=part(dst, j), send_sem=send_sems.at[k * n_split + j],
                    recv_sem=recv_sems.at[k * n_split + j], device_id=(px, py, pc), device_id_type=MESH))
        for cp in sends:
            cp.start()
        for k, m in enumerate(masks):
            px, py, pc = peer(m)
            landed = o_ref.at[slot(k, px, py, pc)]
            for j in range(n_split):
                pltpu.make_async_remote_copy(
                    src_ref=part(own_src, j), dst_ref=part(landed, j), send_sem=send_sems.at[k * n_split + j],
                    recv_sem=recv_sems.at[k * n_split + j], device_id=(px, py, pc), device_id_type=MESH).wait_recv()
        for cp in sends:
            cp.wait_send()
        for cp in local:
            cp.wait()

    return pl.pallas_call(
        body, name=name,
        in_specs=[pl.BlockSpec(memory_space=pl.ANY)], out_specs=pl.BlockSpec(memory_space=pl.ANY),
        out_shape=jax.ShapeDtypeStruct((n_slots,) + tuple(blk_shape), x.dtype),
        scratch_shapes=[pltpu.SemaphoreType.DMA((n * n_split,)), pltpu.SemaphoreType.DMA((n * n_split,)),
                        pltpu.SemaphoreType.DMA((n_split,))],
        compiler_params=pltpu.CompilerParams(has_side_effects=True),
    )(x)


CHIP_MASKS = [(1, 0, 0), (0, 1, 0), (1, 1, 0)]
CORE_MASKS = [(0, 0, 1)]
ALL_MASKS = [(dx, dy, dc) for dx in (0, 1) for dy in (0, 1) for dc in (0, 1) if (dx, dy, dc) != (0, 0, 0)]

HBM_SPEC = pl.BlockSpec(memory_space=pltpu.HBM)
SEM_SPEC = pl.BlockSpec(memory_space=pltpu.SEMAPHORE)
DATAFLOW = pltpu.SideEffectType.DATAFLOW_SIDE_EFFECTING


def _split_copies(x_ref, land_ref, send_sems, recv_sems, masks, slot_kind, per_dest, n_split, with_recvs=True):
    mx, my, mc = lax.axis_index("x"), lax.axis_index("y"), lax.axis_index("c")
    if slot_kind == "other_half":
        x_ref = x_ref.at[pl.ds((1 - mc) * land_ref.shape[1], land_ref.shape[1])]
    slot = lambda px, py, pc: {"chip": 2 * px + py, "core": pc, "other_half": 0}[slot_kind]
    rows = (x_ref.shape[1] if per_dest else x_ref.shape[0]) // n_split
    part = lambda ref, j: ref.at[pl.ds(j * rows, rows)]
    sends, recvs = [], []
    for k, m in enumerate(masks):
        px, py, pc = (mx + m[0]) % 2, (my + m[1]) % 2, (mc + m[2]) % 2
        src = x_ref.at[2 * px + py] if per_dest else x_ref
        own_src = x_ref.at[2 * mx + my] if per_dest else x_ref
        for j in range(n_split):
            sems = dict(send_sem=send_sems.at[k * n_split + j], recv_sem=recv_sems.at[k * n_split + j],
                        device_id=(px, py, pc), device_id_type=MESH)
            sends.append(pltpu.make_async_remote_copy(
                src_ref=part(src, j), dst_ref=part(land_ref.at[slot(mx, my, mc)], j), **sems))
            if with_recvs:
                recvs.append(pltpu.make_async_remote_copy(
                    src_ref=part(own_src, j), dst_ref=part(land_ref.at[slot(px, py, pc)], j), **sems))
    return sends, recvs


def _exchange_start(name, x, masks, slot_kind, per_dest, n_split, dep=None):
    n = len(masks) * n_split
    blk_shape = x.shape[1:] if per_dest else x.shape
    if slot_kind == "other_half":
        blk_shape = (x.shape[0] // 2,) + tuple(x.shape[1:])
    land_shape = ({"chip": 4, "core": 2, "other_half": 1}[slot_kind],) + tuple(blk_shape)
    deps = [] if dep is None else [dep]

    def body(x_ref, land_ref, *rest):
        send_sems, recv_sems, x_thru, land_thru, token = rest[len(deps):]
        for cp in _split_copies(x_ref, land_ref, send_sems, recv_sems, masks, slot_kind, per_dest, n_split, False)[0]:
            cp.start()
        token[...] = jnp.zeros_like(token)

    return pl.pallas_call(
        body, name=name,
        out_shape=(pltpu.SemaphoreType.DMA((n,)), pltpu.SemaphoreType.DMA((n,)), pltpu.HBM(x.shape, x.dtype),
                   pltpu.HBM(land_shape, x.dtype), jax.ShapeDtypeStruct((8, 128), F32)),
        in_specs=(HBM_SPEC, HBM_SPEC) + (pl.BlockSpec(memory_space=pl.ANY),) * len(deps),
        out_specs=(SEM_SPEC, SEM_SPEC, HBM_SPEC, HBM_SPEC, pl.BlockSpec(memory_space=pltpu.VMEM)),
        input_output_aliases={0: 2, 1: 3},
        compiler_params=pltpu.CompilerParams(has_side_effects=DATAFLOW),
    )(pltpu.with_memory_space_constraint(x, pltpu.HBM),
      pltpu.with_memory_space_constraint(lax.empty(land_shape, x.dtype), pltpu.HBM), *deps)


def _exchange_wait(name, started, after, masks, slot_kind, per_dest, n_split):
    send_sems, recv_sems, x_thru, land_thru, _ = started
    after = list(after) if isinstance(after, (list, tuple)) else [after]

    def body(x_ref, land_ref, send_sems, recv_sems, *after_and_out_refs):
        sends, recvs = _split_copies(x_ref, land_ref, send_sems, recv_sems, masks, slot_kind, per_dest, n_split)
        for cp in sends:
            cp.wait_send()
        for cp in recvs:
            cp.wait_recv()

    return pl.pallas_call(
        body, name=name,
        out_shape=(pltpu.HBM(x_thru.shape, x_thru.dtype), pltpu.HBM(land_thru.shape, land_thru.dtype)),
        in_specs=(HBM_SPEC, HBM_SPEC, SEM_SPEC, SEM_SPEC) + (pl.BlockSpec(memory_space=pl.ANY),) * len(after),
        out_specs=(HBM_SPEC, HBM_SPEC), input_output_aliases={0: 0, 1: 1},
        compiler_params=pltpu.CompilerParams(has_side_effects=DATAFLOW),
    )(x_thru, land_thru, send_sems, recv_sems, *after)


def _gather_chips(name, x):
    return _exchange(name, x, CHIP_MASKS, "chip", False)


def _gather_shards(name, shard, n_split, overlap=False, dep=None):
    half = shard.shape[0] // 2
    mine = lax.dynamic_slice_in_dim(shard, lax.axis_index("c") * half, half, axis=0)
    if overlap:
        return _exchange_start(name + "_chips", mine, CHIP_MASKS, "chip", False, n_split, dep)
    by_chip = _exchange(name + "_chips", mine, CHIP_MASKS, "chip", False, n_split, copy_own=False)
    return _gather_tail(name, mine, by_chip)


def _gather_tail(name, mine, by_chip, arrange=None):
    c, chip = lax.axis_index("c"), 2 * lax.axis_index("x") + lax.axis_index("y")
    by_chip = lax.dynamic_update_index_in_dim(by_chip, mine, chip, 0)
    if arrange is not None:
        by_chip = arrange(by_chip)
    both = _exchange(name + "_cores", by_chip, CORE_MASKS, "core", False, N_CHIPS, copy_own=False)
    return lax.dynamic_update_index_in_dim(both, by_chip, c, 0)


def _gather_finish(name, started, after, n_split, arrange=None):
    mine, by_chip = _exchange_wait(name + "_chips_wait", started, after, CHIP_MASKS, "chip", False, n_split)
    return _gather_tail(name, mine, by_chip, arrange)


def _sum_slots(name, x, tr=128):
    S, R, N = x.shape
    tr = _pick(R, tr, 16)

    def body(x_ref, o_ref):
        acc = x_ref[0].astype(F32)
        for s in range(1, S):
            acc = acc + x_ref[s].astype(F32)
        o_ref[...] = acc

    return pl.pallas_call(
        body, name=name, grid=(R // tr,),
        in_specs=[pl.BlockSpec((S, tr, N), lambda i: (0, i, 0))], out_specs=pl.BlockSpec((tr, N), lambda i: (i, 0)),
        out_shape=jax.ShapeDtypeStruct((R, N), F32), compiler_params=_params(("parallel",)),
    )(x)


def _add_to_bf16(name, a, b, tr=128):
    S, R, N = a.shape
    tr = _pick(R, tr, 16)

    def body(a_ref, b_ref, o_ref):
        o_ref[...] = (a_ref[...] + b_ref[...]).astype(BF16)

    spec = pl.BlockSpec((S, tr, N), lambda i: (0, i, 0))
    return pl.pallas_call(
        body, name=name, grid=(R // tr,), in_specs=[spec, spec], out_specs=spec,
        out_shape=jax.ShapeDtypeStruct((S, R, N), BF16), compiler_params=_params(("parallel",)),
    )(a, b)


def _add_slabs_to_bf16(name, full, recv, c, tr=64):
    R, NP = full.shape
    half = R // 2
    nb = half // tr

    def body(c_ref, a_ref, b_ref, o_ref):
        x = a_ref[...] + b_ref[...]
        for s in range(N_CHIPS - 1):
            o_ref[s] = x[:, s * SHARD_COLS:(s + 1) * SHARD_COLS].astype(BF16)
        last = jnp.concatenate([x[:, (N_CHIPS - 1) * SHARD_COLS:GAP_AT], x[:, GAP_AT + GAP:]], axis=1)
        o_ref[N_CHIPS - 1] = last.astype(BF16)

    grid_spec = pltpu.PrefetchScalarGridSpec(
        num_scalar_prefetch=1, grid=(nb,),
        in_specs=[pl.BlockSpec((tr, NP), lambda i, c_ref: (c_ref[0] * nb + i, 0)),
                  pl.BlockSpec((tr, NP), lambda i, c_ref: (i, 0))],
        out_specs=pl.BlockSpec((N_CHIPS, tr, SHARD_COLS), lambda i, c_ref: (0, i, 0)))
    return pl.pallas_call(
        body, name=name, grid_spec=grid_spec,
        out_shape=jax.ShapeDtypeStruct((N_CHIPS, half, SHARD_COLS), BF16), compiler_params=_params(("parallel",)),
    )(jnp.reshape(c, (1,)).astype(jnp.int32), full, recv)


def _adamw(name, gparts, w, m, v, tr=128):
    S, R, N = gparts.shape
    tr = _pick(R, tr)
    c1 = 1.0 / (1.0 - ADAM_B1 ** ADAM_STEP)
    c2 = 1.0 / (1.0 - ADAM_B2 ** ADAM_STEP)

    def body(g_ref, w_ref, m_ref, v_ref, go_ref, d_ref, mo_ref, vo_ref):
        g = g_ref[0]
        for s in range(1, S):
            g = g + g_ref[s]
        m_new = ADAM_B1 * m_ref[...] + (1.0 - ADAM_B1) * g
        v_new = ADAM_B2 * v_ref[...] + (1.0 - ADAM_B2) * jnp.square(g)
        go_ref[...] = g
        mo_ref[...] = m_new
        vo_ref[...] = v_new
        d_ref[...] = -ADAM_LR * ((m_new * c1) / (jnp.sqrt(v_new * c2) + ADAM_EPS) + ADAM_WD * w_ref[...])

    spec = pl.BlockSpec((tr, N), lambda i: (i, 0))
    return pl.pallas_call(
        body, name=name, grid=(R // tr,),
        in_specs=[pl.BlockSpec((S, tr, N), lambda i: (0, i, 0)), spec, spec, spec], out_specs=[spec] * 4,
        out_shape=[jax.ShapeDtypeStruct((R, N), F32)] * 4, compiler_params=_params(("parallel",)),
    )(gparts, w, m, v)


def _reduce_to_shard(name, slabs, n_split, by_columns=False, overlap=False):
    c, chip = lax.axis_index("c"), 2 * lax.axis_index("x") + lax.axis_index("y")
    if by_columns:
        R, N = slabs.shape[0], SHARD_COLS
        from_sibling = _exchange(name + "_sib", slabs, CORE_MASKS, None, False, n_split, other_half=True)[0]
        wire = _add_slabs_to_bf16(name + "_add", slabs, from_sibling, c)
    else:
        _, R, N = slabs.shape
        half = R // 2
        halves = slabs.reshape(N_CHIPS, 2, half, N)
        mine = lax.dynamic_index_in_dim(halves, c, axis=1, keepdims=False)
        theirs = lax.dynamic_index_in_dim(halves, 1 - c, axis=1, keepdims=False)
        from_sibling = _exchange(name + "_sib", theirs, CORE_MASKS, None, False, N_CHIPS)[0]
        wire = _add_to_bf16(name + "_add", mine, from_sibling)
    if overlap:
        return _exchange_start(name + "_chips", wire, CHIP_MASKS, "chip", True, n_split), (R, N)
    got = _exchange(name + "_chips", wire, CHIP_MASKS, "chip", True, n_split, copy_own=False)
    return _reduce_tail(name, wire, got, n_split, R, N)


def _reduce_tail(name, wire, got, n_split, R, N):
    c, chip = lax.axis_index("c"), 2 * lax.axis_index("x") + lax.axis_index("y")
    got = lax.dynamic_update_index_in_dim(got, lax.dynamic_index_in_dim(wire, chip, 0, keepdims=False), chip, 0)
    part = _sum_slots(name + "_sum", got)
    both = _exchange(name + "_cores", part, CORE_MASKS, "core", False, n_split, copy_own=False)
    return lax.dynamic_update_index_in_dim(both, part, c, 0).reshape(1, R, N)


def _reduce_finish(name, started, after, n_split):
    handle, (R, N) = started
    wire, got = _exchange_wait(name + "_chips_wait", handle, after, CHIP_MASKS, "chip", True, n_split)
    return _reduce_tail(name, wire, got, n_split, R, N)


def _pack(pieces, cols, row_mult=8):
    flat = jnp.concatenate([p.reshape(-1) for p in pieces])
    rows = -(-flat.shape[0] // cols)
    rows = -(-rows // row_mult) * row_mult
    return jnp.pad(flat, (0, rows * cols - flat.shape[0])).reshape(rows, cols)


def _unpack(packed, shapes):
    flat = packed.reshape(-1)
    out, off = [], 0
    for shp in shapes:
        n = 1
        for d in shp:
            n *= d
        out.append(flat[off:off + n].reshape(shp))
        off += n
    return out


def kernel(x, meta_tokens, norm_w, w_in, rw_shift_mu, rw_w0, rw_w2, rw_a0, rw_a2, rw_k_k, rw_k_a, rw_r_k, rw_gn_w, rw_gn_b, dn_conv_w, dn_A_log, dn_dt_bias, dn_norm_w, w_out, final_norm_w, loss_target, m_meta_tokens, m_norm_w, m_w_in, m_rw_shift_mu, m_rw_w0, m_rw_w2, m_rw_a0, m_rw_a2, m_rw_k_k, m_rw_k_a, m_rw_r_k, m_rw_gn_w, m_rw_gn_b, m_dn_conv_w, m_dn_A_log, m_dn_dt_bias, m_dn_norm_w, m_w_out, m_final_norm_w, v_meta_tokens, v_norm_w, v_w_in, v_rw_shift_mu, v_rw_w0, v_rw_w2, v_rw_a0, v_rw_a2, v_rw_k_k, v_rw_k_a, v_rw_r_k, v_rw_gn_w, v_rw_gn_b, v_dn_conv_w, v_dn_A_log, v_dn_dt_bias, v_dn_norm_w, v_w_out, v_final_norm_w):
    S = x.shape[1]
    L = N_META + S
    Lp = -(-L // CHUNK) * CHUNK

    small_shapes = [(RW_LORA, 256), (RW_LORA, 256), (CONV_W, 768), (N_META, 512)]
    small_mine = _pack([rw_w2[0], rw_a2[0], dn_conv_w[0], meta_tokens], 1024)
    small_all = _gather_chips("gather_small", small_mine)
    per_chip = [_unpack(small_all[s], small_shapes) for s in range(N_CHIPS)]
    w2, a2, conv_w, meta = [jnp.concatenate([per_chip[s][i] for s in range(N_CHIPS)], axis=1) for i in range(4)]
    w_in_started = _gather_shards("gather_w_in", w_in[0].astype(BF16), 8, overlap=True, dep=small_all)
    w_out_started = _gather_shards("gather_w_out", w_out[0].astype(BF16), 8, overlap=True, dep=w_in_started[4])

    tail = [jnp.zeros((Lp - L, D_MODEL), F32)] if Lp > L else []
    h = jnp.concatenate([meta + w_in_started[4][:1, :1], x[0]] + tail, axis=0)
    (u,) = _rowwise("rms_in", _rms_fn, [_row(h)], [norm_w], [D_MODEL], tm=2 * ROW_TILE, out_dtype=BF16)
    u_t = u.T
    m_in, v_in = m_w_in[0] + w_in_started[4][:1, :1], v_w_in[0] + w_in_started[4][:1, :1]
    cut = GAP_AT - (N_CHIPS - 1) * SHARD_COLS

    def gapped(by_chip):
        last = by_chip[N_CHIPS - 1]
        return jnp.concatenate([by_chip[s] for s in range(N_CHIPS - 1)]
                               + [last[:, :cut], jnp.zeros((last.shape[0], GAP), BF16), last[:, cut:]], axis=1)

    W_gapped = _gather_finish("gather_w_in", w_in_started, [u, u_t, m_in, v_in], 8, arrange=gapped)
    W_gapped = W_gapped.reshape(D_MODEL, NP_COLS)
    p = _mm("in_proj", u, W_gapped, "nn", dep=w_out_started[4])

    mu = rw_shift_mu
    zpad = jnp.zeros((RW_LORA, RW_WIDTH), F32)
    rw_params = [mu[:, 0:1024], mu[:, 1024:2048], mu[:, 2048:3072], mu[:, 3072:3200], rw_w0,
                 jnp.concatenate([w2, zpad], axis=0), rw_a0, jnp.concatenate([zpad, a2], axis=0), rw_k_k, rw_k_a]
    rw_rows = [_row(p, 1024, OFF_R), _row(p, 1024, OFF_K), _row(p, 1024, OFF_V), _row(p, 128, OFF_S1),
               _row(p, 1024, OFF_GATE)]
    rw_pars = rw_params + [rw_r_k, rw_gn_w, rw_gn_b]
    rw_saved_shapes = [(N_GROUPS, CHUNK, GROUP), (N_GROUPS, CHUNK, GROUP), (N_GROUPS, 2 * CHUNK, 2 * GROUP),
                       (N_GROUPS, 2 * CHUNK, GROUP)]

    dn_rows = [_row(p, 1024, OFF_DQ), _row(p, 1024, OFF_DK), _row(p, 1024, OFF_DV), _row(p, 128, OFF_S2),
               _row(p, 1024, OFF_Z)]
    dn_pars = [conv_w[j:j + 1, 1024 * i:1024 * (i + 1)] for i in range(3) for j in range(CONV_W)]
    narrow = lambda t: jnp.pad(t, ((0, 0), (DN_HEADS, DN_HEAD - 2 * DN_HEADS)))
    dn_pars += [narrow(dn_A_log), narrow(dn_dt_bias), jnp.tile(dn_norm_w, (1, DN_HEADS))]
    dn_saved_shapes = [(N_GROUPS, CHUNK, CHUNK), (N_GROUPS, CHUNK, GROUP)]
    mixer_parts = [(_rw_fused_steps, rw_rows, 4, rw_pars, rw_saved_shapes),
                   (_dn_fused_steps, dn_rows, 3, dn_pars, dn_saved_shapes)]
    y, mixer_kept = _mixers_fwd("mixers_fwd", mixer_parts)
    w_out_all = _gather_finish("gather_w_out", w_out_started, y, 8)
    Wo = jnp.concatenate([w_out_all[:, s].reshape(D_MODEL // N_CHIPS, D_MODEL) for s in range(N_CHIPS)], axis=0)
    loss_acc, dz, dz16, d_fw = _loss_head(h, y, Wo, loss_target[0], final_norm_w.reshape(1, D_MODEL))
    loss = lax.psum(loss_acc[0, 0], ("x", "y", "c"))

    d_wo = _mm("d_w_out", y.T, dz16, "nn", tm=512, tn=1024, tk=Lp)
    wo_started = _reduce_to_shard("rs_w_out", d_wo.reshape(N_CHIPS, D_MODEL // N_CHIPS, D_MODEL), 8, overlap=True)
    dy = _mm("d_out_proj", dz16, Wo, "nt", tn=1024, tk=2048, dep=wo_started[0][4])

    dp, (d_rw_pars, d_dn_pars) = _mixers_bwd("mixers_bwd", mixer_parts, mixer_kept, dy)
    d_prep_pars, d_rw_pars = d_rw_pars[:10], d_rw_pars[10:]
    d_conv_parts = [jnp.concatenate(d_dn_pars[4 * i:4 * i + 4], axis=0) for i in range(3)]
    d_a_log_b, d_dt_b = d_dn_pars[12], d_dn_pars[13]
    d_dn_nw = jnp.sum(d_dn_pars[14].reshape(DN_HEADS, DN_HEAD), axis=0, keepdims=True)
    d_W = _mm("d_w_in", u_t, dp, "nn", tm=1024, tn=768, tk=Lp)
    sib = _exchange_start("rs_w_in_sib", d_W, CORE_MASKS, "other_half", False, 8)
    wo_parts = _reduce_finish("rs_w_out", wo_started, sib[4], 8)
    wo_out = _adamw("adam_w_out", wo_parts, w_out[0], m_w_out[0], v_w_out[0])
    d_W, from_sibling = _exchange_wait("rs_w_in_sib_wait", sib, wo_out[0], CORE_MASKS, "other_half", False, 8)
    wire = _add_slabs_to_bf16("rs_w_in_add", d_W, from_sibling[0], lax.axis_index("c"))
    wi_started = (_exchange_start("rs_w_in_chips", wire, CHIP_MASKS, "chip", True, 8), (D_MODEL, SHARD_COLS))
    du = _mm("d_in_proj", dp, W_gapped, "nt", tn=2048, tk=1408, dep=wi_started[0][4])
    norm_res = lambda h_, w_: (_rms_fn(h_, w_)[0], h_)
    (dx,), (d_norm_w_x,) = _rowwise_bwd("rms_in_bwd", norm_res, [_row(h)], [norm_w], [_row(du), _row(dz)],
                                        tm=2 * GROUP, row0=N_META, n_rows=S)
    (d_meta,), (d_norm_w_meta,) = _rowwise_bwd("rms_meta_bwd", norm_res, [_row(h)], [norm_w], [_row(du), _row(dz)],
                                               n_rows=N_META)
    d_norm_w = d_norm_w_x + d_norm_w_meta
    grad_x = dx[None]

    d_mu = jnp.concatenate(d_prep_pars[0:4], axis=1)
    d_w2, d_a2 = d_prep_pars[5][:RW_LORA], d_prep_pars[7][RW_LORA:]
    d_conv = jnp.concatenate(d_conv_parts, axis=1)
    head_sum = lambda t: t[:, DN_HEADS:2 * DN_HEADS]
    rep_names = ["norm_w", "rw_shift_mu", "rw_w0", "rw_a0", "rw_k_k", "rw_k_a", "rw_r_k", "rw_gn_w", "rw_gn_b",
                 "dn_A_log", "dn_dt_bias", "dn_norm_w", "final_norm_w"]
    rep_g = [d_norm_w, d_mu, d_prep_pars[4], d_prep_pars[6], d_prep_pars[8], d_prep_pars[9],
             d_rw_pars[0], d_rw_pars[1], d_rw_pars[2],
             head_sum(d_a_log_b), head_sum(d_dt_b), d_dn_nw, d_fw.reshape(D_MODEL)]
    rep_w = [norm_w, rw_shift_mu, rw_w0, rw_a0, rw_k_k, rw_k_a, rw_r_k, rw_gn_w, rw_gn_b, dn_A_log, dn_dt_bias,
             dn_norm_w, final_norm_w]
    rep_m = [m_norm_w, m_rw_shift_mu, m_rw_w0, m_rw_a0, m_rw_k_k, m_rw_k_a, m_rw_r_k, m_rw_gn_w, m_rw_gn_b,
             m_dn_A_log, m_dn_dt_bias, m_dn_norm_w, m_final_norm_w]
    rep_v = [v_norm_w, v_rw_shift_mu, v_rw_w0, v_rw_a0, v_rw_k_k, v_rw_k_a, v_rw_r_k, v_rw_gn_w, v_rw_gn_b,
             v_dn_A_log, v_dn_dt_bias, v_dn_norm_w, v_final_norm_w]
    rep_shapes = [t.shape for t in rep_w]
    rep_all = _exchange("gather_rep_grads", _pack(rep_g, 128), ALL_MASKS, "dev", False)
    rep_out = _adamw("adam_rep", rep_all, _pack(rep_w, 128), _pack(rep_m, 128), _pack(rep_v, 128))
    rep_out = [dict(zip(rep_names, _unpack(t, rep_shapes))) for t in rep_out]

    sm_slabs = jnp.stack([_pack([d_w2[:, 256 * s:256 * (s + 1)], d_a2[:, 256 * s:256 * (s + 1)],
                                 d_conv[:, 768 * s:768 * (s + 1)], d_meta[:, 512 * s:512 * (s + 1)]], 1024, 64)
                          for s in range(N_CHIPS)])
    sm_parts = _reduce_to_shard("rs_small", sm_slabs, 1)
    sm_w = [rw_w2[0], rw_a2[0], dn_conv_w[0], meta_tokens]
    sm_m = [m_rw_w2[0], m_rw_a2[0], m_dn_conv_w[0], m_meta_tokens]
    sm_v = [v_rw_w2[0], v_rw_a2[0], v_dn_conv_w[0], v_meta_tokens]
    sm_out = _adamw("adam_small", sm_parts, _pack(sm_w, 1024, 64), _pack(sm_m, 1024, 64), _pack(sm_v, 1024, 64))
    sm_names = ["rw_w2", "rw_a2", "dn_conv_w", "meta_tokens"]
    sm_full_shapes = [(1, RW_LORA, 256), (1, RW_LORA, 256), (1, CONV_W, 768), (N_META, 512)]
    sm_out = [dict(zip(sm_names, [t.reshape(shp) for t, shp in zip(_unpack(o, small_shapes), sm_full_shapes)]))
              for o in sm_out]

    wi_parts = _reduce_finish("rs_w_in", wi_started, dx, 8)
    wi_out = _adamw("adam_w_in", wi_parts, w_in[0], m_in, v_in)

    order = ["meta_tokens", "norm_w", "w_in", "rw_shift_mu", "rw_w0", "rw_w2", "rw_a0", "rw_a2", "rw_k_k", "rw_k_a",
             "rw_r_k", "rw_gn_w", "rw_gn_b", "dn_conv_w", "dn_A_log", "dn_dt_bias", "dn_norm_w", "w_out",
             "final_norm_w"]
    outs = [loss, grad_x]
    for kind in range(4):
        table = dict(rep_out[kind])
        table.update(sm_out[kind])
        table["w_in"] = wi_out[kind][None]
        table["w_out"] = wo_out[kind][None]
        outs += [table[n] for n in order]
    return tuple(outs)
```
